```python
import jax, jax.numpy as jnp
from jax import lax
import numpy as np

D_MODEL = 1024
BATCH = 8
SEQ = 4096
DEPTH = 1

D_MIX = 2 * D_MODEL
D_SSD = D_MIX // 2
SSD_HEAD_DIM = 64
SSD_HEADS = D_SSD // SSD_HEAD_DIM
SSD_GROUPS = 4
SSD_STATE = 128
SSD_CONV = 4
SSD_CHUNK = 128
D_POOL = D_MIX - D_SSD
POOL_WINDOWS = (2, 4, 8, 16)
POOL_GROUPS = len(POOL_WINDOWS)
POOL_GROUP_DIM = D_POOL // POOL_GROUPS
D_XBC = D_SSD + 2 * SSD_GROUPS * SSD_STATE
D_IN_PROJ = D_SSD + D_XBC + SSD_HEADS + D_POOL
D_FF = 2816
N_MOD = 9
FFN_RES = 0.5
EPS = 1e-6

kernel_name = "hybrid_ssd_pool_macaron_adaln"


def rms_norm(x, w):
    xf = x.astype(jnp.float32)
    y = xf * lax.rsqrt(jnp.mean(xf * xf, axis=-1, keepdims=True) + EPS)
    return (y * w.astype(jnp.float32)).astype(x.dtype)


def modulate(h, shift, scale):
    return h * (1 + scale[:, None, :]) + shift[:, None, :]


def swiglu(h, w_gate, w_up, w_down):
    return (jax.nn.silu(h @ w_gate) * (h @ w_up)) @ w_down


def causal_depthwise_conv(u, w, b):
    ch = u.shape[-1]
    y = lax.conv_general_dilated(
        u, w[:, None, :].astype(u.dtype), window_strides=(1,),
        padding=[(SSD_CONV - 1, 0)], dimension_numbers=('NWC', 'WIO', 'NWC'),
        feature_group_count=ch)
    return y + b.astype(u.dtype)


def ssd_chunked(xh, dt, a, bm, cm):
    bsz, L, H, P = xh.shape
    G, N = bm.shape[2], bm.shape[3]
    R = H // G
    Q = SSD_CHUNK
    NC = L // Q
    xdt = (xh * dt[..., None]).reshape(bsz, NC, Q, G, R, P)
    adt = jnp.transpose((dt * a).reshape(bsz, NC, Q, G, R), (0, 3, 4, 1, 2))
    a_cs = jnp.cumsum(adt, axis=-1)
    bc = bm.reshape(bsz, NC, Q, G, N)
    cc = cm.reshape(bsz, NC, Q, G, N)
    causal = jnp.tril(jnp.ones((Q, Q), dtype=bool))
    seg = a_cs[..., :, None] - a_cs[..., None, :]
    decay = jnp.exp(jnp.where(causal, seg, -jnp.inf))
    cb = jnp.einsum('bclgn,bcsgn->bgcls', cc, bc)
    scores = cb[:, :, None] * decay
    y_diag = jnp.einsum('bgrcls,bcsgrp->bclgrp', scores, xdt)
    ds = jnp.transpose(jnp.exp(a_cs[..., -1:] - a_cs), (0, 3, 4, 1, 2))
    states = jnp.einsum('bcsgn,bcsgrp->bcgrpn', bc, xdt * ds[..., None])
    chunk_decay = jnp.moveaxis(jnp.exp(a_cs[..., -1]), -1, 0)

    def step(h, inp):
        s, d = inp
        return h * d[..., None, None] + s, h

    h0 = jnp.zeros((bsz, G, R, P, N), states.dtype)
    _, prev = lax.scan(step, h0, (jnp.moveaxis(states, 1, 0), chunk_decay))
    prev = jnp.moveaxis(prev, 0, 1)
    sd = jnp.transpose(jnp.exp(a_cs), (0, 3, 4, 1, 2))
    y_off = jnp.einsum('bclgn,bcgrpn->bclgrp', cc, prev) * sd[..., None]
    return (y_diag + y_off).reshape(bsz, L, H, P)


def pool_mixer(u, pool_w, pool_b, pool_scale):
    bsz, L, _ = u.shape
    uf = u.astype(jnp.float32).reshape(bsz, L, POOL_GROUPS, POOL_GROUP_DIM)
    cs = jnp.cumsum(uf, axis=1)
    pos = jnp.arange(1, L + 1, dtype=jnp.float32)
    pooled = []
    for gi, w in enumerate(POOL_WINDOWS):
        shifted = jnp.pad(cs[:, :L - w, gi], ((0, 0), (w, 0), (0, 0)))
        cnt = jnp.minimum(pos, float(w))[None, :, None]
        pooled.append((cs[:, :, gi] - shifted) / cnt)
    diff = jnp.stack(pooled, axis=2) - uf
    out = jnp.einsum('blgc,gcd->blgd', diff, pool_w.astype(jnp.float32)) + pool_b.astype(jnp.float32)
    out = out.reshape(bsz, L, D_POOL) * pool_scale.astype(jnp.float32)
    return out.astype(u.dtype)


def token_mixer(h, w_in, conv_w, conv_b, dt_bias, a_log, d_skip, ssd_norm_w,
                pool_w, pool_b, pool_scale, w_out):
    bsz, L, _ = h.shape
    f32 = jnp.float32
    proj = h @ w_in
    z, xbc, dt_raw, u = jnp.split(
        proj, [D_SSD, D_SSD + D_XBC, D_SSD + D_XBC + SSD_HEADS], axis=-1)
    xbc = jax.nn.silu(causal_depthwise_conv(xbc, conv_w, conv_b))
    xs, bm, cm = jnp.split(xbc, [D_SSD, D_SSD + SSD_GROUPS * SSD_STATE], axis=-1)
    dt = jax.nn.softplus(dt_raw.astype(f32) + dt_bias.astype(f32))
    a = -jnp.exp(a_log.astype(f32))
    xh = xs.astype(f32).reshape(bsz, L, SSD_HEADS, SSD_HEAD_DIM)
    y = ssd_chunked(xh, dt, a,
                    bm.astype(f32).reshape(bsz, L, SSD_GROUPS, SSD_STATE),
                    cm.astype(f32).reshape(bsz, L, SSD_GROUPS, SSD_STATE))
    y = (y + d_skip.astype(f32)[:, None] * xh).reshape(bsz, L, D_SSD)
    yg = (y * jax.nn.silu(z.astype(f32))).reshape(bsz, L, SSD_GROUPS, D_SSD // SSD_GROUPS)
    yg = yg * lax.rsqrt(jnp.mean(yg * yg, axis=-1, keepdims=True) + EPS)
    y_ssd = (yg.reshape(bsz, L, D_SSD) * ssd_norm_w.astype(f32)).astype(h.dtype)
    y_pool = pool_mixer(u, pool_w, pool_b, pool_scale)
    return jnp.concatenate([y_ssd, y_pool], axis=-1) @ w_out


def _fwd_setup_inputs(seed: int = 0) -> dict:
    key = jax.random.key(seed)
    ks = jax.random.split(key, 32)
    f32 = jnp.float32

    def normal(k, shape, scale):
        return jax.random.normal(k, shape, f32) * scale

    dt0 = jnp.exp(jax.random.uniform(ks[9], (DEPTH, SSD_HEADS), f32,
                                     minval=np.log(1e-3), maxval=np.log(1e-1)))
    return {
        "x": normal(ks[0], (BATCH, SEQ, D_MODEL), 1.0),
        "c": normal(ks[1], (BATCH, D_MODEL), 1.0),
        "w_ada": normal(ks[2], (DEPTH, D_MODEL, N_MOD * D_MODEL), 0.5 * D_MODEL ** -0.5),
        "b_ada": normal(ks[3], (DEPTH, N_MOD * D_MODEL), 0.02),
        "ffn1_norm": 1.0 + normal(ks[4], (DEPTH, D_MODEL), 0.02),
        "ffn1_w_gate": normal(ks[5], (DEPTH, D_MODEL, D_FF), D_MODEL ** -0.5),
        "ffn1_w_up": normal(ks[6], (DEPTH, D_MODEL, D_FF), D_MODEL ** -0.5),
        "ffn1_w_down": normal(ks[7], (DEPTH, D_FF, D_MODEL), D_FF ** -0.5),
        "mix_norm": 1.0 + normal(ks[8], (DEPTH, D_MODEL), 0.02),
        "w_in": normal(ks[10], (DEPTH, D_MODEL, D_IN_PROJ), D_MODEL ** -0.5),
        "conv_w": normal(ks[11], (DEPTH, SSD_CONV, D_XBC), SSD_CONV ** -0.5),
        "conv_b": normal(ks[12], (DEPTH, D_XBC), 0.02),
        "dt_bias": dt0 + jnp.log(-jnp.expm1(-dt0)),
        "a_log": jnp.log(jax.random.uniform(ks[13], (DEPTH, SSD_HEADS), f32, minval=1.0, maxval=16.0)),
        "d_skip": 1.0 + normal(ks[14], (DEPTH, SSD_HEADS), 0.02),
        "ssd_norm_w": 1.0 + normal(ks[15], (DEPTH, D_SSD), 0.02),
        "pool_w": normal(ks[16], (DEPTH, POOL_GROUPS, POOL_GROUP_DIM, POOL_GROUP_DIM), POOL_GROUP_DIM ** -0.5),
        "pool_b": normal(ks[17], (DEPTH, POOL_GROUPS, POOL_GROUP_DIM), 0.02),
        "pool_scale": 1.0 + normal(ks[18], (DEPTH, D_POOL), 0.02),
        "w_out": normal(ks[19], (DEPTH, D_MIX, D_MODEL), D_MIX ** -0.5),
        "ffn2_norm": 1.0 + normal(ks[20], (DEPTH, D_MODEL), 0.02),
        "ffn2_w_gate": normal(ks[21], (DEPTH, D_MODEL, D_FF), D_MODEL ** -0.5),
        "ffn2_w_up": normal(ks[22], (DEPTH, D_MODEL, D_FF), D_MODEL ** -0.5),
        "ffn2_w_down": normal(ks[23], (DEPTH, D_FF, D_MODEL), D_FF ** -0.5),
        "final_norm": 1.0 + normal(ks[24], (D_MODEL,), 0.02),
    }


def _fwd_reference(x, c, w_ada, b_ada, ffn1_norm, ffn1_w_gate, ffn1_w_up, ffn1_w_down,
              mix_norm, w_in, conv_w, conv_b, dt_bias, a_log, d_skip, ssd_norm_w,
              pool_w, pool_b, pool_scale, w_out, ffn2_norm, ffn2_w_gate, ffn2_w_up,
              ffn2_w_down, final_norm):
    c_act = jax.nn.silu(c)
    for i in range(DEPTH):
        mod = c_act @ w_ada[i] + b_ada[i]
        sh1, sc1, g1, sh2, sc2, g2, sh3, sc3, g3 = jnp.split(mod, N_MOD, axis=-1)
        h = modulate(rms_norm(x, ffn1_norm[i]), sh1, sc1)
        x = x + FFN_RES * g1[:, None, :] * swiglu(h, ffn1_w_gate[i], ffn1_w_up[i], ffn1_w_down[i])
        h = modulate(rms_norm(x, mix_norm[i]), sh2, sc2)
        x = x + g2[:, None, :] * token_mixer(
            h, w_in[i], conv_w[i], conv_b[i], dt_bias[i], a_log[i], d_skip[i], ssd_norm_w[i],
            pool_w[i], pool_b[i], pool_scale[i], w_out[i])
        h = modulate(rms_norm(x, ffn2_norm[i]), sh3, sc3)
        x = x + FFN_RES * g3[:, None, :] * swiglu(h, ffn2_w_gate[i], ffn2_w_up[i], ffn2_w_down[i])
    return rms_norm(x, final_norm)


import jax as _jax
import jax.numpy as _jnp

TWIN_FORMAT = 'train_step'
FWD_PARAMS = ['x', 'c', 'w_ada', 'b_ada', 'ffn1_norm', 'ffn1_w_gate', 'ffn1_w_up', 'ffn1_w_down', 'mix_norm', 'w_in', 'conv_w', 'conv_b', 'dt_bias', 'a_log', 'd_skip', 'ssd_norm_w', 'pool_w', 'pool_b', 'pool_scale', 'w_out', 'ffn2_norm', 'ffn2_w_gate', 'ffn2_w_up', 'ffn2_w_down', 'final_norm']
TWIN_WEIGHTS = ['w_ada', 'b_ada', 'ffn1_norm', 'ffn1_w_gate', 'ffn1_w_up', 'ffn1_w_down', 'mix_norm', 'w_in', 'conv_w', 'conv_b', 'dt_bias', 'a_log', 'd_skip', 'ssd_norm_w', 'pool_w', 'pool_b', 'pool_scale', 'w_out', 'ffn2_norm', 'ffn2_w_gate', 'ffn2_w_up', 'ffn2_w_down', 'final_norm']
TWIN_DIFF_INPUT = 'x'
TWIN_INPUTS = ['x', 'c', 'w_ada', 'b_ada', 'ffn1_norm', 'ffn1_w_gate', 'ffn1_w_up', 'ffn1_w_down', 'mix_norm', 'w_in', 'conv_w', 'conv_b', 'dt_bias', 'a_log', 'd_skip', 'ssd_norm_w', 'pool_w', 'pool_b', 'pool_scale', 'w_out', 'ffn2_norm', 'ffn2_w_gate', 'ffn2_w_up', 'ffn2_w_down', 'final_norm', 'loss_target', 'm_w_ada', 'm_b_ada', 'm_ffn1_norm', 'm_ffn1_w_gate', 'm_ffn1_w_up', 'm_ffn1_w_down', 'm_mix_norm', 'm_w_in', 'm_conv_w', 'm_conv_b', 'm_dt_bias', 'm_a_log', 'm_d_skip', 'm_ssd_norm_w', 'm_pool_w', 'm_pool_b', 'm_pool_scale', 'm_w_out', 'm_ffn2_norm', 'm_ffn2_w_gate', 'm_ffn2_w_up', 'm_ffn2_w_down', 'm_final_norm', 'v_w_ada', 'v_b_ada', 'v_ffn1_norm', 'v_ffn1_w_gate', 'v_ffn1_w_up', 'v_ffn1_w_down', 'v_mix_norm', 'v_w_in', 'v_conv_w', 'v_conv_b', 'v_dt_bias', 'v_a_log', 'v_d_skip', 'v_ssd_norm_w', 'v_pool_w', 'v_pool_b', 'v_pool_scale', 'v_w_out', 'v_ffn2_norm', 'v_ffn2_w_gate', 'v_ffn2_w_up', 'v_ffn2_w_down', 'v_final_norm']
TWIN_OUTPUTS = ['loss', 'grad_x', 'grad_w_ada', 'grad_b_ada', 'grad_ffn1_norm', 'grad_ffn1_w_gate', 'grad_ffn1_w_up', 'grad_ffn1_w_down', 'grad_mix_norm', 'grad_w_in', 'grad_conv_w', 'grad_conv_b', 'grad_dt_bias', 'grad_a_log', 'grad_d_skip', 'grad_ssd_norm_w', 'grad_pool_w', 'grad_pool_b', 'grad_pool_scale', 'grad_w_out', 'grad_ffn2_norm', 'grad_ffn2_w_gate', 'grad_ffn2_w_up', 'grad_ffn2_w_down', 'grad_final_norm', 'delta_w_ada', 'delta_b_ada', 'delta_ffn1_norm', 'delta_ffn1_w_gate', 'delta_ffn1_w_up', 'delta_ffn1_w_down', 'delta_mix_norm', 'delta_w_in', 'delta_conv_w', 'delta_conv_b', 'delta_dt_bias', 'delta_a_log', 'delta_d_skip', 'delta_ssd_norm_w', 'delta_pool_w', 'delta_pool_b', 'delta_pool_scale', 'delta_w_out', 'delta_ffn2_norm', 'delta_ffn2_w_gate', 'delta_ffn2_w_up', 'delta_ffn2_w_down', 'delta_final_norm', 'new_m_w_ada', 'new_m_b_ada', 'new_m_ffn1_norm', 'new_m_ffn1_w_gate', 'new_m_ffn1_w_up', 'new_m_ffn1_w_down', 'new_m_mix_norm', 'new_m_w_in', 'new_m_conv_w', 'new_m_conv_b', 'new_m_dt_bias', 'new_m_a_log', 'new_m_d_skip', 'new_m_ssd_norm_w', 'new_m_pool_w', 'new_m_pool_b', 'new_m_pool_scale', 'new_m_w_out', 'new_m_ffn2_norm', 'new_m_ffn2_w_gate', 'new_m_ffn2_w_up', 'new_m_ffn2_w_down', 'new_m_final_norm', 'new_v_w_ada', 'new_v_b_ada', 'new_v_ffn1_norm', 'new_v_ffn1_w_gate', 'new_v_ffn1_w_up', 'new_v_ffn1_w_down', 'new_v_mix_norm', 'new_v_w_in', 'new_v_conv_w', 'new_v_conv_b', 'new_v_dt_bias', 'new_v_a_log', 'new_v_d_skip', 'new_v_ssd_norm_w', 'new_v_pool_w', 'new_v_pool_b', 'new_v_pool_scale', 'new_v_w_out', 'new_v_ffn2_norm', 'new_v_ffn2_w_gate', 'new_v_ffn2_w_up', 'new_v_ffn2_w_down', 'new_v_final_norm']
TWIN_LEAF_KINDS = {'loss': 'loss', 'grad_x': 'grad_x', 'grad_w_ada': 'grad_w', 'grad_b_ada': 'grad_w', 'grad_ffn1_norm': 'grad_w', 'grad_ffn1_w_gate': 'grad_w', 'grad_ffn1_w_up': 'grad_w', 'grad_ffn1_w_down': 'grad_w', 'grad_mix_norm': 'grad_w', 'grad_w_in': 'grad_w', 'grad_conv_w': 'grad_w', 'grad_conv_b': 'grad_w', 'grad_dt_bias': 'grad_w', 'grad_a_log': 'grad_w', 'grad_d_skip': 'grad_w', 'grad_ssd_norm_w': 'grad_w', 'grad_pool_w': 'grad_w', 'grad_pool_b': 'grad_w', 'grad_pool_scale': 'grad_w', 'grad_w_out': 'grad_w', 'grad_ffn2_norm': 'grad_w', 'grad_ffn2_w_gate': 'grad_w', 'grad_ffn2_w_up': 'grad_w', 'grad_ffn2_w_down': 'grad_w', 'grad_final_norm': 'grad_w', 'delta_w_ada': 'delta_w', 'delta_b_ada': 'delta_w', 'delta_ffn1_norm': 'delta_w', 'delta_ffn1_w_gate': 'delta_w', 'delta_ffn1_w_up': 'delta_w', 'delta_ffn1_w_down': 'delta_w', 'delta_mix_norm': 'delta_w', 'delta_w_in': 'delta_w', 'delta_conv_w': 'delta_w', 'delta_conv_b': 'delta_w', 'delta_dt_bias': 'delta_w', 'delta_a_log': 'delta_w', 'delta_d_skip': 'delta_w', 'delta_ssd_norm_w': 'delta_w', 'delta_pool_w': 'delta_w', 'delta_pool_b': 'delta_w', 'delta_pool_scale': 'delta_w', 'delta_w_out': 'delta_w', 'delta_ffn2_norm': 'delta_w', 'delta_ffn2_w_gate': 'delta_w', 'delta_ffn2_w_up': 'delta_w', 'delta_ffn2_w_down': 'delta_w', 'delta_final_norm': 'delta_w', 'new_m_w_ada': 'new_m', 'new_m_b_ada': 'new_m', 'new_m_ffn1_norm': 'new_m', 'new_m_ffn1_w_gate': 'new_m', 'new_m_ffn1_w_up': 'new_m', 'new_m_ffn1_w_down': 'new_m', 'new_m_mix_norm': 'new_m', 'new_m_w_in': 'new_m', 'new_m_conv_w': 'new_m', 'new_m_conv_b': 'new_m', 'new_m_dt_bias': 'new_m', 'new_m_a_log': 'new_m', 'new_m_d_skip': 'new_m', 'new_m_ssd_norm_w': 'new_m', 'new_m_pool_w': 'new_m', 'new_m_pool_b': 'new_m', 'new_m_pool_scale': 'new_m', 'new_m_w_out': 'new_m', 'new_m_ffn2_norm': 'new_m', 'new_m_ffn2_w_gate': 'new_m', 'new_m_ffn2_w_up': 'new_m', 'new_m_ffn2_w_down': 'new_m', 'new_m_final_norm': 'new_m', 'new_v_w_ada': 'new_v', 'new_v_b_ada': 'new_v', 'new_v_ffn1_norm': 'new_v', 'new_v_ffn1_w_gate': 'new_v', 'new_v_ffn1_w_up': 'new_v', 'new_v_ffn1_w_down': 'new_v', 'new_v_mix_norm': 'new_v', 'new_v_w_in': 'new_v', 'new_v_conv_w': 'new_v', 'new_v_conv_b': 'new_v', 'new_v_dt_bias': 'new_v', 'new_v_a_log': 'new_v', 'new_v_d_skip': 'new_v', 'new_v_ssd_norm_w': 'new_v', 'new_v_pool_w': 'new_v', 'new_v_pool_b': 'new_v', 'new_v_pool_scale': 'new_v', 'new_v_w_out': 'new_v', 'new_v_ffn2_norm': 'new_v', 'new_v_ffn2_w_gate': 'new_v', 'new_v_ffn2_w_up': 'new_v', 'new_v_ffn2_w_down': 'new_v', 'new_v_final_norm': 'new_v'}


def _forward(args):
    return _fwd_reference(*[args[k] for k in FWD_PARAMS])


def _output_shape():
    def fwd():
        inp = _fwd_setup_inputs(0)
        return _fwd_reference(*[inp[k] for k in FWD_PARAMS])
    out = _jax.eval_shape(fwd)
    return out.shape, out.dtype

N_MICROBATCH = 1
ADAM_LR = 0.001
ADAM_B1 = 0.9
ADAM_B2 = 0.999
ADAM_EPS = 1e-08
ADAM_WD = 0.01
ADAM_STEP = 10
PER_EXAMPLE_BATCH_AXIS = {'x': 0, 'c': 0, 'loss_target': 0}
SHARED_INPUTS = []
_WEIGHT_DTYPES = {'w_ada': _jnp.float32, 'b_ada': _jnp.float32, 'ffn1_norm': _jnp.float32, 'ffn1_w_gate': _jnp.float32, 'ffn1_w_up': _jnp.float32, 'ffn1_w_down': _jnp.float32, 'mix_norm': _jnp.float32, 'w_in': _jnp.float32, 'conv_w': _jnp.float32, 'conv_b': _jnp.float32, 'dt_bias': _jnp.float32, 'a_log': _jnp.float32, 'd_skip': _jnp.float32, 'ssd_norm_w': _jnp.float32, 'pool_w': _jnp.float32, 'pool_b': _jnp.float32, 'pool_scale': _jnp.float32, 'w_out': _jnp.float32, 'ffn2_norm': _jnp.float32, 'ffn2_w_gate': _jnp.float32, 'ffn2_w_up': _jnp.float32, 'ffn2_w_down': _jnp.float32, 'final_norm': _jnp.float32}
MOMENT_SCALE = {'w_ada': 4.417045e-02, 'b_ada': 7.608835e-02, 'ffn1_norm': 2.681584e-02, 'ffn1_w_gate': 1.192876e-02, 'ffn1_w_up': 1.153368e-02, 'ffn1_w_down': 1.910295e-02, 'mix_norm': 6.435694e-02, 'w_in': 3.261931e-02, 'conv_w': 2.848924e-02, 'conv_b': 3.381834e-02, 'dt_bias': 1.206007e-01, 'a_log': 9.079037e-02, 'd_skip': 2.251015e-01, 'ssd_norm_w': 4.093093e-02, 'pool_w': 3.355318e-02, 'pool_b': 4.104960e-02, 'pool_scale': 3.555735e-02, 'w_out': 4.950545e-02, 'ffn2_norm': 2.544588e-02, 'ffn2_w_gate': 1.135368e-02, 'ffn2_w_up': 1.093945e-02, 'ffn2_w_down': 1.813788e-02, 'final_norm': 3.205318e+01}


def _to_microbatches(a, axis):
    t = _jnp.moveaxis(a, axis, 0)
    t = t.reshape((N_MICROBATCH, t.shape[0] // N_MICROBATCH) + t.shape[1:])
    return _jnp.moveaxis(t, 1, axis + 1)


def setup_inputs(seed: int = 0) -> dict:
    inp = _fwd_setup_inputs(seed)
    key = _jax.random.fold_in(_jax.random.key(seed), 7919)
    shape, _ = _output_shape()
    out = dict(inp)
    out["loss_target"] = _jax.random.normal(_jax.random.fold_in(key, 0), shape, _jnp.float32)
    for i, name in enumerate(TWIN_WEIGHTS):
        w = inp[name].astype(_jnp.float32)
        if MOMENT_SCALE is None:
            s = _jnp.sqrt(_jnp.mean(_jnp.square(w)) + 1e-30)
        else:
            s = MOMENT_SCALE[name]
        km, kv = _jax.random.split(_jax.random.fold_in(key, i + 1))
        out[name] = w
        out["m_" + name] = s * _jax.random.normal(km, w.shape, _jnp.float32)
        out["v_" + name] = (s * s) * _jax.random.uniform(kv, w.shape, _jnp.float32, 0.5, 1.5)
    if N_MICROBATCH > 1:
        for name, axis in PER_EXAMPLE_BATCH_AXIS.items():
            out[name] = _to_microbatches(out[name], axis)
    return {'x': out['x'], 'c': out['c'], 'w_ada': out['w_ada'], 'b_ada': out['b_ada'], 'ffn1_norm': out['ffn1_norm'], 'ffn1_w_gate': out['ffn1_w_gate'], 'ffn1_w_up': out['ffn1_w_up'], 'ffn1_w_down': out['ffn1_w_down'], 'mix_norm': out['mix_norm'], 'w_in': out['w_in'], 'conv_w': out['conv_w'], 'conv_b': out['conv_b'], 'dt_bias': out['dt_bias'], 'a_log': out['a_log'], 'd_skip': out['d_skip'], 'ssd_norm_w': out['ssd_norm_w'], 'pool_w': out['pool_w'], 'pool_b': out['pool_b'], 'pool_scale': out['pool_scale'], 'w_out': out['w_out'], 'ffn2_norm': out['ffn2_norm'], 'ffn2_w_gate': out['ffn2_w_gate'], 'ffn2_w_up': out['ffn2_w_up'], 'ffn2_w_down': out['ffn2_w_down'], 'final_norm': out['final_norm'], 'loss_target': out['loss_target'], 'm_w_ada': out['m_w_ada'], 'm_b_ada': out['m_b_ada'], 'm_ffn1_norm': out['m_ffn1_norm'], 'm_ffn1_w_gate': out['m_ffn1_w_gate'], 'm_ffn1_w_up': out['m_ffn1_w_up'], 'm_ffn1_w_down': out['m_ffn1_w_down'], 'm_mix_norm': out['m_mix_norm'], 'm_w_in': out['m_w_in'], 'm_conv_w': out['m_conv_w'], 'm_conv_b': out['m_conv_b'], 'm_dt_bias': out['m_dt_bias'], 'm_a_log': out['m_a_log'], 'm_d_skip': out['m_d_skip'], 'm_ssd_norm_w': out['m_ssd_norm_w'], 'm_pool_w': out['m_pool_w'], 'm_pool_b': out['m_pool_b'], 'm_pool_scale': out['m_pool_scale'], 'm_w_out': out['m_w_out'], 'm_ffn2_norm': out['m_ffn2_norm'], 'm_ffn2_w_gate': out['m_ffn2_w_gate'], 'm_ffn2_w_up': out['m_ffn2_w_up'], 'm_ffn2_w_down': out['m_ffn2_w_down'], 'm_final_norm': out['m_final_norm'], 'v_w_ada': out['v_w_ada'], 'v_b_ada': out['v_b_ada'], 'v_ffn1_norm': out['v_ffn1_norm'], 'v_ffn1_w_gate': out['v_ffn1_w_gate'], 'v_ffn1_w_up': out['v_ffn1_w_up'], 'v_ffn1_w_down': out['v_ffn1_w_down'], 'v_mix_norm': out['v_mix_norm'], 'v_w_in': out['v_w_in'], 'v_conv_w': out['v_conv_w'], 'v_conv_b': out['v_conv_b'], 'v_dt_bias': out['v_dt_bias'], 'v_a_log': out['v_a_log'], 'v_d_skip': out['v_d_skip'], 'v_ssd_norm_w': out['v_ssd_norm_w'], 'v_pool_w': out['v_pool_w'], 'v_pool_b': out['v_pool_b'], 'v_pool_scale': out['v_pool_scale'], 'v_w_out': out['v_w_out'], 'v_ffn2_norm': out['v_ffn2_norm'], 'v_ffn2_w_gate': out['v_ffn2_w_gate'], 'v_ffn2_w_up': out['v_ffn2_w_up'], 'v_ffn2_w_down': out['v_ffn2_w_down'], 'v_final_norm': out['v_final_norm']}


def _loss(weights, diff, rest, loss_target):
    with _jax.named_scope("forward"):
        args = {**rest, TWIN_DIFF_INPUT: diff, **{k: w.astype(_WEIGHT_DTYPES[k]) for k, w in weights.items()}}
        y = _forward(args)
    with _jax.named_scope("loss_head"):
        err = _jnp.square(y.astype(_jnp.float32) - loss_target)
        return 0.5 * _jnp.sum(_jnp.mean(err, axis=-1)) if err.ndim else 0.5 * err


def _adamw(w, g, m, v):
    m = ADAM_B1 * m + (1.0 - ADAM_B1) * g
    v = ADAM_B2 * v + (1.0 - ADAM_B2) * _jnp.square(g)
    m_hat = m / (1.0 - ADAM_B1 ** ADAM_STEP)
    v_hat = v / (1.0 - ADAM_B2 ** ADAM_STEP)
    delta = -ADAM_LR * (m_hat / (_jnp.sqrt(v_hat) + ADAM_EPS) + ADAM_WD * w)
    return delta, m, v


def reference(x, c, w_ada, b_ada, ffn1_norm, ffn1_w_gate, ffn1_w_up, ffn1_w_down, mix_norm, w_in, conv_w, conv_b, dt_bias, a_log, d_skip, ssd_norm_w, pool_w, pool_b, pool_scale, w_out, ffn2_norm, ffn2_w_gate, ffn2_w_up, ffn2_w_down, final_norm, loss_target, m_w_ada, m_b_ada, m_ffn1_norm, m_ffn1_w_gate, m_ffn1_w_up, m_ffn1_w_down, m_mix_norm, m_w_in, m_conv_w, m_conv_b, m_dt_bias, m_a_log, m_d_skip, m_ssd_norm_w, m_pool_w, m_pool_b, m_pool_scale, m_w_out, m_ffn2_norm, m_ffn2_w_gate, m_ffn2_w_up, m_ffn2_w_down, m_final_norm, v_w_ada, v_b_ada, v_ffn1_norm, v_ffn1_w_gate, v_ffn1_w_up, v_ffn1_w_down, v_mix_norm, v_w_in, v_conv_w, v_conv_b, v_dt_bias, v_a_log, v_d_skip, v_ssd_norm_w, v_pool_w, v_pool_b, v_pool_scale, v_w_out, v_ffn2_norm, v_ffn2_w_gate, v_ffn2_w_up, v_ffn2_w_down, v_final_norm):
    given = dict(x=x, c=c, w_ada=w_ada, b_ada=b_ada, ffn1_norm=ffn1_norm, ffn1_w_gate=ffn1_w_gate, ffn1_w_up=ffn1_w_up, ffn1_w_down=ffn1_w_down, mix_norm=mix_norm, w_in=w_in, conv_w=conv_w, conv_b=conv_b, dt_bias=dt_bias, a_log=a_log, d_skip=d_skip, ssd_norm_w=ssd_norm_w, pool_w=pool_w, pool_b=pool_b, pool_scale=pool_scale, w_out=w_out, ffn2_norm=ffn2_norm, ffn2_w_gate=ffn2_w_gate, ffn2_w_up=ffn2_w_up, ffn2_w_down=ffn2_w_down, final_norm=final_norm, loss_target=loss_target, m_w_ada=m_w_ada, m_b_ada=m_b_ada, m_ffn1_norm=m_ffn1_norm, m_ffn1_w_gate=m_ffn1_w_gate, m_ffn1_w_up=m_ffn1_w_up, m_ffn1_w_down=m_ffn1_w_down, m_mix_norm=m_mix_norm, m_w_in=m_w_in, m_conv_w=m_conv_w, m_conv_b=m_conv_b, m_dt_bias=m_dt_bias, m_a_log=m_a_log, m_d_skip=m_d_skip, m_ssd_norm_w=m_ssd_norm_w, m_pool_w=m_pool_w, m_pool_b=m_pool_b, m_pool_scale=m_pool_scale, m_w_out=m_w_out, m_ffn2_norm=m_ffn2_norm, m_ffn2_w_gate=m_ffn2_w_gate, m_ffn2_w_up=m_ffn2_w_up, m_ffn2_w_down=m_ffn2_w_down, m_final_norm=m_final_norm, v_w_ada=v_w_ada, v_b_ada=v_b_ada, v_ffn1_norm=v_ffn1_norm, v_ffn1_w_gate=v_ffn1_w_gate, v_ffn1_w_up=v_ffn1_w_up, v_ffn1_w_down=v_ffn1_w_down, v_mix_norm=v_mix_norm, v_w_in=v_w_in, v_conv_w=v_conv_w, v_conv_b=v_conv_b, v_dt_bias=v_dt_bias, v_a_log=v_a_log, v_d_skip=v_d_skip, v_ssd_norm_w=v_ssd_norm_w, v_pool_w=v_pool_w, v_pool_b=v_pool_b, v_pool_scale=v_pool_scale, v_w_out=v_w_out, v_ffn2_norm=v_ffn2_norm, v_ffn2_w_gate=v_ffn2_w_gate, v_ffn2_w_up=v_ffn2_w_up, v_ffn2_w_down=v_ffn2_w_down, v_final_norm=v_final_norm)
    weights = {n: given[n] for n in TWIN_WEIGHTS}
    shared = {n: given[n] for n in SHARED_INPUTS}
    per_example = {n: given[n] for n in ['x', 'c']}
    grad_fn = _jax.value_and_grad(_loss, argnums=(0, 1))

    def one_microbatch(ex, loss_target):
        ex = dict(ex)
        diff = ex.pop(TWIN_DIFF_INPUT)
        return grad_fn(weights, diff, {**shared, **ex}, loss_target)

    if N_MICROBATCH == 1:
        loss, (grad_w, grad_x) = one_microbatch(per_example, given["loss_target"])
    else:
        def body(carry, xs):
            loss_sum, grad_sum = carry
            l_k, (gw_k, gx_k) = one_microbatch(xs[0], xs[1])
            with _jax.named_scope("update"):
                return (loss_sum + l_k, _jax.tree.map(_jnp.add, grad_sum, gw_k)), gx_k

        init = (_jnp.zeros((), _jnp.float32), _jax.tree.map(_jnp.zeros_like, weights))
        (loss, grad_w), grad_x = _jax.lax.scan(body, init, (per_example, given["loss_target"]))
    with _jax.named_scope("update"):
        delta_w, new_m, new_v = {}, {}, {}
        for n in TWIN_WEIGHTS:
            delta_w[n], new_m[n], new_v[n] = _adamw(weights[n], grad_w[n], given["m_" + n], given["v_" + n])
    return (loss, grad_x, *[grad_w[n] for n in TWIN_WEIGHTS], *[delta_w[n] for n in TWIN_WEIGHTS],
            *[new_m[n] for n in TWIN_WEIGHTS], *[new_v[n] for n in TWIN_WEIGHTS])
```

```python
import functools
import math

import jax
import jax.numpy as jnp
from jax import lax
from jax.experimental import pallas as pl
from jax.experimental.pallas import tpu as pltpu

F32 = jnp.float32
BF16 = jnp.bfloat16
MESH = pl.DeviceIdType.MESH

N_DEV = 8
D = 1024
FF = 2816
D_SSD = 1024
N_HEADS = 16
HEAD_DIM = 64
N_GROUPS = 4
N_STATE = 128
CHUNK = 128
GROUP_W = D_SSD // N_GROUPS
D_XBC = D_SSD + 2 * N_GROUPS * N_STATE
D_POOL = 1024
POOL_WINDOWS = (2, 4, 8, 16)
POOL_GW = 256
D_IN = 4112
D_IN_PAD = 4224
COL_Z, COL_XBC, COL_U, COL_DT = 0, 1024, 3072, 4096
EPS = 1e-6
FFN_RES = 0.5
LANE = 128
HALO = 16

ADAM_LR, ADAM_B1, ADAM_B2, ADAM_EPS, ADAM_WD, ADAM_STEP = 0.001, 0.9, 0.999, 1e-08, 0.01, 10

VMEM_LIMIT = 56 << 20


def _cp(sem):
    return pltpu.CompilerParams(dimension_semantics=sem, vmem_limit_bytes=VMEM_LIMIT)


def _dot(a, b, ca, cb, prec=None):
    return lax.dot_general(a, b, (((ca,), (cb,)), ((), ())), precision=prec,
                           preferred_element_type=F32)


def _exact_dot(a, b):
    return _dot(a, b, 1, 0, lax.Precision.HIGHEST)


def _sigmoid(v):
    return 1.0 / (1.0 + jnp.exp(-v))


def _silu_grad(v, sg):
    return sg * (1.0 + v * (1.0 - sg))


def _mm_nt(a, bt, tm, tn, out_dtype, name):
    m, k = a.shape
    n = bt.shape[0]

    def body(a_ref, b_ref, o_ref):
        o_ref[...] = _dot(a_ref[...], b_ref[...], 1, 1).astype(out_dtype)

    return pl.pallas_call(
        body, name=name, grid=(n // tn, m // tm),
        in_specs=[pl.BlockSpec((tm, k), lambda j, i: (i, 0)),
                  pl.BlockSpec((tn, k), lambda j, i: (j, 0))],
        out_specs=pl.BlockSpec((tm, tn), lambda j, i: (i, j)),
        out_shape=jax.ShapeDtypeStruct((m, n), out_dtype),
        compiler_params=_cp(("parallel", "parallel")))(a, bt)


def _mm_tn(a, b, tm, tk, name):
    kk, m = a.shape
    n = b.shape[1]
    nk = kk // tk

    def body(a_ref, b_ref, o_ref, acc):
        k = pl.program_id(1)

        @pl.when(k == 0)
        def _():
            acc[...] = jnp.zeros_like(acc)

        acc[...] += _dot(a_ref[...], b_ref[...], 0, 0)

        @pl.when(k == nk - 1)
        def _():
            o_ref[...] = acc[...]

    return pl.pallas_call(
        body, name=name, grid=(m // tm, nk),
        in_specs=[pl.BlockSpec((tk, tm), lambda i, k: (k, i)),
                  pl.BlockSpec((tk, n), lambda i, k: (k, 0))],
        out_specs=pl.BlockSpec((tm, n), lambda i, k: (i, 0)),
        out_shape=jax.ShapeDtypeStruct((m, n), F32),
        scratch_shapes=[pltpu.VMEM((tm, n), F32)],
        compiler_params=_cp(("parallel", "arbitrary")))(a, b)


def _prenorm(x, wn, mod, k, tm, name):
    seq = x.shape[0]

    def body(x_ref, wn_ref, mod_ref, h_ref):
        xv = x_ref[...]
        r = lax.rsqrt(jnp.mean(xv * xv, axis=-1, keepdims=True) + EPS)
        hn = xv * r * wn_ref[...]
        h_ref[...] = (hn * (1.0 + mod_ref[3 * k + 1:3 * k + 2, :]) + mod_ref[3 * k:3 * k + 1, :]).astype(BF16)

    return pl.pallas_call(
        body, name=name, grid=(seq // tm,),
        in_specs=[pl.BlockSpec((tm, D), lambda i: (i, 0)),
                  pl.BlockSpec((1, D), lambda i: (0, 0)),
                  pl.BlockSpec((9, D), lambda i: (0, 0))],
        out_specs=pl.BlockSpec((tm, D), lambda i: (i, 0)),
        out_shape=jax.ShapeDtypeStruct((seq, D), BF16),
        compiler_params=_cp(("parallel",)))(x, wn, mod)


def _scale_cast(dxo, mod, row, res, tm, name):
    seq = dxo.shape[0]

    def body(d_ref, mod_ref, o_ref):
        o_ref[...] = (d_ref[...] * (res * mod_ref[row:row + 1, :])).astype(BF16)

    return pl.pallas_call(
        body, name=name, grid=(seq // tm,),
        in_specs=[pl.BlockSpec((tm, D), lambda i: (i, 0)),
                  pl.BlockSpec((9, D), lambda i: (0, 0))],
        out_specs=pl.BlockSpec((tm, D), lambda i: (i, 0)),
        out_shape=jax.ShapeDtypeStruct((seq, D), BF16),
        compiler_params=_cp(("parallel",)))(dxo, mod)


def _norm_bwd(dh, xv, dxo, branch, wn, sc, res, stats_ref, first):
    r = lax.rsqrt(jnp.mean(xv * xv, axis=-1, keepdims=True) + EPS)
    xn = xv * r
    dhn = dh * (1.0 + sc)
    dxn = dhn * wn
    dx = dxo + r * (dxn - xn * jnp.mean(dxn * xn, axis=-1, keepdims=True))
    rows = jnp.concatenate([
        jnp.sum(dh, axis=0, keepdims=True),
        jnp.sum(dh * (xn * wn), axis=0, keepdims=True),
        jnp.sum(branch * dxo, axis=0, keepdims=True) * res,
        jnp.sum(dhn * xn, axis=0, keepdims=True),
        jnp.zeros((4, D), F32)], axis=0)

    @pl.when(first)
    def _():
        stats_ref[...] = rows

    @pl.when(jnp.logical_not(first))
    def _():
        stats_ref[...] += rows

    return dx


def _loss_head(x3, wf, tgt, tm, name):
    seq = x3.shape[0]

    def body(x_ref, w_ref, t_ref, dx_ref, st_ref):
        xv = x_ref[...]
        wv = w_ref[...]
        r = lax.rsqrt(jnp.mean(xv * xv, axis=-1, keepdims=True) + EPS)
        xn = xv * r
        e = xn * wv - t_ref[...]
        dy = e * (1.0 / D)
        dxn = dy * wv
        dx_ref[...] = r * (dxn - xn * jnp.mean(dxn * xn, axis=-1, keepdims=True))
        rows = jnp.concatenate([
            jnp.sum(dy * xn, axis=0, keepdims=True),
            jnp.sum(e * e, axis=0, keepdims=True) * (0.5 / D),
            jnp.zeros((6, D), F32)], axis=0)

        @pl.when(pl.program_id(0) == 0)
        def _():
            st_ref[...] = rows

        @pl.when(pl.program_id(0) != 0)
        def _():
            st_ref[...] += rows

    return pl.pallas_call(
        body, name=name, grid=(seq // tm,),
        in_specs=[pl.BlockSpec((tm, D), lambda i: (i, 0)),
                  pl.BlockSpec((1, D), lambda i: (0, 0)),
                  pl.BlockSpec((tm, D), lambda i: (i, 0))],
        out_specs=[pl.BlockSpec((tm, D), lambda i: (i, 0)),
                   pl.BlockSpec((8, D), lambda i: (0, 0))],
        out_shape=[jax.ShapeDtypeStruct((seq, D), F32), jax.ShapeDtypeStruct((8, D), F32)],
        compiler_params=_cp(("arbitrary",)))(x3, wf, tgt)


def _ffn_up(h, w, blk, tm, tn, name):
    seq = h.shape[0]
    nj = FF // tn

    def body(h_ref, wg_ref, wu_ref, g_ref, u_ref, a_ref):
        hv = h_ref[...]
        g = _dot(hv, wg_ref[...], 1, 1)
        u = _dot(hv, wu_ref[...], 1, 1)
        g_ref[...] = g.astype(BF16)
        u_ref[...] = u.astype(BF16)
        a_ref[...] = (g * _sigmoid(g) * u).astype(BF16)

    act = pl.BlockSpec((tm, tn), lambda j, i: (i, j))
    return pl.pallas_call(
        body, name=name, grid=(nj, seq // tm),
        in_specs=[pl.BlockSpec((tm, D), lambda j, i: (i, 0)),
                  pl.BlockSpec((tn, D), lambda j, i: (blk * nj + j, 0)),
                  pl.BlockSpec((tn, D), lambda j, i: ((blk + 1) * nj + j, 0))],
        out_specs=[act, act, act],
        out_shape=[jax.ShapeDtypeStruct((seq, FF), BF16)] * 3,
        compiler_params=_cp(("parallel", "parallel")))(h, w, w)


def _ffn_down(a, w, blk, x, mod, grow, tm, tk, name):
    seq = a.shape[0]
    nk = FF // tk

    def body(a_ref, w_ref, x_ref, mod_ref, xo_ref, f_ref, acc):
        k = pl.program_id(1)

        @pl.when(k == 0)
        def _():
            acc[...] = jnp.zeros_like(acc)

        acc[...] += _dot(a_ref[...], w_ref[...], 1, 0)

        @pl.when(k == nk - 1)
        def _():
            f = acc[...]
            f_ref[...] = f.astype(BF16)
            xo_ref[...] = x_ref[...] + (FFN_RES * mod_ref[grow:grow + 1, :]) * f

    return pl.pallas_call(
        body, name=name, grid=(seq // tm, nk),
        in_specs=[pl.BlockSpec((tm, tk), lambda i, k: (i, k)),
                  pl.BlockSpec((tk, D), lambda i, k: ((blk + 2) * nk + k, 0)),
                  pl.BlockSpec((tm, D), lambda i, k: (i, 0)),
                  pl.BlockSpec((9, D), lambda i, k: (0, 0))],
        out_specs=[pl.BlockSpec((tm, D), lambda i, k: (i, 0)),
                   pl.BlockSpec((tm, D), lambda i, k: (i, 0))],
        out_shape=[jax.ShapeDtypeStruct((seq, D), F32), jax.ShapeDtypeStruct((seq, D), BF16)],
        scratch_shapes=[pltpu.VMEM((tm, D), F32)],
        compiler_params=_cp(("parallel", "arbitrary")))(a, w, x, mod)


def _ffn_bwd_da(df, w, blk, g, u, tm, tn, name):
    seq = df.shape[0]
    nj = FF // tn

    def body(df_ref, w_ref, g_ref, u_ref, dg_ref, du_ref):
        da = _dot(df_ref[...], w_ref[...], 1, 1)
        gv = g_ref[...].astype(F32)
        uv = u_ref[...].astype(F32)
        sg = _sigmoid(gv)
        dg_ref[...] = (da * uv * _silu_grad(gv, sg)).astype(BF16)
        du_ref[...] = (da * (gv * sg)).astype(BF16)

    act = pl.BlockSpec((tm, tn), lambda j, i: (i, j))
    return pl.pallas_call(
        body, name=name, grid=(nj, seq // tm),
        in_specs=[pl.BlockSpec((tm, D), lambda j, i: (i, 0)),
                  pl.BlockSpec((tn, D), lambda j, i: ((blk + 2) * nj + j, 0)),
                  act, act],
        out_specs=[act, act],
        out_shape=[jax.ShapeDtypeStruct((seq, FF), BF16)] * 2,
        compiler_params=_cp(("parallel", "parallel")))(df, w, g, u)


def _ffn_bwd_dh(dg, du, w, blk, x, dxo, fb, wn, mod, k, tm, tk, name):
    seq = x.shape[0]
    nk = FF // tk

    def body(dg_ref, du_ref, wg_ref, wu_ref, x_ref, dxo_ref, f_ref, wn_ref, mod_ref, dx_ref, st_ref, acc):
        kk = pl.program_id(1)
        first = pl.program_id(0) == 0

        @pl.when(kk == 0)
        def _():
            acc[...] = jnp.zeros_like(acc)

        acc[...] += _dot(dg_ref[...], wg_ref[...], 1, 0) + _dot(du_ref[...], wu_ref[...], 1, 0)

        @pl.when(kk == nk - 1)
        def _():
            dx_ref[...] = _norm_bwd(acc[...], x_ref[...], dxo_ref[...], f_ref[...].astype(F32), wn_ref[...],
                                    mod_ref[3 * k + 1:3 * k + 2, :], FFN_RES, st_ref, first)

    tok = pl.BlockSpec((tm, D), lambda i, kk: (i, 0))
    return pl.pallas_call(
        body, name=name, grid=(seq // tm, nk),
        in_specs=[pl.BlockSpec((tm, tk), lambda i, kk: (i, kk)),
                  pl.BlockSpec((tm, tk), lambda i, kk: (i, kk)),
                  pl.BlockSpec((tk, D), lambda i, kk: (blk * nk + kk, 0)),
                  pl.BlockSpec((tk, D), lambda i, kk: ((blk + 1) * nk + kk, 0)),
                  tok, tok, tok,
                  pl.BlockSpec((1, D), lambda i, kk: (0, 0)),
                  pl.BlockSpec((9, D), lambda i, kk: (0, 0))],
        out_specs=[tok, pl.BlockSpec((8, D), lambda i, kk: (0, 0))],
        out_shape=[jax.ShapeDtypeStruct((seq, D), F32), jax.ShapeDtypeStruct((8, D), F32)],
        scratch_shapes=[pltpu.VMEM((tm, D), F32)],
        compiler_params=_cp(("arbitrary", "arbitrary")))(dg, du, w, w, x, dxo, fb, wn, mod)


def _ffn_forward(x, w, blk, wn, mod, k, tm, tag):
    h = _prenorm(x, wn, mod, k, tm, f"{tag}_prenorm")
    g, u, a = _ffn_up(h, w, blk, tm, FF // 2, f"{tag}_up")
    xo, fb = _ffn_down(a, w, blk, x, mod, 3 * k + 2, tm, FF // 2, f"{tag}_down")
    return xo, (x, h, g, u, a, fb)


def _ffn_backward(dxo, saved, w, blk, wn, mod, k, tm, tag):
    x, h, g, u, a, fb = saved
    df = _scale_cast(dxo, mod, 3 * k + 2, FFN_RES, tm, f"{tag}_dbranch")
    dg, du = _ffn_bwd_da(df, w, blk, g, u, tm, FF // 2, f"{tag}_bwd_da")
    dx, stats = _ffn_bwd_dh(dg, du, w, blk, x, dxo, fb, wn, mod, k, tm, FF // 2, f"{tag}_bwd_dh")
    seq = x.shape[0]
    tk = min(seq, 512)
    d_gate_t = _mm_tn(dg, h, FF // 2, tk, f"{tag}_dw_gate")
    d_up_t = _mm_tn(du, h, FF // 2, tk, f"{tag}_dw_up")
    d_down = _mm_tn(a, df, FF // 2, tk, f"{tag}_dw_down")
    return dx, stats, (d_gate_t, d_up_t, d_down)


def _prev_rows(tm, col):
    return pl.BlockSpec((HALO, 1024), lambda i, j: (jnp.maximum(i * (tm // HALO) - 1, 0), col + j))


def _conv_pre(ext, cw, cb, rows):
    pre = cb + cw[3:4, :] * ext
    for s in (1, 2, 3):
        pre = pre + cw[3 - s:4 - s, :] * pltpu.roll(ext, s, 0)
    return pre[HALO:HALO + rows]


def _conv_fwd(proj, cw, cb, tm, name):
    seq = proj.shape[0]

    def body(x_ref, p_ref, cw_ref, cb_ref, o_ref):
        prev = jnp.where(pl.program_id(0) == 0, 0.0, p_ref[...])
        ext = jnp.concatenate([prev, x_ref[...]], axis=0)
        pre = _conv_pre(ext, cw_ref[...], cb_ref[...], tm)
        o_ref[...] = pre * _sigmoid(pre)

    c0 = COL_XBC // 1024
    return pl.pallas_call(
        body, name=name, grid=(seq // tm, 2),
        in_specs=[pl.BlockSpec((tm, 1024), lambda i, j: (i, c0 + j)),
                  _prev_rows(tm, c0),
                  pl.BlockSpec((4, 1024), lambda i, j: (0, j)),
                  pl.BlockSpec((1, 1024), lambda i, j: (0, j))],
        out_specs=pl.BlockSpec((tm, 1024), lambda i, j: (i, j)),
        out_shape=jax.ShapeDtypeStruct((seq, D_XBC), F32),
        compiler_params=_cp(("parallel", "parallel")))(proj, proj, cw, cb)


def _conv_bwd(dact, proj, cw, cb, tm, name):
    seq = proj.shape[0]
    ni = seq // tm

    def body(d_ref, dn_ref, x_ref, p_ref, n_ref, cw_ref, cb_ref, o_ref, st_ref):
        i = pl.program_id(1)
        cwv = cw_ref[...]
        prev = jnp.where(i == 0, 0.0, p_ref[...])
        ext = jnp.concatenate([prev, x_ref[...], n_ref[...]], axis=0)
        pre = _conv_pre(ext, cwv, cb_ref[...], tm + HALO)
        dnext = jnp.where(i == ni - 1, 0.0, dn_ref[...])
        dext = jnp.concatenate([d_ref[...], dnext], axis=0)
        dpre = dext * _silu_grad(pre, _sigmoid(pre))
        n = tm + HALO
        dx = cwv[3:4, :] * dpre
        for s in (1, 2, 3):
            dx = dx + cwv[3 - s:4 - s, :] * pltpu.roll(dpre, n - s, 0)
        o_ref[...] = dx[:tm].astype(BF16)
        dcur = dpre[:tm]
        rows = [jnp.sum(dcur * pltpu.roll(ext, 3 - k, 0)[HALO:HALO + tm], axis=0, keepdims=True) for k in range(3)]
        rows.append(jnp.sum(dcur * ext[HALO:HALO + tm], axis=0, keepdims=True))
        rows.append(jnp.sum(dcur, axis=0, keepdims=True))
        rows.append(jnp.zeros((3, 1024), F32))
        rows = jnp.concatenate(rows, axis=0)

        @pl.when(i == 0)
        def _():
            st_ref[...] = rows

        @pl.when(i != 0)
        def _():
            st_ref[...] += rows

    c0 = COL_XBC // 1024
    return pl.pallas_call(
        body, name=name, grid=(2, ni),
        in_specs=[pl.BlockSpec((tm, 1024), lambda j, i: (i, j)),
                  pl.BlockSpec((HALO, 1024), lambda j, i: (jnp.minimum((i + 1) * (tm // HALO), seq // HALO - 1), j)),
                  pl.BlockSpec((tm, 1024), lambda j, i: (i, c0 + j)),
                  pl.BlockSpec((HALO, 1024), lambda j, i: (jnp.maximum(i * (tm // HALO) - 1, 0), c0 + j)),
                  pl.BlockSpec((HALO, 1024), lambda j, i: (jnp.minimum((i + 1) * (tm // HALO), seq // HALO - 1), c0 + j)),
                  pl.BlockSpec((4, 1024), lambda j, i: (0, j)),
                  pl.BlockSpec((1, 1024), lambda j, i: (0, j))],
        out_specs=[pl.BlockSpec((tm, 1024), lambda j, i: (i, j)),
                   pl.BlockSpec((8, 1024), lambda j, i: (0, j))],
        out_shape=[jax.ShapeDtypeStruct((seq, D_XBC), BF16), jax.ShapeDtypeStruct((8, D_XBC), F32)],
        compiler_params=_cp(("parallel", "arbitrary")))(dact, dact, proj, proj, proj, cw, cb)


def _head_expand():
    r = lax.broadcasted_iota(jnp.int32, (LANE, D_SSD), 0)
    c = lax.broadcasted_iota(jnp.int32, (LANE, D_SSD), 1)
    return (c // HEAD_DIM == r).astype(F32)


def _head_reduce():
    r = lax.broadcasted_iota(jnp.int32, (D_SSD, LANE), 0)
    c = lax.broadcasted_iota(jnp.int32, (D_SSD, LANE), 1)
    return (r // HEAD_DIM == c).astype(F32)


def _ssd_common(dtr, par):
    q = CHUNK
    v = dtr + par[0:1, :]
    dt = jnp.maximum(v, 0.0) + jnp.log(1.0 + jnp.exp(-jnp.abs(v)))
    a = -jnp.exp(par[1:2, :])
    adt = dt * a
    li = lax.broadcasted_iota(jnp.int32, (q, q), 0)
    si = lax.broadcasted_iota(jnp.int32, (q, q), 1)
    causal = li >= si
    acs = _exact_dot(causal.astype(F32), adt)
    expand = _head_expand()
    dt_l = _exact_dot(dt, expand)
    acs_l = _exact_dot(acs, expand)
    par_l = _exact_dot(par, expand)
    last_l = acs_l[q - 1:q, :]
    return dict(v=v, dt=dt, a=a, acs=acs, acs_t=acs.T, causal=causal, dt_l=dt_l, acs_l=acs_l,
                ea_l=jnp.exp(acs_l), ds_l=jnp.exp(last_l - acs_l), cd_l=jnp.exp(last_l), dskip_l=par_l[2:3, :])


def _decay(cm, h):
    seg = cm["acs"][:, h:h + 1] - cm["acs_t"][h:h + 1, :]
    return jnp.exp(jnp.where(cm["causal"], seg, -jnp.inf))


def _lane_mask(r):
    lane = lax.broadcasted_iota(jnp.int32, (1, GROUP_W), 1)
    return lane // HEAD_DIM == r


def _ssd_fwd(xbc, proj, par, name):
    seq = xbc.shape[0]
    nc = seq // CHUNK
    q = CHUNK

    def body(x_ref, dt_ref, par_ref, y_ref, hp_ref, state):
        @pl.when(pl.program_id(0) == 0)
        def _():
            state[...] = jnp.zeros_like(state)

        cm = _ssd_common(dt_ref[...], par_ref[...])
        for g in range(N_GROUPS):
            lo = g * GROUP_W
            xs = x_ref[:, lo:lo + GROUP_W]
            bm = x_ref[:, D_SSD + g * N_STATE:D_SSD + (g + 1) * N_STATE].astype(BF16)
            cmat = x_ref[:, D_SSD + N_GROUPS * N_STATE + g * N_STATE:D_SSD + N_GROUPS * N_STATE + (g + 1) * N_STATE].astype(BF16)
            xdt = xs * cm["dt_l"][:, lo:lo + GROUP_W]
            xdt_b = xdt.astype(BF16)
            cb = _dot(cmat, bm, 1, 1)
            yd = jnp.zeros((q, GROUP_W), F32)
            for r in range(4):
                s_h = (cb * _decay(cm, 4 * g + r)).astype(BF16)
                yd = jnp.where(_lane_mask(r), _dot(s_h, xdt_b, 1, 0), yd)
            hg = state[g]
            hp_ref[0, g] = hg
            yo = _dot(cmat, hg.astype(BF16), 1, 0) * cm["ea_l"][:, lo:lo + GROUP_W]
            y_ref[:, lo:lo + GROUP_W] = yd + yo + cm["dskip_l"][:, lo:lo + GROUP_W] * xs
            xds = (xdt * cm["ds_l"][:, lo:lo + GROUP_W]).astype(BF16)
            state[g] = hg * cm["cd_l"][:, lo:lo + GROUP_W] + _dot(bm, xds, 0, 0)

    return pl.pallas_call(
        body, name=name, grid=(nc,),
        in_specs=[pl.BlockSpec((q, D_XBC), lambda c: (c, 0)),
                  pl.BlockSpec((q, LANE), lambda c: (c, COL_DT // LANE)),
                  pl.BlockSpec((8, LANE), lambda c: (0, 0))],
        out_specs=[pl.BlockSpec((q, D_SSD), lambda c: (c, 0)),
                   pl.BlockSpec((1, N_GROUPS, N_STATE, GROUP_W), lambda c: (c, 0, 0, 0))],
        out_shape=[jax.ShapeDtypeStruct((seq, D_SSD), F32),
                   jax.ShapeDtypeStruct((nc, N_GROUPS, N_STATE, GROUP_W), F32)],
        scratch_shapes=[pltpu.VMEM((N_GROUPS, N_STATE, GROUP_W), F32)],
        compiler_params=_cp(("arbitrary",)))(xbc, proj, par)


def _ssd_bwd(dy, xbc, proj, par, hprev, name):
    seq = xbc.shape[0]
    nc = seq // CHUNK
    q = CHUNK

    def body(dy_ref, x_ref, dt_ref, par_ref, hp_ref, dx_ref, ddt_ref, st_ref, dstate):
        step = pl.program_id(0)

        @pl.when(step == 0)
        def _():
            dstate[...] = jnp.zeros_like(dstate)

        par = par_ref[...]
        cm = _ssd_common(dt_ref[...], par)
        reduce = _head_reduce()
        lane128 = lax.broadcasted_iota(jnp.int32, (1, LANE), 1)
        row128 = lax.broadcasted_iota(jnp.int32, (LANE, 1), 0)
        d_acs = jnp.zeros((q, LANE), F32)
        d_acs_t = jnp.zeros((LANE, q), F32)
        last_terms = []
        acs_terms = []
        dxdt_all = []
        for g in range(N_GROUPS):
            lo = g * GROUP_W
            sl = slice(lo, lo + GROUP_W)
            xs = x_ref[:, sl]
            bm32 = x_ref[:, D_SSD + g * N_STATE:D_SSD + (g + 1) * N_STATE]
            cm32 = x_ref[:, D_SSD + N_GROUPS * N_STATE + g * N_STATE:D_SSD + N_GROUPS * N_STATE + (g + 1) * N_STATE]
            bm = bm32.astype(BF16)
            cmat = cm32.astype(BF16)
            dyg = dy_ref[:, sl]
            dyg_b = dyg.astype(BF16)
            xdt = xs * cm["dt_l"][:, sl]
            xdt_b = xdt.astype(BF16)
            hg = hp_ref[0, g]
            hg_b = hg.astype(BF16)
            dhg = dstate[g]
            dhg_b = dhg.astype(BF16)
            ea = cm["ea_l"][:, sl]
            ds = cm["ds_l"][:, sl]
            cd = cm["cd_l"][:, sl]
            yoff = _dot(cmat, hg_b, 1, 0) * ea
            dw = (dyg * ea).astype(BF16)
            d_c = _dot(dw, hg_b, 1, 1)
            d_hprev = _dot(cmat, dw, 0, 0) + dhg * cd
            t_acs = dyg * yoff
            d_last_g = jnp.sum(dhg * hg, axis=0, keepdims=True) * cd
            xds_b = (xdt * ds).astype(BF16)
            dxds = _dot(bm, dhg_b, 1, 0)
            d_b = _dot(xds_b, dhg_b, 1, 1)
            dxdt = dxds * ds
            t_ds = dxds * xdt * ds
            t_acs = t_acs - t_ds
            d_last_g = d_last_g + jnp.sum(t_ds, axis=0, keepdims=True)
            cb = _dot(cmat, bm, 1, 1)
            d_cb = jnp.zeros((q, q), F32)
            for r in range(4):
                h = 4 * g + r
                dec = _decay(cm, h)
                s_h = cb * dec
                mask = _lane_mask(r)
                d_s = _dot(jnp.where(mask, dyg, 0.0).astype(BF16), xdt_b, 1, 1)
                dxdt = dxdt + jnp.where(mask, _dot(s_h.astype(BF16), dyg_b, 0, 0), 0.0)
                d_cb = d_cb + d_s * dec
                d_m = d_s * s_h
                d_acs = d_acs + jnp.where(lane128 == h, jnp.sum(d_m, axis=1, keepdims=True), 0.0)
                d_acs_t = d_acs_t + jnp.where(row128 == h, jnp.sum(d_m, axis=0, keepdims=True), 0.0)
            d_cb_b = d_cb.astype(BF16)
            d_c = d_c + _dot(d_cb_b, bm, 1, 0)
            d_b = d_b + _dot(d_cb_b, cmat, 0, 0)
            dstate[g] = d_hprev
            dx_ref[:, sl] = dxdt * cm["dt_l"][:, sl] + cm["dskip_l"][:, sl] * dyg
            dx_ref[:, D_SSD + g * N_STATE:D_SSD + (g + 1) * N_STATE] = d_b
            dx_ref[:, D_SSD + N_GROUPS * N_STATE + g * N_STATE:D_SSD + N_GROUPS * N_STATE + (g + 1) * N_STATE] = d_c
            acs_terms.append(t_acs)
            dxdt_all.append(dxdt * xs)
            last_terms.append(d_last_g)
        t_acs_l = jnp.concatenate(acs_terms, axis=1)
        d_dt_l = jnp.concatenate(dxdt_all, axis=1)
        d_last_l = jnp.concatenate(last_terms, axis=1)
        d_acs = d_acs + _exact_dot(t_acs_l, reduce) - d_acs_t.T
        last_row = lax.broadcasted_iota(jnp.int32, (q, 1), 0) == q - 1
        d_acs = d_acs + jnp.where(last_row, _exact_dot(jnp.broadcast_to(d_last_l, (8, D_SSD)), reduce)[0:1, :], 0.0)
        li = lax.broadcasted_iota(jnp.int32, (q, q), 0)
        si = lax.broadcasted_iota(jnp.int32, (q, q), 1)
        d_adt = _exact_dot((si >= li).astype(F32), d_acs)
        d_dt = _exact_dot(d_dt_l, reduce) + d_adt * cm["a"]
        d_dtr = d_dt * _sigmoid(cm["v"])
        ddt_ref[...] = d_dtr.astype(BF16)
        d_skip = _exact_dot(jnp.broadcast_to(jnp.sum(dy_ref[...] * x_ref[:, 0:D_SSD], axis=0, keepdims=True), (8, D_SSD)), reduce)[0:1, :]
        rows = jnp.concatenate([
            jnp.sum(d_dtr, axis=0, keepdims=True),
            jnp.sum(d_adt * cm["dt"], axis=0, keepdims=True) * cm["a"],
            d_skip,
            jnp.zeros((5, LANE), F32)], axis=0)

        @pl.when(step == 0)
        def _():
            st_ref[...] = rows

        @pl.when(step != 0)
        def _():
            st_ref[...] += rows

    rev = lambda c: nc - 1 - c
    return pl.pallas_call(
        body, name=name, grid=(nc,),
        in_specs=[pl.BlockSpec((q, D_SSD), lambda c: (rev(c), 0)),
                  pl.BlockSpec((q, D_XBC), lambda c: (rev(c), 0)),
                  pl.BlockSpec((q, LANE), lambda c: (rev(c), COL_DT // LANE)),
                  pl.BlockSpec((8, LANE), lambda c: (0, 0)),
                  pl.BlockSpec((1, N_GROUPS, N_STATE, GROUP_W), lambda c: (rev(c), 0, 0, 0))],
        out_specs=[pl.BlockSpec((q, D_XBC), lambda c: (rev(c), 0)),
                   pl.BlockSpec((q, LANE), lambda c: (rev(c), 0)),
                   pl.BlockSpec((8, LANE), lambda c: (0, 0))],
        out_shape=[jax.ShapeDtypeStruct((seq, D_XBC), F32),
                   jax.ShapeDtypeStruct((seq, LANE), BF16),
                   jax.ShapeDtypeStruct((8, LANE), F32)],
        scratch_shapes=[pltpu.VMEM((N_GROUPS, N_STATE, GROUP_W), F32)],
        compiler_params=_cp(("arbitrary",)))(dy, xbc, proj, par, hprev)


def _gate_norm_fwd(y, proj, wn, tm, name):
    seq = y.shape[0]

    def body(y_ref, z_ref, w_ref, o_ref):
        for g in range(N_GROUPS):
            sl = slice(g * GROUP_W, (g + 1) * GROUP_W)
            zv = z_ref[:, sl]
            yz = y_ref[:, sl] * (zv * _sigmoid(zv))
            r = lax.rsqrt(jnp.mean(yz * yz, axis=-1, keepdims=True) + EPS)
            o_ref[:, sl] = (yz * r * w_ref[:, sl]).astype(BF16)

    tok = pl.BlockSpec((tm, D_SSD), lambda i: (i, 0))
    return pl.pallas_call(
        body, name=name, grid=(seq // tm,),
        in_specs=[tok, tok, pl.BlockSpec((1, D_SSD), lambda i: (0, 0))],
        out_specs=tok,
        out_shape=jax.ShapeDtypeStruct((seq, D_SSD), BF16),
        compiler_params=_cp(("parallel",)))(y, proj, wn)


def _gate_norm_bwd(dys, y, proj, wn, tm, name):
    seq = y.shape[0]

    def body(d_ref, y_ref, z_ref, w_ref, dy_ref, dz_ref, st_ref):
        rows = []
        for g in range(N_GROUPS):
            sl = slice(g * GROUP_W, (g + 1) * GROUP_W)
            zv = z_ref[:, sl]
            yv = y_ref[:, sl]
            sg = _sigmoid(zv)
            sz = zv * sg
            yz = yv * sz
            r = lax.rsqrt(jnp.mean(yz * yz, axis=-1, keepdims=True) + EPS)
            yn = yz * r
            dv = d_ref[:, sl]
            dyn = dv * w_ref[:, sl]
            dyz = r * (dyn - yn * jnp.mean(dyn * yn, axis=-1, keepdims=True))
            dy_ref[:, sl] = dyz * sz
            dz_ref[:, sl] = (dyz * yv * _silu_grad(zv, sg)).astype(BF16)
            rows.append(jnp.sum(dv * yn, axis=0, keepdims=True))
        rows = jnp.concatenate([jnp.concatenate(rows, axis=1), jnp.zeros((7, D_SSD), F32)], axis=0)

        @pl.when(pl.program_id(0) == 0)
        def _():
            st_ref[...] = rows

        @pl.when(pl.program_id(0) != 0)
        def _():
            st_ref[...] += rows

    tok = pl.BlockSpec((tm, D_SSD), lambda i: (i, 0))
    return pl.pallas_call(
        body, name=name, grid=(seq // tm,),
        in_specs=[tok, tok, tok, pl.BlockSpec((1, D_SSD), lambda i: (0, 0))],
        out_specs=[tok, tok, pl.BlockSpec((8, D_SSD), lambda i: (0, 0))],
        out_shape=[jax.ShapeDtypeStruct((seq, D_SSD), F32), jax.ShapeDtypeStruct((seq, D_SSD), BF16),
                   jax.ShapeDtypeStruct((8, D_SSD), F32)],
        compiler_params=_cp(("arbitrary",)))(dys, y, proj, wn)


def _pool_counts(t0, rows, w):
    pos = (t0 + 1 + lax.broadcasted_iota(jnp.int32, (rows, 1), 0)).astype(F32)
    return jnp.minimum(pos, float(w))


def _window_means(ext, t0):
    n = ext.shape[0]
    outs = []
    run = ext
    width = 1
    sums = {}
    while width < 16:
        run = run + pltpu.roll(run, width, 0)
        width *= 2
        sums[width] = run
    for g, w in enumerate(POOL_WINDOWS):
        sl = slice(g * POOL_GW, (g + 1) * POOL_GW)
        cnt = _pool_counts(t0, n - HALO, w)
        outs.append(sums[w][HALO:, sl] / cnt - ext[HALO:, sl])
    return outs


def _pool_fwd(proj, pw, pb, ps, tm, name):
    seq = proj.shape[0]

    def body(u_ref, p_ref, pw_ref, pb_ref, ps_ref, o_ref):
        i = pl.program_id(0)
        prev = jnp.where(i == 0, 0.0, p_ref[...])
        ext = jnp.concatenate([prev, u_ref[...]], axis=0)
        diffs = _window_means(ext, i * tm)
        for g in range(4):
            sl = slice(g * POOL_GW, (g + 1) * POOL_GW)
            out = _dot(diffs[g].astype(BF16), pw_ref[g], 1, 0) + pb_ref[:, sl]
            o_ref[:, sl] = (out * ps_ref[:, sl]).astype(BF16)

    c0 = COL_U // 1024
    vec = pl.BlockSpec((1, D_POOL), lambda i: (0, 0))
    return pl.pallas_call(
        body, name=name, grid=(seq // tm,),
        in_specs=[pl.BlockSpec((tm, 1024), lambda i: (i, c0)),
                  pl.BlockSpec((HALO, 1024), lambda i: (jnp.maximum(i * (tm // HALO) - 1, 0), c0)),
                  pl.BlockSpec((4, POOL_GW, POOL_GW), lambda i: (0, 0, 0)), vec, vec],
        out_specs=pl.BlockSpec((tm, D_POOL), lambda i: (i, 0)),
        out_shape=jax.ShapeDtypeStruct((seq, D_POOL), BF16),
        compiler_params=_cp(("parallel",)))(proj, proj, pw, pb, ps)


def _pool_bwd(dyp, proj, pw, pb, ps, tm, name):
    seq = proj.shape[0]
    ni = seq // tm

    def body(d_ref, dn_ref, u_ref, p_ref, pw_ref, pb_ref, ps_ref, du_ref, dw_ref, st_ref):
        i = pl.program_id(0)
        prev = jnp.where(i == 0, 0.0, p_ref[...])
        ext = jnp.concatenate([prev, u_ref[...]], axis=0)
        diffs = _window_means(ext, i * tm)
        dnext = jnp.where(i == ni - 1, 0.0, dn_ref[...])
        dext = jnp.concatenate([d_ref[...], dnext], axis=0)
        n = tm + HALO
        b_rows, s_rows = [], []
        for g, w in enumerate(POOL_WINDOWS):
            sl = slice(g * POOL_GW, (g + 1) * POOL_GW)
            wg = pw_ref[g]
            dout = dext[:, sl] * ps_ref[:, sl]
            dcur = dout[:tm]
            pre = _dot(diffs[g].astype(BF16), wg, 1, 0) + pb_ref[:, sl]
            s_rows.append(jnp.sum(d_ref[:, sl] * pre, axis=0, keepdims=True))
            b_rows.append(jnp.sum(dcur, axis=0, keepdims=True))
            dwg = _dot(diffs[g].astype(BF16), dcur.astype(BF16), 0, 0)

            @pl.when(i == 0)
            def _():
                dw_ref[g] = dwg

            @pl.when(i != 0)
            def _():
                dw_ref[g] += dwg

            ddiff = _dot(dout.astype(BF16), wg, 1, 1)
            scaled = ddiff / _pool_counts(i * tm, n, w)
            run = scaled
            width = 1
            while width < w:
                run = run + pltpu.roll(run, n - width, 0)
                width *= 2
            du_ref[:, sl] = (run[:tm] - ddiff[:tm]).astype(BF16)
        rows = jnp.concatenate([jnp.concatenate(b_rows, axis=1), jnp.concatenate(s_rows, axis=1),
                                jnp.zeros((6, D_POOL), F32)], axis=0)

        @pl.when(i == 0)
        def _():
            st_ref[...] = rows

        @pl.when(i != 0)
        def _():
            st_ref[...] += rows

    c0 = COL_U // 1024
    vec = pl.BlockSpec((1, D_POOL), lambda i: (0, 0))
    last = seq // HALO - 1
    return pl.pallas_call(
        body, name=name, grid=(ni,),
        in_specs=[pl.BlockSpec((tm, D_POOL), lambda i: (i, 0)),
                  pl.BlockSpec((HALO, D_POOL), lambda i: (jnp.minimum((i + 1) * (tm // HALO), last), 0)),
                  pl.BlockSpec((tm, 1024), lambda i: (i, c0)),
                  pl.BlockSpec((HALO, 1024), lambda i: (jnp.maximum(i * (tm // HALO) - 1, 0), c0)),
                  pl.BlockSpec((4, POOL_GW, POOL_GW), lambda i: (0, 0, 0)), vec, vec],
        out_specs=[pl.BlockSpec((tm, D_POOL), lambda i: (i, 0)),
                   pl.BlockSpec((4, POOL_GW, POOL_GW), lambda i: (0, 0, 0)),
                   pl.BlockSpec((8, D_POOL), lambda i: (0, 0))],
        out_shape=[jax.ShapeDtypeStruct((seq, D_POOL), BF16),
                   jax.ShapeDtypeStruct((4, POOL_GW, POOL_GW), F32),
                   jax.ShapeDtypeStruct((8, D_POOL), F32)],
        compiler_params=_cp(("arbitrary",)))(dyp, dyp, proj, proj, pw, pb, ps)


def _mix_out(ys, yp, wout, x1, mod, tm, name):
    seq = ys.shape[0]

    def body(ys_ref, yp_ref, w_ref, x_ref, mod_ref, xo_ref, m_ref):
        mix = _dot(ys_ref[...], w_ref[0:D_SSD, :], 1, 0) + _dot(yp_ref[...], w_ref[D_SSD:2 * D_SSD, :], 1, 0)
        m_ref[...] = mix.astype(BF16)
        xo_ref[...] = x_ref[...] + mod_ref[5:6, :] * mix

    tok = pl.BlockSpec((tm, D), lambda i: (i, 0))
    return pl.pallas_call(
        body, name=name, grid=(seq // tm,),
        in_specs=[tok, tok, pl.BlockSpec((2 * D_SSD, D), lambda i: (0, 0)), tok,
                  pl.BlockSpec((9, D), lambda i: (0, 0))],
        out_specs=[tok, tok],
        out_shape=[jax.ShapeDtypeStruct((seq, D), F32), jax.ShapeDtypeStruct((seq, D), BF16)],
        compiler_params=_cp(("parallel",)))(ys, yp, wout, x1, mod)


def _mix_bwd_dh(dz, dxbc, du, ddt, win_t, x1, dx2, mixb, wn, mod, tm, name):
    seq = x1.shape[0]

    def body(dz_ref, dx_ref, du_ref, ddt_ref, w_ref, x_ref, dxo_ref, m_ref, wn_ref, mod_ref, o_ref, st_ref):
        dh = (_dot(dz_ref[...], w_ref[COL_Z:COL_Z + 1024, :], 1, 0)
              + _dot(dx_ref[...], w_ref[COL_XBC:COL_XBC + D_XBC, :], 1, 0)
              + _dot(du_ref[...], w_ref[COL_U:COL_U + 1024, :], 1, 0)
              + _dot(ddt_ref[...], w_ref[COL_DT:COL_DT + LANE, :], 1, 0))
        o_ref[...] = _norm_bwd(dh, x_ref[...], dxo_ref[...], m_ref[...].astype(F32), wn_ref[...],
                               mod_ref[4:5, :], 1.0, st_ref, pl.program_id(0) == 0)

    tok = pl.BlockSpec((tm, D), lambda i: (i, 0))
    return pl.pallas_call(
        body, name=name, grid=(seq // tm,),
        in_specs=[tok, pl.BlockSpec((tm, D_XBC), lambda i: (i, 0)), tok,
                  pl.BlockSpec((tm, LANE), lambda i: (i, 0)),
                  pl.BlockSpec((D_IN_PAD, D), lambda i: (0, 0)),
                  tok, tok, tok,
                  pl.BlockSpec((1, D), lambda i: (0, 0)),
                  pl.BlockSpec((9, D), lambda i: (0, 0))],
        out_specs=[tok, pl.BlockSpec((8, D), lambda i: (0, 0))],
        out_shape=[jax.ShapeDtypeStruct((seq, D), F32), jax.ShapeDtypeStruct((8, D), F32)],
        compiler_params=_cp(("arbitrary",)))(dz, dxbc, du, ddt, win_t, x1, dx2, mixb, wn, mod)


def _mix_bwd_dycat(dmix, wout, tm, name):
    seq = dmix.shape[0]

    def body(d_ref, w_ref, a_ref, b_ref):
        dv = d_ref[...]
        a_ref[...] = _dot(dv, w_ref[0:D_SSD, :], 1, 1)
        b_ref[...] = _dot(dv, w_ref[D_SSD:2 * D_SSD, :], 1, 1)

    tok = pl.BlockSpec((tm, D), lambda i: (i, 0))
    return pl.pallas_call(
        body, name=name, grid=(seq // tm,),
        in_specs=[tok, pl.BlockSpec((2 * D_SSD, D), lambda i: (0, 0))],
        out_specs=[tok, tok],
        out_shape=[jax.ShapeDtypeStruct((seq, D), F32)] * 2,
        compiler_params=_cp(("parallel",)))(dmix, wout)


def _local_step(x, tgt, mod, wff, win_t, wout, pool_w, vecs, tm):
    seq = x.shape[0]
    tk = min(seq, 512)
    x1, s1 = _ffn_forward(x, wff, 0, vecs["ffn1_norm"], mod, 0, tm, "ffn1")
    h2 = _prenorm(x1, vecs["mix_norm"], mod, 1, tm, "mix_prenorm")
    proj = _mm_nt(h2, win_t, tm, D_IN_PAD // 3, F32, "mix_in_proj")
    xbc = _conv_fwd(proj, vecs["conv_w"], vecs["conv_b"], tm, "mix_conv")
    y, hprev = _ssd_fwd(xbc, proj, vecs["ssd_par"], "mix_ssd")
    ys = _gate_norm_fwd(y, proj, vecs["ssd_norm_w"], tm, "mix_gate_norm")
    yp = _pool_fwd(proj, pool_w, vecs["pool_b"], vecs["pool_scale"], tm, "mix_pool")
    x2, mixb = _mix_out(ys, yp, wout, x1, mod, tm, "mix_out_proj")
    x3, s3 = _ffn_forward(x2, wff, 3, vecs["ffn2_norm"], mod, 2, tm, "ffn2")
    dx3, st_loss = _loss_head(x3, vecs["final_norm"], tgt, tm, "loss_head")

    dx2, st3, dw3 = _ffn_backward(dx3, s3, wff, 3, vecs["ffn2_norm"], mod, 2, tm, "ffn2")
    dmix = _scale_cast(dx2, mod, 5, 1.0, tm, "mix_dbranch")
    dys, dyp = _mix_bwd_dycat(dmix, wout, tm, "mix_bwd_dycat")
    d_wout = (_mm_tn(ys, dmix, D_SSD, tk, "mix_dw_out_ssd"), _mm_tn(yp, dmix, D_POOL, tk, "mix_dw_out_pool"))
    du, d_pool_w, st_pool = _pool_bwd(dyp, proj, pool_w, vecs["pool_b"], vecs["pool_scale"], tm, "mix_pool_bwd")
    dy, dz, st_gn = _gate_norm_bwd(dys, y, proj, vecs["ssd_norm_w"], tm, "mix_gate_norm_bwd")
    dxbc_act, ddt, st_ssd = _ssd_bwd(dy, xbc, proj, vecs["ssd_par"], hprev, "mix_ssd_bwd")
    dxbc, st_conv = _conv_bwd(dxbc_act, proj, vecs["conv_w"], vecs["conv_b"], tm, "mix_conv_bwd")
    dx1, st2 = _mix_bwd_dh(dz, dxbc, du, ddt, win_t, x1, dx2, mixb, vecs["mix_norm"], mod, min(tm, 256), "mix_bwd_dh")
    d_win = (_mm_tn(dz, h2, 1024, tk, "mix_dw_in_z"), _mm_tn(dxbc, h2, 1024, tk, "mix_dw_in_xbc"),
             _mm_tn(du, h2, 1024, tk, "mix_dw_in_u"), _mm_tn(ddt, h2, LANE, tk, "mix_dw_in_dt"))
    dx0, st1, dw1 = _ffn_backward(dx1, s1, wff, 0, vecs["ffn1_norm"], mod, 0, tm, "ffn1")
    stats = dict(ffn1=st1, mix=st2, ffn2=st3, loss=st_loss, pool=st_pool, gn=st_gn, ssd=st_ssd, conv=st_conv)
    return dx0, stats, dw1, dw3, d_win, d_wout, d_pool_w


HBM_SPEC = pl.BlockSpec(memory_space=pltpu.HBM)


def _mesh_pos():
    return lax.axis_index("x"), lax.axis_index("y"), lax.axis_index("c")


def _other_chips(x, y):
    return [(1 - x, y), (x, 1 - y), (1 - x, 1 - y)]


def _all_gather(src, regions, name):
    total, cols = src.shape
    assert sum(r for _, r in regions) == total

    def body(src_ref, out_ref, send_sems, recv_sems, local_sem):
        x, y, c = _mesh_pos()
        me, sibling = (x, y, c), (x, y, 1 - c)
        chips = _other_chips(x, y)

        def rows_of(dev, off, rows):
            start = pl.multiple_of(N_DEV * off + (4 * dev[0] + 2 * dev[1] + dev[2]) * rows, 8)
            return out_ref.at[pl.ds(start, rows), :]

        def copies(k, block, to, from_src):
            out = []
            for off, rows in regions:
                dst = rows_of(block, off, rows)
                out.append(pltpu.make_async_remote_copy(
                    src_ref=src_ref.at[pl.ds(off, rows), :] if from_src else dst, dst_ref=dst,
                    send_sem=send_sems.at[k], recv_sem=recv_sems.at[k], device_id=to, device_id_type=MESH))
            return out

        def drain(k):
            whole = out_ref.at[pl.ds(0, total), :]
            return pltpu.make_async_remote_copy(src_ref=whole, dst_ref=whole, send_sem=send_sems.at[k],
                                                recv_sem=recv_sems.at[k], device_id=me, device_id_type=MESH)

        for off, rows in regions:
            pltpu.make_async_copy(src_ref.at[pl.ds(off, rows), :], rows_of(me, off, rows), local_sem).start()
        first = copies(0, me, sibling, True)
        for j, chip in enumerate(chips):
            first += copies(1 + j, me, (*chip, c), True)
        for cp in first:
            cp.start()
        for j, chip in enumerate(chips):
            drain(1 + j).wait_recv()
            for cp in copies(4 + j, (*chip, c), sibling, False):
                cp.start()
        drain(0).wait_recv()
        for j in range(3):
            drain(4 + j).wait_recv()
        for k in range(7):
            drain(k).wait_send()
        pltpu.make_async_copy(src_ref, out_ref.at[pl.ds(0, total), :], local_sem).wait()

    return pl.pallas_call(
        body, name=name,
        out_shape=jax.ShapeDtypeStruct((N_DEV * total, cols), src.dtype),
        in_specs=[HBM_SPEC], out_specs=HBM_SPEC,
        scratch_shapes=[pltpu.SemaphoreType.DMA((7,)), pltpu.SemaphoreType.DMA((7,)), pltpu.SemaphoreType.DMA],
    )(src)


def _rs_pair(grads, total, name):
    cols = grads[0][0].shape[1]
    n = len(grads)

    def body(*refs):
        g_refs, recv_ref, own_ref, send_sem, recv_sem, local_sem = refs[:n], refs[n], refs[n + 1], refs[n + 2], refs[n + 3], refs[n + 4]
        x, y, c = _mesh_pos()
        sibling = (x, y, 1 - c)
        started = []
        for q in range(4):
            for g_ref, (_, off, rows) in zip(g_refs, grads):
                theirs = g_ref.at[pl.ds(pl.multiple_of((2 * q + 1 - c) * rows, 8), rows), :]
                mine = g_ref.at[pl.ds(pl.multiple_of((2 * q + c) * rows, 8), rows), :]
                cp = pltpu.make_async_remote_copy(
                    src_ref=theirs, dst_ref=recv_ref.at[q, pl.ds(off, rows), :], send_sem=send_sem, recv_sem=recv_sem,
                    device_id=sibling, device_id_type=MESH)
                cp.start()
                pltpu.make_async_copy(mine, own_ref.at[q, pl.ds(off, rows), :], local_sem).start()
        del started
        whole = pltpu.make_async_remote_copy(src_ref=recv_ref, dst_ref=recv_ref, send_sem=send_sem, recv_sem=recv_sem,
                                             device_id=sibling, device_id_type=MESH)
        whole.wait_send()
        whole.wait_recv()
        pltpu.make_async_copy(recv_ref, own_ref, local_sem).wait()

    shape = jax.ShapeDtypeStruct((4, total, cols), F32)
    return pl.pallas_call(
        body, name=name, out_shape=[shape, shape],
        in_specs=[HBM_SPEC] * n, out_specs=[HBM_SPEC, HBM_SPEC],
        scratch_shapes=[pltpu.SemaphoreType.DMA, pltpu.SemaphoreType.DMA, pltpu.SemaphoreType.DMA],
    )(*[g for g, _, _ in grads])


def _rs_chips(p, name):
    _, total, cols = p.shape

    def body(p_ref, out_ref, send_sems, recv_sems, local_sem):
        x, y, c = _mesh_pos()
        myq = 2 * x + y
        chips = _other_chips(x, y)
        local = pltpu.make_async_copy(p_ref.at[myq], out_ref.at[myq], local_sem)
        local.start()
        cps = []
        for j, chip in enumerate(chips):
            q = 2 * chip[0] + chip[1]
            cps.append(pltpu.make_async_remote_copy(
                src_ref=p_ref.at[q], dst_ref=out_ref.at[myq], send_sem=send_sems.at[j], recv_sem=recv_sems.at[j],
                device_id=(*chip, c), device_id_type=MESH))
        for cp in cps:
            cp.start()
        for j, chip in enumerate(chips):
            q = 2 * chip[0] + chip[1]
            pltpu.make_async_remote_copy(
                src_ref=p_ref.at[q], dst_ref=out_ref.at[q], send_sem=send_sems.at[j], recv_sem=recv_sems.at[j],
                device_id=(*chip, c), device_id_type=MESH).wait_recv()
        for cp in cps:
            cp.wait_send()
        local.wait()

    return pl.pallas_call(
        body, name=name, out_shape=jax.ShapeDtypeStruct((4, total, cols), F32),
        in_specs=[HBM_SPEC], out_specs=HBM_SPEC,
        scratch_shapes=[pltpu.SemaphoreType.DMA((3,)), pltpu.SemaphoreType.DMA((3,)), pltpu.SemaphoreType.DMA],
    )(p)


def _row_tile(rows, cap):
    t = min(rows, cap)
    while rows % t or t % 8:
        t -= 8
    return t


def _add2(a, b, name):
    rows, cols = a.shape
    tr = _row_tile(rows, 1024)

    def body(a_ref, b_ref, o_ref):
        o_ref[...] = a_ref[...] + b_ref[...]

    spec = pl.BlockSpec((tr, cols), lambda i: (i, 0))
    return pl.pallas_call(body, name=name, grid=(rows // tr,), in_specs=[spec, spec], out_specs=spec,
                          out_shape=jax.ShapeDtypeStruct((rows, cols), F32), compiler_params=_cp(("parallel",)))(a, b)


def _sum_slots(v, name):
    slots, rows, cols = v.shape
    tr = _row_tile(rows, 512)

    def body(v_ref, o_ref):
        acc = v_ref[0]
        for s in range(1, slots):
            acc = acc + v_ref[s]
        o_ref[...] = acc

    return pl.pallas_call(body, name=name, grid=(rows // tr,),
                          in_specs=[pl.BlockSpec((slots, tr, cols), lambda i: (0, i, 0))],
                          out_specs=pl.BlockSpec((tr, cols), lambda i: (i, 0)),
                          out_shape=jax.ShapeDtypeStruct((rows, cols), F32), compiler_params=_cp(("parallel",)))(v)


def _ada_mod(c_all, w, b, name):
    n = w.shape[1]

    def body(c_ref, w_ref, b_ref, o_ref):
        cv = c_ref[...]
        o_ref[...] = _exact_dot(cv * _sigmoid(cv), w_ref[...]) + b_ref[...]

    return pl.pallas_call(body, name=name, out_shape=jax.ShapeDtypeStruct((N_DEV, n), F32),
                          compiler_params=pltpu.CompilerParams(vmem_limit_bytes=VMEM_LIMIT))(c_all, w, b)


def _ada_grad(c_all, dmod, name):
    n = dmod.shape[1]

    def body(c_ref, d_ref, o_ref):
        cv = c_ref[...]
        o_ref[...] = _dot(cv * _sigmoid(cv), d_ref[...], 0, 0, lax.Precision.HIGHEST)

    return pl.pallas_call(body, name=name, out_shape=jax.ShapeDtypeStruct((D, n), F32),
                          compiler_params=pltpu.CompilerParams(vmem_limit_bytes=VMEM_LIMIT))(c_all, dmod)


def _adamw(w, g, m, v, name):
    rows, cols = w.shape
    tr = _row_tile(rows, 256) if rows % 8 == 0 else rows
    c1 = 1.0 - ADAM_B1 ** ADAM_STEP
    c2 = 1.0 - ADAM_B2 ** ADAM_STEP

    def body(w_ref, g_ref, m_ref, v_ref, d_ref, mo_ref, vo_ref):
        gv = g_ref[...]
        mn = ADAM_B1 * m_ref[...] + (1.0 - ADAM_B1) * gv
        vn = ADAM_B2 * v_ref[...] + (1.0 - ADAM_B2) * (gv * gv)
        mo_ref[...] = mn
        vo_ref[...] = vn
        d_ref[...] = -ADAM_LR * ((mn / c1) / (jnp.sqrt(vn / c2) + ADAM_EPS) + ADAM_WD * w_ref[...])

    spec = pl.BlockSpec((tr, cols), lambda i: (i, 0))
    shape = jax.ShapeDtypeStruct((rows, cols), F32)
    return pl.pallas_call(body, name=name, grid=(rows // tr,), in_specs=[spec] * 4, out_specs=[spec] * 3,
                          out_shape=[shape] * 3, compiler_params=_cp(("parallel",)))(w, g, m, v)


def _sum8_loss(v, loss_row, name):
    rows = v.shape[0] // N_DEV

    def body(v_ref, o_ref, l_ref):
        acc = v_ref[0:rows, :]
        for k in range(1, N_DEV):
            acc = acc + v_ref[k * rows:(k + 1) * rows, :]
        o_ref[...] = acc
        part = jnp.sum(acc[loss_row:loss_row + 8, :], axis=0, keepdims=True)
        l_ref[...] = jnp.broadcast_to(jnp.sum(part, axis=1, keepdims=True), (8, LANE))

    return pl.pallas_call(body, name=name,
                          out_shape=[jax.ShapeDtypeStruct((rows, LANE), F32), jax.ShapeDtypeStruct((8, LANE), F32)],
                          compiler_params=pltpu.CompilerParams(vmem_limit_bytes=VMEM_LIMIT))(v)


WEIGHT_NAMES = ("w_ada", "b_ada", "ffn1_norm", "ffn1_w_gate", "ffn1_w_up", "ffn1_w_down", "mix_norm", "w_in",
                "conv_w", "conv_b", "dt_bias", "a_log", "d_skip", "ssd_norm_w", "pool_w", "pool_b", "pool_scale",
                "w_out", "ffn2_norm", "ffn2_w_gate", "ffn2_w_up", "ffn2_w_down", "final_norm")

FF_SHARD = FF // N_DEV
IN_SHARD = D_IN // N_DEV
IN_SHARD_PAD = 528
OUT_SHARD = 2 * D_SSD // N_DEV
ADA_SHARD = 9 * D // N_DEV
POOL_SHARD_ROWS = 4 * 32 * POOL_GW // D
PACK = dict(gate1=(0, FF_SHARD), up1=(352, FF_SHARD), down1=(704, FF_SHARD), gate2=(1056, FF_SHARD),
            up2=(1408, FF_SHARD), down2=(1760, FF_SHARD), w_out=(2112, OUT_SHARD), w_in=(2368, IN_SHARD_PAD),
            pool_w=(2896, POOL_SHARD_ROWS))
PACK_W_ROWS = 2896
PACK_G_ROWS = 2928

SMALL_ROWS = dict(dmod=(0, 72), ffn1_norm=(72, 8), mix_norm=(80, 8), ffn2_norm=(88, 8), final_norm=(96, 8),
                  ssd_norm_w=(104, 8), pool_scale=(112, 8), conv_b=(120, 16), conv_w=(136, 64), pool_b=(200, 8),
                  ssd=(208, 3), loss=(216, 8))
SMALL_TOTAL = 224


def _rows128(v, rows):
    flat = v.reshape(-1)
    return jnp.pad(flat, (0, rows * LANE - flat.shape[0])).reshape(rows, LANE)


def _pad_lanes(v):
    return jnp.pad(v.reshape(-1), (0, LANE - v.size))


def kernel(x, c, w_ada, b_ada, ffn1_norm, ffn1_w_gate, ffn1_w_up, ffn1_w_down, mix_norm, w_in, conv_w, conv_b, dt_bias, a_log, d_skip, ssd_norm_w, pool_w, pool_b, pool_scale, w_out, ffn2_norm, ffn2_w_gate, ffn2_w_up, ffn2_w_down, final_norm, loss_target, m_w_ada, m_b_ada, m_ffn1_norm, m_ffn1_w_gate, m_ffn1_w_up, m_ffn1_w_down, m_mix_norm, m_w_in, m_conv_w, m_conv_b, m_dt_bias, m_a_log, m_d_skip, m_ssd_norm_w, m_pool_w, m_pool_b, m_pool_scale, m_w_out, m_ffn2_norm, m_ffn2_w_gate, m_ffn2_w_up, m_ffn2_w_down, m_final_norm, v_w_ada, v_b_ada, v_ffn1_norm, v_ffn1_w_gate, v_ffn1_w_up, v_ffn1_w_down, v_mix_norm, v_w_in, v_conv_w, v_conv_b, v_dt_bias, v_a_log, v_d_skip, v_ssd_norm_w, v_pool_w, v_pool_b, v_pool_scale, v_w_out, v_ffn2_norm, v_ffn2_w_gate, v_ffn2_w_up, v_ffn2_w_down, v_final_norm):
    given = dict(locals())
    w = {n: given[n] for n in WEIGHT_NAMES}
    m = {n: given["m_" + n] for n in WEIGHT_NAMES}
    v = {n: given["v_" + n] for n in WEIGHT_NAMES}
    mx, my, mc = _mesh_pos()
    me = 4 * mx + 2 * my + mc

    small = jnp.concatenate([c.reshape(-1), conv_w.reshape(-1), pool_b.reshape(-1), pool_w.reshape(-1)])
    small_rows = 280
    gs = _all_gather(_rows128(small, small_rows), [(0, small_rows)], "ag_small").reshape(N_DEV, small_rows * LANE)
    c_all = gs[:, 0:D]
    conv_w_full = gs[:, 1024:2048].reshape(N_DEV, 4, 256).transpose(1, 0, 2).reshape(4, D_XBC)
    pool_b_full = gs[:, 2048:2176].reshape(N_DEV, 4, 32).transpose(1, 0, 2).reshape(1, D_POOL)
    pool_w_full = gs[:, 2176:2176 + 32768].reshape(N_DEV, 4, 32, POOL_GW).transpose(1, 0, 2, 3).reshape(4, POOL_GW, POOL_GW).astype(BF16)

    b_ada_cols = lax.dynamic_slice(b_ada, (0, me * ADA_SHARD), (1, ADA_SHARD))
    mod_part = _ada_mod(c_all, w_ada[0], b_ada_cols, "ada_mod")
    mod_all = _all_gather(mod_part, [(0, N_DEV)], "ag_mod").reshape(N_DEV, N_DEV, ADA_SHARD)
    mod = lax.dynamic_index_in_dim(mod_all, me, axis=1, keepdims=False).reshape(9, D)

    win_t_shard = jnp.pad(w_in[0].T, ((0, IN_SHARD_PAD - IN_SHARD), (0, 0)))
    wpack = jnp.concatenate([ffn1_w_gate[0].T, ffn1_w_up[0].T, ffn1_w_down[0], ffn2_w_gate[0].T, ffn2_w_up[0].T,
                             ffn2_w_down[0], w_out[0], win_t_shard], axis=0).astype(BF16)
    regions = [PACK[k] for k in ("gate1", "up1", "down1", "gate2", "up2", "down2", "w_out", "w_in")]
    full = _all_gather(wpack, regions, "ag_weights")
    o_out, o_in = N_DEV * PACK["w_out"][0], N_DEV * PACK["w_in"][0]
    wout_full = full[o_out:o_out + 2 * D_SSD]
    win_g = full[o_in:o_in + N_DEV * IN_SHARD_PAD].reshape(N_DEV, IN_SHARD_PAD, D)[:, :IN_SHARD].reshape(D_IN, D)
    win_t = jnp.concatenate([win_g[0:1024], win_g[1024:3072], win_g[3088:4112], win_g[3072:3088],
                             jnp.zeros((D_IN_PAD - D_IN, D), BF16)], axis=0)

    vecs = dict(ffn1_norm=ffn1_norm, mix_norm=mix_norm, ffn2_norm=ffn2_norm, final_norm=final_norm.reshape(1, D),
                conv_w=conv_w_full, conv_b=conv_b, ssd_norm_w=ssd_norm_w, pool_b=pool_b_full, pool_scale=pool_scale,
                ssd_par=jnp.concatenate([_pad_lanes(dt_bias)[None], _pad_lanes(a_log)[None], _pad_lanes(d_skip)[None],
                                         jnp.zeros((5, LANE), F32)], axis=0))
    dx0, st, dw1, dw3, d_win, d_wout, d_pool_w = _local_step(
        x[0], loss_target[0], mod, full, win_t, wout_full, pool_w_full, vecs, min(512, x.shape[1]))

    dwin = jnp.concatenate([d_win[0], d_win[1], d_win[3][0:16], d_win[2]], axis=0)
    dwin = jnp.pad(dwin.reshape(N_DEV, IN_SHARD, D), ((0, 0), (0, IN_SHARD_PAD - IN_SHARD), (0, 0))).reshape(N_DEV * IN_SHARD_PAD, D)
    dwout = jnp.concatenate(d_wout, axis=0)
    dpool = d_pool_w.reshape(4, N_DEV, 32, POOL_GW).transpose(1, 0, 2, 3).reshape(N_DEV * POOL_SHARD_ROWS, D)
    grads = [(g, *PACK[k]) for g, k in zip((*dw1, *dw3, dwout, dwin, dpool),
                                            ("gate1", "up1", "down1", "gate2", "up2", "down2", "w_out", "w_in", "pool_w"))]
    from_sibling, own = _rs_pair(grads, PACK_G_ROWS, "rs_pair")
    pair = _add2(own.reshape(4 * PACK_G_ROWS, D), from_sibling.reshape(4 * PACK_G_ROWS, D), "rs_pair_sum")
    by_chip = _rs_chips(pair.reshape(4, PACK_G_ROWS, D), "rs_chips")
    gsh = _sum_slots(by_chip, "rs_chip_sum")

    def shard(k, rows=None):
        off, n = PACK[k]
        return gsh[off:off + (n if rows is None else rows)]

    dmod = jnp.concatenate([st["ffn1"][0:3], st["mix"][0:3], st["ffn2"][0:3]], axis=0)
    sg = jnp.concatenate([
        dmod.reshape(-1), st["ffn1"][3], st["mix"][3], st["ffn2"][3], st["loss"][0], st["gn"][0], st["pool"][1],
        st["conv"][4], st["conv"][0:4].reshape(-1), st["pool"][0], st["ssd"][0:3].reshape(-1),
        jnp.zeros((5 * LANE,), F32), st["loss"][1]])
    sg_all = _all_gather(sg.reshape(SMALL_TOTAL, LANE), [(0, SMALL_TOTAL)], "ag_small_grads")
    tot, loss_b = _sum8_loss(sg_all, SMALL_ROWS["loss"][0], "small_sum")
    loss = loss_b[0, 0]
    dmod_all = sg_all.reshape(N_DEV, SMALL_TOTAL * LANE)[:, 0:9 * D]
    g_w_ada = _ada_grad(c_all, lax.dynamic_slice(dmod_all, (0, me * ADA_SHARD), (N_DEV, ADA_SHARD)), "ada_grad")

    def tot_rows(k):
        off, n = SMALL_ROWS[k]
        return tot[off:off + n].reshape(-1)

    g_conv_w = lax.dynamic_slice(tot_rows("conv_w").reshape(4, D_XBC), (0, me * 256), (4, 256))
    g_pool_b = lax.dynamic_slice(tot_rows("pool_b").reshape(4, POOL_GW), (0, me * 32), (4, 32))
    g_ssd = tot_rows("ssd").reshape(3, LANE)
    grad = {
        "w_ada": g_w_ada[None], "b_ada": tot_rows("dmod").reshape(1, 9 * D),
        "ffn1_norm": tot_rows("ffn1_norm")[None], "mix_norm": tot_rows("mix_norm")[None],
        "ffn2_norm": tot_rows("ffn2_norm")[None], "final_norm": tot_rows("final_norm"),
        "ssd_norm_w": tot_rows("ssd_norm_w")[None], "pool_scale": tot_rows("pool_scale")[None],
        "conv_b": tot_rows("conv_b")[None], "conv_w": g_conv_w[None], "pool_b": g_pool_b[None],
        "dt_bias": g_ssd[0:1, 0:N_HEADS], "a_log": g_ssd[1:2, 0:N_HEADS], "d_skip": g_ssd[2:3, 0:N_HEADS],
        "ffn1_w_gate": shard("gate1").T[None], "ffn1_w_up": shard("up1").T[None], "ffn1_w_down": shard("down1")[None],
        "ffn2_w_gate": shard("gate2").T[None], "ffn2_w_up": shard("up2").T[None], "ffn2_w_down": shard("down2")[None],
        "w_out": shard("w_out")[None], "w_in": shard("w_in", IN_SHARD).T[None],
        "pool_w": shard("pool_w").reshape(1, 4, 32, POOL_GW),
    }

    big = ("w_ada", "ffn1_w_gate", "ffn1_w_up", "ffn1_w_down", "w_in", "pool_w", "w_out",
           "ffn2_w_gate", "ffn2_w_up", "ffn2_w_down")
    delta, new_m, new_v = {}, {}, {}
    for n in big:
        shp = w[n].shape
        two_d = (shp[-3] * shp[-2], shp[-1]) if n == "pool_w" else shp[-2:]
        d_, m_, v_ = _adamw(w[n].reshape(two_d), grad[n].reshape(two_d), m[n].reshape(two_d), v[n].reshape(two_d),
                            f"adamw_{n}")
        delta[n], new_m[n], new_v[n] = d_.reshape(shp), m_.reshape(shp), v_.reshape(shp)
    small_names = [n for n in WEIGHT_NAMES if n not in big]
    sizes = [LANE if w[n].size < LANE else w[n].size for n in small_names]
    small_rows_adam = -(-sum(sizes) // (8 * LANE)) * 8

    def pack_small(t):
        return _rows128(jnp.concatenate([_pad_lanes(t[n]) if t[n].size < LANE else t[n].reshape(-1) for n in small_names]),
                        small_rows_adam)

    d_s, m_s, v_s = _adamw(pack_small(w), pack_small(grad), pack_small(m), pack_small(v), "adamw_small")
    off = 0
    for n, size in zip(small_names, sizes):
        for res, packed in ((delta, d_s), (new_m, m_s), (new_v, v_s)):
            res[n] = packed.reshape(-1)[off:off + w[n].size].reshape(w[n].shape)
        off += size

    return (loss, dx0[None], *[grad[n] for n in WEIGHT_NAMES], *[delta[n] for n in WEIGHT_NAMES],
            *[new_m[n] for n in WEIGHT_NAMES], *[new_v[n] for n in WEIGHT_NAMES])
```

```python
import functools
import math

import jax
import jax.numpy as jnp
from jax import lax
from jax.experimental import pallas as pl
from jax.experimental.pallas import tpu as pltpu

F32 = jnp.float32
BF16 = jnp.bfloat16
MESH = pl.DeviceIdType.MESH

N_DEV = 8
D = 1024
FF = 2816
D_SSD = 1024
N_HEADS = 16
HEAD_DIM = 64
N_GROUPS = 4
N_STATE = 128
CHUNK = 128
GROUP_W = D_SSD // N_GROUPS
D_XBC = D_SSD + 2 * N_GROUPS * N_STATE
D_POOL = 1024
POOL_WINDOWS = (2, 4, 8, 16)
POOL_GW = 256
D_IN = 4112
D_IN_PAD = 4224
COL_Z, COL_XBC, COL_U, COL_DT = 0, 1024, 3072, 4096
EPS = 1e-6
FFN_RES = 0.5
LANE = 128
HALO = 16

ADAM_LR, ADAM_B1, ADAM_B2, ADAM_EPS, ADAM_WD, ADAM_STEP = 0.001, 0.9, 0.999, 1e-08, 0.01, 10

VMEM_LIMIT = 56 << 20


def _cp(sem):
    return pltpu.CompilerParams(dimension_semantics=sem, vmem_limit_bytes=VMEM_LIMIT)


def _dot(a, b, ca, cb, prec=None):
    return lax.dot_general(a, b, (((ca,), (cb,)), ((), ())), precision=prec,
                           preferred_element_type=F32)


def _exact_dot(a, b):
    return _dot(a, b, 1, 0, lax.Precision.HIGHEST)


def _sigmoid(v):
    return 1.0 / (1.0 + jnp.exp(-v))


def _silu_grad(v, sg):
    return sg * (1.0 + v * (1.0 - sg))


def _mm_nt(a, bt, tm, tn, out_dtype, name):
    m, k = a.shape
    n = bt.shape[0]

    def body(a_ref, b_ref, o_ref):
        o_ref[...] = _dot(a_ref[...], b_ref[...], 1, 1).astype(out_dtype)

    return pl.pallas_call(
        body, name=name, grid=(n // tn, m // tm),
        in_specs=[pl.BlockSpec((tm, k), lambda j, i: (i, 0)),
                  pl.BlockSpec((tn, k), lambda j, i: (j, 0))],
        out_specs=pl.BlockSpec((tm, tn), lambda j, i: (i, j)),
        out_shape=jax.ShapeDtypeStruct((m, n), out_dtype),
        compiler_params=_cp(("parallel", "parallel")))(a, bt)


def _mm_tn(a, b, tm, tk, name):
    kk, m = a.shape
    n = b.shape[1]
    nk = kk // tk

    def body(a_ref, b_ref, o_ref, acc):
        k = pl.program_id(1)

        @pl.when(k == 0)
        def _():
            acc[...] = jnp.zeros_like(acc)

        acc[...] += _dot(a_ref[...], b_ref[...], 0, 0)

        @pl.when(k == nk - 1)
        def _():
            o_ref[...] = acc[...]

    return pl.pallas_call(
        body, name=name, grid=(m // tm, nk),
        in_specs=[pl.BlockSpec((tk, tm), lambda i, k: (k, i)),
                  pl.BlockSpec((tk, n), lambda i, k: (k, 0))],
        out_specs=pl.BlockSpec((tm, n), lambda i, k: (i, 0)),
        out_shape=jax.ShapeDtypeStruct((m, n), F32),
        scratch_shapes=[pltpu.VMEM((tm, n), F32)],
        compiler_params=_cp(("parallel", "arbitrary")))(a, b)


def _prenorm(x, wn, mod, k, tm, name):
    seq = x.shape[0]

    def body(x_ref, wn_ref, mod_ref, h_ref):
        xv = x_ref[...]
        r = lax.rsqrt(jnp.mean(xv * xv, axis=-1, keepdims=True) + EPS)
        hn = xv * r * wn_ref[...]
        h_ref[...] = (hn * (1.0 + mod_ref[3 * k + 1:3 * k + 2, :]) + mod_ref[3 * k:3 * k + 1, :]).astype(BF16)

    return pl.pallas_call(
        body, name=name, grid=(seq // tm,),
        in_specs=[pl.BlockSpec((tm, D), lambda i: (i, 0)),
                  pl.BlockSpec((1, D), lambda i: (0, 0)),
                  pl.BlockSpec((9, D), lambda i: (0, 0))],
        out_specs=pl.BlockSpec((tm, D), lambda i: (i, 0)),
        out_shape=jax.ShapeDtypeStruct((seq, D), BF16),
        compiler_params=_cp(("parallel",)))(x, wn, mod)


def _scale_cast(dxo, mod, row, res, tm, name):
    seq = dxo.shape[0]

    def body(d_ref, mod_ref, o_ref):
        o_ref[...] = (d_ref[...] * (res * mod_ref[row:row + 1, :])).astype(BF16)

    return pl.pallas_call(
        body, name=name, grid=(seq // tm,),
        in_specs=[pl.BlockSpec((tm, D), lambda i: (i, 0)),
                  pl.BlockSpec((9, D), lambda i: (0, 0))],
        out_specs=pl.BlockSpec((tm, D), lambda i: (i, 0)),
        out_shape=jax.ShapeDtypeStruct((seq, D), BF16),
        compiler_params=_cp(("parallel",)))(dxo, mod)


def _norm_bwd(dh, xv, dxo, branch, wn, sc, res, stats_ref, first):
    r = lax.rsqrt(jnp.mean(xv * xv, axis=-1, keepdims=True) + EPS)
    xn = xv * r
    dhn = dh * (1.0 + sc)
    dxn = dhn * wn
    dx = dxo + r * (dxn - xn * jnp.mean(dxn * xn, axis=-1, keepdims=True))
    rows = jnp.concatenate([
        jnp.sum(dh, axis=0, keepdims=True),
        jnp.sum(dh * (xn * wn), axis=0, keepdims=True),
        jnp.sum(branch * dxo, axis=0, keepdims=True) * res,
        jnp.sum(dhn * xn, axis=0, keepdims=True),
        jnp.zeros((4, D), F32)], axis=0)

    @pl.when(first)
    def _():
        stats_ref[...] = rows

    @pl.when(jnp.logical_not(first))
    def _():
        stats_ref[...] += rows

    return dx


def _loss_head(x3, wf, tgt, tm, name):
    seq = x3.shape[0]

    def body(x_ref, w_ref, t_ref, dx_ref, st_ref):
        xv = x_ref[...]
        wv = w_ref[...]
        r = lax.rsqrt(jnp.mean(xv * xv, axis=-1, keepdims=True) + EPS)
        xn = xv * r
        e = xn * wv - t_ref[...]
        dy = e * (1.0 / D)
        dxn = dy * wv
        dx_ref[...] = r * (dxn - xn * jnp.mean(dxn * xn, axis=-1, keepdims=True))
        rows = jnp.concatenate([
            jnp.sum(dy * xn, axis=0, keepdims=True),
            jnp.sum(e * e, axis=0, keepdims=True) * (0.5 / D),
            jnp.zeros((6, D), F32)], axis=0)

        @pl.when(pl.program_id(0) == 0)
        def _():
            st_ref[...] = rows

        @pl.when(pl.program_id(0) != 0)
        def _():
            st_ref[...] += rows

    return pl.pallas_call(
        body, name=name, grid=(seq // tm,),
        in_specs=[pl.BlockSpec((tm, D), lambda i: (i, 0)),
                  pl.BlockSpec((1, D), lambda i: (0, 0)),
                  pl.BlockSpec((tm, D), lambda i: (i, 0))],
        out_specs=[pl.BlockSpec((tm, D), lambda i: (i, 0)),
                   pl.BlockSpec((8, D), lambda i: (0, 0))],
        out_shape=[jax.ShapeDtypeStruct((seq, D), F32), jax.ShapeDtypeStruct((8, D), F32)],
        compiler_params=_cp(("arbitrary",)))(x3, wf, tgt)


def _ffn_up(h, w, blk, tm, tn, name):
    seq = h.shape[0]
    nj = FF // tn

    def body(h_ref, wg_ref, wu_ref, g_ref, u_ref, a_ref):
        hv = h_ref[...]
        g = _dot(hv, wg_ref[...], 1, 1)
        u = _dot(hv, wu_ref[...], 1, 1)
        g_ref[...] = g.astype(BF16)
        u_ref[...] = u.astype(BF16)
        a_ref[...] = (g * _sigmoid(g) * u).astype(BF16)

    act = pl.BlockSpec((tm, tn), lambda j, i: (i, j))
    return pl.pallas_call(
        body, name=name, grid=(nj, seq // tm),
        in_specs=[pl.BlockSpec((tm, D), lambda j, i: (i, 0)),
                  pl.BlockSpec((tn, D), lambda j, i: (blk * nj + j, 0)),
                  pl.BlockSpec((tn, D), lambda j, i: ((blk + 1) * nj + j, 0))],
        out_specs=[act, act, act],
        out_shape=[jax.ShapeDtypeStruct((seq, FF), BF16)] * 3,
        compiler_params=_cp(("parallel", "parallel")))(h, w, w)


def _ffn_down(a, w, blk, x, mod, grow, tm, tk, name):
    seq = a.shape[0]
    nk = FF // tk

    def body(a_ref, w_ref, x_ref, mod_ref, xo_ref, f_ref, acc):
        k = pl.program_id(1)

        @pl.when(k == 0)
        def _():
            acc[...] = jnp.zeros_like(acc)

        acc[...] += _dot(a_ref[...], w_ref[...], 1, 0)

        @pl.when(k == nk - 1)
        def _():
            f = acc[...]
            f_ref[...] = f.astype(BF16)
            xo_ref[...] = x_ref[...] + (FFN_RES * mod_ref[grow:grow + 1, :]) * f

    return pl.pallas_call(
        body, name=name, grid=(seq // tm, nk),
        in_specs=[pl.BlockSpec((tm, tk), lambda i, k: (i, k)),
                  pl.BlockSpec((tk, D), lambda i, k: ((blk + 2) * nk + k, 0)),
                  pl.BlockSpec((tm, D), lambda i, k: (i, 0)),
                  pl.BlockSpec((9, D), lambda i, k: (0, 0))],
        out_specs=[pl.BlockSpec((tm, D), lambda i, k: (i, 0)),
                   pl.BlockSpec((tm, D), lambda i, k: (i, 0))],
        out_shape=[jax.ShapeDtypeStruct((seq, D), F32), jax.ShapeDtypeStruct((seq, D), BF16)],
        scratch_shapes=[pltpu.VMEM((tm, D), F32)],
        compiler_params=_cp(("parallel", "arbitrary")))(a, w, x, mod)


def _ffn_bwd_da(df, w, blk, g, u, tm, tn, name):
    seq = df.shape[0]
    nj = FF // tn

    def body(df_ref, w_ref, g_ref, u_ref, dg_ref, du_ref):
        da = _dot(df_ref[...], w_ref[...], 1, 1)
        gv = g_ref[...].astype(F32)
        uv = u_ref[...].astype(F32)
        sg = _sigmoid(gv)
        dg_ref[...] = (da * uv * _silu_grad(gv, sg)).astype(BF16)
        du_ref[...] = (da * (gv * sg)).astype(BF16)

    act = pl.BlockSpec((tm, tn), lambda j, i: (i, j))
    return pl.pallas_call(
        body, name=name, grid=(nj, seq // tm),
        in_specs=[pl.BlockSpec((tm, D), lambda j, i: (i, 0)),
                  pl.BlockSpec((tn, D), lambda j, i: ((blk + 2) * nj + j, 0)),
                  act, act],
        out_specs=[act, act],
        out_shape=[jax.ShapeDtypeStruct((seq, FF), BF16)] * 2,
        compiler_params=_cp(("parallel", "parallel")))(df, w, g, u)


def _ffn_bwd_dh(dg, du, w, blk, x, dxo, fb, wn, mod, k, tm, tk, name):
    seq = x.shape[0]
    nk = FF // tk

    def body(dg_ref, du_ref, wg_ref, wu_ref, x_ref, dxo_ref, f_ref, wn_ref, mod_ref, dx_ref, st_ref, acc):
        kk = pl.program_id(1)
        first = pl.program_id(0) == 0

        @pl.when(kk == 0)
        def _():
            acc[...] = jnp.zeros_like(acc)

        acc[...] += _dot(dg_ref[...], wg_ref[...], 1, 0) + _dot(du_ref[...], wu_ref[...], 1, 0)

        @pl.when(kk == nk - 1)
        def _():
            dx_ref[...] = _norm_bwd(acc[...], x_ref[...], dxo_ref[...], f_ref[...].astype(F32), wn_ref[...],
                                    mod_ref[3 * k + 1:3 * k + 2, :], FFN_RES, st_ref, first)

    tok = pl.BlockSpec((tm, D), lambda i, kk: (i, 0))
    return pl.pallas_call(
        body, name=name, grid=(seq // tm, nk),
        in_specs=[pl.BlockSpec((tm, tk), lambda i, kk: (i, kk)),
                  pl.BlockSpec((tm, tk), lambda i, kk: (i, kk)),
                  pl.BlockSpec((tk, D), lambda i, kk: (blk * nk + kk, 0)),
                  pl.BlockSpec((tk, D), lambda i, kk: ((blk + 1) * nk + kk, 0)),
                  tok, tok, tok,
                  pl.BlockSpec((1, D), lambda i, kk: (0, 0)),
                  pl.BlockSpec((9, D), lambda i, kk: (0, 0))],
        out_specs=[tok, pl.BlockSpec((8, D), lambda i, kk: (0, 0))],
        out_shape=[jax.ShapeDtypeStruct((seq, D), F32), jax.ShapeDtypeStruct((8, D), F32)],
        scratch_shapes=[pltpu.VMEM((tm, D), F32)],
        compiler_params=_cp(("arbitrary", "arbitrary")))(dg, du, w, w, x, dxo, fb, wn, mod)


def _ffn_forward(x, w, blk, wn, mod, k, tm, tag):
    h = _prenorm(x, wn, mod, k, tm, f"{tag}_prenorm")
    g, u, a = _ffn_up(h, w, blk, tm, FF // 2, f"{tag}_up")
    xo, fb = _ffn_down(a, w, blk, x, mod, 3 * k + 2, tm, FF // 2, f"{tag}_down")
    return xo, (x, h, g, u, a, fb)


def _ffn_backward(dxo, saved, w, blk, wn, mod, k, tm, tag):
    x, h, g, u, a, fb = saved
    df = _scale_cast(dxo, mod, 3 * k + 2, FFN_RES, tm, f"{tag}_dbranch")
    dg, du = _ffn_bwd_da(df, w, blk, g, u, tm, FF // 2, f"{tag}_bwd_da")
    dx, stats = _ffn_bwd_dh(dg, du, w, blk, x, dxo, fb, wn, mod, k, tm, FF // 2, f"{tag}_bwd_dh")
    seq = x.shape[0]
    tk = min(seq, 512)
    d_gate_t = _mm_tn(dg, h, FF // 2, tk, f"{tag}_dw_gate")
    d_up_t = _mm_tn(du, h, FF // 2, tk, f"{tag}_dw_up")
    d_down = _mm_tn(a, df, FF // 2, tk, f"{tag}_dw_down")
    return dx, stats, (d_gate_t, d_up_t, d_down)


def _prev_rows(tm, col):
    return pl.BlockSpec((HALO, 1024), lambda i, j: (jnp.maximum(i * (tm // HALO) - 1, 0), col + j))


def _conv_pre(ext, cw, cb, rows):
    pre = cb + cw[3:4, :] * ext
    for s in (1, 2, 3):
        pre = pre + cw[3 - s:4 - s, :] * pltpu.roll(ext, s, 0)
    return pre[HALO:HALO + rows]


def _conv_fwd(proj, cw, cb, tm, name):
    seq = proj.shape[0]

    def body(x_ref, p_ref, cw_ref, cb_ref, o_ref):
        prev = jnp.where(pl.program_id(0) == 0, 0.0, p_ref[...])
        ext = jnp.concatenate([prev, x_ref[...]], axis=0)
        pre = _conv_pre(ext, cw_ref[...], cb_ref[...], tm)
        o_ref[...] = pre * _sigmoid(pre)

    c0 = COL_XBC // 1024
    return pl.pallas_call(
        body, name=name, grid=(seq // tm, 2),
        in_specs=[pl.BlockSpec((tm, 1024), lambda i, j: (i, c0 + j)),
                  _prev_rows(tm, c0),
                  pl.BlockSpec((4, 1024), lambda i, j: (0, j)),
                  pl.BlockSpec((1, 1024), lambda i, j: (0, j))],
        out_specs=pl.BlockSpec((tm, 1024), lambda i, j: (i, j)),
        out_shape=jax.ShapeDtypeStruct((seq, D_XBC), F32),
        compiler_params=_cp(("parallel", "parallel")))(proj, proj, cw, cb)


def _conv_bwd(dact, proj, cw, cb, tm, name):
    seq = proj.shape[0]
    ni = seq // tm

    def body(d_ref, dn_ref, x_ref, p_ref, n_ref, cw_ref, cb_ref, o_ref, st_ref):
        i = pl.program_id(1)
        cwv = cw_ref[...]
        prev = jnp.where(i == 0, 0.0, p_ref[...])
        ext = jnp.concatenate([prev, x_ref[...], n_ref[...]], axis=0)
        pre = _conv_pre(ext, cwv, cb_ref[...], tm + HALO)
        dnext = jnp.where(i == ni - 1, 0.0, dn_ref[...])
        dext = jnp.concatenate([d_ref[...], dnext], axis=0)
        dpre = dext * _silu_grad(pre, _sigmoid(pre))
        n = tm + HALO
        dx = cwv[3:4, :] * dpre
        for s in (1, 2, 3):
            dx = dx + cwv[3 - s:4 - s, :] * pltpu.roll(dpre, n - s, 0)
        o_ref[...] = dx[:tm].astype(BF16)
        dcur = dpre[:tm]
        rows = [jnp.sum(dcur * pltpu.roll(ext, 3 - k, 0)[HALO:HALO + tm], axis=0, keepdims=True) for k in range(3)]
        rows.append(jnp.sum(dcur * ext[HALO:HALO + tm], axis=0, keepdims=True))
        rows.append(jnp.sum(dcur, axis=0, keepdims=True))
        rows.append(jnp.zeros((3, 1024), F32))
        rows = jnp.concatenate(rows, axis=0)

        @pl.when(i == 0)
        def _():
            st_ref[...] = rows

        @pl.when(i != 0)
        def _():
            st_ref[...] += rows

    c0 = COL_XBC // 1024
    return pl.pallas_call(
        body, name=name, grid=(2, ni),
        in_specs=[pl.BlockSpec((tm, 1024), lambda j, i: (i, j)),
                  pl.BlockSpec((HALO, 1024), lambda j, i: (jnp.minimum((i + 1) * (tm // HALO), seq // HALO - 1), j)),
                  pl.BlockSpec((tm, 1024), lambda j, i: (i, c0 + j)),
                  pl.BlockSpec((HALO, 1024), lambda j, i: (jnp.maximum(i * (tm // HALO) - 1, 0), c0 + j)),
                  pl.BlockSpec((HALO, 1024), lambda j, i: (jnp.minimum((i + 1) * (tm // HALO), seq // HALO - 1), c0 + j)),
                  pl.BlockSpec((4, 1024), lambda j, i: (0, j)),
                  pl.BlockSpec((1, 1024), lambda j, i: (0, j))],
        out_specs=[pl.BlockSpec((tm, 1024), lambda j, i: (i, j)),
                   pl.BlockSpec((8, 1024), lambda j, i: (0, j))],
        out_shape=[jax.ShapeDtypeStruct((seq, D_XBC), BF16), jax.ShapeDtypeStruct((8, D_XBC), F32)],
        compiler_params=_cp(("parallel", "arbitrary")))(dact, dact, proj, proj, proj, cw, cb)


def _head_expand():
    r = lax.broadcasted_iota(jnp.int32, (LANE, D_SSD), 0)
    c = lax.broadcasted_iota(jnp.int32, (LANE, D_SSD), 1)
    return (c // HEAD_DIM == r).astype(F32)


def _head_reduce():
    r = lax.broadcasted_iota(jnp.int32, (D_SSD, LANE), 0)
    c = lax.broadcasted_iota(jnp.int32, (D_SSD, LANE), 1)
    return (r // HEAD_DIM == c).astype(F32)


def _ssd_common(dtr, par):
    q = CHUNK
    v = dtr + par[0:1, :]
    dt = jnp.maximum(v, 0.0) + jnp.log(1.0 + jnp.exp(-jnp.abs(v)))
    a = -jnp.exp(par[1:2, :])
    adt = dt * a
    li = lax.broadcasted_iota(jnp.int32, (q, q), 0)
    si = lax.broadcasted_iota(jnp.int32, (q, q), 1)
    causal = li >= si
    acs = _exact_dot(causal.astype(F32), adt)
    expand = _head_expand()
    dt_l = _exact_dot(dt, expand)
    acs_l = _exact_dot(acs, expand)
    par_l = _exact_dot(par, expand)
    last_l = acs_l[q - 1:q, :]
    return dict(v=v, dt=dt, a=a, acs=acs, acs_t=acs.T, causal=causal, dt_l=dt_l, acs_l=acs_l,
                ea_l=jnp.exp(acs_l), ds_l=jnp.exp(last_l - acs_l), cd_l=jnp.exp(last_l), dskip_l=par_l[2:3, :])


def _decay(cm, h):
    seg = cm["acs"][:, h:h + 1] - cm["acs_t"][h:h + 1, :]
    return jnp.exp(jnp.where(cm["causal"], seg, -jnp.inf))


def _lane_mask(r):
    lane = lax.broadcasted_iota(jnp.int32, (1, GROUP_W), 1)
    return lane // HEAD_DIM == r


def _ssd_fwd(xbc, proj, par, name):
    seq = xbc.shape[0]
    nc = seq // CHUNK
    q = CHUNK

    def body(x_ref, dt_ref, par_ref, y_ref, hp_ref, state):
        @pl.when(pl.program_id(0) == 0)
        def _():
            state[...] = jnp.zeros_like(state)

        cm = _ssd_common(dt_ref[...], par_ref[...])
        for g in range(N_GROUPS):
            lo = g * GROUP_W
            xs = x_ref[:, lo:lo + GROUP_W]
            bm = x_ref[:, D_SSD + g * N_STATE:D_SSD + (g + 1) * N_STATE].astype(BF16)
            cmat = x_ref[:, D_SSD + N_GROUPS * N_STATE + g * N_STATE:D_SSD + N_GROUPS * N_STATE + (g + 1) * N_STATE].astype(BF16)
            xdt = xs * cm["dt_l"][:, lo:lo + GROUP_W]
            xdt_b = xdt.astype(BF16)
            cb = _dot(cmat, bm, 1, 1)
            yd = jnp.zeros((q, GROUP_W), F32)
            for r in range(4):
                s_h = (cb * _decay(cm, 4 * g + r)).astype(BF16)
                yd = jnp.where(_lane_mask(r), _dot(s_h, xdt_b, 1, 0), yd)
            hg = state[g]
            hp_ref[0, g] = hg
            yo = _dot(cmat, hg.astype(BF16), 1, 0) * cm["ea_l"][:, lo:lo + GROUP_W]
            y_ref[:, lo:lo + GROUP_W] = yd + yo + cm["dskip_l"][:, lo:lo + GROUP_W] * xs
            xds = (xdt * cm["ds_l"][:, lo:lo + GROUP_W]).astype(BF16)
            state[g] = hg * cm["cd_l"][:, lo:lo + GROUP_W] + _dot(bm, xds, 0, 0)

    return pl.pallas_call(
        body, name=name, grid=(nc,),
        in_specs=[pl.BlockSpec((q, D_XBC), lambda c: (c, 0)),
                  pl.BlockSpec((q, LANE), lambda c: (c, COL_DT // LANE)),
                  pl.BlockSpec((8, LANE), lambda c: (0, 0))],
        out_specs=[pl.BlockSpec((q, D_SSD), lambda c: (c, 0)),
                   pl.BlockSpec((1, N_GROUPS, N_STATE, GROUP_W), lambda c: (c, 0, 0, 0))],
        out_shape=[jax.ShapeDtypeStruct((seq, D_SSD), F32),
                   jax.ShapeDtypeStruct((nc, N_GROUPS, N_STATE, GROUP_W), F32)],
        scratch_shapes=[pltpu.VMEM((N_GROUPS, N_STATE, GROUP_W), F32)],
        compiler_params=_cp(("arbitrary",)))(xbc, proj, par)


def _ssd_bwd(dy, xbc, proj, par, hprev, name):
    seq = xbc.shape[0]
    nc = seq // CHUNK
    q = CHUNK

    def body(dy_ref, x_ref, dt_ref, par_ref, hp_ref, dx_ref, ddt_ref, st_ref, dstate):
        step = pl.program_id(0)

        @pl.when(step == 0)
        def _():
            dstate[...] = jnp.zeros_like(dstate)

        par = par_ref[...]
        cm = _ssd_common(dt_ref[...], par)
        reduce = _head_reduce()
        lane128 = lax.broadcasted_iota(jnp.int32, (1, LANE), 1)
        row128 = lax.broadcasted_iota(jnp.int32, (LANE, 1), 0)
        d_acs = jnp.zeros((q, LANE), F32)
        d_acs_t = jnp.zeros((LANE, q), F32)
        last_terms = []
        acs_terms = []
        dxdt_all = []
        for g in range(N_GROUPS):
            lo = g * GROUP_W
            sl = slice(lo, lo + GROUP_W)
            xs = x_ref[:, sl]
            bm32 = x_ref[:, D_SSD + g * N_STATE:D_SSD + (g + 1) * N_STATE]
            cm32 = x_ref[:, D_SSD + N_GROUPS * N_STATE + g * N_STATE:D_SSD + N_GROUPS * N_STATE + (g + 1) * N_STATE]
            bm = bm32.astype(BF16)
            cmat = cm32.astype(BF16)
            dyg = dy_ref[:, sl]
            dyg_b = dyg.astype(BF16)
            xdt = xs * cm["dt_l"][:, sl]
            xdt_b = xdt.astype(BF16)
            hg = hp_ref[0, g]
            hg_b = hg.astype(BF16)
            dhg = dstate[g]
            dhg_b = dhg.astype(BF16)
            ea = cm["ea_l"][:, sl]
            ds = cm["ds_l"][:, sl]
            cd = cm["cd_l"][:, sl]
            yoff = _dot(cmat, hg_b, 1, 0) * ea
            dw = (dyg * ea).astype(BF16)
            d_c = _dot(dw, hg_b, 1, 1)
            d_hprev = _dot(cmat, dw, 0, 0) + dhg * cd
            t_acs = dyg * yoff
            d_last_g = jnp.sum(dhg * hg, axis=0, keepdims=True) * cd
            xds_b = (xdt * ds).astype(BF16)
            dxds = _dot(bm, dhg_b, 1, 0)
            d_b = _dot(xds_b, dhg_b, 1, 1)
            dxdt = dxds * ds
            t_ds = dxds * xdt * ds
            t_acs = t_acs - t_ds
            d_last_g = d_last_g + jnp.sum(t_ds, axis=0, keepdims=True)
            cb = _dot(cmat, bm, 1, 1)
            d_cb = jnp.zeros((q, q), F32)
            for r in range(4):
                h = 4 * g + r
                dec = _decay(cm, h)
                s_h = cb * dec
                mask = _lane_mask(r)
                d_s = _dot(jnp.where(mask, dyg, 0.0).astype(BF16), xdt_b, 1, 1)
                dxdt = dxdt + jnp.where(mask, _dot(s_h.astype(BF16), dyg_b, 0, 0), 0.0)
                d_cb = d_cb + d_s * dec
                d_m = d_s * s_h
                d_acs = d_acs + jnp.where(lane128 == h, jnp.sum(d_m, axis=1, keepdims=True), 0.0)
                d_acs_t = d_acs_t + jnp.where(row128 == h, jnp.sum(d_m, axis=0, keepdims=True), 0.0)
            d_cb_b = d_cb.astype(BF16)
            d_c = d_c + _dot(d_cb_b, bm, 1, 0)
            d_b = d_b + _dot(d_cb_b, cmat, 0, 0)
            dstate[g] = d_hprev
            dx_ref[:, sl] = dxdt * cm["dt_l"][:, sl] + cm["dskip_l"][:, sl] * dyg
            dx_ref[:, D_SSD + g * N_STATE:D_SSD + (g + 1) * N_STATE] = d_b
            dx_ref[:, D_SSD + N_GROUPS * N_STATE + g * N_STATE:D_SSD + N_GROUPS * N_STATE + (g + 1) * N_STATE] = d_c
            acs_terms.append(t_acs)
            dxdt_all.append(dxdt * xs)
            last_terms.append(d_last_g)
        t_acs_l = jnp.concatenate(acs_terms, axis=1)
        d_dt_l = jnp.concatenate(dxdt_all, axis=1)
        d_last_l = jnp.concatenate(last_terms, axis=1)
        d_acs = d_acs + _exact_dot(t_acs_l, reduce) - d_acs_t.T
        last_row = lax.broadcasted_iota(jnp.int32, (q, 1), 0) == q - 1
        d_acs = d_acs + jnp.where(last_row, _exact_dot(jnp.broadcast_to(d_last_l, (8, D_SSD)), reduce)[0:1, :], 0.0)
        li = lax.broadcasted_iota(jnp.int32, (q, q), 0)
        si = lax.broadcasted_iota(jnp.int32, (q, q), 1)
        d_adt = _exact_dot((si >= li).astype(F32), d_acs)
        d_dt = _exact_dot(d_dt_l, reduce) + d_adt * cm["a"]
        d_dtr = d_dt * _sigmoid(cm["v"])
        ddt_ref[...] = d_dtr.astype(BF16)
        d_skip = _exact_dot(jnp.broadcast_to(jnp.sum(dy_ref[...] * x_ref[:, 0:D_SSD], axis=0, keepdims=True), (8, D_SSD)), reduce)[0:1, :]
        rows = jnp.concatenate([
            jnp.sum(d_dtr, axis=0, keepdims=True),
            jnp.sum(d_adt * cm["dt"], axis=0, keepdims=True) * cm["a"],
            d_skip,
            jnp.zeros((5, LANE), F32)], axis=0)

        @pl.when(step == 0)
        def _():
            st_ref[...] = rows

        @pl.when(step != 0)
        def _():
            st_ref[...] += rows

    rev = lambda c: nc - 1 - c
    return pl.pallas_call(
        body, name=name, grid=(nc,),
        in_specs=[pl.BlockSpec((q, D_SSD), lambda c: (rev(c), 0)),
                  pl.BlockSpec((q, D_XBC), lambda c: (rev(c), 0)),
                  pl.BlockSpec((q, LANE), lambda c: (rev(c), COL_DT // LANE)),
                  pl.BlockSpec((8, LANE), lambda c: (0, 0)),
                  pl.BlockSpec((1, N_GROUPS, N_STATE, GROUP_W), lambda c: (rev(c), 0, 0, 0))],
        out_specs=[pl.BlockSpec((q, D_XBC), lambda c: (rev(c), 0)),
                   pl.BlockSpec((q, LANE), lambda c: (rev(c), 0)),
                   pl.BlockSpec((8, LANE), lambda c: (0, 0))],
        out_shape=[jax.ShapeDtypeStruct((seq, D_XBC), F32),
                   jax.ShapeDtypeStruct((seq, LANE), BF16),
                   jax.ShapeDtypeStruct((8, LANE), F32)],
        scratch_shapes=[pltpu.VMEM((N_GROUPS, N_STATE, GROUP_W), F32)],
        compiler_params=_cp(("arbitrary",)))(dy, xbc, proj, par, hprev)


def _gate_norm_fwd(y, proj, wn, tm, name):
    seq = y.shape[0]

    def body(y_ref, z_ref, w_ref, o_ref):
        for g in range(N_GROUPS):
            sl = slice(g * GROUP_W, (g + 1) * GROUP_W)
            zv = z_ref[:, sl]
            yz = y_ref[:, sl] * (zv * _sigmoid(zv))
            r = lax.rsqrt(jnp.mean(yz * yz, axis=-1, keepdims=True) + EPS)
            o_ref[:, sl] = (yz * r * w_ref[:, sl]).astype(BF16)

    tok = pl.BlockSpec((tm, D_SSD), lambda i: (i, 0))
    return pl.pallas_call(
        body, name=name, grid=(seq // tm,),
        in_specs=[tok, tok, pl.BlockSpec((1, D_SSD), lambda i: (0, 0))],
        out_specs=tok,
        out_shape=jax.ShapeDtypeStruct((seq, D_SSD), BF16),
        compiler_params=_cp(("parallel",)))(y, proj, wn)


def _gate_norm_bwd(dys, y, proj, wn, tm, name):
    seq = y.shape[0]

    def body(d_ref, y_ref, z_ref, w_ref, dy_ref, dz_ref, st_ref):
        rows = []
        for g in range(N_GROUPS):
            sl = slice(g * GROUP_W, (g + 1) * GROUP_W)
            zv = z_ref[:, sl]
            yv = y_ref[:, sl]
            sg = _sigmoid(zv)
            sz = zv * sg
            yz = yv * sz
            r = lax.rsqrt(jnp.mean(yz * yz, axis=-1, keepdims=True) + EPS)
            yn = yz * r
            dv = d_ref[:, sl]
            dyn = dv * w_ref[:, sl]
            dyz = r * (dyn - yn * jnp.mean(dyn * yn, axis=-1, keepdims=True))
            dy_ref[:, sl] = dyz * sz
            dz_ref[:, sl] = (dyz * yv * _silu_grad(zv, sg)).astype(BF16)
            rows.append(jnp.sum(dv * yn, axis=0, keepdims=True))
        rows = jnp.concatenate([jnp.concatenate(rows, axis=1), jnp.zeros((7, D_SSD), F32)], axis=0)

        @pl.when(pl.program_id(0) == 0)
        def _():
            st_ref[...] = rows

        @pl.when(pl.program_id(0) != 0)
        def _():
            st_ref[...] += rows

    tok = pl.BlockSpec((tm, D_SSD), lambda i: (i, 0))
    return pl.pallas_call(
        body, name=name, grid=(seq // tm,),
        in_specs=[tok, tok, tok, pl.BlockSpec((1, D_SSD), lambda i: (0, 0))],
        out_specs=[tok, tok, pl.BlockSpec((8, D_SSD), lambda i: (0, 0))],
        out_shape=[jax.ShapeDtypeStruct((seq, D_SSD), F32), jax.ShapeDtypeStruct((seq, D_SSD), BF16),
                   jax.ShapeDtypeStruct((8, D_SSD), F32)],
        compiler_params=_cp(("arbitrary",)))(dys, y, proj, wn)


def _pool_counts(t0, rows, w):
    pos = (t0 + 1 + lax.broadcasted_iota(jnp.int32, (rows, 1), 0)).astype(F32)
    return jnp.minimum(pos, float(w))


def _window_means(ext, t0):
    n = ext.shape[0]
    outs = []
    run = ext
    width = 1
    sums = {}
    while width < 16:
        run = run + pltpu.roll(run, width, 0)
        width *= 2
        sums[width] = run
    for g, w in enumerate(POOL_WINDOWS):
        sl = slice(g * POOL_GW, (g + 1) * POOL_GW)
        cnt = _pool_counts(t0, n - HALO, w)
        outs.append(sums[w][HALO:, sl] / cnt - ext[HALO:, sl])
    return outs


def _pool_fwd(proj, pw, pb, ps, tm, name):
    seq = proj.shape[0]

    def body(u_ref, p_ref, pw_ref, pb_ref, ps_ref, o_ref):
        i = pl.program_id(0)
        prev = jnp.where(i == 0, 0.0, p_ref[...])
        ext = jnp.concatenate([prev, u_ref[...]], axis=0)
        diffs = _window_means(ext, i * tm)
        for g in range(4):
            sl = slice(g * POOL_GW, (g + 1) * POOL_GW)
            out = _dot(diffs[g].astype(BF16), pw_ref[g], 1, 0) + pb_ref[:, sl]
            o_ref[:, sl] = (out * ps_ref[:, sl]).astype(BF16)

    c0 = COL_U // 1024
    vec = pl.BlockSpec((1, D_POOL), lambda i: (0, 0))
    return pl.pallas_call(
        body, name=name, grid=(seq // tm,),
        in_specs=[pl.BlockSpec((tm, 1024), lambda i: (i, c0)),
                  pl.BlockSpec((HALO, 1024), lambda i: (jnp.maximum(i * (tm // HALO) - 1, 0), c0)),
                  pl.BlockSpec((4, POOL_GW, POOL_GW), lambda i: (0, 0, 0)), vec, vec],
        out_specs=pl.BlockSpec((tm, D_POOL), lambda i: (i, 0)),
        out_shape=jax.ShapeDtypeStruct((seq, D_POOL), BF16),
        compiler_params=_cp(("parallel",)))(proj, proj, pw, pb, ps)


def _pool_bwd(dyp, proj, pw, pb, ps, tm, name):
    seq = proj.shape[0]
    ni = seq // tm

    def body(d_ref, dn_ref, u_ref, p_ref, pw_ref, pb_ref, ps_ref, du_ref, dw_ref, st_ref):
        i = pl.program_id(0)
        prev = jnp.where(i == 0, 0.0, p_ref[...])
        ext = jnp.concatenate([prev, u_ref[...]], axis=0)
        diffs = _window_means(ext, i * tm)
        dnext = jnp.where(i == ni - 1, 0.0, dn_ref[...])
        dext = jnp.concatenate([d_ref[...], dnext], axis=0)
        n = tm + HALO
        b_rows, s_rows = [], []
        for g, w in enumerate(POOL_WINDOWS):
            sl = slice(g * POOL_GW, (g + 1) * POOL_GW)
            wg = pw_ref[g]
            dout = dext[:, sl] * ps_ref[:, sl]
            dcur = dout[:tm]
            pre = _dot(diffs[g].astype(BF16), wg, 1, 0) + pb_ref[:, sl]
            s_rows.append(jnp.sum(d_ref[:, sl] * pre, axis=0, keepdims=True))
            b_rows.append(jnp.sum(dcur, axis=0, keepdims=True))
            dwg = _dot(diffs[g].astype(BF16), dcur.astype(BF16), 0, 0)

            @pl.when(i == 0)
            def _():
                dw_ref[g] = dwg

            @pl.when(i != 0)
            def _():
                dw_ref[g] += dwg

            ddiff = _dot(dout.astype(BF16), wg, 1, 1)
            scaled = ddiff / _pool_counts(i * tm, n, w)
            run = scaled
            width = 1
            while width < w:
                run = run + pltpu.roll(run, n - width, 0)
                width *= 2
            du_ref[:, sl] = (run[:tm] - ddiff[:tm]).astype(BF16)
        rows = jnp.concatenate([jnp.concatenate(b_rows, axis=1), jnp.concatenate(s_rows, axis=1),
                                jnp.zeros((6, D_POOL), F32)], axis=0)

        @pl.when(i == 0)
        def _():
            st_ref[...] = rows

        @pl.when(i != 0)
        def _():
            st_ref[...] += rows

    c0 = COL_U // 1024
    vec = pl.BlockSpec((1, D_POOL), lambda i: (0, 0))
    last = seq // HALO - 1
    return pl.pallas_call(
        body, name=name, grid=(ni,),
        in_specs=[pl.BlockSpec((tm, D_POOL), lambda i: (i, 0)),
                  pl.BlockSpec((HALO, D_POOL), lambda i: (jnp.minimum((i + 1) * (tm // HALO), last), 0)),
                  pl.BlockSpec((tm, 1024), lambda i: (i, c0)),
                  pl.BlockSpec((HALO, 1024), lambda i: (jnp.maximum(i * (tm // HALO) - 1, 0), c0)),
                  pl.BlockSpec((4, POOL_GW, POOL_GW), lambda i: (0, 0, 0)), vec, vec],
        out_specs=[pl.BlockSpec((tm, D_POOL), lambda i: (i, 0)),
                   pl.BlockSpec((4, POOL_GW, POOL_GW), lambda i: (0, 0, 0)),
                   pl.BlockSpec((8, D_POOL), lambda i: (0, 0))],
        out_shape=[jax.ShapeDtypeStruct((seq, D_POOL), BF16),
                   jax.ShapeDtypeStruct((4, POOL_GW, POOL_GW), F32),
                   jax.ShapeDtypeStruct((8, D_POOL), F32)],
        compiler_params=_cp(("arbitrary",)))(dyp, dyp, proj, proj, pw, pb, ps)


def _mix_out(ys, yp, wout, x1, mod, tm, name):
    seq = ys.shape[0]

    def body(ys_ref, yp_ref, w_ref, x_ref, mod_ref, xo_ref, m_ref):
        mix = _dot(ys_ref[...], w_ref[0:D_SSD, :], 1, 0) + _dot(yp_ref[...], w_ref[D_SSD:2 * D_SSD, :], 1, 0)
        m_ref[...] = mix.astype(BF16)
        xo_ref[...] = x_ref[...] + mod_ref[5:6, :] * mix

    tok = pl.BlockSpec((tm, D), lambda i: (i, 0))
    return pl.pallas_call(
        body, name=name, grid=(seq // tm,),
        in_specs=[tok, tok, pl.BlockSpec((2 * D_SSD, D), lambda i: (0, 0)), tok,
                  pl.BlockSpec((9, D), lambda i: (0, 0))],
        out_specs=[tok, tok],
        out_shape=[jax.ShapeDtypeStruct((seq, D), F32), jax.ShapeDtypeStruct((seq, D), BF16)],
        compiler_params=_cp(("parallel",)))(ys, yp, wout, x1, mod)


def _mix_bwd_dh(dz, dxbc, du, ddt, win_t, x1, dx2, mixb, wn, mod, tm, name):
    seq = x1.shape[0]

    def body(dz_ref, dx_ref, du_ref, ddt_ref, w_ref, x_ref, dxo_ref, m_ref, wn_ref, mod_ref, o_ref, st_ref):
        dh = (_dot(dz_ref[...], w_ref[COL_Z:COL_Z + 1024, :], 1, 0)
              + _dot(dx_ref[...], w_ref[COL_XBC:COL_XBC + D_XBC, :], 1, 0)
              + _dot(du_ref[...], w_ref[COL_U:COL_U + 1024, :], 1, 0)
              + _dot(ddt_ref[...], w_ref[COL_DT:COL_DT + LANE, :], 1, 0))
        o_ref[...] = _norm_bwd(dh, x_ref[...], dxo_ref[...], m_ref[...].astype(F32), wn_ref[...],
                               mod_ref[4:5, :], 1.0, st_ref, pl.program_id(0) == 0)

    tok = pl.BlockSpec((tm, D), lambda i: (i, 0))
    return pl.pallas_call(
        body, name=name, grid=(seq // tm,),
        in_specs=[tok, pl.BlockSpec((tm, D_XBC), lambda i: (i, 0)), tok,
                  pl.BlockSpec((tm, LANE), lambda i: (i, 0)),
                  pl.BlockSpec((D_IN_PAD, D), lambda i: (0, 0)),
                  tok, tok, tok,
                  pl.BlockSpec((1, D), lambda i: (0, 0)),
                  pl.BlockSpec((9, D), lambda i: (0, 0))],
        out_specs=[tok, pl.BlockSpec((8, D), lambda i: (0, 0))],
        out_shape=[jax.ShapeDtypeStruct((seq, D), F32), jax.ShapeDtypeStruct((8, D), F32)],
        compiler_params=_cp(("arbitrary",)))(dz, dxbc, du, ddt, win_t, x1, dx2, mixb, wn, mod)


def _mix_bwd_dycat(dmix, wout, tm, name):
    seq = dmix.shape[0]

    def body(d_ref, w_ref, a_ref, b_ref):
        dv = d_ref[...]
        a_ref[...] = _dot(dv, w_ref[0:D_SSD, :], 1, 1)
        b_ref[...] = _dot(dv, w_ref[D_SSD:2 * D_SSD, :], 1, 1)

    tok = pl.BlockSpec((tm, D), lambda i: (i, 0))
    return pl.pallas_call(
        body, name=name, grid=(seq // tm,),
        in_specs=[tok, pl.BlockSpec((2 * D_SSD, D), lambda i: (0, 0))],
        out_specs=[tok, tok],
        out_shape=[jax.ShapeDtypeStruct((seq, D), F32)] * 2,
        compiler_params=_cp(("parallel",)))(dmix, wout)


def _local_step(x, tgt, mod, wff, win_t, wout, pool_w, vecs, tm):
    seq = x.shape[0]
    tk = min(seq, 512)
    x1, s1 = _ffn_forward(x, wff, 0, vecs["ffn1_norm"], mod, 0, tm, "ffn1")
    h2 = _prenorm(x1, vecs["mix_norm"], mod, 1, tm, "mix_prenorm")
    proj = _mm_nt(h2, win_t, tm, D_IN_PAD // 3, F32, "mix_in_proj")
    xbc = _conv_fwd(proj, vecs["conv_w"], vecs["conv_b"], tm, "mix_conv")
    y, hprev = _ssd_fwd(xbc, proj, vecs["ssd_par"], "mix_ssd")
    ys = _gate_norm_fwd(y, proj, vecs["ssd_norm_w"], tm, "mix_gate_norm")
    yp = _pool_fwd(proj, pool_w, vecs["pool_b"], vecs["pool_scale"], tm, "mix_pool")
    x2, mixb = _mix_out(ys, yp, wout, x1, mod, tm, "mix_out_proj")
    x3, s3 = _ffn_forward(x2, wff, 3, vecs["ffn2_norm"], mod, 2, tm, "ffn2")
    dx3, st_loss = _loss_head(x3, vecs["final_norm"], tgt, tm, "loss_head")

    dx2, st3, dw3 = _ffn_backward(dx3, s3, wff, 3, vecs["ffn2_norm"], mod, 2, tm, "ffn2")
    dmix = _scale_cast(dx2, mod, 5, 1.0, tm, "mix_dbranch")
    dys, dyp = _mix_bwd_dycat(dmix, wout, tm, "mix_bwd_dycat")
    d_wout = (_mm_tn(ys, dmix, D_SSD, tk, "mix_dw_out_ssd"), _mm_tn(yp, dmix, D_POOL, tk, "mix_dw_out_pool"))
    du, d_pool_w, st_pool = _pool_bwd(dyp, proj, pool_w, vecs["pool_b"], vecs["pool_scale"], tm, "mix_pool_bwd")
    dy, dz, st_gn = _gate_norm_bwd(dys, y, proj, vecs["ssd_norm_w"], tm, "mix_gate_norm_bwd")
    dxbc_act, ddt, st_ssd = _ssd_bwd(dy, xbc, proj, vecs["ssd_par"], hprev, "mix_ssd_bwd")
    dxbc, st_conv = _conv_bwd(dxbc_act, proj, vecs["conv_w"], vecs["conv_b"], tm, "mix_conv_bwd")
    dx1, st2 = _mix_bwd_dh(dz, dxbc, du, ddt, win_t, x1, dx2, mixb, vecs["mix_norm"], mod, min(tm, 256), "mix_bwd_dh")
    d_win = (_mm_tn(dz, h2, 1024, tk, "mix_dw_in_z"), _mm_tn(dxbc, h2, 1024, tk, "mix_dw_in_xbc"),
             _mm_tn(du, h2, 1024, tk, "mix_dw_in_u"), _mm_tn(ddt, h2, LANE, tk, "mix_dw_in_dt"))
    dx0, st1, dw1 = _ffn_backward(dx1, s1, wff, 0, vecs["ffn1_norm"], mod, 0, tm, "ffn1")
    stats = dict(ffn1=st1, mix=st2, ffn2=st3, loss=st_loss, pool=st_pool, gn=st_gn, ssd=st_ssd, conv=st_conv)
    return dx0, stats, dw1, dw3, d_win, d_wout, d_pool_w


HBM_SPEC = pl.BlockSpec(memory_space=pltpu.HBM)


def _mesh_pos():
    return lax.axis_index("x"), lax.axis_index("y"), lax.axis_index("c")


def _other_chips(x, y):
    return [(1 - x, y), (x, 1 - y), (1 - x, 1 - y)]


def _all_gather(src, regions, name):
    total, cols = src.shape
    assert sum(r for _, r in regions) == total

    def body(src_ref, out_ref, send_sems, recv_sems, local_sem):
        x, y, c = _mesh_pos()
        me, sibling = (x, y, c), (x, y, 1 - c)
        chips = _other_chips(x, y)

        def rows_of(dev, off, rows):
            start = pl.multiple_of(N_DEV * off + (4 * dev[0] + 2 * dev[1] + dev[2]) * rows, 8)
            return out_ref.at[pl.ds(start, rows), :]

        def copies(k, block, to, from_src):
            out = []
            for off, rows in regions:
                dst = rows_of(block, off, rows)
                out.append(pltpu.make_async_remote_copy(
                    src_ref=src_ref.at[pl.ds(off, rows), :] if from_src else dst, dst_ref=dst,
                    send_sem=send_sems.at[k], recv_sem=recv_sems.at[k], device_id=to, device_id_type=MESH))
            return out

        def drain(k):
            whole = out_ref.at[pl.ds(0, total), :]
            return pltpu.make_async_remote_copy(src_ref=whole, dst_ref=whole, send_sem=send_sems.at[k],
                                                recv_sem=recv_sems.at[k], device_id=me, device_id_type=MESH)

        for off, rows in regions:
            pltpu.make_async_copy(src_ref.at[pl.ds(off, rows), :], rows_of(me, off, rows), local_sem).start()
        first = copies(0, me, sibling, True)
        for j, chip in enumerate(chips):
            first += copies(1 + j, me, (*chip, c), True)
        for cp in first:
            cp.start()
        for j, chip in enumerate(chips):
            drain(1 + j).wait_recv()
            for cp in copies(4 + j, (*chip, c), sibling, False):
                cp.start()
        drain(0).wait_recv()
        for j in range(3):
            drain(4 + j).wait_recv()
        for k in range(7):
            drain(k).wait_send()
        pltpu.make_async_copy(src_ref, out_ref.at[pl.ds(0, total), :], local_sem).wait()

    return pl.pallas_call(
        body, name=name,
        out_shape=jax.ShapeDtypeStruct((N_DEV * total, cols), src.dtype),
        in_specs=[HBM_SPEC], out_specs=HBM_SPEC,
        scratch_shapes=[pltpu.SemaphoreType.DMA((7,)), pltpu.SemaphoreType.DMA((7,)), pltpu.SemaphoreType.DMA],
    )(src)


def _rs_pair(grads, total, name):
    cols = grads[0][0].shape[1]
    n = len(grads)
    sent = sum(rows for _, _, rows in grads)

    def body(*refs):
        g_refs, recv_ref, send_sem, recv_sem = refs[:n], refs[n], refs[n + 1], refs[n + 2]
        x, y, c = _mesh_pos()
        sibling = (x, y, 1 - c)
        for q in range(4):
            for g_ref, (_, off, rows) in zip(g_refs, grads):
                theirs = g_ref.at[pl.ds(pl.multiple_of((2 * q + 1 - c) * rows, 8), rows), :]
                pltpu.make_async_remote_copy(
                    src_ref=theirs, dst_ref=recv_ref.at[q, pl.ds(off, rows), :], send_sem=send_sem, recv_sem=recv_sem,
                    device_id=sibling, device_id_type=MESH).start()
        everything = recv_ref.at[:, pl.ds(0, sent), :]
        whole = pltpu.make_async_remote_copy(src_ref=everything, dst_ref=everything, send_sem=send_sem,
                                             recv_sem=recv_sem, device_id=sibling, device_id_type=MESH)
        whole.wait_send()
        whole.wait_recv()

    return pl.pallas_call(
        body, name=name, out_shape=jax.ShapeDtypeStruct((4, total, cols), F32),
        in_specs=[HBM_SPEC] * n, out_specs=HBM_SPEC,
        scratch_shapes=[pltpu.SemaphoreType.DMA, pltpu.SemaphoreType.DMA],
    )(*[g for g, _, _ in grads])


def _pair_sum(g, from_sibling, off, rows, pos, name):
    cols = g.shape[1]

    def body(pos_ref, g_ref, r_ref, o_ref, ob_ref):
        s = g_ref[...] + r_ref[...]
        o_ref[...] = s
        ob_ref[...] = s.astype(BF16)

    out = pl.BlockSpec((None, rows, cols), lambda q, pos_ref: (q, 0, 0))
    return pl.pallas_call(
        body, name=name,
        grid_spec=pltpu.PrefetchScalarGridSpec(
            num_scalar_prefetch=1, grid=(4,),
            in_specs=[pl.BlockSpec((None, None, rows, cols), lambda q, pos_ref: (q, pos_ref[0], 0, 0)),
                      pl.BlockSpec((None, rows, cols), lambda q, pos_ref: (q, off // rows, 0))],
            out_specs=[out, out]),
        out_shape=[jax.ShapeDtypeStruct((4, rows, cols), F32), jax.ShapeDtypeStruct((4, rows, cols), BF16)],
        compiler_params=_cp(("parallel",)))(pos, g.reshape(4, 2, rows, cols), from_sibling)


def _rs_chips(parts, total, name):
    cols = parts[0][0].shape[2]
    n = len(parts)
    sent = sum(rows for _, _, rows in parts)

    def body(*refs):
        p_refs, out_ref, send_sems, recv_sems = refs[:n], refs[n], refs[n + 1], refs[n + 2]
        x, y, c = _mesh_pos()
        chips = _other_chips(x, y)
        for j, chip in enumerate(chips):
            q = 2 * chip[0] + chip[1]
            for p_ref, (_, off, rows) in zip(p_refs, parts):
                pltpu.make_async_remote_copy(
                    src_ref=p_ref.at[q], dst_ref=out_ref.at[j, pl.ds(off, rows), :], send_sem=send_sems.at[j],
                    recv_sem=recv_sems.at[j], device_id=(*chip, c), device_id_type=MESH).start()
        for j, chip in enumerate(chips):
            everything = out_ref.at[j, pl.ds(0, sent), :]
            whole = pltpu.make_async_remote_copy(src_ref=everything, dst_ref=everything, send_sem=send_sems.at[j],
                                                 recv_sem=recv_sems.at[j], device_id=(*chip, c), device_id_type=MESH)
            whole.wait_recv()
            whole.wait_send()

    return pl.pallas_call(
        body, name=name, out_shape=jax.ShapeDtypeStruct((3, total, cols), BF16),
        in_specs=[HBM_SPEC] * n, out_specs=HBM_SPEC,
        scratch_shapes=[pltpu.SemaphoreType.DMA((3,)), pltpu.SemaphoreType.DMA((3,))],
    )(*[p for p, _, _ in parts])


def _chip_sum(p, from_chips, off, rows, pos, name):
    cols = p.shape[2]

    def body(pos_ref, p_ref, r_ref, o_ref):
        acc = p_ref[...]
        for j in range(3):
            acc = acc + r_ref[j].astype(F32)
        o_ref[...] = acc

    return pl.pallas_call(
        body, name=name,
        grid_spec=pltpu.PrefetchScalarGridSpec(
            num_scalar_prefetch=1, grid=(1,),
            in_specs=[pl.BlockSpec((None, rows, cols), lambda i, pos_ref: (pos_ref[1], 0, 0)),
                      pl.BlockSpec((3, rows, cols), lambda i, pos_ref: (0, off // rows, 0))],
            out_specs=pl.BlockSpec((rows, cols), lambda i, pos_ref: (0, 0))),
        out_shape=jax.ShapeDtypeStruct((rows, cols), F32),
        compiler_params=_cp(("arbitrary",)))(pos, p, from_chips)


def _row_tile(rows, cap):
    t = min(rows, cap)
    while rows % t or t % 8:
        t -= 8
    return t


def _ada_mod(c_all, w, b, name):
    n = w.shape[1]

    def body(c_ref, w_ref, b_ref, o_ref):
        cv = c_ref[...]
        o_ref[...] = _exact_dot(cv * _sigmoid(cv), w_ref[...]) + b_ref[...]

    return pl.pallas_call(body, name=name, out_shape=jax.ShapeDtypeStruct((N_DEV, n), F32),
                          compiler_params=pltpu.CompilerParams(vmem_limit_bytes=VMEM_LIMIT))(c_all, w, b)


def _ada_grad(c_all, dmod, name):
    n = dmod.shape[1]

    def body(c_ref, d_ref, o_ref):
        cv = c_ref[...]
        o_ref[...] = _dot(cv * _sigmoid(cv), d_ref[...], 0, 0, lax.Precision.HIGHEST)

    return pl.pallas_call(body, name=name, out_shape=jax.ShapeDtypeStruct((D, n), F32),
                          compiler_params=pltpu.CompilerParams(vmem_limit_bytes=VMEM_LIMIT))(c_all, dmod)


def _adamw(w, g, m, v, name):
    rows, cols = w.shape
    tr = _row_tile(rows, 256) if rows % 8 == 0 else rows
    c1 = 1.0 - ADAM_B1 ** ADAM_STEP
    c2 = 1.0 - ADAM_B2 ** ADAM_STEP

    def body(w_ref, g_ref, m_ref, v_ref, d_ref, mo_ref, vo_ref):
        gv = g_ref[...]
        mn = ADAM_B1 * m_ref[...] + (1.0 - ADAM_B1) * gv
        vn = ADAM_B2 * v_ref[...] + (1.0 - ADAM_B2) * (gv * gv)
        mo_ref[...] = mn
        vo_ref[...] = vn
        d_ref[...] = -ADAM_LR * ((mn / c1) / (jnp.sqrt(vn / c2) + ADAM_EPS) + ADAM_WD * w_ref[...])

    spec = pl.BlockSpec((tr, cols), lambda i: (i, 0))
    shape = jax.ShapeDtypeStruct((rows, cols), F32)
    return pl.pallas_call(body, name=name, grid=(rows // tr,), in_specs=[spec] * 4, out_specs=[spec] * 3,
                          out_shape=[shape] * 3, compiler_params=_cp(("parallel",)))(w, g, m, v)


def _sum8_loss(v, loss_row, name):
    rows = v.shape[0] // N_DEV

    def body(v_ref, o_ref, l_ref):
        acc = v_ref[0:rows, :]
        for k in range(1, N_DEV):
            acc = acc + v_ref[k * rows:(k + 1) * rows, :]
        o_ref[...] = acc
        part = jnp.sum(acc[loss_row:loss_row + 8, :], axis=0, keepdims=True)
        l_ref[...] = jnp.broadcast_to(jnp.sum(part, axis=1, keepdims=True), (8, LANE))

    return pl.pallas_call(body, name=name,
                          out_shape=[jax.ShapeDtypeStruct((rows, LANE), F32), jax.ShapeDtypeStruct((8, LANE), F32)],
                          compiler_params=pltpu.CompilerParams(vmem_limit_bytes=VMEM_LIMIT))(v)


WEIGHT_NAMES = ("w_ada", "b_ada", "ffn1_norm", "ffn1_w_gate", "ffn1_w_up", "ffn1_w_down", "mix_norm", "w_in",
                "conv_w", "conv_b", "dt_bias", "a_log", "d_skip", "ssd_norm_w", "pool_w", "pool_b", "pool_scale",
                "w_out", "ffn2_norm", "ffn2_w_gate", "ffn2_w_up", "ffn2_w_down", "final_norm")

FF_SHARD = FF // N_DEV
IN_SHARD = D_IN // N_DEV
IN_SHARD_PAD = 528
OUT_SHARD = 2 * D_SSD // N_DEV
ADA_SHARD = 9 * D // N_DEV
POOL_SHARD_ROWS = 4 * 32 * POOL_GW // D
PACK = dict(gate1=(0, FF_SHARD), up1=(352, FF_SHARD), down1=(704, FF_SHARD), gate2=(1056, FF_SHARD),
            up2=(1408, FF_SHARD), down2=(1760, FF_SHARD), w_out=(2112, OUT_SHARD), w_in=(2368, IN_SHARD_PAD),
            pool_w=(2896, POOL_SHARD_ROWS))
PACK_W_ROWS = 2896
GPACK = dict(w_in=(0, IN_SHARD_PAD), w_out=(768, OUT_SHARD), gate1=(1056, FF_SHARD), up1=(1408, FF_SHARD),
             down1=(1760, FF_SHARD), gate2=(2112, FF_SHARD), up2=(2464, FF_SHARD), down2=(2816, FF_SHARD),
             pool_w=(3168, POOL_SHARD_ROWS))
PACK_G_ROWS = 3200

SMALL_ROWS = dict(dmod=(0, 72), ffn1_norm=(72, 8), mix_norm=(80, 8), ffn2_norm=(88, 8), final_norm=(96, 8),
                  ssd_norm_w=(104, 8), pool_scale=(112, 8), conv_b=(120, 16), conv_w=(136, 64), pool_b=(200, 8),
                  ssd=(208, 3), loss=(216, 8))
SMALL_TOTAL = 224


def _rows128(v, rows):
    flat = v.reshape(-1)
    return jnp.pad(flat, (0, rows * LANE - flat.shape[0])).reshape(rows, LANE)


def _pad_lanes(v):
    return jnp.pad(v.reshape(-1), (0, LANE - v.size))


def kernel(x, c, w_ada, b_ada, ffn1_norm, ffn1_w_gate, ffn1_w_up, ffn1_w_down, mix_norm, w_in, conv_w, conv_b, dt_bias, a_log, d_skip, ssd_norm_w, pool_w, pool_b, pool_scale, w_out, ffn2_norm, ffn2_w_gate, ffn2_w_up, ffn2_w_down, final_norm, loss_target, m_w_ada, m_b_ada, m_ffn1_norm, m_ffn1_w_gate, m_ffn1_w_up, m_ffn1_w_down, m_mix_norm, m_w_in, m_conv_w, m_conv_b, m_dt_bias, m_a_log, m_d_skip, m_ssd_norm_w, m_pool_w, m_pool_b, m_pool_scale, m_w_out, m_ffn2_norm, m_ffn2_w_gate, m_ffn2_w_up, m_ffn2_w_down, m_final_norm, v_w_ada, v_b_ada, v_ffn1_norm, v_ffn1_w_gate, v_ffn1_w_up, v_ffn1_w_down, v_mix_norm, v_w_in, v_conv_w, v_conv_b, v_dt_bias, v_a_log, v_d_skip, v_ssd_norm_w, v_pool_w, v_pool_b, v_pool_scale, v_w_out, v_ffn2_norm, v_ffn2_w_gate, v_ffn2_w_up, v_ffn2_w_down, v_final_norm):
    given = dict(locals())
    w = {n: given[n] for n in WEIGHT_NAMES}
    m = {n: given["m_" + n] for n in WEIGHT_NAMES}
    v = {n: given["v_" + n] for n in WEIGHT_NAMES}
    mx, my, mc = _mesh_pos()
    me = 4 * mx + 2 * my + mc

    small = jnp.concatenate([c.reshape(-1), conv_w.reshape(-1), pool_b.reshape(-1), pool_w.reshape(-1)])
    small_rows = 280
    gs = _all_gather(_rows128(small, small_rows), [(0, small_rows)], "ag_small").reshape(N_DEV, small_rows * LANE)
    c_all = gs[:, 0:D]
    conv_w_full = gs[:, 1024:2048].reshape(N_DEV, 4, 256).transpose(1, 0, 2).reshape(4, D_XBC)
    pool_b_full = gs[:, 2048:2176].reshape(N_DEV, 4, 32).transpose(1, 0, 2).reshape(1, D_POOL)
    pool_w_full = gs[:, 2176:2176 + 32768].reshape(N_DEV, 4, 32, POOL_GW).transpose(1, 0, 2, 3).reshape(4, POOL_GW, POOL_GW).astype(BF16)

    b_ada_cols = lax.dynamic_slice(b_ada, (0, me * ADA_SHARD), (1, ADA_SHARD))
    mod_part = _ada_mod(c_all, w_ada[0], b_ada_cols, "ada_mod")
    mod_all = _all_gather(mod_part, [(0, N_DEV)], "ag_mod").reshape(N_DEV, N_DEV, ADA_SHARD)
    mod = lax.dynamic_index_in_dim(mod_all, me, axis=1, keepdims=False).reshape(9, D)

    win_t_shard = jnp.pad(w_in[0].T, ((0, IN_SHARD_PAD - IN_SHARD), (0, 0)))
    wpack = jnp.concatenate([ffn1_w_gate[0].T, ffn1_w_up[0].T, ffn1_w_down[0], ffn2_w_gate[0].T, ffn2_w_up[0].T,
                             ffn2_w_down[0], w_out[0], win_t_shard], axis=0).astype(BF16)
    regions = [PACK[k] for k in ("gate1", "up1", "down1", "gate2", "up2", "down2", "w_out", "w_in")]
    full = _all_gather(wpack, regions, "ag_weights")
    o_out, o_in = N_DEV * PACK["w_out"][0], N_DEV * PACK["w_in"][0]
    wout_full = full[o_out:o_out + 2 * D_SSD]
    win_g = full[o_in:o_in + N_DEV * IN_SHARD_PAD].reshape(N_DEV, IN_SHARD_PAD, D)[:, :IN_SHARD].reshape(D_IN, D)
    win_t = jnp.concatenate([win_g[0:1024], win_g[1024:3072], win_g[3088:4112], win_g[3072:3088],
                             jnp.zeros((D_IN_PAD - D_IN, D), BF16)], axis=0)

    vecs = dict(ffn1_norm=ffn1_norm, mix_norm=mix_norm, ffn2_norm=ffn2_norm, final_norm=final_norm.reshape(1, D),
                conv_w=conv_w_full, conv_b=conv_b, ssd_norm_w=ssd_norm_w, pool_b=pool_b_full, pool_scale=pool_scale,
                ssd_par=jnp.concatenate([_pad_lanes(dt_bias)[None], _pad_lanes(a_log)[None], _pad_lanes(d_skip)[None],
                                         jnp.zeros((5, LANE), F32)], axis=0))
    dx0, st, dw1, dw3, d_win, d_wout, d_pool_w = _local_step(
        x[0], loss_target[0], mod, full, win_t, wout_full, pool_w_full, vecs, min(512, x.shape[1]))

    dwin = jnp.concatenate([d_win[0], d_win[1], d_win[3][0:16], d_win[2]], axis=0)
    dwin = jnp.pad(dwin.reshape(N_DEV, IN_SHARD, D), ((0, 0), (0, IN_SHARD_PAD - IN_SHARD), (0, 0))).reshape(N_DEV * IN_SHARD_PAD, D)
    dwout = jnp.concatenate(d_wout, axis=0)
    dpool = d_pool_w.reshape(4, N_DEV, 32, POOL_GW).transpose(1, 0, 2, 3).reshape(N_DEV * POOL_SHARD_ROWS, D)
    keys = ("gate1", "up1", "down1", "gate2", "up2", "down2", "w_out", "w_in", "pool_w")
    grads = [(g, *GPACK[k]) for g, k in zip((*dw1, *dw3, dwout, dwin, dpool), keys)]
    pos = jnp.stack([mc, 2 * mx + my]).astype(jnp.int32)
    from_sibling = _rs_pair(grads, PACK_G_ROWS, "rs_pair")
    pairs = {k: _pair_sum(g, from_sibling, off, rows, pos, f"rs_pair_sum_{k}") for (g, off, rows), k in zip(grads, keys)}
    from_chips = _rs_chips([(pairs[k][1], *GPACK[k]) for k in keys], PACK_G_ROWS, "rs_chips")
    reduced = {k: _chip_sum(pairs[k][0], from_chips, *GPACK[k], pos, f"rs_chip_sum_{k}") for k in keys}

    def shard(k, rows=None):
        return reduced[k] if rows is None else reduced[k][0:rows]

    dmod = jnp.concatenate([st["ffn1"][0:3], st["mix"][0:3], st["ffn2"][0:3]], axis=0)
    sg = jnp.concatenate([
        dmod.reshape(-1), st["ffn1"][3], st["mix"][3], st["ffn2"][3], st["loss"][0], st["gn"][0], st["pool"][1],
        st["conv"][4], st["conv"][0:4].reshape(-1), st["pool"][0], st["ssd"][0:3].reshape(-1),
        jnp.zeros((5 * LANE,), F32), st["loss"][1]])
    sg_all = _all_gather(sg.reshape(SMALL_TOTAL, LANE), [(0, SMALL_TOTAL)], "ag_small_grads")
    tot, loss_b = _sum8_loss(sg_all, SMALL_ROWS["loss"][0], "small_sum")
    loss = loss_b[0, 0]
    dmod_all = sg_all.reshape(N_DEV, SMALL_TOTAL * LANE)[:, 0:9 * D]
    g_w_ada = _ada_grad(c_all, lax.dynamic_slice(dmod_all, (0, me * ADA_SHARD), (N_DEV, ADA_SHARD)), "ada_grad")

    def tot_rows(k):
        off, n = SMALL_ROWS[k]
        return tot[off:off + n].reshape(-1)

    g_conv_w = lax.dynamic_slice(tot_rows("conv_w").reshape(4, D_XBC), (0, me * 256), (4, 256))
    g_pool_b = lax.dynamic_slice(tot_rows("pool_b").reshape(4, POOL_GW), (0, me * 32), (4, 32))
    g_ssd = tot_rows("ssd").reshape(3, LANE)
    grad = {
        "w_ada": g_w_ada[None], "b_ada": tot_rows("dmod").reshape(1, 9 * D),
        "ffn1_norm": tot_rows("ffn1_norm")[None], "mix_norm": tot_rows("mix_norm")[None],
        "ffn2_norm": tot_rows("ffn2_norm")[None], "final_norm": tot_rows("final_norm"),
        "ssd_norm_w": tot_rows("ssd_norm_w")[None], "pool_scale": tot_rows("pool_scale")[None],
        "conv_b": tot_rows("conv_b")[None], "conv_w": g_conv_w[None], "pool_b": g_pool_b[None],
        "dt_bias": g_ssd[0:1, 0:N_HEADS], "a_log": g_ssd[1:2, 0:N_HEADS], "d_skip": g_ssd[2:3, 0:N_HEADS],
        "ffn1_w_gate": shard("gate1").T[None], "ffn1_w_up": shard("up1").T[None], "ffn1_w_down": shard("down1")[None],
        "ffn2_w_gate": shard("gate2").T[None], "ffn2_w_up": shard("up2").T[None], "ffn2_w_down": shard("down2")[None],
        "w_out": shard("w_out")[None], "w_in": shard("w_in", IN_SHARD).T[None],
        "pool_w": shard("pool_w").reshape(1, 4, 32, POOL_GW),
    }

    big = ("w_ada", "ffn1_w_gate", "ffn1_w_up", "ffn1_w_down", "w_in", "pool_w", "w_out",
           "ffn2_w_gate", "ffn2_w_up", "ffn2_w_down")
    delta, new_m, new_v = {}, {}, {}
    for n in big:
        shp = w[n].shape
        two_d = (shp[-3] * shp[-2], shp[-1]) if n == "pool_w" else shp[-2:]
        d_, m_, v_ = _adamw(w[n].reshape(two_d), grad[n].reshape(two_d), m[n].reshape(two_d), v[n].reshape(two_d),
                            f"adamw_{n}")
        delta[n], new_m[n], new_v[n] = d_.reshape(shp), m_.reshape(shp), v_.reshape(shp)
    small_names = [n for n in WEIGHT_NAMES if n not in big]
    sizes = [LANE if w[n].size < LANE else w[n].size for n in small_names]
    small_rows_adam = -(-sum(sizes) // (8 * LANE)) * 8

    def pack_small(t):
        return _rows128(jnp.concatenate([_pad_lanes(t[n]) if t[n].size < LANE else t[n].reshape(-1) for n in small_names]),
                        small_rows_adam)

    d_s, m_s, v_s = _adamw(pack_small(w), pack_small(grad), pack_small(m), pack_small(v), "adamw_small")
    off = 0
    for n, size in zip(small_names, sizes):
        for res, packed in ((delta, d_s), (new_m, m_s), (new_v, v_s)):
            res[n] = packed.reshape(-1)[off:off + w[n].size].reshape(w[n].shape)
        off += size

    return (loss, dx0[None], *[grad[n] for n in WEIGHT_NAMES], *[delta[n] for n in WEIGHT_NAMES],
            *[new_m[n] for n in WEIGHT_NAMES], *[new_v[n] for n in WEIGHT_NAMES])
```

```python
import functools
import math

import jax
import jax.numpy as jnp
from jax import lax
from jax.experimental import pallas as pl
from jax.experimental.pallas import tpu as pltpu
from jax.experimental.pallas import tpu_sc as plsc

F32 = jnp.float32
BF16 = jnp.bfloat16
MESH = pl.DeviceIdType.MESH

N_DEV = 8
D = 1024
FF = 2816
D_SSD = 1024
N_HEADS = 16
HEAD_DIM = 64
N_GROUPS = 4
N_STATE = 128
CHUNK = 128
GROUP_W = D_SSD // N_GROUPS
D_XBC = D_SSD + 2 * N_GROUPS * N_STATE
D_POOL = 1024
POOL_WINDOWS = (2, 4, 8, 16)
POOL_GW = 256
D_IN = 4112
D_IN_PAD = 4224
COL_Z, COL_XBC, COL_U, COL_DT = 0, 1024, 3072, 4096
EPS = 1e-6
FFN_RES = 0.5
LANE = 128
HALO = 16

ADAM_LR, ADAM_B1, ADAM_B2, ADAM_EPS, ADAM_WD, ADAM_STEP = 0.001, 0.9, 0.999, 1e-08, 0.01, 10

VMEM_LIMIT = 56 << 20


def _cp(sem):
    return pltpu.CompilerParams(dimension_semantics=sem, vmem_limit_bytes=VMEM_LIMIT)


def _dot(a, b, ca, cb, prec=None):
    return lax.dot_general(a, b, (((ca,), (cb,)), ((), ())), precision=prec,
                           preferred_element_type=F32)


def _exact_dot(a, b):
    return _dot(a, b, 1, 0, lax.Precision.HIGHEST)


def _sigmoid(v):
    return 1.0 / (1.0 + jnp.exp(-v))


def _silu_grad(v, sg):
    return sg * (1.0 + v * (1.0 - sg))


def _mm_nt(a, bt, tm, tn, out_dtype, name):
    m, k = a.shape
    n = bt.shape[0]

    def body(a_ref, b_ref, o_ref):
        o_ref[...] = _dot(a_ref[...], b_ref[...], 1, 1).astype(out_dtype)

    return pl.pallas_call(
        body, name=name, grid=(n // tn, m // tm),
        in_specs=[pl.BlockSpec((tm, k), lambda j, i: (i, 0)),
                  pl.BlockSpec((tn, k), lambda j, i: (j, 0))],
        out_specs=pl.BlockSpec((tm, tn), lambda j, i: (i, j)),
        out_shape=jax.ShapeDtypeStruct((m, n), out_dtype),
        compiler_params=_cp(("parallel", "parallel")))(a, bt)


def _mm_tn(a, b, tm, tk, name):
    kk, m = a.shape
    n = b.shape[1]
    nk = kk // tk

    def body(a_ref, b_ref, o_ref, acc):
        k = pl.program_id(1)

        @pl.when(k == 0)
        def _():
            acc[...] = jnp.zeros_like(acc)

        acc[...] += _dot(a_ref[...], b_ref[...], 0, 0)

        @pl.when(k == nk - 1)
        def _():
            o_ref[...] = acc[...]

    return pl.pallas_call(
        body, name=name, grid=(m // tm, nk),
        in_specs=[pl.BlockSpec((tk, tm), lambda i, k: (k, i)),
                  pl.BlockSpec((tk, n), lambda i, k: (k, 0))],
        out_specs=pl.BlockSpec((tm, n), lambda i, k: (i, 0)),
        out_shape=jax.ShapeDtypeStruct((m, n), F32),
        scratch_shapes=[pltpu.VMEM((tm, n), F32)],
        compiler_params=_cp(("parallel", "arbitrary")))(a, b)


def _prenorm(x, wn, mod, k, tm, name):
    seq = x.shape[0]

    def body(x_ref, wn_ref, mod_ref, h_ref):
        xv = x_ref[...]
        r = lax.rsqrt(jnp.mean(xv * xv, axis=-1, keepdims=True) + EPS)
        hn = xv * r * wn_ref[...]
        h_ref[...] = (hn * (1.0 + mod_ref[3 * k + 1:3 * k + 2, :]) + mod_ref[3 * k:3 * k + 1, :]).astype(BF16)

    return pl.pallas_call(
        body, name=name, grid=(seq // tm,),
        in_specs=[pl.BlockSpec((tm, D), lambda i: (i, 0)),
                  pl.BlockSpec((1, D), lambda i: (0, 0)),
                  pl.BlockSpec((9, D), lambda i: (0, 0))],
        out_specs=pl.BlockSpec((tm, D), lambda i: (i, 0)),
        out_shape=jax.ShapeDtypeStruct((seq, D), BF16),
        compiler_params=_cp(("parallel",)))(x, wn, mod)


def _scale_cast(dxo, mod, row, res, tm, name):
    seq = dxo.shape[0]

    def body(d_ref, mod_ref, o_ref):
        o_ref[...] = (d_ref[...] * (res * mod_ref[row:row + 1, :])).astype(BF16)

    return pl.pallas_call(
        body, name=name, grid=(seq // tm,),
        in_specs=[pl.BlockSpec((tm, D), lambda i: (i, 0)),
                  pl.BlockSpec((9, D), lambda i: (0, 0))],
        out_specs=pl.BlockSpec((tm, D), lambda i: (i, 0)),
        out_shape=jax.ShapeDtypeStruct((seq, D), BF16),
        compiler_params=_cp(("parallel",)))(dxo, mod)


def _norm_bwd(dh, xv, dxo, branch, wn, sc, res, stats_ref, first):
    r = lax.rsqrt(jnp.mean(xv * xv, axis=-1, keepdims=True) + EPS)
    xn = xv * r
    dhn = dh * (1.0 + sc)
    dxn = dhn * wn
    dx = dxo + r * (dxn - xn * jnp.mean(dxn * xn, axis=-1, keepdims=True))
    rows = jnp.concatenate([
        jnp.sum(dh, axis=0, keepdims=True),
        jnp.sum(dh * (xn * wn), axis=0, keepdims=True),
        jnp.sum(branch * dxo, axis=0, keepdims=True) * res,
        jnp.sum(dhn * xn, axis=0, keepdims=True),
        jnp.zeros((4, D), F32)], axis=0)

    @pl.when(first)
    def _():
        stats_ref[...] = rows

    @pl.when(jnp.logical_not(first))
    def _():
        stats_ref[...] += rows

    return dx


def _loss_head(x3, wf, tgt, tm, name):
    seq = x3.shape[0]

    def body(x_ref, w_ref, t_ref, dx_ref, st_ref):
        xv = x_ref[...]
        wv = w_ref[...]
        r = lax.rsqrt(jnp.mean(xv * xv, axis=-1, keepdims=True) + EPS)
        xn = xv * r
        e = xn * wv - t_ref[...]
        dy = e * (1.0 / D)
        dxn = dy * wv
        dx_ref[...] = r * (dxn - xn * jnp.mean(dxn * xn, axis=-1, keepdims=True))
        rows = jnp.concatenate([
            jnp.sum(dy * xn, axis=0, keepdims=True),
            jnp.sum(e * e, axis=0, keepdims=True) * (0.5 / D),
            jnp.zeros((6, D), F32)], axis=0)

        @pl.when(pl.program_id(0) == 0)
        def _():
            st_ref[...] = rows

        @pl.when(pl.program_id(0) != 0)
        def _():
            st_ref[...] += rows

    return pl.pallas_call(
        body, name=name, grid=(seq // tm,),
        in_specs=[pl.BlockSpec((tm, D), lambda i: (i, 0)),
                  pl.BlockSpec((1, D), lambda i: (0, 0)),
                  pl.BlockSpec((tm, D), lambda i: (i, 0))],
        out_specs=[pl.BlockSpec((tm, D), lambda i: (i, 0)),
                   pl.BlockSpec((8, D), lambda i: (0, 0))],
        out_shape=[jax.ShapeDtypeStruct((seq, D), F32), jax.ShapeDtypeStruct((8, D), F32)],
        compiler_params=_cp(("arbitrary",)))(x3, wf, tgt)


def _ffn_up(h, w, blk, tm, tn, name):
    seq = h.shape[0]
    nj = FF // tn

    def body(h_ref, wg_ref, wu_ref, g_ref, u_ref, a_ref):
        hv = h_ref[...]
        g = _dot(hv, wg_ref[...], 1, 1)
        u = _dot(hv, wu_ref[...], 1, 1)
        g_ref[...] = g.astype(BF16)
        u_ref[...] = u.astype(BF16)
        a_ref[...] = (g * _sigmoid(g) * u).astype(BF16)

    act = pl.BlockSpec((tm, tn), lambda j, i: (i, j))
    return pl.pallas_call(
        body, name=name, grid=(nj, seq // tm),
        in_specs=[pl.BlockSpec((tm, D), lambda j, i: (i, 0)),
                  pl.BlockSpec((tn, D), lambda j, i: (blk * nj + j, 0)),
                  pl.BlockSpec((tn, D), lambda j, i: ((blk + 1) * nj + j, 0))],
        out_specs=[act, act, act],
        out_shape=[jax.ShapeDtypeStruct((seq, FF), BF16)] * 3,
        compiler_params=_cp(("parallel", "parallel")))(h, w, w)


def _ffn_down(a, w, blk, x, mod, grow, tm, tk, name):
    seq = a.shape[0]
    nk = FF // tk

    def body(a_ref, w_ref, x_ref, mod_ref, xo_ref, f_ref, acc):
        k = pl.program_id(1)

        @pl.when(k == 0)
        def _():
            acc[...] = jnp.zeros_like(acc)

        acc[...] += _dot(a_ref[...], w_ref[...], 1, 0)

        @pl.when(k == nk - 1)
        def _():
            f = acc[...]
            f_ref[...] = f.astype(BF16)
            xo_ref[...] = x_ref[...] + (FFN_RES * mod_ref[grow:grow + 1, :]) * f

    return pl.pallas_call(
        body, name=name, grid=(seq // tm, nk),
        in_specs=[pl.BlockSpec((tm, tk), lambda i, k: (i, k)),
                  pl.BlockSpec((tk, D), lambda i, k: ((blk + 2) * nk + k, 0)),
                  pl.BlockSpec((tm, D), lambda i, k: (i, 0)),
                  pl.BlockSpec((9, D), lambda i, k: (0, 0))],
        out_specs=[pl.BlockSpec((tm, D), lambda i, k: (i, 0)),
                   pl.BlockSpec((tm, D), lambda i, k: (i, 0))],
        out_shape=[jax.ShapeDtypeStruct((seq, D), F32), jax.ShapeDtypeStruct((seq, D), BF16)],
        scratch_shapes=[pltpu.VMEM((tm, D), F32)],
        compiler_params=_cp(("parallel", "arbitrary")))(a, w, x, mod)


def _ffn_bwd_da(df, w, blk, g, u, tm, tn, name):
    seq = df.shape[0]
    nj = FF // tn

    def body(df_ref, w_ref, g_ref, u_ref, dg_ref, du_ref):
        da = _dot(df_ref[...], w_ref[...], 1, 1)
        gv = g_ref[...].astype(F32)
        uv = u_ref[...].astype(F32)
        sg = _sigmoid(gv)
        dg_ref[...] = (da * uv * _silu_grad(gv, sg)).astype(BF16)
        du_ref[...] = (da * (gv * sg)).astype(BF16)

    act = pl.BlockSpec((tm, tn), lambda j, i: (i, j))
    return pl.pallas_call(
        body, name=name, grid=(nj, seq // tm),
        in_specs=[pl.BlockSpec((tm, D), lambda j, i: (i, 0)),
                  pl.BlockSpec((tn, D), lambda j, i: ((blk + 2) * nj + j, 0)),
                  act, act],
        out_specs=[act, act],
        out_shape=[jax.ShapeDtypeStruct((seq, FF), BF16)] * 2,
        compiler_params=_cp(("parallel", "parallel")))(df, w, g, u)


def _ffn_bwd_dh(dg, du, w, blk, x, dxo, fb, wn, mod, k, tm, tk, name):
    seq = x.shape[0]
    nk = FF // tk

    def body(dg_ref, du_ref, wg_ref, wu_ref, x_ref, dxo_ref, f_ref, wn_ref, mod_ref, dx_ref, st_ref, acc):
        kk = pl.program_id(1)
        first = pl.program_id(0) == 0

        @pl.when(kk == 0)
        def _():
            acc[...] = jnp.zeros_like(acc)

        acc[...] += _dot(dg_ref[...], wg_ref[...], 1, 0) + _dot(du_ref[...], wu_ref[...], 1, 0)

        @pl.when(kk == nk - 1)
        def _():
            dx_ref[...] = _norm_bwd(acc[...], x_ref[...], dxo_ref[...], f_ref[...].astype(F32), wn_ref[...],
                                    mod_ref[3 * k + 1:3 * k + 2, :], FFN_RES, st_ref, first)

    tok = pl.BlockSpec((tm, D), lambda i, kk: (i, 0))
    return pl.pallas_call(
        body, name=name, grid=(seq // tm, nk),
        in_specs=[pl.BlockSpec((tm, tk), lambda i, kk: (i, kk)),
                  pl.BlockSpec((tm, tk), lambda i, kk: (i, kk)),
                  pl.BlockSpec((tk, D), lambda i, kk: (blk * nk + kk, 0)),
                  pl.BlockSpec((tk, D), lambda i, kk: ((blk + 1) * nk + kk, 0)),
                  tok, tok, tok,
                  pl.BlockSpec((1, D), lambda i, kk: (0, 0)),
                  pl.BlockSpec((9, D), lambda i, kk: (0, 0))],
        out_specs=[tok, pl.BlockSpec((8, D), lambda i, kk: (0, 0))],
        out_shape=[jax.ShapeDtypeStruct((seq, D), F32), jax.ShapeDtypeStruct((8, D), F32)],
        scratch_shapes=[pltpu.VMEM((tm, D), F32)],
        compiler_params=_cp(("arbitrary", "arbitrary")))(dg, du, w, w, x, dxo, fb, wn, mod)


def _ffn_forward(x, w, blk, wn, mod, k, tm, tag):
    h = _prenorm(x, wn, mod, k, tm, f"{tag}_prenorm")
    g, u, a = _ffn_up(h, w, blk, tm, FF // 2, f"{tag}_up")
    xo, fb = _ffn_down(a, w, blk, x, mod, 3 * k + 2, tm, FF // 2, f"{tag}_down")
    return xo, (x, h, g, u, a, fb)


def _ffn_backward(dxo, saved, w, blk, wn, mod, k, tm, tag):
    x, h, g, u, a, fb = saved
    df = _scale_cast(dxo, mod, 3 * k + 2, FFN_RES, tm, f"{tag}_dbranch")
    dg, du = _ffn_bwd_da(df, w, blk, g, u, tm, FF // 2, f"{tag}_bwd_da")
    dx, stats = _ffn_bwd_dh(dg, du, w, blk, x, dxo, fb, wn, mod, k, tm, FF // 2, f"{tag}_bwd_dh")
    seq = x.shape[0]
    tk = min(seq, 512)
    d_gate_t = _mm_tn(dg, h, FF // 2, tk, f"{tag}_dw_gate")
    d_up_t = _mm_tn(du, h, FF // 2, tk, f"{tag}_dw_up")
    d_down = _mm_tn(a, df, FF // 2, tk, f"{tag}_dw_down")
    return dx, stats, (d_gate_t, d_up_t, d_down)


def _prev_rows(tm, col):
    return pl.BlockSpec((HALO, 1024), lambda i, j: (jnp.maximum(i * (tm // HALO) - 1, 0), col + j))


def _conv_pre(ext, cw, cb, rows):
    pre = cb + cw[3:4, :] * ext
    for s in (1, 2, 3):
        pre = pre + cw[3 - s:4 - s, :] * pltpu.roll(ext, s, 0)
    return pre[HALO:HALO + rows]


def _conv_fwd(proj, cw, cb, tm, name):
    seq = proj.shape[0]

    def body(x_ref, p_ref, cw_ref, cb_ref, o_ref):
        prev = jnp.where(pl.program_id(0) == 0, 0.0, p_ref[...])
        ext = jnp.concatenate([prev, x_ref[...]], axis=0)
        pre = _conv_pre(ext, cw_ref[...], cb_ref[...], tm)
        o_ref[...] = pre * _sigmoid(pre)

    c0 = COL_XBC // 1024
    return pl.pallas_call(
        body, name=name, grid=(seq // tm, 2),
        in_specs=[pl.BlockSpec((tm, 1024), lambda i, j: (i, c0 + j)),
                  _prev_rows(tm, c0),
                  pl.BlockSpec((4, 1024), lambda i, j: (0, j)),
                  pl.BlockSpec((1, 1024), lambda i, j: (0, j))],
        out_specs=pl.BlockSpec((tm, 1024), lambda i, j: (i, j)),
        out_shape=jax.ShapeDtypeStruct((seq, D_XBC), F32),
        compiler_params=_cp(("parallel", "parallel")))(proj, proj, cw, cb)


def _conv_bwd(dact, proj, cw, cb, tm, name):
    seq = proj.shape[0]
    ni = seq // tm

    def body(d_ref, dn_ref, x_ref, p_ref, n_ref, cw_ref, cb_ref, o_ref, st_ref):
        i = pl.program_id(1)
        cwv = cw_ref[...]
        prev = jnp.where(i == 0, 0.0, p_ref[...])
        ext = jnp.concatenate([prev, x_ref[...], n_ref[...]], axis=0)
        pre = _conv_pre(ext, cwv, cb_ref[...], tm + HALO)
        dnext = jnp.where(i == ni - 1, 0.0, dn_ref[...])
        dext = jnp.concatenate([d_ref[...], dnext], axis=0)
        dpre = dext * _silu_grad(pre, _sigmoid(pre))
        n = tm + HALO
        dx = cwv[3:4, :] * dpre
        for s in (1, 2, 3):
            dx = dx + cwv[3 - s:4 - s, :] * pltpu.roll(dpre, n - s, 0)
        o_ref[...] = dx[:tm].astype(BF16)
        dcur = dpre[:tm]
        rows = [jnp.sum(dcur * pltpu.roll(ext, 3 - k, 0)[HALO:HALO + tm], axis=0, keepdims=True) for k in range(3)]
        rows.append(jnp.sum(dcur * ext[HALO:HALO + tm], axis=0, keepdims=True))
        rows.append(jnp.sum(dcur, axis=0, keepdims=True))
        rows.append(jnp.zeros((3, 1024), F32))
        rows = jnp.concatenate(rows, axis=0)

        @pl.when(i == 0)
        def _():
            st_ref[...] = rows

        @pl.when(i != 0)
        def _():
            st_ref[...] += rows

    c0 = COL_XBC // 1024
    return pl.pallas_call(
        body, name=name, grid=(2, ni),
        in_specs=[pl.BlockSpec((tm, 1024), lambda j, i: (i, j)),
                  pl.BlockSpec((HALO, 1024), lambda j, i: (jnp.minimum((i + 1) * (tm // HALO), seq // HALO - 1), j)),
                  pl.BlockSpec((tm, 1024), lambda j, i: (i, c0 + j)),
                  pl.BlockSpec((HALO, 1024), lambda j, i: (jnp.maximum(i * (tm // HALO) - 1, 0), c0 + j)),
                  pl.BlockSpec((HALO, 1024), lambda j, i: (jnp.minimum((i + 1) * (tm // HALO), seq // HALO - 1), c0 + j)),
                  pl.BlockSpec((4, 1024), lambda j, i: (0, j)),
                  pl.BlockSpec((1, 1024), lambda j, i: (0, j))],
        out_specs=[pl.BlockSpec((tm, 1024), lambda j, i: (i, j)),
                   pl.BlockSpec((8, 1024), lambda j, i: (0, j))],
        out_shape=[jax.ShapeDtypeStruct((seq, D_XBC), BF16), jax.ShapeDtypeStruct((8, D_XBC), F32)],
        compiler_params=_cp(("parallel", "arbitrary")))(dact, dact, proj, proj, proj, cw, cb)


def _head_expand():
    r = lax.broadcasted_iota(jnp.int32, (LANE, D_SSD), 0)
    c = lax.broadcasted_iota(jnp.int32, (LANE, D_SSD), 1)
    return (c // HEAD_DIM == r).astype(F32)


def _head_reduce():
    r = lax.broadcasted_iota(jnp.int32, (D_SSD, LANE), 0)
    c = lax.broadcasted_iota(jnp.int32, (D_SSD, LANE), 1)
    return (r // HEAD_DIM == c).astype(F32)


def _ssd_common(dtr, par):
    q = CHUNK
    v = dtr + par[0:1, :]
    dt = jnp.maximum(v, 0.0) + jnp.log(1.0 + jnp.exp(-jnp.abs(v)))
    a = -jnp.exp(par[1:2, :])
    adt = dt * a
    li = lax.broadcasted_iota(jnp.int32, (q, q), 0)
    si = lax.broadcasted_iota(jnp.int32, (q, q), 1)
    causal = li >= si
    acs = _exact_dot(causal.astype(F32), adt)
    expand = _head_expand()
    dt_l = _exact_dot(dt, expand)
    acs_l = _exact_dot(acs, expand)
    par_l = _exact_dot(par, expand)
    last_l = acs_l[q - 1:q, :]
    return dict(v=v, dt=dt, a=a, acs=acs, acs_t=acs.T, causal=causal, dt_l=dt_l, acs_l=acs_l,
                ea_l=jnp.exp(acs_l), ds_l=jnp.exp(last_l - acs_l), cd_l=jnp.exp(last_l), dskip_l=par_l[2:3, :])


def _decay(cm, h):
    seg = cm["acs"][:, h:h + 1] - cm["acs_t"][h:h + 1, :]
    return jnp.exp(jnp.where(cm["causal"], seg, -jnp.inf))


def _lane_mask(r):
    lane = lax.broadcasted_iota(jnp.int32, (1, GROUP_W), 1)
    return lane // HEAD_DIM == r


def _ssd_fwd(xbc, proj, par, name):
    seq = xbc.shape[0]
    nc = seq // CHUNK
    q = CHUNK

    def body(x_ref, dt_ref, par_ref, y_ref, hp_ref, state):
        @pl.when(pl.program_id(0) == 0)
        def _():
            state[...] = jnp.zeros_like(state)

        cm = _ssd_common(dt_ref[...], par_ref[...])
        for g in range(N_GROUPS):
            lo = g * GROUP_W
            xs = x_ref[:, lo:lo + GROUP_W]
            bm = x_ref[:, D_SSD + g * N_STATE:D_SSD + (g + 1) * N_STATE].astype(BF16)
            cmat = x_ref[:, D_SSD + N_GROUPS * N_STATE + g * N_STATE:D_SSD + N_GROUPS * N_STATE + (g + 1) * N_STATE].astype(BF16)
            xdt = xs * cm["dt_l"][:, lo:lo + GROUP_W]
            xdt_b = xdt.astype(BF16)
            cb = _dot(cmat, bm, 1, 1)
            yd = jnp.zeros((q, GROUP_W), F32)
            for r in range(4):
                s_h = (cb * _decay(cm, 4 * g + r)).astype(BF16)
                yd = jnp.where(_lane_mask(r), _dot(s_h, xdt_b, 1, 0), yd)
            hg = state[g]
            hp_ref[0, g] = hg
            yo = _dot(cmat, hg.astype(BF16), 1, 0) * cm["ea_l"][:, lo:lo + GROUP_W]
            y_ref[:, lo:lo + GROUP_W] = yd + yo + cm["dskip_l"][:, lo:lo + GROUP_W] * xs
            xds = (xdt * cm["ds_l"][:, lo:lo + GROUP_W]).astype(BF16)
            state[g] = hg * cm["cd_l"][:, lo:lo + GROUP_W] + _dot(bm, xds, 0, 0)

    return pl.pallas_call(
        body, name=name, grid=(nc,),
        in_specs=[pl.BlockSpec((q, D_XBC), lambda c: (c, 0)),
                  pl.BlockSpec((q, LANE), lambda c: (c, COL_DT // LANE)),
                  pl.BlockSpec((8, LANE), lambda c: (0, 0))],
        out_specs=[pl.BlockSpec((q, D_SSD), lambda c: (c, 0)),
                   pl.BlockSpec((1, N_GROUPS, N_STATE, GROUP_W), lambda c: (c, 0, 0, 0))],
        out_shape=[jax.ShapeDtypeStruct((seq, D_SSD), F32),
                   jax.ShapeDtypeStruct((nc, N_GROUPS, N_STATE, GROUP_W), F32)],
        scratch_shapes=[pltpu.VMEM((N_GROUPS, N_STATE, GROUP_W), F32)],
        compiler_params=_cp(("arbitrary",)))(xbc, proj, par)


def _ssd_bwd(dy, xbc, proj, par, hprev, name):
    seq = xbc.shape[0]
    nc = seq // CHUNK
    q = CHUNK

    def body(dy_ref, x_ref, dt_ref, par_ref, hp_ref, dx_ref, ddt_ref, st_ref, dstate):
        step = pl.program_id(0)

        @pl.when(step == 0)
        def _():
            dstate[...] = jnp.zeros_like(dstate)

        par = par_ref[...]
        cm = _ssd_common(dt_ref[...], par)
        reduce = _head_reduce()
        lane128 = lax.broadcasted_iota(jnp.int32, (1, LANE), 1)
        row128 = lax.broadcasted_iota(jnp.int32, (LANE, 1), 0)
        d_acs = jnp.zeros((q, LANE), F32)
        d_acs_t = jnp.zeros((LANE, q), F32)
        last_terms = []
        acs_terms = []
        dxdt_all = []
        for g in range(N_GROUPS):
            lo = g * GROUP_W
            sl = slice(lo, lo + GROUP_W)
            xs = x_ref[:, sl]
            bm32 = x_ref[:, D_SSD + g * N_STATE:D_SSD + (g + 1) * N_STATE]
            cm32 = x_ref[:, D_SSD + N_GROUPS * N_STATE + g * N_STATE:D_SSD + N_GROUPS * N_STATE + (g + 1) * N_STATE]
            bm = bm32.astype(BF16)
            cmat = cm32.astype(BF16)
            dyg = dy_ref[:, sl]
            dyg_b = dyg.astype(BF16)
            xdt = xs * cm["dt_l"][:, sl]
            xdt_b = xdt.astype(BF16)
            hg = hp_ref[0, g]
            hg_b = hg.astype(BF16)
            dhg = dstate[g]
            dhg_b = dhg.astype(BF16)
            ea = cm["ea_l"][:, sl]
            ds = cm["ds_l"][:, sl]
            cd = cm["cd_l"][:, sl]
            yoff = _dot(cmat, hg_b, 1, 0) * ea
            dw = (dyg * ea).astype(BF16)
            d_c = _dot(dw, hg_b, 1, 1)
            d_hprev = _dot(cmat, dw, 0, 0) + dhg * cd
            t_acs = dyg * yoff
            d_last_g = jnp.sum(dhg * hg, axis=0, keepdims=True) * cd
            xds_b = (xdt * ds).astype(BF16)
            dxds = _dot(bm, dhg_b, 1, 0)
            d_b = _dot(xds_b, dhg_b, 1, 1)
            dxdt = dxds * ds
            t_ds = dxds * xdt * ds
            t_acs = t_acs - t_ds
            d_last_g = d_last_g + jnp.sum(t_ds, axis=0, keepdims=True)
            cb = _dot(cmat, bm, 1, 1)
            d_cb = jnp.zeros((q, q), F32)
            for r in range(4):
                h = 4 * g + r
                dec = _decay(cm, h)
                s_h = cb * dec
                mask = _lane_mask(r)
                d_s = _dot(jnp.where(mask, dyg, 0.0).astype(BF16), xdt_b, 1, 1)
                dxdt = dxdt + jnp.where(mask, _dot(s_h.astype(BF16), dyg_b, 0, 0), 0.0)
                d_cb = d_cb + d_s * dec
                d_m = d_s * s_h
                d_acs = d_acs + jnp.where(lane128 == h, jnp.sum(d_m, axis=1, keepdims=True), 0.0)
                d_acs_t = d_acs_t + jnp.where(row128 == h, jnp.sum(d_m, axis=0, keepdims=True), 0.0)
            d_cb_b = d_cb.astype(BF16)
            d_c = d_c + _dot(d_cb_b, bm, 1, 0)
            d_b = d_b + _dot(d_cb_b, cmat, 0, 0)
            dstate[g] = d_hprev
            dx_ref[:, sl] = dxdt * cm["dt_l"][:, sl] + cm["dskip_l"][:, sl] * dyg
            dx_ref[:, D_SSD + g * N_STATE:D_SSD + (g + 1) * N_STATE] = d_b
            dx_ref[:, D_SSD + N_GROUPS * N_STATE + g * N_STATE:D_SSD + N_GROUPS * N_STATE + (g + 1) * N_STATE] = d_c
            acs_terms.append(t_acs)
            dxdt_all.append(dxdt * xs)
            last_terms.append(d_last_g)
        t_acs_l = jnp.concatenate(acs_terms, axis=1)
        d_dt_l = jnp.concatenate(dxdt_all, axis=1)
        d_last_l = jnp.concatenate(last_terms, axis=1)
        d_acs = d_acs + _exact_dot(t_acs_l, reduce) - d_acs_t.T
        last_row = lax.broadcasted_iota(jnp.int32, (q, 1), 0) == q - 1
        d_acs = d_acs + jnp.where(last_row, _exact_dot(jnp.broadcast_to(d_last_l, (8, D_SSD)), reduce)[0:1, :], 0.0)
        li = lax.broadcasted_iota(jnp.int32, (q, q), 0)
        si = lax.broadcasted_iota(jnp.int32, (q, q), 1)
        d_adt = _exact_dot((si >= li).astype(F32), d_acs)
        d_dt = _exact_dot(d_dt_l, reduce) + d_adt * cm["a"]
        d_dtr = d_dt * _sigmoid(cm["v"])
        ddt_ref[...] = d_dtr.astype(BF16)
        d_skip = _exact_dot(jnp.broadcast_to(jnp.sum(dy_ref[...] * x_ref[:, 0:D_SSD], axis=0, keepdims=True), (8, D_SSD)), reduce)[0:1, :]
        rows = jnp.concatenate([
            jnp.sum(d_dtr, axis=0, keepdims=True),
            jnp.sum(d_adt * cm["dt"], axis=0, keepdims=True) * cm["a"],
            d_skip,
            jnp.zeros((5, LANE), F32)], axis=0)

        @pl.when(step == 0)
        def _():
            st_ref[...] = rows

        @pl.when(step != 0)
        def _():
            st_ref[...] += rows

    rev = lambda c: nc - 1 - c
    return pl.pallas_call(
        body, name=name, grid=(nc,),
        in_specs=[pl.BlockSpec((q, D_SSD), lambda c: (rev(c), 0)),
                  pl.BlockSpec((q, D_XBC), lambda c: (rev(c), 0)),
                  pl.BlockSpec((q, LANE), lambda c: (rev(c), COL_DT // LANE)),
                  pl.BlockSpec((8, LANE), lambda c: (0, 0)),
                  pl.BlockSpec((1, N_GROUPS, N_STATE, GROUP_W), lambda c: (rev(c), 0, 0, 0))],
        out_specs=[pl.BlockSpec((q, D_XBC), lambda c: (rev(c), 0)),
                   pl.BlockSpec((q, LANE), lambda c: (rev(c), 0)),
                   pl.BlockSpec((8, LANE), lambda c: (0, 0))],
        out_shape=[jax.ShapeDtypeStruct((seq, D_XBC), F32),
                   jax.ShapeDtypeStruct((seq, LANE), BF16),
                   jax.ShapeDtypeStruct((8, LANE), F32)],
        scratch_shapes=[pltpu.VMEM((N_GROUPS, N_STATE, GROUP_W), F32)],
        compiler_params=_cp(("arbitrary",)))(dy, xbc, proj, par, hprev)


def _gate_norm_fwd(y, proj, wn, tm, name):
    seq = y.shape[0]

    def body(y_ref, z_ref, w_ref, o_ref):
        for g in range(N_GROUPS):
            sl = slice(g * GROUP_W, (g + 1) * GROUP_W)
            zv = z_ref[:, sl]
            yz = y_ref[:, sl] * (zv * _sigmoid(zv))
            r = lax.rsqrt(jnp.mean(yz * yz, axis=-1, keepdims=True) + EPS)
            o_ref[:, sl] = (yz * r * w_ref[:, sl]).astype(BF16)

    tok = pl.BlockSpec((tm, D_SSD), lambda i: (i, 0))
    return pl.pallas_call(
        body, name=name, grid=(seq // tm,),
        in_specs=[tok, tok, pl.BlockSpec((1, D_SSD), lambda i: (0, 0))],
        out_specs=tok,
        out_shape=jax.ShapeDtypeStruct((seq, D_SSD), BF16),
        compiler_params=_cp(("parallel",)))(y, proj, wn)


def _gate_norm_bwd(dys, y, proj, wn, tm, name):
    seq = y.shape[0]

    def body(d_ref, y_ref, z_ref, w_ref, dy_ref, dz_ref, st_ref):
        rows = []
        for g in range(N_GROUPS):
            sl = slice(g * GROUP_W, (g + 1) * GROUP_W)
            zv = z_ref[:, sl]
            yv = y_ref[:, sl]
            sg = _sigmoid(zv)
            sz = zv * sg
            yz = yv * sz
            r = lax.rsqrt(jnp.mean(yz * yz, axis=-1, keepdims=True) + EPS)
            yn = yz * r
            dv = d_ref[:, sl]
            dyn = dv * w_ref[:, sl]
            dyz = r * (dyn - yn * jnp.mean(dyn * yn, axis=-1, keepdims=True))
            dy_ref[:, sl] = dyz * sz
            dz_ref[:, sl] = (dyz * yv * _silu_grad(zv, sg)).astype(BF16)
            rows.append(jnp.sum(dv * yn, axis=0, keepdims=True))
        rows = jnp.concatenate([jnp.concatenate(rows, axis=1), jnp.zeros((7, D_SSD), F32)], axis=0)

        @pl.when(pl.program_id(0) == 0)
        def _():
            st_ref[...] = rows

        @pl.when(pl.program_id(0) != 0)
        def _():
            st_ref[...] += rows

    tok = pl.BlockSpec((tm, D_SSD), lambda i: (i, 0))
    return pl.pallas_call(
        body, name=name, grid=(seq // tm,),
        in_specs=[tok, tok, tok, pl.BlockSpec((1, D_SSD), lambda i: (0, 0))],
        out_specs=[tok, tok, pl.BlockSpec((8, D_SSD), lambda i: (0, 0))],
        out_shape=[jax.ShapeDtypeStruct((seq, D_SSD), F32), jax.ShapeDtypeStruct((seq, D_SSD), BF16),
                   jax.ShapeDtypeStruct((8, D_SSD), F32)],
        compiler_params=_cp(("arbitrary",)))(dys, y, proj, wn)


def _pool_counts(t0, rows, w):
    pos = (t0 + 1 + lax.broadcasted_iota(jnp.int32, (rows, 1), 0)).astype(F32)
    return jnp.minimum(pos, float(w))


def _window_means(ext, t0):
    n = ext.shape[0]
    outs = []
    run = ext
    width = 1
    sums = {}
    while width < 16:
        run = run + pltpu.roll(run, width, 0)
        width *= 2
        sums[width] = run
    for g, w in enumerate(POOL_WINDOWS):
        sl = slice(g * POOL_GW, (g + 1) * POOL_GW)
        cnt = _pool_counts(t0, n - HALO, w)
        outs.append(sums[w][HALO:, sl] / cnt - ext[HALO:, sl])
    return outs


def _pool_fwd(proj, pw, pb, ps, tm, name):
    seq = proj.shape[0]

    def body(u_ref, p_ref, pw_ref, pb_ref, ps_ref, o_ref):
        i = pl.program_id(0)
        prev = jnp.where(i == 0, 0.0, p_ref[...])
        ext = jnp.concatenate([prev, u_ref[...]], axis=0)
        diffs = _window_means(ext, i * tm)
        for g in range(4):
            sl = slice(g * POOL_GW, (g + 1) * POOL_GW)
            out = _dot(diffs[g].astype(BF16), pw_ref[g], 1, 0) + pb_ref[:, sl]
            o_ref[:, sl] = (out * ps_ref[:, sl]).astype(BF16)

    c0 = COL_U // 1024
    vec = pl.BlockSpec((1, D_POOL), lambda i: (0, 0))
    return pl.pallas_call(
        body, name=name, grid=(seq // tm,),
        in_specs=[pl.BlockSpec((tm, 1024), lambda i: (i, c0)),
                  pl.BlockSpec((HALO, 1024), lambda i: (jnp.maximum(i * (tm // HALO) - 1, 0), c0)),
                  pl.BlockSpec((4, POOL_GW, POOL_GW), lambda i: (0, 0, 0)), vec, vec],
        out_specs=pl.BlockSpec((tm, D_POOL), lambda i: (i, 0)),
        out_shape=jax.ShapeDtypeStruct((seq, D_POOL), BF16),
        compiler_params=_cp(("parallel",)))(proj, proj, pw, pb, ps)


def _pool_bwd(dyp, proj, pw, pb, ps, tm, name):
    seq = proj.shape[0]
    ni = seq // tm

    def body(d_ref, dn_ref, u_ref, p_ref, pw_ref, pb_ref, ps_ref, du_ref, dw_ref, st_ref):
        i = pl.program_id(0)
        prev = jnp.where(i == 0, 0.0, p_ref[...])
        ext = jnp.concatenate([prev, u_ref[...]], axis=0)
        diffs = _window_means(ext, i * tm)
        dnext = jnp.where(i == ni - 1, 0.0, dn_ref[...])
        dext = jnp.concatenate([d_ref[...], dnext], axis=0)
        n = tm + HALO
        b_rows, s_rows = [], []
        for g, w in enumerate(POOL_WINDOWS):
            sl = slice(g * POOL_GW, (g + 1) * POOL_GW)
            wg = pw_ref[g]
            dout = dext[:, sl] * ps_ref[:, sl]
            dcur = dout[:tm]
            pre = _dot(diffs[g].astype(BF16), wg, 1, 0) + pb_ref[:, sl]
            s_rows.append(jnp.sum(d_ref[:, sl] * pre, axis=0, keepdims=True))
            b_rows.append(jnp.sum(dcur, axis=0, keepdims=True))
            dwg = _dot(diffs[g].astype(BF16), dcur.astype(BF16), 0, 0)

            @pl.when(i == 0)
            def _():
                dw_ref[g] = dwg

            @pl.when(i != 0)
            def _():
                dw_ref[g] += dwg

            ddiff = _dot(dout.astype(BF16), wg, 1, 1)
            scaled = ddiff / _pool_counts(i * tm, n, w)
            run = scaled
            width = 1
            while width < w:
                run = run + pltpu.roll(run, n - width, 0)
                width *= 2
            du_ref[:, sl] = (run[:tm] - ddiff[:tm]).astype(BF16)
        rows = jnp.concatenate([jnp.concatenate(b_rows, axis=1), jnp.concatenate(s_rows, axis=1),
                                jnp.zeros((6, D_POOL), F32)], axis=0)

        @pl.when(i == 0)
        def _():
            st_ref[...] = rows

        @pl.when(i != 0)
        def _():
            st_ref[...] += rows

    c0 = COL_U // 1024
    vec = pl.BlockSpec((1, D_POOL), lambda i: (0, 0))
    last = seq // HALO - 1
    return pl.pallas_call(
        body, name=name, grid=(ni,),
        in_specs=[pl.BlockSpec((tm, D_POOL), lambda i: (i, 0)),
                  pl.BlockSpec((HALO, D_POOL), lambda i: (jnp.minimum((i + 1) * (tm // HALO), last), 0)),
                  pl.BlockSpec((tm, 1024), lambda i: (i, c0)),
                  pl.BlockSpec((HALO, 1024), lambda i: (jnp.maximum(i * (tm // HALO) - 1, 0), c0)),
                  pl.BlockSpec((4, POOL_GW, POOL_GW), lambda i: (0, 0, 0)), vec, vec],
        out_specs=[pl.BlockSpec((tm, D_POOL), lambda i: (i, 0)),
                   pl.BlockSpec((4, POOL_GW, POOL_GW), lambda i: (0, 0, 0)),
                   pl.BlockSpec((8, D_POOL), lambda i: (0, 0))],
        out_shape=[jax.ShapeDtypeStruct((seq, D_POOL), BF16),
                   jax.ShapeDtypeStruct((4, POOL_GW, POOL_GW), F32),
                   jax.ShapeDtypeStruct((8, D_POOL), F32)],
        compiler_params=_cp(("arbitrary",)))(dyp, dyp, proj, proj, pw, pb, ps)


def _mix_out(ys, yp, wout, x1, mod, tm, name):
    seq = ys.shape[0]

    def body(ys_ref, yp_ref, w_ref, x_ref, mod_ref, xo_ref, m_ref):
        mix = _dot(ys_ref[...], w_ref[0:D_SSD, :], 1, 0) + _dot(yp_ref[...], w_ref[D_SSD:2 * D_SSD, :], 1, 0)
        m_ref[...] = mix.astype(BF16)
        xo_ref[...] = x_ref[...] + mod_ref[5:6, :] * mix

    tok = pl.BlockSpec((tm, D), lambda i: (i, 0))
    return pl.pallas_call(
        body, name=name, grid=(seq // tm,),
        in_specs=[tok, tok, pl.BlockSpec((2 * D_SSD, D), lambda i: (0, 0)), tok,
                  pl.BlockSpec((9, D), lambda i: (0, 0))],
        out_specs=[tok, tok],
        out_shape=[jax.ShapeDtypeStruct((seq, D), F32), jax.ShapeDtypeStruct((seq, D), BF16)],
        compiler_params=_cp(("parallel",)))(ys, yp, wout, x1, mod)


def _mix_bwd_dh(dz, dxbc, du, ddt, win_t, x1, dx2, mixb, wn, mod, tm, name):
    seq = x1.shape[0]

    def body(dz_ref, dx_ref, du_ref, ddt_ref, w_ref, x_ref, dxo_ref, m_ref, wn_ref, mod_ref, o_ref, st_ref):
        dh = (_dot(dz_ref[...], w_ref[COL_Z:COL_Z + 1024, :], 1, 0)
              + _dot(dx_ref[...], w_ref[COL_XBC:COL_XBC + D_XBC, :], 1, 0)
              + _dot(du_ref[...], w_ref[COL_U:COL_U + 1024, :], 1, 0)
              + _dot(ddt_ref[...], w_ref[COL_DT:COL_DT + LANE, :], 1, 0))
        o_ref[...] = _norm_bwd(dh, x_ref[...], dxo_ref[...], m_ref[...].astype(F32), wn_ref[...],
                               mod_ref[4:5, :], 1.0, st_ref, pl.program_id(0) == 0)

    tok = pl.BlockSpec((tm, D), lambda i: (i, 0))
    return pl.pallas_call(
        body, name=name, grid=(seq // tm,),
        in_specs=[tok, pl.BlockSpec((tm, D_XBC), lambda i: (i, 0)), tok,
                  pl.BlockSpec((tm, LANE), lambda i: (i, 0)),
                  pl.BlockSpec((D_IN_PAD, D), lambda i: (0, 0)),
                  tok, tok, tok,
                  pl.BlockSpec((1, D), lambda i: (0, 0)),
                  pl.BlockSpec((9, D), lambda i: (0, 0))],
        out_specs=[tok, pl.BlockSpec((8, D), lambda i: (0, 0))],
        out_shape=[jax.ShapeDtypeStruct((seq, D), F32), jax.ShapeDtypeStruct((8, D), F32)],
        compiler_params=_cp(("arbitrary",)))(dz, dxbc, du, ddt, win_t, x1, dx2, mixb, wn, mod)


def _mix_bwd_dycat(dmix, wout, tm, name):
    seq = dmix.shape[0]

    def body(d_ref, w_ref, a_ref, b_ref):
        dv = d_ref[...]
        a_ref[...] = _dot(dv, w_ref[0:D_SSD, :], 1, 1)
        b_ref[...] = _dot(dv, w_ref[D_SSD:2 * D_SSD, :], 1, 1)

    tok = pl.BlockSpec((tm, D), lambda i: (i, 0))
    return pl.pallas_call(
        body, name=name, grid=(seq // tm,),
        in_specs=[tok, pl.BlockSpec((2 * D_SSD, D), lambda i: (0, 0))],
        out_specs=[tok, tok],
        out_shape=[jax.ShapeDtypeStruct((seq, D), F32)] * 2,
        compiler_params=_cp(("parallel",)))(dmix, wout)


def _local_step(x, tgt, mod, wff1, wff2, win_t, wout, pool_w, vecs, tm):
    seq = x.shape[0]
    tk = min(seq, 512)
    x1, s1 = _ffn_forward(x, wff1, 0, vecs["ffn1_norm"], mod, 0, tm, "ffn1")
    h2 = _prenorm(x1, vecs["mix_norm"], mod, 1, tm, "mix_prenorm")
    proj = _mm_nt(h2, win_t, tm, D_IN_PAD // 3, F32, "mix_in_proj")
    xbc = _conv_fwd(proj, vecs["conv_w"], vecs["conv_b"], tm, "mix_conv")
    y, hprev = _ssd_fwd(xbc, proj, vecs["ssd_par"], "mix_ssd")
    ys = _gate_norm_fwd(y, proj, vecs["ssd_norm_w"], tm, "mix_gate_norm")
    yp = _pool_fwd(proj, pool_w, vecs["pool_b"], vecs["pool_scale"], tm, "mix_pool")
    x2, mixb = _mix_out(ys, yp, wout, x1, mod, tm, "mix_out_proj")
    x3, s3 = _ffn_forward(x2, wff2, 0, vecs["ffn2_norm"], mod, 2, tm, "ffn2")
    dx3, st_loss = _loss_head(x3, vecs["final_norm"], tgt, tm, "loss_head")

    dx2, st3, dw3 = _ffn_backward(dx3, s3, wff2, 0, vecs["ffn2_norm"], mod, 2, tm, "ffn2")
    dmix = _scale_cast(dx2, mod, 5, 1.0, tm, "mix_dbranch")
    dys, dyp = _mix_bwd_dycat(dmix, wout, tm, "mix_bwd_dycat")
    d_wout = (_mm_tn(ys, dmix, D_SSD, tk, "mix_dw_out_ssd"), _mm_tn(yp, dmix, D_POOL, tk, "mix_dw_out_pool"))
    du, d_pool_w, st_pool = _pool_bwd(dyp, proj, pool_w, vecs["pool_b"], vecs["pool_scale"], tm, "mix_pool_bwd")
    dy, dz, st_gn = _gate_norm_bwd(dys, y, proj, vecs["ssd_norm_w"], tm, "mix_gate_norm_bwd")
    dxbc_act, ddt, st_ssd = _ssd_bwd(dy, xbc, proj, vecs["ssd_par"], hprev, "mix_ssd_bwd")
    dxbc, st_conv = _conv_bwd(dxbc_act, proj, vecs["conv_w"], vecs["conv_b"], tm, "mix_conv_bwd")
    dx1, st2 = _mix_bwd_dh(dz, dxbc, du, ddt, win_t, x1, dx2, mixb, vecs["mix_norm"], mod, min(tm, 256), "mix_bwd_dh")
    d_win = (_mm_tn(dz, h2, 1024, tk, "mix_dw_in_z"), _mm_tn(dxbc, h2, 1024, tk, "mix_dw_in_xbc"),
             _mm_tn(du, h2, 1024, tk, "mix_dw_in_u"), _mm_tn(ddt, h2, LANE, tk, "mix_dw_in_dt"))
    dx0, st1, dw1 = _ffn_backward(dx1, s1, wff1, 0, vecs["ffn1_norm"], mod, 0, tm, "ffn1")
    stats = dict(ffn1=st1, mix=st2, ffn2=st3, loss=st_loss, pool=st_pool, gn=st_gn, ssd=st_ssd, conv=st_conv)
    return dx0, stats, dw1, dw3, d_win, d_wout, d_pool_w


HBM_SPEC = pl.BlockSpec(memory_space=pltpu.HBM)


def _mesh_pos():
    return lax.axis_index("x"), lax.axis_index("y"), lax.axis_index("c")


def _other_chips(x, y):
    return [(1 - x, y), (x, 1 - y), (1 - x, 1 - y)]


def _all_gather(src, regions, name):
    total, cols = src.shape
    assert sum(r for _, r in regions) == total
    body = _all_gather_body(regions, total, False)
    return pl.pallas_call(
        body, name=name,
        out_shape=jax.ShapeDtypeStruct((N_DEV * total, cols), src.dtype),
        in_specs=[HBM_SPEC], out_specs=HBM_SPEC,
        scratch_shapes=[pltpu.SemaphoreType.DMA((7,)), pltpu.SemaphoreType.DMA((7,)), pltpu.SemaphoreType.DMA],
    )(src)


def _all_gather_async(src, regions, name, collective_id):
    total, cols = src.shape
    assert sum(r for _, r in regions) == total
    src_ref = jax.new_ref(src, memory_space=pltpu.MemorySpace.HBM)
    out_ref = jax.empty_ref(jax.ShapeDtypeStruct((N_DEV * total, cols), src.dtype), memory_space=pltpu.MemorySpace.HBM)
    body = _all_gather_body(regions, total, True)

    @pl.kernel(mesh=plsc.ScalarSubcoreMesh(axis_name="seq", num_cores=1), name=name,
               scratch_types=(pltpu.SemaphoreType.DMA((7,)), pltpu.SemaphoreType.DMA((7,)), pltpu.SemaphoreType.DMA),
               compiler_params=pltpu.CompilerParams(collective_id=collective_id))
    def launch(send_sems, recv_sems, local_sem):
        body(src_ref, out_ref, send_sems, recv_sems, local_sem)

    launch()
    return out_ref[...]


def _all_gather_body(regions, total, handshake):
    def body(src_ref, out_ref, send_sems, recv_sems, local_sem):
        x, y, c = _mesh_pos()
        me, sibling = (x, y, c), (x, y, 1 - c)
        chips = _other_chips(x, y)
        if handshake:
            barrier = pltpu.get_barrier_semaphore()
            for peer in [sibling] + [(*chip, c) for chip in chips]:
                pl.semaphore_signal(barrier, inc=1, device_id=peer, device_id_type=MESH)
            pl.semaphore_wait(barrier, 4)

        def rows_of(dev, off, rows):
            start = pl.multiple_of(N_DEV * off + (4 * dev[0] + 2 * dev[1] + dev[2]) * rows, 8)
            return out_ref.at[pl.ds(start, rows), :]

        def copies(k, block, to, from_src):
            out = []
            for off, rows in regions:
                dst = rows_of(block, off, rows)
                out.append(pltpu.make_async_remote_copy(
                    src_ref=src_ref.at[pl.ds(off, rows), :] if from_src else dst, dst_ref=dst,
                    send_sem=send_sems.at[k], recv_sem=recv_sems.at[k], device_id=to, device_id_type=MESH))
            return out

        def drain(k):
            whole = out_ref.at[pl.ds(0, total), :]
            return pltpu.make_async_remote_copy(src_ref=whole, dst_ref=whole, send_sem=send_sems.at[k],
                                                recv_sem=recv_sems.at[k], device_id=me, device_id_type=MESH)

        for off, rows in regions:
            pltpu.make_async_copy(src_ref.at[pl.ds(off, rows), :], rows_of(me, off, rows), local_sem).start()
        first = copies(0, me, sibling, True)
        for j, chip in enumerate(chips):
            first += copies(1 + j, me, (*chip, c), True)
        for cp in first:
            cp.start()
        for j, chip in enumerate(chips):
            drain(1 + j).wait_recv()
            for cp in copies(4 + j, (*chip, c), sibling, False):
                cp.start()
        drain(0).wait_recv()
        for j in range(3):
            drain(4 + j).wait_recv()
        for k in range(7):
            drain(k).wait_send()
        pltpu.make_async_copy(src_ref, out_ref.at[pl.ds(0, total), :], local_sem).wait()

    return body


def _rs_pair(grads, total, name):
    cols = grads[0][0].shape[1]
    n = len(grads)
    sent = sum(rows for _, _, rows in grads)

    def body(*refs):
        g_refs, recv_ref, send_sem, recv_sem = refs[:n], refs[n], refs[n + 1], refs[n + 2]
        x, y, c = _mesh_pos()
        sibling = (x, y, 1 - c)
        for q in range(4):
            for g_ref, (_, off, rows) in zip(g_refs, grads):
                theirs = g_ref.at[pl.ds(pl.multiple_of((2 * q + 1 - c) * rows, 8), rows), :]
                pltpu.make_async_remote_copy(
                    src_ref=theirs, dst_ref=recv_ref.at[q, pl.ds(off, rows), :], send_sem=send_sem, recv_sem=recv_sem,
                    device_id=sibling, device_id_type=MESH).start()
        everything = recv_ref.at[:, pl.ds(0, sent), :]
        whole = pltpu.make_async_remote_copy(src_ref=everything, dst_ref=everything, send_sem=send_sem,
                                             recv_sem=recv_sem, device_id=sibling, device_id_type=MESH)
        whole.wait_send()
        whole.wait_recv()

    return pl.pallas_call(
        body, name=name, out_shape=jax.ShapeDtypeStruct((4, total, cols), F32),
        in_specs=[HBM_SPEC] * n, out_specs=HBM_SPEC,
        scratch_shapes=[pltpu.SemaphoreType.DMA, pltpu.SemaphoreType.DMA],
    )(*[g for g, _, _ in grads])


def _pair_sum(g, from_sibling, off, rows, pos, name):
    cols = g.shape[1]

    def body(pos_ref, g_ref, r_ref, o_ref, ob_ref):
        s = g_ref[...] + r_ref[...]
        o_ref[...] = s
        ob_ref[...] = s.astype(BF16)

    out = pl.BlockSpec((None, rows, cols), lambda q, pos_ref: (q, 0, 0))
    return pl.pallas_call(
        body, name=name,
        grid_spec=pltpu.PrefetchScalarGridSpec(
            num_scalar_prefetch=1, grid=(4,),
            in_specs=[pl.BlockSpec((None, None, rows, cols), lambda q, pos_ref: (q, pos_ref[0], 0, 0)),
                      pl.BlockSpec((None, rows, cols), lambda q, pos_ref: (q, off // rows, 0))],
            out_specs=[out, out]),
        out_shape=[jax.ShapeDtypeStruct((4, rows, cols), F32), jax.ShapeDtypeStruct((4, rows, cols), BF16)],
        compiler_params=_cp(("parallel",)))(pos, g.reshape(4, 2, rows, cols), from_sibling)


def _rs_chips(parts, total, name):
    cols = parts[0][0].shape[2]
    n = len(parts)
    sent = sum(rows for _, _, rows in parts)

    def body(*refs):
        p_refs, out_ref, send_sems, recv_sems = refs[:n], refs[n], refs[n + 1], refs[n + 2]
        x, y, c = _mesh_pos()
        chips = _other_chips(x, y)
        for j, chip in enumerate(chips):
            q = 2 * chip[0] + chip[1]
            for p_ref, (_, off, rows) in zip(p_refs, parts):
                pltpu.make_async_remote_copy(
                    src_ref=p_ref.at[q], dst_ref=out_ref.at[j, pl.ds(off, rows), :], send_sem=send_sems.at[j],
                    recv_sem=recv_sems.at[j], device_id=(*chip, c), device_id_type=MESH).start()
        for j, chip in enumerate(chips):
            everything = out_ref.at[j, pl.ds(0, sent), :]
            whole = pltpu.make_async_remote_copy(src_ref=everything, dst_ref=everything, send_sem=send_sems.at[j],
                                                 recv_sem=recv_sems.at[j], device_id=(*chip, c), device_id_type=MESH)
            whole.wait_recv()
            whole.wait_send()

    return pl.pallas_call(
        body, name=name, out_shape=jax.ShapeDtypeStruct((3, total, cols), BF16),
        in_specs=[HBM_SPEC] * n, out_specs=HBM_SPEC,
        scratch_shapes=[pltpu.SemaphoreType.DMA((3,)), pltpu.SemaphoreType.DMA((3,))],
    )(*[p for p, _, _ in parts])


def _chip_sum(p, from_chips, off, rows, pos, name):
    cols = p.shape[2]

    def body(pos_ref, p_ref, r_ref, o_ref):
        acc = p_ref[...]
        for j in range(3):
            acc = acc + r_ref[j].astype(F32)
        o_ref[...] = acc

    return pl.pallas_call(
        body, name=name,
        grid_spec=pltpu.PrefetchScalarGridSpec(
            num_scalar_prefetch=1, grid=(1,),
            in_specs=[pl.BlockSpec((None, rows, cols), lambda i, pos_ref: (pos_ref[1], 0, 0)),
                      pl.BlockSpec((3, rows, cols), lambda i, pos_ref: (0, off // rows, 0))],
            out_specs=pl.BlockSpec((rows, cols), lambda i, pos_ref: (0, 0))),
        out_shape=jax.ShapeDtypeStruct((rows, cols), F32),
        compiler_params=_cp(("arbitrary",)))(pos, p, from_chips)


def _row_tile(rows, cap):
    t = min(rows, cap)
    while rows % t or t % 8:
        t -= 8
    return t


def _ada_mod(c_all, w, b, name):
    n = w.shape[1]

    def body(c_ref, w_ref, b_ref, o_ref):
        cv = c_ref[...]
        o_ref[...] = _exact_dot(cv * _sigmoid(cv), w_ref[...]) + b_ref[...]

    return pl.pallas_call(body, name=name, out_shape=jax.ShapeDtypeStruct((N_DEV, n), F32),
                          compiler_params=pltpu.CompilerParams(vmem_limit_bytes=VMEM_LIMIT))(c_all, w, b)


def _ada_grad(c_all, dmod, name):
    n = dmod.shape[1]

    def body(c_ref, d_ref, o_ref):
        cv = c_ref[...]
        o_ref[...] = _dot(cv * _sigmoid(cv), d_ref[...], 0, 0, lax.Precision.HIGHEST)

    return pl.pallas_call(body, name=name, out_shape=jax.ShapeDtypeStruct((D, n), F32),
                          compiler_params=pltpu.CompilerParams(vmem_limit_bytes=VMEM_LIMIT))(c_all, dmod)


def _adamw(w, g, m, v, name):
    rows, cols = w.shape
    tr = _row_tile(rows, 256) if rows % 8 == 0 else rows
    c1 = 1.0 - ADAM_B1 ** ADAM_STEP
    c2 = 1.0 - ADAM_B2 ** ADAM_STEP

    def body(w_ref, g_ref, m_ref, v_ref, d_ref, mo_ref, vo_ref):
        gv = g_ref[...]
        mn = ADAM_B1 * m_ref[...] + (1.0 - ADAM_B1) * gv
        vn = ADAM_B2 * v_ref[...] + (1.0 - ADAM_B2) * (gv * gv)
        mo_ref[...] = mn
        vo_ref[...] = vn
        d_ref[...] = -ADAM_LR * ((mn / c1) / (jnp.sqrt(vn / c2) + ADAM_EPS) + ADAM_WD * w_ref[...])

    spec = pl.BlockSpec((tr, cols), lambda i: (i, 0))
    shape = jax.ShapeDtypeStruct((rows, cols), F32)
    return pl.pallas_call(body, name=name, grid=(rows // tr,), in_specs=[spec] * 4, out_specs=[spec] * 3,
                          out_shape=[shape] * 3, compiler_params=_cp(("parallel",)))(w, g, m, v)


def _sum8_loss(v, loss_row, name):
    rows = v.shape[0] // N_DEV

    def body(v_ref, o_ref, l_ref):
        acc = v_ref[0:rows, :]
        for k in range(1, N_DEV):
            acc = acc + v_ref[k * rows:(k + 1) * rows, :]
        o_ref[...] = acc
        part = jnp.sum(acc[loss_row:loss_row + 8, :], axis=0, keepdims=True)
        l_ref[...] = jnp.broadcast_to(jnp.sum(part, axis=1, keepdims=True), (8, LANE))

    return pl.pallas_call(body, name=name,
                          out_shape=[jax.ShapeDtypeStruct((rows, LANE), F32), jax.ShapeDtypeStruct((8, LANE), F32)],
                          compiler_params=pltpu.CompilerParams(vmem_limit_bytes=VMEM_LIMIT))(v)


WEIGHT_NAMES = ("w_ada", "b_ada", "ffn1_norm", "ffn1_w_gate", "ffn1_w_up", "ffn1_w_down", "mix_norm", "w_in",
                "conv_w", "conv_b", "dt_bias", "a_log", "d_skip", "ssd_norm_w", "pool_w", "pool_b", "pool_scale",
                "w_out", "ffn2_norm", "ffn2_w_gate", "ffn2_w_up", "ffn2_w_down", "final_norm")

FF_SHARD = FF // N_DEV
IN_SHARD = D_IN // N_DEV
IN_SHARD_PAD = 528
OUT_SHARD = 2 * D_SSD // N_DEV
ADA_SHARD = 9 * D // N_DEV
POOL_SHARD_ROWS = 4 * 32 * POOL_GW // D
PACK = dict(gate1=(0, FF_SHARD), up1=(352, FF_SHARD), down1=(704, FF_SHARD), gate2=(1056, FF_SHARD),
            up2=(1408, FF_SHARD), down2=(1760, FF_SHARD), w_out=(2112, OUT_SHARD), w_in=(2368, IN_SHARD_PAD),
            pool_w=(2896, POOL_SHARD_ROWS))
PACK_W_ROWS = 2896
GPACK = dict(w_in=(0, IN_SHARD_PAD), w_out=(768, OUT_SHARD), gate1=(1056, FF_SHARD), up1=(1408, FF_SHARD),
             down1=(1760, FF_SHARD), gate2=(2112, FF_SHARD), up2=(2464, FF_SHARD), down2=(2816, FF_SHARD),
             pool_w=(3168, POOL_SHARD_ROWS))
PACK_G_ROWS = 3200

SMALL_ROWS = dict(dmod=(0, 72), ffn1_norm=(72, 8), mix_norm=(80, 8), ffn2_norm=(88, 8), final_norm=(96, 8),
                  ssd_norm_w=(104, 8), pool_scale=(112, 8), conv_b=(120, 16), conv_w=(136, 64), pool_b=(200, 8),
                  ssd=(208, 3), loss=(216, 8))
SMALL_TOTAL = 224


def _rows128(v, rows):
    flat = v.reshape(-1)
    return jnp.pad(flat, (0, rows * LANE - flat.shape[0])).reshape(rows, LANE)


def _pad_lanes(v):
    return jnp.pad(v.reshape(-1), (0, LANE - v.size))


def kernel(x, c, w_ada, b_ada, ffn1_norm, ffn1_w_gate, ffn1_w_up, ffn1_w_down, mix_norm, w_in, conv_w, conv_b, dt_bias, a_log, d_skip, ssd_norm_w, pool_w, pool_b, pool_scale, w_out, ffn2_norm, ffn2_w_gate, ffn2_w_up, ffn2_w_down, final_norm, loss_target, m_w_ada, m_b_ada, m_ffn1_norm, m_ffn1_w_gate, m_ffn1_w_up, m_ffn1_w_down, m_mix_norm, m_w_in, m_conv_w, m_conv_b, m_dt_bias, m_a_log, m_d_skip, m_ssd_norm_w, m_pool_w, m_pool_b, m_pool_scale, m_w_out, m_ffn2_norm, m_ffn2_w_gate, m_ffn2_w_up, m_ffn2_w_down, m_final_norm, v_w_ada, v_b_ada, v_ffn1_norm, v_ffn1_w_gate, v_ffn1_w_up, v_ffn1_w_down, v_mix_norm, v_w_in, v_conv_w, v_conv_b, v_dt_bias, v_a_log, v_d_skip, v_ssd_norm_w, v_pool_w, v_pool_b, v_pool_scale, v_w_out, v_ffn2_norm, v_ffn2_w_gate, v_ffn2_w_up, v_ffn2_w_down, v_final_norm):
    given = dict(locals())
    w = {n: given[n] for n in WEIGHT_NAMES}
    m = {n: given["m_" + n] for n in WEIGHT_NAMES}
    v = {n: given["v_" + n] for n in WEIGHT_NAMES}
    mx, my, mc = _mesh_pos()
    me = 4 * mx + 2 * my + mc

    small = jnp.concatenate([c.reshape(-1), conv_w.reshape(-1), pool_b.reshape(-1), pool_w.reshape(-1)])
    small_rows = 280
    gs = _all_gather(_rows128(small, small_rows), [(0, small_rows)], "ag_small").reshape(N_DEV, small_rows * LANE)
    c_all = gs[:, 0:D]
    conv_w_full = gs[:, 1024:2048].reshape(N_DEV, 4, 256).transpose(1, 0, 2).reshape(4, D_XBC)
    pool_b_full = gs[:, 2048:2176].reshape(N_DEV, 4, 32).transpose(1, 0, 2).reshape(1, D_POOL)
    pool_w_full = gs[:, 2176:2176 + 32768].reshape(N_DEV, 4, 32, POOL_GW).transpose(1, 0, 2, 3).reshape(4, POOL_GW, POOL_GW).astype(BF16)

    b_ada_cols = lax.dynamic_slice(b_ada, (0, me * ADA_SHARD), (1, ADA_SHARD))
    mod_part = _ada_mod(c_all, w_ada[0], b_ada_cols, "ada_mod")
    mod_all = _all_gather(mod_part, [(0, N_DEV)], "ag_mod").reshape(N_DEV, N_DEV, ADA_SHARD)
    mod = lax.dynamic_index_in_dim(mod_all, me, axis=1, keepdims=False).reshape(9, D)

    win_t_shard = jnp.pad(w_in[0].T, ((0, IN_SHARD_PAD - IN_SHARD), (0, 0)))
    pack_a = jnp.concatenate([ffn1_w_gate[0].T, ffn1_w_up[0].T, ffn1_w_down[0]], axis=0).astype(BF16)
    pack_b = jnp.concatenate([ffn2_w_gate[0].T, ffn2_w_up[0].T, ffn2_w_down[0], w_out[0], win_t_shard], axis=0).astype(BF16)
    ffn_regions = [(0, FF_SHARD), (FF_SHARD, FF_SHARD), (2 * FF_SHARD, FF_SHARD)]
    full_a = _all_gather_async(pack_a, ffn_regions, "ag_weights_ffn1", 1)
    full_b = _all_gather_async(pack_b, ffn_regions + [(3 * FF_SHARD, OUT_SHARD), (3 * FF_SHARD + OUT_SHARD, IN_SHARD_PAD)],
                               "ag_weights_rest", 2)
    o_out = N_DEV * 3 * FF_SHARD
    o_in = o_out + N_DEV * OUT_SHARD
    wout_full = full_b[o_out:o_in]
    win_g = full_b[o_in:o_in + N_DEV * IN_SHARD_PAD].reshape(N_DEV, IN_SHARD_PAD, D)[:, :IN_SHARD].reshape(D_IN, D)
    win_t = jnp.concatenate([win_g[0:1024], win_g[1024:3072], win_g[3088:4112], win_g[3072:3088],
                             jnp.zeros((D_IN_PAD - D_IN, D), BF16)], axis=0)

    vecs = dict(ffn1_norm=ffn1_norm, mix_norm=mix_norm, ffn2_norm=ffn2_norm, final_norm=final_norm.reshape(1, D),
                conv_w=conv_w_full, conv_b=conv_b, ssd_norm_w=ssd_norm_w, pool_b=pool_b_full, pool_scale=pool_scale,
                ssd_par=jnp.concatenate([_pad_lanes(dt_bias)[None], _pad_lanes(a_log)[None], _pad_lanes(d_skip)[None],
                                         jnp.zeros((5, LANE), F32)], axis=0))
    dx0, st, dw1, dw3, d_win, d_wout, d_pool_w = _local_step(
        x[0], loss_target[0], mod, full_a, full_b, win_t, wout_full, pool_w_full, vecs, min(512, x.shape[1]))

    dwin = jnp.concatenate([d_win[0], d_win[1], d_win[3][0:16], d_win[2]], axis=0)
    dwin = jnp.pad(dwin.reshape(N_DEV, IN_SHARD, D), ((0, 0), (0, IN_SHARD_PAD - IN_SHARD), (0, 0))).reshape(N_DEV * IN_SHARD_PAD, D)
    dwout = jnp.concatenate(d_wout, axis=0)
    dpool = d_pool_w.reshape(4, N_DEV, 32, POOL_GW).transpose(1, 0, 2, 3).reshape(N_DEV * POOL_SHARD_ROWS, D)
    keys = ("gate1", "up1", "down1", "gate2", "up2", "down2", "w_out", "w_in", "pool_w")
    grads = [(g, *GPACK[k]) for g, k in zip((*dw1, *dw3, dwout, dwin, dpool), keys)]
    pos = jnp.stack([mc, 2 * mx + my]).astype(jnp.int32)
    from_sibling = _rs_pair(grads, PACK_G_ROWS, "rs_pair")
    pairs = {k: _pair_sum(g, from_sibling, off, rows, pos, f"rs_pair_sum_{k}") for (g, off, rows), k in zip(grads, keys)}
    from_chips = _rs_chips([(pairs[k][1], *GPACK[k]) for k in keys], PACK_G_ROWS, "rs_chips")
    reduced = {k: _chip_sum(pairs[k][0], from_chips, *GPACK[k], pos, f"rs_chip_sum_{k}") for k in keys}

    def shard(k, rows=None):
        return reduced[k] if rows is None else reduced[k][0:rows]

    dmod = jnp.concatenate([st["ffn1"][0:3], st["mix"][0:3], st["ffn2"][0:3]], axis=0)
    sg = jnp.concatenate([
        dmod.reshape(-1), st["ffn1"][3], st["mix"][3], st["ffn2"][3], st["loss"][0], st["gn"][0], st["pool"][1],
        st["conv"][4], st["conv"][0:4].reshape(-1), st["pool"][0], st["ssd"][0:3].reshape(-1),
        jnp.zeros((5 * LANE,), F32), st["loss"][1]])
    sg_all = _all_gather(sg.reshape(SMALL_TOTAL, LANE), [(0, SMALL_TOTAL)], "ag_small_grads")
    tot, loss_b = _sum8_loss(sg_all, SMALL_ROWS["loss"][0], "small_sum")
    loss = loss_b[0, 0]
    dmod_all = sg_all.reshape(N_DEV, SMALL_TOTAL * LANE)[:, 0:9 * D]
    g_w_ada = _ada_grad(c_all, lax.dynamic_slice(dmod_all, (0, me * ADA_SHARD), (N_DEV, ADA_SHARD)), "ada_grad")

    def tot_rows(k):
        off, n = SMALL_ROWS[k]
        return tot[off:off + n].reshape(-1)

    g_conv_w = lax.dynamic_slice(tot_rows("conv_w").reshape(4, D_XBC), (0, me * 256), (4, 256))
    g_pool_b = lax.dynamic_slice(tot_rows("pool_b").reshape(4, POOL_GW), (0, me * 32), (4, 32))
    g_ssd = tot_rows("ssd").reshape(3, LANE)
    grad = {
        "w_ada": g_w_ada[None], "b_ada": tot_rows("dmod").reshape(1, 9 * D),
        "ffn1_norm": tot_rows("ffn1_norm")[None], "mix_norm": tot_rows("mix_norm")[None],
        "ffn2_norm": tot_rows("ffn2_norm")[None], "final_norm": tot_rows("final_norm"),
        "ssd_norm_w": tot_rows("ssd_norm_w")[None], "pool_scale": tot_rows("pool_scale")[None],
        "conv_b": tot_rows("conv_b")[None], "conv_w": g_conv_w[None], "pool_b": g_pool_b[None],
        "dt_bias": g_ssd[0:1, 0:N_HEADS], "a_log": g_ssd[1:2, 0:N_HEADS], "d_skip": g_ssd[2:3, 0:N_HEADS],
        "ffn1_w_gate": shard("gate1").T[None], "ffn1_w_up": shard("up1").T[None], "ffn1_w_down": shard("down1")[None],
        "ffn2_w_gate": shard("gate2").T[None], "ffn2_w_up": shard("up2").T[None], "ffn2_w_down": shard("down2")[None],
        "w_out": shard("w_out")[None], "w_in": shard("w_in", IN_SHARD).T[None],
        "pool_w": shard("pool_w").reshape(1, 4, 32, POOL_GW),
    }

    big = ("w_ada", "ffn1_w_gate", "ffn1_w_up", "ffn1_w_down", "w_in", "pool_w", "w_out",
           "ffn2_w_gate", "ffn2_w_up", "ffn2_w_down")
    delta, new_m, new_v = {}, {}, {}
    for n in big:
        shp = w[n].shape
        two_d = (shp[-3] * shp[-2], shp[-1]) if n == "pool_w" else shp[-2:]
        d_, m_, v_ = _adamw(w[n].reshape(two_d), grad[n].reshape(two_d), m[n].reshape(two_d), v[n].reshape(two_d),
                            f"adamw_{n}")
        delta[n], new_m[n], new_v[n] = d_.reshape(shp), m_.reshape(shp), v_.reshape(shp)
    small_names = [n for n in WEIGHT_NAMES if n not in big]
    sizes = [LANE if w[n].size < LANE else w[n].size for n in small_names]
    small_rows_adam = -(-sum(sizes) // (8 * LANE)) * 8

    def pack_small(t):
        return _rows128(jnp.concatenate([_pad_lanes(t[n]) if t[n].size < LANE else t[n].reshape(-1) for n in small_names]),
                        small_rows_adam)

    d_s, m_s, v_s = _adamw(pack_small(w), pack_small(grad), pack_small(m), pack_small(v), "adamw_small")
    off = 0
    for n, size in zip(small_names, sizes):
        for res, packed in ((delta, d_s), (new_m, m_s), (new_v, v_s)):
            res[n] = packed.reshape(-1)[off:off + w[n].size].reshape(w[n].shape)
        off += size

    return (loss, dx0[None], *[grad[n] for n in WEIGHT_NAMES], *[delta[n] for n in WEIGHT_NAMES],
            *[new_m[n] for n in WEIGHT_NAMES], *[new_v[n] for n in WEIGHT_NAMES])
```

```python
import functools
import math

import jax
import jax.numpy as jnp
from jax import lax
from jax.experimental import pallas as pl
from jax.experimental.pallas import tpu as pltpu
from jax.experimental.pallas import tpu_sc as plsc

F32 = jnp.float32
BF16 = jnp.bfloat16
MESH = pl.DeviceIdType.MESH

N_DEV = 8
D = 1024
FF = 2816
D_SSD = 1024
N_HEADS = 16
HEAD_DIM = 64
N_GROUPS = 4
N_STATE = 128
CHUNK = 128
GROUP_W = D_SSD // N_GROUPS
D_XBC = D_SSD + 2 * N_GROUPS * N_STATE
D_POOL = 1024
POOL_WINDOWS = (2, 4, 8, 16)
POOL_GW = 256
D_IN = 4112
D_IN_PAD = 4224
COL_Z, COL_XBC, COL_U, COL_DT = 0, 1024, 3072, 4096
EPS = 1e-6
FFN_RES = 0.5
LANE = 128
HALO = 16

ADAM_LR, ADAM_B1, ADAM_B2, ADAM_EPS, ADAM_WD, ADAM_STEP = 0.001, 0.9, 0.999, 1e-08, 0.01, 10

VMEM_LIMIT = 56 << 20


def _cp(sem):
    return pltpu.CompilerParams(dimension_semantics=sem, vmem_limit_bytes=VMEM_LIMIT)


def _dot(a, b, ca, cb, prec=None):
    return lax.dot_general(a, b, (((ca,), (cb,)), ((), ())), precision=prec,
                           preferred_element_type=F32)


def _exact_dot(a, b):
    return _dot(a, b, 1, 0, lax.Precision.HIGHEST)


def _sigmoid(v):
    return 1.0 / (1.0 + jnp.exp(-v))


def _silu_grad(v, sg):
    return sg * (1.0 + v * (1.0 - sg))


def _mm_nt(a, bt, tm, tn, out_dtype, name):
    m, k = a.shape
    n = bt.shape[0]

    def body(a_ref, b_ref, o_ref):
        o_ref[...] = _dot(a_ref[...], b_ref[...], 1, 1).astype(out_dtype)

    return pl.pallas_call(
        body, name=name, grid=(n // tn, m // tm),
        in_specs=[pl.BlockSpec((tm, k), lambda j, i: (i, 0)),
                  pl.BlockSpec((tn, k), lambda j, i: (j, 0))],
        out_specs=pl.BlockSpec((tm, tn), lambda j, i: (i, j)),
        out_shape=jax.ShapeDtypeStruct((m, n), out_dtype),
        compiler_params=_cp(("parallel", "parallel")))(a, bt)


def _mm_tn(a, b, tm, tk, name):
    kk, m = a.shape
    n = b.shape[1]
    nk = kk // tk

    def body(a_ref, b_ref, o_ref, acc):
        k = pl.program_id(1)

        @pl.when(k == 0)
        def _():
            acc[...] = jnp.zeros_like(acc)

        acc[...] += _dot(a_ref[...], b_ref[...], 0, 0)

        @pl.when(k == nk - 1)
        def _():
            o_ref[...] = acc[...]

    return pl.pallas_call(
        body, name=name, grid=(m // tm, nk),
        in_specs=[pl.BlockSpec((tk, tm), lambda i, k: (k, i)),
                  pl.BlockSpec((tk, n), lambda i, k: (k, 0))],
        out_specs=pl.BlockSpec((tm, n), lambda i, k: (i, 0)),
        out_shape=jax.ShapeDtypeStruct((m, n), F32),
        scratch_shapes=[pltpu.VMEM((tm, n), F32)],
        compiler_params=_cp(("parallel", "arbitrary")))(a, b)


def _prenorm(x, wn, mod, k, tm, name):
    seq = x.shape[0]

    def body(x_ref, wn_ref, mod_ref, h_ref):
        xv = x_ref[...]
        r = lax.rsqrt(jnp.mean(xv * xv, axis=-1, keepdims=True) + EPS)
        hn = xv * r * wn_ref[...]
        h_ref[...] = (hn * (1.0 + mod_ref[3 * k + 1:3 * k + 2, :]) + mod_ref[3 * k:3 * k + 1, :]).astype(BF16)

    return pl.pallas_call(
        body, name=name, grid=(seq // tm,),
        in_specs=[pl.BlockSpec((tm, D), lambda i: (i, 0)),
                  pl.BlockSpec((1, D), lambda i: (0, 0)),
                  pl.BlockSpec((9, D), lambda i: (0, 0))],
        out_specs=pl.BlockSpec((tm, D), lambda i: (i, 0)),
        out_shape=jax.ShapeDtypeStruct((seq, D), BF16),
        compiler_params=_cp(("parallel",)))(x, wn, mod)


def _scale_cast(dxo, mod, row, res, tm, name):
    seq = dxo.shape[0]

    def body(d_ref, mod_ref, o_ref):
        o_ref[...] = (d_ref[...] * (res * mod_ref[row:row + 1, :])).astype(BF16)

    return pl.pallas_call(
        body, name=name, grid=(seq // tm,),
        in_specs=[pl.BlockSpec((tm, D), lambda i: (i, 0)),
                  pl.BlockSpec((9, D), lambda i: (0, 0))],
        out_specs=pl.BlockSpec((tm, D), lambda i: (i, 0)),
        out_shape=jax.ShapeDtypeStruct((seq, D), BF16),
        compiler_params=_cp(("parallel",)))(dxo, mod)


def _norm_bwd(dh, xv, dxo, branch, wn, sc, res, stats_ref, first):
    r = lax.rsqrt(jnp.mean(xv * xv, axis=-1, keepdims=True) + EPS)
    xn = xv * r
    dhn = dh * (1.0 + sc)
    dxn = dhn * wn
    dx = dxo + r * (dxn - xn * jnp.mean(dxn * xn, axis=-1, keepdims=True))
    rows = jnp.concatenate([
        jnp.sum(dh, axis=0, keepdims=True),
        jnp.sum(dh * (xn * wn), axis=0, keepdims=True),
        jnp.sum(branch * dxo, axis=0, keepdims=True) * res,
        jnp.sum(dhn * xn, axis=0, keepdims=True),
        jnp.zeros((4, D), F32)], axis=0)

    @pl.when(first)
    def _():
        stats_ref[...] = rows

    @pl.when(jnp.logical_not(first))
    def _():
        stats_ref[...] += rows

    return dx


def _loss_head(x3, wf, tgt, tm, name):
    seq = x3.shape[0]

    def body(x_ref, w_ref, t_ref, dx_ref, st_ref):
        xv = x_ref[...]
        wv = w_ref[...]
        r = lax.rsqrt(jnp.mean(xv * xv, axis=-1, keepdims=True) + EPS)
        xn = xv * r
        e = xn * wv - t_ref[...]
        dy = e * (1.0 / D)
        dxn = dy * wv
        dx_ref[...] = r * (dxn - xn * jnp.mean(dxn * xn, axis=-1, keepdims=True))
        rows = jnp.concatenate([
            jnp.sum(dy * xn, axis=0, keepdims=True),
            jnp.sum(e * e, axis=0, keepdims=True) * (0.5 / D),
            jnp.zeros((6, D), F32)], axis=0)

        @pl.when(pl.program_id(0) == 0)
        def _():
            st_ref[...] = rows

        @pl.when(pl.program_id(0) != 0)
        def _():
            st_ref[...] += rows

    return pl.pallas_call(
        body, name=name, grid=(seq // tm,),
        in_specs=[pl.BlockSpec((tm, D), lambda i: (i, 0)),
                  pl.BlockSpec((1, D), lambda i: (0, 0)),
                  pl.BlockSpec((tm, D), lambda i: (i, 0))],
        out_specs=[pl.BlockSpec((tm, D), lambda i: (i, 0)),
                   pl.BlockSpec((8, D), lambda i: (0, 0))],
        out_shape=[jax.ShapeDtypeStruct((seq, D), F32), jax.ShapeDtypeStruct((8, D), F32)],
        compiler_params=_cp(("arbitrary",)))(x3, wf, tgt)


def _ffn_up(h, w, blk, tm, tn, name):
    seq = h.shape[0]
    nj = FF // tn

    def body(h_ref, wg_ref, wu_ref, g_ref, u_ref, a_ref):
        hv = h_ref[...]
        g = _dot(hv, wg_ref[...], 1, 1)
        u = _dot(hv, wu_ref[...], 1, 1)
        g_ref[...] = g.astype(BF16)
        u_ref[...] = u.astype(BF16)
        a_ref[...] = (g * _sigmoid(g) * u).astype(BF16)

    act = pl.BlockSpec((tm, tn), lambda j, i: (i, j))
    return pl.pallas_call(
        body, name=name, grid=(nj, seq // tm),
        in_specs=[pl.BlockSpec((tm, D), lambda j, i: (i, 0)),
                  pl.BlockSpec((tn, D), lambda j, i: (blk * nj + j, 0)),
                  pl.BlockSpec((tn, D), lambda j, i: ((blk + 1) * nj + j, 0))],
        out_specs=[act, act, act],
        out_shape=[jax.ShapeDtypeStruct((seq, FF), BF16)] * 3,
        compiler_params=_cp(("parallel", "parallel")))(h, w, w)


def _ffn_down(a, w, blk, x, mod, grow, tm, tk, name):
    seq = a.shape[0]
    nk = FF // tk

    def body(a_ref, w_ref, x_ref, mod_ref, xo_ref, f_ref, acc):
        k = pl.program_id(1)

        @pl.when(k == 0)
        def _():
            acc[...] = jnp.zeros_like(acc)

        acc[...] += _dot(a_ref[...], w_ref[...], 1, 0)

        @pl.when(k == nk - 1)
        def _():
            f = acc[...]
            f_ref[...] = f.astype(BF16)
            xo_ref[...] = x_ref[...] + (FFN_RES * mod_ref[grow:grow + 1, :]) * f

    return pl.pallas_call(
        body, name=name, grid=(seq // tm, nk),
        in_specs=[pl.BlockSpec((tm, tk), lambda i, k: (i, k)),
                  pl.BlockSpec((tk, D), lambda i, k: (blk * nk + k, 0)),
                  pl.BlockSpec((tm, D), lambda i, k: (i, 0)),
                  pl.BlockSpec((9, D), lambda i, k: (0, 0))],
        out_specs=[pl.BlockSpec((tm, D), lambda i, k: (i, 0)),
                   pl.BlockSpec((tm, D), lambda i, k: (i, 0))],
        out_shape=[jax.ShapeDtypeStruct((seq, D), F32), jax.ShapeDtypeStruct((seq, D), BF16)],
        scratch_shapes=[pltpu.VMEM((tm, D), F32)],
        compiler_params=_cp(("parallel", "arbitrary")))(a, w, x, mod)


def _ffn_bwd_da(df, w, blk, g, u, tm, tn, name):
    seq = df.shape[0]
    nj = FF // tn

    def body(df_ref, w_ref, g_ref, u_ref, dg_ref, du_ref):
        da = _dot(df_ref[...], w_ref[...], 1, 1)
        gv = g_ref[...].astype(F32)
        uv = u_ref[...].astype(F32)
        sg = _sigmoid(gv)
        dg_ref[...] = (da * uv * _silu_grad(gv, sg)).astype(BF16)
        du_ref[...] = (da * (gv * sg)).astype(BF16)

    act = pl.BlockSpec((tm, tn), lambda j, i: (i, j))
    return pl.pallas_call(
        body, name=name, grid=(nj, seq // tm),
        in_specs=[pl.BlockSpec((tm, D), lambda j, i: (i, 0)),
                  pl.BlockSpec((tn, D), lambda j, i: (blk * nj + j, 0)),
                  act, act],
        out_specs=[act, act],
        out_shape=[jax.ShapeDtypeStruct((seq, FF), BF16)] * 2,
        compiler_params=_cp(("parallel", "parallel")))(df, w, g, u)


def _ffn_bwd_dh(dg, du, w, blk, x, dxo, fb, wn, mod, k, tm, tk, name):
    seq = x.shape[0]
    nk = FF // tk

    def body(dg_ref, du_ref, wg_ref, wu_ref, x_ref, dxo_ref, f_ref, wn_ref, mod_ref, dx_ref, st_ref, acc):
        kk = pl.program_id(1)
        first = pl.program_id(0) == 0

        @pl.when(kk == 0)
        def _():
            acc[...] = jnp.zeros_like(acc)

        acc[...] += _dot(dg_ref[...], wg_ref[...], 1, 0) + _dot(du_ref[...], wu_ref[...], 1, 0)

        @pl.when(kk == nk - 1)
        def _():
            dx_ref[...] = _norm_bwd(acc[...], x_ref[...], dxo_ref[...], f_ref[...].astype(F32), wn_ref[...],
                                    mod_ref[3 * k + 1:3 * k + 2, :], FFN_RES, st_ref, first)

    tok = pl.BlockSpec((tm, D), lambda i, kk: (i, 0))
    return pl.pallas_call(
        body, name=name, grid=(seq // tm, nk),
        in_specs=[pl.BlockSpec((tm, tk), lambda i, kk: (i, kk)),
                  pl.BlockSpec((tm, tk), lambda i, kk: (i, kk)),
                  pl.BlockSpec((tk, D), lambda i, kk: (blk * nk + kk, 0)),
                  pl.BlockSpec((tk, D), lambda i, kk: ((blk + 1) * nk + kk, 0)),
                  tok, tok, tok,
                  pl.BlockSpec((1, D), lambda i, kk: (0, 0)),
                  pl.BlockSpec((9, D), lambda i, kk: (0, 0))],
        out_specs=[tok, pl.BlockSpec((8, D), lambda i, kk: (0, 0))],
        out_shape=[jax.ShapeDtypeStruct((seq, D), F32), jax.ShapeDtypeStruct((8, D), F32)],
        scratch_shapes=[pltpu.VMEM((tm, D), F32)],
        compiler_params=_cp(("arbitrary", "arbitrary")))(dg, du, w, w, x, dxo, fb, wn, mod)


def _ffn_forward(x, w, wn, mod, k, tm, tag):
    w_gu, w_d, blk_d = w
    h = _prenorm(x, wn, mod, k, tm, f"{tag}_prenorm")
    g, u, a = _ffn_up(h, w_gu, 0, tm, FF // 2, f"{tag}_up")
    xo, fb = _ffn_down(a, w_d, blk_d, x, mod, 3 * k + 2, tm, FF // 2, f"{tag}_down")
    return xo, (x, h, g, u, a, fb)


def _ffn_backward(dxo, saved, w, wn, mod, k, tm, tag):
    w_gu, w_d, blk_d = w
    x, h, g, u, a, fb = saved
    df = _scale_cast(dxo, mod, 3 * k + 2, FFN_RES, tm, f"{tag}_dbranch")
    dg, du = _ffn_bwd_da(df, w_d, blk_d, g, u, tm, FF // 2, f"{tag}_bwd_da")
    dx, stats = _ffn_bwd_dh(dg, du, w_gu, 0, x, dxo, fb, wn, mod, k, tm, FF // 2, f"{tag}_bwd_dh")
    seq = x.shape[0]
    tk = min(seq, 512)
    d_gate_t = _mm_tn(dg, h, FF // 2, tk, f"{tag}_dw_gate")
    d_up_t = _mm_tn(du, h, FF // 2, tk, f"{tag}_dw_up")
    d_down = _mm_tn(a, df, FF // 2, tk, f"{tag}_dw_down")
    return dx, stats, (d_gate_t, d_up_t, d_down)


def _prev_rows(tm, col):
    return pl.BlockSpec((HALO, 1024), lambda i, j: (jnp.maximum(i * (tm // HALO) - 1, 0), col + j))


def _conv_pre(ext, cw, cb, rows):
    pre = cb + cw[3:4, :] * ext
    for s in (1, 2, 3):
        pre = pre + cw[3 - s:4 - s, :] * pltpu.roll(ext, s, 0)
    return pre[HALO:HALO + rows]


def _conv_fwd(proj, cw, cb, tm, name):
    seq = proj.shape[0]

    def body(x_ref, p_ref, cw_ref, cb_ref, o_ref):
        prev = jnp.where(pl.program_id(0) == 0, 0.0, p_ref[...])
        ext = jnp.concatenate([prev, x_ref[...]], axis=0)
        pre = _conv_pre(ext, cw_ref[...], cb_ref[...], tm)
        o_ref[...] = pre * _sigmoid(pre)

    c0 = COL_XBC // 1024
    return pl.pallas_call(
        body, name=name, grid=(seq // tm, 2),
        in_specs=[pl.BlockSpec((tm, 1024), lambda i, j: (i, c0 + j)),
                  _prev_rows(tm, c0),
                  pl.BlockSpec((4, 1024), lambda i, j: (0, j)),
                  pl.BlockSpec((1, 1024), lambda i, j: (0, j))],
        out_specs=pl.BlockSpec((tm, 1024), lambda i, j: (i, j)),
        out_shape=jax.ShapeDtypeStruct((seq, D_XBC), F32),
        compiler_params=_cp(("parallel", "parallel")))(proj, proj, cw, cb)


def _conv_bwd(dact, proj, cw, cb, tm, name):
    seq = proj.shape[0]
    ni = seq // tm

    def body(d_ref, dn_ref, x_ref, p_ref, n_ref, cw_ref, cb_ref, o_ref, st_ref):
        i = pl.program_id(1)
        cwv = cw_ref[...]
        prev = jnp.where(i == 0, 0.0, p_ref[...])
        ext = jnp.concatenate([prev, x_ref[...], n_ref[...]], axis=0)
        pre = _conv_pre(ext, cwv, cb_ref[...], tm + HALO)
        dnext = jnp.where(i == ni - 1, 0.0, dn_ref[...])
        dext = jnp.concatenate([d_ref[...], dnext], axis=0)
        dpre = dext * _silu_grad(pre, _sigmoid(pre))
        n = tm + HALO
        dx = cwv[3:4, :] * dpre
        for s in (1, 2, 3):
            dx = dx + cwv[3 - s:4 - s, :] * pltpu.roll(dpre, n - s, 0)
        o_ref[...] = dx[:tm].astype(BF16)
        dcur = dpre[:tm]
        rows = [jnp.sum(dcur * pltpu.roll(ext, 3 - k, 0)[HALO:HALO + tm], axis=0, keepdims=True) for k in range(3)]
        rows.append(jnp.sum(dcur * ext[HALO:HALO + tm], axis=0, keepdims=True))
        rows.append(jnp.sum(dcur, axis=0, keepdims=True))
        rows.append(jnp.zeros((3, 1024), F32))
        rows = jnp.concatenate(rows, axis=0)

        @pl.when(i == 0)
        def _():
            st_ref[...] = rows

        @pl.when(i != 0)
        def _():
            st_ref[...] += rows

    c0 = COL_XBC // 1024
    return pl.pallas_call(
        body, name=name, grid=(2, ni),
        in_specs=[pl.BlockSpec((tm, 1024), lambda j, i: (i, j)),
                  pl.BlockSpec((HALO, 1024), lambda j, i: (jnp.minimum((i + 1) * (tm // HALO), seq // HALO - 1), j)),
                  pl.BlockSpec((tm, 1024), lambda j, i: (i, c0 + j)),
                  pl.BlockSpec((HALO, 1024), lambda j, i: (jnp.maximum(i * (tm // HALO) - 1, 0), c0 + j)),
                  pl.BlockSpec((HALO, 1024), lambda j, i: (jnp.minimum((i + 1) * (tm // HALO), seq // HALO - 1), c0 + j)),
                  pl.BlockSpec((4, 1024), lambda j, i: (0, j)),
                  pl.BlockSpec((1, 1024), lambda j, i: (0, j))],
        out_specs=[pl.BlockSpec((tm, 1024), lambda j, i: (i, j)),
                   pl.BlockSpec((8, 1024), lambda j, i: (0, j))],
        out_shape=[jax.ShapeDtypeStruct((seq, D_XBC), BF16), jax.ShapeDtypeStruct((8, D_XBC), F32)],
        compiler_params=_cp(("parallel", "arbitrary")))(dact, dact, proj, proj, proj, cw, cb)


def _head_expand():
    r = lax.broadcasted_iota(jnp.int32, (LANE, D_SSD), 0)
    c = lax.broadcasted_iota(jnp.int32, (LANE, D_SSD), 1)
    return (c // HEAD_DIM == r).astype(F32)


def _head_reduce():
    r = lax.broadcasted_iota(jnp.int32, (D_SSD, LANE), 0)
    c = lax.broadcasted_iota(jnp.int32, (D_SSD, LANE), 1)
    return (r // HEAD_DIM == c).astype(F32)


def _ssd_common(dtr, par):
    q = CHUNK
    v = dtr + par[0:1, :]
    dt = jnp.maximum(v, 0.0) + jnp.log(1.0 + jnp.exp(-jnp.abs(v)))
    a = -jnp.exp(par[1:2, :])
    adt = dt * a
    li = lax.broadcasted_iota(jnp.int32, (q, q), 0)
    si = lax.broadcasted_iota(jnp.int32, (q, q), 1)
    causal = li >= si
    acs = _exact_dot(causal.astype(F32), adt)
    expand = _head_expand()
    dt_l = _exact_dot(dt, expand)
    acs_l = _exact_dot(acs, expand)
    par_l = _exact_dot(par, expand)
    last_l = acs_l[q - 1:q, :]
    return dict(v=v, dt=dt, a=a, acs=acs, acs_t=acs.T, causal=causal, dt_l=dt_l, acs_l=acs_l,
                ea_l=jnp.exp(acs_l), ds_l=jnp.exp(last_l - acs_l), cd_l=jnp.exp(last_l), dskip_l=par_l[2:3, :])


def _decay(cm, h):
    seg = cm["acs"][:, h:h + 1] - cm["acs_t"][h:h + 1, :]
    return jnp.exp(jnp.where(cm["causal"], seg, -jnp.inf))


def _lane_mask(r):
    lane = lax.broadcasted_iota(jnp.int32, (1, GROUP_W), 1)
    return lane // HEAD_DIM == r


def _ssd_fwd(xbc, proj, par, name):
    seq = xbc.shape[0]
    nc = seq // CHUNK
    q = CHUNK

    def body(x_ref, dt_ref, par_ref, y_ref, hp_ref, state):
        @pl.when(pl.program_id(0) == 0)
        def _():
            state[...] = jnp.zeros_like(state)

        cm = _ssd_common(dt_ref[...], par_ref[...])
        for g in range(N_GROUPS):
            lo = g * GROUP_W
            xs = x_ref[:, lo:lo + GROUP_W]
            bm = x_ref[:, D_SSD + g * N_STATE:D_SSD + (g + 1) * N_STATE].astype(BF16)
            cmat = x_ref[:, D_SSD + N_GROUPS * N_STATE + g * N_STATE:D_SSD + N_GROUPS * N_STATE + (g + 1) * N_STATE].astype(BF16)
            xdt = xs * cm["dt_l"][:, lo:lo + GROUP_W]
            xdt_b = xdt.astype(BF16)
            cb = _dot(cmat, bm, 1, 1)
            yd = jnp.zeros((q, GROUP_W), F32)
            for r in range(4):
                s_h = (cb * _decay(cm, 4 * g + r)).astype(BF16)
                yd = jnp.where(_lane_mask(r), _dot(s_h, xdt_b, 1, 0), yd)
            hg = state[g]
            hp_ref[0, g] = hg
            yo = _dot(cmat, hg.astype(BF16), 1, 0) * cm["ea_l"][:, lo:lo + GROUP_W]
            y_ref[:, lo:lo + GROUP_W] = yd + yo + cm["dskip_l"][:, lo:lo + GROUP_W] * xs
            xds = (xdt * cm["ds_l"][:, lo:lo + GROUP_W]).astype(BF16)
            state[g] = hg * cm["cd_l"][:, lo:lo + GROUP_W] + _dot(bm, xds, 0, 0)

    return pl.pallas_call(
        body, name=name, grid=(nc,),
        in_specs=[pl.BlockSpec((q, D_XBC), lambda c: (c, 0)),
                  pl.BlockSpec((q, LANE), lambda c: (c, COL_DT // LANE)),
                  pl.BlockSpec((8, LANE), lambda c: (0, 0))],
        out_specs=[pl.BlockSpec((q, D_SSD), lambda c: (c, 0)),
                   pl.BlockSpec((1, N_GROUPS, N_STATE, GROUP_W), lambda c: (c, 0, 0, 0))],
        out_shape=[jax.ShapeDtypeStruct((seq, D_SSD), F32),
                   jax.ShapeDtypeStruct((nc, N_GROUPS, N_STATE, GROUP_W), F32)],
        scratch_shapes=[pltpu.VMEM((N_GROUPS, N_STATE, GROUP_W), F32)],
        compiler_params=_cp(("arbitrary",)))(xbc, proj, par)


def _ssd_bwd(dy, xbc, proj, par, hprev, name):
    seq = xbc.shape[0]
    nc = seq // CHUNK
    q = CHUNK

    def body(dy_ref, x_ref, dt_ref, par_ref, hp_ref, dx_ref, ddt_ref, st_ref, dstate):
        step = pl.program_id(0)

        @pl.when(step == 0)
        def _():
            dstate[...] = jnp.zeros_like(dstate)

        par = par_ref[...]
        cm = _ssd_common(dt_ref[...], par)
        reduce = _head_reduce()
        lane128 = lax.broadcasted_iota(jnp.int32, (1, LANE), 1)
        row128 = lax.broadcasted_iota(jnp.int32, (LANE, 1), 0)
        d_acs = jnp.zeros((q, LANE), F32)
        d_acs_t = jnp.zeros((LANE, q), F32)
        last_terms = []
        acs_terms = []
        dxdt_all = []
        for g in range(N_GROUPS):
            lo = g * GROUP_W
            sl = slice(lo, lo + GROUP_W)
            xs = x_ref[:, sl]
            bm32 = x_ref[:, D_SSD + g * N_STATE:D_SSD + (g + 1) * N_STATE]
            cm32 = x_ref[:, D_SSD + N_GROUPS * N_STATE + g * N_STATE:D_SSD + N_GROUPS * N_STATE + (g + 1) * N_STATE]
            bm = bm32.astype(BF16)
            cmat = cm32.astype(BF16)
            dyg = dy_ref[:, sl]
            dyg_b = dyg.astype(BF16)
            xdt = xs * cm["dt_l"][:, sl]
            xdt_b = xdt.astype(BF16)
            hg = hp_ref[0, g]
            hg_b = hg.astype(BF16)
            dhg = dstate[g]
            dhg_b = dhg.astype(BF16)
            ea = cm["ea_l"][:, sl]
            ds = cm["ds_l"][:, sl]
            cd = cm["cd_l"][:, sl]
            yoff = _dot(cmat, hg_b, 1, 0) * ea
            dw = (dyg * ea).astype(BF16)
            d_c = _dot(dw, hg_b, 1, 1)
            d_hprev = _dot(cmat, dw, 0, 0) + dhg * cd
            t_acs = dyg * yoff
            d_last_g = jnp.sum(dhg * hg, axis=0, keepdims=True) * cd
            xds_b = (xdt * ds).astype(BF16)
            dxds = _dot(bm, dhg_b, 1, 0)
            d_b = _dot(xds_b, dhg_b, 1, 1)
            dxdt = dxds * ds
            t_ds = dxds * xdt * ds
            t_acs = t_acs - t_ds
            d_last_g = d_last_g + jnp.sum(t_ds, axis=0, keepdims=True)
            cb = _dot(cmat, bm, 1, 1)
            d_cb = jnp.zeros((q, q), F32)
            for r in range(4):
                h = 4 * g + r
                dec = _decay(cm, h)
                s_h = cb * dec
                mask = _lane_mask(r)
                d_s = _dot(jnp.where(mask, dyg, 0.0).astype(BF16), xdt_b, 1, 1)
                dxdt = dxdt + jnp.where(mask, _dot(s_h.astype(BF16), dyg_b, 0, 0), 0.0)
                d_cb = d_cb + d_s * dec
                d_m = d_s * s_h
                d_acs = d_acs + jnp.where(lane128 == h, jnp.sum(d_m, axis=1, keepdims=True), 0.0)
                d_acs_t = d_acs_t + jnp.where(row128 == h, jnp.sum(d_m, axis=0, keepdims=True), 0.0)
            d_cb_b = d_cb.astype(BF16)
            d_c = d_c + _dot(d_cb_b, bm, 1, 0)
            d_b = d_b + _dot(d_cb_b, cmat, 0, 0)
            dstate[g] = d_hprev
            dx_ref[:, sl] = dxdt * cm["dt_l"][:, sl] + cm["dskip_l"][:, sl] * dyg
            dx_ref[:, D_SSD + g * N_STATE:D_SSD + (g + 1) * N_STATE] = d_b
            dx_ref[:, D_SSD + N_GROUPS * N_STATE + g * N_STATE:D_SSD + N_GROUPS * N_STATE + (g + 1) * N_STATE] = d_c
            acs_terms.append(t_acs)
            dxdt_all.append(dxdt * xs)
            last_terms.append(d_last_g)
        t_acs_l = jnp.concatenate(acs_terms, axis=1)
        d_dt_l = jnp.concatenate(dxdt_all, axis=1)
        d_last_l = jnp.concatenate(last_terms, axis=1)
        d_acs = d_acs + _exact_dot(t_acs_l, reduce) - d_acs_t.T
        last_row = lax.broadcasted_iota(jnp.int32, (q, 1), 0) == q - 1
        d_acs = d_acs + jnp.where(last_row, _exact_dot(jnp.broadcast_to(d_last_l, (8, D_SSD)), reduce)[0:1, :], 0.0)
        li = lax.broadcasted_iota(jnp.int32, (q, q), 0)
        si = lax.broadcasted_iota(jnp.int32, (q, q), 1)
        d_adt = _exact_dot((si >= li).astype(F32), d_acs)
        d_dt = _exact_dot(d_dt_l, reduce) + d_adt * cm["a"]
        d_dtr = d_dt * _sigmoid(cm["v"])
        ddt_ref[...] = d_dtr.astype(BF16)
        d_skip = _exact_dot(jnp.broadcast_to(jnp.sum(dy_ref[...] * x_ref[:, 0:D_SSD], axis=0, keepdims=True), (8, D_SSD)), reduce)[0:1, :]
        rows = jnp.concatenate([
            jnp.sum(d_dtr, axis=0, keepdims=True),
            jnp.sum(d_adt * cm["dt"], axis=0, keepdims=True) * cm["a"],
            d_skip,
            jnp.zeros((5, LANE), F32)], axis=0)

        @pl.when(step == 0)
        def _():
            st_ref[...] = rows

        @pl.when(step != 0)
        def _():
            st_ref[...] += rows

    rev = lambda c: nc - 1 - c
    return pl.pallas_call(
        body, name=name, grid=(nc,),
        in_specs=[pl.BlockSpec((q, D_SSD), lambda c: (rev(c), 0)),
                  pl.BlockSpec((q, D_XBC), lambda c: (rev(c), 0)),
                  pl.BlockSpec((q, LANE), lambda c: (rev(c), COL_DT // LANE)),
                  pl.BlockSpec((8, LANE), lambda c: (0, 0)),
                  pl.BlockSpec((1, N_GROUPS, N_STATE, GROUP_W), lambda c: (rev(c), 0, 0, 0))],
        out_specs=[pl.BlockSpec((q, D_XBC), lambda c: (rev(c), 0)),
                   pl.BlockSpec((q, LANE), lambda c: (rev(c), 0)),
                   pl.BlockSpec((8, LANE), lambda c: (0, 0))],
        out_shape=[jax.ShapeDtypeStruct((seq, D_XBC), F32),
                   jax.ShapeDtypeStruct((seq, LANE), BF16),
                   jax.ShapeDtypeStruct((8, LANE), F32)],
        scratch_shapes=[pltpu.VMEM((N_GROUPS, N_STATE, GROUP_W), F32)],
        compiler_params=_cp(("arbitrary",)))(dy, xbc, proj, par, hprev)


def _gate_norm_fwd(y, proj, wn, tm, name):
    seq = y.shape[0]

    def body(y_ref, z_ref, w_ref, o_ref):
        for g in range(N_GROUPS):
            sl = slice(g * GROUP_W, (g + 1) * GROUP_W)
            zv = z_ref[:, sl]
            yz = y_ref[:, sl] * (zv * _sigmoid(zv))
            r = lax.rsqrt(jnp.mean(yz * yz, axis=-1, keepdims=True) + EPS)
            o_ref[:, sl] = (yz * r * w_ref[:, sl]).astype(BF16)

    tok = pl.BlockSpec((tm, D_SSD), lambda i: (i, 0))
    return pl.pallas_call(
        body, name=name, grid=(seq // tm,),
        in_specs=[tok, tok, pl.BlockSpec((1, D_SSD), lambda i: (0, 0))],
        out_specs=tok,
        out_shape=jax.ShapeDtypeStruct((seq, D_SSD), BF16),
        compiler_params=_cp(("parallel",)))(y, proj, wn)


def _gate_norm_bwd(dys, y, proj, wn, tm, name):
    seq = y.shape[0]

    def body(d_ref, y_ref, z_ref, w_ref, dy_ref, dz_ref, st_ref):
        rows = []
        for g in range(N_GROUPS):
            sl = slice(g * GROUP_W, (g + 1) * GROUP_W)
            zv = z_ref[:, sl]
            yv = y_ref[:, sl]
            sg = _sigmoid(zv)
            sz = zv * sg
            yz = yv * sz
            r = lax.rsqrt(jnp.mean(yz * yz, axis=-1, keepdims=True) + EPS)
            yn = yz * r
            dv = d_ref[:, sl]
            dyn = dv * w_ref[:, sl]
            dyz = r * (dyn - yn * jnp.mean(dyn * yn, axis=-1, keepdims=True))
            dy_ref[:, sl] = dyz * sz
            dz_ref[:, sl] = (dyz * yv * _silu_grad(zv, sg)).astype(BF16)
            rows.append(jnp.sum(dv * yn, axis=0, keepdims=True))
        rows = jnp.concatenate([jnp.concatenate(rows, axis=1), jnp.zeros((7, D_SSD), F32)], axis=0)

        @pl.when(pl.program_id(0) == 0)
        def _():
            st_ref[...] = rows

        @pl.when(pl.program_id(0) != 0)
        def _():
            st_ref[...] += rows

    tok = pl.BlockSpec((tm, D_SSD), lambda i: (i, 0))
    return pl.pallas_call(
        body, name=name, grid=(seq // tm,),
        in_specs=[tok, tok, tok, pl.BlockSpec((1, D_SSD), lambda i: (0, 0))],
        out_specs=[tok, tok, pl.BlockSpec((8, D_SSD), lambda i: (0, 0))],
        out_shape=[jax.ShapeDtypeStruct((seq, D_SSD), F32), jax.ShapeDtypeStruct((seq, D_SSD), BF16),
                   jax.ShapeDtypeStruct((8, D_SSD), F32)],
        compiler_params=_cp(("arbitrary",)))(dys, y, proj, wn)


def _pool_counts(t0, rows, w):
    pos = (t0 + 1 + lax.broadcasted_iota(jnp.int32, (rows, 1), 0)).astype(F32)
    return jnp.minimum(pos, float(w))


def _window_means(ext, t0):
    n = ext.shape[0]
    outs = []
    run = ext
    width = 1
    sums = {}
    while width < 16:
        run = run + pltpu.roll(run, width, 0)
        width *= 2
        sums[width] = run
    for g, w in enumerate(POOL_WINDOWS):
        sl = slice(g * POOL_GW, (g + 1) * POOL_GW)
        cnt = _pool_counts(t0, n - HALO, w)
        outs.append(sums[w][HALO:, sl] / cnt - ext[HALO:, sl])
    return outs


def _pool_fwd(proj, pw, pb, ps, tm, name):
    seq = proj.shape[0]

    def body(u_ref, p_ref, pw_ref, pb_ref, ps_ref, o_ref):
        i = pl.program_id(0)
        prev = jnp.where(i == 0, 0.0, p_ref[...])
        ext = jnp.concatenate([prev, u_ref[...]], axis=0)
        diffs = _window_means(ext, i * tm)
        for g in range(4):
            sl = slice(g * POOL_GW, (g + 1) * POOL_GW)
            out = _dot(diffs[g].astype(BF16), pw_ref[g], 1, 0) + pb_ref[:, sl]
            o_ref[:, sl] = (out * ps_ref[:, sl]).astype(BF16)

    c0 = COL_U // 1024
    vec = pl.BlockSpec((1, D_POOL), lambda i: (0, 0))
    return pl.pallas_call(
        body, name=name, grid=(seq // tm,),
        in_specs=[pl.BlockSpec((tm, 1024), lambda i: (i, c0)),
                  pl.BlockSpec((HALO, 1024), lambda i: (jnp.maximum(i * (tm // HALO) - 1, 0), c0)),
                  pl.BlockSpec((4, POOL_GW, POOL_GW), lambda i: (0, 0, 0)), vec, vec],
        out_specs=pl.BlockSpec((tm, D_POOL), lambda i: (i, 0)),
        out_shape=jax.ShapeDtypeStruct((seq, D_POOL), BF16),
        compiler_params=_cp(("parallel",)))(proj, proj, pw, pb, ps)


def _pool_bwd(dyp, proj, pw, pb, ps, tm, name):
    seq = proj.shape[0]
    ni = seq // tm

    def body(d_ref, dn_ref, u_ref, p_ref, pw_ref, pb_ref, ps_ref, du_ref, dw_ref, st_ref):
        i = pl.program_id(0)
        prev = jnp.where(i == 0, 0.0, p_ref[...])
        ext = jnp.concatenate([prev, u_ref[...]], axis=0)
        diffs = _window_means(ext, i * tm)
        dnext = jnp.where(i == ni - 1, 0.0, dn_ref[...])
        dext = jnp.concatenate([d_ref[...], dnext], axis=0)
        n = tm + HALO
        b_rows, s_rows = [], []
        for g, w in enumerate(POOL_WINDOWS):
            sl = slice(g * POOL_GW, (g + 1) * POOL_GW)
            wg = pw_ref[g]
            dout = dext[:, sl] * ps_ref[:, sl]
            dcur = dout[:tm]
            pre = _dot(diffs[g].astype(BF16), wg, 1, 0) + pb_ref[:, sl]
            s_rows.append(jnp.sum(d_ref[:, sl] * pre, axis=0, keepdims=True))
            b_rows.append(jnp.sum(dcur, axis=0, keepdims=True))
            dwg = _dot(diffs[g].astype(BF16), dcur.astype(BF16), 0, 0)

            @pl.when(i == 0)
            def _():
                dw_ref[g] = dwg

            @pl.when(i != 0)
            def _():
                dw_ref[g] += dwg

            ddiff = _dot(dout.astype(BF16), wg, 1, 1)
            scaled = ddiff / _pool_counts(i * tm, n, w)
            run = scaled
            width = 1
            while width < w:
                run = run + pltpu.roll(run, n - width, 0)
                width *= 2
            du_ref[:, sl] = (run[:tm] - ddiff[:tm]).astype(BF16)
        rows = jnp.concatenate([jnp.concatenate(b_rows, axis=1), jnp.concatenate(s_rows, axis=1),
                                jnp.zeros((6, D_POOL), F32)], axis=0)

        @pl.when(i == 0)
        def _():
            st_ref[...] = rows

        @pl.when(i != 0)
        def _():
            st_ref[...] += rows

    c0 = COL_U // 1024
    vec = pl.BlockSpec((1, D_POOL), lambda i: (0, 0))
    last = seq // HALO - 1
    return pl.pallas_call(
        body, name=name, grid=(ni,),
        in_specs=[pl.BlockSpec((tm, D_POOL), lambda i: (i, 0)),
                  pl.BlockSpec((HALO, D_POOL), lambda i: (jnp.minimum((i + 1) * (tm // HALO), last), 0)),
                  pl.BlockSpec((tm, 1024), lambda i: (i, c0)),
                  pl.BlockSpec((HALO, 1024), lambda i: (jnp.maximum(i * (tm // HALO) - 1, 0), c0)),
                  pl.BlockSpec((4, POOL_GW, POOL_GW), lambda i: (0, 0, 0)), vec, vec],
        out_specs=[pl.BlockSpec((tm, D_POOL), lambda i: (i, 0)),
                   pl.BlockSpec((4, POOL_GW, POOL_GW), lambda i: (0, 0, 0)),
                   pl.BlockSpec((8, D_POOL), lambda i: (0, 0))],
        out_shape=[jax.ShapeDtypeStruct((seq, D_POOL), BF16),
                   jax.ShapeDtypeStruct((4, POOL_GW, POOL_GW), F32),
                   jax.ShapeDtypeStruct((8, D_POOL), F32)],
        compiler_params=_cp(("arbitrary",)))(dyp, dyp, proj, proj, pw, pb, ps)


def _mix_out(ys, yp, wout, x1, mod, tm, name):
    seq = ys.shape[0]

    def body(ys_ref, yp_ref, w_ref, x_ref, mod_ref, xo_ref, m_ref):
        mix = _dot(ys_ref[...], w_ref[0:D_SSD, :], 1, 0) + _dot(yp_ref[...], w_ref[D_SSD:2 * D_SSD, :], 1, 0)
        m_ref[...] = mix.astype(BF16)
        xo_ref[...] = x_ref[...] + mod_ref[5:6, :] * mix

    tok = pl.BlockSpec((tm, D), lambda i: (i, 0))
    return pl.pallas_call(
        body, name=name, grid=(seq // tm,),
        in_specs=[tok, tok, pl.BlockSpec((2 * D_SSD, D), lambda i: (0, 0)), tok,
                  pl.BlockSpec((9, D), lambda i: (0, 0))],
        out_specs=[tok, tok],
        out_shape=[jax.ShapeDtypeStruct((seq, D), F32), jax.ShapeDtypeStruct((seq, D), BF16)],
        compiler_params=_cp(("parallel",)))(ys, yp, wout, x1, mod)


def _mix_bwd_dh(dz, dxbc, du, ddt, win_t, x1, dx2, mixb, wn, mod, tm, name):
    seq = x1.shape[0]

    def body(dz_ref, dx_ref, du_ref, ddt_ref, w_ref, x_ref, dxo_ref, m_ref, wn_ref, mod_ref, o_ref, st_ref):
        dh = (_dot(dz_ref[...], w_ref[COL_Z:COL_Z + 1024, :], 1, 0)
              + _dot(dx_ref[...], w_ref[COL_XBC:COL_XBC + D_XBC, :], 1, 0)
              + _dot(du_ref[...], w_ref[COL_U:COL_U + 1024, :], 1, 0)
              + _dot(ddt_ref[...], w_ref[COL_DT:COL_DT + LANE, :], 1, 0))
        o_ref[...] = _norm_bwd(dh, x_ref[...], dxo_ref[...], m_ref[...].astype(F32), wn_ref[...],
                               mod_ref[4:5, :], 1.0, st_ref, pl.program_id(0) == 0)

    tok = pl.BlockSpec((tm, D), lambda i: (i, 0))
    return pl.pallas_call(
        body, name=name, grid=(seq // tm,),
        in_specs=[tok, pl.BlockSpec((tm, D_XBC), lambda i: (i, 0)), tok,
                  pl.BlockSpec((tm, LANE), lambda i: (i, 0)),
                  pl.BlockSpec((D_IN_PAD, D), lambda i: (0, 0)),
                  tok, tok, tok,
                  pl.BlockSpec((1, D), lambda i: (0, 0)),
                  pl.BlockSpec((9, D), lambda i: (0, 0))],
        out_specs=[tok, pl.BlockSpec((8, D), lambda i: (0, 0))],
        out_shape=[jax.ShapeDtypeStruct((seq, D), F32), jax.ShapeDtypeStruct((8, D), F32)],
        compiler_params=_cp(("arbitrary",)))(dz, dxbc, du, ddt, win_t, x1, dx2, mixb, wn, mod)


def _mix_bwd_dycat(dmix, wout, tm, name):
    seq = dmix.shape[0]

    def body(d_ref, w_ref, a_ref, b_ref):
        dv = d_ref[...]
        a_ref[...] = _dot(dv, w_ref[0:D_SSD, :], 1, 1)
        b_ref[...] = _dot(dv, w_ref[D_SSD:2 * D_SSD, :], 1, 1)

    tok = pl.BlockSpec((tm, D), lambda i: (i, 0))
    return pl.pallas_call(
        body, name=name, grid=(seq // tm,),
        in_specs=[tok, pl.BlockSpec((2 * D_SSD, D), lambda i: (0, 0))],
        out_specs=[tok, tok],
        out_shape=[jax.ShapeDtypeStruct((seq, D), F32)] * 2,
        compiler_params=_cp(("parallel",)))(dmix, wout)


def _local_step(x, tgt, mod, wff1, wff2, win_t, wout, pool_w, vecs, tm):
    seq = x.shape[0]
    tk = min(seq, 512)
    x1, s1 = _ffn_forward(x, wff1, vecs["ffn1_norm"], mod, 0, tm, "ffn1")
    h2 = _prenorm(x1, vecs["mix_norm"], mod, 1, tm, "mix_prenorm")
    proj = _mm_nt(h2, win_t, tm, D_IN_PAD // 3, F32, "mix_in_proj")
    xbc = _conv_fwd(proj, vecs["conv_w"], vecs["conv_b"], tm, "mix_conv")
    y, hprev = _ssd_fwd(xbc, proj, vecs["ssd_par"], "mix_ssd")
    ys = _gate_norm_fwd(y, proj, vecs["ssd_norm_w"], tm, "mix_gate_norm")
    yp = _pool_fwd(proj, pool_w, vecs["pool_b"], vecs["pool_scale"], tm, "mix_pool")
    x2, mixb = _mix_out(ys, yp, wout, x1, mod, tm, "mix_out_proj")
    x3, s3 = _ffn_forward(x2, wff2, vecs["ffn2_norm"], mod, 2, tm, "ffn2")
    dx3, st_loss = _loss_head(x3, vecs["final_norm"], tgt, tm, "loss_head")

    dx2, st3, dw3 = _ffn_backward(dx3, s3, wff2, vecs["ffn2_norm"], mod, 2, tm, "ffn2")
    dmix = _scale_cast(dx2, mod, 5, 1.0, tm, "mix_dbranch")
    dys, dyp = _mix_bwd_dycat(dmix, wout, tm, "mix_bwd_dycat")
    d_wout = (_mm_tn(ys, dmix, D_SSD, tk, "mix_dw_out_ssd"), _mm_tn(yp, dmix, D_POOL, tk, "mix_dw_out_pool"))
    du, d_pool_w, st_pool = _pool_bwd(dyp, proj, pool_w, vecs["pool_b"], vecs["pool_scale"], tm, "mix_pool_bwd")
    dy, dz, st_gn = _gate_norm_bwd(dys, y, proj, vecs["ssd_norm_w"], tm, "mix_gate_norm_bwd")
    dxbc_act, ddt, st_ssd = _ssd_bwd(dy, xbc, proj, vecs["ssd_par"], hprev, "mix_ssd_bwd")
    dxbc, st_conv = _conv_bwd(dxbc_act, proj, vecs["conv_w"], vecs["conv_b"], tm, "mix_conv_bwd")
    dx1, st2 = _mix_bwd_dh(dz, dxbc, du, ddt, win_t, x1, dx2, mixb, vecs["mix_norm"], mod, min(tm, 256), "mix_bwd_dh")
    d_win = (_mm_tn(dz, h2, 1024, tk, "mix_dw_in_z"), _mm_tn(dxbc, h2, 1024, tk, "mix_dw_in_xbc"),
             _mm_tn(du, h2, 1024, tk, "mix_dw_in_u"), _mm_tn(ddt, h2, LANE, tk, "mix_dw_in_dt"))
    dx0, st1, dw1 = _ffn_backward(dx1, s1, wff1, vecs["ffn1_norm"], mod, 0, tm, "ffn1")
    stats = dict(ffn1=st1, mix=st2, ffn2=st3, loss=st_loss, pool=st_pool, gn=st_gn, ssd=st_ssd, conv=st_conv)
    return dx0, stats, dw1, dw3, d_win, d_wout, d_pool_w


HBM_SPEC = pl.BlockSpec(memory_space=pltpu.HBM)


def _mesh_pos():
    return lax.axis_index("x"), lax.axis_index("y"), lax.axis_index("c")


def _other_chips(x, y):
    return [(1 - x, y), (x, 1 - y), (1 - x, 1 - y)]


def _all_gather(src, regions, name):
    total, cols = src.shape
    assert sum(r for _, r in regions) == total
    body = _all_gather_body(regions, total, False)
    return pl.pallas_call(
        body, name=name,
        out_shape=jax.ShapeDtypeStruct((N_DEV * total, cols), src.dtype),
        in_specs=[HBM_SPEC], out_specs=HBM_SPEC,
        scratch_shapes=[pltpu.SemaphoreType.DMA((7,)), pltpu.SemaphoreType.DMA((7,)), pltpu.SemaphoreType.DMA],
    )(src)


def _all_gather_async(src, regions, name, collective_id):
    total, cols = src.shape
    assert sum(r for _, r in regions) == total
    src_ref = jax.new_ref(src, memory_space=pltpu.MemorySpace.HBM)
    out_ref = jax.empty_ref(jax.ShapeDtypeStruct((N_DEV * total, cols), src.dtype), memory_space=pltpu.MemorySpace.HBM)
    body = _all_gather_body(regions, total, True)

    @pl.kernel(mesh=plsc.ScalarSubcoreMesh(axis_name="seq", num_cores=1), name=name,
               scratch_types=(pltpu.SemaphoreType.DMA((7,)), pltpu.SemaphoreType.DMA((7,)), pltpu.SemaphoreType.DMA),
               compiler_params=pltpu.CompilerParams(collective_id=collective_id))
    def launch(send_sems, recv_sems, local_sem):
        body(src_ref, out_ref, send_sems, recv_sems, local_sem)

    launch()
    return out_ref[...]


def _all_gather_body(regions, total, handshake):
    def body(src_ref, out_ref, send_sems, recv_sems, local_sem):
        x, y, c = _mesh_pos()
        me, sibling = (x, y, c), (x, y, 1 - c)
        chips = _other_chips(x, y)
        if handshake:
            barrier = pltpu.get_barrier_semaphore()
            for peer in [sibling] + [(*chip, c) for chip in chips]:
                pl.semaphore_signal(barrier, inc=1, device_id=peer, device_id_type=MESH)
            pl.semaphore_wait(barrier, 4)

        def rows_of(dev, off, rows):
            start = pl.multiple_of(N_DEV * off + (4 * dev[0] + 2 * dev[1] + dev[2]) * rows, 8)
            return out_ref.at[pl.ds(start, rows), :]

        def copies(k, block, to, from_src):
            out = []
            for off, rows in regions:
                dst = rows_of(block, off, rows)
                out.append(pltpu.make_async_remote_copy(
                    src_ref=src_ref.at[pl.ds(off, rows), :] if from_src else dst, dst_ref=dst,
                    send_sem=send_sems.at[k], recv_sem=recv_sems.at[k], device_id=to, device_id_type=MESH))
            return out

        def drain(k):
            whole = out_ref.at[pl.ds(0, total), :]
            return pltpu.make_async_remote_copy(src_ref=whole, dst_ref=whole, send_sem=send_sems.at[k],
                                                recv_sem=recv_sems.at[k], device_id=me, device_id_type=MESH)

        for off, rows in regions:
            pltpu.make_async_copy(src_ref.at[pl.ds(off, rows), :], rows_of(me, off, rows), local_sem).start()
        first = copies(0, me, sibling, True)
        for j, chip in enumerate(chips):
            first += copies(1 + j, me, (*chip, c), True)
        for cp in first:
            cp.start()
        for j, chip in enumerate(chips):
            drain(1 + j).wait_recv()
            for cp in copies(4 + j, (*chip, c), sibling, False):
                cp.start()
        drain(0).wait_recv()
        for j in range(3):
            drain(4 + j).wait_recv()
        for k in range(7):
            drain(k).wait_send()
        pltpu.make_async_copy(src_ref, out_ref.at[pl.ds(0, total), :], local_sem).wait()

    return body


def _rs_pair(grads, total, name, collective_id):
    cols = grads[0][0].shape[1]
    sent = sum(rows for _, _, rows in grads)
    g_refs = [jax.new_ref(g, memory_space=pltpu.MemorySpace.HBM) for g, _, _ in grads]
    recv_ref = jax.empty_ref(jax.ShapeDtypeStruct((4, total, cols), F32), memory_space=pltpu.MemorySpace.HBM)

    @pl.kernel(mesh=plsc.ScalarSubcoreMesh(axis_name="seq", num_cores=1), name=name,
               scratch_types=(pltpu.SemaphoreType.DMA, pltpu.SemaphoreType.DMA),
               compiler_params=pltpu.CompilerParams(collective_id=collective_id))
    def launch(send_sem, recv_sem):
        x, y, c = _mesh_pos()
        sibling = (x, y, 1 - c)
        barrier = pltpu.get_barrier_semaphore()
        pl.semaphore_signal(barrier, inc=1, device_id=sibling, device_id_type=MESH)
        pl.semaphore_wait(barrier, 1)
        for q in range(4):
            for g_ref, (_, off, rows) in zip(g_refs, grads):
                theirs = g_ref.at[pl.ds(pl.multiple_of((2 * q + 1 - c) * rows, 8), rows), :]
                pltpu.make_async_remote_copy(
                    src_ref=theirs, dst_ref=recv_ref.at[q, pl.ds(off, rows), :], send_sem=send_sem, recv_sem=recv_sem,
                    device_id=sibling, device_id_type=MESH).start()
        everything = recv_ref.at[:, pl.ds(0, sent), :]
        whole = pltpu.make_async_remote_copy(src_ref=everything, dst_ref=everything, send_sem=send_sem,
                                             recv_sem=recv_sem, device_id=sibling, device_id_type=MESH)
        whole.wait_send()
        whole.wait_recv()

    launch()
    return recv_ref[...]


def _pair_sum(g, from_sibling, off, rows, pos, name):
    cols = g.shape[1]

    def body(pos_ref, g_ref, r_ref, o_ref, ob_ref):
        s = g_ref[...] + r_ref[...]
        o_ref[...] = s
        ob_ref[...] = s.astype(BF16)

    out = pl.BlockSpec((None, rows, cols), lambda q, pos_ref: (q, 0, 0))
    return pl.pallas_call(
        body, name=name,
        grid_spec=pltpu.PrefetchScalarGridSpec(
            num_scalar_prefetch=1, grid=(4,),
            in_specs=[pl.BlockSpec((None, None, rows, cols), lambda q, pos_ref: (q, pos_ref[0], 0, 0)),
                      pl.BlockSpec((None, rows, cols), lambda q, pos_ref: (q, off // rows, 0))],
            out_specs=[out, out]),
        out_shape=[jax.ShapeDtypeStruct((4, rows, cols), F32), jax.ShapeDtypeStruct((4, rows, cols), BF16)],
        compiler_params=_cp(("parallel",)))(pos, g.reshape(4, 2, rows, cols), from_sibling)


def _rs_chips(parts, total, name, collective_id):
    cols = parts[0][0].shape[2]
    sent = sum(rows for _, _, rows in parts)
    p_refs = [jax.new_ref(p, memory_space=pltpu.MemorySpace.HBM) for p, _, _ in parts]
    out_ref = jax.empty_ref(jax.ShapeDtypeStruct((3, total, cols), BF16), memory_space=pltpu.MemorySpace.HBM)

    @pl.kernel(mesh=plsc.ScalarSubcoreMesh(axis_name="seq", num_cores=1), name=name,
               scratch_types=(pltpu.SemaphoreType.DMA((3,)), pltpu.SemaphoreType.DMA((3,))),
               compiler_params=pltpu.CompilerParams(collective_id=collective_id))
    def launch(send_sems, recv_sems):
        x, y, c = _mesh_pos()
        chips = _other_chips(x, y)
        barrier = pltpu.get_barrier_semaphore()
        for chip in chips:
            pl.semaphore_signal(barrier, inc=1, device_id=(*chip, c), device_id_type=MESH)
        pl.semaphore_wait(barrier, 3)
        for j, chip in enumerate(chips):
            q = 2 * chip[0] + chip[1]
            for p_ref, (_, off, rows) in zip(p_refs, parts):
                pltpu.make_async_remote_copy(
                    src_ref=p_ref.at[q], dst_ref=out_ref.at[j, pl.ds(off, rows), :], send_sem=send_sems.at[j],
                    recv_sem=recv_sems.at[j], device_id=(*chip, c), device_id_type=MESH).start()
        for j, chip in enumerate(chips):
            everything = out_ref.at[j, pl.ds(0, sent), :]
            whole = pltpu.make_async_remote_copy(src_ref=everything, dst_ref=everything, send_sem=send_sems.at[j],
                                                 recv_sem=recv_sems.at[j], device_id=(*chip, c), device_id_type=MESH)
            whole.wait_recv()
            whole.wait_send()

    launch()
    return out_ref[...]


def _chip_sum(p, from_chips, off, rows, pos, name):
    cols = p.shape[2]

    def body(pos_ref, p_ref, r_ref, o_ref):
        acc = p_ref[...]
        for j in range(3):
            acc = acc + r_ref[j].astype(F32)
        o_ref[...] = acc

    return pl.pallas_call(
        body, name=name,
        grid_spec=pltpu.PrefetchScalarGridSpec(
            num_scalar_prefetch=1, grid=(1,),
            in_specs=[pl.BlockSpec((None, rows, cols), lambda i, pos_ref: (pos_ref[1], 0, 0)),
                      pl.BlockSpec((3, rows, cols), lambda i, pos_ref: (0, off // rows, 0))],
            out_specs=pl.BlockSpec((rows, cols), lambda i, pos_ref: (0, 0))),
        out_shape=jax.ShapeDtypeStruct((rows, cols), F32),
        compiler_params=_cp(("arbitrary",)))(pos, p, from_chips)


def _row_tile(rows, cap):
    t = min(rows, cap)
    while rows % t or t % 8:
        t -= 8
    return t


def _ada_mod(c_all, w, b, name):
    n = w.shape[1]

    def body(c_ref, w_ref, b_ref, o_ref):
        cv = c_ref[...]
        o_ref[...] = _exact_dot(cv * _sigmoid(cv), w_ref[...]) + b_ref[...]

    return pl.pallas_call(body, name=name, out_shape=jax.ShapeDtypeStruct((N_DEV, n), F32),
                          compiler_params=pltpu.CompilerParams(vmem_limit_bytes=VMEM_LIMIT))(c_all, w, b)


def _ada_grad(c_all, dmod, name):
    n = dmod.shape[1]

    def body(c_ref, d_ref, o_ref):
        cv = c_ref[...]
        o_ref[...] = _dot(cv * _sigmoid(cv), d_ref[...], 0, 0, lax.Precision.HIGHEST)

    return pl.pallas_call(body, name=name, out_shape=jax.ShapeDtypeStruct((D, n), F32),
                          compiler_params=pltpu.CompilerParams(vmem_limit_bytes=VMEM_LIMIT))(c_all, dmod)


def _adamw(w, g, m, v, name):
    rows, cols = w.shape
    tr = _row_tile(rows, 256) if rows % 8 == 0 else rows
    c1 = 1.0 - ADAM_B1 ** ADAM_STEP
    c2 = 1.0 - ADAM_B2 ** ADAM_STEP

    def body(w_ref, g_ref, m_ref, v_ref, d_ref, mo_ref, vo_ref):
        gv = g_ref[...]
        mn = ADAM_B1 * m_ref[...] + (1.0 - ADAM_B1) * gv
        vn = ADAM_B2 * v_ref[...] + (1.0 - ADAM_B2) * (gv * gv)
        mo_ref[...] = mn
        vo_ref[...] = vn
        d_ref[...] = -ADAM_LR * ((mn / c1) / (jnp.sqrt(vn / c2) + ADAM_EPS) + ADAM_WD * w_ref[...])

    spec = pl.BlockSpec((tr, cols), lambda i: (i, 0))
    shape = jax.ShapeDtypeStruct((rows, cols), F32)
    return pl.pallas_call(body, name=name, grid=(rows // tr,), in_specs=[spec] * 4, out_specs=[spec] * 3,
                          out_shape=[shape] * 3, compiler_params=_cp(("parallel",)))(w, g, m, v)


def _sum8_loss(v, loss_row, name):
    rows = v.shape[0] // N_DEV

    def body(v_ref, o_ref, l_ref):
        acc = v_ref[0:rows, :]
        for k in range(1, N_DEV):
            acc = acc + v_ref[k * rows:(k + 1) * rows, :]
        o_ref[...] = acc
        part = jnp.sum(acc[loss_row:loss_row + 8, :], axis=0, keepdims=True)
        l_ref[...] = jnp.broadcast_to(jnp.sum(part, axis=1, keepdims=True), (8, LANE))

    return pl.pallas_call(body, name=name,
                          out_shape=[jax.ShapeDtypeStruct((rows, LANE), F32), jax.ShapeDtypeStruct((8, LANE), F32)],
                          compiler_params=pltpu.CompilerParams(vmem_limit_bytes=VMEM_LIMIT))(v)


WEIGHT_NAMES = ("w_ada", "b_ada", "ffn1_norm", "ffn1_w_gate", "ffn1_w_up", "ffn1_w_down", "mix_norm", "w_in",
                "conv_w", "conv_b", "dt_bias", "a_log", "d_skip", "ssd_norm_w", "pool_w", "pool_b", "pool_scale",
                "w_out", "ffn2_norm", "ffn2_w_gate", "ffn2_w_up", "ffn2_w_down", "final_norm")

FF_SHARD = FF // N_DEV
IN_SHARD = D_IN // N_DEV
IN_SHARD_PAD = 528
OUT_SHARD = 2 * D_SSD // N_DEV
ADA_SHARD = 9 * D // N_DEV
POOL_SHARD_ROWS = 4 * 32 * POOL_GW // D
PACK = dict(gate1=(0, FF_SHARD), up1=(352, FF_SHARD), down1=(704, FF_SHARD), gate2=(1056, FF_SHARD),
            up2=(1408, FF_SHARD), down2=(1760, FF_SHARD), w_out=(2112, OUT_SHARD), w_in=(2368, IN_SHARD_PAD),
            pool_w=(2896, POOL_SHARD_ROWS))
PACK_W_ROWS = 2896
GPACK = dict(w_in=(0, IN_SHARD_PAD), w_out=(768, OUT_SHARD), pool_w=(1024, POOL_SHARD_ROWS),
             gate1=(0, FF_SHARD), up1=(352, FF_SHARD), down1=(704, FF_SHARD),
             gate2=(0, FF_SHARD), up2=(352, FF_SHARD), down2=(704, FF_SHARD))
GROUP_ROWS = 1056

SMALL_ROWS = dict(dmod=(0, 72), ffn1_norm=(72, 8), mix_norm=(80, 8), ffn2_norm=(88, 8), final_norm=(96, 8),
                  ssd_norm_w=(104, 8), pool_scale=(112, 8), conv_b=(120, 16), conv_w=(136, 64), pool_b=(200, 8),
                  ssd=(208, 3), loss=(216, 8))
SMALL_TOTAL = 224


def _rows128(v, rows):
    flat = v.reshape(-1)
    return jnp.pad(flat, (0, rows * LANE - flat.shape[0])).reshape(rows, LANE)


def _pad_lanes(v):
    return jnp.pad(v.reshape(-1), (0, LANE - v.size))


def kernel(x, c, w_ada, b_ada, ffn1_norm, ffn1_w_gate, ffn1_w_up, ffn1_w_down, mix_norm, w_in, conv_w, conv_b, dt_bias, a_log, d_skip, ssd_norm_w, pool_w, pool_b, pool_scale, w_out, ffn2_norm, ffn2_w_gate, ffn2_w_up, ffn2_w_down, final_norm, loss_target, m_w_ada, m_b_ada, m_ffn1_norm, m_ffn1_w_gate, m_ffn1_w_up, m_ffn1_w_down, m_mix_norm, m_w_in, m_conv_w, m_conv_b, m_dt_bias, m_a_log, m_d_skip, m_ssd_norm_w, m_pool_w, m_pool_b, m_pool_scale, m_w_out, m_ffn2_norm, m_ffn2_w_gate, m_ffn2_w_up, m_ffn2_w_down, m_final_norm, v_w_ada, v_b_ada, v_ffn1_norm, v_ffn1_w_gate, v_ffn1_w_up, v_ffn1_w_down, v_mix_norm, v_w_in, v_conv_w, v_conv_b, v_dt_bias, v_a_log, v_d_skip, v_ssd_norm_w, v_pool_w, v_pool_b, v_pool_scale, v_w_out, v_ffn2_norm, v_ffn2_w_gate, v_ffn2_w_up, v_ffn2_w_down, v_final_norm):
    given = dict(locals())
    w = {n: given[n] for n in WEIGHT_NAMES}
    m = {n: given["m_" + n] for n in WEIGHT_NAMES}
    v = {n: given["v_" + n] for n in WEIGHT_NAMES}
    mx, my, mc = _mesh_pos()
    me = 4 * mx + 2 * my + mc

    small = jnp.concatenate([c.reshape(-1), conv_w.reshape(-1), pool_b.reshape(-1), pool_w.reshape(-1)])
    small_rows = 280
    gs = _all_gather(_rows128(small, small_rows), [(0, small_rows)], "ag_small").reshape(N_DEV, small_rows * LANE)
    c_all = gs[:, 0:D]
    conv_w_full = gs[:, 1024:2048].reshape(N_DEV, 4, 256).transpose(1, 0, 2).reshape(4, D_XBC)
    pool_b_full = gs[:, 2048:2176].reshape(N_DEV, 4, 32).transpose(1, 0, 2).reshape(1, D_POOL)
    pool_w_full = gs[:, 2176:2176 + 32768].reshape(N_DEV, 4, 32, POOL_GW).transpose(1, 0, 2, 3).reshape(4, POOL_GW, POOL_GW).astype(BF16)

    b_ada_cols = lax.dynamic_slice(b_ada, (0, me * ADA_SHARD), (1, ADA_SHARD))
    mod_part = _ada_mod(c_all, w_ada[0], b_ada_cols, "ada_mod")
    mod_all = _all_gather(mod_part, [(0, N_DEV)], "ag_mod").reshape(N_DEV, N_DEV, ADA_SHARD)
    mod = lax.dynamic_index_in_dim(mod_all, me, axis=1, keepdims=False).reshape(9, D)

    win_t_shard = jnp.pad(w_in[0].T, ((0, IN_SHARD_PAD - IN_SHARD), (0, 0)))
    pack_a = jnp.concatenate([ffn1_w_gate[0].T, ffn1_w_up[0].T], axis=0).astype(BF16)
    pack_d = ffn1_w_down[0].astype(BF16)
    pack_b = jnp.concatenate([ffn2_w_gate[0].T, ffn2_w_up[0].T, ffn2_w_down[0], w_out[0], win_t_shard], axis=0).astype(BF16)
    pack_a, pack_d, pack_b, mod = lax.optimization_barrier((pack_a, pack_d, pack_b, mod))
    ffn_regions = [(0, FF_SHARD), (FF_SHARD, FF_SHARD), (2 * FF_SHARD, FF_SHARD)]
    full_a = _all_gather_async(pack_a, ffn_regions[0:2], "ag_weights_ffn1_in", 1)
    full_d = _all_gather_async(pack_d, ffn_regions[0:1], "ag_weights_ffn1_out", 2)
    full_b = _all_gather_async(pack_b, ffn_regions + [(3 * FF_SHARD, OUT_SHARD), (3 * FF_SHARD + OUT_SHARD, IN_SHARD_PAD)],
                               "ag_weights_rest", 9)
    o_out = N_DEV * 3 * FF_SHARD
    o_in = o_out + N_DEV * OUT_SHARD
    wout_full = full_b[o_out:o_in]
    win_g = full_b[o_in:o_in + N_DEV * IN_SHARD_PAD].reshape(N_DEV, IN_SHARD_PAD, D)[:, :IN_SHARD].reshape(D_IN, D)
    win_t = jnp.concatenate([win_g[0:1024], win_g[1024:3072], win_g[3088:4112], win_g[3072:3088],
                             jnp.zeros((D_IN_PAD - D_IN, D), BF16)], axis=0)

    vecs = dict(ffn1_norm=ffn1_norm, mix_norm=mix_norm, ffn2_norm=ffn2_norm, final_norm=final_norm.reshape(1, D),
                conv_w=conv_w_full, conv_b=conv_b, ssd_norm_w=ssd_norm_w, pool_b=pool_b_full, pool_scale=pool_scale,
                ssd_par=jnp.concatenate([_pad_lanes(dt_bias)[None], _pad_lanes(a_log)[None], _pad_lanes(d_skip)[None],
                                         jnp.zeros((5, LANE), F32)], axis=0))
    dx0, st, dw1, dw3, d_win, d_wout, d_pool_w = _local_step(
        x[0], loss_target[0], mod, (full_a, full_d, 0), (full_b, full_b, 2), win_t, wout_full, pool_w_full, vecs,
        min(512, x.shape[1]))

    dwin = jnp.concatenate([d_win[0], d_win[1], d_win[3][0:16], d_win[2]], axis=0)
    dwin = jnp.pad(dwin.reshape(N_DEV, IN_SHARD, D), ((0, 0), (0, IN_SHARD_PAD - IN_SHARD), (0, 0))).reshape(N_DEV * IN_SHARD_PAD, D)
    dwout = jnp.concatenate(d_wout, axis=0)
    dpool = d_pool_w.reshape(4, N_DEV, 32, POOL_GW).transpose(1, 0, 2, 3).reshape(N_DEV * POOL_SHARD_ROWS, D)
    pos = jnp.stack([mc, 2 * mx + my]).astype(jnp.int32)
    by_key = dict(zip(("gate1", "up1", "down1", "gate2", "up2", "down2", "w_out", "w_in", "pool_w"),
                      (*dw1, *dw3, dwout, dwin, dpool)))
    reduced = {}
    for tag, keys, cid in (("ffn2", ("gate2", "up2", "down2"), 3), ("mix", ("w_in", "w_out", "pool_w"), 5),
                           ("ffn1", ("gate1", "up1", "down1"), 7)):
        grads = [(by_key[k], *GPACK[k]) for k in keys]
        from_sibling = _rs_pair(grads, GROUP_ROWS, f"rs_pair_{tag}", cid)
        pairs = {k: _pair_sum(g, from_sibling, off, rows, pos, f"rs_pair_sum_{k}") for (g, off, rows), k in zip(grads, keys)}
        from_chips = _rs_chips([(pairs[k][1], *GPACK[k]) for k in keys], GROUP_ROWS, f"rs_chips_{tag}", cid + 1)
        for k in keys:
            reduced[k] = _chip_sum(pairs[k][0], from_chips, *GPACK[k], pos, f"rs_chip_sum_{k}")

    def shard(k, rows=None):
        return reduced[k] if rows is None else reduced[k][0:rows]

    dmod = jnp.concatenate([st["ffn1"][0:3], st["mix"][0:3], st["ffn2"][0:3]], axis=0)
    sg = jnp.concatenate([
        dmod.reshape(-1), st["ffn1"][3], st["mix"][3], st["ffn2"][3], st["loss"][0], st["gn"][0], st["pool"][1],
        st["conv"][4], st["conv"][0:4].reshape(-1), st["pool"][0], st["ssd"][0:3].reshape(-1),
        jnp.zeros((5 * LANE,), F32), st["loss"][1]])
    sg_all = _all_gather(sg.reshape(SMALL_TOTAL, LANE), [(0, SMALL_TOTAL)], "ag_small_grads")
    tot, loss_b = _sum8_loss(sg_all, SMALL_ROWS["loss"][0], "small_sum")
    loss = loss_b[0, 0]
    dmod_all = sg_all.reshape(N_DEV, SMALL_TOTAL * LANE)[:, 0:9 * D]
    g_w_ada = _ada_grad(c_all, lax.dynamic_slice(dmod_all, (0, me * ADA_SHARD), (N_DEV, ADA_SHARD)), "ada_grad")

    def tot_rows(k):
        off, n = SMALL_ROWS[k]
        return tot[off:off + n].reshape(-1)

    g_conv_w = lax.dynamic_slice(tot_rows("conv_w").reshape(4, D_XBC), (0, me * 256), (4, 256))
    g_pool_b = lax.dynamic_slice(tot_rows("pool_b").reshape(4, POOL_GW), (0, me * 32), (4, 32))
    g_ssd = tot_rows("ssd").reshape(3, LANE)
    grad = {
        "w_ada": g_w_ada[None], "b_ada": tot_rows("dmod").reshape(1, 9 * D),
        "ffn1_norm": tot_rows("ffn1_norm")[None], "mix_norm": tot_rows("mix_norm")[None],
        "ffn2_norm": tot_rows("ffn2_norm")[None], "final_norm": tot_rows("final_norm"),
        "ssd_norm_w": tot_rows("ssd_norm_w")[None], "pool_scale": tot_rows("pool_scale")[None],
        "conv_b": tot_rows("conv_b")[None], "conv_w": g_conv_w[None], "pool_b": g_pool_b[None],
        "dt_bias": g_ssd[0:1, 0:N_HEADS], "a_log": g_ssd[1:2, 0:N_HEADS], "d_skip": g_ssd[2:3, 0:N_HEADS],
        "ffn1_w_gate": shard("gate1").T[None], "ffn1_w_up": shard("up1").T[None], "ffn1_w_down": shard("down1")[None],
        "ffn2_w_gate": shard("gate2").T[None], "ffn2_w_up": shard("up2").T[None], "ffn2_w_down": shard("down2")[None],
        "w_out": shard("w_out")[None], "w_in": shard("w_in", IN_SHARD).T[None],
        "pool_w": shard("pool_w").reshape(1, 4, 32, POOL_GW),
    }

    big = ("w_ada", "ffn1_w_gate", "ffn1_w_up", "ffn1_w_down", "w_in", "pool_w", "w_out",
           "ffn2_w_gate", "ffn2_w_up", "ffn2_w_down")
    delta, new_m, new_v = {}, {}, {}
    for n in big:
        shp = w[n].shape
        two_d = (shp[-3] * shp[-2], shp[-1]) if n == "pool_w" else shp[-2:]
        d_, m_, v_ = _adamw(w[n].reshape(two_d), grad[n].reshape(two_d), m[n].reshape(two_d), v[n].reshape(two_d),
                            f"adamw_{n}")
        delta[n], new_m[n], new_v[n] = d_.reshape(shp), m_.reshape(shp), v_.reshape(shp)
    small_names = [n for n in WEIGHT_NAMES if n not in big]
    sizes = [LANE if w[n].size < LANE else w[n].size for n in small_names]
    small_rows_adam = -(-sum(sizes) // (8 * LANE)) * 8

    def pack_small(t):
        return _rows128(jnp.concatenate([_pad_lanes(t[n]) if t[n].size < LANE else t[n].reshape(-1) for n in small_names]),
                        small_rows_adam)

    d_s, m_s, v_s = _adamw(pack_small(w), pack_small(grad), pack_small(m), pack_small(v), "adamw_small")
    off = 0
    for n, size in zip(small_names, sizes):
        for res, packed in ((delta, d_s), (new_m, m_s), (new_v, v_s)):
            res[n] = packed.reshape(-1)[off:off + w[n].size].reshape(w[n].shape)
        off += size

    return (loss, dx0[None], *[grad[n] for n in WEIGHT_NAMES], *[delta[n] for n in WEIGHT_NAMES],
            *[new_m[n] for n in WEIGHT_NAMES], *[new_v[n] for n in WEIGHT_NAMES])
```

```python
import functools
import math

import jax
import jax.numpy as jnp
from jax import lax
from jax.experimental import pallas as pl
from jax.experimental.pallas import tpu as pltpu
from jax.experimental.pallas import tpu_sc as plsc

F32 = jnp.float32
BF16 = jnp.bfloat16
MESH = pl.DeviceIdType.MESH

N_DEV = 8
D = 1024
FF = 2816
D_SSD = 1024
N_HEADS = 16
HEAD_DIM = 64
N_GROUPS = 4
N_STATE = 128
CHUNK = 128
GROUP_W = D_SSD // N_GROUPS
D_XBC = D_SSD + 2 * N_GROUPS * N_STATE
D_POOL = 1024
POOL_WINDOWS = (2, 4, 8, 16)
POOL_GW = 256
D_IN = 4112
D_IN_PAD = 4224
COL_Z, COL_XBC, COL_U, COL_DT = 0, 1024, 3072, 4096
EPS = 1e-6
FFN_RES = 0.5
LANE = 128
HALO = 16

ADAM_LR, ADAM_B1, ADAM_B2, ADAM_EPS, ADAM_WD, ADAM_STEP = 0.001, 0.9, 0.999, 1e-08, 0.01, 10

VMEM_LIMIT = 56 << 20


def _cp(sem):
    return pltpu.CompilerParams(dimension_semantics=sem, vmem_limit_bytes=VMEM_LIMIT)


def _dot(a, b, ca, cb, prec=None):
    return lax.dot_general(a, b, (((ca,), (cb,)), ((), ())), precision=prec,
                           preferred_element_type=F32)


def _exact_dot(a, b):
    return _dot(a, b, 1, 0, lax.Precision.HIGHEST)


def _sigmoid(v):
    return 1.0 / (1.0 + jnp.exp(-v))


def _silu_grad(v, sg):
    return sg * (1.0 + v * (1.0 - sg))


def _mm_nt(a, bt, tm, tn, out_dtype, name):
    m, k = a.shape
    n = bt.shape[0]

    def body(a_ref, b_ref, o_ref):
        o_ref[...] = _dot(a_ref[...], b_ref[...], 1, 1).astype(out_dtype)

    return pl.pallas_call(
        body, name=name, grid=(n // tn, m // tm),
        in_specs=[pl.BlockSpec((tm, k), lambda j, i: (i, 0)),
                  pl.BlockSpec((tn, k), lambda j, i: (j, 0))],
        out_specs=pl.BlockSpec((tm, tn), lambda j, i: (i, j)),
        out_shape=jax.ShapeDtypeStruct((m, n), out_dtype),
        compiler_params=_cp(("parallel", "parallel")))(a, bt)


def _mm_tn(a, b, tm, tk, name):
    kk, m = a.shape
    n = b.shape[1]
    nk = kk // tk

    def body(a_ref, b_ref, o_ref, acc):
        k = pl.program_id(1)

        @pl.when(k == 0)
        def _():
            acc[...] = jnp.zeros_like(acc)

        acc[...] += _dot(a_ref[...], b_ref[...], 0, 0)

        @pl.when(k == nk - 1)
        def _():
            o_ref[...] = acc[...]

    return pl.pallas_call(
        body, name=name, grid=(m // tm, nk),
        in_specs=[pl.BlockSpec((tk, tm), lambda i, k: (k, i)),
                  pl.BlockSpec((tk, n), lambda i, k: (k, 0))],
        out_specs=pl.BlockSpec((tm, n), lambda i, k: (i, 0)),
        out_shape=jax.ShapeDtypeStruct((m, n), F32),
        scratch_shapes=[pltpu.VMEM((tm, n), F32)],
        compiler_params=_cp(("parallel", "arbitrary")))(a, b)


def _modulated(xv, wn, mod_ref, k):
    r = lax.rsqrt(jnp.mean(xv * xv, axis=-1, keepdims=True) + EPS)
    hn = xv * r * wn
    return (hn * (1.0 + mod_ref[3 * k + 1:3 * k + 2, :]) + mod_ref[3 * k:3 * k + 1, :]).astype(BF16)


def _prenorm(x, wn, mod, k, tm, name):
    seq = x.shape[0]

    def body(x_ref, wn_ref, mod_ref, h_ref):
        h_ref[...] = _modulated(x_ref[...], wn_ref[...], mod_ref, k)

    return pl.pallas_call(
        body, name=name, grid=(seq // tm,),
        in_specs=[pl.BlockSpec((tm, D), lambda i: (i, 0)),
                  pl.BlockSpec((1, D), lambda i: (0, 0)),
                  pl.BlockSpec((9, D), lambda i: (0, 0))],
        out_specs=pl.BlockSpec((tm, D), lambda i: (i, 0)),
        out_shape=jax.ShapeDtypeStruct((seq, D), BF16),
        compiler_params=_cp(("parallel",)))(x, wn, mod)


def _norm_bwd(dh, xv, dxo, branch, wn, sc, res, stats_ref, first):
    r = lax.rsqrt(jnp.mean(xv * xv, axis=-1, keepdims=True) + EPS)
    xn = xv * r
    dhn = dh * (1.0 + sc)
    dxn = dhn * wn
    dx = dxo + r * (dxn - xn * jnp.mean(dxn * xn, axis=-1, keepdims=True))
    rows = jnp.concatenate([
        jnp.sum(dh, axis=0, keepdims=True),
        jnp.sum(dh * (xn * wn), axis=0, keepdims=True),
        jnp.sum(branch * dxo, axis=0, keepdims=True) * res,
        jnp.sum(dhn * xn, axis=0, keepdims=True),
        jnp.zeros((4, D), F32)], axis=0)

    @pl.when(first)
    def _():
        stats_ref[...] = rows

    @pl.when(jnp.logical_not(first))
    def _():
        stats_ref[...] += rows

    return dx


def _loss_head(x3, wf, tgt, mod, tm, name):
    seq = x3.shape[0]

    def body(x_ref, w_ref, t_ref, mod_ref, dx_ref, df_ref, st_ref):
        xv = x_ref[...]
        wv = w_ref[...]
        r = lax.rsqrt(jnp.mean(xv * xv, axis=-1, keepdims=True) + EPS)
        xn = xv * r
        e = xn * wv - t_ref[...]
        dy = e * (1.0 / D)
        dxn = dy * wv
        dx = r * (dxn - xn * jnp.mean(dxn * xn, axis=-1, keepdims=True))
        dx_ref[...] = dx
        df_ref[...] = (dx * (FFN_RES * mod_ref[8:9, :])).astype(BF16)
        rows = jnp.concatenate([
            jnp.sum(dy * xn, axis=0, keepdims=True),
            jnp.sum(e * e, axis=0, keepdims=True) * (0.5 / D),
            jnp.zeros((6, D), F32)], axis=0)

        @pl.when(pl.program_id(0) == 0)
        def _():
            st_ref[...] = rows

        @pl.when(pl.program_id(0) != 0)
        def _():
            st_ref[...] += rows

    return pl.pallas_call(
        body, name=name, grid=(seq // tm,),
        in_specs=[pl.BlockSpec((tm, D), lambda i: (i, 0)),
                  pl.BlockSpec((1, D), lambda i: (0, 0)),
                  pl.BlockSpec((tm, D), lambda i: (i, 0)),
                  pl.BlockSpec((9, D), lambda i: (0, 0))],
        out_specs=[pl.BlockSpec((tm, D), lambda i: (i, 0)),
                   pl.BlockSpec((tm, D), lambda i: (i, 0)),
                   pl.BlockSpec((8, D), lambda i: (0, 0))],
        out_shape=[jax.ShapeDtypeStruct((seq, D), F32), jax.ShapeDtypeStruct((seq, D), BF16),
                   jax.ShapeDtypeStruct((8, D), F32)],
        compiler_params=_cp(("arbitrary",)))(x3, wf, tgt, mod)


def _ffn_up(h, w, blk, tm, tn, name):
    seq = h.shape[0]
    nj = FF // tn

    def body(h_ref, wg_ref, wu_ref, g_ref, u_ref, a_ref):
        hv = h_ref[...]
        g = _dot(hv, wg_ref[...], 1, 1)
        u = _dot(hv, wu_ref[...], 1, 1)
        g_ref[...] = g.astype(BF16)
        u_ref[...] = u.astype(BF16)
        a_ref[...] = (g * _sigmoid(g) * u).astype(BF16)

    act = pl.BlockSpec((tm, tn), lambda j, i: (i, j))
    return pl.pallas_call(
        body, name=name, grid=(nj, seq // tm),
        in_specs=[pl.BlockSpec((tm, D), lambda j, i: (i, 0)),
                  pl.BlockSpec((tn, D), lambda j, i: (blk * nj + j, 0)),
                  pl.BlockSpec((tn, D), lambda j, i: ((blk + 1) * nj + j, 0))],
        out_specs=[act, act, act],
        out_shape=[jax.ShapeDtypeStruct((seq, FF), BF16)] * 3,
        compiler_params=_cp(("parallel", "parallel")))(h, w, w)


def _ffn_down(a, w, blk, x, mod, grow, wn_next, k_next, tm, tk, name):
    seq = a.shape[0]
    nk = FF // tk
    chain = k_next is not None

    def body(a_ref, w_ref, x_ref, mod_ref, wn_ref, xo_ref, f_ref, *rest):
        acc = rest[-1]
        k = pl.program_id(1)

        @pl.when(k == 0)
        def _():
            acc[...] = jnp.zeros_like(acc)

        acc[...] += _dot(a_ref[...], w_ref[...], 1, 0)

        @pl.when(k == nk - 1)
        def _():
            f = acc[...]
            f_ref[...] = f.astype(BF16)
            xo = x_ref[...] + (FFN_RES * mod_ref[grow:grow + 1, :]) * f
            xo_ref[...] = xo
            if chain:
                rest[0][...] = _modulated(xo, wn_ref[...], mod_ref, k_next)

    tok = pl.BlockSpec((tm, D), lambda i, k: (i, 0))
    return pl.pallas_call(
        body, name=name, grid=(seq // tm, nk),
        in_specs=[pl.BlockSpec((tm, tk), lambda i, k: (i, k)),
                  pl.BlockSpec((tk, D), lambda i, k: (blk * nk + k, 0)),
                  tok,
                  pl.BlockSpec((9, D), lambda i, k: (0, 0)),
                  pl.BlockSpec((1, D), lambda i, k: (0, 0))],
        out_specs=[tok, tok] + ([tok] if chain else []),
        out_shape=[jax.ShapeDtypeStruct((seq, D), F32), jax.ShapeDtypeStruct((seq, D), BF16)]
        + ([jax.ShapeDtypeStruct((seq, D), BF16)] if chain else []),
        scratch_shapes=[pltpu.VMEM((tm, D), F32)],
        compiler_params=_cp(("parallel", "arbitrary")))(a, w, x, mod, wn_next)


def _ffn_bwd_da(df, w, blk, g, u, tm, tn, name):
    seq = df.shape[0]
    nj = FF // tn

    def body(df_ref, w_ref, g_ref, u_ref, dg_ref, du_ref):
        da = _dot(df_ref[...], w_ref[...], 1, 1)
        gv = g_ref[...].astype(F32)
        uv = u_ref[...].astype(F32)
        sg = _sigmoid(gv)
        dg_ref[...] = (da * uv * _silu_grad(gv, sg)).astype(BF16)
        du_ref[...] = (da * (gv * sg)).astype(BF16)

    act = pl.BlockSpec((tm, tn), lambda j, i: (i, j))
    return pl.pallas_call(
        body, name=name, grid=(nj, seq // tm),
        in_specs=[pl.BlockSpec((tm, D), lambda j, i: (i, 0)),
                  pl.BlockSpec((tn, D), lambda j, i: (blk * nj + j, 0)),
                  act, act],
        out_specs=[act, act],
        out_shape=[jax.ShapeDtypeStruct((seq, FF), BF16)] * 2,
        compiler_params=_cp(("parallel", "parallel")))(df, w, g, u)


def _ffn_bwd_dh(dg, du, w, blk, x, dxo, fb, wn, mod, k, nxt, tm, tk, name):
    seq = x.shape[0]
    nk = FF // tk

    def body(dg_ref, du_ref, wg_ref, wu_ref, x_ref, dxo_ref, f_ref, wn_ref, mod_ref, dx_ref, st_ref, *rest):
        acc = rest[-1]
        kk = pl.program_id(1)
        first = pl.program_id(0) == 0

        @pl.when(kk == 0)
        def _():
            acc[...] = jnp.zeros_like(acc)

        acc[...] += _dot(dg_ref[...], wg_ref[...], 1, 0) + _dot(du_ref[...], wu_ref[...], 1, 0)

        @pl.when(kk == nk - 1)
        def _():
            dx = _norm_bwd(acc[...], x_ref[...], dxo_ref[...], f_ref[...].astype(F32), wn_ref[...],
                           mod_ref[3 * k + 1:3 * k + 2, :], FFN_RES, st_ref, first)
            dx_ref[...] = dx
            if nxt is not None:
                rest[0][...] = (dx * (nxt[1] * mod_ref[nxt[0]:nxt[0] + 1, :])).astype(BF16)

    tok = pl.BlockSpec((tm, D), lambda i, kk: (i, 0))
    return pl.pallas_call(
        body, name=name, grid=(seq // tm, nk),
        in_specs=[pl.BlockSpec((tm, tk), lambda i, kk: (i, kk)),
                  pl.BlockSpec((tm, tk), lambda i, kk: (i, kk)),
                  pl.BlockSpec((tk, D), lambda i, kk: (blk * nk + kk, 0)),
                  pl.BlockSpec((tk, D), lambda i, kk: ((blk + 1) * nk + kk, 0)),
                  tok, tok, tok,
                  pl.BlockSpec((1, D), lambda i, kk: (0, 0)),
                  pl.BlockSpec((9, D), lambda i, kk: (0, 0))],
        out_specs=[tok, pl.BlockSpec((8, D), lambda i, kk: (0, 0))] + ([tok] if nxt is not None else []),
        out_shape=[jax.ShapeDtypeStruct((seq, D), F32), jax.ShapeDtypeStruct((8, D), F32)]
        + ([jax.ShapeDtypeStruct((seq, D), BF16)] if nxt is not None else []),
        scratch_shapes=[pltpu.VMEM((tm, D), F32)],
        compiler_params=_cp(("arbitrary", "arbitrary")))(dg, du, w, w, x, dxo, fb, wn, mod)


def _ffn_forward(x, h, w, mod, k, wn_next, k_next, tm, tag):
    w_gu, w_d, blk_d = w
    g, u, a = _ffn_up(h, w_gu, 0, tm, FF // 2, f"{tag}_up")
    outs = _ffn_down(a, w_d, blk_d, x, mod, 3 * k + 2, wn_next, k_next, tm, FF // 2, f"{tag}_down")
    return outs[0], (outs[2] if k_next is not None else None), (x, h, g, u, a, outs[1])


def _ffn_backward(dxo, df, saved, w, wn, mod, k, nxt, tm, tag):
    w_gu, w_d, blk_d = w
    x, h, g, u, a, fb = saved
    dg, du = _ffn_bwd_da(df, w_d, blk_d, g, u, tm, FF // 2, f"{tag}_bwd_da")
    seq = x.shape[0]
    tk = min(seq, 512)
    d_gate_t = _mm_tn(dg, h, FF // 2, tk, f"{tag}_dw_gate")
    d_up_t = _mm_tn(du, h, FF // 2, tk, f"{tag}_dw_up")
    d_down = _mm_tn(a, df, FF // 2, tk, f"{tag}_dw_down")
    dws, dg, du = lax.optimization_barrier(((d_gate_t, d_up_t, d_down), dg, du))
    outs = _ffn_bwd_dh(dg, du, w_gu, 0, x, dxo, fb, wn, mod, k, nxt, tm, FF // 2, f"{tag}_bwd_dh")
    return outs[0], (outs[2] if nxt is not None else None), outs[1], dws


def _prev_rows(tm, col):
    return pl.BlockSpec((HALO, 1024), lambda i, j: (jnp.maximum(i * (tm // HALO) - 1, 0), col + j))


def _conv_pre(ext, cw, cb, rows):
    pre = cb + cw[3:4, :] * ext
    for s in (1, 2, 3):
        pre = pre + cw[3 - s:4 - s, :] * pltpu.roll(ext, s, 0)
    return pre[HALO:HALO + rows]


def _conv_fwd(proj, cw, cb, tm, name):
    seq = proj.shape[0]

    def body(x_ref, p_ref, cw_ref, cb_ref, o_ref):
        prev = jnp.where(pl.program_id(0) == 0, 0.0, p_ref[...])
        ext = jnp.concatenate([prev, x_ref[...]], axis=0)
        pre = _conv_pre(ext, cw_ref[...], cb_ref[...], tm)
        o_ref[...] = pre * _sigmoid(pre)

    c0 = COL_XBC // 1024
    return pl.pallas_call(
        body, name=name, grid=(seq // tm, 2),
        in_specs=[pl.BlockSpec((tm, 1024), lambda i, j: (i, c0 + j)),
                  _prev_rows(tm, c0),
                  pl.BlockSpec((4, 1024), lambda i, j: (0, j)),
                  pl.BlockSpec((1, 1024), lambda i, j: (0, j))],
        out_specs=pl.BlockSpec((tm, 1024), lambda i, j: (i, j)),
        out_shape=jax.ShapeDtypeStruct((seq, D_XBC), F32),
        compiler_params=_cp(("parallel", "parallel")))(proj, proj, cw, cb)


def _conv_bwd(dact, proj, cw, cb, tm, name):
    seq = proj.shape[0]
    ni = seq // tm

    def body(d_ref, dn_ref, x_ref, p_ref, n_ref, cw_ref, cb_ref, o_ref, st_ref):
        i = pl.program_id(1)
        cwv = cw_ref[...]
        prev = jnp.where(i == 0, 0.0, p_ref[...])
        ext = jnp.concatenate([prev, x_ref[...], n_ref[...]], axis=0)
        pre = _conv_pre(ext, cwv, cb_ref[...], tm + HALO)
        dnext = jnp.where(i == ni - 1, 0.0, dn_ref[...])
        dext = jnp.concatenate([d_ref[...], dnext], axis=0)
        dpre = dext * _silu_grad(pre, _sigmoid(pre))
        n = tm + HALO
        dx = cwv[3:4, :] * dpre
        for s in (1, 2, 3):
            dx = dx + cwv[3 - s:4 - s, :] * pltpu.roll(dpre, n - s, 0)
        o_ref[...] = dx[:tm].astype(BF16)
        dcur = dpre[:tm]
        rows = [jnp.sum(dcur * pltpu.roll(ext, 3 - k, 0)[HALO:HALO + tm], axis=0, keepdims=True) for k in range(3)]
        rows.append(jnp.sum(dcur * ext[HALO:HALO + tm], axis=0, keepdims=True))
        rows.append(jnp.sum(dcur, axis=0, keepdims=True))
        rows.append(jnp.zeros((3, 1024), F32))
        rows = jnp.concatenate(rows, axis=0)

        @pl.when(i == 0)
        def _():
            st_ref[...] = rows

        @pl.when(i != 0)
        def _():
            st_ref[...] += rows

    c0 = COL_XBC // 1024
    return pl.pallas_call(
        body, name=name, grid=(2, ni),
        in_specs=[pl.BlockSpec((tm, 1024), lambda j, i: (i, j)),
                  pl.BlockSpec((HALO, 1024), lambda j, i: (jnp.minimum((i + 1) * (tm // HALO), seq // HALO - 1), j)),
                  pl.BlockSpec((tm, 1024), lambda j, i: (i, c0 + j)),
                  pl.BlockSpec((HALO, 1024), lambda j, i: (jnp.maximum(i * (tm // HALO) - 1, 0), c0 + j)),
                  pl.BlockSpec((HALO, 1024), lambda j, i: (jnp.minimum((i + 1) * (tm // HALO), seq // HALO - 1), c0 + j)),
                  pl.BlockSpec((4, 1024), lambda j, i: (0, j)),
                  pl.BlockSpec((1, 1024), lambda j, i: (0, j))],
        out_specs=[pl.BlockSpec((tm, 1024), lambda j, i: (i, j)),
                   pl.BlockSpec((8, 1024), lambda j, i: (0, j))],
        out_shape=[jax.ShapeDtypeStruct((seq, D_XBC), BF16), jax.ShapeDtypeStruct((8, D_XBC), F32)],
        compiler_params=_cp(("parallel", "arbitrary")))(dact, dact, proj, proj, proj, cw, cb)


def _head_expand():
    r = lax.broadcasted_iota(jnp.int32, (LANE, D_SSD), 0)
    c = lax.broadcasted_iota(jnp.int32, (LANE, D_SSD), 1)
    return (c // HEAD_DIM == r).astype(F32)


def _head_reduce():
    r = lax.broadcasted_iota(jnp.int32, (D_SSD, LANE), 0)
    c = lax.broadcasted_iota(jnp.int32, (D_SSD, LANE), 1)
    return (r // HEAD_DIM == c).astype(F32)


def _ssd_common(dtr, par):
    q = CHUNK
    v = dtr + par[0:1, :]
    dt = jnp.maximum(v, 0.0) + jnp.log(1.0 + jnp.exp(-jnp.abs(v)))
    a = -jnp.exp(par[1:2, :])
    adt = dt * a
    li = lax.broadcasted_iota(jnp.int32, (q, q), 0)
    si = lax.broadcasted_iota(jnp.int32, (q, q), 1)
    causal = li >= si
    acs = _exact_dot(causal.astype(F32), adt)
    expand = _head_expand()
    dt_l = _exact_dot(dt, expand)
    acs_l = _exact_dot(acs, expand)
    par_l = _exact_dot(par, expand)
    last_l = acs_l[q - 1:q, :]
    return dict(v=v, dt=dt, a=a, acs=acs, acs_t=acs.T, causal=causal, dt_l=dt_l, acs_l=acs_l,
                ea_l=jnp.exp(acs_l), ds_l=jnp.exp(last_l - acs_l), cd_l=jnp.exp(last_l), dskip_l=par_l[2:3, :])


def _decay(cm, h):
    seg = cm["acs"][:, h:h + 1] - cm["acs_t"][h:h + 1, :]
    return jnp.exp(jnp.where(cm["causal"], seg, -jnp.inf))


def _lane_mask(r):
    lane = lax.broadcasted_iota(jnp.int32, (1, GROUP_W), 1)
    return lane // HEAD_DIM == r


def _ssd_fwd(xbc, proj, par, name):
    seq = xbc.shape[0]
    nc = seq // CHUNK
    q = CHUNK

    def body(x_ref, dt_ref, par_ref, y_ref, hp_ref, state):
        @pl.when(pl.program_id(0) == 0)
        def _():
            state[...] = jnp.zeros_like(state)

        cm = _ssd_common(dt_ref[...], par_ref[...])
        for g in range(N_GROUPS):
            lo = g * GROUP_W
            xs = x_ref[:, lo:lo + GROUP_W]
            bm = x_ref[:, D_SSD + g * N_STATE:D_SSD + (g + 1) * N_STATE].astype(BF16)
            cmat = x_ref[:, D_SSD + N_GROUPS * N_STATE + g * N_STATE:D_SSD + N_GROUPS * N_STATE + (g + 1) * N_STATE].astype(BF16)
            xdt = xs * cm["dt_l"][:, lo:lo + GROUP_W]
            xdt_b = xdt.astype(BF16)
            cb = _dot(cmat, bm, 1, 1)
            yd = jnp.zeros((q, GROUP_W), F32)
            for r in range(4):
                s_h = (cb * _decay(cm, 4 * g + r)).astype(BF16)
                yd = jnp.where(_lane_mask(r), _dot(s_h, xdt_b, 1, 0), yd)
            hg = state[g]
            hp_ref[0, g] = hg
            yo = _dot(cmat, hg.astype(BF16), 1, 0) * cm["ea_l"][:, lo:lo + GROUP_W]
            y_ref[:, lo:lo + GROUP_W] = yd + yo + cm["dskip_l"][:, lo:lo + GROUP_W] * xs
            xds = (xdt * cm["ds_l"][:, lo:lo + GROUP_W]).astype(BF16)
            state[g] = hg * cm["cd_l"][:, lo:lo + GROUP_W] + _dot(bm, xds, 0, 0)

    return pl.pallas_call(
        body, name=name, grid=(nc,),
        in_specs=[pl.BlockSpec((q, D_XBC), lambda c: (c, 0)),
                  pl.BlockSpec((q, LANE), lambda c: (c, COL_DT // LANE)),
                  pl.BlockSpec((8, LANE), lambda c: (0, 0))],
        out_specs=[pl.BlockSpec((q, D_SSD), lambda c: (c, 0)),
                   pl.BlockSpec((1, N_GROUPS, N_STATE, GROUP_W), lambda c: (c, 0, 0, 0))],
        out_shape=[jax.ShapeDtypeStruct((seq, D_SSD), F32),
                   jax.ShapeDtypeStruct((nc, N_GROUPS, N_STATE, GROUP_W), F32)],
        scratch_shapes=[pltpu.VMEM((N_GROUPS, N_STATE, GROUP_W), F32)],
        compiler_params=_cp(("arbitrary",)))(xbc, proj, par)


def _ssd_bwd(dy, xbc, proj, par, hprev, name):
    seq = xbc.shape[0]
    nc = seq // CHUNK
    q = CHUNK

    def body(dy_ref, x_ref, dt_ref, par_ref, hp_ref, dx_ref, ddt_ref, st_ref, dstate):
        step = pl.program_id(0)

        @pl.when(step == 0)
        def _():
            dstate[...] = jnp.zeros_like(dstate)

        par = par_ref[...]
        cm = _ssd_common(dt_ref[...], par)
        reduce = _head_reduce()
        lane128 = lax.broadcasted_iota(jnp.int32, (1, LANE), 1)
        row128 = lax.broadcasted_iota(jnp.int32, (LANE, 1), 0)
        d_acs = jnp.zeros((q, LANE), F32)
        d_acs_t = jnp.zeros((LANE, q), F32)
        last_terms = []
        acs_terms = []
        dxdt_all = []
        for g in range(N_GROUPS):
            lo = g * GROUP_W
            sl = slice(lo, lo + GROUP_W)
            xs = x_ref[:, sl]
            bm32 = x_ref[:, D_SSD + g * N_STATE:D_SSD + (g + 1) * N_STATE]
            cm32 = x_ref[:, D_SSD + N_GROUPS * N_STATE + g * N_STATE:D_SSD + N_GROUPS * N_STATE + (g + 1) * N_STATE]
            bm = bm32.astype(BF16)
            cmat = cm32.astype(BF16)
            dyg = dy_ref[:, sl]
            dyg_b = dyg.astype(BF16)
            xdt = xs * cm["dt_l"][:, sl]
            xdt_b = xdt.astype(BF16)
            hg = hp_ref[0, g]
            hg_b = hg.astype(BF16)
            dhg = dstate[g]
            dhg_b = dhg.astype(BF16)
            ea = cm["ea_l"][:, sl]
            ds = cm["ds_l"][:, sl]
            cd = cm["cd_l"][:, sl]
            yoff = _dot(cmat, hg_b, 1, 0) * ea
            dw = (dyg * ea).astype(BF16)
            d_c = _dot(dw, hg_b, 1, 1)
            d_hprev = _dot(cmat, dw, 0, 0) + dhg * cd
            t_acs = dyg * yoff
            d_last_g = jnp.sum(dhg * hg, axis=0, keepdims=True) * cd
            xds_b = (xdt * ds).astype(BF16)
            dxds = _dot(bm, dhg_b, 1, 0)
            d_b = _dot(xds_b, dhg_b, 1, 1)
            dxdt = dxds * ds
            t_ds = dxds * xdt * ds
            t_acs = t_acs - t_ds
            d_last_g = d_last_g + jnp.sum(t_ds, axis=0, keepdims=True)
            cb = _dot(cmat, bm, 1, 1)
            d_cb = jnp.zeros((q, q), F32)
            for r in range(4):
                h = 4 * g + r
                dec = _decay(cm, h)
                s_h = cb * dec
                mask = _lane_mask(r)
                d_s = _dot(jnp.where(mask, dyg, 0.0).astype(BF16), xdt_b, 1, 1)
                dxdt = dxdt + jnp.where(mask, _dot(s_h.astype(BF16), dyg_b, 0, 0), 0.0)
                d_cb = d_cb + d_s * dec
                d_m = d_s * s_h
                d_acs = d_acs + jnp.where(lane128 == h, jnp.sum(d_m, axis=1, keepdims=True), 0.0)
                d_acs_t = d_acs_t + jnp.where(row128 == h, jnp.sum(d_m, axis=0, keepdims=True), 0.0)
            d_cb_b = d_cb.astype(BF16)
            d_c = d_c + _dot(d_cb_b, bm, 1, 0)
            d_b = d_b + _dot(d_cb_b, cmat, 0, 0)
            dstate[g] = d_hprev
            dx_ref[:, sl] = dxdt * cm["dt_l"][:, sl] + cm["dskip_l"][:, sl] * dyg
            dx_ref[:, D_SSD + g * N_STATE:D_SSD + (g + 1) * N_STATE] = d_b
            dx_ref[:, D_SSD + N_GROUPS * N_STATE + g * N_STATE:D_SSD + N_GROUPS * N_STATE + (g + 1) * N_STATE] = d_c
            acs_terms.append(t_acs)
            dxdt_all.append(dxdt * xs)
            last_terms.append(d_last_g)
        t_acs_l = jnp.concatenate(acs_terms, axis=1)
        d_dt_l = jnp.concatenate(dxdt_all, axis=1)
        d_last_l = jnp.concatenate(last_terms, axis=1)
        d_acs = d_acs + _exact_dot(t_acs_l, reduce) - d_acs_t.T
        last_row = lax.broadcasted_iota(jnp.int32, (q, 1), 0) == q - 1
        d_acs = d_acs + jnp.where(last_row, _exact_dot(jnp.broadcast_to(d_last_l, (8, D_SSD)), reduce)[0:1, :], 0.0)
        li = lax.broadcasted_iota(jnp.int32, (q, q), 0)
        si = lax.broadcasted_iota(jnp.int32, (q, q), 1)
        d_adt = _exact_dot((si >= li).astype(F32), d_acs)
        d_dt = _exact_dot(d_dt_l, reduce) + d_adt * cm["a"]
        d_dtr = d_dt * _sigmoid(cm["v"])
        ddt_ref[...] = d_dtr.astype(BF16)
        d_skip = _exact_dot(jnp.broadcast_to(jnp.sum(dy_ref[...] * x_ref[:, 0:D_SSD], axis=0, keepdims=True), (8, D_SSD)), reduce)[0:1, :]
        rows = jnp.concatenate([
            jnp.sum(d_dtr, axis=0, keepdims=True),
            jnp.sum(d_adt * cm["dt"], axis=0, keepdims=True) * cm["a"],
            d_skip,
            jnp.zeros((5, LANE), F32)], axis=0)

        @pl.when(step == 0)
        def _():
            st_ref[...] = rows

        @pl.when(step != 0)
        def _():
            st_ref[...] += rows

    rev = lambda c: nc - 1 - c
    return pl.pallas_call(
        body, name=name, grid=(nc,),
        in_specs=[pl.BlockSpec((q, D_SSD), lambda c: (rev(c), 0)),
                  pl.BlockSpec((q, D_XBC), lambda c: (rev(c), 0)),
                  pl.BlockSpec((q, LANE), lambda c: (rev(c), COL_DT // LANE)),
                  pl.BlockSpec((8, LANE), lambda c: (0, 0)),
                  pl.BlockSpec((1, N_GROUPS, N_STATE, GROUP_W), lambda c: (rev(c), 0, 0, 0))],
        out_specs=[pl.BlockSpec((q, D_XBC), lambda c: (rev(c), 0)),
                   pl.BlockSpec((q, LANE), lambda c: (rev(c), 0)),
                   pl.BlockSpec((8, LANE), lambda c: (0, 0))],
        out_shape=[jax.ShapeDtypeStruct((seq, D_XBC), F32),
                   jax.ShapeDtypeStruct((seq, LANE), BF16),
                   jax.ShapeDtypeStruct((8, LANE), F32)],
        scratch_shapes=[pltpu.VMEM((N_GROUPS, N_STATE, GROUP_W), F32)],
        compiler_params=_cp(("arbitrary",)))(dy, xbc, proj, par, hprev)


def _gate_norm_fwd(y, proj, wn, tm, name):
    seq = y.shape[0]

    def body(y_ref, z_ref, w_ref, o_ref):
        for g in range(N_GROUPS):
            sl = slice(g * GROUP_W, (g + 1) * GROUP_W)
            zv = z_ref[:, sl]
            yz = y_ref[:, sl] * (zv * _sigmoid(zv))
            r = lax.rsqrt(jnp.mean(yz * yz, axis=-1, keepdims=True) + EPS)
            o_ref[:, sl] = (yz * r * w_ref[:, sl]).astype(BF16)

    tok = pl.BlockSpec((tm, D_SSD), lambda i: (i, 0))
    return pl.pallas_call(
        body, name=name, grid=(seq // tm,),
        in_specs=[tok, tok, pl.BlockSpec((1, D_SSD), lambda i: (0, 0))],
        out_specs=tok,
        out_shape=jax.ShapeDtypeStruct((seq, D_SSD), BF16),
        compiler_params=_cp(("parallel",)))(y, proj, wn)


def _gate_norm_bwd(dys, y, proj, wn, tm, name):
    seq = y.shape[0]

    def body(d_ref, y_ref, z_ref, w_ref, dy_ref, dz_ref, st_ref):
        rows = []
        for g in range(N_GROUPS):
            sl = slice(g * GROUP_W, (g + 1) * GROUP_W)
            zv = z_ref[:, sl]
            yv = y_ref[:, sl]
            sg = _sigmoid(zv)
            sz = zv * sg
            yz = yv * sz
            r = lax.rsqrt(jnp.mean(yz * yz, axis=-1, keepdims=True) + EPS)
            yn = yz * r
            dv = d_ref[:, sl]
            dyn = dv * w_ref[:, sl]
            dyz = r * (dyn - yn * jnp.mean(dyn * yn, axis=-1, keepdims=True))
            dy_ref[:, sl] = dyz * sz
            dz_ref[:, sl] = (dyz * yv * _silu_grad(zv, sg)).astype(BF16)
            rows.append(jnp.sum(dv * yn, axis=0, keepdims=True))
        rows = jnp.concatenate([jnp.concatenate(rows, axis=1), jnp.zeros((7, D_SSD), F32)], axis=0)

        @pl.when(pl.program_id(0) == 0)
        def _():
            st_ref[...] = rows

        @pl.when(pl.program_id(0) != 0)
        def _():
            st_ref[...] += rows

    tok = pl.BlockSpec((tm, D_SSD), lambda i: (i, 0))
    return pl.pallas_call(
        body, name=name, grid=(seq // tm,),
        in_specs=[tok, tok, tok, pl.BlockSpec((1, D_SSD), lambda i: (0, 0))],
        out_specs=[tok, tok, pl.BlockSpec((8, D_SSD), lambda i: (0, 0))],
        out_shape=[jax.ShapeDtypeStruct((seq, D_SSD), F32), jax.ShapeDtypeStruct((seq, D_SSD), BF16),
                   jax.ShapeDtypeStruct((8, D_SSD), F32)],
        compiler_params=_cp(("arbitrary",)))(dys, y, proj, wn)


def _pool_counts(t0, rows, w):
    pos = (t0 + 1 + lax.broadcasted_iota(jnp.int32, (rows, 1), 0)).astype(F32)
    return jnp.minimum(pos, float(w))


def _window_means(ext, t0):
    n = ext.shape[0]
    outs = []
    run = ext
    width = 1
    sums = {}
    while width < 16:
        run = run + pltpu.roll(run, width, 0)
        width *= 2
        sums[width] = run
    for g, w in enumerate(POOL_WINDOWS):
        sl = slice(g * POOL_GW, (g + 1) * POOL_GW)
        cnt = _pool_counts(t0, n - HALO, w)
        outs.append(sums[w][HALO:, sl] / cnt - ext[HALO:, sl])
    return outs


def _pool_fwd(proj, pw, pb, ps, tm, name):
    seq = proj.shape[0]

    def body(u_ref, p_ref, pw_ref, pb_ref, ps_ref, o_ref):
        i = pl.program_id(0)
        prev = jnp.where(i == 0, 0.0, p_ref[...])
        ext = jnp.concatenate([prev, u_ref[...]], axis=0)
        diffs = _window_means(ext, i * tm)
        for g in range(4):
            sl = slice(g * POOL_GW, (g + 1) * POOL_GW)
            out = _dot(diffs[g].astype(BF16), pw_ref[g], 1, 0) + pb_ref[:, sl]
            o_ref[:, sl] = (out * ps_ref[:, sl]).astype(BF16)

    c0 = COL_U // 1024
    vec = pl.BlockSpec((1, D_POOL), lambda i: (0, 0))
    return pl.pallas_call(
        body, name=name, grid=(seq // tm,),
        in_specs=[pl.BlockSpec((tm, 1024), lambda i: (i, c0)),
                  pl.BlockSpec((HALO, 1024), lambda i: (jnp.maximum(i * (tm // HALO) - 1, 0), c0)),
                  pl.BlockSpec((4, POOL_GW, POOL_GW), lambda i: (0, 0, 0)), vec, vec],
        out_specs=pl.BlockSpec((tm, D_POOL), lambda i: (i, 0)),
        out_shape=jax.ShapeDtypeStruct((seq, D_POOL), BF16),
        compiler_params=_cp(("parallel",)))(proj, proj, pw, pb, ps)


def _pool_bwd(dyp, proj, pw, pb, ps, tm, name):
    seq = proj.shape[0]
    ni = seq // tm

    def body(d_ref, dn_ref, u_ref, p_ref, pw_ref, pb_ref, ps_ref, du_ref, dw_ref, st_ref):
        i = pl.program_id(0)
        prev = jnp.where(i == 0, 0.0, p_ref[...])
        ext = jnp.concatenate([prev, u_ref[...]], axis=0)
        diffs = _window_means(ext, i * tm)
        dnext = jnp.where(i == ni - 1, 0.0, dn_ref[...])
        dext = jnp.concatenate([d_ref[...], dnext], axis=0)
        n = tm + HALO
        b_rows, s_rows = [], []
        for g, w in enumerate(POOL_WINDOWS):
            sl = slice(g * POOL_GW, (g + 1) * POOL_GW)
            wg = pw_ref[g]
            dout = dext[:, sl] * ps_ref[:, sl]
            dcur = dout[:tm]
            pre = _dot(diffs[g].astype(BF16), wg, 1, 0) + pb_ref[:, sl]
            s_rows.append(jnp.sum(d_ref[:, sl] * pre, axis=0, keepdims=True))
            b_rows.append(jnp.sum(dcur, axis=0, keepdims=True))
            dwg = _dot(diffs[g].astype(BF16), dcur.astype(BF16), 0, 0)

            @pl.when(i == 0)
            def _():
                dw_ref[g] = dwg

            @pl.when(i != 0)
            def _():
                dw_ref[g] += dwg

            ddiff = _dot(dout.astype(BF16), wg, 1, 1)
            scaled = ddiff / _pool_counts(i * tm, n, w)
            run = scaled
            width = 1
            while width < w:
                run = run + pltpu.roll(run, n - width, 0)
                width *= 2
            du_ref[:, sl] = (run[:tm] - ddiff[:tm]).astype(BF16)
        rows = jnp.concatenate([jnp.concatenate(b_rows, axis=1), jnp.concatenate(s_rows, axis=1),
                                jnp.zeros((6, D_POOL), F32)], axis=0)

        @pl.when(i == 0)
        def _():
            st_ref[...] = rows

        @pl.when(i != 0)
        def _():
            st_ref[...] += rows

    c0 = COL_U // 1024
    vec = pl.BlockSpec((1, D_POOL), lambda i: (0, 0))
    last = seq // HALO - 1
    return pl.pallas_call(
        body, name=name, grid=(ni,),
        in_specs=[pl.BlockSpec((tm, D_POOL), lambda i: (i, 0)),
                  pl.BlockSpec((HALO, D_POOL), lambda i: (jnp.minimum((i + 1) * (tm // HALO), last), 0)),
                  pl.BlockSpec((tm, 1024), lambda i: (i, c0)),
                  pl.BlockSpec((HALO, 1024), lambda i: (jnp.maximum(i * (tm // HALO) - 1, 0), c0)),
                  pl.BlockSpec((4, POOL_GW, POOL_GW), lambda i: (0, 0, 0)), vec, vec],
        out_specs=[pl.BlockSpec((tm, D_POOL), lambda i: (i, 0)),
                   pl.BlockSpec((4, POOL_GW, POOL_GW), lambda i: (0, 0, 0)),
                   pl.BlockSpec((8, D_POOL), lambda i: (0, 0))],
        out_shape=[jax.ShapeDtypeStruct((seq, D_POOL), BF16),
                   jax.ShapeDtypeStruct((4, POOL_GW, POOL_GW), F32),
                   jax.ShapeDtypeStruct((8, D_POOL), F32)],
        compiler_params=_cp(("arbitrary",)))(dyp, dyp, proj, proj, pw, pb, ps)


def _mix_out(ys, yp, wout, x1, mod, wn_next, tm, name):
    seq = ys.shape[0]

    def body(ys_ref, yp_ref, w_ref, x_ref, mod_ref, wn_ref, xo_ref, m_ref, h_ref):
        mix = _dot(ys_ref[...], w_ref[0:D_SSD, :], 1, 0) + _dot(yp_ref[...], w_ref[D_SSD:2 * D_SSD, :], 1, 0)
        m_ref[...] = mix.astype(BF16)
        xo = x_ref[...] + mod_ref[5:6, :] * mix
        xo_ref[...] = xo
        h_ref[...] = _modulated(xo, wn_ref[...], mod_ref, 2)

    tok = pl.BlockSpec((tm, D), lambda i: (i, 0))
    return pl.pallas_call(
        body, name=name, grid=(seq // tm,),
        in_specs=[tok, tok, pl.BlockSpec((2 * D_SSD, D), lambda i: (0, 0)), tok,
                  pl.BlockSpec((9, D), lambda i: (0, 0)), pl.BlockSpec((1, D), lambda i: (0, 0))],
        out_specs=[tok, tok, tok],
        out_shape=[jax.ShapeDtypeStruct((seq, D), F32), jax.ShapeDtypeStruct((seq, D), BF16),
                   jax.ShapeDtypeStruct((seq, D), BF16)],
        compiler_params=_cp(("parallel",)))(ys, yp, wout, x1, mod, wn_next)


def _mix_bwd_dh(dz, dxbc, du, ddt, win_t, x1, dx2, mixb, wn, mod, tm, name):
    seq = x1.shape[0]

    def body(dz_ref, dx_ref, du_ref, ddt_ref, w_ref, x_ref, dxo_ref, m_ref, wn_ref, mod_ref, o_ref, st_ref, df_ref):
        dh = (_dot(dz_ref[...], w_ref[COL_Z:COL_Z + 1024, :], 1, 0)
              + _dot(dx_ref[...], w_ref[COL_XBC:COL_XBC + D_XBC, :], 1, 0)
              + _dot(du_ref[...], w_ref[COL_U:COL_U + 1024, :], 1, 0)
              + _dot(ddt_ref[...], w_ref[COL_DT:COL_DT + LANE, :], 1, 0))
        dx = _norm_bwd(dh, x_ref[...], dxo_ref[...], m_ref[...].astype(F32), wn_ref[...],
                       mod_ref[4:5, :], 1.0, st_ref, pl.program_id(0) == 0)
        o_ref[...] = dx
        df_ref[...] = (dx * (FFN_RES * mod_ref[2:3, :])).astype(BF16)

    tok = pl.BlockSpec((tm, D), lambda i: (i, 0))
    return pl.pallas_call(
        body, name=name, grid=(seq // tm,),
        in_specs=[tok, pl.BlockSpec((tm, D_XBC), lambda i: (i, 0)), tok,
                  pl.BlockSpec((tm, LANE), lambda i: (i, 0)),
                  pl.BlockSpec((D_IN_PAD, D), lambda i: (0, 0)),
                  tok, tok, tok,
                  pl.BlockSpec((1, D), lambda i: (0, 0)),
                  pl.BlockSpec((9, D), lambda i: (0, 0))],
        out_specs=[tok, pl.BlockSpec((8, D), lambda i: (0, 0)), tok],
        out_shape=[jax.ShapeDtypeStruct((seq, D), F32), jax.ShapeDtypeStruct((8, D), F32),
                   jax.ShapeDtypeStruct((seq, D), BF16)],
        compiler_params=_cp(("arbitrary",)))(dz, dxbc, du, ddt, win_t, x1, dx2, mixb, wn, mod)


def _mix_bwd_dycat(dmix, wout, tm, name):
    seq = dmix.shape[0]

    def body(d_ref, w_ref, a_ref, b_ref):
        dv = d_ref[...]
        a_ref[...] = _dot(dv, w_ref[0:D_SSD, :], 1, 1)
        b_ref[...] = _dot(dv, w_ref[D_SSD:2 * D_SSD, :], 1, 1)

    tok = pl.BlockSpec((tm, D), lambda i: (i, 0))
    return pl.pallas_call(
        body, name=name, grid=(seq // tm,),
        in_specs=[tok, pl.BlockSpec((2 * D_SSD, D), lambda i: (0, 0))],
        out_specs=[tok, tok],
        out_shape=[jax.ShapeDtypeStruct((seq, D), F32)] * 2,
        compiler_params=_cp(("parallel",)))(dmix, wout)


def _local_step(x, tgt, mod, wff1, later_weights, pool_w, vecs, tm):
    seq = x.shape[0]
    tk = min(seq, 512)
    h1 = _prenorm(x, vecs["ffn1_norm"], mod, 0, tm, "ffn1_prenorm")
    x1, h2, s1 = _ffn_forward(x, h1, wff1, mod, 0, vecs["mix_norm"], 1, tm, "ffn1")
    x1, wff2, win_t, wout = later_weights(x1)
    proj = _mm_nt(h2, win_t, tm, D_IN_PAD // 3, F32, "mix_in_proj")
    xbc = _conv_fwd(proj, vecs["conv_w"], vecs["conv_b"], tm, "mix_conv")
    y, hprev = _ssd_fwd(xbc, proj, vecs["ssd_par"], "mix_ssd")
    ys = _gate_norm_fwd(y, proj, vecs["ssd_norm_w"], tm, "mix_gate_norm")
    yp = _pool_fwd(proj, pool_w, vecs["pool_b"], vecs["pool_scale"], tm, "mix_pool")
    x2, mixb, h3 = _mix_out(ys, yp, wout, x1, mod, vecs["ffn2_norm"], tm, "mix_out_proj")
    x3, _, s3 = _ffn_forward(x2, h3, wff2, mod, 2, vecs["ffn2_norm"], None, tm, "ffn2")
    dx3, df3, st_loss = _loss_head(x3, vecs["final_norm"], tgt, mod, tm, "loss_head")

    dx2, dmix, st3, dw3 = _ffn_backward(dx3, df3, s3, wff2, vecs["ffn2_norm"], mod, 2, (5, 1.0), tm, "ffn2")
    dys, dyp = _mix_bwd_dycat(dmix, wout, tm, "mix_bwd_dycat")
    d_wout = (_mm_tn(ys, dmix, D_SSD, tk, "mix_dw_out_ssd"), _mm_tn(yp, dmix, D_POOL, tk, "mix_dw_out_pool"))
    du, d_pool_w, st_pool = _pool_bwd(dyp, proj, pool_w, vecs["pool_b"], vecs["pool_scale"], tm, "mix_pool_bwd")
    dy, dz, st_gn = _gate_norm_bwd(dys, y, proj, vecs["ssd_norm_w"], tm, "mix_gate_norm_bwd")
    dxbc_act, ddt, st_ssd = _ssd_bwd(dy, xbc, proj, vecs["ssd_par"], hprev, "mix_ssd_bwd")
    dxbc, st_conv = _conv_bwd(dxbc_act, proj, vecs["conv_w"], vecs["conv_b"], tm, "mix_conv_bwd")
    dx1, st2, df1 = _mix_bwd_dh(dz, dxbc, du, ddt, win_t, x1, dx2, mixb, vecs["mix_norm"], mod, min(tm, 256), "mix_bwd_dh")
    d_win = (_mm_tn(dz, h2, 1024, tk, "mix_dw_in_z"), _mm_tn(dxbc, h2, 1024, tk, "mix_dw_in_xbc"),
             _mm_tn(du, h2, 1024, tk, "mix_dw_in_u"), _mm_tn(ddt, h2, LANE, tk, "mix_dw_in_dt"))
    dx0, _, st1, dw1 = _ffn_backward(dx1, df1, s1, wff1, vecs["ffn1_norm"], mod, 0, None, tm, "ffn1")
    stats = dict(ffn1=st1, mix=st2, ffn2=st3, loss=st_loss, pool=st_pool, gn=st_gn, ssd=st_ssd, conv=st_conv)
    return dx0, stats, dw1, dw3, d_win, d_wout, d_pool_w


HBM_SPEC = pl.BlockSpec(memory_space=pltpu.HBM)


def _mesh_pos():
    return lax.axis_index("x"), lax.axis_index("y"), lax.axis_index("c")


def _other_chips(x, y):
    return [(1 - x, y), (x, 1 - y), (1 - x, 1 - y)]


def _all_gather(src, regions, name):
    total, cols = src.shape
    assert sum(r for _, r in regions) == total
    body = _all_gather_body(regions, total, False)
    return pl.pallas_call(
        body, name=name,
        out_shape=jax.ShapeDtypeStruct((N_DEV * total, cols), src.dtype),
        in_specs=[HBM_SPEC], out_specs=HBM_SPEC,
        scratch_shapes=[pltpu.SemaphoreType.DMA((7,)), pltpu.SemaphoreType.DMA((7,)), pltpu.SemaphoreType.DMA],
    )(src)


def _all_gather_async(src, regions, name, collective_id):
    total, cols = src.shape
    assert sum(r for _, r in regions) == total
    return pl.kernel(
        _all_gather_body(regions, total, True), name=name,
        out_type=jax.ShapeDtypeStruct((N_DEV * total, cols), src.dtype),
        mesh=plsc.ScalarSubcoreMesh(axis_name="seq", num_cores=1),
        scratch_types=(pltpu.SemaphoreType.DMA((7,)), pltpu.SemaphoreType.DMA((7,)), pltpu.SemaphoreType.DMA),
        compiler_params=pltpu.CompilerParams(collective_id=collective_id))(src)


def _all_gather_body(regions, total, handshake):
    def body(src_ref, out_ref, send_sems, recv_sems, local_sem):
        x, y, c = _mesh_pos()
        me, sibling = (x, y, c), (x, y, 1 - c)
        chips = _other_chips(x, y)
        if handshake:
            barrier = pltpu.get_barrier_semaphore()
            for peer in [sibling] + [(*chip, c) for chip in chips]:
                pl.semaphore_signal(barrier, inc=1, device_id=peer, device_id_type=MESH)
            pl.semaphore_wait(barrier, 4)

        def rows_of(dev, off, rows):
            start = pl.multiple_of(N_DEV * off + (4 * dev[0] + 2 * dev[1] + dev[2]) * rows, 8)
            return out_ref.at[pl.ds(start, rows), :]

        def copies(k, block, to, from_src):
            out = []
            for off, rows in regions:
                dst = rows_of(block, off, rows)
                out.append(pltpu.make_async_remote_copy(
                    src_ref=src_ref.at[pl.ds(off, rows), :] if from_src else dst, dst_ref=dst,
                    send_sem=send_sems.at[k], recv_sem=recv_sems.at[k], device_id=to, device_id_type=MESH))
            return out

        def drain(k):
            whole = out_ref.at[pl.ds(0, total), :]
            return pltpu.make_async_remote_copy(src_ref=whole, dst_ref=whole, send_sem=send_sems.at[k],
                                                recv_sem=recv_sems.at[k], device_id=me, device_id_type=MESH)

        for off, rows in regions:
            pltpu.make_async_copy(src_ref.at[pl.ds(off, rows), :], rows_of(me, off, rows), local_sem).start()
        first = copies(0, me, sibling, True)
        for j, chip in enumerate(chips):
            first += copies(1 + j, me, (*chip, c), True)
        for cp in first:
            cp.start()
        for j, chip in enumerate(chips):
            drain(1 + j).wait_recv()
            for cp in copies(4 + j, (*chip, c), sibling, False):
                cp.start()
        drain(0).wait_recv()
        for j in range(3):
            drain(4 + j).wait_recv()
        for k in range(7):
            drain(k).wait_send()
        pltpu.make_async_copy(src_ref, out_ref.at[pl.ds(0, total), :], local_sem).wait()

    return body


def _rs_pair(grads, total, name, collective_id):
    cols = grads[0][0].shape[1]
    sent = sum(rows for _, _, rows in grads)
    n = len(grads)

    def body(*refs):
        g_refs, recv_ref, send_sem, recv_sem = refs[:n], refs[n], refs[n + 1], refs[n + 2]
        x, y, c = _mesh_pos()
        sibling = (x, y, 1 - c)
        barrier = pltpu.get_barrier_semaphore()
        pl.semaphore_signal(barrier, inc=1, device_id=sibling, device_id_type=MESH)
        pl.semaphore_wait(barrier, 1)
        for q in range(4):
            for g_ref, (_, off, rows) in zip(g_refs, grads):
                theirs = g_ref.at[pl.ds(pl.multiple_of((2 * q + 1 - c) * rows, 8), rows), :]
                pltpu.make_async_remote_copy(
                    src_ref=theirs, dst_ref=recv_ref.at[q, pl.ds(off, rows), :], send_sem=send_sem, recv_sem=recv_sem,
                    device_id=sibling, device_id_type=MESH).start()
        everything = recv_ref.at[:, pl.ds(0, sent), :]
        whole = pltpu.make_async_remote_copy(src_ref=everything, dst_ref=everything, send_sem=send_sem,
                                             recv_sem=recv_sem, device_id=sibling, device_id_type=MESH)
        whole.wait_send()
        whole.wait_recv()

    return pl.kernel(
        body, name=name, out_type=jax.ShapeDtypeStruct((4, total, cols), F32),
        mesh=plsc.ScalarSubcoreMesh(axis_name="seq", num_cores=1),
        scratch_types=(pltpu.SemaphoreType.DMA, pltpu.SemaphoreType.DMA),
        compiler_params=pltpu.CompilerParams(collective_id=collective_id))(*[g for g, _, _ in grads])


def _pair_sum(g, from_sibling, off, rows, pos, name):
    cols = g.shape[1]

    def body(pos_ref, g_ref, r_ref, o_ref, ob_ref):
        s = g_ref[...] + r_ref[...]
        o_ref[...] = s
        ob_ref[...] = s.astype(BF16)

    out = pl.BlockSpec((None, rows, cols), lambda q, pos_ref: (q, 0, 0))
    return pl.pallas_call(
        body, name=name,
        grid_spec=pltpu.PrefetchScalarGridSpec(
            num_scalar_prefetch=1, grid=(4,),
            in_specs=[pl.BlockSpec((None, None, rows, cols), lambda q, pos_ref: (q, pos_ref[0], 0, 0)),
                      pl.BlockSpec((None, rows, cols), lambda q, pos_ref: (q, off // rows, 0))],
            out_specs=[out, out]),
        out_shape=[jax.ShapeDtypeStruct((4, rows, cols), F32), jax.ShapeDtypeStruct((4, rows, cols), BF16)],
        compiler_params=_cp(("parallel",)))(pos, g.reshape(4, 2, rows, cols), from_sibling)


def _rs_chips(parts, total, name, collective_id):
    cols = parts[0][0].shape[2]
    sent = sum(rows for _, _, rows in parts)
    n = len(parts)

    def body(*refs):
        p_refs, out_ref, send_sems, recv_sems = refs[:n], refs[n], refs[n + 1], refs[n + 2]
        x, y, c = _mesh_pos()
        chips = _other_chips(x, y)
        barrier = pltpu.get_barrier_semaphore()
        for chip in chips:
            pl.semaphore_signal(barrier, inc=1, device_id=(*chip, c), device_id_type=MESH)
        pl.semaphore_wait(barrier, 3)
        for j, chip in enumerate(chips):
            q = 2 * chip[0] + chip[1]
            for p_ref, (_, off, rows) in zip(p_refs, parts):
                pltpu.make_async_remote_copy(
                    src_ref=p_ref.at[q], dst_ref=out_ref.at[j, pl.ds(off, rows), :], send_sem=send_sems.at[j],
                    recv_sem=recv_sems.at[j], device_id=(*chip, c), device_id_type=MESH).start()
        for j, chip in enumerate(chips):
            everything = out_ref.at[j, pl.ds(0, sent), :]
            whole = pltpu.make_async_remote_copy(src_ref=everything, dst_ref=everything, send_sem=send_sems.at[j],
                                                 recv_sem=recv_sems.at[j], device_id=(*chip, c), device_id_type=MESH)
            whole.wait_recv()
            whole.wait_send()

    return pl.kernel(
        body, name=name, out_type=jax.ShapeDtypeStruct((3, total, cols), BF16),
        mesh=plsc.ScalarSubcoreMesh(axis_name="seq", num_cores=1),
        scratch_types=(pltpu.SemaphoreType.DMA((3,)), pltpu.SemaphoreType.DMA((3,))),
        compiler_params=pltpu.CompilerParams(collective_id=collective_id))(*[p for p, _, _ in parts])


def _chip_sum(p, from_chips, off, rows, pos, name):
    cols = p.shape[2]

    def body(pos_ref, p_ref, r_ref, o_ref):
        acc = p_ref[...]
        for j in range(3):
            acc = acc + r_ref[j].astype(F32)
        o_ref[...] = acc

    return pl.pallas_call(
        body, name=name,
        grid_spec=pltpu.PrefetchScalarGridSpec(
            num_scalar_prefetch=1, grid=(1,),
            in_specs=[pl.BlockSpec((None, rows, cols), lambda i, pos_ref: (pos_ref[1], 0, 0)),
                      pl.BlockSpec((3, rows, cols), lambda i, pos_ref: (0, off // rows, 0))],
            out_specs=pl.BlockSpec((rows, cols), lambda i, pos_ref: (0, 0))),
        out_shape=jax.ShapeDtypeStruct((rows, cols), F32),
        compiler_params=_cp(("arbitrary",)))(pos, p, from_chips)


def _row_tile(rows, cap):
    t = min(rows, cap)
    while rows % t or t % 8:
        t -= 8
    return t


def _ada_mod(c_all, w, b, name):
    n = w.shape[1]

    def body(c_ref, w_ref, b_ref, o_ref):
        cv = c_ref[...]
        o_ref[...] = _exact_dot(cv * _sigmoid(cv), w_ref[...]) + b_ref[...]

    return pl.pallas_call(body, name=name, out_shape=jax.ShapeDtypeStruct((N_DEV, n), F32),
                          compiler_params=pltpu.CompilerParams(vmem_limit_bytes=VMEM_LIMIT))(c_all, w, b)


def _ada_grad(c_all, dmod, name):
    n = dmod.shape[1]

    def body(c_ref, d_ref, o_ref):
        cv = c_ref[...]
        o_ref[...] = _dot(cv * _sigmoid(cv), d_ref[...], 0, 0, lax.Precision.HIGHEST)

    return pl.pallas_call(body, name=name, out_shape=jax.ShapeDtypeStruct((D, n), F32),
                          compiler_params=pltpu.CompilerParams(vmem_limit_bytes=VMEM_LIMIT))(c_all, dmod)


def _adamw(w, g, m, v, name):
    rows, cols = w.shape
    tr = _row_tile(rows, 256) if rows % 8 == 0 else rows
    c1 = 1.0 - ADAM_B1 ** ADAM_STEP
    c2 = 1.0 - ADAM_B2 ** ADAM_STEP

    def body(w_ref, g_ref, m_ref, v_ref, d_ref, mo_ref, vo_ref):
        gv = g_ref[...]
        mn = ADAM_B1 * m_ref[...] + (1.0 - ADAM_B1) * gv
        vn = ADAM_B2 * v_ref[...] + (1.0 - ADAM_B2) * (gv * gv)
        mo_ref[...] = mn
        vo_ref[...] = vn
        d_ref[...] = -ADAM_LR * ((mn / c1) / (jnp.sqrt(vn / c2) + ADAM_EPS) + ADAM_WD * w_ref[...])

    spec = pl.BlockSpec((tr, cols), lambda i: (i, 0))
    shape = jax.ShapeDtypeStruct((rows, cols), F32)
    return pl.pallas_call(body, name=name, grid=(rows // tr,), in_specs=[spec] * 4, out_specs=[spec] * 3,
                          out_shape=[shape] * 3, compiler_params=_cp(("parallel",)))(w, g, m, v)


def _sum8_loss(v, loss_row, name):
    rows = v.shape[0] // N_DEV

    def body(v_ref, o_ref, l_ref):
        acc = v_ref[0:rows, :]
        for k in range(1, N_DEV):
            acc = acc + v_ref[k * rows:(k + 1) * rows, :]
        o_ref[...] = acc
        part = jnp.sum(acc[loss_row:loss_row + 8, :], axis=0, keepdims=True)
        l_ref[...] = jnp.broadcast_to(jnp.sum(part, axis=1, keepdims=True), (8, LANE))

    return pl.pallas_call(body, name=name,
                          out_shape=[jax.ShapeDtypeStruct((rows, LANE), F32), jax.ShapeDtypeStruct((8, LANE), F32)],
                          compiler_params=pltpu.CompilerParams(vmem_limit_bytes=VMEM_LIMIT))(v)


WEIGHT_NAMES = ("w_ada", "b_ada", "ffn1_norm", "ffn1_w_gate", "ffn1_w_up", "ffn1_w_down", "mix_norm", "w_in",
                "conv_w", "conv_b", "dt_bias", "a_log", "d_skip", "ssd_norm_w", "pool_w", "pool_b", "pool_scale",
                "w_out", "ffn2_norm", "ffn2_w_gate", "ffn2_w_up", "ffn2_w_down", "final_norm")

FF_SHARD = FF // N_DEV
IN_SHARD = D_IN // N_DEV
IN_SHARD_PAD = 528
OUT_SHARD = 2 * D_SSD // N_DEV
ADA_SHARD = 9 * D // N_DEV
POOL_SHARD_ROWS = 4 * 32 * POOL_GW // D
PACK = dict(gate1=(0, FF_SHARD), up1=(352, FF_SHARD), down1=(704, FF_SHARD), gate2=(1056, FF_SHARD),
            up2=(1408, FF_SHARD), down2=(1760, FF_SHARD), w_out=(2112, OUT_SHARD), w_in=(2368, IN_SHARD_PAD),
            pool_w=(2896, POOL_SHARD_ROWS))
PACK_W_ROWS = 2896
GPACK = dict(w_in=(0, IN_SHARD_PAD), w_out=(768, OUT_SHARD), pool_w=(1024, POOL_SHARD_ROWS),
             gate1=(0, FF_SHARD), up1=(352, FF_SHARD), down1=(704, FF_SHARD),
             gate2=(0, FF_SHARD), up2=(352, FF_SHARD), down2=(704, FF_SHARD))
GROUP_ROWS = 1056

SMALL_ROWS = dict(dmod=(0, 72), ffn1_norm=(72, 8), mix_norm=(80, 8), ffn2_norm=(88, 8), final_norm=(96, 8),
                  ssd_norm_w=(104, 8), pool_scale=(112, 8), conv_b=(120, 16), conv_w=(136, 64), pool_b=(200, 8),
                  ssd=(208, 3), loss=(216, 8))
SMALL_TOTAL = 224


def _rows128(v, rows):
    flat = v.reshape(-1)
    return jnp.pad(flat, (0, rows * LANE - flat.shape[0])).reshape(rows, LANE)


def _pad_lanes(v):
    return jnp.pad(v.reshape(-1), (0, LANE - v.size))


def kernel(x, c, w_ada, b_ada, ffn1_norm, ffn1_w_gate, ffn1_w_up, ffn1_w_down, mix_norm, w_in, conv_w, conv_b, dt_bias, a_log, d_skip, ssd_norm_w, pool_w, pool_b, pool_scale, w_out, ffn2_norm, ffn2_w_gate, ffn2_w_up, ffn2_w_down, final_norm, loss_target, m_w_ada, m_b_ada, m_ffn1_norm, m_ffn1_w_gate, m_ffn1_w_up, m_ffn1_w_down, m_mix_norm, m_w_in, m_conv_w, m_conv_b, m_dt_bias, m_a_log, m_d_skip, m_ssd_norm_w, m_pool_w, m_pool_b, m_pool_scale, m_w_out, m_ffn2_norm, m_ffn2_w_gate, m_ffn2_w_up, m_ffn2_w_down, m_final_norm, v_w_ada, v_b_ada, v_ffn1_norm, v_ffn1_w_gate, v_ffn1_w_up, v_ffn1_w_down, v_mix_norm, v_w_in, v_conv_w, v_conv_b, v_dt_bias, v_a_log, v_d_skip, v_ssd_norm_w, v_pool_w, v_pool_b, v_pool_scale, v_w_out, v_ffn2_norm, v_ffn2_w_gate, v_ffn2_w_up, v_ffn2_w_down, v_final_norm):
    given = dict(locals())
    w = {n: given[n] for n in WEIGHT_NAMES}
    m = {n: given["m_" + n] for n in WEIGHT_NAMES}
    v = {n: given["v_" + n] for n in WEIGHT_NAMES}
    mx, my, mc = _mesh_pos()
    me = 4 * mx + 2 * my + mc

    small = jnp.concatenate([c.reshape(-1), conv_w.reshape(-1), pool_b.reshape(-1), pool_w.reshape(-1)])
    small_rows = 280
    gs = _all_gather(_rows128(small, small_rows), [(0, small_rows)], "ag_small").reshape(N_DEV, small_rows * LANE)
    c_all = gs[:, 0:D]
    conv_w_full = gs[:, 1024:2048].reshape(N_DEV, 4, 256).transpose(1, 0, 2).reshape(4, D_XBC)
    pool_b_full = gs[:, 2048:2176].reshape(N_DEV, 4, 32).transpose(1, 0, 2).reshape(1, D_POOL)
    pool_w_full = gs[:, 2176:2176 + 32768].reshape(N_DEV, 4, 32, POOL_GW).transpose(1, 0, 2, 3).reshape(4, POOL_GW, POOL_GW).astype(BF16)

    b_ada_cols = lax.dynamic_slice(b_ada, (0, me * ADA_SHARD), (1, ADA_SHARD))
    mod_part = _ada_mod(c_all, w_ada[0], b_ada_cols, "ada_mod")
    mod_all = _all_gather(mod_part, [(0, N_DEV)], "ag_mod").reshape(N_DEV, N_DEV, ADA_SHARD)
    mod = lax.dynamic_index_in_dim(mod_all, me, axis=1, keepdims=False).reshape(9, D)

    win_t_shard = jnp.pad(w_in[0].T, ((0, IN_SHARD_PAD - IN_SHARD), (0, 0)))
    pack_a = jnp.concatenate([ffn1_w_gate[0].T, ffn1_w_up[0].T], axis=0).astype(BF16)
    pack_d = ffn1_w_down[0].astype(BF16)
    pack_b = jnp.concatenate([ffn2_w_gate[0].T, ffn2_w_up[0].T, ffn2_w_down[0], w_out[0], win_t_shard], axis=0).astype(BF16)
    pack_a, pack_d, pack_b, mod = lax.optimization_barrier((pack_a, pack_d, pack_b, mod))
    ffn_regions = [(0, FF_SHARD), (FF_SHARD, FF_SHARD), (2 * FF_SHARD, FF_SHARD)]
    full_a = _all_gather_async(pack_a, ffn_regions[0:2], "ag_weights_ffn1_in", 1)
    full_d = _all_gather_async(pack_d, ffn_regions[0:1], "ag_weights_ffn1_out", 2)
    full_b = _all_gather_async(pack_b, ffn_regions + [(3 * FF_SHARD, OUT_SHARD), (3 * FF_SHARD + OUT_SHARD, IN_SHARD_PAD)],
                               "ag_weights_rest", 9)

    def later_weights(x1):
        rest, x1 = lax.optimization_barrier((full_b, x1))
        o_out = N_DEV * 3 * FF_SHARD
        o_in = o_out + N_DEV * OUT_SHARD
        win_g = rest[o_in:o_in + N_DEV * IN_SHARD_PAD].reshape(N_DEV, IN_SHARD_PAD, D)[:, :IN_SHARD].reshape(D_IN, D)
        win_t = jnp.concatenate([win_g[0:1024], win_g[1024:3072], win_g[3088:4112], win_g[3072:3088],
                                 jnp.zeros((D_IN_PAD - D_IN, D), BF16)], axis=0)
        return x1, (rest, rest, 2), win_t, rest[o_out:o_in]

    vecs = dict(ffn1_norm=ffn1_norm, mix_norm=mix_norm, ffn2_norm=ffn2_norm, final_norm=final_norm.reshape(1, D),
                conv_w=conv_w_full, conv_b=conv_b, ssd_norm_w=ssd_norm_w, pool_b=pool_b_full, pool_scale=pool_scale,
                ssd_par=jnp.concatenate([_pad_lanes(dt_bias)[None], _pad_lanes(a_log)[None], _pad_lanes(d_skip)[None],
                                         jnp.zeros((5, LANE), F32)], axis=0))
    dx0, st, dw1, dw3, d_win, d_wout, d_pool_w = _local_step(
        x[0], loss_target[0], mod, (full_a, full_d, 0), later_weights, pool_w_full, vecs, min(512, x.shape[1]))

    dwin = jnp.concatenate([d_win[0], d_win[1], d_win[3][0:16], d_win[2]], axis=0)
    dwin = jnp.pad(dwin.reshape(N_DEV, IN_SHARD, D), ((0, 0), (0, IN_SHARD_PAD - IN_SHARD), (0, 0))).reshape(N_DEV * IN_SHARD_PAD, D)
    dwout = jnp.concatenate(d_wout, axis=0)
    dpool = d_pool_w.reshape(4, N_DEV, 32, POOL_GW).transpose(1, 0, 2, 3).reshape(N_DEV * POOL_SHARD_ROWS, D)
    pos = jnp.stack([mc, 2 * mx + my]).astype(jnp.int32)
    by_key = dict(zip(("gate1", "up1", "down1", "gate2", "up2", "down2", "w_out", "w_in", "pool_w"),
                      (*dw1, *dw3, dwout, dwin, dpool)))
    reduced = {}
    for tag, keys, cid in (("ffn2", ("gate2", "up2", "down2"), 3), ("mix", ("w_in", "w_out", "pool_w"), 5),
                           ("ffn1", ("gate1", "up1", "down1"), 7)):
        grads = [(by_key[k], *GPACK[k]) for k in keys]
        from_sibling = _rs_pair(grads, GROUP_ROWS, f"rs_pair_{tag}", cid)
        pairs = {k: _pair_sum(g, from_sibling, off, rows, pos, f"rs_pair_sum_{k}") for (g, off, rows), k in zip(grads, keys)}
        from_chips = _rs_chips([(pairs[k][1], *GPACK[k]) for k in keys], GROUP_ROWS, f"rs_chips_{tag}", cid + 1)
        for k in keys:
            reduced[k] = _chip_sum(pairs[k][0], from_chips, *GPACK[k], pos, f"rs_chip_sum_{k}")

    def shard(k, rows=None):
        return reduced[k] if rows is None else reduced[k][0:rows]

    dmod = jnp.concatenate([st["ffn1"][0:3], st["mix"][0:3], st["ffn2"][0:3]], axis=0)
    sg = jnp.concatenate([
        dmod.reshape(-1), st["ffn1"][3], st["mix"][3], st["ffn2"][3], st["loss"][0], st["gn"][0], st["pool"][1],
        st["conv"][4], st["conv"][0:4].reshape(-1), st["pool"][0], st["ssd"][0:3].reshape(-1),
        jnp.zeros((5 * LANE,), F32), st["loss"][1]])
    sg_all = _all_gather(sg.reshape(SMALL_TOTAL, LANE), [(0, SMALL_TOTAL)], "ag_small_grads")
    tot, loss_b = _sum8_loss(sg_all, SMALL_ROWS["loss"][0], "small_sum")
    loss = loss_b[0, 0]
    dmod_all = sg_all.reshape(N_DEV, SMALL_TOTAL * LANE)[:, 0:9 * D]
    g_w_ada = _ada_grad(c_all, lax.dynamic_slice(dmod_all, (0, me * ADA_SHARD), (N_DEV, ADA_SHARD)), "ada_grad")

    def tot_rows(k):
        off, n = SMALL_ROWS[k]
        return tot[off:off + n].reshape(-1)

    g_conv_w = lax.dynamic_slice(tot_rows("conv_w").reshape(4, D_XBC), (0, me * 256), (4, 256))
    g_pool_b = lax.dynamic_slice(tot_rows("pool_b").reshape(4, POOL_GW), (0, me * 32), (4, 32))
    g_ssd = tot_rows("ssd").reshape(3, LANE)
    grad = {
        "w_ada": g_w_ada[None], "b_ada": tot_rows("dmod").reshape(1, 9 * D),
        "ffn1_norm": tot_rows("ffn1_norm")[None], "mix_norm": tot_rows("mix_norm")[None],
        "ffn2_norm": tot_rows("ffn2_norm")[None], "final_norm": tot_rows("final_norm"),
        "ssd_norm_w": tot_rows("ssd_norm_w")[None], "pool_scale": tot_rows("pool_scale")[None],
        "conv_b": tot_rows("conv_b")[None], "conv_w": g_conv_w[None], "pool_b": g_pool_b[None],
        "dt_bias": g_ssd[0:1, 0:N_HEADS], "a_log": g_ssd[1:2, 0:N_HEADS], "d_skip": g_ssd[2:3, 0:N_HEADS],
        "ffn1_w_gate": shard("gate1").T[None], "ffn1_w_up": shard("up1").T[None], "ffn1_w_down": shard("down1")[None],
        "ffn2_w_gate": shard("gate2").T[None], "ffn2_w_up": shard("up2").T[None], "ffn2_w_down": shard("down2")[None],
        "w_out": shard("w_out")[None], "w_in": shard("w_in", IN_SHARD).T[None],
        "pool_w": shard("pool_w").reshape(1, 4, 32, POOL_GW),
    }

    big = ("w_ada", "ffn1_w_gate", "ffn1_w_up", "ffn1_w_down", "w_in", "pool_w", "w_out",
           "ffn2_w_gate", "ffn2_w_up", "ffn2_w_down")
    delta, new_m, new_v = {}, {}, {}
    for n in big:
        shp = w[n].shape
        two_d = (shp[-3] * shp[-2], shp[-1]) if n == "pool_w" else shp[-2:]
        d_, m_, v_ = _adamw(w[n].reshape(two_d), grad[n].reshape(two_d), m[n].reshape(two_d), v[n].reshape(two_d),
                            f"adamw_{n}")
        delta[n], new_m[n], new_v[n] = d_.reshape(shp), m_.reshape(shp), v_.reshape(shp)
    small_names = [n for n in WEIGHT_NAMES if n not in big]
    sizes = [LANE if w[n].size < LANE else w[n].size for n in small_names]
    small_rows_adam = -(-sum(sizes) // (8 * LANE)) * 8

    def pack_small(t):
        return _rows128(jnp.concatenate([_pad_lanes(t[n]) if t[n].size < LANE else t[n].reshape(-1) for n in small_names]),
                        small_rows_adam)

    d_s, m_s, v_s = _adamw(pack_small(w), pack_small(grad), pack_small(m), pack_small(v), "adamw_small")
    off = 0
    for n, size in zip(small_names, sizes):
        for res, packed in ((delta, d_s), (new_m, m_s), (new_v, v_s)):
            res[n] = packed.reshape(-1)[off:off + w[n].size].reshape(w[n].shape)
        off += size

    return (loss, dx0[None], *[grad[n] for n in WEIGHT_NAMES], *[delta[n] for n in WEIGHT_NAMES],
            *[new_m[n] for n in WEIGHT_NAMES], *[new_v[n] for n in WEIGHT_NAMES])
```

```python
import functools
import math

import jax
import jax.numpy as jnp
from jax import lax
from jax.experimental import pallas as pl
from jax.experimental.pallas import tpu as pltpu
from jax.experimental.pallas import tpu_sc as plsc

F32 = jnp.float32
BF16 = jnp.bfloat16
MESH = pl.DeviceIdType.MESH

N_DEV = 8
D = 1024
FF = 2816
D_SSD = 1024
N_HEADS = 16
HEAD_DIM = 64
N_GROUPS = 4
N_STATE = 128
CHUNK = 128
GROUP_W = D_SSD // N_GROUPS
D_XBC = D_SSD + 2 * N_GROUPS * N_STATE
D_POOL = 1024
POOL_WINDOWS = (2, 4, 8, 16)
POOL_GW = 256
D_IN = 4112
D_IN_PAD = 4224
COL_Z, COL_XBC, COL_U, COL_DT = 0, 1024, 3072, 4096
EPS = 1e-6
FFN_RES = 0.5
LANE = 128
HALO = 16

ADAM_LR, ADAM_B1, ADAM_B2, ADAM_EPS, ADAM_WD, ADAM_STEP = 0.001, 0.9, 0.999, 1e-08, 0.01, 10

VMEM_LIMIT = 56 << 20


def _cp(sem):
    return pltpu.CompilerParams(dimension_semantics=sem, vmem_limit_bytes=VMEM_LIMIT)


def _dot(a, b, ca, cb, prec=None):
    return lax.dot_general(a, b, (((ca,), (cb,)), ((), ())), precision=prec,
                           preferred_element_type=F32)


def _exact_dot(a, b):
    return _dot(a, b, 1, 0, lax.Precision.HIGHEST)


def _sigmoid(v):
    return 1.0 / (1.0 + jnp.exp(-v))


def _silu_grad(v, sg):
    return sg * (1.0 + v * (1.0 - sg))


def _mm_nt(a, bt, tm, tn, out_dtype, name):
    m, k = a.shape
    n = bt.shape[0]

    def body(a_ref, b_ref, o_ref):
        o_ref[...] = _dot(a_ref[...], b_ref[...], 1, 1).astype(out_dtype)

    return pl.pallas_call(
        body, name=name, grid=(n // tn, m // tm),
        in_specs=[pl.BlockSpec((tm, k), lambda j, i: (i, 0)),
                  pl.BlockSpec((tn, k), lambda j, i: (j, 0))],
        out_specs=pl.BlockSpec((tm, tn), lambda j, i: (i, j)),
        out_shape=jax.ShapeDtypeStruct((m, n), out_dtype),
        compiler_params=_cp(("parallel", "parallel")))(a, bt)


def _mm_tn(a, b, tm, tk, name):
    kk, m = a.shape
    n = b.shape[1]
    nk = kk // tk

    def body(a_ref, b_ref, o_ref, acc):
        k = pl.program_id(1)

        @pl.when(k == 0)
        def _():
            acc[...] = jnp.zeros_like(acc)

        acc[...] += _dot(a_ref[...], b_ref[...], 0, 0)

        @pl.when(k == nk - 1)
        def _():
            o_ref[...] = acc[...]

    return pl.pallas_call(
        body, name=name, grid=(m // tm, nk),
        in_specs=[pl.BlockSpec((tk, tm), lambda i, k: (k, i)),
                  pl.BlockSpec((tk, n), lambda i, k: (k, 0))],
        out_specs=pl.BlockSpec((tm, n), lambda i, k: (i, 0)),
        out_shape=jax.ShapeDtypeStruct((m, n), F32),
        scratch_shapes=[pltpu.VMEM((tm, n), F32)],
        compiler_params=_cp(("parallel", "arbitrary")))(a, b)


def _modulated(xv, wn, mod_ref, k):
    r = lax.rsqrt(jnp.mean(xv * xv, axis=-1, keepdims=True) + EPS)
    hn = xv * r * wn
    return (hn * (1.0 + mod_ref[3 * k + 1:3 * k + 2, :]) + mod_ref[3 * k:3 * k + 1, :]).astype(BF16)


def _prenorm(x, wn, mod, k, tm, name):
    seq = x.shape[0]

    def body(x_ref, wn_ref, mod_ref, h_ref):
        h_ref[...] = _modulated(x_ref[...], wn_ref[...], mod_ref, k)

    return pl.pallas_call(
        body, name=name, grid=(seq // tm,),
        in_specs=[pl.BlockSpec((tm, D), lambda i: (i, 0)),
                  pl.BlockSpec((1, D), lambda i: (0, 0)),
                  pl.BlockSpec((9, D), lambda i: (0, 0))],
        out_specs=pl.BlockSpec((tm, D), lambda i: (i, 0)),
        out_shape=jax.ShapeDtypeStruct((seq, D), BF16),
        compiler_params=_cp(("parallel",)))(x, wn, mod)


def _norm_bwd(dh, xv, dxo, branch, wn, sc, res, stats_ref, first):
    r = lax.rsqrt(jnp.mean(xv * xv, axis=-1, keepdims=True) + EPS)
    xn = xv * r
    dhn = dh * (1.0 + sc)
    dxn = dhn * wn
    dx = dxo + r * (dxn - xn * jnp.mean(dxn * xn, axis=-1, keepdims=True))
    rows = jnp.concatenate([
        jnp.sum(dh, axis=0, keepdims=True),
        jnp.sum(dh * (xn * wn), axis=0, keepdims=True),
        jnp.sum(branch * dxo, axis=0, keepdims=True) * res,
        jnp.sum(dhn * xn, axis=0, keepdims=True),
        jnp.zeros((4, D), F32)], axis=0)

    @pl.when(first)
    def _():
        stats_ref[...] = rows

    @pl.when(jnp.logical_not(first))
    def _():
        stats_ref[...] += rows

    return dx


def _loss_head(x3, wf, tgt, mod, tm, name):
    seq = x3.shape[0]

    def body(x_ref, w_ref, t_ref, mod_ref, dx_ref, df_ref, st_ref):
        xv = x_ref[...]
        wv = w_ref[...]
        r = lax.rsqrt(jnp.mean(xv * xv, axis=-1, keepdims=True) + EPS)
        xn = xv * r
        e = xn * wv - t_ref[...]
        dy = e * (1.0 / D)
        dxn = dy * wv
        dx = r * (dxn - xn * jnp.mean(dxn * xn, axis=-1, keepdims=True))
        dx_ref[...] = dx
        df_ref[...] = (dx * (FFN_RES * mod_ref[8:9, :])).astype(BF16)
        rows = jnp.concatenate([
            jnp.sum(dy * xn, axis=0, keepdims=True),
            jnp.sum(e * e, axis=0, keepdims=True) * (0.5 / D),
            jnp.zeros((6, D), F32)], axis=0)

        @pl.when(pl.program_id(0) == 0)
        def _():
            st_ref[...] = rows

        @pl.when(pl.program_id(0) != 0)
        def _():
            st_ref[...] += rows

    return pl.pallas_call(
        body, name=name, grid=(seq // tm,),
        in_specs=[pl.BlockSpec((tm, D), lambda i: (i, 0)),
                  pl.BlockSpec((1, D), lambda i: (0, 0)),
                  pl.BlockSpec((tm, D), lambda i: (i, 0)),
                  pl.BlockSpec((9, D), lambda i: (0, 0))],
        out_specs=[pl.BlockSpec((tm, D), lambda i: (i, 0)),
                   pl.BlockSpec((tm, D), lambda i: (i, 0)),
                   pl.BlockSpec((8, D), lambda i: (0, 0))],
        out_shape=[jax.ShapeDtypeStruct((seq, D), F32), jax.ShapeDtypeStruct((seq, D), BF16),
                   jax.ShapeDtypeStruct((8, D), F32)],
        compiler_params=_cp(("arbitrary",)))(x3, wf, tgt, mod)


def _ffn_up(h, w, blk, tm, tn, name):
    seq = h.shape[0]
    nj = FF // tn

    def body(h_ref, wg_ref, wu_ref, g_ref, u_ref, a_ref):
        hv = h_ref[...]
        g = _dot(hv, wg_ref[...], 1, 1)
        u = _dot(hv, wu_ref[...], 1, 1)
        g_ref[...] = g.astype(BF16)
        u_ref[...] = u.astype(BF16)
        a_ref[...] = (g * _sigmoid(g) * u).astype(BF16)

    act = pl.BlockSpec((tm, tn), lambda j, i: (i, j))
    return pl.pallas_call(
        body, name=name, grid=(nj, seq // tm),
        in_specs=[pl.BlockSpec((tm, D), lambda j, i: (i, 0)),
                  pl.BlockSpec((tn, D), lambda j, i: (blk * nj + j, 0)),
                  pl.BlockSpec((tn, D), lambda j, i: ((blk + 1) * nj + j, 0))],
        out_specs=[act, act, act],
        out_shape=[jax.ShapeDtypeStruct((seq, FF), BF16)] * 3,
        compiler_params=_cp(("parallel", "parallel")))(h, w, w)


def _ffn_down(a, w, blk, x, mod, grow, wn_next, k_next, tm, tk, name):
    seq = a.shape[0]
    nk = FF // tk
    chain = k_next is not None

    def body(a_ref, w_ref, x_ref, mod_ref, wn_ref, xo_ref, f_ref, *rest):
        acc = rest[-1]
        k = pl.program_id(1)

        @pl.when(k == 0)
        def _():
            acc[...] = jnp.zeros_like(acc)

        acc[...] += _dot(a_ref[...], w_ref[...], 1, 0)

        @pl.when(k == nk - 1)
        def _():
            f = acc[...]
            f_ref[...] = f.astype(BF16)
            xo = x_ref[...] + (FFN_RES * mod_ref[grow:grow + 1, :]) * f
            xo_ref[...] = xo
            if chain:
                rest[0][...] = _modulated(xo, wn_ref[...], mod_ref, k_next)

    tok = pl.BlockSpec((tm, D), lambda i, k: (i, 0))
    return pl.pallas_call(
        body, name=name, grid=(seq // tm, nk),
        in_specs=[pl.BlockSpec((tm, tk), lambda i, k: (i, k)),
                  pl.BlockSpec((tk, D), lambda i, k: (blk * nk + k, 0)),
                  tok,
                  pl.BlockSpec((9, D), lambda i, k: (0, 0)),
                  pl.BlockSpec((1, D), lambda i, k: (0, 0))],
        out_specs=[tok, tok] + ([tok] if chain else []),
        out_shape=[jax.ShapeDtypeStruct((seq, D), F32), jax.ShapeDtypeStruct((seq, D), BF16)]
        + ([jax.ShapeDtypeStruct((seq, D), BF16)] if chain else []),
        scratch_shapes=[pltpu.VMEM((tm, D), F32)],
        compiler_params=_cp(("parallel", "arbitrary")))(a, w, x, mod, wn_next)


def _ffn_bwd_da(df, w, blk, g, u, tm, tn, name):
    seq = df.shape[0]
    nj = FF // tn

    def body(df_ref, w_ref, g_ref, u_ref, dg_ref, du_ref):
        da = _dot(df_ref[...], w_ref[...], 1, 1)
        gv = g_ref[...].astype(F32)
        uv = u_ref[...].astype(F32)
        sg = _sigmoid(gv)
        dg_ref[...] = (da * uv * _silu_grad(gv, sg)).astype(BF16)
        du_ref[...] = (da * (gv * sg)).astype(BF16)

    act = pl.BlockSpec((tm, tn), lambda j, i: (i, j))
    return pl.pallas_call(
        body, name=name, grid=(nj, seq // tm),
        in_specs=[pl.BlockSpec((tm, D), lambda j, i: (i, 0)),
                  pl.BlockSpec((tn, D), lambda j, i: (blk * nj + j, 0)),
                  act, act],
        out_specs=[act, act],
        out_shape=[jax.ShapeDtypeStruct((seq, FF), BF16)] * 2,
        compiler_params=_cp(("parallel", "parallel")))(df, w, g, u)


def _ffn_bwd_dh(dg, du, w, blk, x, dxo, fb, wn, mod, k, nxt, tm, tk, name):
    seq = x.shape[0]
    nk = FF // tk

    def body(dg_ref, du_ref, wg_ref, wu_ref, x_ref, dxo_ref, f_ref, wn_ref, mod_ref, dx_ref, st_ref, *rest):
        acc = rest[-1]
        kk = pl.program_id(1)
        first = pl.program_id(0) == 0

        @pl.when(kk == 0)
        def _():
            acc[...] = jnp.zeros_like(acc)

        acc[...] += _dot(dg_ref[...], wg_ref[...], 1, 0) + _dot(du_ref[...], wu_ref[...], 1, 0)

        @pl.when(kk == nk - 1)
        def _():
            dx = _norm_bwd(acc[...], x_ref[...], dxo_ref[...], f_ref[...].astype(F32), wn_ref[...],
                           mod_ref[3 * k + 1:3 * k + 2, :], FFN_RES, st_ref, first)
            dx_ref[...] = dx
            if nxt is not None:
                rest[0][...] = (dx * (nxt[1] * mod_ref[nxt[0]:nxt[0] + 1, :])).astype(BF16)

    tok = pl.BlockSpec((tm, D), lambda i, kk: (i, 0))
    return pl.pallas_call(
        body, name=name, grid=(seq // tm, nk),
        in_specs=[pl.BlockSpec((tm, tk), lambda i, kk: (i, kk)),
                  pl.BlockSpec((tm, tk), lambda i, kk: (i, kk)),
                  pl.BlockSpec((tk, D), lambda i, kk: (blk * nk + kk, 0)),
                  pl.BlockSpec((tk, D), lambda i, kk: ((blk + 1) * nk + kk, 0)),
                  tok, tok, tok,
                  pl.BlockSpec((1, D), lambda i, kk: (0, 0)),
                  pl.BlockSpec((9, D), lambda i, kk: (0, 0))],
        out_specs=[tok, pl.BlockSpec((8, D), lambda i, kk: (0, 0))] + ([tok] if nxt is not None else []),
        out_shape=[jax.ShapeDtypeStruct((seq, D), F32), jax.ShapeDtypeStruct((8, D), F32)]
        + ([jax.ShapeDtypeStruct((seq, D), BF16)] if nxt is not None else []),
        scratch_shapes=[pltpu.VMEM((tm, D), F32)],
        compiler_params=_cp(("arbitrary", "arbitrary")))(dg, du, w, w, x, dxo, fb, wn, mod)


def _ffn_forward(x, h, w, mod, k, wn_next, k_next, tm, tag):
    w_gu, w_d, blk_d = w
    g, u, a = _ffn_up(h, w_gu, 0, tm, FF // 2, f"{tag}_up")
    outs = _ffn_down(a, w_d, blk_d, x, mod, 3 * k + 2, wn_next, k_next, tm, FF // 2, f"{tag}_down")
    return outs[0], (outs[2] if k_next is not None else None), (x, h, g, u, a, outs[1])


def _ffn_backward(dxo, df, saved, w, wn, mod, k, nxt, tm, tag):
    w_gu, w_d, blk_d = w
    x, h, g, u, a, fb = saved
    dg, du = _ffn_bwd_da(df, w_d, blk_d, g, u, tm, FF // 2, f"{tag}_bwd_da")
    seq = x.shape[0]
    tk = min(seq, 512)
    d_gate_t = _mm_tn(dg, h, FF // 2, tk, f"{tag}_dw_gate")
    d_up_t = _mm_tn(du, h, FF // 2, tk, f"{tag}_dw_up")
    d_down = _mm_tn(a, df, FF // 2, tk, f"{tag}_dw_down")
    dws, dg, du = lax.optimization_barrier(((d_gate_t, d_up_t, d_down), dg, du))
    outs = _ffn_bwd_dh(dg, du, w_gu, 0, x, dxo, fb, wn, mod, k, nxt, tm, FF // 2, f"{tag}_bwd_dh")
    return outs[0], (outs[2] if nxt is not None else None), outs[1], dws


def _prev_rows(tm, col):
    return pl.BlockSpec((HALO, 1024), lambda i, j: (jnp.maximum(i * (tm // HALO) - 1, 0), col + j))


def _conv_pre(ext, cw, cb, rows):
    pre = cb + cw[3:4, :] * ext
    for s in (1, 2, 3):
        pre = pre + cw[3 - s:4 - s, :] * pltpu.roll(ext, s, 0)
    return pre[HALO:HALO + rows]


def _conv_fwd(proj, cw, cb, tm, name):
    seq = proj.shape[0]

    def body(x_ref, p_ref, cw_ref, cb_ref, o_ref):
        prev = jnp.where(pl.program_id(0) == 0, 0.0, p_ref[...])
        ext = jnp.concatenate([prev, x_ref[...]], axis=0)
        pre = _conv_pre(ext, cw_ref[...], cb_ref[...], tm)
        o_ref[...] = pre * _sigmoid(pre)

    c0 = COL_XBC // 1024
    return pl.pallas_call(
        body, name=name, grid=(seq // tm, 2),
        in_specs=[pl.BlockSpec((tm, 1024), lambda i, j: (i, c0 + j)),
                  _prev_rows(tm, c0),
                  pl.BlockSpec((4, 1024), lambda i, j: (0, j)),
                  pl.BlockSpec((1, 1024), lambda i, j: (0, j))],
        out_specs=pl.BlockSpec((tm, 1024), lambda i, j: (i, j)),
        out_shape=jax.ShapeDtypeStruct((seq, D_XBC), F32),
        compiler_params=_cp(("parallel", "parallel")))(proj, proj, cw, cb)


def _conv_bwd(dact, proj, cw, cb, tm, name):
    seq = proj.shape[0]
    ni = seq // tm

    def body(d_ref, dn_ref, x_ref, p_ref, n_ref, cw_ref, cb_ref, o_ref, st_ref):
        i = pl.program_id(1)
        cwv = cw_ref[...]
        prev = jnp.where(i == 0, 0.0, p_ref[...])
        ext = jnp.concatenate([prev, x_ref[...], n_ref[...]], axis=0)
        pre = _conv_pre(ext, cwv, cb_ref[...], tm + HALO)
        dnext = jnp.where(i == ni - 1, 0.0, dn_ref[...])
        dext = jnp.concatenate([d_ref[...], dnext], axis=0)
        dpre = dext * _silu_grad(pre, _sigmoid(pre))
        n = tm + HALO
        dx = cwv[3:4, :] * dpre
        for s in (1, 2, 3):
            dx = dx + cwv[3 - s:4 - s, :] * pltpu.roll(dpre, n - s, 0)
        o_ref[...] = dx[:tm].astype(BF16)
        dcur = dpre[:tm]
        rows = [jnp.sum(dcur * pltpu.roll(ext, 3 - k, 0)[HALO:HALO + tm], axis=0, keepdims=True) for k in range(3)]
        rows.append(jnp.sum(dcur * ext[HALO:HALO + tm], axis=0, keepdims=True))
        rows.append(jnp.sum(dcur, axis=0, keepdims=True))
        rows.append(jnp.zeros((3, 1024), F32))
        rows = jnp.concatenate(rows, axis=0)

        @pl.when(i == 0)
        def _():
            st_ref[...] = rows

        @pl.when(i != 0)
        def _():
            st_ref[...] += rows

    c0 = COL_XBC // 1024
    return pl.pallas_call(
        body, name=name, grid=(2, ni),
        in_specs=[pl.BlockSpec((tm, 1024), lambda j, i: (i, j)),
                  pl.BlockSpec((HALO, 1024), lambda j, i: (jnp.minimum((i + 1) * (tm // HALO), seq // HALO - 1), j)),
                  pl.BlockSpec((tm, 1024), lambda j, i: (i, c0 + j)),
                  pl.BlockSpec((HALO, 1024), lambda j, i: (jnp.maximum(i * (tm // HALO) - 1, 0), c0 + j)),
                  pl.BlockSpec((HALO, 1024), lambda j, i: (jnp.minimum((i + 1) * (tm // HALO), seq // HALO - 1), c0 + j)),
                  pl.BlockSpec((4, 1024), lambda j, i: (0, j)),
                  pl.BlockSpec((1, 1024), lambda j, i: (0, j))],
        out_specs=[pl.BlockSpec((tm, 1024), lambda j, i: (i, j)),
                   pl.BlockSpec((8, 1024), lambda j, i: (0, j))],
        out_shape=[jax.ShapeDtypeStruct((seq, D_XBC), BF16), jax.ShapeDtypeStruct((8, D_XBC), F32)],
        compiler_params=_cp(("parallel", "arbitrary")))(dact, dact, proj, proj, proj, cw, cb)


def _head_expand():
    r = lax.broadcasted_iota(jnp.int32, (LANE, D_SSD), 0)
    c = lax.broadcasted_iota(jnp.int32, (LANE, D_SSD), 1)
    return (c // HEAD_DIM == r).astype(F32)


def _head_reduce():
    r = lax.broadcasted_iota(jnp.int32, (D_SSD, LANE), 0)
    c = lax.broadcasted_iota(jnp.int32, (D_SSD, LANE), 1)
    return (r // HEAD_DIM == c).astype(F32)


def _ssd_common(dtr, par):
    q = CHUNK
    v = dtr + par[0:1, :]
    dt = jnp.maximum(v, 0.0) + jnp.log(1.0 + jnp.exp(-jnp.abs(v)))
    a = -jnp.exp(par[1:2, :])
    adt = dt * a
    li = lax.broadcasted_iota(jnp.int32, (q, q), 0)
    si = lax.broadcasted_iota(jnp.int32, (q, q), 1)
    causal = li >= si
    acs = _exact_dot(causal.astype(F32), adt)
    expand = _head_expand()
    dt_l = _exact_dot(dt, expand)
    acs_l = _exact_dot(acs, expand)
    par_l = _exact_dot(par, expand)
    last_l = acs_l[q - 1:q, :]
    return dict(v=v, dt=dt, a=a, acs=acs, acs_t=acs.T, causal=causal, dt_l=dt_l, acs_l=acs_l,
                ea_l=jnp.exp(acs_l), ds_l=jnp.exp(last_l - acs_l), cd_l=jnp.exp(last_l), dskip_l=par_l[2:3, :])


def _decay(cm, h):
    seg = cm["acs"][:, h:h + 1] - cm["acs_t"][h:h + 1, :]
    return jnp.exp(jnp.where(cm["causal"], seg, -jnp.inf))


def _lane_mask(r):
    lane = lax.broadcasted_iota(jnp.int32, (1, GROUP_W), 1)
    return lane // HEAD_DIM == r


def _ssd_fwd(xbc, proj, par, name):
    seq = xbc.shape[0]
    nc = seq // CHUNK
    q = CHUNK

    def body(x_ref, dt_ref, par_ref, y_ref, hp_ref, state):
        @pl.when(pl.program_id(0) == 0)
        def _():
            state[...] = jnp.zeros_like(state)

        cm = _ssd_common(dt_ref[...], par_ref[...])
        for g in range(N_GROUPS):
            lo = g * GROUP_W
            xs = x_ref[:, lo:lo + GROUP_W]
            bm = x_ref[:, D_SSD + g * N_STATE:D_SSD + (g + 1) * N_STATE].astype(BF16)
            cmat = x_ref[:, D_SSD + N_GROUPS * N_STATE + g * N_STATE:D_SSD + N_GROUPS * N_STATE + (g + 1) * N_STATE].astype(BF16)
            xdt = xs * cm["dt_l"][:, lo:lo + GROUP_W]
            xdt_b = xdt.astype(BF16)
            cb = _dot(cmat, bm, 1, 1)
            yd = jnp.zeros((q, GROUP_W), F32)
            for r in range(4):
                s_h = (cb * _decay(cm, 4 * g + r)).astype(BF16)
                yd = jnp.where(_lane_mask(r), _dot(s_h, xdt_b, 1, 0), yd)
            hg = state[g]
            hp_ref[0, g] = hg
            yo = _dot(cmat, hg.astype(BF16), 1, 0) * cm["ea_l"][:, lo:lo + GROUP_W]
            y_ref[:, lo:lo + GROUP_W] = yd + yo + cm["dskip_l"][:, lo:lo + GROUP_W] * xs
            xds = (xdt * cm["ds_l"][:, lo:lo + GROUP_W]).astype(BF16)
            state[g] = hg * cm["cd_l"][:, lo:lo + GROUP_W] + _dot(bm, xds, 0, 0)

    return pl.pallas_call(
        body, name=name, grid=(nc,),
        in_specs=[pl.BlockSpec((q, D_XBC), lambda c: (c, 0)),
                  pl.BlockSpec((q, LANE), lambda c: (c, COL_DT // LANE)),
                  pl.BlockSpec((8, LANE), lambda c: (0, 0))],
        out_specs=[pl.BlockSpec((q, D_SSD), lambda c: (c, 0)),
                   pl.BlockSpec((1, N_GROUPS, N_STATE, GROUP_W), lambda c: (c, 0, 0, 0))],
        out_shape=[jax.ShapeDtypeStruct((seq, D_SSD), F32),
                   jax.ShapeDtypeStruct((nc, N_GROUPS, N_STATE, GROUP_W), F32)],
        scratch_shapes=[pltpu.VMEM((N_GROUPS, N_STATE, GROUP_W), F32)],
        compiler_params=_cp(("arbitrary",)))(xbc, proj, par)


def _ssd_bwd(dy, xbc, proj, par, hprev, name):
    seq = xbc.shape[0]
    nc = seq // CHUNK
    q = CHUNK

    def body(dy_ref, x_ref, dt_ref, par_ref, hp_ref, dx_ref, ddt_ref, st_ref, dstate):
        step = pl.program_id(0)

        @pl.when(step == 0)
        def _():
            dstate[...] = jnp.zeros_like(dstate)

        par = par_ref[...]
        cm = _ssd_common(dt_ref[...], par)
        reduce = _head_reduce()
        lane128 = lax.broadcasted_iota(jnp.int32, (1, LANE), 1)
        row128 = lax.broadcasted_iota(jnp.int32, (LANE, 1), 0)
        d_acs = jnp.zeros((q, LANE), F32)
        d_acs_t = jnp.zeros((LANE, q), F32)
        last_terms = []
        acs_terms = []
        dxdt_all = []
        for g in range(N_GROUPS):
            lo = g * GROUP_W
            sl = slice(lo, lo + GROUP_W)
            xs = x_ref[:, sl]
            bm32 = x_ref[:, D_SSD + g * N_STATE:D_SSD + (g + 1) * N_STATE]
            cm32 = x_ref[:, D_SSD + N_GROUPS * N_STATE + g * N_STATE:D_SSD + N_GROUPS * N_STATE + (g + 1) * N_STATE]
            bm = bm32.astype(BF16)
            cmat = cm32.astype(BF16)
            dyg = dy_ref[:, sl]
            dyg_b = dyg.astype(BF16)
            xdt = xs * cm["dt_l"][:, sl]
            xdt_b = xdt.astype(BF16)
            hg = hp_ref[0, g]
            hg_b = hg.astype(BF16)
            dhg = dstate[g]
            dhg_b = dhg.astype(BF16)
            ea = cm["ea_l"][:, sl]
            ds = cm["ds_l"][:, sl]
            cd = cm["cd_l"][:, sl]
            yoff = _dot(cmat, hg_b, 1, 0) * ea
            dw = (dyg * ea).astype(BF16)
            d_c = _dot(dw, hg_b, 1, 1)
            d_hprev = _dot(cmat, dw, 0, 0) + dhg * cd
            t_acs = dyg * yoff
            d_last_g = jnp.sum(dhg * hg, axis=0, keepdims=True) * cd
            xds_b = (xdt * ds).astype(BF16)
            dxds = _dot(bm, dhg_b, 1, 0)
            d_b = _dot(xds_b, dhg_b, 1, 1)
            dxdt = dxds * ds
            t_ds = dxds * xdt * ds
            t_acs = t_acs - t_ds
            d_last_g = d_last_g + jnp.sum(t_ds, axis=0, keepdims=True)
            cb = _dot(cmat, bm, 1, 1)
            d_cb = jnp.zeros((q, q), F32)
            for r in range(4):
                h = 4 * g + r
                dec = _decay(cm, h)
                s_h = cb * dec
                mask = _lane_mask(r)
                d_s = _dot(jnp.where(mask, dyg, 0.0).astype(BF16), xdt_b, 1, 1)
                dxdt = dxdt + jnp.where(mask, _dot(s_h.astype(BF16), dyg_b, 0, 0), 0.0)
                d_cb = d_cb + d_s * dec
                d_m = d_s * s_h
                d_acs = d_acs + jnp.where(lane128 == h, jnp.sum(d_m, axis=1, keepdims=True), 0.0)
                d_acs_t = d_acs_t + jnp.where(row128 == h, jnp.sum(d_m, axis=0, keepdims=True), 0.0)
            d_cb_b = d_cb.astype(BF16)
            d_c = d_c + _dot(d_cb_b, bm, 1, 0)
            d_b = d_b + _dot(d_cb_b, cmat, 0, 0)
            dstate[g] = d_hprev
            dx_ref[:, sl] = dxdt * cm["dt_l"][:, sl] + cm["dskip_l"][:, sl] * dyg
            dx_ref[:, D_SSD + g * N_STATE:D_SSD + (g + 1) * N_STATE] = d_b
            dx_ref[:, D_SSD + N_GROUPS * N_STATE + g * N_STATE:D_SSD + N_GROUPS * N_STATE + (g + 1) * N_STATE] = d_c
            acs_terms.append(t_acs)
            dxdt_all.append(dxdt * xs)
            last_terms.append(d_last_g)
        t_acs_l = jnp.concatenate(acs_terms, axis=1)
        d_dt_l = jnp.concatenate(dxdt_all, axis=1)
        d_last_l = jnp.concatenate(last_terms, axis=1)
        d_acs = d_acs + _exact_dot(t_acs_l, reduce) - d_acs_t.T
        last_row = lax.broadcasted_iota(jnp.int32, (q, 1), 0) == q - 1
        d_acs = d_acs + jnp.where(last_row, _exact_dot(jnp.broadcast_to(d_last_l, (8, D_SSD)), reduce)[0:1, :], 0.0)
        li = lax.broadcasted_iota(jnp.int32, (q, q), 0)
        si = lax.broadcasted_iota(jnp.int32, (q, q), 1)
        d_adt = _exact_dot((si >= li).astype(F32), d_acs)
        d_dt = _exact_dot(d_dt_l, reduce) + d_adt * cm["a"]
        d_dtr = d_dt * _sigmoid(cm["v"])
        ddt_ref[...] = d_dtr.astype(BF16)
        d_skip = _exact_dot(jnp.broadcast_to(jnp.sum(dy_ref[...] * x_ref[:, 0:D_SSD], axis=0, keepdims=True), (8, D_SSD)), reduce)[0:1, :]
        rows = jnp.concatenate([
            jnp.sum(d_dtr, axis=0, keepdims=True),
            jnp.sum(d_adt * cm["dt"], axis=0, keepdims=True) * cm["a"],
            d_skip,
            jnp.zeros((5, LANE), F32)], axis=0)

        @pl.when(step == 0)
        def _():
            st_ref[...] = rows

        @pl.when(step != 0)
        def _():
            st_ref[...] += rows

    rev = lambda c: nc - 1 - c
    return pl.pallas_call(
        body, name=name, grid=(nc,),
        in_specs=[pl.BlockSpec((q, D_SSD), lambda c: (rev(c), 0)),
                  pl.BlockSpec((q, D_XBC), lambda c: (rev(c), 0)),
                  pl.BlockSpec((q, LANE), lambda c: (rev(c), COL_DT // LANE)),
                  pl.BlockSpec((8, LANE), lambda c: (0, 0)),
                  pl.BlockSpec((1, N_GROUPS, N_STATE, GROUP_W), lambda c: (rev(c), 0, 0, 0))],
        out_specs=[pl.BlockSpec((q, D_XBC), lambda c: (rev(c), 0)),
                   pl.BlockSpec((q, LANE), lambda c: (rev(c), 0)),
                   pl.BlockSpec((8, LANE), lambda c: (0, 0))],
        out_shape=[jax.ShapeDtypeStruct((seq, D_XBC), F32),
                   jax.ShapeDtypeStruct((seq, LANE), BF16),
                   jax.ShapeDtypeStruct((8, LANE), F32)],
        scratch_shapes=[pltpu.VMEM((N_GROUPS, N_STATE, GROUP_W), F32)],
        compiler_params=_cp(("arbitrary",)))(dy, xbc, proj, par, hprev)


def _gate_norm_fwd(y, proj, wn, tm, name):
    seq = y.shape[0]

    def body(y_ref, z_ref, w_ref, o_ref):
        for g in range(N_GROUPS):
            sl = slice(g * GROUP_W, (g + 1) * GROUP_W)
            zv = z_ref[:, sl]
            yz = y_ref[:, sl] * (zv * _sigmoid(zv))
            r = lax.rsqrt(jnp.mean(yz * yz, axis=-1, keepdims=True) + EPS)
            o_ref[:, sl] = (yz * r * w_ref[:, sl]).astype(BF16)

    tok = pl.BlockSpec((tm, D_SSD), lambda i: (i, 0))
    return pl.pallas_call(
        body, name=name, grid=(seq // tm,),
        in_specs=[tok, tok, pl.BlockSpec((1, D_SSD), lambda i: (0, 0))],
        out_specs=tok,
        out_shape=jax.ShapeDtypeStruct((seq, D_SSD), BF16),
        compiler_params=_cp(("parallel",)))(y, proj, wn)


def _gate_norm_bwd(dys, y, proj, wn, tm, name):
    seq = y.shape[0]

    def body(d_ref, y_ref, z_ref, w_ref, dy_ref, dz_ref, st_ref):
        rows = []
        for g in range(N_GROUPS):
            sl = slice(g * GROUP_W, (g + 1) * GROUP_W)
            zv = z_ref[:, sl]
            yv = y_ref[:, sl]
            sg = _sigmoid(zv)
            sz = zv * sg
            yz = yv * sz
            r = lax.rsqrt(jnp.mean(yz * yz, axis=-1, keepdims=True) + EPS)
            yn = yz * r
            dv = d_ref[:, sl]
            dyn = dv * w_ref[:, sl]
            dyz = r * (dyn - yn * jnp.mean(dyn * yn, axis=-1, keepdims=True))
            dy_ref[:, sl] = dyz * sz
            dz_ref[:, sl] = (dyz * yv * _silu_grad(zv, sg)).astype(BF16)
            rows.append(jnp.sum(dv * yn, axis=0, keepdims=True))
        rows = jnp.concatenate([jnp.concatenate(rows, axis=1), jnp.zeros((7, D_SSD), F32)], axis=0)

        @pl.when(pl.program_id(0) == 0)
        def _():
            st_ref[...] = rows

        @pl.when(pl.program_id(0) != 0)
        def _():
            st_ref[...] += rows

    tok = pl.BlockSpec((tm, D_SSD), lambda i: (i, 0))
    return pl.pallas_call(
        body, name=name, grid=(seq // tm,),
        in_specs=[tok, tok, tok, pl.BlockSpec((1, D_SSD), lambda i: (0, 0))],
        out_specs=[tok, tok, pl.BlockSpec((8, D_SSD), lambda i: (0, 0))],
        out_shape=[jax.ShapeDtypeStruct((seq, D_SSD), F32), jax.ShapeDtypeStruct((seq, D_SSD), BF16),
                   jax.ShapeDtypeStruct((8, D_SSD), F32)],
        compiler_params=_cp(("arbitrary",)))(dys, y, proj, wn)


def _pool_counts(t0, rows, w):
    pos = (t0 + 1 + lax.broadcasted_iota(jnp.int32, (rows, 1), 0)).astype(F32)
    return jnp.minimum(pos, float(w))


def _window_means(ext, t0):
    n = ext.shape[0]
    outs = []
    run = ext
    width = 1
    sums = {}
    while width < 16:
        run = run + pltpu.roll(run, width, 0)
        width *= 2
        sums[width] = run
    for g, w in enumerate(POOL_WINDOWS):
        sl = slice(g * POOL_GW, (g + 1) * POOL_GW)
        cnt = _pool_counts(t0, n - HALO, w)
        outs.append(sums[w][HALO:, sl] / cnt - ext[HALO:, sl])
    return outs


def _pool_fwd(proj, pw, pb, ps, tm, name):
    seq = proj.shape[0]

    def body(u_ref, p_ref, pw_ref, pb_ref, ps_ref, o_ref):
        i = pl.program_id(0)
        prev = jnp.where(i == 0, 0.0, p_ref[...])
        ext = jnp.concatenate([prev, u_ref[...]], axis=0)
        diffs = _window_means(ext, i * tm)
        for g in range(4):
            sl = slice(g * POOL_GW, (g + 1) * POOL_GW)
            out = _dot(diffs[g].astype(BF16), pw_ref[g], 1, 0) + pb_ref[:, sl]
            o_ref[:, sl] = (out * ps_ref[:, sl]).astype(BF16)

    c0 = COL_U // 1024
    vec = pl.BlockSpec((1, D_POOL), lambda i: (0, 0))
    return pl.pallas_call(
        body, name=name, grid=(seq // tm,),
        in_specs=[pl.BlockSpec((tm, 1024), lambda i: (i, c0)),
                  pl.BlockSpec((HALO, 1024), lambda i: (jnp.maximum(i * (tm // HALO) - 1, 0), c0)),
                  pl.BlockSpec((4, POOL_GW, POOL_GW), lambda i: (0, 0, 0)), vec, vec],
        out_specs=pl.BlockSpec((tm, D_POOL), lambda i: (i, 0)),
        out_shape=jax.ShapeDtypeStruct((seq, D_POOL), BF16),
        compiler_params=_cp(("parallel",)))(proj, proj, pw, pb, ps)


def _pool_bwd(dyp, proj, pw, pb, ps, tm, name):
    seq = proj.shape[0]
    ni = seq // tm

    def body(d_ref, dn_ref, u_ref, p_ref, pw_ref, pb_ref, ps_ref, du_ref, dw_ref, st_ref):
        i = pl.program_id(0)
        prev = jnp.where(i == 0, 0.0, p_ref[...])
        ext = jnp.concatenate([prev, u_ref[...]], axis=0)
        diffs = _window_means(ext, i * tm)
        dnext = jnp.where(i == ni - 1, 0.0, dn_ref[...])
        dext = jnp.concatenate([d_ref[...], dnext], axis=0)
        n = tm + HALO
        b_rows, s_rows = [], []
        for g, w in enumerate(POOL_WINDOWS):
            sl = slice(g * POOL_GW, (g + 1) * POOL_GW)
            wg = pw_ref[g]
            dout = dext[:, sl] * ps_ref[:, sl]
            dcur = dout[:tm]
            pre = _dot(diffs[g].astype(BF16), wg, 1, 0) + pb_ref[:, sl]
            s_rows.append(jnp.sum(d_ref[:, sl] * pre, axis=0, keepdims=True))
            b_rows.append(jnp.sum(dcur, axis=0, keepdims=True))
            dwg = _dot(diffs[g].astype(BF16), dcur.astype(BF16), 0, 0)

            @pl.when(i == 0)
            def _():
                dw_ref[g] = dwg

            @pl.when(i != 0)
            def _():
                dw_ref[g] += dwg

            ddiff = _dot(dout.astype(BF16), wg, 1, 1)
            scaled = ddiff / _pool_counts(i * tm, n, w)
            run = scaled
            width = 1
            while width < w:
                run = run + pltpu.roll(run, n - width, 0)
                width *= 2
            du_ref[:, sl] = (run[:tm] - ddiff[:tm]).astype(BF16)
        rows = jnp.concatenate([jnp.concatenate(b_rows, axis=1), jnp.concatenate(s_rows, axis=1),
                                jnp.zeros((6, D_POOL), F32)], axis=0)

        @pl.when(i == 0)
        def _():
            st_ref[...] = rows

        @pl.when(i != 0)
        def _():
            st_ref[...] += rows

    c0 = COL_U // 1024
    vec = pl.BlockSpec((1, D_POOL), lambda i: (0, 0))
    last = seq // HALO - 1
    return pl.pallas_call(
        body, name=name, grid=(ni,),
        in_specs=[pl.BlockSpec((tm, D_POOL), lambda i: (i, 0)),
                  pl.BlockSpec((HALO, D_POOL), lambda i: (jnp.minimum((i + 1) * (tm // HALO), last), 0)),
                  pl.BlockSpec((tm, 1024), lambda i: (i, c0)),
                  pl.BlockSpec((HALO, 1024), lambda i: (jnp.maximum(i * (tm // HALO) - 1, 0), c0)),
                  pl.BlockSpec((4, POOL_GW, POOL_GW), lambda i: (0, 0, 0)), vec, vec],
        out_specs=[pl.BlockSpec((tm, D_POOL), lambda i: (i, 0)),
                   pl.BlockSpec((4, POOL_GW, POOL_GW), lambda i: (0, 0, 0)),
                   pl.BlockSpec((8, D_POOL), lambda i: (0, 0))],
        out_shape=[jax.ShapeDtypeStruct((seq, D_POOL), BF16),
                   jax.ShapeDtypeStruct((4, POOL_GW, POOL_GW), F32),
                   jax.ShapeDtypeStruct((8, D_POOL), F32)],
        compiler_params=_cp(("arbitrary",)))(dyp, dyp, proj, proj, pw, pb, ps)


def _mix_out(ys, yp, wout, x1, mod, wn_next, tm, name):
    seq = ys.shape[0]

    def body(ys_ref, yp_ref, w_ref, x_ref, mod_ref, wn_ref, xo_ref, m_ref, h_ref):
        mix = _dot(ys_ref[...], w_ref[0:D_SSD, :], 1, 0) + _dot(yp_ref[...], w_ref[D_SSD:2 * D_SSD, :], 1, 0)
        m_ref[...] = mix.astype(BF16)
        xo = x_ref[...] + mod_ref[5:6, :] * mix
        xo_ref[...] = xo
        h_ref[...] = _modulated(xo, wn_ref[...], mod_ref, 2)

    tok = pl.BlockSpec((tm, D), lambda i: (i, 0))
    return pl.pallas_call(
        body, name=name, grid=(seq // tm,),
        in_specs=[tok, tok, pl.BlockSpec((2 * D_SSD, D), lambda i: (0, 0)), tok,
                  pl.BlockSpec((9, D), lambda i: (0, 0)), pl.BlockSpec((1, D), lambda i: (0, 0))],
        out_specs=[tok, tok, tok],
        out_shape=[jax.ShapeDtypeStruct((seq, D), F32), jax.ShapeDtypeStruct((seq, D), BF16),
                   jax.ShapeDtypeStruct((seq, D), BF16)],
        compiler_params=_cp(("parallel",)))(ys, yp, wout, x1, mod, wn_next)


def _mix_bwd_dh(dz, dxbc, du, ddt, win_t, x1, dx2, mixb, wn, mod, tm, name):
    seq = x1.shape[0]

    def body(dz_ref, dx_ref, du_ref, ddt_ref, w_ref, x_ref, dxo_ref, m_ref, wn_ref, mod_ref, o_ref, st_ref, df_ref):
        dh = (_dot(dz_ref[...], w_ref[COL_Z:COL_Z + 1024, :], 1, 0)
              + _dot(dx_ref[...], w_ref[COL_XBC:COL_XBC + D_XBC, :], 1, 0)
              + _dot(du_ref[...], w_ref[COL_U:COL_U + 1024, :], 1, 0)
              + _dot(ddt_ref[...], w_ref[COL_DT:COL_DT + LANE, :], 1, 0))
        dx = _norm_bwd(dh, x_ref[...], dxo_ref[...], m_ref[...].astype(F32), wn_ref[...],
                       mod_ref[4:5, :], 1.0, st_ref, pl.program_id(0) == 0)
        o_ref[...] = dx
        df_ref[...] = (dx * (FFN_RES * mod_ref[2:3, :])).astype(BF16)

    tok = pl.BlockSpec((tm, D), lambda i: (i, 0))
    return pl.pallas_call(
        body, name=name, grid=(seq // tm,),
        in_specs=[tok, pl.BlockSpec((tm, D_XBC), lambda i: (i, 0)), tok,
                  pl.BlockSpec((tm, LANE), lambda i: (i, 0)),
                  pl.BlockSpec((D_IN_PAD, D), lambda i: (0, 0)),
                  tok, tok, tok,
                  pl.BlockSpec((1, D), lambda i: (0, 0)),
                  pl.BlockSpec((9, D), lambda i: (0, 0))],
        out_specs=[tok, pl.BlockSpec((8, D), lambda i: (0, 0)), tok],
        out_shape=[jax.ShapeDtypeStruct((seq, D), F32), jax.ShapeDtypeStruct((8, D), F32),
                   jax.ShapeDtypeStruct((seq, D), BF16)],
        compiler_params=_cp(("arbitrary",)))(dz, dxbc, du, ddt, win_t, x1, dx2, mixb, wn, mod)


def _mix_bwd_dycat(dmix, wout, tm, name):
    seq = dmix.shape[0]

    def body(d_ref, w_ref, a_ref, b_ref):
        dv = d_ref[...]
        a_ref[...] = _dot(dv, w_ref[0:D_SSD, :], 1, 1)
        b_ref[...] = _dot(dv, w_ref[D_SSD:2 * D_SSD, :], 1, 1)

    tok = pl.BlockSpec((tm, D), lambda i: (i, 0))
    return pl.pallas_call(
        body, name=name, grid=(seq // tm,),
        in_specs=[tok, pl.BlockSpec((2 * D_SSD, D), lambda i: (0, 0))],
        out_specs=[tok, tok],
        out_shape=[jax.ShapeDtypeStruct((seq, D), F32)] * 2,
        compiler_params=_cp(("parallel",)))(dmix, wout)


def _local_step(x, tgt, mod, wff1, later_weights, pool_w, vecs, tm):
    seq = x.shape[0]
    tk = min(seq, 512)
    h1 = _prenorm(x, vecs["ffn1_norm"], mod, 0, tm, "ffn1_prenorm")
    x1, h2, s1 = _ffn_forward(x, h1, wff1, mod, 0, vecs["mix_norm"], 1, tm, "ffn1")
    x1, wff2, win_t, wout = later_weights(x1)
    proj = _mm_nt(h2, win_t, tm, D_IN_PAD // 3, F32, "mix_in_proj")
    xbc = _conv_fwd(proj, vecs["conv_w"], vecs["conv_b"], tm, "mix_conv")
    y, hprev = _ssd_fwd(xbc, proj, vecs["ssd_par"], "mix_ssd")
    ys = _gate_norm_fwd(y, proj, vecs["ssd_norm_w"], tm, "mix_gate_norm")
    yp = _pool_fwd(proj, pool_w, vecs["pool_b"], vecs["pool_scale"], tm, "mix_pool")
    x2, mixb, h3 = _mix_out(ys, yp, wout, x1, mod, vecs["ffn2_norm"], tm, "mix_out_proj")
    x3, _, s3 = _ffn_forward(x2, h3, wff2, mod, 2, vecs["ffn2_norm"], None, tm, "ffn2")
    dx3, df3, st_loss = _loss_head(x3, vecs["final_norm"], tgt, mod, tm, "loss_head")

    dx2, dmix, st3, dw3 = _ffn_backward(dx3, df3, s3, wff2, vecs["ffn2_norm"], mod, 2, (5, 1.0), tm, "ffn2")
    dys, dyp = _mix_bwd_dycat(dmix, wout, tm, "mix_bwd_dycat")
    d_wout = (_mm_tn(ys, dmix, D_SSD, tk, "mix_dw_out_ssd"), _mm_tn(yp, dmix, D_POOL, tk, "mix_dw_out_pool"))
    du, d_pool_w, st_pool = _pool_bwd(dyp, proj, pool_w, vecs["pool_b"], vecs["pool_scale"], tm, "mix_pool_bwd")
    dy, dz, st_gn = _gate_norm_bwd(dys, y, proj, vecs["ssd_norm_w"], tm, "mix_gate_norm_bwd")
    dxbc_act, ddt, st_ssd = _ssd_bwd(dy, xbc, proj, vecs["ssd_par"], hprev, "mix_ssd_bwd")
    dxbc, st_conv = _conv_bwd(dxbc_act, proj, vecs["conv_w"], vecs["conv_b"], tm, "mix_conv_bwd")
    dx1, st2, df1 = _mix_bwd_dh(dz, dxbc, du, ddt, win_t, x1, dx2, mixb, vecs["mix_norm"], mod, min(tm, 256), "mix_bwd_dh")
    d_win = (_mm_tn(dz, h2, 1024, tk, "mix_dw_in_z"), _mm_tn(dxbc, h2, 1024, tk, "mix_dw_in_xbc"),
             _mm_tn(du, h2, 1024, tk, "mix_dw_in_u"), _mm_tn(ddt, h2, LANE, tk, "mix_dw_in_dt"))
    dx0, _, st1, dw1 = _ffn_backward(dx1, df1, s1, wff1, vecs["ffn1_norm"], mod, 0, None, tm, "ffn1")
    stats = dict(ffn1=st1, mix=st2, ffn2=st3, loss=st_loss, pool=st_pool, gn=st_gn, ssd=st_ssd, conv=st_conv)
    return dx0, stats, dw1, dw3, d_win, d_wout, d_pool_w


HBM_SPEC = pl.BlockSpec(memory_space=pltpu.HBM)


def _mesh_pos():
    return lax.axis_index("x"), lax.axis_index("y"), lax.axis_index("c")


def _other_chips(x, y):
    return [(1 - x, y), (x, 1 - y), (1 - x, 1 - y)]


def _all_gather(src, regions, name):
    total, cols = src.shape
    assert sum(r for _, r in regions) == total
    body = _all_gather_body(regions, total, False)
    return pl.pallas_call(
        body, name=name,
        out_shape=jax.ShapeDtypeStruct((N_DEV * total, cols), src.dtype),
        in_specs=[HBM_SPEC], out_specs=HBM_SPEC,
        scratch_shapes=[pltpu.SemaphoreType.DMA((7,)), pltpu.SemaphoreType.DMA((7,)), pltpu.SemaphoreType.DMA],
    )(src)


def _all_gather_async(src, regions, name, collective_id):
    total, cols = src.shape
    assert sum(r for _, r in regions) == total
    return pl.kernel(
        _all_gather_body(regions, total, True), name=name,
        out_type=jax.ShapeDtypeStruct((N_DEV * total, cols), src.dtype),
        mesh=plsc.ScalarSubcoreMesh(axis_name="seq", num_cores=1),
        scratch_types=(pltpu.SemaphoreType.DMA((7,)), pltpu.SemaphoreType.DMA((7,)), pltpu.SemaphoreType.DMA),
        compiler_params=pltpu.CompilerParams(collective_id=collective_id))(src)


def _all_gather_body(regions, total, handshake):
    def body(src_ref, out_ref, send_sems, recv_sems, local_sem):
        x, y, c = _mesh_pos()
        me, sibling = (x, y, c), (x, y, 1 - c)
        chips = _other_chips(x, y)
        if handshake:
            barrier = pltpu.get_barrier_semaphore()
            for peer in [sibling] + [(*chip, c) for chip in chips]:
                pl.semaphore_signal(barrier, inc=1, device_id=peer, device_id_type=MESH)
            pl.semaphore_wait(barrier, 4)

        def rows_of(dev, off, rows):
            start = pl.multiple_of(N_DEV * off + (4 * dev[0] + 2 * dev[1] + dev[2]) * rows, 8)
            return out_ref.at[pl.ds(start, rows), :]

        def copies(k, block, to, from_src):
            out = []
            for off, rows in regions:
                dst = rows_of(block, off, rows)
                out.append(pltpu.make_async_remote_copy(
                    src_ref=src_ref.at[pl.ds(off, rows), :] if from_src else dst, dst_ref=dst,
                    send_sem=send_sems.at[k], recv_sem=recv_sems.at[k], device_id=to, device_id_type=MESH))
            return out

        def drain(k):
            whole = out_ref.at[pl.ds(0, total), :]
            return pltpu.make_async_remote_copy(src_ref=whole, dst_ref=whole, send_sem=send_sems.at[k],
                                                recv_sem=recv_sems.at[k], device_id=me, device_id_type=MESH)

        for off, rows in regions:
            pltpu.make_async_copy(src_ref.at[pl.ds(off, rows), :], rows_of(me, off, rows), local_sem).start()
        first = copies(0, me, sibling, True)
        for j, chip in enumerate(chips):
            first += copies(1 + j, me, (*chip, c), True)
        for cp in first:
            cp.start()
        for j, chip in enumerate(chips):
            drain(1 + j).wait_recv()
            for cp in copies(4 + j, (*chip, c), sibling, False):
                cp.start()
        drain(0).wait_recv()
        for j in range(3):
            drain(4 + j).wait_recv()
        for k in range(7):
            drain(k).wait_send()
        pltpu.make_async_copy(src_ref, out_ref.at[pl.ds(0, total), :], local_sem).wait()

    return body


def _rs_pair(grads, total, name, collective_id):
    cols = grads[0][0].shape[1]
    sent = sum(rows for _, _, rows in grads)
    n = len(grads)

    def body(*refs):
        g_refs, recv_ref, send_sem, recv_sem = refs[:n], refs[n], refs[n + 1], refs[n + 2]
        x, y, c = _mesh_pos()
        sibling = (x, y, 1 - c)
        barrier = pltpu.get_barrier_semaphore()
        pl.semaphore_signal(barrier, inc=1, device_id=sibling, device_id_type=MESH)
        pl.semaphore_wait(barrier, 1)
        for q in range(4):
            for g_ref, (_, off, rows) in zip(g_refs, grads):
                theirs = g_ref.at[pl.ds(pl.multiple_of((2 * q + 1 - c) * rows, 8), rows), :]
                pltpu.make_async_remote_copy(
                    src_ref=theirs, dst_ref=recv_ref.at[q, pl.ds(off, rows), :], send_sem=send_sem, recv_sem=recv_sem,
                    device_id=sibling, device_id_type=MESH).start()
        everything = recv_ref.at[:, pl.ds(0, sent), :]
        whole = pltpu.make_async_remote_copy(src_ref=everything, dst_ref=everything, send_sem=send_sem,
                                             recv_sem=recv_sem, device_id=sibling, device_id_type=MESH)
        whole.wait_send()
        whole.wait_recv()

    return pl.kernel(
        body, name=name, out_type=jax.ShapeDtypeStruct((4, total, cols), F32),
        mesh=plsc.ScalarSubcoreMesh(axis_name="seq", num_cores=1),
        scratch_types=(pltpu.SemaphoreType.DMA, pltpu.SemaphoreType.DMA),
        compiler_params=pltpu.CompilerParams(collective_id=collective_id))(*[g for g, _, _ in grads])


def _pair_sum(g, from_sibling, off, rows, pos, name):
    cols = g.shape[1]

    def body(pos_ref, g_ref, r_ref, o_ref, ob_ref):
        s = g_ref[...] + r_ref[...]
        o_ref[...] = s
        ob_ref[...] = s.astype(BF16)

    out = pl.BlockSpec((None, rows, cols), lambda q, pos_ref: (q, 0, 0))
    return pl.pallas_call(
        body, name=name,
        grid_spec=pltpu.PrefetchScalarGridSpec(
            num_scalar_prefetch=1, grid=(4,),
            in_specs=[pl.BlockSpec((None, None, rows, cols), lambda q, pos_ref: (q, pos_ref[0], 0, 0)),
                      pl.BlockSpec((None, rows, cols), lambda q, pos_ref: (q, off // rows, 0))],
            out_specs=[out, out]),
        out_shape=[jax.ShapeDtypeStruct((4, rows, cols), F32), jax.ShapeDtypeStruct((4, rows, cols), BF16)],
        compiler_params=_cp(("parallel",)))(pos, g.reshape(4, 2, rows, cols), from_sibling)


def _rs_chips(parts, total, name, collective_id):
    cols = parts[0][0].shape[2]
    sent = sum(rows for _, _, rows in parts)
    n = len(parts)

    def body(*refs):
        p_refs, out_ref, send_sems, recv_sems = refs[:n], refs[n], refs[n + 1], refs[n + 2]
        x, y, c = _mesh_pos()
        chips = _other_chips(x, y)
        barrier = pltpu.get_barrier_semaphore()
        for chip in chips:
            pl.semaphore_signal(barrier, inc=1, device_id=(*chip, c), device_id_type=MESH)
        pl.semaphore_wait(barrier, 3)
        for j, chip in enumerate(chips):
            q = 2 * chip[0] + chip[1]
            for p_ref, (_, off, rows) in zip(p_refs, parts):
                pltpu.make_async_remote_copy(
                    src_ref=p_ref.at[q], dst_ref=out_ref.at[j, pl.ds(off, rows), :], send_sem=send_sems.at[j],
                    recv_sem=recv_sems.at[j], device_id=(*chip, c), device_id_type=MESH).start()
        for j, chip in enumerate(chips):
            everything = out_ref.at[j, pl.ds(0, sent), :]
            whole = pltpu.make_async_remote_copy(src_ref=everything, dst_ref=everything, send_sem=send_sems.at[j],
                                                 recv_sem=recv_sems.at[j], device_id=(*chip, c), device_id_type=MESH)
            whole.wait_recv()
            whole.wait_send()

    return pl.kernel(
        body, name=name, out_type=jax.ShapeDtypeStruct((3, total, cols), BF16),
        mesh=plsc.ScalarSubcoreMesh(axis_name="seq", num_cores=1),
        scratch_types=(pltpu.SemaphoreType.DMA((3,)), pltpu.SemaphoreType.DMA((3,))),
        compiler_params=pltpu.CompilerParams(collective_id=collective_id))(*[p for p, _, _ in parts])


def _chip_sum(p, from_chips, off, rows, pos, name):
    cols = p.shape[2]

    def body(pos_ref, p_ref, r_ref, o_ref):
        acc = p_ref[...]
        for j in range(3):
            acc = acc + r_ref[j].astype(F32)
        o_ref[...] = acc

    return pl.pallas_call(
        body, name=name,
        grid_spec=pltpu.PrefetchScalarGridSpec(
            num_scalar_prefetch=1, grid=(1,),
            in_specs=[pl.BlockSpec((None, rows, cols), lambda i, pos_ref: (pos_ref[1], 0, 0)),
                      pl.BlockSpec((3, rows, cols), lambda i, pos_ref: (0, off // rows, 0))],
            out_specs=pl.BlockSpec((rows, cols), lambda i, pos_ref: (0, 0))),
        out_shape=jax.ShapeDtypeStruct((rows, cols), F32),
        compiler_params=_cp(("arbitrary",)))(pos, p, from_chips)


def _row_tile(rows, cap):
    t = min(rows, cap)
    while rows % t or t % 8:
        t -= 8
    return t


def _ada_mod(c_all, w, b, name):
    n = w.shape[1]

    def body(c_ref, w_ref, b_ref, o_ref):
        cv = c_ref[...]
        o_ref[...] = _exact_dot(cv * _sigmoid(cv), w_ref[...]) + b_ref[...]

    return pl.pallas_call(body, name=name, out_shape=jax.ShapeDtypeStruct((N_DEV, n), F32),
                          compiler_params=pltpu.CompilerParams(vmem_limit_bytes=VMEM_LIMIT))(c_all, w, b)


def _ada_grad(c_all, dmod, name):
    n = dmod.shape[1]

    def body(c_ref, d_ref, o_ref):
        cv = c_ref[...]
        o_ref[...] = _dot(cv * _sigmoid(cv), d_ref[...], 0, 0, lax.Precision.HIGHEST)

    return pl.pallas_call(body, name=name, out_shape=jax.ShapeDtypeStruct((D, n), F32),
                          compiler_params=pltpu.CompilerParams(vmem_limit_bytes=VMEM_LIMIT))(c_all, dmod)


def _adamw(w, g, m, v, name):
    rows, cols = w.shape
    tr = _row_tile(rows, 256) if rows % 8 == 0 else rows
    c1 = 1.0 - ADAM_B1 ** ADAM_STEP
    c2 = 1.0 - ADAM_B2 ** ADAM_STEP

    def body(w_ref, g_ref, m_ref, v_ref, d_ref, mo_ref, vo_ref):
        gv = g_ref[...]
        mn = ADAM_B1 * m_ref[...] + (1.0 - ADAM_B1) * gv
        vn = ADAM_B2 * v_ref[...] + (1.0 - ADAM_B2) * (gv * gv)
        mo_ref[...] = mn
        vo_ref[...] = vn
        d_ref[...] = -ADAM_LR * ((mn / c1) / (jnp.sqrt(vn / c2) + ADAM_EPS) + ADAM_WD * w_ref[...])

    spec = pl.BlockSpec((tr, cols), lambda i: (i, 0))
    shape = jax.ShapeDtypeStruct((rows, cols), F32)
    return pl.pallas_call(body, name=name, grid=(rows // tr,), in_specs=[spec] * 4, out_specs=[spec] * 3,
                          out_shape=[shape] * 3, compiler_params=_cp(("parallel",)))(w, g, m, v)


def _sum8_loss(v, loss_row, name):
    rows = v.shape[0] // N_DEV

    def body(v_ref, o_ref, l_ref):
        acc = v_ref[0:rows, :]
        for k in range(1, N_DEV):
            acc = acc + v_ref[k * rows:(k + 1) * rows, :]
        o_ref[...] = acc
        part = jnp.sum(acc[loss_row:loss_row + 8, :], axis=0, keepdims=True)
        l_ref[...] = jnp.broadcast_to(jnp.sum(part, axis=1, keepdims=True), (8, LANE))

    return pl.pallas_call(body, name=name,
                          out_shape=[jax.ShapeDtypeStruct((rows, LANE), F32), jax.ShapeDtypeStruct((8, LANE), F32)],
                          compiler_params=pltpu.CompilerParams(vmem_limit_bytes=VMEM_LIMIT))(v)


WEIGHT_NAMES = ("w_ada", "b_ada", "ffn1_norm", "ffn1_w_gate", "ffn1_w_up", "ffn1_w_down", "mix_norm", "w_in",
                "conv_w", "conv_b", "dt_bias", "a_log", "d_skip", "ssd_norm_w", "pool_w", "pool_b", "pool_scale",
                "w_out", "ffn2_norm", "ffn2_w_gate", "ffn2_w_up", "ffn2_w_down", "final_norm")

FF_SHARD = FF // N_DEV
IN_SHARD = D_IN // N_DEV
IN_SHARD_PAD = 528
OUT_SHARD = 2 * D_SSD // N_DEV
ADA_SHARD = 9 * D // N_DEV
POOL_SHARD_ROWS = 4 * 32 * POOL_GW // D
PACK = dict(gate1=(0, FF_SHARD), up1=(352, FF_SHARD), down1=(704, FF_SHARD), gate2=(1056, FF_SHARD),
            up2=(1408, FF_SHARD), down2=(1760, FF_SHARD), w_out=(2112, OUT_SHARD), w_in=(2368, IN_SHARD_PAD),
            pool_w=(2896, POOL_SHARD_ROWS))
PACK_W_ROWS = 2896
GPACK = dict(w_in=(0, IN_SHARD_PAD), w_out=(768, OUT_SHARD), pool_w=(1024, POOL_SHARD_ROWS),
             gate1=(0, FF_SHARD), up1=(352, FF_SHARD), down1=(704, FF_SHARD),
             gate2=(0, FF_SHARD), up2=(352, FF_SHARD), down2=(704, FF_SHARD))
GROUP_ROWS = 1056

SMALL_ROWS = dict(dmod=(0, 72), ffn1_norm=(72, 8), mix_norm=(80, 8), ffn2_norm=(88, 8), final_norm=(96, 8),
                  ssd_norm_w=(104, 8), pool_scale=(112, 8), conv_b=(120, 16), conv_w=(136, 64), pool_b=(200, 8),
                  ssd=(208, 3), loss=(216, 8))
SMALL_TOTAL = 224


def _rows128(v, rows):
    flat = v.reshape(-1)
    return jnp.pad(flat, (0, rows * LANE - flat.shape[0])).reshape(rows, LANE)


def _pad_lanes(v):
    return jnp.pad(v.reshape(-1), (0, LANE - v.size))


def kernel(x, c, w_ada, b_ada, ffn1_norm, ffn1_w_gate, ffn1_w_up, ffn1_w_down, mix_norm, w_in, conv_w, conv_b, dt_bias, a_log, d_skip, ssd_norm_w, pool_w, pool_b, pool_scale, w_out, ffn2_norm, ffn2_w_gate, ffn2_w_up, ffn2_w_down, final_norm, loss_target, m_w_ada, m_b_ada, m_ffn1_norm, m_ffn1_w_gate, m_ffn1_w_up, m_ffn1_w_down, m_mix_norm, m_w_in, m_conv_w, m_conv_b, m_dt_bias, m_a_log, m_d_skip, m_ssd_norm_w, m_pool_w, m_pool_b, m_pool_scale, m_w_out, m_ffn2_norm, m_ffn2_w_gate, m_ffn2_w_up, m_ffn2_w_down, m_final_norm, v_w_ada, v_b_ada, v_ffn1_norm, v_ffn1_w_gate, v_ffn1_w_up, v_ffn1_w_down, v_mix_norm, v_w_in, v_conv_w, v_conv_b, v_dt_bias, v_a_log, v_d_skip, v_ssd_norm_w, v_pool_w, v_pool_b, v_pool_scale, v_w_out, v_ffn2_norm, v_ffn2_w_gate, v_ffn2_w_up, v_ffn2_w_down, v_final_norm):
    given = dict(locals())
    w = {n: given[n] for n in WEIGHT_NAMES}
    m = {n: given["m_" + n] for n in WEIGHT_NAMES}
    v = {n: given["v_" + n] for n in WEIGHT_NAMES}
    mx, my, mc = _mesh_pos()
    me = 4 * mx + 2 * my + mc

    small = jnp.concatenate([c.reshape(-1), conv_w.reshape(-1), pool_b.reshape(-1), pool_w.reshape(-1)])
    small_rows = 280
    gs = _all_gather(_rows128(small, small_rows), [(0, small_rows)], "ag_small").reshape(N_DEV, small_rows * LANE)
    c_all = gs[:, 0:D]
    conv_w_full = gs[:, 1024:2048].reshape(N_DEV, 4, 256).transpose(1, 0, 2).reshape(4, D_XBC)
    pool_b_full = gs[:, 2048:2176].reshape(N_DEV, 4, 32).transpose(1, 0, 2).reshape(1, D_POOL)
    pool_w_full = gs[:, 2176:2176 + 32768].reshape(N_DEV, 4, 32, POOL_GW).transpose(1, 0, 2, 3).reshape(4, POOL_GW, POOL_GW).astype(BF16)

    b_ada_cols = lax.dynamic_slice(b_ada, (0, me * ADA_SHARD), (1, ADA_SHARD))
    mod_part = _ada_mod(c_all, w_ada[0], b_ada_cols, "ada_mod")
    mod_all = _all_gather(mod_part, [(0, N_DEV)], "ag_mod").reshape(N_DEV, N_DEV, ADA_SHARD)
    mod = lax.dynamic_index_in_dim(mod_all, me, axis=1, keepdims=False).reshape(9, D)

    win_t_shard = jnp.pad(w_in[0].T, ((0, IN_SHARD_PAD - IN_SHARD), (0, 0)))
    packs = (jnp.concatenate([ffn1_w_gate[0].T, ffn1_w_up[0].T], axis=0).astype(BF16),
             ffn1_w_down[0].astype(BF16),
             win_t_shard.astype(BF16),
             w_out[0].astype(BF16),
             jnp.concatenate([ffn2_w_gate[0].T, ffn2_w_up[0].T, ffn2_w_down[0]], axis=0).astype(BF16))
    packs, _ = lax.optimization_barrier((packs, c_all))
    ffn_regions = [(0, FF_SHARD), (FF_SHARD, FF_SHARD), (2 * FF_SHARD, FF_SHARD)]
    full_a = _all_gather_async(packs[0], ffn_regions[0:2], "ag_weights_ffn1_in", 1)
    full_d = _all_gather_async(packs[1], ffn_regions[0:1], "ag_weights_ffn1_out", 2)
    full_in = _all_gather_async(packs[2], [(0, IN_SHARD_PAD)], "ag_weights_in_proj", 9)
    full_out = _all_gather_async(packs[3], [(0, OUT_SHARD)], "ag_weights_out_proj", 10)
    full_2 = _all_gather_async(packs[4], ffn_regions, "ag_weights_ffn2", 11)

    def later_weights(x1):
        (w_i, w_o, w_2), x1 = lax.optimization_barrier(((full_in, full_out, full_2), x1))
        win_g = w_i.reshape(N_DEV, IN_SHARD_PAD, D)[:, :IN_SHARD].reshape(D_IN, D)
        win_t = jnp.concatenate([win_g[0:1024], win_g[1024:3072], win_g[3088:4112], win_g[3072:3088],
                                 jnp.zeros((D_IN_PAD - D_IN, D), BF16)], axis=0)
        return x1, (w_2, w_2, 2), win_t, w_o

    vecs = dict(ffn1_norm=ffn1_norm, mix_norm=mix_norm, ffn2_norm=ffn2_norm, final_norm=final_norm.reshape(1, D),
                conv_w=conv_w_full, conv_b=conv_b, ssd_norm_w=ssd_norm_w, pool_b=pool_b_full, pool_scale=pool_scale,
                ssd_par=jnp.concatenate([_pad_lanes(dt_bias)[None], _pad_lanes(a_log)[None], _pad_lanes(d_skip)[None],
                                         jnp.zeros((5, LANE), F32)], axis=0))
    dx0, st, dw1, dw3, d_win, d_wout, d_pool_w = _local_step(
        x[0], loss_target[0], mod, (full_a, full_d, 0), later_weights, pool_w_full, vecs, min(512, x.shape[1]))

    dwin = jnp.concatenate([d_win[0], d_win[1], d_win[3][0:16], d_win[2]], axis=0)
    dwin = jnp.pad(dwin.reshape(N_DEV, IN_SHARD, D), ((0, 0), (0, IN_SHARD_PAD - IN_SHARD), (0, 0))).reshape(N_DEV * IN_SHARD_PAD, D)
    dwout = jnp.concatenate(d_wout, axis=0)
    dpool = d_pool_w.reshape(4, N_DEV, 32, POOL_GW).transpose(1, 0, 2, 3).reshape(N_DEV * POOL_SHARD_ROWS, D)
    pos = jnp.stack([mc, 2 * mx + my]).astype(jnp.int32)
    by_key = dict(zip(("gate1", "up1", "down1", "gate2", "up2", "down2", "w_out", "w_in", "pool_w"),
                      (*dw1, *dw3, dwout, dwin, dpool)))
    reduced = {}
    for tag, keys, cid in (("ffn2", ("gate2", "up2", "down2"), 3), ("mix", ("w_in", "w_out", "pool_w"), 5),
                           ("ffn1", ("gate1", "up1", "down1"), 7)):
        grads = [(by_key[k], *GPACK[k]) for k in keys]
        from_sibling = _rs_pair(grads, GROUP_ROWS, f"rs_pair_{tag}", cid)
        pairs = {k: _pair_sum(g, from_sibling, off, rows, pos, f"rs_pair_sum_{k}") for (g, off, rows), k in zip(grads, keys)}
        from_chips = _rs_chips([(pairs[k][1], *GPACK[k]) for k in keys], GROUP_ROWS, f"rs_chips_{tag}", cid + 1)
        for k in keys:
            reduced[k] = _chip_sum(pairs[k][0], from_chips, *GPACK[k], pos, f"rs_chip_sum_{k}")

    def shard(k, rows=None):
        return reduced[k] if rows is None else reduced[k][0:rows]

    dmod = jnp.concatenate([st["ffn1"][0:3], st["mix"][0:3], st["ffn2"][0:3]], axis=0)
    sg = jnp.concatenate([
        dmod.reshape(-1), st["ffn1"][3], st["mix"][3], st["ffn2"][3], st["loss"][0], st["gn"][0], st["pool"][1],
        st["conv"][4], st["conv"][0:4].reshape(-1), st["pool"][0], st["ssd"][0:3].reshape(-1),
        jnp.zeros((5 * LANE,), F32), st["loss"][1]])
    sg_all = _all_gather(sg.reshape(SMALL_TOTAL, LANE), [(0, SMALL_TOTAL)], "ag_small_grads")
    tot, loss_b = _sum8_loss(sg_all, SMALL_ROWS["loss"][0], "small_sum")
    loss = loss_b[0, 0]
    dmod_all = sg_all.reshape(N_DEV, SMALL_TOTAL * LANE)[:, 0:9 * D]
    g_w_ada = _ada_grad(c_all, lax.dynamic_slice(dmod_all, (0, me * ADA_SHARD), (N_DEV, ADA_SHARD)), "ada_grad")

    def tot_rows(k):
        off, n = SMALL_ROWS[k]
        return tot[off:off + n].reshape(-1)

    g_conv_w = lax.dynamic_slice(tot_rows("conv_w").reshape(4, D_XBC), (0, me * 256), (4, 256))
    g_pool_b = lax.dynamic_slice(tot_rows("pool_b").reshape(4, POOL_GW), (0, me * 32), (4, 32))
    g_ssd = tot_rows("ssd").reshape(3, LANE)
    grad = {
        "w_ada": g_w_ada[None], "b_ada": tot_rows("dmod").reshape(1, 9 * D),
        "ffn1_norm": tot_rows("ffn1_norm")[None], "mix_norm": tot_rows("mix_norm")[None],
        "ffn2_norm": tot_rows("ffn2_norm")[None], "final_norm": tot_rows("final_norm"),
        "ssd_norm_w": tot_rows("ssd_norm_w")[None], "pool_scale": tot_rows("pool_scale")[None],
        "conv_b": tot_rows("conv_b")[None], "conv_w": g_conv_w[None], "pool_b": g_pool_b[None],
        "dt_bias": g_ssd[0:1, 0:N_HEADS], "a_log": g_ssd[1:2, 0:N_HEADS], "d_skip": g_ssd[2:3, 0:N_HEADS],
        "ffn1_w_gate": shard("gate1").T[None], "ffn1_w_up": shard("up1").T[None], "ffn1_w_down": shard("down1")[None],
        "ffn2_w_gate": shard("gate2").T[None], "ffn2_w_up": shard("up2").T[None], "ffn2_w_down": shard("down2")[None],
        "w_out": shard("w_out")[None], "w_in": shard("w_in", IN_SHARD).T[None],
        "pool_w": shard("pool_w").reshape(1, 4, 32, POOL_GW),
    }

    big = ("w_ada", "ffn1_w_gate", "ffn1_w_up", "ffn1_w_down", "w_in", "pool_w", "w_out",
           "ffn2_w_gate", "ffn2_w_up", "ffn2_w_down")
    delta, new_m, new_v = {}, {}, {}
    transposed = ("ffn1_w_gate", "ffn1_w_up", "w_in", "ffn2_w_gate", "ffn2_w_up")
    for n in big:
        shp = w[n].shape
        two_d = (shp[-3] * shp[-2], shp[-1]) if n == "pool_w" else shp[-2:]
        view = (lambda t: t.reshape(two_d).T) if n in transposed else (lambda t: t.reshape(two_d))
        back = (lambda t: t.T.reshape(shp)) if n in transposed else (lambda t: t.reshape(shp))
        d_, m_, v_ = _adamw(view(w[n]), view(grad[n]), view(m[n]), view(v[n]), f"adamw_{n}")
        delta[n], new_m[n], new_v[n] = back(d_), back(m_), back(v_)
    small_names = [n for n in WEIGHT_NAMES if n not in big]
    sizes = [LANE if w[n].size < LANE else w[n].size for n in small_names]
    small_rows_adam = -(-sum(sizes) // (8 * LANE)) * 8

    def pack_small(t):
        return _rows128(jnp.concatenate([_pad_lanes(t[n]) if t[n].size < LANE else t[n].reshape(-1) for n in small_names]),
                        small_rows_adam)

    d_s, m_s, v_s = _adamw(pack_small(w), pack_small(grad), pack_small(m), pack_small(v), "adamw_small")
    off = 0
    for n, size in zip(small_names, sizes):
        for res, packed in ((delta, d_s), (new_m, m_s), (new_v, v_s)):
            res[n] = packed.reshape(-1)[off:off + w[n].size].reshape(w[n].shape)
        off += size

    return (loss, dx0[None], *[grad[n] for n in WEIGHT_NAMES], *[delta[n] for n in WEIGHT_NAMES],
            *[new_m[n] for n in WEIGHT_NAMES], *[new_v[n] for n in WEIGHT_NAMES])
```

```python
import functools
import math

import jax
import jax.numpy as jnp
from jax import lax
from jax.experimental import pallas as pl
from jax.experimental.pallas import tpu as pltpu
from jax.experimental.pallas import tpu_sc as plsc

F32 = jnp.float32
BF16 = jnp.bfloat16
MESH = pl.DeviceIdType.MESH

N_DEV = 8
D = 1024
FF = 2816
D_SSD = 1024
N_HEADS = 16
HEAD_DIM = 64
N_GROUPS = 4
N_STATE = 128
CHUNK = 128
GROUP_W = D_SSD // N_GROUPS
D_XBC = D_SSD + 2 * N_GROUPS * N_STATE
D_POOL = 1024
POOL_WINDOWS = (2, 4, 8, 16)
POOL_GW = 256
D_IN = 4112
D_IN_PAD = 4224
COL_Z, COL_XBC, COL_U, COL_DT = 0, 1024, 3072, 4096
EPS = 1e-6
FFN_RES = 0.5
LANE = 128
HALO = 16

ADAM_LR, ADAM_B1, ADAM_B2, ADAM_EPS, ADAM_WD, ADAM_STEP = 0.001, 0.9, 0.999, 1e-08, 0.01, 10

VMEM_LIMIT = 56 << 20


def _cp(sem):
    return pltpu.CompilerParams(dimension_semantics=sem, vmem_limit_bytes=VMEM_LIMIT)


def _dot(a, b, ca, cb, prec=None):
    return lax.dot_general(a, b, (((ca,), (cb,)), ((), ())), precision=prec,
                           preferred_element_type=F32)


def _exact_dot(a, b):
    return _dot(a, b, 1, 0, lax.Precision.HIGHEST)


def _sigmoid(v):
    return 1.0 / (1.0 + jnp.exp(-v))


def _silu_grad(v, sg):
    return sg * (1.0 + v * (1.0 - sg))


def _mm_nt(a, bt, tm, tn, out_dtype, name):
    m, k = a.shape
    n = bt.shape[0]

    def body(a_ref, b_ref, o_ref):
        o_ref[...] = _dot(a_ref[...], b_ref[...], 1, 1).astype(out_dtype)

    return pl.pallas_call(
        body, name=name, grid=(n // tn, m // tm),
        in_specs=[pl.BlockSpec((tm, k), lambda j, i: (i, 0)),
                  pl.BlockSpec((tn, k), lambda j, i: (j, 0))],
        out_specs=pl.BlockSpec((tm, tn), lambda j, i: (i, j)),
        out_shape=jax.ShapeDtypeStruct((m, n), out_dtype),
        compiler_params=_cp(("parallel", "parallel")))(a, bt)


def _mm_tn(a, b, tm, tk, name):
    kk, m = a.shape
    n = b.shape[1]
    nk = kk // tk

    def body(a_ref, b_ref, o_ref, acc):
        k = pl.program_id(1)

        @pl.when(k == 0)
        def _():
            acc[...] = jnp.zeros_like(acc)

        acc[...] += _dot(a_ref[...], b_ref[...], 0, 0)

        @pl.when(k == nk - 1)
        def _():
            o_ref[...] = acc[...]

    return pl.pallas_call(
        body, name=name, grid=(m // tm, nk),
        in_specs=[pl.BlockSpec((tk, tm), lambda i, k: (k, i)),
                  pl.BlockSpec((tk, n), lambda i, k: (k, 0))],
        out_specs=pl.BlockSpec((tm, n), lambda i, k: (i, 0)),
        out_shape=jax.ShapeDtypeStruct((m, n), F32),
        scratch_shapes=[pltpu.VMEM((tm, n), F32)],
        compiler_params=_cp(("parallel", "arbitrary")))(a, b)


def _modulated(xv, wn, mod_ref, k):
    r = lax.rsqrt(jnp.mean(xv * xv, axis=-1, keepdims=True) + EPS)
    hn = xv * r * wn
    return (hn * (1.0 + mod_ref[3 * k + 1:3 * k + 2, :]) + mod_ref[3 * k:3 * k + 1, :]).astype(BF16)


def _prenorm(x, wn, mod, k, tm, name):
    seq = x.shape[0]

    def body(x_ref, wn_ref, mod_ref, h_ref):
        h_ref[...] = _modulated(x_ref[...], wn_ref[...], mod_ref, k)

    return pl.pallas_call(
        body, name=name, grid=(seq // tm,),
        in_specs=[pl.BlockSpec((tm, D), lambda i: (i, 0)),
                  pl.BlockSpec((1, D), lambda i: (0, 0)),
                  pl.BlockSpec((9, D), lambda i: (0, 0))],
        out_specs=pl.BlockSpec((tm, D), lambda i: (i, 0)),
        out_shape=jax.ShapeDtypeStruct((seq, D), BF16),
        compiler_params=_cp(("parallel",)))(x, wn, mod)


def _norm_bwd(dh, xv, dxo, branch, wn, sc, res, stats_ref, first):
    r = lax.rsqrt(jnp.mean(xv * xv, axis=-1, keepdims=True) + EPS)
    xn = xv * r
    dhn = dh * (1.0 + sc)
    dxn = dhn * wn
    dx = dxo + r * (dxn - xn * jnp.mean(dxn * xn, axis=-1, keepdims=True))
    rows = jnp.concatenate([
        jnp.sum(dh, axis=0, keepdims=True),
        jnp.sum(dh * (xn * wn), axis=0, keepdims=True),
        jnp.sum(branch * dxo, axis=0, keepdims=True) * res,
        jnp.sum(dhn * xn, axis=0, keepdims=True),
        jnp.zeros((4, D), F32)], axis=0)

    @pl.when(first)
    def _():
        stats_ref[...] = rows

    @pl.when(jnp.logical_not(first))
    def _():
        stats_ref[...] += rows

    return dx


def _loss_head(x3, wf, tgt, mod, tm, name):
    seq = x3.shape[0]

    def body(x_ref, w_ref, t_ref, mod_ref, dx_ref, df_ref, st_ref):
        xv = x_ref[...]
        wv = w_ref[...]
        r = lax.rsqrt(jnp.mean(xv * xv, axis=-1, keepdims=True) + EPS)
        xn = xv * r
        e = xn * wv - t_ref[...]
        dy = e * (1.0 / D)
        dxn = dy * wv
        dx = r * (dxn - xn * jnp.mean(dxn * xn, axis=-1, keepdims=True))
        dx_ref[...] = dx
        df_ref[...] = (dx * (FFN_RES * mod_ref[8:9, :])).astype(BF16)
        rows = jnp.concatenate([
            jnp.sum(dy * xn, axis=0, keepdims=True),
            jnp.sum(e * e, axis=0, keepdims=True) * (0.5 / D),
            jnp.zeros((6, D), F32)], axis=0)

        @pl.when(pl.program_id(0) == 0)
        def _():
            st_ref[...] = rows

        @pl.when(pl.program_id(0) != 0)
        def _():
            st_ref[...] += rows

    return pl.pallas_call(
        body, name=name, grid=(seq // tm,),
        in_specs=[pl.BlockSpec((tm, D), lambda i: (i, 0)),
                  pl.BlockSpec((1, D), lambda i: (0, 0)),
                  pl.BlockSpec((tm, D), lambda i: (i, 0)),
                  pl.BlockSpec((9, D), lambda i: (0, 0))],
        out_specs=[pl.BlockSpec((tm, D), lambda i: (i, 0)),
                   pl.BlockSpec((tm, D), lambda i: (i, 0)),
                   pl.BlockSpec((8, D), lambda i: (0, 0))],
        out_shape=[jax.ShapeDtypeStruct((seq, D), F32), jax.ShapeDtypeStruct((seq, D), BF16),
                   jax.ShapeDtypeStruct((8, D), F32)],
        compiler_params=_cp(("arbitrary",)))(x3, wf, tgt, mod)


def _ffn_up(h, w, blk, tm, tn, name):
    seq = h.shape[0]
    nj = FF // tn

    def body(h_ref, wg_ref, wu_ref, g_ref, u_ref, a_ref):
        hv = h_ref[...]
        g = _dot(hv, wg_ref[...], 1, 1)
        u = _dot(hv, wu_ref[...], 1, 1)
        g_ref[...] = g.astype(BF16)
        u_ref[...] = u.astype(BF16)
        a_ref[...] = (g * _sigmoid(g) * u).astype(BF16)

    act = pl.BlockSpec((tm, tn), lambda j, i: (i, j))
    return pl.pallas_call(
        body, name=name, grid=(nj, seq // tm),
        in_specs=[pl.BlockSpec((tm, D), lambda j, i: (i, 0)),
                  pl.BlockSpec((tn, D), lambda j, i: (blk * nj + j, 0)),
                  pl.BlockSpec((tn, D), lambda j, i: ((blk + 1) * nj + j, 0))],
        out_specs=[act, act, act],
        out_shape=[jax.ShapeDtypeStruct((seq, FF), BF16)] * 3,
        compiler_params=_cp(("parallel", "parallel")))(h, w, w)


def _ffn_down(a, w, blk, x, mod, grow, wn_next, k_next, tm, tk, name):
    seq = a.shape[0]
    nk = FF // tk
    chain = k_next is not None

    def body(a_ref, w_ref, x_ref, mod_ref, wn_ref, xo_ref, f_ref, *rest):
        acc = rest[-1]
        k = pl.program_id(1)

        @pl.when(k == 0)
        def _():
            acc[...] = jnp.zeros_like(acc)

        acc[...] += _dot(a_ref[...], w_ref[...], 1, 0)

        @pl.when(k == nk - 1)
        def _():
            f = acc[...]
            f_ref[...] = f.astype(BF16)
            xo = x_ref[...] + (FFN_RES * mod_ref[grow:grow + 1, :]) * f
            xo_ref[...] = xo
            if chain:
                rest[0][...] = _modulated(xo, wn_ref[...], mod_ref, k_next)

    tok = pl.BlockSpec((tm, D), lambda i, k: (i, 0))
    return pl.pallas_call(
        body, name=name, grid=(seq // tm, nk),
        in_specs=[pl.BlockSpec((tm, tk), lambda i, k: (i, k)),
                  pl.BlockSpec((tk, D), lambda i, k: (blk * nk + k, 0)),
                  tok,
                  pl.BlockSpec((9, D), lambda i, k: (0, 0)),
                  pl.BlockSpec((1, D), lambda i, k: (0, 0))],
        out_specs=[tok, tok] + ([tok] if chain else []),
        out_shape=[jax.ShapeDtypeStruct((seq, D), F32), jax.ShapeDtypeStruct((seq, D), BF16)]
        + ([jax.ShapeDtypeStruct((seq, D), BF16)] if chain else []),
        scratch_shapes=[pltpu.VMEM((tm, D), F32)],
        compiler_params=_cp(("parallel", "arbitrary")))(a, w, x, mod, wn_next)


def _ffn_bwd_da(df, w, blk, g, u, tm, tn, name):
    seq = df.shape[0]
    nj = FF // tn

    def body(df_ref, w_ref, g_ref, u_ref, dg_ref, du_ref):
        da = _dot(df_ref[...], w_ref[...], 1, 1)
        gv = g_ref[...].astype(F32)
        uv = u_ref[...].astype(F32)
        sg = _sigmoid(gv)
        dg_ref[...] = (da * uv * _silu_grad(gv, sg)).astype(BF16)
        du_ref[...] = (da * (gv * sg)).astype(BF16)

    act = pl.BlockSpec((tm, tn), lambda j, i: (i, j))
    return pl.pallas_call(
        body, name=name, grid=(nj, seq // tm),
        in_specs=[pl.BlockSpec((tm, D), lambda j, i: (i, 0)),
                  pl.BlockSpec((tn, D), lambda j, i: (blk * nj + j, 0)),
                  act, act],
        out_specs=[act, act],
        out_shape=[jax.ShapeDtypeStruct((seq, FF), BF16)] * 2,
        compiler_params=_cp(("parallel", "parallel")))(df, w, g, u)


def _ffn_bwd_dh(dg, du, w, blk, x, dxo, fb, wn, mod, k, nxt, tm, tk, name):
    seq = x.shape[0]
    nk = FF // tk

    def body(dg_ref, du_ref, wg_ref, wu_ref, x_ref, dxo_ref, f_ref, wn_ref, mod_ref, dx_ref, st_ref, *rest):
        acc = rest[-1]
        kk = pl.program_id(1)
        first = pl.program_id(0) == 0

        @pl.when(kk == 0)
        def _():
            acc[...] = jnp.zeros_like(acc)

        acc[...] += _dot(dg_ref[...], wg_ref[...], 1, 0) + _dot(du_ref[...], wu_ref[...], 1, 0)

        @pl.when(kk == nk - 1)
        def _():
            dx = _norm_bwd(acc[...], x_ref[...], dxo_ref[...], f_ref[...].astype(F32), wn_ref[...],
                           mod_ref[3 * k + 1:3 * k + 2, :], FFN_RES, st_ref, first)
            dx_ref[...] = dx
            if nxt is not None:
                rest[0][...] = (dx * (nxt[1] * mod_ref[nxt[0]:nxt[0] + 1, :])).astype(BF16)

    tok = pl.BlockSpec((tm, D), lambda i, kk: (i, 0))
    return pl.pallas_call(
        body, name=name, grid=(seq // tm, nk),
        in_specs=[pl.BlockSpec((tm, tk), lambda i, kk: (i, kk)),
                  pl.BlockSpec((tm, tk), lambda i, kk: (i, kk)),
                  pl.BlockSpec((tk, D), lambda i, kk: (blk * nk + kk, 0)),
                  pl.BlockSpec((tk, D), lambda i, kk: ((blk + 1) * nk + kk, 0)),
                  tok, tok, tok,
                  pl.BlockSpec((1, D), lambda i, kk: (0, 0)),
                  pl.BlockSpec((9, D), lambda i, kk: (0, 0))],
        out_specs=[tok, pl.BlockSpec((8, D), lambda i, kk: (0, 0))] + ([tok] if nxt is not None else []),
        out_shape=[jax.ShapeDtypeStruct((seq, D), F32), jax.ShapeDtypeStruct((8, D), F32)]
        + ([jax.ShapeDtypeStruct((seq, D), BF16)] if nxt is not None else []),
        scratch_shapes=[pltpu.VMEM((tm, D), F32)],
        compiler_params=_cp(("arbitrary", "arbitrary")))(dg, du, w, w, x, dxo, fb, wn, mod)


def _ffn_forward(x, h, w, mod, k, wn_next, k_next, tm, tag):
    w_gu, w_d, blk_d = w
    g, u, a = _ffn_up(h, w_gu, 0, tm, FF // 2, f"{tag}_up")
    outs = _ffn_down(a, w_d, blk_d, x, mod, 3 * k + 2, wn_next, k_next, tm, FF // 2, f"{tag}_down")
    return outs[0], (outs[2] if k_next is not None else None), (x, h, g, u, a, outs[1])


def _ffn_backward(dxo, df, saved, w, wn, mod, k, nxt, tm, tag):
    w_gu, w_d, blk_d = w
    x, h, g, u, a, fb = saved
    dg, du = _ffn_bwd_da(df, w_d, blk_d, g, u, tm, FF // 2, f"{tag}_bwd_da")
    seq = x.shape[0]
    tk = min(seq, 512)
    d_gate_t = _mm_tn(dg, h, FF // 2, tk, f"{tag}_dw_gate")
    d_up_t = _mm_tn(du, h, FF // 2, tk, f"{tag}_dw_up")
    d_down = _mm_tn(a, df, FF // 2, tk, f"{tag}_dw_down")
    dws, dg, du = lax.optimization_barrier(((d_gate_t, d_up_t, d_down), dg, du))
    outs = _ffn_bwd_dh(dg, du, w_gu, 0, x, dxo, fb, wn, mod, k, nxt, tm, FF // 2, f"{tag}_bwd_dh")
    return outs[0], (outs[2] if nxt is not None else None), outs[1], dws


def _prev_rows(tm, col):
    return pl.BlockSpec((HALO, 1024), lambda i, j: (jnp.maximum(i * (tm // HALO) - 1, 0), col + j))


def _conv_pre(ext, cw, cb, rows):
    pre = cb + cw[3:4, :] * ext
    for s in (1, 2, 3):
        pre = pre + cw[3 - s:4 - s, :] * pltpu.roll(ext, s, 0)
    return pre[HALO:HALO + rows]


def _conv_fwd(proj, cw, cb, tm, name):
    seq = proj.shape[0]

    def body(x_ref, p_ref, cw_ref, cb_ref, o_ref):
        prev = jnp.where(pl.program_id(0) == 0, 0.0, p_ref[...])
        ext = jnp.concatenate([prev, x_ref[...]], axis=0)
        pre = _conv_pre(ext, cw_ref[...], cb_ref[...], tm)
        o_ref[...] = pre * _sigmoid(pre)

    c0 = COL_XBC // 1024
    return pl.pallas_call(
        body, name=name, grid=(seq // tm, 2),
        in_specs=[pl.BlockSpec((tm, 1024), lambda i, j: (i, c0 + j)),
                  _prev_rows(tm, c0),
                  pl.BlockSpec((4, 1024), lambda i, j: (0, j)),
                  pl.BlockSpec((1, 1024), lambda i, j: (0, j))],
        out_specs=pl.BlockSpec((tm, 1024), lambda i, j: (i, j)),
        out_shape=jax.ShapeDtypeStruct((seq, D_XBC), F32),
        compiler_params=_cp(("parallel", "parallel")))(proj, proj, cw, cb)


def _conv_bwd(dact, proj, cw, cb, tm, name):
    seq = proj.shape[0]
    ni = seq // tm

    def body(d_ref, dn_ref, x_ref, p_ref, n_ref, cw_ref, cb_ref, o_ref, st_ref):
        i = pl.program_id(1)
        cwv = cw_ref[...]
        prev = jnp.where(i == 0, 0.0, p_ref[...])
        ext = jnp.concatenate([prev, x_ref[...], n_ref[...]], axis=0)
        pre = _conv_pre(ext, cwv, cb_ref[...], tm + HALO)
        dnext = jnp.where(i == ni - 1, 0.0, dn_ref[...])
        dext = jnp.concatenate([d_ref[...], dnext], axis=0)
        dpre = dext * _silu_grad(pre, _sigmoid(pre))
        n = tm + HALO
        dx = cwv[3:4, :] * dpre
        for s in (1, 2, 3):
            dx = dx + cwv[3 - s:4 - s, :] * pltpu.roll(dpre, n - s, 0)
        o_ref[...] = dx[:tm].astype(BF16)
        dcur = dpre[:tm]
        rows = [jnp.sum(dcur * pltpu.roll(ext, 3 - k, 0)[HALO:HALO + tm], axis=0, keepdims=True) for k in range(3)]
        rows.append(jnp.sum(dcur * ext[HALO:HALO + tm], axis=0, keepdims=True))
        rows.append(jnp.sum(dcur, axis=0, keepdims=True))
        rows.append(jnp.zeros((3, 1024), F32))
        rows = jnp.concatenate(rows, axis=0)

        @pl.when(i == 0)
        def _():
            st_ref[...] = rows

        @pl.when(i != 0)
        def _():
            st_ref[...] += rows

    c0 = COL_XBC // 1024
    return pl.pallas_call(
        body, name=name, grid=(2, ni),
        in_specs=[pl.BlockSpec((tm, 1024), lambda j, i: (i, j)),
                  pl.BlockSpec((HALO, 1024), lambda j, i: (jnp.minimum((i + 1) * (tm // HALO), seq // HALO - 1), j)),
                  pl.BlockSpec((tm, 1024), lambda j, i: (i, c0 + j)),
                  pl.BlockSpec((HALO, 1024), lambda j, i: (jnp.maximum(i * (tm // HALO) - 1, 0), c0 + j)),
                  pl.BlockSpec((HALO, 1024), lambda j, i: (jnp.minimum((i + 1) * (tm // HALO), seq // HALO - 1), c0 + j)),
                  pl.BlockSpec((4, 1024), lambda j, i: (0, j)),
                  pl.BlockSpec((1, 1024), lambda j, i: (0, j))],
        out_specs=[pl.BlockSpec((tm, 1024), lambda j, i: (i, j)),
                   pl.BlockSpec((8, 1024), lambda j, i: (0, j))],
        out_shape=[jax.ShapeDtypeStruct((seq, D_XBC), BF16), jax.ShapeDtypeStruct((8, D_XBC), F32)],
        compiler_params=_cp(("parallel", "arbitrary")))(dact, dact, proj, proj, proj, cw, cb)


def _head_expand():
    r = lax.broadcasted_iota(jnp.int32, (LANE, D_SSD), 0)
    c = lax.broadcasted_iota(jnp.int32, (LANE, D_SSD), 1)
    return (c // HEAD_DIM == r).astype(F32)


def _head_reduce():
    r = lax.broadcasted_iota(jnp.int32, (D_SSD, LANE), 0)
    c = lax.broadcasted_iota(jnp.int32, (D_SSD, LANE), 1)
    return (r // HEAD_DIM == c).astype(F32)


def _ssd_common(dtr, par):
    q = CHUNK
    v = dtr + par[0:1, :]
    dt = jnp.maximum(v, 0.0) + jnp.log(1.0 + jnp.exp(-jnp.abs(v)))
    a = -jnp.exp(par[1:2, :])
    adt = dt * a
    li = lax.broadcasted_iota(jnp.int32, (q, q), 0)
    si = lax.broadcasted_iota(jnp.int32, (q, q), 1)
    causal = li >= si
    acs = _exact_dot(causal.astype(F32), adt)
    expand = _head_expand()
    dt_l = _exact_dot(dt, expand)
    acs_l = _exact_dot(acs, expand)
    par_l = _exact_dot(par, expand)
    last_l = acs_l[q - 1:q, :]
    return dict(v=v, dt=dt, a=a, acs=acs, acs_t=acs.T, causal=causal, dt_l=dt_l, acs_l=acs_l,
                ea_l=jnp.exp(acs_l), ds_l=jnp.exp(last_l - acs_l), cd_l=jnp.exp(last_l), dskip_l=par_l[2:3, :])


def _decay(cm, h):
    seg = cm["acs"][:, h:h + 1] - cm["acs_t"][h:h + 1, :]
    return jnp.exp(jnp.where(cm["causal"], seg, -jnp.inf))


def _lane_mask(r):
    lane = lax.broadcasted_iota(jnp.int32, (1, GROUP_W), 1)
    return lane // HEAD_DIM == r


def _ssd_fwd(xbc, proj, par, name):
    seq = xbc.shape[0]
    nc = seq // CHUNK
    q = CHUNK

    def body(x_ref, dt_ref, par_ref, y_ref, hp_ref, state):
        @pl.when(pl.program_id(0) == 0)
        def _():
            state[...] = jnp.zeros_like(state)

        cm = _ssd_common(dt_ref[...], par_ref[...])
        for g in range(N_GROUPS):
            lo = g * GROUP_W
            xs = x_ref[:, lo:lo + GROUP_W]
            bm = x_ref[:, D_SSD + g * N_STATE:D_SSD + (g + 1) * N_STATE].astype(BF16)
            cmat = x_ref[:, D_SSD + N_GROUPS * N_STATE + g * N_STATE:D_SSD + N_GROUPS * N_STATE + (g + 1) * N_STATE].astype(BF16)
            xdt = xs * cm["dt_l"][:, lo:lo + GROUP_W]
            xdt_b = xdt.astype(BF16)
            cb = _dot(cmat, bm, 1, 1)
            yd = jnp.zeros((q, GROUP_W), F32)
            for r in range(4):
                s_h = (cb * _decay(cm, 4 * g + r)).astype(BF16)
                yd = jnp.where(_lane_mask(r), _dot(s_h, xdt_b, 1, 0), yd)
            hg = state[g]
            hp_ref[0, g] = hg
            yo = _dot(cmat, hg.astype(BF16), 1, 0) * cm["ea_l"][:, lo:lo + GROUP_W]
            y_ref[:, lo:lo + GROUP_W] = yd + yo + cm["dskip_l"][:, lo:lo + GROUP_W] * xs
            xds = (xdt * cm["ds_l"][:, lo:lo + GROUP_W]).astype(BF16)
            state[g] = hg * cm["cd_l"][:, lo:lo + GROUP_W] + _dot(bm, xds, 0, 0)

    return pl.pallas_call(
        body, name=name, grid=(nc,),
        in_specs=[pl.BlockSpec((q, D_XBC), lambda c: (c, 0)),
                  pl.BlockSpec((q, LANE), lambda c: (c, COL_DT // LANE)),
                  pl.BlockSpec((8, LANE), lambda c: (0, 0))],
        out_specs=[pl.BlockSpec((q, D_SSD), lambda c: (c, 0)),
                   pl.BlockSpec((1, N_GROUPS, N_STATE, GROUP_W), lambda c: (c, 0, 0, 0))],
        out_shape=[jax.ShapeDtypeStruct((seq, D_SSD), F32),
                   jax.ShapeDtypeStruct((nc, N_GROUPS, N_STATE, GROUP_W), F32)],
        scratch_shapes=[pltpu.VMEM((N_GROUPS, N_STATE, GROUP_W), F32)],
        compiler_params=_cp(("arbitrary",)))(xbc, proj, par)


def _ssd_bwd(dy, xbc, proj, par, hprev, name):
    seq = xbc.shape[0]
    nc = seq // CHUNK
    q = CHUNK

    def body(dy_ref, x_ref, dt_ref, par_ref, hp_ref, dx_ref, ddt_ref, st_ref, dstate):
        step = pl.program_id(0)

        @pl.when(step == 0)
        def _():
            dstate[...] = jnp.zeros_like(dstate)

        par = par_ref[...]
        cm = _ssd_common(dt_ref[...], par)
        reduce = _head_reduce()
        lane128 = lax.broadcasted_iota(jnp.int32, (1, LANE), 1)
        row128 = lax.broadcasted_iota(jnp.int32, (LANE, 1), 0)
        d_acs = jnp.zeros((q, LANE), F32)
        d_acs_t = jnp.zeros((LANE, q), F32)
        last_terms = []
        acs_terms = []
        dxdt_all = []
        for g in range(N_GROUPS):
            lo = g * GROUP_W
            sl = slice(lo, lo + GROUP_W)
            xs = x_ref[:, sl]
            bm32 = x_ref[:, D_SSD + g * N_STATE:D_SSD + (g + 1) * N_STATE]
            cm32 = x_ref[:, D_SSD + N_GROUPS * N_STATE + g * N_STATE:D_SSD + N_GROUPS * N_STATE + (g + 1) * N_STATE]
            bm = bm32.astype(BF16)
            cmat = cm32.astype(BF16)
            dyg = dy_ref[:, sl]
            dyg_b = dyg.astype(BF16)
            xdt = xs * cm["dt_l"][:, sl]
            xdt_b = xdt.astype(BF16)
            hg = hp_ref[0, g]
            hg_b = hg.astype(BF16)
            dhg = dstate[g]
            dhg_b = dhg.astype(BF16)
            ea = cm["ea_l"][:, sl]
            ds = cm["ds_l"][:, sl]
            cd = cm["cd_l"][:, sl]
            yoff = _dot(cmat, hg_b, 1, 0) * ea
            dw = (dyg * ea).astype(BF16)
            d_c = _dot(dw, hg_b, 1, 1)
            d_hprev = _dot(cmat, dw, 0, 0) + dhg * cd
            t_acs = dyg * yoff
            d_last_g = jnp.sum(dhg * hg, axis=0, keepdims=True) * cd
            xds_b = (xdt * ds).astype(BF16)
            dxds = _dot(bm, dhg_b, 1, 0)
            d_b = _dot(xds_b, dhg_b, 1, 1)
            dxdt = dxds * ds
            t_ds = dxds * xdt * ds
            t_acs = t_acs - t_ds
            d_last_g = d_last_g + jnp.sum(t_ds, axis=0, keepdims=True)
            cb = _dot(cmat, bm, 1, 1)
            d_cb = jnp.zeros((q, q), F32)
            for r in range(4):
                h = 4 * g + r
                dec = _decay(cm, h)
                s_h = cb * dec
                mask = _lane_mask(r)
                d_s = _dot(jnp.where(mask, dyg, 0.0).astype(BF16), xdt_b, 1, 1)
                dxdt = dxdt + jnp.where(mask, _dot(s_h.astype(BF16), dyg_b, 0, 0), 0.0)
                d_cb = d_cb + d_s * dec
                d_m = d_s * s_h
                d_acs = d_acs + jnp.where(lane128 == h, jnp.sum(d_m, axis=1, keepdims=True), 0.0)
                d_acs_t = d_acs_t + jnp.where(row128 == h, jnp.sum(d_m, axis=0, keepdims=True), 0.0)
            d_cb_b = d_cb.astype(BF16)
            d_c = d_c + _dot(d_cb_b, bm, 1, 0)
            d_b = d_b + _dot(d_cb_b, cmat, 0, 0)
            dstate[g] = d_hprev
            dx_ref[:, sl] = dxdt * cm["dt_l"][:, sl] + cm["dskip_l"][:, sl] * dyg
            dx_ref[:, D_SSD + g * N_STATE:D_SSD + (g + 1) * N_STATE] = d_b
            dx_ref[:, D_SSD + N_GROUPS * N_STATE + g * N_STATE:D_SSD + N_GROUPS * N_STATE + (g + 1) * N_STATE] = d_c
            acs_terms.append(t_acs)
            dxdt_all.append(dxdt * xs)
            last_terms.append(d_last_g)
        t_acs_l = jnp.concatenate(acs_terms, axis=1)
        d_dt_l = jnp.concatenate(dxdt_all, axis=1)
        d_last_l = jnp.concatenate(last_terms, axis=1)
        d_acs = d_acs + _exact_dot(t_acs_l, reduce) - d_acs_t.T
        last_row = lax.broadcasted_iota(jnp.int32, (q, 1), 0) == q - 1
        d_acs = d_acs + jnp.where(last_row, _exact_dot(jnp.broadcast_to(d_last_l, (8, D_SSD)), reduce)[0:1, :], 0.0)
        li = lax.broadcasted_iota(jnp.int32, (q, q), 0)
        si = lax.broadcasted_iota(jnp.int32, (q, q), 1)
        d_adt = _exact_dot((si >= li).astype(F32), d_acs)
        d_dt = _exact_dot(d_dt_l, reduce) + d_adt * cm["a"]
        d_dtr = d_dt * _sigmoid(cm["v"])
        ddt_ref[...] = d_dtr.astype(BF16)
        d_skip = _exact_dot(jnp.broadcast_to(jnp.sum(dy_ref[...] * x_ref[:, 0:D_SSD], axis=0, keepdims=True), (8, D_SSD)), reduce)[0:1, :]
        rows = jnp.concatenate([
            jnp.sum(d_dtr, axis=0, keepdims=True),
            jnp.sum(d_adt * cm["dt"], axis=0, keepdims=True) * cm["a"],
            d_skip,
            jnp.zeros((5, LANE), F32)], axis=0)

        @pl.when(step == 0)
        def _():
            st_ref[...] = rows

        @pl.when(step != 0)
        def _():
            st_ref[...] += rows

    rev = lambda c: nc - 1 - c
    return pl.pallas_call(
        body, name=name, grid=(nc,),
        in_specs=[pl.BlockSpec((q, D_SSD), lambda c: (rev(c), 0)),
                  pl.BlockSpec((q, D_XBC), lambda c: (rev(c), 0)),
                  pl.BlockSpec((q, LANE), lambda c: (rev(c), COL_DT // LANE)),
                  pl.BlockSpec((8, LANE), lambda c: (0, 0)),
                  pl.BlockSpec((1, N_GROUPS, N_STATE, GROUP_W), lambda c: (rev(c), 0, 0, 0))],
        out_specs=[pl.BlockSpec((q, D_XBC), lambda c: (rev(c), 0)),
                   pl.BlockSpec((q, LANE), lambda c: (rev(c), 0)),
                   pl.BlockSpec((8, LANE), lambda c: (0, 0))],
        out_shape=[jax.ShapeDtypeStruct((seq, D_XBC), F32),
                   jax.ShapeDtypeStruct((seq, LANE), BF16),
                   jax.ShapeDtypeStruct((8, LANE), F32)],
        scratch_shapes=[pltpu.VMEM((N_GROUPS, N_STATE, GROUP_W), F32)],
        compiler_params=_cp(("arbitrary",)))(dy, xbc, proj, par, hprev)


def _gate_norm_fwd(y, proj, wn, tm, name):
    seq = y.shape[0]

    def body(y_ref, z_ref, w_ref, o_ref):
        for g in range(N_GROUPS):
            sl = slice(g * GROUP_W, (g + 1) * GROUP_W)
            zv = z_ref[:, sl]
            yz = y_ref[:, sl] * (zv * _sigmoid(zv))
            r = lax.rsqrt(jnp.mean(yz * yz, axis=-1, keepdims=True) + EPS)
            o_ref[:, sl] = (yz * r * w_ref[:, sl]).astype(BF16)

    tok = pl.BlockSpec((tm, D_SSD), lambda i: (i, 0))
    return pl.pallas_call(
        body, name=name, grid=(seq // tm,),
        in_specs=[tok, tok, pl.BlockSpec((1, D_SSD), lambda i: (0, 0))],
        out_specs=tok,
        out_shape=jax.ShapeDtypeStruct((seq, D_SSD), BF16),
        compiler_params=_cp(("parallel",)))(y, proj, wn)


def _gate_norm_bwd(dys, y, proj, wn, tm, name):
    seq = y.shape[0]

    def body(d_ref, y_ref, z_ref, w_ref, dy_ref, dz_ref, st_ref):
        rows = []
        for g in range(N_GROUPS):
            sl = slice(g * GROUP_W, (g + 1) * GROUP_W)
            zv = z_ref[:, sl]
            yv = y_ref[:, sl]
            sg = _sigmoid(zv)
            sz = zv * sg
            yz = yv * sz
            r = lax.rsqrt(jnp.mean(yz * yz, axis=-1, keepdims=True) + EPS)
            yn = yz * r
            dv = d_ref[:, sl]
            dyn = dv * w_ref[:, sl]
            dyz = r * (dyn - yn * jnp.mean(dyn * yn, axis=-1, keepdims=True))
            dy_ref[:, sl] = dyz * sz
            dz_ref[:, sl] = (dyz * yv * _silu_grad(zv, sg)).astype(BF16)
            rows.append(jnp.sum(dv * yn, axis=0, keepdims=True))
        rows = jnp.concatenate([jnp.concatenate(rows, axis=1), jnp.zeros((7, D_SSD), F32)], axis=0)

        @pl.when(pl.program_id(0) == 0)
        def _():
            st_ref[...] = rows

        @pl.when(pl.program_id(0) != 0)
        def _():
            st_ref[...] += rows

    tok = pl.BlockSpec((tm, D_SSD), lambda i: (i, 0))
    return pl.pallas_call(
        body, name=name, grid=(seq // tm,),
        in_specs=[tok, tok, tok, pl.BlockSpec((1, D_SSD), lambda i: (0, 0))],
        out_specs=[tok, tok, pl.BlockSpec((8, D_SSD), lambda i: (0, 0))],
        out_shape=[jax.ShapeDtypeStruct((seq, D_SSD), F32), jax.ShapeDtypeStruct((seq, D_SSD), BF16),
                   jax.ShapeDtypeStruct((8, D_SSD), F32)],
        compiler_params=_cp(("arbitrary",)))(dys, y, proj, wn)


def _pool_counts(t0, rows, w):
    pos = (t0 + 1 + lax.broadcasted_iota(jnp.int32, (rows, 1), 0)).astype(F32)
    return jnp.minimum(pos, float(w))


def _window_means(ext, t0):
    n = ext.shape[0]
    outs = []
    run = ext
    width = 1
    sums = {}
    while width < 16:
        run = run + pltpu.roll(run, width, 0)
        width *= 2
        sums[width] = run
    for g, w in enumerate(POOL_WINDOWS):
        sl = slice(g * POOL_GW, (g + 1) * POOL_GW)
        cnt = _pool_counts(t0, n - HALO, w)
        outs.append(sums[w][HALO:, sl] / cnt - ext[HALO:, sl])
    return outs


def _pool_fwd(proj, pw, pb, ps, tm, name):
    seq = proj.shape[0]

    def body(u_ref, p_ref, pw_ref, pb_ref, ps_ref, o_ref):
        i = pl.program_id(0)
        prev = jnp.where(i == 0, 0.0, p_ref[...])
        ext = jnp.concatenate([prev, u_ref[...]], axis=0)
        diffs = _window_means(ext, i * tm)
        for g in range(4):
            sl = slice(g * POOL_GW, (g + 1) * POOL_GW)
            out = _dot(diffs[g].astype(BF16), pw_ref[g], 1, 0) + pb_ref[:, sl]
            o_ref[:, sl] = (out * ps_ref[:, sl]).astype(BF16)

    c0 = COL_U // 1024
    vec = pl.BlockSpec((1, D_POOL), lambda i: (0, 0))
    return pl.pallas_call(
        body, name=name, grid=(seq // tm,),
        in_specs=[pl.BlockSpec((tm, 1024), lambda i: (i, c0)),
                  pl.BlockSpec((HALO, 1024), lambda i: (jnp.maximum(i * (tm // HALO) - 1, 0), c0)),
                  pl.BlockSpec((4, POOL_GW, POOL_GW), lambda i: (0, 0, 0)), vec, vec],
        out_specs=pl.BlockSpec((tm, D_POOL), lambda i: (i, 0)),
        out_shape=jax.ShapeDtypeStruct((seq, D_POOL), BF16),
        compiler_params=_cp(("parallel",)))(proj, proj, pw, pb, ps)


def _pool_bwd(dyp, proj, pw, pb, ps, tm, name):
    seq = proj.shape[0]
    ni = seq // tm

    def body(d_ref, dn_ref, u_ref, p_ref, pw_ref, pb_ref, ps_ref, du_ref, dw_ref, st_ref):
        i = pl.program_id(0)
        prev = jnp.where(i == 0, 0.0, p_ref[...])
        ext = jnp.concatenate([prev, u_ref[...]], axis=0)
        diffs = _window_means(ext, i * tm)
        dnext = jnp.where(i == ni - 1, 0.0, dn_ref[...])
        dext = jnp.concatenate([d_ref[...], dnext], axis=0)
        n = tm + HALO
        b_rows, s_rows = [], []
        for g, w in enumerate(POOL_WINDOWS):
            sl = slice(g * POOL_GW, (g + 1) * POOL_GW)
            wg = pw_ref[g]
            dout = dext[:, sl] * ps_ref[:, sl]
            dcur = dout[:tm]
            pre = _dot(diffs[g].astype(BF16), wg, 1, 0) + pb_ref[:, sl]
            s_rows.append(jnp.sum(d_ref[:, sl] * pre, axis=0, keepdims=True))
            b_rows.append(jnp.sum(dcur, axis=0, keepdims=True))
            dwg = _dot(diffs[g].astype(BF16), dcur.astype(BF16), 0, 0)

            @pl.when(i == 0)
            def _():
                dw_ref[g] = dwg

            @pl.when(i != 0)
            def _():
                dw_ref[g] += dwg

            ddiff = _dot(dout.astype(BF16), wg, 1, 1)
            scaled = ddiff / _pool_counts(i * tm, n, w)
            run = scaled
            width = 1
            while width < w:
                run = run + pltpu.roll(run, n - width, 0)
                width *= 2
            du_ref[:, sl] = (run[:tm] - ddiff[:tm]).astype(BF16)
        rows = jnp.concatenate([jnp.concatenate(b_rows, axis=1), jnp.concatenate(s_rows, axis=1),
                                jnp.zeros((6, D_POOL), F32)], axis=0)

        @pl.when(i == 0)
        def _():
            st_ref[...] = rows

        @pl.when(i != 0)
        def _():
            st_ref[...] += rows

    c0 = COL_U // 1024
    vec = pl.BlockSpec((1, D_POOL), lambda i: (0, 0))
    last = seq // HALO - 1
    return pl.pallas_call(
        body, name=name, grid=(ni,),
        in_specs=[pl.BlockSpec((tm, D_POOL), lambda i: (i, 0)),
                  pl.BlockSpec((HALO, D_POOL), lambda i: (jnp.minimum((i + 1) * (tm // HALO), last), 0)),
                  pl.BlockSpec((tm, 1024), lambda i: (i, c0)),
                  pl.BlockSpec((HALO, 1024), lambda i: (jnp.maximum(i * (tm // HALO) - 1, 0), c0)),
                  pl.BlockSpec((4, POOL_GW, POOL_GW), lambda i: (0, 0, 0)), vec, vec],
        out_specs=[pl.BlockSpec((tm, D_POOL), lambda i: (i, 0)),
                   pl.BlockSpec((4, POOL_GW, POOL_GW), lambda i: (0, 0, 0)),
                   pl.BlockSpec((8, D_POOL), lambda i: (0, 0))],
        out_shape=[jax.ShapeDtypeStruct((seq, D_POOL), BF16),
                   jax.ShapeDtypeStruct((4, POOL_GW, POOL_GW), F32),
                   jax.ShapeDtypeStruct((8, D_POOL), F32)],
        compiler_params=_cp(("arbitrary",)))(dyp, dyp, proj, proj, pw, pb, ps)


def _mix_out(ys, yp, wout, x1, mod, wn_next, tm, name):
    seq = ys.shape[0]

    def body(ys_ref, yp_ref, w_ref, x_ref, mod_ref, wn_ref, xo_ref, m_ref, h_ref):
        mix = _dot(ys_ref[...], w_ref[0:D_SSD, :], 1, 0) + _dot(yp_ref[...], w_ref[D_SSD:2 * D_SSD, :], 1, 0)
        m_ref[...] = mix.astype(BF16)
        xo = x_ref[...] + mod_ref[5:6, :] * mix
        xo_ref[...] = xo
        h_ref[...] = _modulated(xo, wn_ref[...], mod_ref, 2)

    tok = pl.BlockSpec((tm, D), lambda i: (i, 0))
    return pl.pallas_call(
        body, name=name, grid=(seq // tm,),
        in_specs=[tok, tok, pl.BlockSpec((2 * D_SSD, D), lambda i: (0, 0)), tok,
                  pl.BlockSpec((9, D), lambda i: (0, 0)), pl.BlockSpec((1, D), lambda i: (0, 0))],
        out_specs=[tok, tok, tok],
        out_shape=[jax.ShapeDtypeStruct((seq, D), F32), jax.ShapeDtypeStruct((seq, D), BF16),
                   jax.ShapeDtypeStruct((seq, D), BF16)],
        compiler_params=_cp(("parallel",)))(ys, yp, wout, x1, mod, wn_next)


def _mix_bwd_dh(dz, dxbc, du, ddt, win_t, x1, dx2, mixb, wn, mod, tm, name):
    seq = x1.shape[0]

    def body(dz_ref, dx_ref, du_ref, ddt_ref, w_ref, x_ref, dxo_ref, m_ref, wn_ref, mod_ref, o_ref, st_ref, df_ref):
        dh = (_dot(dz_ref[...], w_ref[COL_Z:COL_Z + 1024, :], 1, 0)
              + _dot(dx_ref[...], w_ref[COL_XBC:COL_XBC + D_XBC, :], 1, 0)
              + _dot(du_ref[...], w_ref[COL_U:COL_U + 1024, :], 1, 0)
              + _dot(ddt_ref[...], w_ref[COL_DT:COL_DT + LANE, :], 1, 0))
        dx = _norm_bwd(dh, x_ref[...], dxo_ref[...], m_ref[...].astype(F32), wn_ref[...],
                       mod_ref[4:5, :], 1.0, st_ref, pl.program_id(0) == 0)
        o_ref[...] = dx
        df_ref[...] = (dx * (FFN_RES * mod_ref[2:3, :])).astype(BF16)

    tok = pl.BlockSpec((tm, D), lambda i: (i, 0))
    return pl.pallas_call(
        body, name=name, grid=(seq // tm,),
        in_specs=[tok, pl.BlockSpec((tm, D_XBC), lambda i: (i, 0)), tok,
                  pl.BlockSpec((tm, LANE), lambda i: (i, 0)),
                  pl.BlockSpec((D_IN_PAD, D), lambda i: (0, 0)),
                  tok, tok, tok,
                  pl.BlockSpec((1, D), lambda i: (0, 0)),
                  pl.BlockSpec((9, D), lambda i: (0, 0))],
        out_specs=[tok, pl.BlockSpec((8, D), lambda i: (0, 0)), tok],
        out_shape=[jax.ShapeDtypeStruct((seq, D), F32), jax.ShapeDtypeStruct((8, D), F32),
                   jax.ShapeDtypeStruct((seq, D), BF16)],
        compiler_params=_cp(("arbitrary",)))(dz, dxbc, du, ddt, win_t, x1, dx2, mixb, wn, mod)


def _mix_bwd_dycat(dmix, wout, tm, name):
    seq = dmix.shape[0]

    def body(d_ref, w_ref, a_ref, b_ref):
        dv = d_ref[...]
        a_ref[...] = _dot(dv, w_ref[0:D_SSD, :], 1, 1)
        b_ref[...] = _dot(dv, w_ref[D_SSD:2 * D_SSD, :], 1, 1)

    tok = pl.BlockSpec((tm, D), lambda i: (i, 0))
    return pl.pallas_call(
        body, name=name, grid=(seq // tm,),
        in_specs=[tok, pl.BlockSpec((2 * D_SSD, D), lambda i: (0, 0))],
        out_specs=[tok, tok],
        out_shape=[jax.ShapeDtypeStruct((seq, D), F32)] * 2,
        compiler_params=_cp(("parallel",)))(dmix, wout)


def _local_step(x, tgt, mod, wff1, later_weights, pool_w, vecs, tm):
    seq = x.shape[0]
    tk = min(seq, 512)
    in_proj_weights, out_proj_weights, ffn2_weights = later_weights
    h1 = _prenorm(x, vecs["ffn1_norm"], mod, 0, tm, "ffn1_prenorm")
    x1, h2, s1 = _ffn_forward(x, h1, wff1, mod, 0, vecs["mix_norm"], 1, tm, "ffn1")
    x1, win_t = in_proj_weights(x1)
    proj = _mm_nt(h2, win_t, tm, D_IN_PAD // 3, F32, "mix_in_proj")
    xbc = _conv_fwd(proj, vecs["conv_w"], vecs["conv_b"], tm, "mix_conv")
    y, hprev = _ssd_fwd(xbc, proj, vecs["ssd_par"], "mix_ssd")
    ys = _gate_norm_fwd(y, proj, vecs["ssd_norm_w"], tm, "mix_gate_norm")
    yp = _pool_fwd(proj, pool_w, vecs["pool_b"], vecs["pool_scale"], tm, "mix_pool")
    ys, wout = out_proj_weights(ys)
    x2, mixb, h3 = _mix_out(ys, yp, wout, x1, mod, vecs["ffn2_norm"], tm, "mix_out_proj")
    x2, wff2 = ffn2_weights(x2)
    x3, _, s3 = _ffn_forward(x2, h3, wff2, mod, 2, vecs["ffn2_norm"], None, tm, "ffn2")
    dx3, df3, st_loss = _loss_head(x3, vecs["final_norm"], tgt, mod, tm, "loss_head")

    dx2, dmix, st3, dw3 = _ffn_backward(dx3, df3, s3, wff2, vecs["ffn2_norm"], mod, 2, (5, 1.0), tm, "ffn2")
    dys, dyp = _mix_bwd_dycat(dmix, wout, tm, "mix_bwd_dycat")
    d_wout = (_mm_tn(ys, dmix, D_SSD, tk, "mix_dw_out_ssd"), _mm_tn(yp, dmix, D_POOL, tk, "mix_dw_out_pool"))
    du, d_pool_w, st_pool = _pool_bwd(dyp, proj, pool_w, vecs["pool_b"], vecs["pool_scale"], tm, "mix_pool_bwd")
    dy, dz, st_gn = _gate_norm_bwd(dys, y, proj, vecs["ssd_norm_w"], tm, "mix_gate_norm_bwd")
    dxbc_act, ddt, st_ssd = _ssd_bwd(dy, xbc, proj, vecs["ssd_par"], hprev, "mix_ssd_bwd")
    dxbc, st_conv = _conv_bwd(dxbc_act, proj, vecs["conv_w"], vecs["conv_b"], tm, "mix_conv_bwd")
    dx1, st2, df1 = _mix_bwd_dh(dz, dxbc, du, ddt, win_t, x1, dx2, mixb, vecs["mix_norm"], mod, min(tm, 256), "mix_bwd_dh")
    d_win = (_mm_tn(dz, h2, 1024, tk, "mix_dw_in_z"), _mm_tn(dxbc, h2, 1024, tk, "mix_dw_in_xbc"),
             _mm_tn(du, h2, 1024, tk, "mix_dw_in_u"), _mm_tn(ddt, h2, LANE, tk, "mix_dw_in_dt"))
    dx0, _, st1, dw1 = _ffn_backward(dx1, df1, s1, wff1, vecs["ffn1_norm"], mod, 0, None, tm, "ffn1")
    stats = dict(ffn1=st1, mix=st2, ffn2=st3, loss=st_loss, pool=st_pool, gn=st_gn, ssd=st_ssd, conv=st_conv)
    return dx0, stats, dw1, dw3, d_win, d_wout, d_pool_w


HBM_SPEC = pl.BlockSpec(memory_space=pltpu.HBM)


def _mesh_pos():
    return lax.axis_index("x"), lax.axis_index("y"), lax.axis_index("c")


def _other_chips(x, y):
    return [(1 - x, y), (x, 1 - y), (1 - x, 1 - y)]


def _all_gather(src, regions, name):
    total, cols = src.shape
    assert sum(r for _, r in regions) == total
    body = _all_gather_body(regions, total, False)
    return pl.pallas_call(
        body, name=name,
        out_shape=jax.ShapeDtypeStruct((N_DEV * total, cols), src.dtype),
        in_specs=[HBM_SPEC], out_specs=HBM_SPEC,
        scratch_shapes=[pltpu.SemaphoreType.DMA((7,)), pltpu.SemaphoreType.DMA((7,)), pltpu.SemaphoreType.DMA],
    )(src)


def _all_gather_async(src, regions, name, collective_id):
    total, cols = src.shape
    assert sum(r for _, r in regions) == total
    return pl.kernel(
        _all_gather_body(regions, total, True), name=name,
        out_type=jax.ShapeDtypeStruct((N_DEV * total, cols), src.dtype),
        mesh=plsc.ScalarSubcoreMesh(axis_name="seq", num_cores=1),
        scratch_types=(pltpu.SemaphoreType.DMA((7,)), pltpu.SemaphoreType.DMA((7,)), pltpu.SemaphoreType.DMA),
        compiler_params=pltpu.CompilerParams(collective_id=collective_id))(src)


def _all_gather_body(regions, total, handshake):
    def body(src_ref, out_ref, send_sems, recv_sems, local_sem):
        x, y, c = _mesh_pos()
        me, sibling = (x, y, c), (x, y, 1 - c)
        chips = _other_chips(x, y)
        if handshake:
            barrier = pltpu.get_barrier_semaphore()
            for peer in [sibling] + [(*chip, c) for chip in chips]:
                pl.semaphore_signal(barrier, inc=1, device_id=peer, device_id_type=MESH)
            pl.semaphore_wait(barrier, 4)

        def rows_of(dev, off, rows):
            start = pl.multiple_of(N_DEV * off + (4 * dev[0] + 2 * dev[1] + dev[2]) * rows, 8)
            return out_ref.at[pl.ds(start, rows), :]

        def copies(k, block, to, from_src):
            out = []
            for off, rows in regions:
                dst = rows_of(block, off, rows)
                out.append(pltpu.make_async_remote_copy(
                    src_ref=src_ref.at[pl.ds(off, rows), :] if from_src else dst, dst_ref=dst,
                    send_sem=send_sems.at[k], recv_sem=recv_sems.at[k], device_id=to, device_id_type=MESH))
            return out

        def drain(k):
            whole = out_ref.at[pl.ds(0, total), :]
            return pltpu.make_async_remote_copy(src_ref=whole, dst_ref=whole, send_sem=send_sems.at[k],
                                                recv_sem=recv_sems.at[k], device_id=me, device_id_type=MESH)

        for off, rows in regions:
            pltpu.make_async_copy(src_ref.at[pl.ds(off, rows), :], rows_of(me, off, rows), local_sem).start()
        first = copies(0, me, sibling, True)
        for j, chip in enumerate(chips):
            first += copies(1 + j, me, (*chip, c), True)
        for cp in first:
            cp.start()
        for j, chip in enumerate(chips):
            drain(1 + j).wait_recv()
            for cp in copies(4 + j, (*chip, c), sibling, False):
                cp.start()
        drain(0).wait_recv()
        for j in range(3):
            drain(4 + j).wait_recv()
        for k in range(7):
            drain(k).wait_send()
        pltpu.make_async_copy(src_ref, out_ref.at[pl.ds(0, total), :], local_sem).wait()

    return body


def _rs_pair(grads, total, name, collective_id):
    cols = grads[0][0].shape[1]
    sent = sum(rows for _, _, rows in grads)
    n = len(grads)

    def body(*refs):
        g_refs, recv_ref, send_sem, recv_sem = refs[:n], refs[n], refs[n + 1], refs[n + 2]
        x, y, c = _mesh_pos()
        sibling = (x, y, 1 - c)
        barrier = pltpu.get_barrier_semaphore()
        pl.semaphore_signal(barrier, inc=1, device_id=sibling, device_id_type=MESH)
        pl.semaphore_wait(barrier, 1)
        for q in range(4):
            for g_ref, (_, off, rows) in zip(g_refs, grads):
                theirs = g_ref.at[pl.ds(pl.multiple_of((2 * q + 1 - c) * rows, 8), rows), :]
                pltpu.make_async_remote_copy(
                    src_ref=theirs, dst_ref=recv_ref.at[q, pl.ds(off, rows), :], send_sem=send_sem, recv_sem=recv_sem,
                    device_id=sibling, device_id_type=MESH).start()
        everything = recv_ref.at[:, pl.ds(0, sent), :]
        whole = pltpu.make_async_remote_copy(src_ref=everything, dst_ref=everything, send_sem=send_sem,
                                             recv_sem=recv_sem, device_id=sibling, device_id_type=MESH)
        whole.wait_send()
        whole.wait_recv()

    return pl.kernel(
        body, name=name, out_type=jax.ShapeDtypeStruct((4, total, cols), F32),
        mesh=plsc.ScalarSubcoreMesh(axis_name="seq", num_cores=1),
        scratch_types=(pltpu.SemaphoreType.DMA, pltpu.SemaphoreType.DMA),
        compiler_params=pltpu.CompilerParams(collective_id=collective_id))(*[g for g, _, _ in grads])


def _pair_sum(g, from_sibling, off, rows, pos, name):
    cols = g.shape[1]

    def body(pos_ref, g_ref, r_ref, o_ref, ob_ref):
        s = g_ref[...] + r_ref[...]
        o_ref[...] = s
        ob_ref[...] = s.astype(BF16)

    out = pl.BlockSpec((None, rows, cols), lambda q, pos_ref: (q, 0, 0))
    return pl.pallas_call(
        body, name=name,
        grid_spec=pltpu.PrefetchScalarGridSpec(
            num_scalar_prefetch=1, grid=(4,),
            in_specs=[pl.BlockSpec((None, None, rows, cols), lambda q, pos_ref: (q, pos_ref[0], 0, 0)),
                      pl.BlockSpec((None, rows, cols), lambda q, pos_ref: (q, off // rows, 0))],
            out_specs=[out, out]),
        out_shape=[jax.ShapeDtypeStruct((4, rows, cols), F32), jax.ShapeDtypeStruct((4, rows, cols), BF16)],
        compiler_params=_cp(("parallel",)))(pos, g.reshape(4, 2, rows, cols), from_sibling)


def _rs_chips(parts, total, name, collective_id):
    cols = parts[0][0].shape[2]
    sent = sum(rows for _, _, rows in parts)
    n = len(parts)

    def body(*refs):
        p_refs, out_ref, send_sems, recv_sems = refs[:n], refs[n], refs[n + 1], refs[n + 2]
        x, y, c = _mesh_pos()
        chips = _other_chips(x, y)
        barrier = pltpu.get_barrier_semaphore()
        for chip in chips:
            pl.semaphore_signal(barrier, inc=1, device_id=(*chip, c), device_id_type=MESH)
        pl.semaphore_wait(barrier, 3)
        for j, chip in enumerate(chips):
            q = 2 * chip[0] + chip[1]
            for p_ref, (_, off, rows) in zip(p_refs, parts):
                pltpu.make_async_remote_copy(
                    src_ref=p_ref.at[q], dst_ref=out_ref.at[j, pl.ds(off, rows), :], send_sem=send_sems.at[j],
                    recv_sem=recv_sems.at[j], device_id=(*chip, c), device_id_type=MESH).start()
        for j, chip in enumerate(chips):
            everything = out_ref.at[j, pl.ds(0, sent), :]
            whole = pltpu.make_async_remote_copy(src_ref=everything, dst_ref=everything, send_sem=send_sems.at[j],
                                                 recv_sem=recv_sems.at[j], device_id=(*chip, c), device_id_type=MESH)
            whole.wait_recv()
            whole.wait_send()

    return pl.kernel(
        body, name=name, out_type=jax.ShapeDtypeStruct((3, total, cols), BF16),
        mesh=plsc.ScalarSubcoreMesh(axis_name="seq", num_cores=1),
        scratch_types=(pltpu.SemaphoreType.DMA((3,)), pltpu.SemaphoreType.DMA((3,))),
        compiler_params=pltpu.CompilerParams(collective_id=collective_id))(*[p for p, _, _ in parts])


def _chip_sum(p, from_chips, off, rows, pos, name):
    cols = p.shape[2]

    def body(pos_ref, p_ref, r_ref, o_ref):
        acc = p_ref[...]
        for j in range(3):
            acc = acc + r_ref[j].astype(F32)
        o_ref[...] = acc

    return pl.pallas_call(
        body, name=name,
        grid_spec=pltpu.PrefetchScalarGridSpec(
            num_scalar_prefetch=1, grid=(1,),
            in_specs=[pl.BlockSpec((None, rows, cols), lambda i, pos_ref: (pos_ref[1], 0, 0)),
                      pl.BlockSpec((3, rows, cols), lambda i, pos_ref: (0, off // rows, 0))],
            out_specs=pl.BlockSpec((rows, cols), lambda i, pos_ref: (0, 0))),
        out_shape=jax.ShapeDtypeStruct((rows, cols), F32),
        compiler_params=_cp(("arbitrary",)))(pos, p, from_chips)


def _row_tile(rows, cap):
    t = min(rows, cap)
    while rows % t or t % 8:
        t -= 8
    return t


def _ada_mod(c_all, w, b, name):
    n = w.shape[1]

    def body(c_ref, w_ref, b_ref, o_ref):
        cv = c_ref[...]
        o_ref[...] = _exact_dot(cv * _sigmoid(cv), w_ref[...]) + b_ref[...]

    return pl.pallas_call(body, name=name, out_shape=jax.ShapeDtypeStruct((N_DEV, n), F32),
                          compiler_params=pltpu.CompilerParams(vmem_limit_bytes=VMEM_LIMIT))(c_all, w, b)


def _ada_grad(c_all, dmod, name):
    n = dmod.shape[1]

    def body(c_ref, d_ref, o_ref):
        cv = c_ref[...]
        o_ref[...] = _dot(cv * _sigmoid(cv), d_ref[...], 0, 0, lax.Precision.HIGHEST)

    return pl.pallas_call(body, name=name, out_shape=jax.ShapeDtypeStruct((D, n), F32),
                          compiler_params=pltpu.CompilerParams(vmem_limit_bytes=VMEM_LIMIT))(c_all, dmod)


def _adamw(w, g, m, v, name):
    rows, cols = w.shape
    tr = _row_tile(rows, 256) if rows % 8 == 0 else rows
    c1 = 1.0 - ADAM_B1 ** ADAM_STEP
    c2 = 1.0 - ADAM_B2 ** ADAM_STEP

    def body(w_ref, g_ref, m_ref, v_ref, d_ref, mo_ref, vo_ref):
        gv = g_ref[...]
        mn = ADAM_B1 * m_ref[...] + (1.0 - ADAM_B1) * gv
        vn = ADAM_B2 * v_ref[...] + (1.0 - ADAM_B2) * (gv * gv)
        mo_ref[...] = mn
        vo_ref[...] = vn
        d_ref[...] = -ADAM_LR * ((mn / c1) / (jnp.sqrt(vn / c2) + ADAM_EPS) + ADAM_WD * w_ref[...])

    spec = pl.BlockSpec((tr, cols), lambda i: (i, 0))
    shape = jax.ShapeDtypeStruct((rows, cols), F32)
    return pl.pallas_call(body, name=name, grid=(rows // tr,), in_specs=[spec] * 4, out_specs=[spec] * 3,
                          out_shape=[shape] * 3, compiler_params=_cp(("parallel",)))(w, g, m, v)


def _sum8_loss(v, loss_row, name):
    rows = v.shape[0] // N_DEV

    def body(v_ref, o_ref, l_ref):
        acc = v_ref[0:rows, :]
        for k in range(1, N_DEV):
            acc = acc + v_ref[k * rows:(k + 1) * rows, :]
        o_ref[...] = acc
        part = jnp.sum(acc[loss_row:loss_row + 8, :], axis=0, keepdims=True)
        l_ref[...] = jnp.broadcast_to(jnp.sum(part, axis=1, keepdims=True), (8, LANE))

    return pl.pallas_call(body, name=name,
                          out_shape=[jax.ShapeDtypeStruct((rows, LANE), F32), jax.ShapeDtypeStruct((8, LANE), F32)],
                          compiler_params=pltpu.CompilerParams(vmem_limit_bytes=VMEM_LIMIT))(v)


WEIGHT_NAMES = ("w_ada", "b_ada", "ffn1_norm", "ffn1_w_gate", "ffn1_w_up", "ffn1_w_down", "mix_norm", "w_in",
                "conv_w", "conv_b", "dt_bias", "a_log", "d_skip", "ssd_norm_w", "pool_w", "pool_b", "pool_scale",
                "w_out", "ffn2_norm", "ffn2_w_gate", "ffn2_w_up", "ffn2_w_down", "final_norm")

FF_SHARD = FF // N_DEV
IN_SHARD = D_IN // N_DEV
IN_SHARD_PAD = 528
OUT_SHARD = 2 * D_SSD // N_DEV
ADA_SHARD = 9 * D // N_DEV
POOL_SHARD_ROWS = 4 * 32 * POOL_GW // D
PACK = dict(gate1=(0, FF_SHARD), up1=(352, FF_SHARD), down1=(704, FF_SHARD), gate2=(1056, FF_SHARD),
            up2=(1408, FF_SHARD), down2=(1760, FF_SHARD), w_out=(2112, OUT_SHARD), w_in=(2368, IN_SHARD_PAD),
            pool_w=(2896, POOL_SHARD_ROWS))
PACK_W_ROWS = 2896
GPACK = dict(w_in=(0, IN_SHARD_PAD), w_out=(768, OUT_SHARD), pool_w=(1024, POOL_SHARD_ROWS),
             gate1=(0, FF_SHARD), up1=(352, FF_SHARD), down1=(704, FF_SHARD),
             gate2=(0, FF_SHARD), up2=(352, FF_SHARD), down2=(704, FF_SHARD))
GROUP_ROWS = 1056

SMALL_ROWS = dict(dmod=(0, 72), ffn1_norm=(72, 8), mix_norm=(80, 8), ffn2_norm=(88, 8), final_norm=(96, 8),
                  ssd_norm_w=(104, 8), pool_scale=(112, 8), conv_b=(120, 16), conv_w=(136, 64), pool_b=(200, 8),
                  ssd=(208, 3), loss=(216, 8))
SMALL_TOTAL = 224


def _rows128(v, rows):
    flat = v.reshape(-1)
    return jnp.pad(flat, (0, rows * LANE - flat.shape[0])).reshape(rows, LANE)


def _pad_lanes(v):
    return jnp.pad(v.reshape(-1), (0, LANE - v.size))


def kernel(x, c, w_ada, b_ada, ffn1_norm, ffn1_w_gate, ffn1_w_up, ffn1_w_down, mix_norm, w_in, conv_w, conv_b, dt_bias, a_log, d_skip, ssd_norm_w, pool_w, pool_b, pool_scale, w_out, ffn2_norm, ffn2_w_gate, ffn2_w_up, ffn2_w_down, final_norm, loss_target, m_w_ada, m_b_ada, m_ffn1_norm, m_ffn1_w_gate, m_ffn1_w_up, m_ffn1_w_down, m_mix_norm, m_w_in, m_conv_w, m_conv_b, m_dt_bias, m_a_log, m_d_skip, m_ssd_norm_w, m_pool_w, m_pool_b, m_pool_scale, m_w_out, m_ffn2_norm, m_ffn2_w_gate, m_ffn2_w_up, m_ffn2_w_down, m_final_norm, v_w_ada, v_b_ada, v_ffn1_norm, v_ffn1_w_gate, v_ffn1_w_up, v_ffn1_w_down, v_mix_norm, v_w_in, v_conv_w, v_conv_b, v_dt_bias, v_a_log, v_d_skip, v_ssd_norm_w, v_pool_w, v_pool_b, v_pool_scale, v_w_out, v_ffn2_norm, v_ffn2_w_gate, v_ffn2_w_up, v_ffn2_w_down, v_final_norm):
    given = dict(locals())
    w = {n: given[n] for n in WEIGHT_NAMES}
    m = {n: given["m_" + n] for n in WEIGHT_NAMES}
    v = {n: given["v_" + n] for n in WEIGHT_NAMES}
    mx, my, mc = _mesh_pos()
    me = 4 * mx + 2 * my + mc

    small = jnp.concatenate([c.reshape(-1), conv_w.reshape(-1), pool_b.reshape(-1), pool_w.reshape(-1)])
    small_rows = 280
    gs = _all_gather(_rows128(small, small_rows), [(0, small_rows)], "ag_small").reshape(N_DEV, small_rows * LANE)
    c_all = gs[:, 0:D]
    conv_w_full = gs[:, 1024:2048].reshape(N_DEV, 4, 256).transpose(1, 0, 2).reshape(4, D_XBC)
    pool_b_full = gs[:, 2048:2176].reshape(N_DEV, 4, 32).transpose(1, 0, 2).reshape(1, D_POOL)
    pool_w_full = gs[:, 2176:2176 + 32768].reshape(N_DEV, 4, 32, POOL_GW).transpose(1, 0, 2, 3).reshape(4, POOL_GW, POOL_GW).astype(BF16)

    b_ada_cols = lax.dynamic_slice(b_ada, (0, me * ADA_SHARD), (1, ADA_SHARD))
    mod_part = _ada_mod(c_all, w_ada[0], b_ada_cols, "ada_mod")
    mod_all = _all_gather(mod_part, [(0, N_DEV)], "ag_mod").reshape(N_DEV, N_DEV, ADA_SHARD)
    mod = lax.dynamic_index_in_dim(mod_all, me, axis=1, keepdims=False).reshape(9, D)

    win_t_shard = jnp.pad(w_in[0].T, ((0, IN_SHARD_PAD - IN_SHARD), (0, 0)))
    packs = (jnp.concatenate([ffn1_w_gate[0].T, ffn1_w_up[0].T], axis=0).astype(BF16),
             ffn1_w_down[0].astype(BF16),
             win_t_shard.astype(BF16),
             w_out[0].astype(BF16),
             jnp.concatenate([ffn2_w_gate[0].T, ffn2_w_up[0].T, ffn2_w_down[0]], axis=0).astype(BF16))
    packs, _ = lax.optimization_barrier((packs, c_all))
    ffn_regions = [(0, FF_SHARD), (FF_SHARD, FF_SHARD), (2 * FF_SHARD, FF_SHARD)]
    full_a = _all_gather_async(packs[0], ffn_regions[0:2], "ag_weights_ffn1_in", 1)
    full_d = _all_gather_async(packs[1], ffn_regions[0:1], "ag_weights_ffn1_out", 2)
    full_in = _all_gather_async(packs[2], [(0, IN_SHARD_PAD)], "ag_weights_in_proj", 9)
    full_out = _all_gather_async(packs[3], [(0, OUT_SHARD)], "ag_weights_out_proj", 10)
    full_2 = _all_gather_async(packs[4], ffn_regions, "ag_weights_ffn2", 11)

    def in_proj_weights(x1):
        w_i, x1 = lax.optimization_barrier((full_in, x1))
        win_g = w_i.reshape(N_DEV, IN_SHARD_PAD, D)[:, :IN_SHARD].reshape(D_IN, D)
        win_t = jnp.concatenate([win_g[0:1024], win_g[1024:3072], win_g[3088:4112], win_g[3072:3088],
                                 jnp.zeros((D_IN_PAD - D_IN, D), BF16)], axis=0)
        return x1, win_t

    def out_proj_weights(ys):
        w_o, ys = lax.optimization_barrier((full_out, ys))
        return ys, w_o

    def ffn2_weights(x2):
        w_2, x2 = lax.optimization_barrier((full_2, x2))
        return x2, (w_2, w_2, 2)

    later_weights = (in_proj_weights, out_proj_weights, ffn2_weights)

    vecs = dict(ffn1_norm=ffn1_norm, mix_norm=mix_norm, ffn2_norm=ffn2_norm, final_norm=final_norm.reshape(1, D),
                conv_w=conv_w_full, conv_b=conv_b, ssd_norm_w=ssd_norm_w, pool_b=pool_b_full, pool_scale=pool_scale,
                ssd_par=jnp.concatenate([_pad_lanes(dt_bias)[None], _pad_lanes(a_log)[None], _pad_lanes(d_skip)[None],
                                         jnp.zeros((5, LANE), F32)], axis=0))
    dx0, st, dw1, dw3, d_win, d_wout, d_pool_w = _local_step(
        x[0], loss_target[0], mod, (full_a, full_d, 0), later_weights, pool_w_full, vecs, min(512, x.shape[1]))

    dwin = jnp.concatenate([d_win[0], d_win[1], d_win[3][0:16], d_win[2]], axis=0)
    dwin = jnp.pad(dwin.reshape(N_DEV, IN_SHARD, D), ((0, 0), (0, IN_SHARD_PAD - IN_SHARD), (0, 0))).reshape(N_DEV * IN_SHARD_PAD, D)
    dwout = jnp.concatenate(d_wout, axis=0)
    dpool = d_pool_w.reshape(4, N_DEV, 32, POOL_GW).transpose(1, 0, 2, 3).reshape(N_DEV * POOL_SHARD_ROWS, D)
    pos = jnp.stack([mc, 2 * mx + my]).astype(jnp.int32)
    by_key = dict(zip(("gate1", "up1", "down1", "gate2", "up2", "down2", "w_out", "w_in", "pool_w"),
                      (*dw1, *dw3, dwout, dwin, dpool)))
    reduced = {}
    for tag, keys, cid in (("ffn2", ("gate2", "up2", "down2"), 3), ("mix", ("w_in", "w_out", "pool_w"), 5),
                           ("ffn1", ("gate1", "up1", "down1"), 7)):
        grads = [(by_key[k], *GPACK[k]) for k in keys]
        from_sibling = _rs_pair(grads, GROUP_ROWS, f"rs_pair_{tag}", cid)
        pairs = {k: _pair_sum(g, from_sibling, off, rows, pos, f"rs_pair_sum_{k}") for (g, off, rows), k in zip(grads, keys)}
        from_chips = _rs_chips([(pairs[k][1], *GPACK[k]) for k in keys], GROUP_ROWS, f"rs_chips_{tag}", cid + 1)
        for k in keys:
            reduced[k] = _chip_sum(pairs[k][0], from_chips, *GPACK[k], pos, f"rs_chip_sum_{k}")

    def shard(k, rows=None):
        return reduced[k] if rows is None else reduced[k][0:rows]

    dmod = jnp.concatenate([st["ffn1"][0:3], st["mix"][0:3], st["ffn2"][0:3]], axis=0)
    sg = jnp.concatenate([
        dmod.reshape(-1), st["ffn1"][3], st["mix"][3], st["ffn2"][3], st["loss"][0], st["gn"][0], st["pool"][1],
        st["conv"][4], st["conv"][0:4].reshape(-1), st["pool"][0], st["ssd"][0:3].reshape(-1),
        jnp.zeros((5 * LANE,), F32), st["loss"][1]])
    sg_all = _all_gather(sg.reshape(SMALL_TOTAL, LANE), [(0, SMALL_TOTAL)], "ag_small_grads")
    tot, loss_b = _sum8_loss(sg_all, SMALL_ROWS["loss"][0], "small_sum")
    loss = loss_b[0, 0]
    dmod_all = sg_all.reshape(N_DEV, SMALL_TOTAL * LANE)[:, 0:9 * D]
    g_w_ada = _ada_grad(c_all, lax.dynamic_slice(dmod_all, (0, me * ADA_SHARD), (N_DEV, ADA_SHARD)), "ada_grad")

    def tot_rows(k):
        off, n = SMALL_ROWS[k]
        return tot[off:off + n].reshape(-1)

    g_conv_w = lax.dynamic_slice(tot_rows("conv_w").reshape(4, D_XBC), (0, me * 256), (4, 256))
    g_pool_b = lax.dynamic_slice(tot_rows("pool_b").reshape(4, POOL_GW), (0, me * 32), (4, 32))
    g_ssd = tot_rows("ssd").reshape(3, LANE)
    grad = {
        "w_ada": g_w_ada[None], "b_ada": tot_rows("dmod").reshape(1, 9 * D),
        "ffn1_norm": tot_rows("ffn1_norm")[None], "mix_norm": tot_rows("mix_norm")[None],
        "ffn2_norm": tot_rows("ffn2_norm")[None], "final_norm": tot_rows("final_norm"),
        "ssd_norm_w": tot_rows("ssd_norm_w")[None], "pool_scale": tot_rows("pool_scale")[None],
        "conv_b": tot_rows("conv_b")[None], "conv_w": g_conv_w[None], "pool_b": g_pool_b[None],
        "dt_bias": g_ssd[0:1, 0:N_HEADS], "a_log": g_ssd[1:2, 0:N_HEADS], "d_skip": g_ssd[2:3, 0:N_HEADS],
        "ffn1_w_gate": shard("gate1").T[None], "ffn1_w_up": shard("up1").T[None], "ffn1_w_down": shard("down1")[None],
        "ffn2_w_gate": shard("gate2").T[None], "ffn2_w_up": shard("up2").T[None], "ffn2_w_down": shard("down2")[None],
        "w_out": shard("w_out")[None], "w_in": shard("w_in", IN_SHARD).T[None],
        "pool_w": shard("pool_w").reshape(1, 4, 32, POOL_GW),
    }

    big = ("w_ada", "ffn1_w_gate", "ffn1_w_up", "ffn1_w_down", "w_in", "pool_w", "w_out",
           "ffn2_w_gate", "ffn2_w_up", "ffn2_w_down")
    delta, new_m, new_v = {}, {}, {}
    transposed = ("ffn1_w_gate", "ffn1_w_up", "w_in", "ffn2_w_gate", "ffn2_w_up")
    for n in big:
        shp = w[n].shape
        two_d = (shp[-3] * shp[-2], shp[-1]) if n == "pool_w" else shp[-2:]
        view = (lambda t: t.reshape(two_d).T) if n in transposed else (lambda t: t.reshape(two_d))
        back = (lambda t: t.T.reshape(shp)) if n in transposed else (lambda t: t.reshape(shp))
        d_, m_, v_ = _adamw(view(w[n]), view(grad[n]), view(m[n]), view(v[n]), f"adamw_{n}")
        delta[n], new_m[n], new_v[n] = back(d_), back(m_), back(v_)
    small_names = [n for n in WEIGHT_NAMES if n not in big]
    sizes = [LANE if w[n].size < LANE else w[n].size for n in small_names]
    small_rows_adam = -(-sum(sizes) // (8 * LANE)) * 8

    def pack_small(t):
        return _rows128(jnp.concatenate([_pad_lanes(t[n]) if t[n].size < LANE else t[n].reshape(-1) for n in small_names]),
                        small_rows_adam)

    d_s, m_s, v_s = _adamw(pack_small(w), pack_small(grad), pack_small(m), pack_small(v), "adamw_small")
    off = 0
    for n, size in zip(small_names, sizes):
        for res, packed in ((delta, d_s), (new_m, m_s), (new_v, v_s)):
            res[n] = packed.reshape(-1)[off:off + w[n].size].reshape(w[n].shape)
        off += size

    return (loss, dx0[None], *[grad[n] for n in WEIGHT_NAMES], *[delta[n] for n in WEIGHT_NAMES],
            *[new_m[n] for n in WEIGHT_NAMES], *[new_v[n] for n in WEIGHT_NAMES])
```

```python
import functools
import math

import jax
import jax.numpy as jnp
from jax import lax
from jax.experimental import pallas as pl
from jax.experimental.pallas import tpu as pltpu
from jax.experimental.pallas import tpu_sc as plsc

F32 = jnp.float32
BF16 = jnp.bfloat16
MESH = pl.DeviceIdType.MESH

N_DEV = 8
D = 1024
FF = 2816
D_SSD = 1024
N_HEADS = 16
HEAD_DIM = 64
N_GROUPS = 4
N_STATE = 128
CHUNK = 128
GROUP_W = D_SSD // N_GROUPS
D_XBC = D_SSD + 2 * N_GROUPS * N_STATE
D_POOL = 1024
POOL_WINDOWS = (2, 4, 8, 16)
POOL_GW = 256
D_IN = 4112
D_IN_PAD = 4224
COL_Z, COL_XBC, COL_U, COL_DT = 0, 1024, 3072, 4096
EPS = 1e-6
FFN_RES = 0.5
LANE = 128
HALO = 16

ADAM_LR, ADAM_B1, ADAM_B2, ADAM_EPS, ADAM_WD, ADAM_STEP = 0.001, 0.9, 0.999, 1e-08, 0.01, 10

VMEM_LIMIT = 56 << 20


def _cp(sem):
    return pltpu.CompilerParams(dimension_semantics=sem, vmem_limit_bytes=VMEM_LIMIT)


def _dot(a, b, ca, cb, prec=None):
    return lax.dot_general(a, b, (((ca,), (cb,)), ((), ())), precision=prec,
                           preferred_element_type=F32)


def _exact_dot(a, b):
    return _dot(a, b, 1, 0, lax.Precision.HIGHEST)


def _sigmoid(v):
    return 1.0 / (1.0 + jnp.exp(-v))


def _silu_grad(v, sg):
    return sg * (1.0 + v * (1.0 - sg))


def _mm_nt(a, bt, tm, tn, out_dtype, name):
    m, k = a.shape
    n = bt.shape[0]

    def body(a_ref, b_ref, o_ref):
        o_ref[...] = _dot(a_ref[...], b_ref[...], 1, 1).astype(out_dtype)

    return pl.pallas_call(
        body, name=name, grid=(n // tn, m // tm),
        in_specs=[pl.BlockSpec((tm, k), lambda j, i: (i, 0)),
                  pl.BlockSpec((tn, k), lambda j, i: (j, 0))],
        out_specs=pl.BlockSpec((tm, tn), lambda j, i: (i, j)),
        out_shape=jax.ShapeDtypeStruct((m, n), out_dtype),
        compiler_params=_cp(("parallel", "parallel")))(a, bt)


def _mm_tn(a, b, tm, tk, name):
    kk, m = a.shape
    n = b.shape[1]
    nk = kk // tk

    def body(a_ref, b_ref, o_ref, acc):
        k = pl.program_id(1)

        @pl.when(k == 0)
        def _():
            acc[...] = jnp.zeros_like(acc)

        acc[...] += _dot(a_ref[...], b_ref[...], 0, 0)

        @pl.when(k == nk - 1)
        def _():
            o_ref[...] = acc[...]

    return pl.pallas_call(
        body, name=name, grid=(m // tm, nk),
        in_specs=[pl.BlockSpec((tk, tm), lambda i, k: (k, i)),
                  pl.BlockSpec((tk, n), lambda i, k: (k, 0))],
        out_specs=pl.BlockSpec((tm, n), lambda i, k: (i, 0)),
        out_shape=jax.ShapeDtypeStruct((m, n), F32),
        scratch_shapes=[pltpu.VMEM((tm, n), F32)],
        compiler_params=_cp(("parallel", "arbitrary")))(a, b)


def _modulated(xv, wn, mod_ref, k):
    r = lax.rsqrt(jnp.mean(xv * xv, axis=-1, keepdims=True) + EPS)
    hn = xv * r * wn
    return (hn * (1.0 + mod_ref[3 * k + 1:3 * k + 2, :]) + mod_ref[3 * k:3 * k + 1, :]).astype(BF16)


def _prenorm(x, wn, mod, k, tm, name):
    seq = x.shape[0]

    def body(x_ref, wn_ref, mod_ref, h_ref):
        h_ref[...] = _modulated(x_ref[...], wn_ref[...], mod_ref, k)

    return pl.pallas_call(
        body, name=name, grid=(seq // tm,),
        in_specs=[pl.BlockSpec((tm, D), lambda i: (i, 0)),
                  pl.BlockSpec((1, D), lambda i: (0, 0)),
                  pl.BlockSpec((9, D), lambda i: (0, 0))],
        out_specs=pl.BlockSpec((tm, D), lambda i: (i, 0)),
        out_shape=jax.ShapeDtypeStruct((seq, D), BF16),
        compiler_params=_cp(("parallel",)))(x, wn, mod)


def _norm_bwd(dh, xv, dxo, branch, wn, sc, res, stats_ref, first):
    r = lax.rsqrt(jnp.mean(xv * xv, axis=-1, keepdims=True) + EPS)
    xn = xv * r
    dhn = dh * (1.0 + sc)
    dxn = dhn * wn
    dx = dxo + r * (dxn - xn * jnp.mean(dxn * xn, axis=-1, keepdims=True))
    rows = jnp.concatenate([
        jnp.sum(dh, axis=0, keepdims=True),
        jnp.sum(dh * (xn * wn), axis=0, keepdims=True),
        jnp.sum(branch * dxo, axis=0, keepdims=True) * res,
        jnp.sum(dhn * xn, axis=0, keepdims=True),
        jnp.zeros((4, D), F32)], axis=0)

    @pl.when(first)
    def _():
        stats_ref[...] = rows

    @pl.when(jnp.logical_not(first))
    def _():
        stats_ref[...] += rows

    return dx


def _loss_head(x3, wf, tgt, mod, tm, name):
    seq = x3.shape[0]

    def body(x_ref, w_ref, t_ref, mod_ref, dx_ref, df_ref, st_ref):
        xv = x_ref[...]
        wv = w_ref[...]
        r = lax.rsqrt(jnp.mean(xv * xv, axis=-1, keepdims=True) + EPS)
        xn = xv * r
        e = xn * wv - t_ref[...]
        dy = e * (1.0 / D)
        dxn = dy * wv
        dx = r * (dxn - xn * jnp.mean(dxn * xn, axis=-1, keepdims=True))
        dx_ref[...] = dx
        df_ref[...] = (dx * (FFN_RES * mod_ref[8:9, :])).astype(BF16)
        rows = jnp.concatenate([
            jnp.sum(dy * xn, axis=0, keepdims=True),
            jnp.sum(e * e, axis=0, keepdims=True) * (0.5 / D),
            jnp.zeros((6, D), F32)], axis=0)

        @pl.when(pl.program_id(0) == 0)
        def _():
            st_ref[...] = rows

        @pl.when(pl.program_id(0) != 0)
        def _():
            st_ref[...] += rows

    return pl.pallas_call(
        body, name=name, grid=(seq // tm,),
        in_specs=[pl.BlockSpec((tm, D), lambda i: (i, 0)),
                  pl.BlockSpec((1, D), lambda i: (0, 0)),
                  pl.BlockSpec((tm, D), lambda i: (i, 0)),
                  pl.BlockSpec((9, D), lambda i: (0, 0))],
        out_specs=[pl.BlockSpec((tm, D), lambda i: (i, 0)),
                   pl.BlockSpec((tm, D), lambda i: (i, 0)),
                   pl.BlockSpec((8, D), lambda i: (0, 0))],
        out_shape=[jax.ShapeDtypeStruct((seq, D), F32), jax.ShapeDtypeStruct((seq, D), BF16),
                   jax.ShapeDtypeStruct((8, D), F32)],
        compiler_params=_cp(("arbitrary",)))(x3, wf, tgt, mod)


def _ffn_up(h, w, blk, tm, tn, name):
    seq = h.shape[0]
    nj = FF // tn

    def body(h_ref, wg_ref, wu_ref, g_ref, u_ref, a_ref):
        hv = h_ref[...]
        g = _dot(hv, wg_ref[...], 1, 1)
        u = _dot(hv, wu_ref[...], 1, 1)
        g_ref[...] = g.astype(BF16)
        u_ref[...] = u.astype(BF16)
        a_ref[...] = (g * _sigmoid(g) * u).astype(BF16)

    act = pl.BlockSpec((tm, tn), lambda j, i: (i, j))
    return pl.pallas_call(
        body, name=name, grid=(nj, seq // tm),
        in_specs=[pl.BlockSpec((tm, D), lambda j, i: (i, 0)),
                  pl.BlockSpec((tn, D), lambda j, i: (blk * nj + j, 0)),
                  pl.BlockSpec((tn, D), lambda j, i: ((blk + 1) * nj + j, 0))],
        out_specs=[act, act, act],
        out_shape=[jax.ShapeDtypeStruct((seq, FF), BF16)] * 3,
        compiler_params=_cp(("parallel", "parallel")))(h, w, w)


def _ffn_down(a, w, blk, x, mod, grow, wn_next, k_next, tm, tk, name):
    seq = a.shape[0]
    nk = FF // tk
    chain = k_next is not None

    def body(a_ref, w_ref, x_ref, mod_ref, wn_ref, xo_ref, f_ref, *rest):
        acc = rest[-1]
        k = pl.program_id(1)

        @pl.when(k == 0)
        def _():
            acc[...] = jnp.zeros_like(acc)

        acc[...] += _dot(a_ref[...], w_ref[...], 1, 0)

        @pl.when(k == nk - 1)
        def _():
            f = acc[...]
            f_ref[...] = f.astype(BF16)
            xo = x_ref[...] + (FFN_RES * mod_ref[grow:grow + 1, :]) * f
            xo_ref[...] = xo
            if chain:
                rest[0][...] = _modulated(xo, wn_ref[...], mod_ref, k_next)

    tok = pl.BlockSpec((tm, D), lambda i, k: (i, 0))
    return pl.pallas_call(
        body, name=name, grid=(seq // tm, nk),
        in_specs=[pl.BlockSpec((tm, tk), lambda i, k: (i, k)),
                  pl.BlockSpec((tk, D), lambda i, k: (blk * nk + k, 0)),
                  tok,
                  pl.BlockSpec((9, D), lambda i, k: (0, 0)),
                  pl.BlockSpec((1, D), lambda i, k: (0, 0))],
        out_specs=[tok, tok] + ([tok] if chain else []),
        out_shape=[jax.ShapeDtypeStruct((seq, D), F32), jax.ShapeDtypeStruct((seq, D), BF16)]
        + ([jax.ShapeDtypeStruct((seq, D), BF16)] if chain else []),
        scratch_shapes=[pltpu.VMEM((tm, D), F32)],
        compiler_params=_cp(("parallel", "arbitrary")))(a, w, x, mod, wn_next)


def _ffn_bwd_da(df, w, blk, g, u, tm, tn, name):
    seq = df.shape[0]
    nj = FF // tn

    def body(df_ref, w_ref, g_ref, u_ref, dg_ref, du_ref):
        da = _dot(df_ref[...], w_ref[...], 1, 1)
        gv = g_ref[...].astype(F32)
        uv = u_ref[...].astype(F32)
        sg = _sigmoid(gv)
        dg_ref[...] = (da * uv * _silu_grad(gv, sg)).astype(BF16)
        du_ref[...] = (da * (gv * sg)).astype(BF16)

    act = pl.BlockSpec((tm, tn), lambda j, i: (i, j))
    return pl.pallas_call(
        body, name=name, grid=(nj, seq // tm),
        in_specs=[pl.BlockSpec((tm, D), lambda j, i: (i, 0)),
                  pl.BlockSpec((tn, D), lambda j, i: (blk * nj + j, 0)),
                  act, act],
        out_specs=[act, act],
        out_shape=[jax.ShapeDtypeStruct((seq, FF), BF16)] * 2,
        compiler_params=_cp(("parallel", "parallel")))(df, w, g, u)


def _ffn_bwd_dh(dg, du, w, blk, x, dxo, fb, wn, mod, k, nxt, tm, tk, name):
    seq = x.shape[0]
    nk = FF // tk

    def body(dg_ref, du_ref, wg_ref, wu_ref, x_ref, dxo_ref, f_ref, wn_ref, mod_ref, dx_ref, st_ref, *rest):
        acc = rest[-1]
        kk = pl.program_id(1)
        first = pl.program_id(0) == 0

        @pl.when(kk == 0)
        def _():
            acc[...] = jnp.zeros_like(acc)

        acc[...] += _dot(dg_ref[...], wg_ref[...], 1, 0) + _dot(du_ref[...], wu_ref[...], 1, 0)

        @pl.when(kk == nk - 1)
        def _():
            dx = _norm_bwd(acc[...], x_ref[...], dxo_ref[...], f_ref[...].astype(F32), wn_ref[...],
                           mod_ref[3 * k + 1:3 * k + 2, :], FFN_RES, st_ref, first)
            dx_ref[...] = dx
            if nxt is not None:
                rest[0][...] = (dx * (nxt[1] * mod_ref[nxt[0]:nxt[0] + 1, :])).astype(BF16)

    tok = pl.BlockSpec((tm, D), lambda i, kk: (i, 0))
    return pl.pallas_call(
        body, name=name, grid=(seq // tm, nk),
        in_specs=[pl.BlockSpec((tm, tk), lambda i, kk: (i, kk)),
                  pl.BlockSpec((tm, tk), lambda i, kk: (i, kk)),
                  pl.BlockSpec((tk, D), lambda i, kk: (blk * nk + kk, 0)),
                  pl.BlockSpec((tk, D), lambda i, kk: ((blk + 1) * nk + kk, 0)),
                  tok, tok, tok,
                  pl.BlockSpec((1, D), lambda i, kk: (0, 0)),
                  pl.BlockSpec((9, D), lambda i, kk: (0, 0))],
        out_specs=[tok, pl.BlockSpec((8, D), lambda i, kk: (0, 0))] + ([tok] if nxt is not None else []),
        out_shape=[jax.ShapeDtypeStruct((seq, D), F32), jax.ShapeDtypeStruct((8, D), F32)]
        + ([jax.ShapeDtypeStruct((seq, D), BF16)] if nxt is not None else []),
        scratch_shapes=[pltpu.VMEM((tm, D), F32)],
        compiler_params=_cp(("arbitrary", "arbitrary")))(dg, du, w, w, x, dxo, fb, wn, mod)


def _ffn_forward(x, h, w, mod, k, wn_next, k_next, tm, tag):
    w_gu, w_d, blk_d = w
    g, u, a = _ffn_up(h, w_gu, 0, tm, FF // 2, f"{tag}_up")
    outs = _ffn_down(a, w_d, blk_d, x, mod, 3 * k + 2, wn_next, k_next, tm, FF // 2, f"{tag}_down")
    return outs[0], (outs[2] if k_next is not None else None), (x, h, g, u, a, outs[1])


def _ffn_backward(dxo, df, saved, w, wn, mod, k, nxt, tm, tag):
    w_gu, w_d, blk_d = w
    x, h, g, u, a, fb = saved
    dg, du = _ffn_bwd_da(df, w_d, blk_d, g, u, tm, FF // 2, f"{tag}_bwd_da")
    seq = x.shape[0]
    tk = min(seq, 512)
    d_gate_t = _mm_tn(dg, h, FF // 2, tk, f"{tag}_dw_gate")
    d_up_t = _mm_tn(du, h, FF // 2, tk, f"{tag}_dw_up")
    d_down = _mm_tn(a, df, FF // 2, tk, f"{tag}_dw_down")
    dws, dg, du = lax.optimization_barrier(((d_gate_t, d_up_t, d_down), dg, du))
    outs = _ffn_bwd_dh(dg, du, w_gu, 0, x, dxo, fb, wn, mod, k, nxt, tm, FF // 2, f"{tag}_bwd_dh")
    return outs[0], (outs[2] if nxt is not None else None), outs[1], dws


def _prev_rows(tm, col):
    return pl.BlockSpec((HALO, 1024), lambda i, j: (jnp.maximum(i * (tm // HALO) - 1, 0), col + j))


def _conv_pre(ext, cw, cb, rows):
    pre = cb + cw[3:4, :] * ext
    for s in (1, 2, 3):
        pre = pre + cw[3 - s:4 - s, :] * pltpu.roll(ext, s, 0)
    return pre[HALO:HALO + rows]


def _conv_fwd(proj, cw, cb, tm, name):
    seq = proj.shape[0]

    def body(x_ref, p_ref, cw_ref, cb_ref, o_ref):
        prev = jnp.where(pl.program_id(0) == 0, 0.0, p_ref[...])
        ext = jnp.concatenate([prev, x_ref[...]], axis=0)
        pre = _conv_pre(ext, cw_ref[...], cb_ref[...], tm)
        o_ref[...] = pre * _sigmoid(pre)

    c0 = COL_XBC // 1024
    return pl.pallas_call(
        body, name=name, grid=(seq // tm, 2),
        in_specs=[pl.BlockSpec((tm, 1024), lambda i, j: (i, c0 + j)),
                  _prev_rows(tm, c0),
                  pl.BlockSpec((4, 1024), lambda i, j: (0, j)),
                  pl.BlockSpec((1, 1024), lambda i, j: (0, j))],
        out_specs=pl.BlockSpec((tm, 1024), lambda i, j: (i, j)),
        out_shape=jax.ShapeDtypeStruct((seq, D_XBC), F32),
        compiler_params=_cp(("parallel", "parallel")))(proj, proj, cw, cb)


def _conv_bwd(dact, proj, cw, cb, tm, name):
    seq = proj.shape[0]
    ni = seq // tm

    def body(d_ref, dn_ref, x_ref, p_ref, n_ref, cw_ref, cb_ref, o_ref, st_ref):
        i = pl.program_id(1)
        cwv = cw_ref[...]
        prev = jnp.where(i == 0, 0.0, p_ref[...])
        ext = jnp.concatenate([prev, x_ref[...], n_ref[...]], axis=0)
        pre = _conv_pre(ext, cwv, cb_ref[...], tm + HALO)
        dnext = jnp.where(i == ni - 1, 0.0, dn_ref[...])
        dext = jnp.concatenate([d_ref[...], dnext], axis=0)
        dpre = dext * _silu_grad(pre, _sigmoid(pre))
        n = tm + HALO
        dx = cwv[3:4, :] * dpre
        for s in (1, 2, 3):
            dx = dx + cwv[3 - s:4 - s, :] * pltpu.roll(dpre, n - s, 0)
        o_ref[...] = dx[:tm].astype(BF16)
        dcur = dpre[:tm]
        rows = [jnp.sum(dcur * pltpu.roll(ext, 3 - k, 0)[HALO:HALO + tm], axis=0, keepdims=True) for k in range(3)]
        rows.append(jnp.sum(dcur * ext[HALO:HALO + tm], axis=0, keepdims=True))
        rows.append(jnp.sum(dcur, axis=0, keepdims=True))
        rows.append(jnp.zeros((3, 1024), F32))
        rows = jnp.concatenate(rows, axis=0)

        @pl.when(i == 0)
        def _():
            st_ref[...] = rows

        @pl.when(i != 0)
        def _():
            st_ref[...] += rows

    c0 = COL_XBC // 1024
    return pl.pallas_call(
        body, name=name, grid=(2, ni),
        in_specs=[pl.BlockSpec((tm, 1024), lambda j, i: (i, j)),
                  pl.BlockSpec((HALO, 1024), lambda j, i: (jnp.minimum((i + 1) * (tm // HALO), seq // HALO - 1), j)),
                  pl.BlockSpec((tm, 1024), lambda j, i: (i, c0 + j)),
                  pl.BlockSpec((HALO, 1024), lambda j, i: (jnp.maximum(i * (tm // HALO) - 1, 0), c0 + j)),
                  pl.BlockSpec((HALO, 1024), lambda j, i: (jnp.minimum((i + 1) * (tm // HALO), seq // HALO - 1), c0 + j)),
                  pl.BlockSpec((4, 1024), lambda j, i: (0, j)),
                  pl.BlockSpec((1, 1024), lambda j, i: (0, j))],
        out_specs=[pl.BlockSpec((tm, 1024), lambda j, i: (i, j)),
                   pl.BlockSpec((8, 1024), lambda j, i: (0, j))],
        out_shape=[jax.ShapeDtypeStruct((seq, D_XBC), BF16), jax.ShapeDtypeStruct((8, D_XBC), F32)],
        compiler_params=_cp(("parallel", "arbitrary")))(dact, dact, proj, proj, proj, cw, cb)


def _bf16_parts(x, n):
    parts, rest = [], x
    for _ in range(n):
        p = rest.astype(BF16)
        parts.append(p)
        rest = rest - p.astype(F32)
    return parts


def _pick(x, sel, n):
    m = x.shape[0]
    prod = _dot(jnp.concatenate(_bf16_parts(x, n), axis=0), sel, 1, 0)
    acc = prod[0:m]
    for i in range(1, n):
        acc = acc + prod[i * m:(i + 1) * m]
    return acc


def _running(mask, x, n):
    k = x.shape[1]
    prod = _dot(mask, jnp.concatenate(_bf16_parts(x, n), axis=1), 1, 0)
    acc = prod[:, 0:k]
    for i in range(1, n):
        acc = acc + prod[:, i * k:(i + 1) * k]
    return acc


def _head_expand():
    r = lax.broadcasted_iota(jnp.int32, (LANE, D_SSD), 0)
    c = lax.broadcasted_iota(jnp.int32, (LANE, D_SSD), 1)
    return (c // HEAD_DIM == r).astype(BF16)


def _head_reduce():
    r = lax.broadcasted_iota(jnp.int32, (D_SSD, LANE), 0)
    c = lax.broadcasted_iota(jnp.int32, (D_SSD, LANE), 1)
    return (r // HEAD_DIM == c).astype(BF16)


def _ssd_common(dtr, par):
    q = CHUNK
    v = dtr + par[0:1, :]
    dt = jnp.maximum(v, 0.0) + jnp.log(1.0 + jnp.exp(-jnp.abs(v)))
    a = -jnp.exp(par[1:2, :])
    adt = dt * a
    li = lax.broadcasted_iota(jnp.int32, (q, q), 0)
    si = lax.broadcasted_iota(jnp.int32, (q, q), 1)
    causal = li >= si
    acs = _running(causal.astype(BF16), adt, 3)
    expand = _head_expand()
    both_l = _pick(jnp.concatenate([dt, acs], axis=0), expand, 3)
    dt_l, acs_l = both_l[0:q], both_l[q:2 * q]
    dskip_l = _pick(jnp.broadcast_to(par[2:3, :], (16, LANE)), expand, 3)[0:1, :]
    last_l = acs_l[q - 1:q, :]
    return dict(v=v, dt=dt, a=a, acs=acs, acs_t=acs.T, causal=causal, dt_l=dt_l, acs_l=acs_l,
                ea_l=jnp.exp(acs_l), ds_l=jnp.exp(last_l - acs_l), cd_l=jnp.exp(last_l), dskip_l=dskip_l)


def _decay(cm, h):
    seg = cm["acs"][:, h:h + 1] - cm["acs_t"][h:h + 1, :]
    return jnp.exp(jnp.where(cm["causal"], seg, -jnp.inf))


def _lane_mask(r):
    lane = lax.broadcasted_iota(jnp.int32, (1, GROUP_W), 1)
    return lane // HEAD_DIM == r


def _ssd_fwd(xbc, proj, par, name):
    seq = xbc.shape[0]
    nc = seq // CHUNK
    q = CHUNK

    def body(x_ref, dt_ref, par_ref, y_ref, hp_ref, state):
        @pl.when(pl.program_id(0) == 0)
        def _():
            state[...] = jnp.zeros_like(state)

        cm = _ssd_common(dt_ref[...], par_ref[...])
        for g in range(N_GROUPS):
            lo = g * GROUP_W
            xs = x_ref[:, lo:lo + GROUP_W]
            bm = x_ref[:, D_SSD + g * N_STATE:D_SSD + (g + 1) * N_STATE].astype(BF16)
            cmat = x_ref[:, D_SSD + N_GROUPS * N_STATE + g * N_STATE:D_SSD + N_GROUPS * N_STATE + (g + 1) * N_STATE].astype(BF16)
            xdt = xs * cm["dt_l"][:, lo:lo + GROUP_W]
            xdt_b = xdt.astype(BF16)
            cb = _dot(cmat, bm, 1, 1)
            yd = jnp.zeros((q, GROUP_W), F32)
            for r in range(4):
                s_h = (cb * _decay(cm, 4 * g + r)).astype(BF16)
                yd = jnp.where(_lane_mask(r), _dot(s_h, xdt_b, 1, 0), yd)
            hg = state[g]
            hp_ref[0, g] = hg
            yo = _dot(cmat, hg.astype(BF16), 1, 0) * cm["ea_l"][:, lo:lo + GROUP_W]
            y_ref[:, lo:lo + GROUP_W] = yd + yo + cm["dskip_l"][:, lo:lo + GROUP_W] * xs
            xds = (xdt * cm["ds_l"][:, lo:lo + GROUP_W]).astype(BF16)
            state[g] = hg * cm["cd_l"][:, lo:lo + GROUP_W] + _dot(bm, xds, 0, 0)

    return pl.pallas_call(
        body, name=name, grid=(nc,),
        in_specs=[pl.BlockSpec((q, D_XBC), lambda c: (c, 0)),
                  pl.BlockSpec((q, LANE), lambda c: (c, COL_DT // LANE)),
                  pl.BlockSpec((8, LANE), lambda c: (0, 0))],
        out_specs=[pl.BlockSpec((q, D_SSD), lambda c: (c, 0)),
                   pl.BlockSpec((1, N_GROUPS, N_STATE, GROUP_W), lambda c: (c, 0, 0, 0))],
        out_shape=[jax.ShapeDtypeStruct((seq, D_SSD), F32),
                   jax.ShapeDtypeStruct((nc, N_GROUPS, N_STATE, GROUP_W), F32)],
        scratch_shapes=[pltpu.VMEM((N_GROUPS, N_STATE, GROUP_W), F32)],
        compiler_params=_cp(("arbitrary",)))(xbc, proj, par)


def _ssd_bwd(dy, xbc, proj, par, hprev, name):
    seq = xbc.shape[0]
    nc = seq // CHUNK
    q = CHUNK

    def body(dy_ref, x_ref, dt_ref, par_ref, hp_ref, dx_ref, ddt_ref, st_ref, dstate):
        step = pl.program_id(0)

        @pl.when(step == 0)
        def _():
            dstate[...] = jnp.zeros_like(dstate)

        par = par_ref[...]
        cm = _ssd_common(dt_ref[...], par)
        reduce = _head_reduce()
        lane128 = lax.broadcasted_iota(jnp.int32, (1, LANE), 1)
        row128 = lax.broadcasted_iota(jnp.int32, (LANE, 1), 0)
        d_acs = jnp.zeros((q, LANE), F32)
        d_acs_t = jnp.zeros((LANE, q), F32)
        last_terms = []
        acs_terms = []
        dxdt_all = []
        for g in range(N_GROUPS):
            lo = g * GROUP_W
            sl = slice(lo, lo + GROUP_W)
            xs = x_ref[:, sl]
            bm32 = x_ref[:, D_SSD + g * N_STATE:D_SSD + (g + 1) * N_STATE]
            cm32 = x_ref[:, D_SSD + N_GROUPS * N_STATE + g * N_STATE:D_SSD + N_GROUPS * N_STATE + (g + 1) * N_STATE]
            bm = bm32.astype(BF16)
            cmat = cm32.astype(BF16)
            dyg = dy_ref[:, sl]
            dyg_b = dyg.astype(BF16)
            xdt = xs * cm["dt_l"][:, sl]
            xdt_b = xdt.astype(BF16)
            hg = hp_ref[0, g]
            hg_b = hg.astype(BF16)
            dhg = dstate[g]
            dhg_b = dhg.astype(BF16)
            ea = cm["ea_l"][:, sl]
            ds = cm["ds_l"][:, sl]
            cd = cm["cd_l"][:, sl]
            yoff = _dot(cmat, hg_b, 1, 0) * ea
            dw = (dyg * ea).astype(BF16)
            d_c = _dot(dw, hg_b, 1, 1)
            d_hprev = _dot(cmat, dw, 0, 0) + dhg * cd
            t_acs = dyg * yoff
            d_last_g = jnp.sum(dhg * hg, axis=0, keepdims=True) * cd
            xds_b = (xdt * ds).astype(BF16)
            dxds = _dot(bm, dhg_b, 1, 0)
            d_b = _dot(xds_b, dhg_b, 1, 1)
            dxdt = dxds * ds
            t_ds = dxds * xdt * ds
            t_acs = t_acs - t_ds
            d_last_g = d_last_g + jnp.sum(t_ds, axis=0, keepdims=True)
            cb = _dot(cmat, bm, 1, 1)
            d_cb = jnp.zeros((q, q), F32)
            for r in range(4):
                h = 4 * g + r
                dec = _decay(cm, h)
                s_h = cb * dec
                mask = _lane_mask(r)
                d_s = _dot(jnp.where(mask, dyg, 0.0).astype(BF16), xdt_b, 1, 1)
                dxdt = dxdt + jnp.where(mask, _dot(s_h.astype(BF16), dyg_b, 0, 0), 0.0)
                d_cb = d_cb + d_s * dec
                d_m = d_s * s_h
                d_acs = d_acs + jnp.where(lane128 == h, jnp.sum(d_m, axis=1, keepdims=True), 0.0)
                d_acs_t = d_acs_t + jnp.where(row128 == h, jnp.sum(d_m, axis=0, keepdims=True), 0.0)
            d_cb_b = d_cb.astype(BF16)
            d_c = d_c + _dot(d_cb_b, bm, 1, 0)
            d_b = d_b + _dot(d_cb_b, cmat, 0, 0)
            dstate[g] = d_hprev
            dx_ref[:, sl] = dxdt * cm["dt_l"][:, sl] + cm["dskip_l"][:, sl] * dyg
            dx_ref[:, D_SSD + g * N_STATE:D_SSD + (g + 1) * N_STATE] = d_b
            dx_ref[:, D_SSD + N_GROUPS * N_STATE + g * N_STATE:D_SSD + N_GROUPS * N_STATE + (g + 1) * N_STATE] = d_c
            acs_terms.append(t_acs)
            dxdt_all.append(dxdt * xs)
            last_terms.append(d_last_g)
        t_acs_l = jnp.concatenate(acs_terms, axis=1)
        d_dt_l = jnp.concatenate(dxdt_all, axis=1)
        d_last_l = jnp.concatenate(last_terms, axis=1)
        per_head = _pick(jnp.concatenate([t_acs_l, d_dt_l], axis=0), reduce, 2)
        skip_l = jnp.sum(dy_ref[...] * x_ref[:, 0:D_SSD], axis=0, keepdims=True)
        singles = _pick(jnp.concatenate([d_last_l, skip_l, jnp.zeros((14, D_SSD), F32)], axis=0), reduce, 3)
        d_acs = d_acs + per_head[0:q] - d_acs_t.T
        last_row = lax.broadcasted_iota(jnp.int32, (q, 1), 0) == q - 1
        d_acs = d_acs + jnp.where(last_row, singles[0:1, :], 0.0)
        li = lax.broadcasted_iota(jnp.int32, (q, q), 0)
        si = lax.broadcasted_iota(jnp.int32, (q, q), 1)
        d_adt = _running((si >= li).astype(BF16), d_acs, 3)
        d_dt = per_head[q:2 * q] + d_adt * cm["a"]
        d_dtr = d_dt * _sigmoid(cm["v"])
        ddt_ref[...] = d_dtr.astype(BF16)
        d_skip = singles[1:2, :]
        rows = jnp.concatenate([
            jnp.sum(d_dtr, axis=0, keepdims=True),
            jnp.sum(d_adt * cm["dt"], axis=0, keepdims=True) * cm["a"],
            d_skip,
            jnp.zeros((5, LANE), F32)], axis=0)

        @pl.when(step == 0)
        def _():
            st_ref[...] = rows

        @pl.when(step != 0)
        def _():
            st_ref[...] += rows

    rev = lambda c: nc - 1 - c
    return pl.pallas_call(
        body, name=name, grid=(nc,),
        in_specs=[pl.BlockSpec((q, D_SSD), lambda c: (rev(c), 0)),
                  pl.BlockSpec((q, D_XBC), lambda c: (rev(c), 0)),
                  pl.BlockSpec((q, LANE), lambda c: (rev(c), COL_DT // LANE)),
                  pl.BlockSpec((8, LANE), lambda c: (0, 0)),
                  pl.BlockSpec((1, N_GROUPS, N_STATE, GROUP_W), lambda c: (rev(c), 0, 0, 0))],
        out_specs=[pl.BlockSpec((q, D_XBC), lambda c: (rev(c), 0)),
                   pl.BlockSpec((q, LANE), lambda c: (rev(c), 0)),
                   pl.BlockSpec((8, LANE), lambda c: (0, 0))],
        out_shape=[jax.ShapeDtypeStruct((seq, D_XBC), F32),
                   jax.ShapeDtypeStruct((seq, LANE), BF16),
                   jax.ShapeDtypeStruct((8, LANE), F32)],
        scratch_shapes=[pltpu.VMEM((N_GROUPS, N_STATE, GROUP_W), F32)],
        compiler_params=_cp(("arbitrary",)))(dy, xbc, proj, par, hprev)


def _gate_norm_fwd(y, proj, wn, tm, name):
    seq = y.shape[0]

    def body(y_ref, z_ref, w_ref, o_ref):
        for g in range(N_GROUPS):
            sl = slice(g * GROUP_W, (g + 1) * GROUP_W)
            zv = z_ref[:, sl]
            yz = y_ref[:, sl] * (zv * _sigmoid(zv))
            r = lax.rsqrt(jnp.mean(yz * yz, axis=-1, keepdims=True) + EPS)
            o_ref[:, sl] = (yz * r * w_ref[:, sl]).astype(BF16)

    tok = pl.BlockSpec((tm, D_SSD), lambda i: (i, 0))
    return pl.pallas_call(
        body, name=name, grid=(seq // tm,),
        in_specs=[tok, tok, pl.BlockSpec((1, D_SSD), lambda i: (0, 0))],
        out_specs=tok,
        out_shape=jax.ShapeDtypeStruct((seq, D_SSD), BF16),
        compiler_params=_cp(("parallel",)))(y, proj, wn)


def _gate_norm_bwd(dys, y, proj, wn, tm, name):
    seq = y.shape[0]

    def body(d_ref, y_ref, z_ref, w_ref, dy_ref, dz_ref, st_ref):
        rows = []
        for g in range(N_GROUPS):
            sl = slice(g * GROUP_W, (g + 1) * GROUP_W)
            zv = z_ref[:, sl]
            yv = y_ref[:, sl]
            sg = _sigmoid(zv)
            sz = zv * sg
            yz = yv * sz
            r = lax.rsqrt(jnp.mean(yz * yz, axis=-1, keepdims=True) + EPS)
            yn = yz * r
            dv = d_ref[:, sl]
            dyn = dv * w_ref[:, sl]
            dyz = r * (dyn - yn * jnp.mean(dyn * yn, axis=-1, keepdims=True))
            dy_ref[:, sl] = dyz * sz
            dz_ref[:, sl] = (dyz * yv * _silu_grad(zv, sg)).astype(BF16)
            rows.append(jnp.sum(dv * yn, axis=0, keepdims=True))
        rows = jnp.concatenate([jnp.concatenate(rows, axis=1), jnp.zeros((7, D_SSD), F32)], axis=0)

        @pl.when(pl.program_id(0) == 0)
        def _():
            st_ref[...] = rows

        @pl.when(pl.program_id(0) != 0)
        def _():
            st_ref[...] += rows

    tok = pl.BlockSpec((tm, D_SSD), lambda i: (i, 0))
    return pl.pallas_call(
        body, name=name, grid=(seq // tm,),
        in_specs=[tok, tok, tok, pl.BlockSpec((1, D_SSD), lambda i: (0, 0))],
        out_specs=[tok, tok, pl.BlockSpec((8, D_SSD), lambda i: (0, 0))],
        out_shape=[jax.ShapeDtypeStruct((seq, D_SSD), F32), jax.ShapeDtypeStruct((seq, D_SSD), BF16),
                   jax.ShapeDtypeStruct((8, D_SSD), F32)],
        compiler_params=_cp(("arbitrary",)))(dys, y, proj, wn)


def _pool_counts(t0, rows, w):
    pos = (t0 + 1 + lax.broadcasted_iota(jnp.int32, (rows, 1), 0)).astype(F32)
    return jnp.minimum(pos, float(w))


def _window_means(ext, t0):
    n = ext.shape[0]
    outs = []
    run = ext
    width = 1
    sums = {}
    while width < 16:
        run = run + pltpu.roll(run, width, 0)
        width *= 2
        sums[width] = run
    for g, w in enumerate(POOL_WINDOWS):
        sl = slice(g * POOL_GW, (g + 1) * POOL_GW)
        cnt = _pool_counts(t0, n - HALO, w)
        outs.append(sums[w][HALO:, sl] / cnt - ext[HALO:, sl])
    return outs


def _pool_fwd(proj, pw, pb, ps, tm, name):
    seq = proj.shape[0]

    def body(u_ref, p_ref, pw_ref, pb_ref, ps_ref, o_ref):
        i = pl.program_id(0)
        prev = jnp.where(i == 0, 0.0, p_ref[...])
        ext = jnp.concatenate([prev, u_ref[...]], axis=0)
        diffs = _window_means(ext, i * tm)
        for g in range(4):
            sl = slice(g * POOL_GW, (g + 1) * POOL_GW)
            out = _dot(diffs[g].astype(BF16), pw_ref[g], 1, 0) + pb_ref[:, sl]
            o_ref[:, sl] = (out * ps_ref[:, sl]).astype(BF16)

    c0 = COL_U // 1024
    vec = pl.BlockSpec((1, D_POOL), lambda i: (0, 0))
    return pl.pallas_call(
        body, name=name, grid=(seq // tm,),
        in_specs=[pl.BlockSpec((tm, 1024), lambda i: (i, c0)),
                  pl.BlockSpec((HALO, 1024), lambda i: (jnp.maximum(i * (tm // HALO) - 1, 0), c0)),
                  pl.BlockSpec((4, POOL_GW, POOL_GW), lambda i: (0, 0, 0)), vec, vec],
        out_specs=pl.BlockSpec((tm, D_POOL), lambda i: (i, 0)),
        out_shape=jax.ShapeDtypeStruct((seq, D_POOL), BF16),
        compiler_params=_cp(("parallel",)))(proj, proj, pw, pb, ps)


def _pool_bwd(dyp, proj, pw, pb, ps, tm, name):
    seq = proj.shape[0]
    ni = seq // tm

    def body(d_ref, dn_ref, u_ref, p_ref, pw_ref, pb_ref, ps_ref, du_ref, dw_ref, st_ref):
        i = pl.program_id(0)
        prev = jnp.where(i == 0, 0.0, p_ref[...])
        ext = jnp.concatenate([prev, u_ref[...]], axis=0)
        diffs = _window_means(ext, i * tm)
        dnext = jnp.where(i == ni - 1, 0.0, dn_ref[...])
        dext = jnp.concatenate([d_ref[...], dnext], axis=0)
        n = tm + HALO
        b_rows, s_rows = [], []
        for g, w in enumerate(POOL_WINDOWS):
            sl = slice(g * POOL_GW, (g + 1) * POOL_GW)
            wg = pw_ref[g]
            dout = dext[:, sl] * ps_ref[:, sl]
            dcur = dout[:tm]
            pre = _dot(diffs[g].astype(BF16), wg, 1, 0) + pb_ref[:, sl]
            s_rows.append(jnp.sum(d_ref[:, sl] * pre, axis=0, keepdims=True))
            b_rows.append(jnp.sum(dcur, axis=0, keepdims=True))
            dwg = _dot(diffs[g].astype(BF16), dcur.astype(BF16), 0, 0)

            @pl.when(i == 0)
            def _():
                dw_ref[g] = dwg

            @pl.when(i != 0)
            def _():
                dw_ref[g] += dwg

            ddiff = _dot(dout.astype(BF16), wg, 1, 1)
            scaled = ddiff / _pool_counts(i * tm, n, w)
            run = scaled
            width = 1
            while width < w:
                run = run + pltpu.roll(run, n - width, 0)
                width *= 2
            du_ref[:, sl] = (run[:tm] - ddiff[:tm]).astype(BF16)
        rows = jnp.concatenate([jnp.concatenate(b_rows, axis=1), jnp.concatenate(s_rows, axis=1),
                                jnp.zeros((6, D_POOL), F32)], axis=0)

        @pl.when(i == 0)
        def _():
            st_ref[...] = rows

        @pl.when(i != 0)
        def _():
            st_ref[...] += rows

    c0 = COL_U // 1024
    vec = pl.BlockSpec((1, D_POOL), lambda i: (0, 0))
    last = seq // HALO - 1
    return pl.pallas_call(
        body, name=name, grid=(ni,),
        in_specs=[pl.BlockSpec((tm, D_POOL), lambda i: (i, 0)),
                  pl.BlockSpec((HALO, D_POOL), lambda i: (jnp.minimum((i + 1) * (tm // HALO), last), 0)),
                  pl.BlockSpec((tm, 1024), lambda i: (i, c0)),
                  pl.BlockSpec((HALO, 1024), lambda i: (jnp.maximum(i * (tm // HALO) - 1, 0), c0)),
                  pl.BlockSpec((4, POOL_GW, POOL_GW), lambda i: (0, 0, 0)), vec, vec],
        out_specs=[pl.BlockSpec((tm, D_POOL), lambda i: (i, 0)),
                   pl.BlockSpec((4, POOL_GW, POOL_GW), lambda i: (0, 0, 0)),
                   pl.BlockSpec((8, D_POOL), lambda i: (0, 0))],
        out_shape=[jax.ShapeDtypeStruct((seq, D_POOL), BF16),
                   jax.ShapeDtypeStruct((4, POOL_GW, POOL_GW), F32),
                   jax.ShapeDtypeStruct((8, D_POOL), F32)],
        compiler_params=_cp(("arbitrary",)))(dyp, dyp, proj, proj, pw, pb, ps)


def _mix_out(ys, yp, wout, x1, mod, wn_next, tm, name):
    seq = ys.shape[0]

    def body(ys_ref, yp_ref, w_ref, x_ref, mod_ref, wn_ref, xo_ref, m_ref, h_ref):
        mix = _dot(ys_ref[...], w_ref[0:D_SSD, :], 1, 0) + _dot(yp_ref[...], w_ref[D_SSD:2 * D_SSD, :], 1, 0)
        m_ref[...] = mix.astype(BF16)
        xo = x_ref[...] + mod_ref[5:6, :] * mix
        xo_ref[...] = xo
        h_ref[...] = _modulated(xo, wn_ref[...], mod_ref, 2)

    tok = pl.BlockSpec((tm, D), lambda i: (i, 0))
    return pl.pallas_call(
        body, name=name, grid=(seq // tm,),
        in_specs=[tok, tok, pl.BlockSpec((2 * D_SSD, D), lambda i: (0, 0)), tok,
                  pl.BlockSpec((9, D), lambda i: (0, 0)), pl.BlockSpec((1, D), lambda i: (0, 0))],
        out_specs=[tok, tok, tok],
        out_shape=[jax.ShapeDtypeStruct((seq, D), F32), jax.ShapeDtypeStruct((seq, D), BF16),
                   jax.ShapeDtypeStruct((seq, D), BF16)],
        compiler_params=_cp(("parallel",)))(ys, yp, wout, x1, mod, wn_next)


def _mix_bwd_dh(dz, dxbc, du, ddt, win_t, x1, dx2, mixb, wn, mod, tm, name):
    seq = x1.shape[0]

    def body(dz_ref, dx_ref, du_ref, ddt_ref, w_ref, x_ref, dxo_ref, m_ref, wn_ref, mod_ref, o_ref, st_ref, df_ref):
        dh = (_dot(dz_ref[...], w_ref[COL_Z:COL_Z + 1024, :], 1, 0)
              + _dot(dx_ref[...], w_ref[COL_XBC:COL_XBC + D_XBC, :], 1, 0)
              + _dot(du_ref[...], w_ref[COL_U:COL_U + 1024, :], 1, 0)
              + _dot(ddt_ref[...], w_ref[COL_DT:COL_DT + LANE, :], 1, 0))
        dx = _norm_bwd(dh, x_ref[...], dxo_ref[...], m_ref[...].astype(F32), wn_ref[...],
                       mod_ref[4:5, :], 1.0, st_ref, pl.program_id(0) == 0)
        o_ref[...] = dx
        df_ref[...] = (dx * (FFN_RES * mod_ref[2:3, :])).astype(BF16)

    tok = pl.BlockSpec((tm, D), lambda i: (i, 0))
    return pl.pallas_call(
        body, name=name, grid=(seq // tm,),
        in_specs=[tok, pl.BlockSpec((tm, D_XBC), lambda i: (i, 0)), tok,
                  pl.BlockSpec((tm, LANE), lambda i: (i, 0)),
                  pl.BlockSpec((D_IN_PAD, D), lambda i: (0, 0)),
                  tok, tok, tok,
                  pl.BlockSpec((1, D), lambda i: (0, 0)),
                  pl.BlockSpec((9, D), lambda i: (0, 0))],
        out_specs=[tok, pl.BlockSpec((8, D), lambda i: (0, 0)), tok],
        out_shape=[jax.ShapeDtypeStruct((seq, D), F32), jax.ShapeDtypeStruct((8, D), F32),
                   jax.ShapeDtypeStruct((seq, D), BF16)],
        compiler_params=_cp(("arbitrary",)))(dz, dxbc, du, ddt, win_t, x1, dx2, mixb, wn, mod)


def _mix_bwd_dycat(dmix, wout, tm, name):
    seq = dmix.shape[0]

    def body(d_ref, w_ref, a_ref, b_ref):
        dv = d_ref[...]
        a_ref[...] = _dot(dv, w_ref[0:D_SSD, :], 1, 1)
        b_ref[...] = _dot(dv, w_ref[D_SSD:2 * D_SSD, :], 1, 1)

    tok = pl.BlockSpec((tm, D), lambda i: (i, 0))
    return pl.pallas_call(
        body, name=name, grid=(seq // tm,),
        in_specs=[tok, pl.BlockSpec((2 * D_SSD, D), lambda i: (0, 0))],
        out_specs=[tok, tok],
        out_shape=[jax.ShapeDtypeStruct((seq, D), F32)] * 2,
        compiler_params=_cp(("parallel",)))(dmix, wout)


def _local_step(x, tgt, mod, wff1, later_weights, pool_w, vecs, tm):
    seq = x.shape[0]
    tk = min(seq, 512)
    in_proj_weights, out_proj_weights, ffn2_weights = later_weights
    h1 = _prenorm(x, vecs["ffn1_norm"], mod, 0, tm, "ffn1_prenorm")
    x1, h2, s1 = _ffn_forward(x, h1, wff1, mod, 0, vecs["mix_norm"], 1, tm, "ffn1")
    x1, win_t = in_proj_weights(x1)
    proj = _mm_nt(h2, win_t, tm, D_IN_PAD // 3, F32, "mix_in_proj")
    xbc = _conv_fwd(proj, vecs["conv_w"], vecs["conv_b"], tm, "mix_conv")
    y, hprev = _ssd_fwd(xbc, proj, vecs["ssd_par"], "mix_ssd")
    ys = _gate_norm_fwd(y, proj, vecs["ssd_norm_w"], tm, "mix_gate_norm")
    yp = _pool_fwd(proj, pool_w, vecs["pool_b"], vecs["pool_scale"], tm, "mix_pool")
    ys, wout = out_proj_weights(ys)
    x2, mixb, h3 = _mix_out(ys, yp, wout, x1, mod, vecs["ffn2_norm"], tm, "mix_out_proj")
    x2, wff2 = ffn2_weights(x2)
    x3, _, s3 = _ffn_forward(x2, h3, wff2, mod, 2, vecs["ffn2_norm"], None, tm, "ffn2")
    dx3, df3, st_loss = _loss_head(x3, vecs["final_norm"], tgt, mod, tm, "loss_head")

    dx2, dmix, st3, dw3 = _ffn_backward(dx3, df3, s3, wff2, vecs["ffn2_norm"], mod, 2, (5, 1.0), tm, "ffn2")
    dys, dyp = _mix_bwd_dycat(dmix, wout, tm, "mix_bwd_dycat")
    d_wout = (_mm_tn(ys, dmix, D_SSD, tk, "mix_dw_out_ssd"), _mm_tn(yp, dmix, D_POOL, tk, "mix_dw_out_pool"))
    du, d_pool_w, st_pool = _pool_bwd(dyp, proj, pool_w, vecs["pool_b"], vecs["pool_scale"], tm, "mix_pool_bwd")
    dy, dz, st_gn = _gate_norm_bwd(dys, y, proj, vecs["ssd_norm_w"], tm, "mix_gate_norm_bwd")
    dxbc_act, ddt, st_ssd = _ssd_bwd(dy, xbc, proj, vecs["ssd_par"], hprev, "mix_ssd_bwd")
    dxbc, st_conv = _conv_bwd(dxbc_act, proj, vecs["conv_w"], vecs["conv_b"], tm, "mix_conv_bwd")
    dx1, st2, df1 = _mix_bwd_dh(dz, dxbc, du, ddt, win_t, x1, dx2, mixb, vecs["mix_norm"], mod, min(tm, 256), "mix_bwd_dh")
    d_win = (_mm_tn(dz, h2, 1024, tk, "mix_dw_in_z"), _mm_tn(dxbc, h2, 1024, tk, "mix_dw_in_xbc"),
             _mm_tn(du, h2, 1024, tk, "mix_dw_in_u"), _mm_tn(ddt, h2, LANE, tk, "mix_dw_in_dt"))
    dx0, _, st1, dw1 = _ffn_backward(dx1, df1, s1, wff1, vecs["ffn1_norm"], mod, 0, None, tm, "ffn1")
    stats = dict(ffn1=st1, mix=st2, ffn2=st3, loss=st_loss, pool=st_pool, gn=st_gn, ssd=st_ssd, conv=st_conv)
    return dx0, stats, dw1, dw3, d_win, d_wout, d_pool_w


HBM_SPEC = pl.BlockSpec(memory_space=pltpu.HBM)


def _mesh_pos():
    return lax.axis_index("x"), lax.axis_index("y"), lax.axis_index("c")


def _other_chips(x, y):
    return [(1 - x, y), (x, 1 - y), (1 - x, 1 - y)]


def _all_gather(src, regions, name):
    total, cols = src.shape
    assert sum(r for _, r in regions) == total
    body = _all_gather_body(regions, total, False)
    return pl.pallas_call(
        body, name=name,
        out_shape=jax.ShapeDtypeStruct((N_DEV * total, cols), src.dtype),
        in_specs=[HBM_SPEC], out_specs=HBM_SPEC,
        scratch_shapes=[pltpu.SemaphoreType.DMA((7,)), pltpu.SemaphoreType.DMA((7,)), pltpu.SemaphoreType.DMA],
    )(src)


def _all_gather_async(src, regions, name, collective_id):
    total, cols = src.shape
    assert sum(r for _, r in regions) == total
    return pl.kernel(
        _all_gather_body(regions, total, True), name=name,
        out_type=jax.ShapeDtypeStruct((N_DEV * total, cols), src.dtype),
        mesh=plsc.ScalarSubcoreMesh(axis_name="seq", num_cores=1),
        scratch_types=(pltpu.SemaphoreType.DMA((7,)), pltpu.SemaphoreType.DMA((7,)), pltpu.SemaphoreType.DMA),
        compiler_params=pltpu.CompilerParams(collective_id=collective_id))(src)


def _all_gather_body(regions, total, handshake):
    def body(src_ref, out_ref, send_sems, recv_sems, local_sem):
        x, y, c = _mesh_pos()
        me, sibling = (x, y, c), (x, y, 1 - c)
        chips = _other_chips(x, y)
        if handshake:
            barrier = pltpu.get_barrier_semaphore()
            for peer in [sibling] + [(*chip, c) for chip in chips]:
                pl.semaphore_signal(barrier, inc=1, device_id=peer, device_id_type=MESH)
            pl.semaphore_wait(barrier, 4)

        def rows_of(dev, off, rows):
            start = pl.multiple_of(N_DEV * off + (4 * dev[0] + 2 * dev[1] + dev[2]) * rows, 8)
            return out_ref.at[pl.ds(start, rows), :]

        def copies(k, block, to, from_src):
            out = []
            for off, rows in regions:
                dst = rows_of(block, off, rows)
                out.append(pltpu.make_async_remote_copy(
                    src_ref=src_ref.at[pl.ds(off, rows), :] if from_src else dst, dst_ref=dst,
                    send_sem=send_sems.at[k], recv_sem=recv_sems.at[k], device_id=to, device_id_type=MESH))
            return out

        def drain(k):
            whole = out_ref.at[pl.ds(0, total), :]
            return pltpu.make_async_remote_copy(src_ref=whole, dst_ref=whole, send_sem=send_sems.at[k],
                                                recv_sem=recv_sems.at[k], device_id=me, device_id_type=MESH)

        for off, rows in regions:
            pltpu.make_async_copy(src_ref.at[pl.ds(off, rows), :], rows_of(me, off, rows), local_sem).start()
        first = copies(0, me, sibling, True)
        for j, chip in enumerate(chips):
            first += copies(1 + j, me, (*chip, c), True)
        for cp in first:
            cp.start()
        for j, chip in enumerate(chips):
            drain(1 + j).wait_recv()
            for cp in copies(4 + j, (*chip, c), sibling, False):
                cp.start()
        drain(0).wait_recv()
        for j in range(3):
            drain(4 + j).wait_recv()
        for k in range(7):
            drain(k).wait_send()
        pltpu.make_async_copy(src_ref, out_ref.at[pl.ds(0, total), :], local_sem).wait()

    return body


def _rs_pair(grads, total, name, collective_id):
    cols = grads[0][0].shape[1]
    sent = sum(rows for _, _, rows in grads)
    n = len(grads)

    def body(*refs):
        g_refs, recv_ref, send_sem, recv_sem = refs[:n], refs[n], refs[n + 1], refs[n + 2]
        x, y, c = _mesh_pos()
        sibling = (x, y, 1 - c)
        barrier = pltpu.get_barrier_semaphore()
        pl.semaphore_signal(barrier, inc=1, device_id=sibling, device_id_type=MESH)
        pl.semaphore_wait(barrier, 1)
        for q in range(4):
            for g_ref, (_, off, rows) in zip(g_refs, grads):
                theirs = g_ref.at[pl.ds(pl.multiple_of((2 * q + 1 - c) * rows, 8), rows), :]
                pltpu.make_async_remote_copy(
                    src_ref=theirs, dst_ref=recv_ref.at[q, pl.ds(off, rows), :], send_sem=send_sem, recv_sem=recv_sem,
                    device_id=sibling, device_id_type=MESH).start()
        everything = recv_ref.at[:, pl.ds(0, sent), :]
        whole = pltpu.make_async_remote_copy(src_ref=everything, dst_ref=everything, send_sem=send_sem,
                                             recv_sem=recv_sem, device_id=sibling, device_id_type=MESH)
        whole.wait_send()
        whole.wait_recv()

    return pl.kernel(
        body, name=name, out_type=jax.ShapeDtypeStruct((4, total, cols), F32),
        mesh=plsc.ScalarSubcoreMesh(axis_name="seq", num_cores=1),
        scratch_types=(pltpu.SemaphoreType.DMA, pltpu.SemaphoreType.DMA),
        compiler_params=pltpu.CompilerParams(collective_id=collective_id))(*[g for g, _, _ in grads])


def _pair_sum(grads, from_sibling, pos, name):
    cols = grads[0][0].shape[1]
    n = len(grads)

    def body(pos_ref, *refs):
        for i in range(n):
            s = refs[i][...] + refs[n + i][...]
            refs[2 * n + 2 * i][...] = s
            refs[2 * n + 2 * i + 1][...] = s.astype(BF16)

    in_specs = [pl.BlockSpec((None, None, rows, cols), lambda q, pos_ref: (q, pos_ref[0], 0, 0)) for _, _, rows in grads]
    in_specs += [pl.BlockSpec((None, rows, cols), lambda q, pos_ref, blk=off // rows: (q, blk, 0)) for _, off, rows in grads]
    out_specs, out_shape = [], []
    for _, _, rows in grads:
        out_specs += [pl.BlockSpec((None, rows, cols), lambda q, pos_ref: (q, 0, 0))] * 2
        out_shape += [jax.ShapeDtypeStruct((4, rows, cols), F32), jax.ShapeDtypeStruct((4, rows, cols), BF16)]
    outs = pl.pallas_call(
        body, name=name,
        grid_spec=pltpu.PrefetchScalarGridSpec(num_scalar_prefetch=1, grid=(4,), in_specs=in_specs, out_specs=out_specs),
        out_shape=out_shape,
        compiler_params=_cp(("parallel",)))(pos, *[g.reshape(4, 2, rows, cols) for g, _, rows in grads],
                                            *[from_sibling] * n)
    return [(outs[2 * i], outs[2 * i + 1]) for i in range(n)]


def _rs_chips(parts, total, name, collective_id):
    cols = parts[0][0].shape[2]
    sent = sum(rows for _, _, rows in parts)
    n = len(parts)

    def body(*refs):
        p_refs, out_ref, send_sems, recv_sems = refs[:n], refs[n], refs[n + 1], refs[n + 2]
        x, y, c = _mesh_pos()
        chips = _other_chips(x, y)
        barrier = pltpu.get_barrier_semaphore()
        for chip in chips:
            pl.semaphore_signal(barrier, inc=1, device_id=(*chip, c), device_id_type=MESH)
        pl.semaphore_wait(barrier, 3)
        for j, chip in enumerate(chips):
            q = 2 * chip[0] + chip[1]
            for p_ref, (_, off, rows) in zip(p_refs, parts):
                pltpu.make_async_remote_copy(
                    src_ref=p_ref.at[q], dst_ref=out_ref.at[j, pl.ds(off, rows), :], send_sem=send_sems.at[j],
                    recv_sem=recv_sems.at[j], device_id=(*chip, c), device_id_type=MESH).start()
        for j, chip in enumerate(chips):
            everything = out_ref.at[j, pl.ds(0, sent), :]
            whole = pltpu.make_async_remote_copy(src_ref=everything, dst_ref=everything, send_sem=send_sems.at[j],
                                                 recv_sem=recv_sems.at[j], device_id=(*chip, c), device_id_type=MESH)
            whole.wait_recv()
            whole.wait_send()

    return pl.kernel(
        body, name=name, out_type=jax.ShapeDtypeStruct((3, total, cols), BF16),
        mesh=plsc.ScalarSubcoreMesh(axis_name="seq", num_cores=1),
        scratch_types=(pltpu.SemaphoreType.DMA((3,)), pltpu.SemaphoreType.DMA((3,))),
        compiler_params=pltpu.CompilerParams(collective_id=collective_id))(*[p for p, _, _ in parts])


def _chip_sum(p, from_chips, off, rows, pos, name):
    cols = p.shape[2]

    def body(pos_ref, p_ref, r_ref, o_ref):
        acc = p_ref[...]
        for j in range(3):
            acc = acc + r_ref[j].astype(F32)
        o_ref[...] = acc

    return pl.pallas_call(
        body, name=name,
        grid_spec=pltpu.PrefetchScalarGridSpec(
            num_scalar_prefetch=1, grid=(1,),
            in_specs=[pl.BlockSpec((None, rows, cols), lambda i, pos_ref: (pos_ref[1], 0, 0)),
                      pl.BlockSpec((3, rows, cols), lambda i, pos_ref: (0, off // rows, 0))],
            out_specs=pl.BlockSpec((rows, cols), lambda i, pos_ref: (0, 0))),
        out_shape=jax.ShapeDtypeStruct((rows, cols), F32),
        compiler_params=_cp(("arbitrary",)))(pos, p, from_chips)


def _chip_sum_adamw(p, from_chips, off, rows, pos, w, m, v, tr, name):
    cols = p.shape[2]
    c1 = 1.0 - ADAM_B1 ** ADAM_STEP
    c2 = 1.0 - ADAM_B2 ** ADAM_STEP

    def body(pos_ref, p_ref, r_ref, w_ref, m_ref, v_ref, g_ref, d_ref, mo_ref, vo_ref):
        gv = p_ref[...]
        for j in range(3):
            gv = gv + r_ref[j].astype(F32)
        g_ref[...] = gv
        mn = ADAM_B1 * m_ref[...] + (1.0 - ADAM_B1) * gv
        vn = ADAM_B2 * v_ref[...] + (1.0 - ADAM_B2) * (gv * gv)
        mo_ref[...] = mn
        vo_ref[...] = vn
        d_ref[...] = -ADAM_LR * ((mn / c1) / (jnp.sqrt(vn / c2) + ADAM_EPS) + ADAM_WD * w_ref[...])

    tile = pl.BlockSpec((tr, cols), lambda i, pos_ref: (i, 0))
    shape = jax.ShapeDtypeStruct((rows, cols), F32)
    return pl.pallas_call(
        body, name=name,
        grid_spec=pltpu.PrefetchScalarGridSpec(
            num_scalar_prefetch=1, grid=(rows // tr,),
            in_specs=[pl.BlockSpec((None, tr, cols), lambda i, pos_ref: (pos_ref[1], i, 0)),
                      pl.BlockSpec((3, tr, cols), lambda i, pos_ref: (0, off // tr + i, 0)),
                      tile, tile, tile],
            out_specs=[tile] * 4),
        out_shape=[shape] * 4,
        compiler_params=_cp(("parallel",)))(pos, p, from_chips, w, m, v)


def _row_tile(rows, cap):
    t = min(rows, cap)
    while rows % t or t % 8:
        t -= 8
    return t


def _ada_mod(c_all, w, b, name):
    n = w.shape[1]

    def body(c_ref, w_ref, b_ref, o_ref):
        cv = c_ref[...]
        o_ref[...] = _exact_dot(cv * _sigmoid(cv), w_ref[...]) + b_ref[...]

    return pl.pallas_call(body, name=name, out_shape=jax.ShapeDtypeStruct((N_DEV, n), F32),
                          compiler_params=pltpu.CompilerParams(vmem_limit_bytes=VMEM_LIMIT))(c_all, w, b)


def _ada_grad(c_all, dmod, name):
    n = dmod.shape[1]

    def body(c_ref, d_ref, o_ref):
        cv = c_ref[...]
        o_ref[...] = _dot(cv * _sigmoid(cv), d_ref[...], 0, 0, lax.Precision.HIGHEST)

    return pl.pallas_call(body, name=name, out_shape=jax.ShapeDtypeStruct((D, n), F32),
                          compiler_params=pltpu.CompilerParams(vmem_limit_bytes=VMEM_LIMIT))(c_all, dmod)


def _adamw(w, g, m, v, name):
    rows, cols = w.shape
    tr = _row_tile(rows, 256) if rows % 8 == 0 else rows
    c1 = 1.0 - ADAM_B1 ** ADAM_STEP
    c2 = 1.0 - ADAM_B2 ** ADAM_STEP

    def body(w_ref, g_ref, m_ref, v_ref, d_ref, mo_ref, vo_ref):
        gv = g_ref[...]
        mn = ADAM_B1 * m_ref[...] + (1.0 - ADAM_B1) * gv
        vn = ADAM_B2 * v_ref[...] + (1.0 - ADAM_B2) * (gv * gv)
        mo_ref[...] = mn
        vo_ref[...] = vn
        d_ref[...] = -ADAM_LR * ((mn / c1) / (jnp.sqrt(vn / c2) + ADAM_EPS) + ADAM_WD * w_ref[...])

    spec = pl.BlockSpec((tr, cols), lambda i: (i, 0))
    shape = jax.ShapeDtypeStruct((rows, cols), F32)
    return pl.pallas_call(body, name=name, grid=(rows // tr,), in_specs=[spec] * 4, out_specs=[spec] * 3,
                          out_shape=[shape] * 3, compiler_params=_cp(("parallel",)))(w, g, m, v)


def _sum8_loss(v, loss_row, name):
    rows = v.shape[0] // N_DEV

    def body(v_ref, o_ref, l_ref):
        acc = v_ref[0:rows, :]
        for k in range(1, N_DEV):
            acc = acc + v_ref[k * rows:(k + 1) * rows, :]
        o_ref[...] = acc
        part = jnp.sum(acc[loss_row:loss_row + 8, :], axis=0, keepdims=True)
        l_ref[...] = jnp.broadcast_to(jnp.sum(part, axis=1, keepdims=True), (8, LANE))

    return pl.pallas_call(body, name=name,
                          out_shape=[jax.ShapeDtypeStruct((rows, LANE), F32), jax.ShapeDtypeStruct((8, LANE), F32)],
                          compiler_params=pltpu.CompilerParams(vmem_limit_bytes=VMEM_LIMIT))(v)


WEIGHT_NAMES = ("w_ada", "b_ada", "ffn1_norm", "ffn1_w_gate", "ffn1_w_up", "ffn1_w_down", "mix_norm", "w_in",
                "conv_w", "conv_b", "dt_bias", "a_log", "d_skip", "ssd_norm_w", "pool_w", "pool_b", "pool_scale",
                "w_out", "ffn2_norm", "ffn2_w_gate", "ffn2_w_up", "ffn2_w_down", "final_norm")

FF_SHARD = FF // N_DEV
IN_SHARD = D_IN // N_DEV
IN_SHARD_PAD = 528
OUT_SHARD = 2 * D_SSD // N_DEV
ADA_SHARD = 9 * D // N_DEV
POOL_SHARD_ROWS = 4 * 32 * POOL_GW // D
PACK = dict(gate1=(0, FF_SHARD), up1=(352, FF_SHARD), down1=(704, FF_SHARD), gate2=(1056, FF_SHARD),
            up2=(1408, FF_SHARD), down2=(1760, FF_SHARD), w_out=(2112, OUT_SHARD), w_in=(2368, IN_SHARD_PAD),
            pool_w=(2896, POOL_SHARD_ROWS))
PACK_W_ROWS = 2896
GPACK = dict(w_in=(0, IN_SHARD_PAD), w_out=(768, OUT_SHARD), pool_w=(1024, POOL_SHARD_ROWS),
             gate1=(0, FF_SHARD), up1=(352, FF_SHARD), down1=(704, FF_SHARD),
             gate2=(0, FF_SHARD), up2=(352, FF_SHARD), down2=(704, FF_SHARD))
GROUP_ROWS = 1056

SMALL_ROWS = dict(dmod=(0, 72), ffn1_norm=(72, 8), mix_norm=(80, 8), ffn2_norm=(88, 8), final_norm=(96, 8),
                  ssd_norm_w=(104, 8), pool_scale=(112, 8), conv_b=(120, 16), conv_w=(136, 64), pool_b=(200, 8),
                  ssd=(208, 3), loss=(216, 8))
SMALL_TOTAL = 224


def _rows128(v, rows):
    flat = v.reshape(-1)
    return jnp.pad(flat, (0, rows * LANE - flat.shape[0])).reshape(rows, LANE)


def _pad_lanes(v):
    return jnp.pad(v.reshape(-1), (0, LANE - v.size))


def kernel(x, c, w_ada, b_ada, ffn1_norm, ffn1_w_gate, ffn1_w_up, ffn1_w_down, mix_norm, w_in, conv_w, conv_b, dt_bias, a_log, d_skip, ssd_norm_w, pool_w, pool_b, pool_scale, w_out, ffn2_norm, ffn2_w_gate, ffn2_w_up, ffn2_w_down, final_norm, loss_target, m_w_ada, m_b_ada, m_ffn1_norm, m_ffn1_w_gate, m_ffn1_w_up, m_ffn1_w_down, m_mix_norm, m_w_in, m_conv_w, m_conv_b, m_dt_bias, m_a_log, m_d_skip, m_ssd_norm_w, m_pool_w, m_pool_b, m_pool_scale, m_w_out, m_ffn2_norm, m_ffn2_w_gate, m_ffn2_w_up, m_ffn2_w_down, m_final_norm, v_w_ada, v_b_ada, v_ffn1_norm, v_ffn1_w_gate, v_ffn1_w_up, v_ffn1_w_down, v_mix_norm, v_w_in, v_conv_w, v_conv_b, v_dt_bias, v_a_log, v_d_skip, v_ssd_norm_w, v_pool_w, v_pool_b, v_pool_scale, v_w_out, v_ffn2_norm, v_ffn2_w_gate, v_ffn2_w_up, v_ffn2_w_down, v_final_norm):
    given = dict(locals())
    w = {n: given[n] for n in WEIGHT_NAMES}
    m = {n: given["m_" + n] for n in WEIGHT_NAMES}
    v = {n: given["v_" + n] for n in WEIGHT_NAMES}
    mx, my, mc = _mesh_pos()
    me = 4 * mx + 2 * my + mc

    small = jnp.concatenate([c.reshape(-1), conv_w.reshape(-1), pool_b.reshape(-1), pool_w.reshape(-1)])
    small_rows = 280
    gs = _all_gather(_rows128(small, small_rows), [(0, small_rows)], "ag_small").reshape(N_DEV, small_rows * LANE)
    c_all = gs[:, 0:D]
    conv_w_full = gs[:, 1024:2048].reshape(N_DEV, 4, 256).transpose(1, 0, 2).reshape(4, D_XBC)
    pool_b_full = gs[:, 2048:2176].reshape(N_DEV, 4, 32).transpose(1, 0, 2).reshape(1, D_POOL)
    pool_w_full = gs[:, 2176:2176 + 32768].reshape(N_DEV, 4, 32, POOL_GW).transpose(1, 0, 2, 3).reshape(4, POOL_GW, POOL_GW).astype(BF16)

    b_ada_cols = lax.dynamic_slice(b_ada, (0, me * ADA_SHARD), (1, ADA_SHARD))
    mod_part = _ada_mod(c_all, w_ada[0], b_ada_cols, "ada_mod")
    mod_all = _all_gather(mod_part, [(0, N_DEV)], "ag_mod").reshape(N_DEV, N_DEV, ADA_SHARD)
    mod = lax.dynamic_index_in_dim(mod_all, me, axis=1, keepdims=False).reshape(9, D)

    win_t_shard = jnp.pad(w_in[0].T, ((0, IN_SHARD_PAD - IN_SHARD), (0, 0)))
    packs = (jnp.concatenate([ffn1_w_gate[0].T, ffn1_w_up[0].T], axis=0).astype(BF16),
             ffn1_w_down[0].astype(BF16),
             win_t_shard.astype(BF16),
             w_out[0].astype(BF16),
             jnp.concatenate([ffn2_w_gate[0].T, ffn2_w_up[0].T, ffn2_w_down[0]], axis=0).astype(BF16))
    packs, _ = lax.optimization_barrier((packs, c_all))
    ffn_regions = [(0, FF_SHARD), (FF_SHARD, FF_SHARD), (2 * FF_SHARD, FF_SHARD)]
    full_a = _all_gather_async(packs[0], ffn_regions[0:2], "ag_weights_ffn1_in", 1)
    full_d = _all_gather_async(packs[1], ffn_regions[0:1], "ag_weights_ffn1_out", 2)
    full_in = _all_gather_async(packs[2], [(0, IN_SHARD_PAD)], "ag_weights_in_proj", 9)
    full_out = _all_gather_async(packs[3], [(0, OUT_SHARD)], "ag_weights_out_proj", 10)
    full_2 = _all_gather_async(packs[4], ffn_regions, "ag_weights_ffn2", 11)

    def in_proj_weights(x1):
        w_i, x1 = lax.optimization_barrier((full_in, x1))
        win_g = w_i.reshape(N_DEV, IN_SHARD_PAD, D)[:, :IN_SHARD].reshape(D_IN, D)
        win_t = jnp.concatenate([win_g[0:1024], win_g[1024:3072], win_g[3088:4112], win_g[3072:3088],
                                 jnp.zeros((D_IN_PAD - D_IN, D), BF16)], axis=0)
        return x1, win_t

    def out_proj_weights(ys):
        w_o, ys = lax.optimization_barrier((full_out, ys))
        return ys, w_o

    def ffn2_weights(x2):
        w_2, x2 = lax.optimization_barrier((full_2, x2))
        return x2, (w_2, w_2, 2)

    later_weights = (in_proj_weights, out_proj_weights, ffn2_weights)

    vecs = dict(ffn1_norm=ffn1_norm, mix_norm=mix_norm, ffn2_norm=ffn2_norm, final_norm=final_norm.reshape(1, D),
                conv_w=conv_w_full, conv_b=conv_b, ssd_norm_w=ssd_norm_w, pool_b=pool_b_full, pool_scale=pool_scale,
                ssd_par=jnp.concatenate([_pad_lanes(dt_bias)[None], _pad_lanes(a_log)[None], _pad_lanes(d_skip)[None],
                                         jnp.zeros((5, LANE), F32)], axis=0))
    dx0, st, dw1, dw3, d_win, d_wout, d_pool_w = _local_step(
        x[0], loss_target[0], mod, (full_a, full_d, 0), later_weights, pool_w_full, vecs, min(512, x.shape[1]))

    dwin = jnp.concatenate([d_win[0], d_win[1], d_win[3][0:16], d_win[2]], axis=0)
    dwin = jnp.pad(dwin.reshape(N_DEV, IN_SHARD, D), ((0, 0), (0, IN_SHARD_PAD - IN_SHARD), (0, 0))).reshape(N_DEV * IN_SHARD_PAD, D)
    dwout = jnp.concatenate(d_wout, axis=0)
    dpool = d_pool_w.reshape(4, N_DEV, 32, POOL_GW).transpose(1, 0, 2, 3).reshape(N_DEV * POOL_SHARD_ROWS, D)
    pos = jnp.stack([mc, 2 * mx + my]).astype(jnp.int32)
    by_key = dict(zip(("gate1", "up1", "down1", "gate2", "up2", "down2", "w_out", "w_in", "pool_w"),
                      (*dw1, *dw3, dwout, dwin, dpool)))
    reduced = {}
    for tag, keys, cid in (("ffn2", ("gate2", "up2", "down2"), 3), ("mix", ("w_in", "w_out", "pool_w"), 5),
                           ("ffn1", ("gate1", "up1", "down1"), 7)):
        grads = [(by_key[k], *GPACK[k]) for k in keys]
        from_sibling = _rs_pair(grads, GROUP_ROWS, f"rs_pair_{tag}", cid)
        pairs = dict(zip(keys, _pair_sum(grads, from_sibling, pos, f"rs_pair_sum_{tag}")))
        from_chips = _rs_chips([(pairs[k][1], *GPACK[k]) for k in keys], GROUP_ROWS, f"rs_chips_{tag}", cid + 1)
        for k in keys:
            reduced[k] = (pairs[k][0], from_chips)

    delta, new_m, new_v, shard_grad = {}, {}, {}, {}
    fused = dict(gate1=("ffn1_w_gate", True, 176), up1=("ffn1_w_up", True, 176), down1=("ffn1_w_down", False, 176),
                 gate2=("ffn2_w_gate", True, 176), up2=("ffn2_w_up", True, 176), down2=("ffn2_w_down", False, 176),
                 w_out=("w_out", False, 128), pool_w=("pool_w", False, POOL_SHARD_ROWS))
    for k, (n, is_transposed, tr) in fused.items():
        shp = w[n].shape
        rows = GPACK[k][1]
        view = (lambda t: t[0].T) if is_transposed else (lambda t: t.reshape(rows, D))
        back = (lambda t: t.T[None]) if is_transposed else (lambda t: t.reshape(shp))
        g_, d_, m_, v_ = _chip_sum_adamw(reduced[k][0], reduced[k][1], *GPACK[k], pos, view(w[n]), view(m[n]), view(v[n]),
                                         tr, f"adamw_{n}")
        shard_grad[n], delta[n], new_m[n], new_v[n] = back(g_), back(d_), back(m_), back(v_)
    g_win_t = _chip_sum(reduced["w_in"][0], reduced["w_in"][1], *GPACK["w_in"], pos, "rs_chip_sum_w_in")[0:IN_SHARD]

    dmod = jnp.concatenate([st["ffn1"][0:3], st["mix"][0:3], st["ffn2"][0:3]], axis=0)
    sg = jnp.concatenate([
        dmod.reshape(-1), st["ffn1"][3], st["mix"][3], st["ffn2"][3], st["loss"][0], st["gn"][0], st["pool"][1],
        st["conv"][4], st["conv"][0:4].reshape(-1), st["pool"][0], st["ssd"][0:3].reshape(-1),
        jnp.zeros((5 * LANE,), F32), st["loss"][1]])
    sg_all = _all_gather(sg.reshape(SMALL_TOTAL, LANE), [(0, SMALL_TOTAL)], "ag_small_grads")
    tot, loss_b = _sum8_loss(sg_all, SMALL_ROWS["loss"][0], "small_sum")
    loss = loss_b[0, 0]
    dmod_all = sg_all.reshape(N_DEV, SMALL_TOTAL * LANE)[:, 0:9 * D]
    g_w_ada = _ada_grad(c_all, lax.dynamic_slice(dmod_all, (0, me * ADA_SHARD), (N_DEV, ADA_SHARD)), "ada_grad")

    def tot_rows(k):
        off, n = SMALL_ROWS[k]
        return tot[off:off + n].reshape(-1)

    g_conv_w = lax.dynamic_slice(tot_rows("conv_w").reshape(4, D_XBC), (0, me * 256), (4, 256))
    g_pool_b = lax.dynamic_slice(tot_rows("pool_b").reshape(4, POOL_GW), (0, me * 32), (4, 32))
    g_ssd = tot_rows("ssd").reshape(3, LANE)
    grad = {
        "w_ada": g_w_ada[None], "b_ada": tot_rows("dmod").reshape(1, 9 * D),
        "ffn1_norm": tot_rows("ffn1_norm")[None], "mix_norm": tot_rows("mix_norm")[None],
        "ffn2_norm": tot_rows("ffn2_norm")[None], "final_norm": tot_rows("final_norm"),
        "ssd_norm_w": tot_rows("ssd_norm_w")[None], "pool_scale": tot_rows("pool_scale")[None],
        "conv_b": tot_rows("conv_b")[None], "conv_w": g_conv_w[None], "pool_b": g_pool_b[None],
        "dt_bias": g_ssd[0:1, 0:N_HEADS], "a_log": g_ssd[1:2, 0:N_HEADS], "d_skip": g_ssd[2:3, 0:N_HEADS],
        "w_in": g_win_t.T[None], **shard_grad,
    }

    d_, m_, v_ = _adamw(w_ada[0], g_w_ada, m_w_ada[0], v_w_ada[0], "adamw_w_ada")
    delta["w_ada"], new_m["w_ada"], new_v["w_ada"] = d_[None], m_[None], v_[None]
    d_, m_, v_ = _adamw(w_in[0].T, g_win_t, m_w_in[0].T, v_w_in[0].T, "adamw_w_in")
    delta["w_in"], new_m["w_in"], new_v["w_in"] = d_.T[None], m_.T[None], v_.T[None]
    big = ("w_ada", "w_in") + tuple(n for n, _, _ in fused.values())
    small_names = [n for n in WEIGHT_NAMES if n not in big]
    sizes = [LANE if w[n].size < LANE else w[n].size for n in small_names]
    small_rows_adam = -(-sum(sizes) // (8 * LANE)) * 8

    def pack_small(t):
        return _rows128(jnp.concatenate([_pad_lanes(t[n]) if t[n].size < LANE else t[n].reshape(-1) for n in small_names]),
                        small_rows_adam)

    d_s, m_s, v_s = _adamw(pack_small(w), pack_small(grad), pack_small(m), pack_small(v), "adamw_small")
    off = 0
    for n, size in zip(small_names, sizes):
        for res, packed in ((delta, d_s), (new_m, m_s), (new_v, v_s)):
            res[n] = packed.reshape(-1)[off:off + w[n].size].reshape(w[n].shape)
        off += size

    return (loss, dx0[None], *[grad[n] for n in WEIGHT_NAMES], *[delta[n] for n in WEIGHT_NAMES],
            *[new_m[n] for n in WEIGHT_NAMES], *[new_v[n] for n in WEIGHT_NAMES])
```

```python
import functools
import math

import jax
import jax.numpy as jnp
from jax import lax
from jax.experimental import pallas as pl
from jax.experimental.pallas import tpu as pltpu
from jax.experimental.pallas import tpu_sc as plsc

F32 = jnp.float32
BF16 = jnp.bfloat16
MESH = pl.DeviceIdType.MESH

N_DEV = 8
D = 1024
FF = 2816
D_SSD = 1024
N_HEADS = 16
HEAD_DIM = 64
N_GROUPS = 4
N_STATE = 128
CHUNK = 128
GROUP_W = D_SSD // N_GROUPS
D_XBC = D_SSD + 2 * N_GROUPS * N_STATE
D_POOL = 1024
POOL_WINDOWS = (2, 4, 8, 16)
POOL_GW = 256
D_IN = 4112
D_IN_PAD = 4224
COL_Z, COL_XBC, COL_U, COL_DT = 0, 1024, 3072, 4096
EPS = 1e-6
FFN_RES = 0.5
LANE = 128
HALO = 16

ADAM_LR, ADAM_B1, ADAM_B2, ADAM_EPS, ADAM_WD, ADAM_STEP = 0.001, 0.9, 0.999, 1e-08, 0.01, 10

VMEM_LIMIT = 56 << 20


def _cp(sem):
    return pltpu.CompilerParams(dimension_semantics=sem, vmem_limit_bytes=VMEM_LIMIT)


def _dot(a, b, ca, cb, prec=None):
    return lax.dot_general(a, b, (((ca,), (cb,)), ((), ())), precision=prec,
                           preferred_element_type=F32)


def _exact_dot(a, b):
    return _dot(a, b, 1, 0, lax.Precision.HIGHEST)


def _sigmoid(v):
    return 1.0 / (1.0 + jnp.exp(-v))


def _silu_grad(v, sg):
    return sg * (1.0 + v * (1.0 - sg))


def _mm_nt(a, bt, tm, tn, out_dtype, name):
    m, k = a.shape
    n = bt.shape[0]

    def body(a_ref, b_ref, o_ref):
        o_ref[...] = _dot(a_ref[...], b_ref[...], 1, 1).astype(out_dtype)

    return pl.pallas_call(
        body, name=name, grid=(n // tn, m // tm),
        in_specs=[pl.BlockSpec((tm, k), lambda j, i: (i, 0)),
                  pl.BlockSpec((tn, k), lambda j, i: (j, 0))],
        out_specs=pl.BlockSpec((tm, tn), lambda j, i: (i, j)),
        out_shape=jax.ShapeDtypeStruct((m, n), out_dtype),
        compiler_params=_cp(("parallel", "parallel")))(a, bt)


def _mm_tn(a, b, tm, tk, name):
    kk, m = a.shape
    n = b.shape[1]
    nk = kk // tk
    if nk == 1:
        def whole(a_ref, b_ref, o_ref):
            o_ref[...] = _dot(a_ref[...], b_ref[...], 0, 0)

        return pl.pallas_call(
            whole, name=name, grid=(m // tm,),
            in_specs=[pl.BlockSpec((kk, tm), lambda i: (0, i)),
                      pl.BlockSpec((kk, n), lambda i: (0, 0))],
            out_specs=pl.BlockSpec((tm, n), lambda i: (i, 0)),
            out_shape=jax.ShapeDtypeStruct((m, n), F32),
            compiler_params=_cp(("parallel",)))(a, b)

    def body(a_ref, b_ref, o_ref, acc):
        k = pl.program_id(1)

        @pl.when(k == 0)
        def _():
            acc[...] = jnp.zeros_like(acc)

        acc[...] += _dot(a_ref[...], b_ref[...], 0, 0)

        @pl.when(k == nk - 1)
        def _():
            o_ref[...] = acc[...]

    return pl.pallas_call(
        body, name=name, grid=(m // tm, nk),
        in_specs=[pl.BlockSpec((tk, tm), lambda i, k: (k, i)),
                  pl.BlockSpec((tk, n), lambda i, k: (k, 0))],
        out_specs=pl.BlockSpec((tm, n), lambda i, k: (i, 0)),
        out_shape=jax.ShapeDtypeStruct((m, n), F32),
        scratch_shapes=[pltpu.VMEM((tm, n), F32)],
        compiler_params=_cp(("parallel", "arbitrary")))(a, b)


def _modulated(xv, wn, mod_ref, k):
    r = lax.rsqrt(jnp.mean(xv * xv, axis=-1, keepdims=True) + EPS)
    hn = xv * r * wn
    return (hn * (1.0 + mod_ref[3 * k + 1:3 * k + 2, :]) + mod_ref[3 * k:3 * k + 1, :]).astype(BF16)


def _prenorm(x, wn, mod, k, tm, name):
    seq = x.shape[0]

    def body(x_ref, wn_ref, mod_ref, h_ref):
        h_ref[...] = _modulated(x_ref[...], wn_ref[...], mod_ref, k)

    return pl.pallas_call(
        body, name=name, grid=(seq // tm,),
        in_specs=[pl.BlockSpec((tm, D), lambda i: (i, 0)),
                  pl.BlockSpec((1, D), lambda i: (0, 0)),
                  pl.BlockSpec((9, D), lambda i: (0, 0))],
        out_specs=pl.BlockSpec((tm, D), lambda i: (i, 0)),
        out_shape=jax.ShapeDtypeStruct((seq, D), BF16),
        compiler_params=_cp(("parallel",)))(x, wn, mod)


def _norm_bwd(dh, xv, dxo, branch, wn, sc, res, stats_ref, first):
    r = lax.rsqrt(jnp.mean(xv * xv, axis=-1, keepdims=True) + EPS)
    xn = xv * r
    dhn = dh * (1.0 + sc)
    dxn = dhn * wn
    dx = dxo + r * (dxn - xn * jnp.mean(dxn * xn, axis=-1, keepdims=True))
    rows = jnp.concatenate([
        jnp.sum(dh, axis=0, keepdims=True),
        jnp.sum(dh * (xn * wn), axis=0, keepdims=True),
        jnp.sum(branch * dxo, axis=0, keepdims=True) * res,
        jnp.sum(dhn * xn, axis=0, keepdims=True),
        jnp.zeros((4, D), F32)], axis=0)

    @pl.when(first)
    def _():
        stats_ref[...] = rows

    @pl.when(jnp.logical_not(first))
    def _():
        stats_ref[...] += rows

    return dx


def _loss_head(x3, wf, tgt, mod, tm, name):
    seq = x3.shape[0]

    def body(x_ref, w_ref, t_ref, mod_ref, dx_ref, df_ref, st_ref):
        xv = x_ref[...]
        wv = w_ref[...]
        r = lax.rsqrt(jnp.mean(xv * xv, axis=-1, keepdims=True) + EPS)
        xn = xv * r
        e = xn * wv - t_ref[...]
        dy = e * (1.0 / D)
        dxn = dy * wv
        dx = r * (dxn - xn * jnp.mean(dxn * xn, axis=-1, keepdims=True))
        dx_ref[...] = dx
        df_ref[...] = (dx * (FFN_RES * mod_ref[8:9, :])).astype(BF16)
        rows = jnp.concatenate([
            jnp.sum(dy * xn, axis=0, keepdims=True),
            jnp.sum(e * e, axis=0, keepdims=True) * (0.5 / D),
            jnp.zeros((6, D), F32)], axis=0)

        @pl.when(pl.program_id(0) == 0)
        def _():
            st_ref[...] = rows

        @pl.when(pl.program_id(0) != 0)
        def _():
            st_ref[...] += rows

    return pl.pallas_call(
        body, name=name, grid=(seq // tm,),
        in_specs=[pl.BlockSpec((tm, D), lambda i: (i, 0)),
                  pl.BlockSpec((1, D), lambda i: (0, 0)),
                  pl.BlockSpec((tm, D), lambda i: (i, 0)),
                  pl.BlockSpec((9, D), lambda i: (0, 0))],
        out_specs=[pl.BlockSpec((tm, D), lambda i: (i, 0)),
                   pl.BlockSpec((tm, D), lambda i: (i, 0)),
                   pl.BlockSpec((8, D), lambda i: (0, 0))],
        out_shape=[jax.ShapeDtypeStruct((seq, D), F32), jax.ShapeDtypeStruct((seq, D), BF16),
                   jax.ShapeDtypeStruct((8, D), F32)],
        compiler_params=_cp(("arbitrary",)))(x3, wf, tgt, mod)


def _ffn_up(h, w, blk, tm, tn, name):
    seq = h.shape[0]
    nj = FF // tn

    def body(h_ref, wg_ref, wu_ref, g_ref, u_ref, a_ref):
        hv = h_ref[...]
        g = _dot(hv, wg_ref[...], 1, 1)
        u = _dot(hv, wu_ref[...], 1, 1)
        g_ref[...] = g.astype(BF16)
        u_ref[...] = u.astype(BF16)
        a_ref[...] = (g * _sigmoid(g) * u).astype(BF16)

    act = pl.BlockSpec((tm, tn), lambda j, i: (i, j))
    return pl.pallas_call(
        body, name=name, grid=(nj, seq // tm),
        in_specs=[pl.BlockSpec((tm, D), lambda j, i: (i, 0)),
                  pl.BlockSpec((tn, D), lambda j, i: (blk * nj + j, 0)),
                  pl.BlockSpec((tn, D), lambda j, i: ((blk + 1) * nj + j, 0))],
        out_specs=[act, act, act],
        out_shape=[jax.ShapeDtypeStruct((seq, FF), BF16)] * 3,
        compiler_params=_cp(("parallel", "parallel")))(h, w, w)


def _ffn_down(a, w, blk, x, mod, grow, wn_next, k_next, tm, name):
    seq = a.shape[0]
    chain = k_next is not None

    def body(a_ref, w_ref, x_ref, mod_ref, wn_ref, xo_ref, f_ref, *rest):
        f = _dot(a_ref[...], w_ref[...], 1, 0)
        f_ref[...] = f.astype(BF16)
        xo = x_ref[...] + (FFN_RES * mod_ref[grow:grow + 1, :]) * f
        xo_ref[...] = xo
        if chain:
            rest[0][...] = _modulated(xo, wn_ref[...], mod_ref, k_next)

    tok = pl.BlockSpec((tm, D), lambda i: (i, 0))
    return pl.pallas_call(
        body, name=name, grid=(seq // tm,),
        in_specs=[pl.BlockSpec((tm, FF), lambda i: (i, 0)),
                  pl.BlockSpec((FF, D), lambda i: (blk, 0)),
                  tok,
                  pl.BlockSpec((9, D), lambda i: (0, 0)),
                  pl.BlockSpec((1, D), lambda i: (0, 0))],
        out_specs=[tok, tok] + ([tok] if chain else []),
        out_shape=[jax.ShapeDtypeStruct((seq, D), F32), jax.ShapeDtypeStruct((seq, D), BF16)]
        + ([jax.ShapeDtypeStruct((seq, D), BF16)] if chain else []),
        compiler_params=_cp(("parallel",)))(a, w, x, mod, wn_next)


def _ffn_bwd_da(df, w, blk, g, u, tm, tn, name):
    seq = df.shape[0]
    nj = FF // tn

    def body(df_ref, w_ref, g_ref, u_ref, dg_ref, du_ref):
        da = _dot(df_ref[...], w_ref[...], 1, 1)
        gv = g_ref[...].astype(F32)
        uv = u_ref[...].astype(F32)
        sg = _sigmoid(gv)
        dg_ref[...] = (da * uv * _silu_grad(gv, sg)).astype(BF16)
        du_ref[...] = (da * (gv * sg)).astype(BF16)

    act = pl.BlockSpec((tm, tn), lambda j, i: (i, j))
    return pl.pallas_call(
        body, name=name, grid=(nj, seq // tm),
        in_specs=[pl.BlockSpec((tm, D), lambda j, i: (i, 0)),
                  pl.BlockSpec((tn, D), lambda j, i: (blk * nj + j, 0)),
                  act, act],
        out_specs=[act, act],
        out_shape=[jax.ShapeDtypeStruct((seq, FF), BF16)] * 2,
        compiler_params=_cp(("parallel", "parallel")))(df, w, g, u)


def _ffn_bwd_dh(dg, du, w, blk, x, dxo, fb, wn, mod, k, nxt, tm, name):
    seq = x.shape[0]

    def body(dg_ref, du_ref, wg_ref, wu_ref, x_ref, dxo_ref, f_ref, wn_ref, mod_ref, dx_ref, st_ref, *rest):
        dh = _dot(dg_ref[...], wg_ref[...], 1, 0) + _dot(du_ref[...], wu_ref[...], 1, 0)
        dx = _norm_bwd(dh, x_ref[...], dxo_ref[...], f_ref[...].astype(F32), wn_ref[...],
                       mod_ref[3 * k + 1:3 * k + 2, :], FFN_RES, st_ref, pl.program_id(0) == 0)
        dx_ref[...] = dx
        if nxt is not None:
            rest[0][...] = (dx * (nxt[1] * mod_ref[nxt[0]:nxt[0] + 1, :])).astype(BF16)

    tok = pl.BlockSpec((tm, D), lambda i: (i, 0))
    act = pl.BlockSpec((tm, FF), lambda i: (i, 0))
    return pl.pallas_call(
        body, name=name, grid=(seq // tm,),
        in_specs=[act, act,
                  pl.BlockSpec((FF, D), lambda i: (blk, 0)),
                  pl.BlockSpec((FF, D), lambda i: (blk + 1, 0)),
                  tok, tok, tok,
                  pl.BlockSpec((1, D), lambda i: (0, 0)),
                  pl.BlockSpec((9, D), lambda i: (0, 0))],
        out_specs=[tok, pl.BlockSpec((8, D), lambda i: (0, 0))] + ([tok] if nxt is not None else []),
        out_shape=[jax.ShapeDtypeStruct((seq, D), F32), jax.ShapeDtypeStruct((8, D), F32)]
        + ([jax.ShapeDtypeStruct((seq, D), BF16)] if nxt is not None else []),
        compiler_params=_cp(("arbitrary",)))(dg, du, w, w, x, dxo, fb, wn, mod)


def _ffn_forward(x, h, w, mod, k, wn_next, k_next, tm, tag):
    w_gu, w_d, blk_d = w
    g, u, a = _ffn_up(h, w_gu, 0, tm, FF // 2, f"{tag}_up")
    outs = _ffn_down(a, w_d, blk_d, x, mod, 3 * k + 2, wn_next, k_next, tm, f"{tag}_down")
    return outs[0], (outs[2] if k_next is not None else None), (x, h, g, u, a, outs[1])


def _ffn_backward(dxo, df, saved, w, wn, mod, k, nxt, tm, tag):
    w_gu, w_d, blk_d = w
    x, h, g, u, a, fb = saved
    dg, du = _ffn_bwd_da(df, w_d, blk_d, g, u, tm, FF // 2, f"{tag}_bwd_da")
    seq = x.shape[0]
    d_gate_t = _mm_tn(dg, h, 256, seq, f"{tag}_dw_gate")
    d_up_t = _mm_tn(du, h, 256, seq, f"{tag}_dw_up")
    d_down = _mm_tn(a, df, 256, seq, f"{tag}_dw_down")
    dws, dg, du = lax.optimization_barrier(((d_gate_t, d_up_t, d_down), dg, du))
    outs = _ffn_bwd_dh(dg, du, w_gu, 0, x, dxo, fb, wn, mod, k, nxt, min(tm, 256), f"{tag}_bwd_dh")
    return outs[0], (outs[2] if nxt is not None else None), outs[1], dws


def _prev_rows(tm, col):
    return pl.BlockSpec((HALO, 1024), lambda i, j: (jnp.maximum(i * (tm // HALO) - 1, 0), col + j))


def _conv_pre(ext, cw, cb, rows):
    pre = cb + cw[3:4, :] * ext
    for s in (1, 2, 3):
        pre = pre + cw[3 - s:4 - s, :] * pltpu.roll(ext, s, 0)
    return pre[HALO:HALO + rows]


def _conv_fwd(proj, cw, cb, tm, name):
    seq = proj.shape[0]

    def body(x_ref, p_ref, cw_ref, cb_ref, o_ref):
        prev = jnp.where(pl.program_id(0) == 0, 0.0, p_ref[...])
        ext = jnp.concatenate([prev, x_ref[...]], axis=0)
        pre = _conv_pre(ext, cw_ref[...], cb_ref[...], tm)
        o_ref[...] = pre * _sigmoid(pre)

    c0 = COL_XBC // 1024
    return pl.pallas_call(
        body, name=name, grid=(seq // tm, 2),
        in_specs=[pl.BlockSpec((tm, 1024), lambda i, j: (i, c0 + j)),
                  _prev_rows(tm, c0),
                  pl.BlockSpec((4, 1024), lambda i, j: (0, j)),
                  pl.BlockSpec((1, 1024), lambda i, j: (0, j))],
        out_specs=pl.BlockSpec((tm, 1024), lambda i, j: (i, j)),
        out_shape=jax.ShapeDtypeStruct((seq, D_XBC), F32),
        compiler_params=_cp(("parallel", "parallel")))(proj, proj, cw, cb)


def _conv_bwd(dact, proj, cw, cb, tm, name):
    seq = proj.shape[0]
    ni = seq // tm

    def body(d_ref, dn_ref, x_ref, p_ref, n_ref, cw_ref, cb_ref, o_ref, st_ref):
        i = pl.program_id(1)
        cwv = cw_ref[...]
        prev = jnp.where(i == 0, 0.0, p_ref[...])
        ext = jnp.concatenate([prev, x_ref[...], n_ref[...]], axis=0)
        pre = _conv_pre(ext, cwv, cb_ref[...], tm + HALO)
        dnext = jnp.where(i == ni - 1, 0.0, dn_ref[...])
        dext = jnp.concatenate([d_ref[...], dnext], axis=0)
        dpre = dext * _silu_grad(pre, _sigmoid(pre))
        n = tm + HALO
        dx = cwv[3:4, :] * dpre
        for s in (1, 2, 3):
            dx = dx + cwv[3 - s:4 - s, :] * pltpu.roll(dpre, n - s, 0)
        o_ref[...] = dx[:tm].astype(BF16)
        dcur = dpre[:tm]
        rows = [jnp.sum(dcur * pltpu.roll(ext, 3 - k, 0)[HALO:HALO + tm], axis=0, keepdims=True) for k in range(3)]
        rows.append(jnp.sum(dcur * ext[HALO:HALO + tm], axis=0, keepdims=True))
        rows.append(jnp.sum(dcur, axis=0, keepdims=True))
        rows.append(jnp.zeros((3, 1024), F32))
        rows = jnp.concatenate(rows, axis=0)

        @pl.when(i == 0)
        def _():
            st_ref[...] = rows

        @pl.when(i != 0)
        def _():
            st_ref[...] += rows

    c0 = COL_XBC // 1024
    return pl.pallas_call(
        body, name=name, grid=(2, ni),
        in_specs=[pl.BlockSpec((tm, 1024), lambda j, i: (i, j)),
                  pl.BlockSpec((HALO, 1024), lambda j, i: (jnp.minimum((i + 1) * (tm // HALO), seq // HALO - 1), j)),
                  pl.BlockSpec((tm, 1024), lambda j, i: (i, c0 + j)),
                  pl.BlockSpec((HALO, 1024), lambda j, i: (jnp.maximum(i * (tm // HALO) - 1, 0), c0 + j)),
                  pl.BlockSpec((HALO, 1024), lambda j, i: (jnp.minimum((i + 1) * (tm // HALO), seq // HALO - 1), c0 + j)),
                  pl.BlockSpec((4, 1024), lambda j, i: (0, j)),
                  pl.BlockSpec((1, 1024), lambda j, i: (0, j))],
        out_specs=[pl.BlockSpec((tm, 1024), lambda j, i: (i, j)),
                   pl.BlockSpec((8, 1024), lambda j, i: (0, j))],
        out_shape=[jax.ShapeDtypeStruct((seq, D_XBC), BF16), jax.ShapeDtypeStruct((8, D_XBC), F32)],
        compiler_params=_cp(("parallel", "arbitrary")))(dact, dact, proj, proj, proj, cw, cb)


def _bf16_parts(x, n):
    parts, rest = [], x
    for _ in range(n):
        p = rest.astype(BF16)
        parts.append(p)
        rest = rest - p.astype(F32)
    return parts


def _pick(x, sel, n):
    m = x.shape[0]
    prod = _dot(jnp.concatenate(_bf16_parts(x, n), axis=0), sel, 1, 0)
    acc = prod[0:m]
    for i in range(1, n):
        acc = acc + prod[i * m:(i + 1) * m]
    return acc


def _running(mask, x, n):
    k = x.shape[1]
    prod = _dot(mask, jnp.concatenate(_bf16_parts(x, n), axis=1), 1, 0)
    acc = prod[:, 0:k]
    for i in range(1, n):
        acc = acc + prod[:, i * k:(i + 1) * k]
    return acc


def _head_expand():
    r = lax.broadcasted_iota(jnp.int32, (LANE, D_SSD), 0)
    c = lax.broadcasted_iota(jnp.int32, (LANE, D_SSD), 1)
    return (c // HEAD_DIM == r).astype(BF16)


def _head_reduce():
    r = lax.broadcasted_iota(jnp.int32, (D_SSD, LANE), 0)
    c = lax.broadcasted_iota(jnp.int32, (D_SSD, LANE), 1)
    return (r // HEAD_DIM == c).astype(BF16)


def _ssd_common(dtr, par):
    q = CHUNK
    v = dtr + par[0:1, :]
    dt = jnp.maximum(v, 0.0) + jnp.log(1.0 + jnp.exp(-jnp.abs(v)))
    a = -jnp.exp(par[1:2, :])
    adt = dt * a
    li = lax.broadcasted_iota(jnp.int32, (q, q), 0)
    si = lax.broadcasted_iota(jnp.int32, (q, q), 1)
    causal = li >= si
    acs = _running(causal.astype(BF16), adt, 3)
    expand = _head_expand()
    both_l = _pick(jnp.concatenate([dt, acs], axis=0), expand, 3)
    dt_l, acs_l = both_l[0:q], both_l[q:2 * q]
    dskip_l = _pick(jnp.broadcast_to(par[2:3, :], (16, LANE)), expand, 3)[0:1, :]
    last_l = acs_l[q - 1:q, :]
    return dict(v=v, dt=dt, a=a, acs=acs, acs_t=acs.T, causal=causal, dt_l=dt_l, acs_l=acs_l,
                ea_l=jnp.exp(acs_l), ds_l=jnp.exp(last_l - acs_l), cd_l=jnp.exp(last_l), dskip_l=dskip_l)


def _decay(cm, h):
    seg = cm["acs"][:, h:h + 1] - cm["acs_t"][h:h + 1, :]
    return jnp.exp(jnp.where(cm["causal"], seg, -jnp.inf))


def _lane_mask(r):
    lane = lax.broadcasted_iota(jnp.int32, (1, GROUP_W), 1)
    return lane // HEAD_DIM == r


def _ssd_fwd(xbc, proj, par, name):
    seq = xbc.shape[0]
    nc = seq // CHUNK
    q = CHUNK

    def body(x_ref, dt_ref, par_ref, y_ref, hp_ref, state):
        @pl.when(pl.program_id(0) == 0)
        def _():
            state[...] = jnp.zeros_like(state)

        cm = _ssd_common(dt_ref[...], par_ref[...])
        for g in range(N_GROUPS):
            lo = g * GROUP_W
            xs = x_ref[:, lo:lo + GROUP_W]
            bm = x_ref[:, D_SSD + g * N_STATE:D_SSD + (g + 1) * N_STATE].astype(BF16)
            cmat = x_ref[:, D_SSD + N_GROUPS * N_STATE + g * N_STATE:D_SSD + N_GROUPS * N_STATE + (g + 1) * N_STATE].astype(BF16)
            xdt = xs * cm["dt_l"][:, lo:lo + GROUP_W]
            xdt_b = xdt.astype(BF16)
            cb = _dot(cmat, bm, 1, 1)
            yd = jnp.zeros((q, GROUP_W), F32)
            for r in range(4):
                s_h = (cb * _decay(cm, 4 * g + r)).astype(BF16)
                yd = jnp.where(_lane_mask(r), _dot(s_h, xdt_b, 1, 0), yd)
            hg = state[g]
            hp_ref[0, g] = hg
            yo = _dot(cmat, hg.astype(BF16), 1, 0) * cm["ea_l"][:, lo:lo + GROUP_W]
            y_ref[:, lo:lo + GROUP_W] = yd + yo + cm["dskip_l"][:, lo:lo + GROUP_W] * xs
            xds = (xdt * cm["ds_l"][:, lo:lo + GROUP_W]).astype(BF16)
            state[g] = hg * cm["cd_l"][:, lo:lo + GROUP_W] + _dot(bm, xds, 0, 0)

    return pl.pallas_call(
        body, name=name, grid=(nc,),
        in_specs=[pl.BlockSpec((q, D_XBC), lambda c: (c, 0)),
                  pl.BlockSpec((q, LANE), lambda c: (c, COL_DT // LANE)),
                  pl.BlockSpec((8, LANE), lambda c: (0, 0))],
        out_specs=[pl.BlockSpec((q, D_SSD), lambda c: (c, 0)),
                   pl.BlockSpec((1, N_GROUPS, N_STATE, GROUP_W), lambda c: (c, 0, 0, 0))],
        out_shape=[jax.ShapeDtypeStruct((seq, D_SSD), F32),
                   jax.ShapeDtypeStruct((nc, N_GROUPS, N_STATE, GROUP_W), F32)],
        scratch_shapes=[pltpu.VMEM((N_GROUPS, N_STATE, GROUP_W), F32)],
        compiler_params=_cp(("arbitrary",)))(xbc, proj, par)


def _ssd_bwd(dy, xbc, proj, par, hprev, name):
    seq = xbc.shape[0]
    nc = seq // CHUNK
    q = CHUNK

    def body(dy_ref, x_ref, dt_ref, par_ref, hp_ref, dx_ref, ddt_ref, st_ref, dstate):
        step = pl.program_id(0)

        @pl.when(step == 0)
        def _():
            dstate[...] = jnp.zeros_like(dstate)

        par = par_ref[...]
        cm = _ssd_common(dt_ref[...], par)
        reduce = _head_reduce()
        lane128 = lax.broadcasted_iota(jnp.int32, (1, LANE), 1)
        row128 = lax.broadcasted_iota(jnp.int32, (LANE, 1), 0)
        d_acs = jnp.zeros((q, LANE), F32)
        d_acs_t = jnp.zeros((LANE, q), F32)
        last_terms = []
        acs_terms = []
        dxdt_all = []
        for g in range(N_GROUPS):
            lo = g * GROUP_W
            sl = slice(lo, lo + GROUP_W)
            xs = x_ref[:, sl]
            bm32 = x_ref[:, D_SSD + g * N_STATE:D_SSD + (g + 1) * N_STATE]
            cm32 = x_ref[:, D_SSD + N_GROUPS * N_STATE + g * N_STATE:D_SSD + N_GROUPS * N_STATE + (g + 1) * N_STATE]
            bm = bm32.astype(BF16)
            cmat = cm32.astype(BF16)
            dyg = dy_ref[:, sl]
            dyg_b = dyg.astype(BF16)
            xdt = xs * cm["dt_l"][:, sl]
            xdt_b = xdt.astype(BF16)
            hg = hp_ref[0, g]
            hg_b = hg.astype(BF16)
            dhg = dstate[g]
            dhg_b = dhg.astype(BF16)
            ea = cm["ea_l"][:, sl]
            ds = cm["ds_l"][:, sl]
            cd = cm["cd_l"][:, sl]
            yoff = _dot(cmat, hg_b, 1, 0) * ea
            dw = (dyg * ea).astype(BF16)
            d_c = _dot(dw, hg_b, 1, 1)
            d_hprev = _dot(cmat, dw, 0, 0) + dhg * cd
            t_acs = dyg * yoff
            d_last_g = jnp.sum(dhg * hg, axis=0, keepdims=True) * cd
            xds_b = (xdt * ds).astype(BF16)
            dxds = _dot(bm, dhg_b, 1, 0)
            d_b = _dot(xds_b, dhg_b, 1, 1)
            dxdt = dxds * ds
            t_ds = dxds * xdt * ds
            t_acs = t_acs - t_ds
            d_last_g = d_last_g + jnp.sum(t_ds, axis=0, keepdims=True)
            cb = _dot(cmat, bm, 1, 1)
            d_cb = jnp.zeros((q, q), F32)
            for r in range(4):
                h = 4 * g + r
                dec = _decay(cm, h)
                s_h = cb * dec
                mask = _lane_mask(r)
                d_s = _dot(jnp.where(mask, dyg, 0.0).astype(BF16), xdt_b, 1, 1)
                dxdt = dxdt + jnp.where(mask, _dot(s_h.astype(BF16), dyg_b, 0, 0), 0.0)
                d_cb = d_cb + d_s * dec
                d_m = d_s * s_h
                d_acs = d_acs + jnp.where(lane128 == h, jnp.sum(d_m, axis=1, keepdims=True), 0.0)
                d_acs_t = d_acs_t + jnp.where(row128 == h, jnp.sum(d_m, axis=0, keepdims=True), 0.0)
            d_cb_b = d_cb.astype(BF16)
            d_c = d_c + _dot(d_cb_b, bm, 1, 0)
            d_b = d_b + _dot(d_cb_b, cmat, 0, 0)
            dstate[g] = d_hprev
            dx_ref[:, sl] = dxdt * cm["dt_l"][:, sl] + cm["dskip_l"][:, sl] * dyg
            dx_ref[:, D_SSD + g * N_STATE:D_SSD + (g + 1) * N_STATE] = d_b
            dx_ref[:, D_SSD + N_GROUPS * N_STATE + g * N_STATE:D_SSD + N_GROUPS * N_STATE + (g + 1) * N_STATE] = d_c
            acs_terms.append(t_acs)
            dxdt_all.append(dxdt * xs)
            last_terms.append(d_last_g)
        t_acs_l = jnp.concatenate(acs_terms, axis=1)
        d_dt_l = jnp.concatenate(dxdt_all, axis=1)
        d_last_l = jnp.concatenate(last_terms, axis=1)
        per_head = _pick(jnp.concatenate([t_acs_l, d_dt_l], axis=0), reduce, 2)
        skip_l = jnp.sum(dy_ref[...] * x_ref[:, 0:D_SSD], axis=0, keepdims=True)
        singles = _pick(jnp.concatenate([d_last_l, skip_l, jnp.zeros((14, D_SSD), F32)], axis=0), reduce, 3)
        d_acs = d_acs + per_head[0:q] - d_acs_t.T
        last_row = lax.broadcasted_iota(jnp.int32, (q, 1), 0) == q - 1
        d_acs = d_acs + jnp.where(last_row, singles[0:1, :], 0.0)
        li = lax.broadcasted_iota(jnp.int32, (q, q), 0)
        si = lax.broadcasted_iota(jnp.int32, (q, q), 1)
        d_adt = _running((si >= li).astype(BF16), d_acs, 3)
        d_dt = per_head[q:2 * q] + d_adt * cm["a"]
        d_dtr = d_dt * _sigmoid(cm["v"])
        ddt_ref[...] = d_dtr.astype(BF16)
        d_skip = singles[1:2, :]
        rows = jnp.concatenate([
            jnp.sum(d_dtr, axis=0, keepdims=True),
            jnp.sum(d_adt * cm["dt"], axis=0, keepdims=True) * cm["a"],
            d_skip,
            jnp.zeros((5, LANE), F32)], axis=0)

        @pl.when(step == 0)
        def _():
            st_ref[...] = rows

        @pl.when(step != 0)
        def _():
            st_ref[...] += rows

    rev = lambda c: nc - 1 - c
    return pl.pallas_call(
        body, name=name, grid=(nc,),
        in_specs=[pl.BlockSpec((q, D_SSD), lambda c: (rev(c), 0)),
                  pl.BlockSpec((q, D_XBC), lambda c: (rev(c), 0)),
                  pl.BlockSpec((q, LANE), lambda c: (rev(c), COL_DT // LANE)),
                  pl.BlockSpec((8, LANE), lambda c: (0, 0)),
                  pl.BlockSpec((1, N_GROUPS, N_STATE, GROUP_W), lambda c: (rev(c), 0, 0, 0))],
        out_specs=[pl.BlockSpec((q, D_XBC), lambda c: (rev(c), 0)),
                   pl.BlockSpec((q, LANE), lambda c: (rev(c), 0)),
                   pl.BlockSpec((8, LANE), lambda c: (0, 0))],
        out_shape=[jax.ShapeDtypeStruct((seq, D_XBC), F32),
                   jax.ShapeDtypeStruct((seq, LANE), BF16),
                   jax.ShapeDtypeStruct((8, LANE), F32)],
        scratch_shapes=[pltpu.VMEM((N_GROUPS, N_STATE, GROUP_W), F32)],
        compiler_params=_cp(("arbitrary",)))(dy, xbc, proj, par, hprev)


def _gate_norm_fwd(y, proj, wn, tm, name):
    seq = y.shape[0]

    def body(y_ref, z_ref, w_ref, o_ref):
        for g in range(N_GROUPS):
            sl = slice(g * GROUP_W, (g + 1) * GROUP_W)
            zv = z_ref[:, sl]
            yz = y_ref[:, sl] * (zv * _sigmoid(zv))
            r = lax.rsqrt(jnp.mean(yz * yz, axis=-1, keepdims=True) + EPS)
            o_ref[:, sl] = (yz * r * w_ref[:, sl]).astype(BF16)

    tok = pl.BlockSpec((tm, D_SSD), lambda i: (i, 0))
    return pl.pallas_call(
        body, name=name, grid=(seq // tm,),
        in_specs=[tok, tok, pl.BlockSpec((1, D_SSD), lambda i: (0, 0))],
        out_specs=tok,
        out_shape=jax.ShapeDtypeStruct((seq, D_SSD), BF16),
        compiler_params=_cp(("parallel",)))(y, proj, wn)


def _gate_norm_bwd(dys, y, proj, wn, tm, name):
    seq = y.shape[0]

    def body(d_ref, y_ref, z_ref, w_ref, dy_ref, dz_ref, st_ref):
        rows = []
        for g in range(N_GROUPS):
            sl = slice(g * GROUP_W, (g + 1) * GROUP_W)
            zv = z_ref[:, sl]
            yv = y_ref[:, sl]
            sg = _sigmoid(zv)
            sz = zv * sg
            yz = yv * sz
            r = lax.rsqrt(jnp.mean(yz * yz, axis=-1, keepdims=True) + EPS)
            yn = yz * r
            dv = d_ref[:, sl]
            dyn = dv * w_ref[:, sl]
            dyz = r * (dyn - yn * jnp.mean(dyn * yn, axis=-1, keepdims=True))
            dy_ref[:, sl] = dyz * sz
            dz_ref[:, sl] = (dyz * yv * _silu_grad(zv, sg)).astype(BF16)
            rows.append(jnp.sum(dv * yn, axis=0, keepdims=True))
        rows = jnp.concatenate([jnp.concatenate(rows, axis=1), jnp.zeros((7, D_SSD), F32)], axis=0)

        @pl.when(pl.program_id(0) == 0)
        def _():
            st_ref[...] = rows

        @pl.when(pl.program_id(0) != 0)
        def _():
            st_ref[...] += rows

    tok = pl.BlockSpec((tm, D_SSD), lambda i: (i, 0))
    return pl.pallas_call(
        body, name=name, grid=(seq // tm,),
        in_specs=[tok, tok, tok, pl.BlockSpec((1, D_SSD), lambda i: (0, 0))],
        out_specs=[tok, tok, pl.BlockSpec((8, D_SSD), lambda i: (0, 0))],
        out_shape=[jax.ShapeDtypeStruct((seq, D_SSD), F32), jax.ShapeDtypeStruct((seq, D_SSD), BF16),
                   jax.ShapeDtypeStruct((8, D_SSD), F32)],
        compiler_params=_cp(("arbitrary",)))(dys, y, proj, wn)


def _pool_counts(t0, rows, w):
    pos = (t0 + 1 + lax.broadcasted_iota(jnp.int32, (rows, 1), 0)).astype(F32)
    return jnp.minimum(pos, float(w))


def _window_means(ext, t0):
    n = ext.shape[0]
    outs = []
    run = ext
    width = 1
    sums = {}
    while width < 16:
        run = run + pltpu.roll(run, width, 0)
        width *= 2
        sums[width] = run
    for g, w in enumerate(POOL_WINDOWS):
        sl = slice(g * POOL_GW, (g + 1) * POOL_GW)
        cnt = _pool_counts(t0, n - HALO, w)
        outs.append(sums[w][HALO:, sl] / cnt - ext[HALO:, sl])
    return outs


def _pool_fwd(proj, pw, pb, ps, tm, name):
    seq = proj.shape[0]

    def body(u_ref, p_ref, pw_ref, pb_ref, ps_ref, o_ref):
        i = pl.program_id(0)
        prev = jnp.where(i == 0, 0.0, p_ref[...])
        ext = jnp.concatenate([prev, u_ref[...]], axis=0)
        diffs = _window_means(ext, i * tm)
        for g in range(4):
            sl = slice(g * POOL_GW, (g + 1) * POOL_GW)
            out = _dot(diffs[g].astype(BF16), pw_ref[g], 1, 0) + pb_ref[:, sl]
            o_ref[:, sl] = (out * ps_ref[:, sl]).astype(BF16)

    c0 = COL_U // 1024
    vec = pl.BlockSpec((1, D_POOL), lambda i: (0, 0))
    return pl.pallas_call(
        body, name=name, grid=(seq // tm,),
        in_specs=[pl.BlockSpec((tm, 1024), lambda i: (i, c0)),
                  pl.BlockSpec((HALO, 1024), lambda i: (jnp.maximum(i * (tm // HALO) - 1, 0), c0)),
                  pl.BlockSpec((4, POOL_GW, POOL_GW), lambda i: (0, 0, 0)), vec, vec],
        out_specs=pl.BlockSpec((tm, D_POOL), lambda i: (i, 0)),
        out_shape=jax.ShapeDtypeStruct((seq, D_POOL), BF16),
        compiler_params=_cp(("parallel",)))(proj, proj, pw, pb, ps)


def _pool_bwd(dyp, proj, pw, pb, ps, tm, name):
    seq = proj.shape[0]
    ni = seq // tm

    def body(d_ref, dn_ref, u_ref, p_ref, pw_ref, pb_ref, ps_ref, du_ref, dw_ref, st_ref):
        i = pl.program_id(0)
        prev = jnp.where(i == 0, 0.0, p_ref[...])
        ext = jnp.concatenate([prev, u_ref[...]], axis=0)
        diffs = _window_means(ext, i * tm)
        dnext = jnp.where(i == ni - 1, 0.0, dn_ref[...])
        dext = jnp.concatenate([d_ref[...], dnext], axis=0)
        n = tm + HALO
        b_rows, s_rows = [], []
        for g, w in enumerate(POOL_WINDOWS):
            sl = slice(g * POOL_GW, (g + 1) * POOL_GW)
            wg = pw_ref[g]
            dout = dext[:, sl] * ps_ref[:, sl]
            dcur = dout[:tm]
            pre = _dot(diffs[g].astype(BF16), wg, 1, 0) + pb_ref[:, sl]
            s_rows.append(jnp.sum(d_ref[:, sl] * pre, axis=0, keepdims=True))
            b_rows.append(jnp.sum(dcur, axis=0, keepdims=True))
            dwg = _dot(diffs[g].astype(BF16), dcur.astype(BF16), 0, 0)

            @pl.when(i == 0)
            def _():
                dw_ref[g] = dwg

            @pl.when(i != 0)
            def _():
                dw_ref[g] += dwg

            ddiff = _dot(dout.astype(BF16), wg, 1, 1)
            scaled = ddiff / _pool_counts(i * tm, n, w)
            run = scaled
            width = 1
            while width < w:
                run = run + pltpu.roll(run, n - width, 0)
                width *= 2
            du_ref[:, sl] = (run[:tm] - ddiff[:tm]).astype(BF16)
        rows = jnp.concatenate([jnp.concatenate(b_rows, axis=1), jnp.concatenate(s_rows, axis=1),
                                jnp.zeros((6, D_POOL), F32)], axis=0)

        @pl.when(i == 0)
        def _():
            st_ref[...] = rows

        @pl.when(i != 0)
        def _():
            st_ref[...] += rows

    c0 = COL_U // 1024
    vec = pl.BlockSpec((1, D_POOL), lambda i: (0, 0))
    last = seq // HALO - 1
    return pl.pallas_call(
        body, name=name, grid=(ni,),
        in_specs=[pl.BlockSpec((tm, D_POOL), lambda i: (i, 0)),
                  pl.BlockSpec((HALO, D_POOL), lambda i: (jnp.minimum((i + 1) * (tm // HALO), last), 0)),
                  pl.BlockSpec((tm, 1024), lambda i: (i, c0)),
                  pl.BlockSpec((HALO, 1024), lambda i: (jnp.maximum(i * (tm // HALO) - 1, 0), c0)),
                  pl.BlockSpec((4, POOL_GW, POOL_GW), lambda i: (0, 0, 0)), vec, vec],
        out_specs=[pl.BlockSpec((tm, D_POOL), lambda i: (i, 0)),
                   pl.BlockSpec((4, POOL_GW, POOL_GW), lambda i: (0, 0, 0)),
                   pl.BlockSpec((8, D_POOL), lambda i: (0, 0))],
        out_shape=[jax.ShapeDtypeStruct((seq, D_POOL), BF16),
                   jax.ShapeDtypeStruct((4, POOL_GW, POOL_GW), F32),
                   jax.ShapeDtypeStruct((8, D_POOL), F32)],
        compiler_params=_cp(("arbitrary",)))(dyp, dyp, proj, proj, pw, pb, ps)


def _mix_out(ys, yp, wout, x1, mod, wn_next, tm, name):
    seq = ys.shape[0]

    def body(ys_ref, yp_ref, w_ref, x_ref, mod_ref, wn_ref, xo_ref, m_ref, h_ref):
        mix = _dot(ys_ref[...], w_ref[0:D_SSD, :], 1, 0) + _dot(yp_ref[...], w_ref[D_SSD:2 * D_SSD, :], 1, 0)
        m_ref[...] = mix.astype(BF16)
        xo = x_ref[...] + mod_ref[5:6, :] * mix
        xo_ref[...] = xo
        h_ref[...] = _modulated(xo, wn_ref[...], mod_ref, 2)

    tok = pl.BlockSpec((tm, D), lambda i: (i, 0))
    return pl.pallas_call(
        body, name=name, grid=(seq // tm,),
        in_specs=[tok, tok, pl.BlockSpec((2 * D_SSD, D), lambda i: (0, 0)), tok,
                  pl.BlockSpec((9, D), lambda i: (0, 0)), pl.BlockSpec((1, D), lambda i: (0, 0))],
        out_specs=[tok, tok, tok],
        out_shape=[jax.ShapeDtypeStruct((seq, D), F32), jax.ShapeDtypeStruct((seq, D), BF16),
                   jax.ShapeDtypeStruct((seq, D), BF16)],
        compiler_params=_cp(("parallel",)))(ys, yp, wout, x1, mod, wn_next)


def _mix_bwd_dh(dz, dxbc, du, ddt, win_t, x1, dx2, mixb, wn, mod, tm, name):
    seq = x1.shape[0]

    def body(dz_ref, dx_ref, du_ref, ddt_ref, w_ref, x_ref, dxo_ref, m_ref, wn_ref, mod_ref, o_ref, st_ref, df_ref):
        dh = (_dot(dz_ref[...], w_ref[COL_Z:COL_Z + 1024, :], 1, 0)
              + _dot(dx_ref[...], w_ref[COL_XBC:COL_XBC + D_XBC, :], 1, 0)
              + _dot(du_ref[...], w_ref[COL_U:COL_U + 1024, :], 1, 0)
              + _dot(ddt_ref[...], w_ref[COL_DT:COL_DT + LANE, :], 1, 0))
        dx = _norm_bwd(dh, x_ref[...], dxo_ref[...], m_ref[...].astype(F32), wn_ref[...],
                       mod_ref[4:5, :], 1.0, st_ref, pl.program_id(0) == 0)
        o_ref[...] = dx
        df_ref[...] = (dx * (FFN_RES * mod_ref[2:3, :])).astype(BF16)

    tok = pl.BlockSpec((tm, D), lambda i: (i, 0))
    return pl.pallas_call(
        body, name=name, grid=(seq // tm,),
        in_specs=[tok, pl.BlockSpec((tm, D_XBC), lambda i: (i, 0)), tok,
                  pl.BlockSpec((tm, LANE), lambda i: (i, 0)),
                  pl.BlockSpec((D_IN_PAD, D), lambda i: (0, 0)),
                  tok, tok, tok,
                  pl.BlockSpec((1, D), lambda i: (0, 0)),
                  pl.BlockSpec((9, D), lambda i: (0, 0))],
        out_specs=[tok, pl.BlockSpec((8, D), lambda i: (0, 0)), tok],
        out_shape=[jax.ShapeDtypeStruct((seq, D), F32), jax.ShapeDtypeStruct((8, D), F32),
                   jax.ShapeDtypeStruct((seq, D), BF16)],
        compiler_params=_cp(("arbitrary",)))(dz, dxbc, du, ddt, win_t, x1, dx2, mixb, wn, mod)


def _mix_bwd_dycat(dmix, wout, tm, name):
    seq = dmix.shape[0]

    def body(d_ref, w_ref, a_ref, b_ref):
        dv = d_ref[...]
        a_ref[...] = _dot(dv, w_ref[0:D_SSD, :], 1, 1)
        b_ref[...] = _dot(dv, w_ref[D_SSD:2 * D_SSD, :], 1, 1)

    tok = pl.BlockSpec((tm, D), lambda i: (i, 0))
    return pl.pallas_call(
        body, name=name, grid=(seq // tm,),
        in_specs=[tok, pl.BlockSpec((2 * D_SSD, D), lambda i: (0, 0))],
        out_specs=[tok, tok],
        out_shape=[jax.ShapeDtypeStruct((seq, D), F32)] * 2,
        compiler_params=_cp(("parallel",)))(dmix, wout)


def _local_step(x, tgt, mod, wff1, later_weights, pool_w, vecs, tm):
    seq = x.shape[0]
    in_proj_weights, out_proj_weights, ffn2_weights = later_weights
    h1 = _prenorm(x, vecs["ffn1_norm"], mod, 0, tm, "ffn1_prenorm")
    x1, h2, s1 = _ffn_forward(x, h1, wff1, mod, 0, vecs["mix_norm"], 1, tm, "ffn1")
    x1, win_t = in_proj_weights(x1)
    proj = _mm_nt(h2, win_t, tm, D_IN_PAD // 3, F32, "mix_in_proj")
    xbc = _conv_fwd(proj, vecs["conv_w"], vecs["conv_b"], tm, "mix_conv")
    y, hprev = _ssd_fwd(xbc, proj, vecs["ssd_par"], "mix_ssd")
    ys = _gate_norm_fwd(y, proj, vecs["ssd_norm_w"], tm, "mix_gate_norm")
    yp = _pool_fwd(proj, pool_w, vecs["pool_b"], vecs["pool_scale"], tm, "mix_pool")
    ys, wout = out_proj_weights(ys)
    x2, mixb, h3 = _mix_out(ys, yp, wout, x1, mod, vecs["ffn2_norm"], tm, "mix_out_proj")
    x2, wff2 = ffn2_weights(x2)
    x3, _, s3 = _ffn_forward(x2, h3, wff2, mod, 2, vecs["ffn2_norm"], None, tm, "ffn2")
    dx3, df3, st_loss = _loss_head(x3, vecs["final_norm"], tgt, mod, tm, "loss_head")

    dx2, dmix, st3, dw3 = _ffn_backward(dx3, df3, s3, wff2, vecs["ffn2_norm"], mod, 2, (5, 1.0), tm, "ffn2")
    dys, dyp = _mix_bwd_dycat(dmix, wout, tm, "mix_bwd_dycat")
    d_wout = (_mm_tn(ys, dmix, 256, seq, "mix_dw_out_ssd"), _mm_tn(yp, dmix, 256, seq, "mix_dw_out_pool"))
    du, d_pool_w, st_pool = _pool_bwd(dyp, proj, pool_w, vecs["pool_b"], vecs["pool_scale"], tm, "mix_pool_bwd")
    dy, dz, st_gn = _gate_norm_bwd(dys, y, proj, vecs["ssd_norm_w"], tm, "mix_gate_norm_bwd")
    dxbc_act, ddt, st_ssd = _ssd_bwd(dy, xbc, proj, vecs["ssd_par"], hprev, "mix_ssd_bwd")
    dxbc, st_conv = _conv_bwd(dxbc_act, proj, vecs["conv_w"], vecs["conv_b"], tm, "mix_conv_bwd")
    dx1, st2, df1 = _mix_bwd_dh(dz, dxbc, du, ddt, win_t, x1, dx2, mixb, vecs["mix_norm"], mod, min(tm, 256), "mix_bwd_dh")
    d_win = (_mm_tn(dz, h2, 256, seq, "mix_dw_in_z"), _mm_tn(dxbc, h2, 256, seq, "mix_dw_in_xbc"),
             _mm_tn(du, h2, 256, seq, "mix_dw_in_u"), _mm_tn(ddt, h2, LANE, seq, "mix_dw_in_dt"))
    dx0, _, st1, dw1 = _ffn_backward(dx1, df1, s1, wff1, vecs["ffn1_norm"], mod, 0, None, tm, "ffn1")
    stats = dict(ffn1=st1, mix=st2, ffn2=st3, loss=st_loss, pool=st_pool, gn=st_gn, ssd=st_ssd, conv=st_conv)
    return dx0, stats, dw1, dw3, d_win, d_wout, d_pool_w


HBM_SPEC = pl.BlockSpec(memory_space=pltpu.HBM)


def _mesh_pos():
    return lax.axis_index("x"), lax.axis_index("y"), lax.axis_index("c")


def _other_chips(x, y):
    return [(1 - x, y), (x, 1 - y), (1 - x, 1 - y)]


def _all_gather(src, regions, name):
    total, cols = src.shape
    assert sum(r for _, r in regions) == total
    body = _all_gather_body(regions, total, False)
    return pl.pallas_call(
        body, name=name,
        out_shape=jax.ShapeDtypeStruct((N_DEV * total, cols), src.dtype),
        in_specs=[HBM_SPEC], out_specs=HBM_SPEC,
        scratch_shapes=[pltpu.SemaphoreType.DMA((7,)), pltpu.SemaphoreType.DMA((7,)), pltpu.SemaphoreType.DMA],
    )(src)


def _all_gather_async(src, regions, name, collective_id):
    total, cols = src.shape
    assert sum(r for _, r in regions) == total
    return pl.kernel(
        _all_gather_body(regions, total, True), name=name,
        out_type=jax.ShapeDtypeStruct((N_DEV * total, cols), src.dtype),
        mesh=plsc.ScalarSubcoreMesh(axis_name="seq", num_cores=1),
        scratch_types=(pltpu.SemaphoreType.DMA((7,)), pltpu.SemaphoreType.DMA((7,)), pltpu.SemaphoreType.DMA),
        compiler_params=pltpu.CompilerParams(collective_id=collective_id))(src)


def _all_gather_body(regions, total, handshake):
    def body(src_ref, out_ref, send_sems, recv_sems, local_sem):
        x, y, c = _mesh_pos()
        me, sibling = (x, y, c), (x, y, 1 - c)
        chips = _other_chips(x, y)
        if handshake:
            barrier = pltpu.get_barrier_semaphore()
            for peer in [sibling] + [(*chip, c) for chip in chips]:
                pl.semaphore_signal(barrier, inc=1, device_id=peer, device_id_type=MESH)
            pl.semaphore_wait(barrier, 4)

        def rows_of(dev, off, rows):
            start = pl.multiple_of(N_DEV * off + (4 * dev[0] + 2 * dev[1] + dev[2]) * rows, 8)
            return out_ref.at[pl.ds(start, rows), :]

        def copies(k, block, to, from_src):
            out = []
            for off, rows in regions:
                dst = rows_of(block, off, rows)
                out.append(pltpu.make_async_remote_copy(
                    src_ref=src_ref.at[pl.ds(off, rows), :] if from_src else dst, dst_ref=dst,
                    send_sem=send_sems.at[k], recv_sem=recv_sems.at[k], device_id=to, device_id_type=MESH))
            return out

        def drain(k):
            whole = out_ref.at[pl.ds(0, total), :]
            return pltpu.make_async_remote_copy(src_ref=whole, dst_ref=whole, send_sem=send_sems.at[k],
                                                recv_sem=recv_sems.at[k], device_id=me, device_id_type=MESH)

        for off, rows in regions:
            pltpu.make_async_copy(src_ref.at[pl.ds(off, rows), :], rows_of(me, off, rows), local_sem).start()
        first = copies(0, me, sibling, True)
        for j, chip in enumerate(chips):
            first += copies(1 + j, me, (*chip, c), True)
        for cp in first:
            cp.start()
        for j, chip in enumerate(chips):
            drain(1 + j).wait_recv()
            for cp in copies(4 + j, (*chip, c), sibling, False):
                cp.start()
        drain(0).wait_recv()
        for j in range(3):
            drain(4 + j).wait_recv()
        for k in range(7):
            drain(k).wait_send()
        pltpu.make_async_copy(src_ref, out_ref.at[pl.ds(0, total), :], local_sem).wait()

    return body


def _rs_pair(grads, total, name, collective_id):
    cols = grads[0][0].shape[1]
    sent = sum(rows for _, _, rows in grads)
    n = len(grads)

    def body(*refs):
        g_refs, recv_ref, send_sem, recv_sem = refs[:n], refs[n], refs[n + 1], refs[n + 2]
        x, y, c = _mesh_pos()
        sibling = (x, y, 1 - c)
        barrier = pltpu.get_barrier_semaphore()
        pl.semaphore_signal(barrier, inc=1, device_id=sibling, device_id_type=MESH)
        pl.semaphore_wait(barrier, 1)
        for q in range(4):
            for g_ref, (_, off, rows) in zip(g_refs, grads):
                theirs = g_ref.at[pl.ds(pl.multiple_of((2 * q + 1 - c) * rows, 8), rows), :]
                pltpu.make_async_remote_copy(
                    src_ref=theirs, dst_ref=recv_ref.at[q, pl.ds(off, rows), :], send_sem=send_sem, recv_sem=recv_sem,
                    device_id=sibling, device_id_type=MESH).start()
        everything = recv_ref.at[:, pl.ds(0, sent), :]
        whole = pltpu.make_async_remote_copy(src_ref=everything, dst_ref=everything, send_sem=send_sem,
                                             recv_sem=recv_sem, device_id=sibling, device_id_type=MESH)
        whole.wait_send()
        whole.wait_recv()

    return pl.kernel(
        body, name=name, out_type=jax.ShapeDtypeStruct((4, total, cols), F32),
        mesh=plsc.ScalarSubcoreMesh(axis_name="seq", num_cores=1),
        scratch_types=(pltpu.SemaphoreType.DMA, pltpu.SemaphoreType.DMA),
        compiler_params=pltpu.CompilerParams(collective_id=collective_id))(*[g for g, _, _ in grads])


def _pair_sum(grads, from_sibling, pos, name):
    cols = grads[0][0].shape[1]
    n = len(grads)

    def body(pos_ref, *refs):
        for i in range(n):
            s = refs[i][...] + refs[n + i][...]
            refs[2 * n + 2 * i][...] = s
            refs[2 * n + 2 * i + 1][...] = s.astype(BF16)

    in_specs = [pl.BlockSpec((None, None, rows, cols), lambda q, pos_ref: (q, pos_ref[0], 0, 0)) for _, _, rows in grads]
    in_specs += [pl.BlockSpec((None, rows, cols), lambda q, pos_ref, blk=off // rows: (q, blk, 0)) for _, off, rows in grads]
    out_specs, out_shape = [], []
    for _, _, rows in grads:
        out_specs += [pl.BlockSpec((None, rows, cols), lambda q, pos_ref: (q, 0, 0))] * 2
        out_shape += [jax.ShapeDtypeStruct((4, rows, cols), F32), jax.ShapeDtypeStruct((4, rows, cols), BF16)]
    outs = pl.pallas_call(
        body, name=name,
        grid_spec=pltpu.PrefetchScalarGridSpec(num_scalar_prefetch=1, grid=(4,), in_specs=in_specs, out_specs=out_specs),
        out_shape=out_shape,
        compiler_params=_cp(("parallel",)))(pos, *[g.reshape(4, 2, rows, cols) for g, _, rows in grads],
                                            *[from_sibling] * n)
    return [(outs[2 * i], outs[2 * i + 1]) for i in range(n)]


def _rs_chips(parts, total, name, collective_id):
    cols = parts[0][0].shape[2]
    sent = sum(rows for _, _, rows in parts)
    n = len(parts)

    def body(*refs):
        p_refs, out_ref, send_sems, recv_sems = refs[:n], refs[n], refs[n + 1], refs[n + 2]
        x, y, c = _mesh_pos()
        chips = _other_chips(x, y)
        barrier = pltpu.get_barrier_semaphore()
        for chip in chips:
            pl.semaphore_signal(barrier, inc=1, device_id=(*chip, c), device_id_type=MESH)
        pl.semaphore_wait(barrier, 3)
        for j, chip in enumerate(chips):
            q = 2 * chip[0] + chip[1]
            for p_ref, (_, off, rows) in zip(p_refs, parts):
                pltpu.make_async_remote_copy(
                    src_ref=p_ref.at[q], dst_ref=out_ref.at[j, pl.ds(off, rows), :], send_sem=send_sems.at[j],
                    recv_sem=recv_sems.at[j], device_id=(*chip, c), device_id_type=MESH).start()
        for j, chip in enumerate(chips):
            everything = out_ref.at[j, pl.ds(0, sent), :]
            whole = pltpu.make_async_remote_copy(src_ref=everything, dst_ref=everything, send_sem=send_sems.at[j],
                                                 recv_sem=recv_sems.at[j], device_id=(*chip, c), device_id_type=MESH)
            whole.wait_recv()
            whole.wait_send()

    return pl.kernel(
        body, name=name, out_type=jax.ShapeDtypeStruct((3, total, cols), BF16),
        mesh=plsc.ScalarSubcoreMesh(axis_name="seq", num_cores=1),
        scratch_types=(pltpu.SemaphoreType.DMA((3,)), pltpu.SemaphoreType.DMA((3,))),
        compiler_params=pltpu.CompilerParams(collective_id=collective_id))(*[p for p, _, _ in parts])


def _chip_sum(p, from_chips, off, rows, pos, name):
    cols = p.shape[2]

    def body(pos_ref, p_ref, r_ref, o_ref):
        acc = p_ref[...]
        for j in range(3):
            acc = acc + r_ref[j].astype(F32)
        o_ref[...] = acc

    return pl.pallas_call(
        body, name=name,
        grid_spec=pltpu.PrefetchScalarGridSpec(
            num_scalar_prefetch=1, grid=(1,),
            in_specs=[pl.BlockSpec((None, rows, cols), lambda i, pos_ref: (pos_ref[1], 0, 0)),
                      pl.BlockSpec((3, rows, cols), lambda i, pos_ref: (0, off // rows, 0))],
            out_specs=pl.BlockSpec((rows, cols), lambda i, pos_ref: (0, 0))),
        out_shape=jax.ShapeDtypeStruct((rows, cols), F32),
        compiler_params=_cp(("arbitrary",)))(pos, p, from_chips)


def _chip_sum_adamw(p, from_chips, off, rows, pos, w, m, v, tr, name):
    cols = p.shape[2]
    c1 = 1.0 - ADAM_B1 ** ADAM_STEP
    c2 = 1.0 - ADAM_B2 ** ADAM_STEP

    def body(pos_ref, p_ref, r_ref, w_ref, m_ref, v_ref, g_ref, d_ref, mo_ref, vo_ref):
        gv = p_ref[...]
        for j in range(3):
            gv = gv + r_ref[j].astype(F32)
        g_ref[...] = gv
        mn = ADAM_B1 * m_ref[...] + (1.0 - ADAM_B1) * gv
        vn = ADAM_B2 * v_ref[...] + (1.0 - ADAM_B2) * (gv * gv)
        mo_ref[...] = mn
        vo_ref[...] = vn
        d_ref[...] = -ADAM_LR * ((mn / c1) / (jnp.sqrt(vn / c2) + ADAM_EPS) + ADAM_WD * w_ref[...])

    tile = pl.BlockSpec((tr, cols), lambda i, pos_ref: (i, 0))
    shape = jax.ShapeDtypeStruct((rows, cols), F32)
    return pl.pallas_call(
        body, name=name,
        grid_spec=pltpu.PrefetchScalarGridSpec(
            num_scalar_prefetch=1, grid=(rows // tr,),
            in_specs=[pl.BlockSpec((None, tr, cols), lambda i, pos_ref: (pos_ref[1], i, 0)),
                      pl.BlockSpec((3, tr, cols), lambda i, pos_ref: (0, off // tr + i, 0)),
                      tile, tile, tile],
            out_specs=[tile] * 4),
        out_shape=[shape] * 4,
        compiler_params=_cp(("parallel",)))(pos, p, from_chips, w, m, v)


def _row_tile(rows, cap):
    t = min(rows, cap)
    while rows % t or t % 8:
        t -= 8
    return t


def _ada_mod(c_all, w, b, name):
    n = w.shape[1]

    def body(c_ref, w_ref, b_ref, o_ref):
        cv = c_ref[...]
        o_ref[...] = _exact_dot(cv * _sigmoid(cv), w_ref[...]) + b_ref[...]

    return pl.pallas_call(body, name=name, out_shape=jax.ShapeDtypeStruct((N_DEV, n), F32),
                          compiler_params=pltpu.CompilerParams(vmem_limit_bytes=VMEM_LIMIT))(c_all, w, b)


def _ada_grad(c_all, dmod, name):
    n = dmod.shape[1]

    def body(c_ref, d_ref, o_ref):
        cv = c_ref[...]
        o_ref[...] = _dot(cv * _sigmoid(cv), d_ref[...], 0, 0, lax.Precision.HIGHEST)

    return pl.pallas_call(body, name=name, out_shape=jax.ShapeDtypeStruct((D, n), F32),
                          compiler_params=pltpu.CompilerParams(vmem_limit_bytes=VMEM_LIMIT))(c_all, dmod)


def _adamw(w, g, m, v, name):
    rows, cols = w.shape
    tr = _row_tile(rows, 256) if rows % 8 == 0 else rows
    c1 = 1.0 - ADAM_B1 ** ADAM_STEP
    c2 = 1.0 - ADAM_B2 ** ADAM_STEP

    def body(w_ref, g_ref, m_ref, v_ref, d_ref, mo_ref, vo_ref):
        gv = g_ref[...]
        mn = ADAM_B1 * m_ref[...] + (1.0 - ADAM_B1) * gv
        vn = ADAM_B2 * v_ref[...] + (1.0 - ADAM_B2) * (gv * gv)
        mo_ref[...] = mn
        vo_ref[...] = vn
        d_ref[...] = -ADAM_LR * ((mn / c1) / (jnp.sqrt(vn / c2) + ADAM_EPS) + ADAM_WD * w_ref[...])

    spec = pl.BlockSpec((tr, cols), lambda i: (i, 0))
    shape = jax.ShapeDtypeStruct((rows, cols), F32)
    return pl.pallas_call(body, name=name, grid=(rows // tr,), in_specs=[spec] * 4, out_specs=[spec] * 3,
                          out_shape=[shape] * 3, compiler_params=_cp(("parallel",)))(w, g, m, v)


def _sum8_loss(v, loss_row, name):
    rows = v.shape[0] // N_DEV

    def body(v_ref, o_ref, l_ref):
        acc = v_ref[0:rows, :]
        for k in range(1, N_DEV):
            acc = acc + v_ref[k * rows:(k + 1) * rows, :]
        o_ref[...] = acc
        part = jnp.sum(acc[loss_row:loss_row + 8, :], axis=0, keepdims=True)
        l_ref[...] = jnp.broadcast_to(jnp.sum(part, axis=1, keepdims=True), (8, LANE))

    return pl.pallas_call(body, name=name,
                          out_shape=[jax.ShapeDtypeStruct((rows, LANE), F32), jax.ShapeDtypeStruct((8, LANE), F32)],
                          compiler_params=pltpu.CompilerParams(vmem_limit_bytes=VMEM_LIMIT))(v)


WEIGHT_NAMES = ("w_ada", "b_ada", "ffn1_norm", "ffn1_w_gate", "ffn1_w_up", "ffn1_w_down", "mix_norm", "w_in",
                "conv_w", "conv_b", "dt_bias", "a_log", "d_skip", "ssd_norm_w", "pool_w", "pool_b", "pool_scale",
                "w_out", "ffn2_norm", "ffn2_w_gate", "ffn2_w_up", "ffn2_w_down", "final_norm")

FF_SHARD = FF // N_DEV
IN_SHARD = D_IN // N_DEV
IN_SHARD_PAD = 528
OUT_SHARD = 2 * D_SSD // N_DEV
ADA_SHARD = 9 * D // N_DEV
POOL_SHARD_ROWS = 4 * 32 * POOL_GW // D
PACK = dict(gate1=(0, FF_SHARD), up1=(352, FF_SHARD), down1=(704, FF_SHARD), gate2=(1056, FF_SHARD),
            up2=(1408, FF_SHARD), down2=(1760, FF_SHARD), w_out=(2112, OUT_SHARD), w_in=(2368, IN_SHARD_PAD),
            pool_w=(2896, POOL_SHARD_ROWS))
PACK_W_ROWS = 2896
GPACK = dict(w_in=(0, IN_SHARD_PAD), w_out=(768, OUT_SHARD), pool_w=(1024, POOL_SHARD_ROWS),
             gate1=(0, FF_SHARD), up1=(352, FF_SHARD), down1=(704, FF_SHARD),
             gate2=(0, FF_SHARD), up2=(352, FF_SHARD), down2=(704, FF_SHARD))
GROUP_ROWS = 1056

SMALL_ROWS = dict(dmod=(0, 72), ffn1_norm=(72, 8), mix_norm=(80, 8), ffn2_norm=(88, 8), final_norm=(96, 8),
                  ssd_norm_w=(104, 8), pool_scale=(112, 8), conv_b=(120, 16), conv_w=(136, 64), pool_b=(200, 8),
                  ssd=(208, 3), loss=(216, 8))
SMALL_TOTAL = 224


def _rows128(v, rows):
    flat = v.reshape(-1)
    return jnp.pad(flat, (0, rows * LANE - flat.shape[0])).reshape(rows, LANE)


def _pad_lanes(v):
    return jnp.pad(v.reshape(-1), (0, LANE - v.size))


def kernel(x, c, w_ada, b_ada, ffn1_norm, ffn1_w_gate, ffn1_w_up, ffn1_w_down, mix_norm, w_in, conv_w, conv_b, dt_bias, a_log, d_skip, ssd_norm_w, pool_w, pool_b, pool_scale, w_out, ffn2_norm, ffn2_w_gate, ffn2_w_up, ffn2_w_down, final_norm, loss_target, m_w_ada, m_b_ada, m_ffn1_norm, m_ffn1_w_gate, m_ffn1_w_up, m_ffn1_w_down, m_mix_norm, m_w_in, m_conv_w, m_conv_b, m_dt_bias, m_a_log, m_d_skip, m_ssd_norm_w, m_pool_w, m_pool_b, m_pool_scale, m_w_out, m_ffn2_norm, m_ffn2_w_gate, m_ffn2_w_up, m_ffn2_w_down, m_final_norm, v_w_ada, v_b_ada, v_ffn1_norm, v_ffn1_w_gate, v_ffn1_w_up, v_ffn1_w_down, v_mix_norm, v_w_in, v_conv_w, v_conv_b, v_dt_bias, v_a_log, v_d_skip, v_ssd_norm_w, v_pool_w, v_pool_b, v_pool_scale, v_w_out, v_ffn2_norm, v_ffn2_w_gate, v_ffn2_w_up, v_ffn2_w_down, v_final_norm):
    given = dict(locals())
    w = {n: given[n] for n in WEIGHT_NAMES}
    m = {n: given["m_" + n] for n in WEIGHT_NAMES}
    v = {n: given["v_" + n] for n in WEIGHT_NAMES}
    mx, my, mc = _mesh_pos()
    me = 4 * mx + 2 * my + mc

    small = jnp.concatenate([c.reshape(-1), conv_w.reshape(-1), pool_b.reshape(-1), pool_w.reshape(-1)])
    small_rows = 280
    gs = _all_gather(_rows128(small, small_rows), [(0, small_rows)], "ag_small").reshape(N_DEV, small_rows * LANE)
    c_all = gs[:, 0:D]
    conv_w_full = gs[:, 1024:2048].reshape(N_DEV, 4, 256).transpose(1, 0, 2).reshape(4, D_XBC)
    pool_b_full = gs[:, 2048:2176].reshape(N_DEV, 4, 32).transpose(1, 0, 2).reshape(1, D_POOL)
    pool_w_full = gs[:, 2176:2176 + 32768].reshape(N_DEV, 4, 32, POOL_GW).transpose(1, 0, 2, 3).reshape(4, POOL_GW, POOL_GW).astype(BF16)

    b_ada_cols = lax.dynamic_slice(b_ada, (0, me * ADA_SHARD), (1, ADA_SHARD))
    mod_part = _ada_mod(c_all, w_ada[0], b_ada_cols, "ada_mod")
    mod_all = _all_gather(mod_part, [(0, N_DEV)], "ag_mod").reshape(N_DEV, N_DEV, ADA_SHARD)
    mod = lax.dynamic_index_in_dim(mod_all, me, axis=1, keepdims=False).reshape(9, D)

    win_t_shard = jnp.pad(w_in[0].T, ((0, IN_SHARD_PAD - IN_SHARD), (0, 0)))
    packs = (jnp.concatenate([ffn1_w_gate[0].T, ffn1_w_up[0].T], axis=0).astype(BF16),
             ffn1_w_down[0].astype(BF16),
             win_t_shard.astype(BF16),
             w_out[0].astype(BF16),
             jnp.concatenate([ffn2_w_gate[0].T, ffn2_w_up[0].T, ffn2_w_down[0]], axis=0).astype(BF16))
    packs, _ = lax.optimization_barrier((packs, c_all))
    ffn_regions = [(0, FF_SHARD), (FF_SHARD, FF_SHARD), (2 * FF_SHARD, FF_SHARD)]
    full_a = _all_gather_async(packs[0], ffn_regions[0:2], "ag_weights_ffn1_in", 1)
    full_d = _all_gather_async(packs[1], ffn_regions[0:1], "ag_weights_ffn1_out", 2)
    full_in = _all_gather_async(packs[2], [(0, IN_SHARD_PAD)], "ag_weights_in_proj", 9)
    full_out = _all_gather_async(packs[3], [(0, OUT_SHARD)], "ag_weights_out_proj", 10)
    full_2 = _all_gather_async(packs[4], ffn_regions, "ag_weights_ffn2", 11)

    def in_proj_weights(x1):
        w_i, x1 = lax.optimization_barrier((full_in, x1))
        win_g = w_i.reshape(N_DEV, IN_SHARD_PAD, D)[:, :IN_SHARD].reshape(D_IN, D)
        win_t = jnp.concatenate([win_g[0:1024], win_g[1024:3072], win_g[3088:4112], win_g[3072:3088],
                                 jnp.zeros((D_IN_PAD - D_IN, D), BF16)], axis=0)
        return x1, win_t

    def out_proj_weights(ys):
        w_o, ys = lax.optimization_barrier((full_out, ys))
        return ys, w_o

    def ffn2_weights(x2):
        w_2, x2 = lax.optimization_barrier((full_2, x2))
        return x2, (w_2, w_2, 2)

    later_weights = (in_proj_weights, out_proj_weights, ffn2_weights)

    vecs = dict(ffn1_norm=ffn1_norm, mix_norm=mix_norm, ffn2_norm=ffn2_norm, final_norm=final_norm.reshape(1, D),
                conv_w=conv_w_full, conv_b=conv_b, ssd_norm_w=ssd_norm_w, pool_b=pool_b_full, pool_scale=pool_scale,
                ssd_par=jnp.concatenate([_pad_lanes(dt_bias)[None], _pad_lanes(a_log)[None], _pad_lanes(d_skip)[None],
                                         jnp.zeros((5, LANE), F32)], axis=0))
    dx0, st, dw1, dw3, d_win, d_wout, d_pool_w = _local_step(
        x[0], loss_target[0], mod, (full_a, full_d, 0), later_weights, pool_w_full, vecs, min(512, x.shape[1]))

    dwin = jnp.concatenate([d_win[0], d_win[1], d_win[3][0:16], d_win[2]], axis=0)
    dwin = jnp.pad(dwin.reshape(N_DEV, IN_SHARD, D), ((0, 0), (0, IN_SHARD_PAD - IN_SHARD), (0, 0))).reshape(N_DEV * IN_SHARD_PAD, D)
    dwout = jnp.concatenate(d_wout, axis=0)
    dpool = d_pool_w.reshape(4, N_DEV, 32, POOL_GW).transpose(1, 0, 2, 3).reshape(N_DEV * POOL_SHARD_ROWS, D)
    pos = jnp.stack([mc, 2 * mx + my]).astype(jnp.int32)
    by_key = dict(zip(("gate1", "up1", "down1", "gate2", "up2", "down2", "w_out", "w_in", "pool_w"),
                      (*dw1, *dw3, dwout, dwin, dpool)))
    reduced = {}
    for tag, keys, cid in (("ffn2", ("gate2", "up2", "down2"), 3), ("mix", ("w_in", "w_out", "pool_w"), 5),
                           ("ffn1", ("gate1", "up1", "down1"), 7)):
        grads = [(by_key[k], *GPACK[k]) for k in keys]
        from_sibling = _rs_pair(grads, GROUP_ROWS, f"rs_pair_{tag}", cid)
        pairs = dict(zip(keys, _pair_sum(grads, from_sibling, pos, f"rs_pair_sum_{tag}")))
        from_chips = _rs_chips([(pairs[k][1], *GPACK[k]) for k in keys], GROUP_ROWS, f"rs_chips_{tag}", cid + 1)
        for k in keys:
            reduced[k] = (pairs[k][0], from_chips)

    delta, new_m, new_v, shard_grad = {}, {}, {}, {}
    fused = dict(gate1=("ffn1_w_gate", True, 176), up1=("ffn1_w_up", True, 176), down1=("ffn1_w_down", False, 176),
                 gate2=("ffn2_w_gate", True, 176), up2=("ffn2_w_up", True, 176), down2=("ffn2_w_down", False, 176),
                 w_out=("w_out", False, 128), pool_w=("pool_w", False, POOL_SHARD_ROWS))
    for k, (n, is_transposed, tr) in fused.items():
        shp = w[n].shape
        rows = GPACK[k][1]
        view = (lambda t: t[0].T) if is_transposed else (lambda t: t.reshape(rows, D))
        back = (lambda t: t.T[None]) if is_transposed else (lambda t: t.reshape(shp))
        g_, d_, m_, v_ = _chip_sum_adamw(reduced[k][0], reduced[k][1], *GPACK[k], pos, view(w[n]), view(m[n]), view(v[n]),
                                         tr, f"adamw_{n}")
        shard_grad[n], delta[n], new_m[n], new_v[n] = back(g_), back(d_), back(m_), back(v_)
    g_win_t = _chip_sum(reduced["w_in"][0], reduced["w_in"][1], *GPACK["w_in"], pos, "rs_chip_sum_w_in")[0:IN_SHARD]

    dmod = jnp.concatenate([st["ffn1"][0:3], st["mix"][0:3], st["ffn2"][0:3]], axis=0)
    sg = jnp.concatenate([
        dmod.reshape(-1), st["ffn1"][3], st["mix"][3], st["ffn2"][3], st["loss"][0], st["gn"][0], st["pool"][1],
        st["conv"][4], st["conv"][0:4].reshape(-1), st["pool"][0], st["ssd"][0:3].reshape(-1),
        jnp.zeros((5 * LANE,), F32), st["loss"][1]])
    sg_all = _all_gather(sg.reshape(SMALL_TOTAL, LANE), [(0, SMALL_TOTAL)], "ag_small_grads")
    tot, loss_b = _sum8_loss(sg_all, SMALL_ROWS["loss"][0], "small_sum")
    loss = loss_b[0, 0]
    dmod_all = sg_all.reshape(N_DEV, SMALL_TOTAL * LANE)[:, 0:9 * D]
    g_w_ada = _ada_grad(c_all, lax.dynamic_slice(dmod_all, (0, me * ADA_SHARD), (N_DEV, ADA_SHARD)), "ada_grad")

    def tot_rows(k):
        off, n = SMALL_ROWS[k]
        return tot[off:off + n].reshape(-1)

    g_conv_w = lax.dynamic_slice(tot_rows("conv_w").reshape(4, D_XBC), (0, me * 256), (4, 256))
    g_pool_b = lax.dynamic_slice(tot_rows("pool_b").reshape(4, POOL_GW), (0, me * 32), (4, 32))
    g_ssd = tot_rows("ssd").reshape(3, LANE)
    grad = {
        "w_ada": g_w_ada[None], "b_ada": tot_rows("dmod").reshape(1, 9 * D),
        "ffn1_norm": tot_rows("ffn1_norm")[None], "mix_norm": tot_rows("mix_norm")[None],
        "ffn2_norm": tot_rows("ffn2_norm")[None], "final_norm": tot_rows("final_norm"),
        "ssd_norm_w": tot_rows("ssd_norm_w")[None], "pool_scale": tot_rows("pool_scale")[None],
        "conv_b": tot_rows("conv_b")[None], "conv_w": g_conv_w[None], "pool_b": g_pool_b[None],
        "dt_bias": g_ssd[0:1, 0:N_HEADS], "a_log": g_ssd[1:2, 0:N_HEADS], "d_skip": g_ssd[2:3, 0:N_HEADS],
        "w_in": g_win_t.T[None], **shard_grad,
    }

    d_, m_, v_ = _adamw(w_ada[0], g_w_ada, m_w_ada[0], v_w_ada[0], "adamw_w_ada")
    delta["w_ada"], new_m["w_ada"], new_v["w_ada"] = d_[None], m_[None], v_[None]
    d_, m_, v_ = _adamw(w_in[0].T, g_win_t, m_w_in[0].T, v_w_in[0].T, "adamw_w_in")
    delta["w_in"], new_m["w_in"], new_v["w_in"] = d_.T[None], m_.T[None], v_.T[None]
    big = ("w_ada", "w_in") + tuple(n for n, _, _ in fused.values())
    small_names = [n for n in WEIGHT_NAMES if n not in big]
    sizes = [LANE if w[n].size < LANE else w[n].size for n in small_names]
    small_rows_adam = -(-sum(sizes) // (8 * LANE)) * 8

    def pack_small(t):
        return _rows128(jnp.concatenate([_pad_lanes(t[n]) if t[n].size < LANE else t[n].reshape(-1) for n in small_names]),
                        small_rows_adam)

    d_s, m_s, v_s = _adamw(pack_small(w), pack_small(grad), pack_small(m), pack_small(v), "adamw_small")
    off = 0
    for n, size in zip(small_names, sizes):
        for res, packed in ((delta, d_s), (new_m, m_s), (new_v, v_s)):
            res[n] = packed.reshape(-1)[off:off + w[n].size].reshape(w[n].shape)
        off += size

    return (loss, dx0[None], *[grad[n] for n in WEIGHT_NAMES], *[delta[n] for n in WEIGHT_NAMES],
            *[new_m[n] for n in WEIGHT_NAMES], *[new_v[n] for n in WEIGHT_NAMES])
```

```python
import functools
import math

import jax
import jax.numpy as jnp
from jax import lax
from jax.experimental import pallas as pl
from jax.experimental.pallas import tpu as pltpu
from jax.experimental.pallas import tpu_sc as plsc

F32 = jnp.float32
BF16 = jnp.bfloat16
MESH = pl.DeviceIdType.MESH

N_DEV = 8
D = 1024
FF = 2816
D_SSD = 1024
N_HEADS = 16
HEAD_DIM = 64
N_GROUPS = 4
N_STATE = 128
CHUNK = 128
GROUP_W = D_SSD // N_GROUPS
D_XBC = D_SSD + 2 * N_GROUPS * N_STATE
D_POOL = 1024
POOL_WINDOWS = (2, 4, 8, 16)
POOL_GW = 256
D_IN = 4112
D_MAIN = 4096
COL_Z, COL_XBC, COL_U = 0, 1024, 3072
EPS = 1e-6
FFN_RES = 0.5
LANE = 128
HALO = 16

ADAM_LR, ADAM_B1, ADAM_B2, ADAM_EPS, ADAM_WD, ADAM_STEP = 0.001, 0.9, 0.999, 1e-08, 0.01, 10

VMEM_LIMIT = 56 << 20


def _cp(sem):
    return pltpu.CompilerParams(dimension_semantics=sem, vmem_limit_bytes=VMEM_LIMIT)


def _dot(a, b, ca, cb, prec=None):
    return lax.dot_general(a, b, (((ca,), (cb,)), ((), ())), precision=prec,
                           preferred_element_type=F32)


def _exact_dot(a, b):
    return _dot(a, b, 1, 0, lax.Precision.HIGHEST)


def _sigmoid(v):
    return 1.0 / (1.0 + jnp.exp(-v))


def _silu_grad(v, sg):
    return sg * (1.0 + v * (1.0 - sg))


def _mm_nt(a, bt, tm, tn, out_dtype, name):
    m, k = a.shape
    n = bt.shape[0]

    def body(a_ref, b_ref, o_ref):
        o_ref[...] = _dot(a_ref[...], b_ref[...], 1, 1).astype(out_dtype)

    return pl.pallas_call(
        body, name=name, grid=(n // tn, m // tm),
        in_specs=[pl.BlockSpec((tm, k), lambda j, i: (i, 0)),
                  pl.BlockSpec((tn, k), lambda j, i: (j, 0))],
        out_specs=pl.BlockSpec((tm, tn), lambda j, i: (i, j)),
        out_shape=jax.ShapeDtypeStruct((m, n), out_dtype),
        compiler_params=_cp(("parallel", "parallel")))(a, bt)


def _mm_tn(a, b, tm, tk, name):
    kk, m = a.shape
    n = b.shape[1]
    nk = kk // tk
    if nk == 1:
        def whole(a_ref, b_ref, o_ref):
            o_ref[...] = _dot(a_ref[...], b_ref[...], 0, 0)

        return pl.pallas_call(
            whole, name=name, grid=(m // tm,),
            in_specs=[pl.BlockSpec((kk, tm), lambda i: (0, i)),
                      pl.BlockSpec((kk, n), lambda i: (0, 0))],
            out_specs=pl.BlockSpec((tm, n), lambda i: (i, 0)),
            out_shape=jax.ShapeDtypeStruct((m, n), F32),
            compiler_params=_cp(("parallel",)))(a, b)

    def body(a_ref, b_ref, o_ref, acc):
        k = pl.program_id(1)

        @pl.when(k == 0)
        def _():
            acc[...] = jnp.zeros_like(acc)

        acc[...] += _dot(a_ref[...], b_ref[...], 0, 0)

        @pl.when(k == nk - 1)
        def _():
            o_ref[...] = acc[...]

    return pl.pallas_call(
        body, name=name, grid=(m // tm, nk),
        in_specs=[pl.BlockSpec((tk, tm), lambda i, k: (k, i)),
                  pl.BlockSpec((tk, n), lambda i, k: (k, 0))],
        out_specs=pl.BlockSpec((tm, n), lambda i, k: (i, 0)),
        out_shape=jax.ShapeDtypeStruct((m, n), F32),
        scratch_shapes=[pltpu.VMEM((tm, n), F32)],
        compiler_params=_cp(("parallel", "arbitrary")))(a, b)


def _mm_tn_rows(parts, b, tm, name):
    kk, n = b.shape
    blocks = [a.shape[1] // tm for a in parts]
    starts = [sum(blocks[:p]) for p in range(len(parts))]

    def body(*refs):
        a_refs, b_ref, o_ref = refs[:len(parts)], refs[len(parts)], refs[len(parts) + 1]
        i = pl.program_id(0)
        for p, a_ref in enumerate(a_refs):
            @pl.when(jnp.logical_and(i >= starts[p], i < starts[p] + blocks[p]))
            def _():
                o_ref[...] = _dot(a_ref[...], b_ref[...], 0, 0)

    def part_spec(p):
        return pl.BlockSpec((kk, tm), lambda i: (0, jnp.clip(i - starts[p], 0, blocks[p] - 1)))

    return pl.pallas_call(
        body, name=name, grid=(sum(blocks),),
        in_specs=[part_spec(p) for p in range(len(parts))] + [pl.BlockSpec((kk, n), lambda i: (0, 0))],
        out_specs=pl.BlockSpec((tm, n), lambda i: (i, 0)),
        out_shape=jax.ShapeDtypeStruct((sum(blocks) * tm, n), F32),
        compiler_params=_cp(("parallel",)))(*parts, b)


def _modulated(xv, wn, mod_ref, k):
    r = lax.rsqrt(jnp.mean(xv * xv, axis=-1, keepdims=True) + EPS)
    hn = xv * r * wn
    return (hn * (1.0 + mod_ref[3 * k + 1:3 * k + 2, :]) + mod_ref[3 * k:3 * k + 1, :]).astype(BF16)


def _prenorm(x, wn, mod, k, tm, name):
    seq = x.shape[0]

    def body(x_ref, wn_ref, mod_ref, h_ref):
        h_ref[...] = _modulated(x_ref[...], wn_ref[...], mod_ref, k)

    return pl.pallas_call(
        body, name=name, grid=(seq // tm,),
        in_specs=[pl.BlockSpec((tm, D), lambda i: (i, 0)),
                  pl.BlockSpec((1, D), lambda i: (0, 0)),
                  pl.BlockSpec((9, D), lambda i: (0, 0))],
        out_specs=pl.BlockSpec((tm, D), lambda i: (i, 0)),
        out_shape=jax.ShapeDtypeStruct((seq, D), BF16),
        compiler_params=_cp(("parallel",)))(x, wn, mod)


def _norm_bwd(dh, xv, dxo, branch, wn, sc, res, stats_ref, first):
    r = lax.rsqrt(jnp.mean(xv * xv, axis=-1, keepdims=True) + EPS)
    xn = xv * r
    dhn = dh * (1.0 + sc)
    dxn = dhn * wn
    dx = dxo + r * (dxn - xn * jnp.mean(dxn * xn, axis=-1, keepdims=True))
    rows = jnp.concatenate([
        jnp.sum(dh, axis=0, keepdims=True),
        jnp.sum(dh * (xn * wn), axis=0, keepdims=True),
        jnp.sum(branch * dxo, axis=0, keepdims=True) * res,
        jnp.sum(dhn * xn, axis=0, keepdims=True),
        jnp.zeros((4, D), F32)], axis=0)

    @pl.when(first)
    def _():
        stats_ref[...] = rows

    @pl.when(jnp.logical_not(first))
    def _():
        stats_ref[...] += rows

    return dx


def _loss_head(x3, wf, tgt, mod, tm, name):
    seq = x3.shape[0]

    def body(x_ref, w_ref, t_ref, mod_ref, dx_ref, df_ref, st_ref):
        xv = x_ref[...]
        wv = w_ref[...]
        r = lax.rsqrt(jnp.mean(xv * xv, axis=-1, keepdims=True) + EPS)
        xn = xv * r
        e = xn * wv - t_ref[...]
        dy = e * (1.0 / D)
        dxn = dy * wv
        dx = r * (dxn - xn * jnp.mean(dxn * xn, axis=-1, keepdims=True))
        dx_ref[...] = dx
        df_ref[...] = (dx * (FFN_RES * mod_ref[8:9, :])).astype(BF16)
        rows = jnp.concatenate([
            jnp.sum(dy * xn, axis=0, keepdims=True),
            jnp.sum(e * e, axis=0, keepdims=True) * (0.5 / D),
            jnp.zeros((6, D), F32)], axis=0)

        @pl.when(pl.program_id(0) == 0)
        def _():
            st_ref[...] = rows

        @pl.when(pl.program_id(0) != 0)
        def _():
            st_ref[...] += rows

    return pl.pallas_call(
        body, name=name, grid=(seq // tm,),
        in_specs=[pl.BlockSpec((tm, D), lambda i: (i, 0)),
                  pl.BlockSpec((1, D), lambda i: (0, 0)),
                  pl.BlockSpec((tm, D), lambda i: (i, 0)),
                  pl.BlockSpec((9, D), lambda i: (0, 0))],
        out_specs=[pl.BlockSpec((tm, D), lambda i: (i, 0)),
                   pl.BlockSpec((tm, D), lambda i: (i, 0)),
                   pl.BlockSpec((8, D), lambda i: (0, 0))],
        out_shape=[jax.ShapeDtypeStruct((seq, D), F32), jax.ShapeDtypeStruct((seq, D), BF16),
                   jax.ShapeDtypeStruct((8, D), F32)],
        compiler_params=_cp(("arbitrary",)))(x3, wf, tgt, mod)


def _ffn_up(h, wg, wu, tm, tn, name):
    seq = h.shape[0]
    nj = FF // tn

    def body(h_ref, wg_ref, wu_ref, a_ref, pg_ref, pu_ref):
        hv = h_ref[...]
        g = _dot(hv, wg_ref[...], 1, 1)
        u = _dot(hv, wu_ref[...], 1, 1)
        sg = _sigmoid(g)
        s = g * sg
        a_ref[...] = (s * u).astype(BF16)
        pg_ref[...] = (u * _silu_grad(g, sg)).astype(BF16)
        pu_ref[...] = s.astype(BF16)

    act = pl.BlockSpec((tm, tn), lambda j, i: (i, j))
    return pl.pallas_call(
        body, name=name, grid=(nj, seq // tm),
        in_specs=[pl.BlockSpec((tm, D), lambda j, i: (i, 0)),
                  pl.BlockSpec((tn, D), lambda j, i: (wg[1] * nj + j, 0)),
                  pl.BlockSpec((tn, D), lambda j, i: (wu[1] * nj + j, 0))],
        out_specs=[act, act, act],
        out_shape=[jax.ShapeDtypeStruct((seq, FF), BF16)] * 3,
        compiler_params=_cp(("parallel", "parallel")))(h, wg[0], wu[0])


def _ffn_down(a, w, blk, x, mod, grow, wn_next, k_next, tm, name):
    seq = a.shape[0]
    chain = k_next is not None

    def body(a_ref, w_ref, x_ref, mod_ref, wn_ref, xo_ref, f_ref, *rest):
        f = _dot(a_ref[...], w_ref[...], 1, 0)
        f_ref[...] = f.astype(BF16)
        xo = x_ref[...] + (FFN_RES * mod_ref[grow:grow + 1, :]) * f
        xo_ref[...] = xo
        if chain:
            rest[0][...] = _modulated(xo, wn_ref[...], mod_ref, k_next)

    tok = pl.BlockSpec((tm, D), lambda i: (i, 0))
    return pl.pallas_call(
        body, name=name, grid=(seq // tm,),
        in_specs=[pl.BlockSpec((tm, FF), lambda i: (i, 0)),
                  pl.BlockSpec((FF, D), lambda i: (blk, 0)),
                  tok,
                  pl.BlockSpec((9, D), lambda i: (0, 0)),
                  pl.BlockSpec((1, D), lambda i: (0, 0))],
        out_specs=[tok, tok] + ([tok] if chain else []),
        out_shape=[jax.ShapeDtypeStruct((seq, D), F32), jax.ShapeDtypeStruct((seq, D), BF16)]
        + ([jax.ShapeDtypeStruct((seq, D), BF16)] if chain else []),
        compiler_params=_cp(("parallel",)))(a, w, x, mod, wn_next)


def _ffn_bwd_da(df, w, blk, pg, pu, tm, tn, name):
    seq = df.shape[0]
    nj = FF // tn

    def body(df_ref, w_ref, pg_ref, pu_ref, dg_ref, du_ref):
        da = _dot(df_ref[...], w_ref[...], 1, 1)
        dg_ref[...] = (da * pg_ref[...].astype(F32)).astype(BF16)
        du_ref[...] = (da * pu_ref[...].astype(F32)).astype(BF16)

    act = pl.BlockSpec((tm, tn), lambda j, i: (i, j))
    return pl.pallas_call(
        body, name=name, grid=(nj, seq // tm),
        in_specs=[pl.BlockSpec((tm, D), lambda j, i: (i, 0)),
                  pl.BlockSpec((tn, D), lambda j, i: (blk * nj + j, 0)),
                  act, act],
        out_specs=[act, act],
        out_shape=[jax.ShapeDtypeStruct((seq, FF), BF16)] * 2,
        compiler_params=_cp(("parallel", "parallel")))(df, w, pg, pu)


def _ffn_bwd_dh(dg, du, wg, wu, x, dxo, fb, wn, mod, k, nxt, tm, name):
    seq = x.shape[0]

    def body(dg_ref, du_ref, wg_ref, wu_ref, x_ref, dxo_ref, f_ref, wn_ref, mod_ref, dx_ref, st_ref, *rest):
        dh = _dot(dg_ref[...], wg_ref[...], 1, 0) + _dot(du_ref[...], wu_ref[...], 1, 0)
        dx = _norm_bwd(dh, x_ref[...], dxo_ref[...], f_ref[...].astype(F32), wn_ref[...],
                       mod_ref[3 * k + 1:3 * k + 2, :], FFN_RES, st_ref, pl.program_id(0) == 0)
        dx_ref[...] = dx
        if nxt is not None:
            rest[0][...] = (dx * (nxt[1] * mod_ref[nxt[0]:nxt[0] + 1, :])).astype(BF16)

    tok = pl.BlockSpec((tm, D), lambda i: (i, 0))
    act = pl.BlockSpec((tm, FF), lambda i: (i, 0))
    return pl.pallas_call(
        body, name=name, grid=(seq // tm,),
        in_specs=[act, act,
                  pl.BlockSpec((FF, D), lambda i: (wg[1], 0)),
                  pl.BlockSpec((FF, D), lambda i: (wu[1], 0)),
                  tok, tok, tok,
                  pl.BlockSpec((1, D), lambda i: (0, 0)),
                  pl.BlockSpec((9, D), lambda i: (0, 0))],
        out_specs=[tok, pl.BlockSpec((8, D), lambda i: (0, 0))] + ([tok] if nxt is not None else []),
        out_shape=[jax.ShapeDtypeStruct((seq, D), F32), jax.ShapeDtypeStruct((8, D), F32)]
        + ([jax.ShapeDtypeStruct((seq, D), BF16)] if nxt is not None else []),
        compiler_params=_cp(("arbitrary",)))(dg, du, wg[0], wu[0], x, dxo, fb, wn, mod)


def _ffn_forward(x, h, w, mod, k, wn_next, k_next, tm, tag):
    a, pg, pu = _ffn_up(h, w["gate"], w["up"], tm, FF // 2, f"{tag}_up")
    outs = _ffn_down(a, *w["down"], x, mod, 3 * k + 2, wn_next, k_next, tm, f"{tag}_down")
    return outs[0], (outs[2] if k_next is not None else None), (x, h, pg, pu, a, outs[1])


def _ffn_backward(dxo, df, saved, w, wn, mod, k, nxt, tm, tag):
    x, h, pg, pu, a, fb = saved
    dg, du = _ffn_bwd_da(df, *w["down"], pg, pu, tm, FF // 2, f"{tag}_bwd_da")
    seq = x.shape[0]
    d_gate_t = _mm_tn(dg, h, 256, seq, f"{tag}_dw_gate")
    d_up_t = _mm_tn(du, h, 256, seq, f"{tag}_dw_up")
    d_down = _mm_tn(a, df, 256, seq, f"{tag}_dw_down")
    dws, dg, du = lax.optimization_barrier(((d_gate_t, d_up_t, d_down), dg, du))
    outs = _ffn_bwd_dh(dg, du, w["gate"], w["up"], x, dxo, fb, wn, mod, k, nxt, min(tm, 256), f"{tag}_bwd_dh")
    return outs[0], (outs[2] if nxt is not None else None), outs[1], dws


def _prev_rows(tm, col):
    return pl.BlockSpec((HALO, 1024), lambda i, j: (jnp.maximum(i * (tm // HALO) - 1, 0), col + j))


def _conv_pre(ext, cw, cb, rows):
    pre = cb + cw[3:4, :] * ext
    for s in (1, 2, 3):
        pre = pre + cw[3 - s:4 - s, :] * pltpu.roll(ext, s, 0)
    return pre[HALO:HALO + rows]


def _conv_fwd(proj, cw, cb, tm, name):
    seq = proj.shape[0]

    def body(x_ref, p_ref, cw_ref, cb_ref, o_ref):
        prev = jnp.where(pl.program_id(0) == 0, 0.0, p_ref[...])
        ext = jnp.concatenate([prev, x_ref[...]], axis=0)
        pre = _conv_pre(ext, cw_ref[...], cb_ref[...], tm)
        o_ref[...] = pre * _sigmoid(pre)

    c0 = COL_XBC // 1024
    return pl.pallas_call(
        body, name=name, grid=(seq // tm, 2),
        in_specs=[pl.BlockSpec((tm, 1024), lambda i, j: (i, c0 + j)),
                  _prev_rows(tm, c0),
                  pl.BlockSpec((4, 1024), lambda i, j: (0, j)),
                  pl.BlockSpec((1, 1024), lambda i, j: (0, j))],
        out_specs=pl.BlockSpec((tm, 1024), lambda i, j: (i, j)),
        out_shape=jax.ShapeDtypeStruct((seq, D_XBC), F32),
        compiler_params=_cp(("parallel", "parallel")))(proj, proj, cw, cb)


def _conv_bwd(dact, proj, cw, cb, tm, name):
    seq = proj.shape[0]
    ni = seq // tm

    def body(d_ref, dn_ref, x_ref, p_ref, n_ref, cw_ref, cb_ref, o_ref, st_ref):
        i = pl.program_id(1)
        cwv = cw_ref[...]
        prev = jnp.where(i == 0, 0.0, p_ref[...])
        ext = jnp.concatenate([prev, x_ref[...], n_ref[...]], axis=0)
        pre = _conv_pre(ext, cwv, cb_ref[...], tm + HALO)
        dnext = jnp.where(i == ni - 1, 0.0, dn_ref[...])
        dext = jnp.concatenate([d_ref[...], dnext], axis=0)
        dpre = dext * _silu_grad(pre, _sigmoid(pre))
        n = tm + HALO
        dx = cwv[3:4, :] * dpre
        for s in (1, 2, 3):
            dx = dx + cwv[3 - s:4 - s, :] * pltpu.roll(dpre, n - s, 0)
        o_ref[...] = dx[:tm].astype(BF16)
        dcur = dpre[:tm]
        rows = [jnp.sum(dcur * pltpu.roll(ext, 3 - k, 0)[HALO:HALO + tm], axis=0, keepdims=True) for k in range(3)]
        rows.append(jnp.sum(dcur * ext[HALO:HALO + tm], axis=0, keepdims=True))
        rows.append(jnp.sum(dcur, axis=0, keepdims=True))
        rows.append(jnp.zeros((3, 1024), F32))
        rows = jnp.concatenate(rows, axis=0)

        @pl.when(i == 0)
        def _():
            st_ref[...] = rows

        @pl.when(i != 0)
        def _():
            st_ref[...] += rows

    c0 = COL_XBC // 1024
    return pl.pallas_call(
        body, name=name, grid=(2, ni),
        in_specs=[pl.BlockSpec((tm, 1024), lambda j, i: (i, j)),
                  pl.BlockSpec((HALO, 1024), lambda j, i: (jnp.minimum((i + 1) * (tm // HALO), seq // HALO - 1), j)),
                  pl.BlockSpec((tm, 1024), lambda j, i: (i, c0 + j)),
                  pl.BlockSpec((HALO, 1024), lambda j, i: (jnp.maximum(i * (tm // HALO) - 1, 0), c0 + j)),
                  pl.BlockSpec((HALO, 1024), lambda j, i: (jnp.minimum((i + 1) * (tm // HALO), seq // HALO - 1), c0 + j)),
                  pl.BlockSpec((4, 1024), lambda j, i: (0, j)),
                  pl.BlockSpec((1, 1024), lambda j, i: (0, j))],
        out_specs=[pl.BlockSpec((tm, 1024), lambda j, i: (i, j)),
                   pl.BlockSpec((8, 1024), lambda j, i: (0, j))],
        out_shape=[jax.ShapeDtypeStruct((seq, D_XBC), BF16), jax.ShapeDtypeStruct((8, D_XBC), F32)],
        compiler_params=_cp(("parallel", "arbitrary")))(dact, dact, proj, proj, proj, cw, cb)


def _bf16_parts(x, n):
    parts, rest = [], x
    for _ in range(n):
        p = rest.astype(BF16)
        parts.append(p)
        rest = rest - p.astype(F32)
    return parts


def _pick(x, sel, n):
    m = x.shape[0]
    prod = _dot(jnp.concatenate(_bf16_parts(x, n), axis=0), sel, 1, 0)
    acc = prod[0:m]
    for i in range(1, n):
        acc = acc + prod[i * m:(i + 1) * m]
    return acc


def _running(mask, x, n):
    k = x.shape[1]
    prod = _dot(mask, jnp.concatenate(_bf16_parts(x, n), axis=1), 1, 0)
    acc = prod[:, 0:k]
    for i in range(1, n):
        acc = acc + prod[:, i * k:(i + 1) * k]
    return acc


def _head_expand():
    r = lax.broadcasted_iota(jnp.int32, (LANE, D_SSD), 0)
    c = lax.broadcasted_iota(jnp.int32, (LANE, D_SSD), 1)
    return (c // HEAD_DIM == r).astype(BF16)


def _head_reduce():
    r = lax.broadcasted_iota(jnp.int32, (D_SSD, LANE), 0)
    c = lax.broadcasted_iota(jnp.int32, (D_SSD, LANE), 1)
    return (r // HEAD_DIM == c).astype(BF16)


def _ssd_common(dtr, par):
    q = CHUNK
    v = dtr + par[0:1, :]
    dt = jnp.maximum(v, 0.0) + jnp.log(1.0 + jnp.exp(-jnp.abs(v)))
    a = -jnp.exp(par[1:2, :])
    adt = dt * a
    li = lax.broadcasted_iota(jnp.int32, (q, q), 0)
    si = lax.broadcasted_iota(jnp.int32, (q, q), 1)
    causal = li >= si
    acs = _running(causal.astype(BF16), adt, 3)
    expand = _head_expand()
    both_l = _pick(jnp.concatenate([dt, acs], axis=0), expand, 3)
    dt_l, acs_l = both_l[0:q], both_l[q:2 * q]
    dskip_l = _pick(jnp.broadcast_to(par[2:3, :], (16, LANE)), expand, 3)[0:1, :]
    last_l = acs_l[q - 1:q, :]
    return dict(v=v, dt=dt, a=a, acs=acs, acs_t=acs.T, causal=causal, dt_l=dt_l, acs_l=acs_l,
                ea_l=jnp.exp(acs_l), ds_l=jnp.exp(last_l - acs_l), cd_l=jnp.exp(last_l), dskip_l=dskip_l)


def _decay(cm, h):
    seg = cm["acs"][:, h:h + 1] - cm["acs_t"][h:h + 1, :]
    return jnp.exp(jnp.where(cm["causal"], seg, -jnp.inf))


def _lane_mask(r):
    lane = lax.broadcasted_iota(jnp.int32, (1, GROUP_W), 1)
    return lane // HEAD_DIM == r


def _ssd_fwd(xbc, proj, par, name):
    seq = xbc.shape[0]
    nc = seq // CHUNK
    q = CHUNK

    def body(x_ref, dt_ref, par_ref, y_ref, hp_ref, state):
        @pl.when(pl.program_id(0) == 0)
        def _():
            state[...] = jnp.zeros_like(state)

        cm = _ssd_common(dt_ref[...], par_ref[...])
        for g in range(N_GROUPS):
            lo = g * GROUP_W
            xs = x_ref[:, lo:lo + GROUP_W]
            bm = x_ref[:, D_SSD + g * N_STATE:D_SSD + (g + 1) * N_STATE].astype(BF16)
            cmat = x_ref[:, D_SSD + N_GROUPS * N_STATE + g * N_STATE:D_SSD + N_GROUPS * N_STATE + (g + 1) * N_STATE].astype(BF16)
            xdt = xs * cm["dt_l"][:, lo:lo + GROUP_W]
            xdt_b = xdt.astype(BF16)
            cb = _dot(cmat, bm, 1, 1)
            yd = jnp.zeros((q, GROUP_W), F32)
            for r in range(4):
                s_h = (cb * _decay(cm, 4 * g + r)).astype(BF16)
                yd = jnp.where(_lane_mask(r), _dot(s_h, xdt_b, 1, 0), yd)
            hg = state[g]
            hp_ref[0, g] = hg
            yo = _dot(cmat, hg.astype(BF16), 1, 0) * cm["ea_l"][:, lo:lo + GROUP_W]
            y_ref[:, lo:lo + GROUP_W] = yd + yo + cm["dskip_l"][:, lo:lo + GROUP_W] * xs
            xds = (xdt * cm["ds_l"][:, lo:lo + GROUP_W]).astype(BF16)
            state[g] = hg * cm["cd_l"][:, lo:lo + GROUP_W] + _dot(bm, xds, 0, 0)

    return pl.pallas_call(
        body, name=name, grid=(nc,),
        in_specs=[pl.BlockSpec((q, D_XBC), lambda c: (c, 0)),
                  pl.BlockSpec((q, LANE), lambda c: (c, 0)),
                  pl.BlockSpec((8, LANE), lambda c: (0, 0))],
        out_specs=[pl.BlockSpec((q, D_SSD), lambda c: (c, 0)),
                   pl.BlockSpec((1, N_GROUPS, N_STATE, GROUP_W), lambda c: (c, 0, 0, 0))],
        out_shape=[jax.ShapeDtypeStruct((seq, D_SSD), F32),
                   jax.ShapeDtypeStruct((nc, N_GROUPS, N_STATE, GROUP_W), F32)],
        scratch_shapes=[pltpu.VMEM((N_GROUPS, N_STATE, GROUP_W), F32)],
        compiler_params=_cp(("arbitrary",)))(xbc, proj, par)


def _ssd_bwd(dy, xbc, proj, par, hprev, name):
    seq = xbc.shape[0]
    nc = seq // CHUNK
    q = CHUNK

    def body(dy_ref, x_ref, dt_ref, par_ref, hp_ref, dx_ref, ddt_ref, st_ref, dstate):
        step = pl.program_id(0)

        @pl.when(step == 0)
        def _():
            dstate[...] = jnp.zeros_like(dstate)

        par = par_ref[...]
        cm = _ssd_common(dt_ref[...], par)
        reduce = _head_reduce()
        lane128 = lax.broadcasted_iota(jnp.int32, (1, LANE), 1)
        row128 = lax.broadcasted_iota(jnp.int32, (LANE, 1), 0)
        d_acs = jnp.zeros((q, LANE), F32)
        d_acs_t = jnp.zeros((LANE, q), F32)
        last_terms = []
        acs_terms = []
        dxdt_all = []
        for g in range(N_GROUPS):
            lo = g * GROUP_W
            sl = slice(lo, lo + GROUP_W)
            xs = x_ref[:, sl]
            bm32 = x_ref[:, D_SSD + g * N_STATE:D_SSD + (g + 1) * N_STATE]
            cm32 = x_ref[:, D_SSD + N_GROUPS * N_STATE + g * N_STATE:D_SSD + N_GROUPS * N_STATE + (g + 1) * N_STATE]
            bm = bm32.astype(BF16)
            cmat = cm32.astype(BF16)
            dyg = dy_ref[:, sl]
            dyg_b = dyg.astype(BF16)
            xdt = xs * cm["dt_l"][:, sl]
            xdt_b = xdt.astype(BF16)
            hg = hp_ref[0, g]
            hg_b = hg.astype(BF16)
            dhg = dstate[g]
            dhg_b = dhg.astype(BF16)
            ea = cm["ea_l"][:, sl]
            ds = cm["ds_l"][:, sl]
            cd = cm["cd_l"][:, sl]
            yoff = _dot(cmat, hg_b, 1, 0) * ea
            dw = (dyg * ea).astype(BF16)
            d_c = _dot(dw, hg_b, 1, 1)
            d_hprev = _dot(cmat, dw, 0, 0) + dhg * cd
            t_acs = dyg * yoff
            d_last_g = jnp.sum(dhg * hg, axis=0, keepdims=True) * cd
            xds_b = (xdt * ds).astype(BF16)
            dxds = _dot(bm, dhg_b, 1, 0)
            d_b = _dot(xds_b, dhg_b, 1, 1)
            dxdt = dxds * ds
            t_ds = dxds * xdt * ds
            t_acs = t_acs - t_ds
            d_last_g = d_last_g + jnp.sum(t_ds, axis=0, keepdims=True)
            cb = _dot(cmat, bm, 1, 1)
            d_cb = jnp.zeros((q, q), F32)
            for r in range(4):
                h = 4 * g + r
                dec = _decay(cm, h)
                s_h = cb * dec
                mask = _lane_mask(r)
                d_s = _dot(jnp.where(mask, dyg, 0.0).astype(BF16), xdt_b, 1, 1)
                dxdt = dxdt + jnp.where(mask, _dot(s_h.astype(BF16), dyg_b, 0, 0), 0.0)
                d_cb = d_cb + d_s * dec
                d_m = d_s * s_h
                d_acs = d_acs + jnp.where(lane128 == h, jnp.sum(d_m, axis=1, keepdims=True), 0.0)
                d_acs_t = d_acs_t + jnp.where(row128 == h, jnp.sum(d_m, axis=0, keepdims=True), 0.0)
            d_cb_b = d_cb.astype(BF16)
            d_c = d_c + _dot(d_cb_b, bm, 1, 0)
            d_b = d_b + _dot(d_cb_b, cmat, 0, 0)
            dstate[g] = d_hprev
            dx_ref[:, sl] = dxdt * cm["dt_l"][:, sl] + cm["dskip_l"][:, sl] * dyg
            dx_ref[:, D_SSD + g * N_STATE:D_SSD + (g + 1) * N_STATE] = d_b
            dx_ref[:, D_SSD + N_GROUPS * N_STATE + g * N_STATE:D_SSD + N_GROUPS * N_STATE + (g + 1) * N_STATE] = d_c
            acs_terms.append(t_acs)
            dxdt_all.append(dxdt * xs)
            last_terms.append(d_last_g)
        t_acs_l = jnp.concatenate(acs_terms, axis=1)
        d_dt_l = jnp.concatenate(dxdt_all, axis=1)
        d_last_l = jnp.concatenate(last_terms, axis=1)
        per_head = _pick(jnp.concatenate([t_acs_l, d_dt_l], axis=0), reduce, 2)
        skip_l = jnp.sum(dy_ref[...] * x_ref[:, 0:D_SSD], axis=0, keepdims=True)
        singles = _pick(jnp.concatenate([d_last_l, skip_l, jnp.zeros((14, D_SSD), F32)], axis=0), reduce, 3)
        d_acs = d_acs + per_head[0:q] - d_acs_t.T
        last_row = lax.broadcasted_iota(jnp.int32, (q, 1), 0) == q - 1
        d_acs = d_acs + jnp.where(last_row, singles[0:1, :], 0.0)
        li = lax.broadcasted_iota(jnp.int32, (q, q), 0)
        si = lax.broadcasted_iota(jnp.int32, (q, q), 1)
        d_adt = _running((si >= li).astype(BF16), d_acs, 3)
        d_dt = per_head[q:2 * q] + d_adt * cm["a"]
        d_dtr = d_dt * _sigmoid(cm["v"])
        ddt_ref[...] = d_dtr.astype(BF16)
        d_skip = singles[1:2, :]
        rows = jnp.concatenate([
            jnp.sum(d_dtr, axis=0, keepdims=True),
            jnp.sum(d_adt * cm["dt"], axis=0, keepdims=True) * cm["a"],
            d_skip,
            jnp.zeros((5, LANE), F32)], axis=0)

        @pl.when(step == 0)
        def _():
            st_ref[...] = rows

        @pl.when(step != 0)
        def _():
            st_ref[...] += rows

    rev = lambda c: nc - 1 - c
    return pl.pallas_call(
        body, name=name, grid=(nc,),
        in_specs=[pl.BlockSpec((q, D_SSD), lambda c: (rev(c), 0)),
                  pl.BlockSpec((q, D_XBC), lambda c: (rev(c), 0)),
                  pl.BlockSpec((q, LANE), lambda c: (rev(c), 0)),
                  pl.BlockSpec((8, LANE), lambda c: (0, 0)),
                  pl.BlockSpec((1, N_GROUPS, N_STATE, GROUP_W), lambda c: (rev(c), 0, 0, 0))],
        out_specs=[pl.BlockSpec((q, D_XBC), lambda c: (rev(c), 0)),
                   pl.BlockSpec((q, LANE), lambda c: (rev(c), 0)),
                   pl.BlockSpec((8, LANE), lambda c: (0, 0))],
        out_shape=[jax.ShapeDtypeStruct((seq, D_XBC), F32),
                   jax.ShapeDtypeStruct((seq, LANE), BF16),
                   jax.ShapeDtypeStruct((8, LANE), F32)],
        scratch_shapes=[pltpu.VMEM((N_GROUPS, N_STATE, GROUP_W), F32)],
        compiler_params=_cp(("arbitrary",)))(dy, xbc, proj, par, hprev)


def _gate_norm_fwd(y, proj, wn, tm, name):
    seq = y.shape[0]

    def body(y_ref, z_ref, w_ref, o_ref):
        for g in range(N_GROUPS):
            sl = slice(g * GROUP_W, (g + 1) * GROUP_W)
            zv = z_ref[:, sl]
            yz = y_ref[:, sl] * (zv * _sigmoid(zv))
            r = lax.rsqrt(jnp.mean(yz * yz, axis=-1, keepdims=True) + EPS)
            o_ref[:, sl] = (yz * r * w_ref[:, sl]).astype(BF16)

    tok = pl.BlockSpec((tm, D_SSD), lambda i: (i, 0))
    return pl.pallas_call(
        body, name=name, grid=(seq // tm,),
        in_specs=[tok, tok, pl.BlockSpec((1, D_SSD), lambda i: (0, 0))],
        out_specs=tok,
        out_shape=jax.ShapeDtypeStruct((seq, D_SSD), BF16),
        compiler_params=_cp(("parallel",)))(y, proj, wn)


def _gate_norm_bwd(dys, y, proj, wn, tm, name):
    seq = y.shape[0]

    def body(d_ref, y_ref, z_ref, w_ref, dy_ref, dz_ref, st_ref):
        rows = []
        for g in range(N_GROUPS):
            sl = slice(g * GROUP_W, (g + 1) * GROUP_W)
            zv = z_ref[:, sl]
            yv = y_ref[:, sl]
            sg = _sigmoid(zv)
            sz = zv * sg
            yz = yv * sz
            r = lax.rsqrt(jnp.mean(yz * yz, axis=-1, keepdims=True) + EPS)
            yn = yz * r
            dv = d_ref[:, sl]
            dyn = dv * w_ref[:, sl]
            dyz = r * (dyn - yn * jnp.mean(dyn * yn, axis=-1, keepdims=True))
            dy_ref[:, sl] = dyz * sz
            dz_ref[:, sl] = (dyz * yv * _silu_grad(zv, sg)).astype(BF16)
            rows.append(jnp.sum(dv * yn, axis=0, keepdims=True))
        rows = jnp.concatenate([jnp.concatenate(rows, axis=1), jnp.zeros((7, D_SSD), F32)], axis=0)

        @pl.when(pl.program_id(0) == 0)
        def _():
            st_ref[...] = rows

        @pl.when(pl.program_id(0) != 0)
        def _():
            st_ref[...] += rows

    tok = pl.BlockSpec((tm, D_SSD), lambda i: (i, 0))
    return pl.pallas_call(
        body, name=name, grid=(seq // tm,),
        in_specs=[tok, tok, tok, pl.BlockSpec((1, D_SSD), lambda i: (0, 0))],
        out_specs=[tok, tok, pl.BlockSpec((8, D_SSD), lambda i: (0, 0))],
        out_shape=[jax.ShapeDtypeStruct((seq, D_SSD), F32), jax.ShapeDtypeStruct((seq, D_SSD), BF16),
                   jax.ShapeDtypeStruct((8, D_SSD), F32)],
        compiler_params=_cp(("arbitrary",)))(dys, y, proj, wn)


def _pool_counts(t0, rows, w):
    pos = (t0 + 1 + lax.broadcasted_iota(jnp.int32, (rows, 1), 0)).astype(F32)
    return jnp.minimum(pos, float(w))


def _window_means(ext, t0):
    n = ext.shape[0]
    outs = []
    run = ext
    width = 1
    sums = {}
    while width < 16:
        run = run + pltpu.roll(run, width, 0)
        width *= 2
        sums[width] = run
    for g, w in enumerate(POOL_WINDOWS):
        sl = slice(g * POOL_GW, (g + 1) * POOL_GW)
        cnt = _pool_counts(t0, n - HALO, w)
        outs.append(sums[w][HALO:, sl] / cnt - ext[HALO:, sl])
    return outs


def _pool_fwd(proj, pw, pb, ps, tm, name):
    seq = proj.shape[0]

    def body(u_ref, p_ref, pw_ref, pb_ref, ps_ref, o_ref):
        i = pl.program_id(0)
        prev = jnp.where(i == 0, 0.0, p_ref[...])
        ext = jnp.concatenate([prev, u_ref[...]], axis=0)
        diffs = _window_means(ext, i * tm)
        for g in range(4):
            sl = slice(g * POOL_GW, (g + 1) * POOL_GW)
            out = _dot(diffs[g].astype(BF16), pw_ref[g], 1, 0) + pb_ref[:, sl]
            o_ref[:, sl] = (out * ps_ref[:, sl]).astype(BF16)

    c0 = COL_U // 1024
    vec = pl.BlockSpec((1, D_POOL), lambda i: (0, 0))
    return pl.pallas_call(
        body, name=name, grid=(seq // tm,),
        in_specs=[pl.BlockSpec((tm, 1024), lambda i: (i, c0)),
                  pl.BlockSpec((HALO, 1024), lambda i: (jnp.maximum(i * (tm // HALO) - 1, 0), c0)),
                  pl.BlockSpec((4, POOL_GW, POOL_GW), lambda i: (0, 0, 0)), vec, vec],
        out_specs=pl.BlockSpec((tm, D_POOL), lambda i: (i, 0)),
        out_shape=jax.ShapeDtypeStruct((seq, D_POOL), BF16),
        compiler_params=_cp(("parallel",)))(proj, proj, pw, pb, ps)


def _pool_bwd(dyp, proj, pw, pb, ps, tm, name):
    seq = proj.shape[0]
    ni = seq // tm

    def body(d_ref, dn_ref, u_ref, p_ref, pw_ref, pb_ref, ps_ref, du_ref, dw_ref, st_ref):
        i = pl.program_id(0)
        prev = jnp.where(i == 0, 0.0, p_ref[...])
        ext = jnp.concatenate([prev, u_ref[...]], axis=0)
        diffs = _window_means(ext, i * tm)
        dnext = jnp.where(i == ni - 1, 0.0, dn_ref[...])
        dext = jnp.concatenate([d_ref[...], dnext], axis=0)
        n = tm + HALO
        b_rows, s_rows = [], []
        for g, w in enumerate(POOL_WINDOWS):
            sl = slice(g * POOL_GW, (g + 1) * POOL_GW)
            wg = pw_ref[g]
            dout = dext[:, sl] * ps_ref[:, sl]
            dcur = dout[:tm]
            pre = _dot(diffs[g].astype(BF16), wg, 1, 0) + pb_ref[:, sl]
            s_rows.append(jnp.sum(d_ref[:, sl] * pre, axis=0, keepdims=True))
            b_rows.append(jnp.sum(dcur, axis=0, keepdims=True))
            dwg = _dot(diffs[g].astype(BF16), dcur.astype(BF16), 0, 0)

            @pl.when(i == 0)
            def _():
                dw_ref[g] = dwg

            @pl.when(i != 0)
            def _():
                dw_ref[g] += dwg

            ddiff = _dot(dout.astype(BF16), wg, 1, 1)
            scaled = ddiff / _pool_counts(i * tm, n, w)
            run = scaled
            width = 1
            while width < w:
                run = run + pltpu.roll(run, n - width, 0)
                width *= 2
            du_ref[:, sl] = (run[:tm] - ddiff[:tm]).astype(BF16)
        rows = jnp.concatenate([jnp.concatenate(b_rows, axis=1), jnp.concatenate(s_rows, axis=1),
                                jnp.zeros((6, D_POOL), F32)], axis=0)

        @pl.when(i == 0)
        def _():
            st_ref[...] = rows

        @pl.when(i != 0)
        def _():
            st_ref[...] += rows

    c0 = COL_U // 1024
    vec = pl.BlockSpec((1, D_POOL), lambda i: (0, 0))
    last = seq // HALO - 1
    return pl.pallas_call(
        body, name=name, grid=(ni,),
        in_specs=[pl.BlockSpec((tm, D_POOL), lambda i: (i, 0)),
                  pl.BlockSpec((HALO, D_POOL), lambda i: (jnp.minimum((i + 1) * (tm // HALO), last), 0)),
                  pl.BlockSpec((tm, 1024), lambda i: (i, c0)),
                  pl.BlockSpec((HALO, 1024), lambda i: (jnp.maximum(i * (tm // HALO) - 1, 0), c0)),
                  pl.BlockSpec((4, POOL_GW, POOL_GW), lambda i: (0, 0, 0)), vec, vec],
        out_specs=[pl.BlockSpec((tm, D_POOL), lambda i: (i, 0)),
                   pl.BlockSpec((4, POOL_GW, POOL_GW), lambda i: (0, 0, 0)),
                   pl.BlockSpec((8, D_POOL), lambda i: (0, 0))],
        out_shape=[jax.ShapeDtypeStruct((seq, D_POOL), BF16),
                   jax.ShapeDtypeStruct((4, POOL_GW, POOL_GW), F32),
                   jax.ShapeDtypeStruct((8, D_POOL), F32)],
        compiler_params=_cp(("arbitrary",)))(dyp, dyp, proj, proj, pw, pb, ps)


def _mix_out(ys, yp, wout, x1, mod, wn_next, tm, name):
    seq = ys.shape[0]

    def body(ys_ref, yp_ref, w_ref, x_ref, mod_ref, wn_ref, xo_ref, m_ref, h_ref):
        mix = _dot(ys_ref[...], w_ref[0:D_SSD, :], 1, 0) + _dot(yp_ref[...], w_ref[D_SSD:2 * D_SSD, :], 1, 0)
        m_ref[...] = mix.astype(BF16)
        xo = x_ref[...] + mod_ref[5:6, :] * mix
        xo_ref[...] = xo
        h_ref[...] = _modulated(xo, wn_ref[...], mod_ref, 2)

    tok = pl.BlockSpec((tm, D), lambda i: (i, 0))
    return pl.pallas_call(
        body, name=name, grid=(seq // tm,),
        in_specs=[tok, tok, pl.BlockSpec((2 * D_SSD, D), lambda i: (0, 0)), tok,
                  pl.BlockSpec((9, D), lambda i: (0, 0)), pl.BlockSpec((1, D), lambda i: (0, 0))],
        out_specs=[tok, tok, tok],
        out_shape=[jax.ShapeDtypeStruct((seq, D), F32), jax.ShapeDtypeStruct((seq, D), BF16),
                   jax.ShapeDtypeStruct((seq, D), BF16)],
        compiler_params=_cp(("parallel",)))(ys, yp, wout, x1, mod, wn_next)


def _mix_bwd_dh(dz, dxbc, du, ddt, w_main, w_dt, x1, dx2, mixb, wn, mod, tm, name):
    seq = x1.shape[0]

    def body(dz_ref, dx_ref, du_ref, ddt_ref, w_ref, wdt_ref, x_ref, dxo_ref, m_ref, wn_ref, mod_ref, o_ref, st_ref, df_ref):
        dh = (_dot(dz_ref[...], w_ref[COL_Z:COL_Z + 1024, :], 1, 0)
              + _dot(dx_ref[...], w_ref[COL_XBC:COL_XBC + D_XBC, :], 1, 0)
              + _dot(du_ref[...], w_ref[COL_U:COL_U + 1024, :], 1, 0)
              + _dot(ddt_ref[...], wdt_ref[...], 1, 0))
        dx = _norm_bwd(dh, x_ref[...], dxo_ref[...], m_ref[...].astype(F32), wn_ref[...],
                       mod_ref[4:5, :], 1.0, st_ref, pl.program_id(0) == 0)
        o_ref[...] = dx
        df_ref[...] = (dx * (FFN_RES * mod_ref[2:3, :])).astype(BF16)

    tok = pl.BlockSpec((tm, D), lambda i: (i, 0))
    return pl.pallas_call(
        body, name=name, grid=(seq // tm,),
        in_specs=[tok, pl.BlockSpec((tm, D_XBC), lambda i: (i, 0)), tok,
                  pl.BlockSpec((tm, LANE), lambda i: (i, 0)),
                  pl.BlockSpec((D_MAIN, D), lambda i: (0, 0)),
                  pl.BlockSpec((LANE, D), lambda i: (0, 0)),
                  tok, tok, tok,
                  pl.BlockSpec((1, D), lambda i: (0, 0)),
                  pl.BlockSpec((9, D), lambda i: (0, 0))],
        out_specs=[tok, pl.BlockSpec((8, D), lambda i: (0, 0)), tok],
        out_shape=[jax.ShapeDtypeStruct((seq, D), F32), jax.ShapeDtypeStruct((8, D), F32),
                   jax.ShapeDtypeStruct((seq, D), BF16)],
        compiler_params=_cp(("arbitrary",)))(dz, dxbc, du, ddt, w_main, w_dt, x1, dx2, mixb, wn, mod)


def _mix_bwd_dycat(dmix, wout, tm, name):
    seq = dmix.shape[0]

    def body(d_ref, w_ref, a_ref, b_ref):
        dv = d_ref[...]
        a_ref[...] = _dot(dv, w_ref[0:D_SSD, :], 1, 1)
        b_ref[...] = _dot(dv, w_ref[D_SSD:2 * D_SSD, :], 1, 1)

    tok = pl.BlockSpec((tm, D), lambda i: (i, 0))
    return pl.pallas_call(
        body, name=name, grid=(seq // tm,),
        in_specs=[tok, pl.BlockSpec((2 * D_SSD, D), lambda i: (0, 0))],
        out_specs=[tok, tok],
        out_shape=[jax.ShapeDtypeStruct((seq, D), F32)] * 2,
        compiler_params=_cp(("parallel",)))(dmix, wout)


def _local_step(x, tgt, mod, wff1, w_dt, later_weights, pool_w, vecs, tm):
    seq = x.shape[0]
    in_proj_weights, out_proj_weights, ffn2_weights = later_weights
    h1 = _prenorm(x, vecs["ffn1_norm"], mod, 0, tm, "ffn1_prenorm")
    x1, h2, s1 = _ffn_forward(x, h1, wff1, mod, 0, vecs["mix_norm"], 1, tm, "ffn1")
    x1, w_main = in_proj_weights(x1)
    proj = _mm_nt(h2, w_main, tm, 1024, F32, "mix_in_proj")
    proj_dt = _mm_nt(h2, w_dt, tm, LANE, F32, "mix_in_proj_dt")
    xbc = _conv_fwd(proj, vecs["conv_w"], vecs["conv_b"], tm, "mix_conv")
    y, hprev = _ssd_fwd(xbc, proj_dt, vecs["ssd_par"], "mix_ssd")
    ys = _gate_norm_fwd(y, proj, vecs["ssd_norm_w"], tm, "mix_gate_norm")
    yp = _pool_fwd(proj, pool_w, vecs["pool_b"], vecs["pool_scale"], tm, "mix_pool")
    ys, wout = out_proj_weights(ys)
    x2, mixb, h3 = _mix_out(ys, yp, wout, x1, mod, vecs["ffn2_norm"], tm, "mix_out_proj")
    x2, wff2 = ffn2_weights(x2)
    x3, _, s3 = _ffn_forward(x2, h3, wff2, mod, 2, vecs["ffn2_norm"], None, tm, "ffn2")
    dx3, df3, st_loss = _loss_head(x3, vecs["final_norm"], tgt, mod, tm, "loss_head")

    dx2, dmix, st3, dw3 = _ffn_backward(dx3, df3, s3, wff2, vecs["ffn2_norm"], mod, 2, (5, 1.0), tm, "ffn2")
    dys, dyp = _mix_bwd_dycat(dmix, wout, tm, "mix_bwd_dycat")
    d_wout = (_mm_tn(ys, dmix, 256, seq, "mix_dw_out_ssd"), _mm_tn(yp, dmix, 256, seq, "mix_dw_out_pool"))
    du, d_pool_w, st_pool = _pool_bwd(dyp, proj, pool_w, vecs["pool_b"], vecs["pool_scale"], tm, "mix_pool_bwd")
    dy, dz, st_gn = _gate_norm_bwd(dys, y, proj, vecs["ssd_norm_w"], tm, "mix_gate_norm_bwd")
    dxbc_act, ddt, st_ssd = _ssd_bwd(dy, xbc, proj_dt, vecs["ssd_par"], hprev, "mix_ssd_bwd")
    dxbc, st_conv = _conv_bwd(dxbc_act, proj, vecs["conv_w"], vecs["conv_b"], tm, "mix_conv_bwd")
    dx1, st2, df1 = _mix_bwd_dh(dz, dxbc, du, ddt, w_main, w_dt, x1, dx2, mixb, vecs["mix_norm"], mod, min(tm, 256),
                                "mix_bwd_dh")
    d_win = (_mm_tn_rows([dz, dxbc, du], h2, 256, "mix_dw_in"), _mm_tn(ddt, h2, LANE, seq, "mix_dw_in_dt"))
    dx0, _, st1, dw1 = _ffn_backward(dx1, df1, s1, wff1, vecs["ffn1_norm"], mod, 0, None, tm, "ffn1")
    stats = dict(ffn1=st1, mix=st2, ffn2=st3, loss=st_loss, pool=st_pool, gn=st_gn, ssd=st_ssd, conv=st_conv)
    return dx0, stats, dw1, dw3, d_win, d_wout, d_pool_w


HBM_SPEC = pl.BlockSpec(memory_space=pltpu.HBM)


def _mesh_pos():
    return lax.axis_index("x"), lax.axis_index("y"), lax.axis_index("c")


def _other_chips(x, y):
    return [(1 - x, y), (x, 1 - y), (1 - x, 1 - y)]


def _all_gather(src, regions, name):
    total, cols = src.shape
    assert sum(r for _, r in regions) == total
    body = _all_gather_body(regions, total, False)
    return pl.pallas_call(
        body, name=name,
        out_shape=jax.ShapeDtypeStruct((N_DEV * total, cols), src.dtype),
        in_specs=[HBM_SPEC], out_specs=HBM_SPEC,
        scratch_shapes=[pltpu.SemaphoreType.DMA((7,)), pltpu.SemaphoreType.DMA((7,)), pltpu.SemaphoreType.DMA],
    )(src)


def _all_gather_async(src, regions, name, collective_id):
    total, cols = src.shape
    assert sum(r for _, r in regions) == total
    return pl.kernel(
        _all_gather_body(regions, total, True), name=name,
        out_type=jax.ShapeDtypeStruct((N_DEV * total, cols), src.dtype),
        mesh=plsc.ScalarSubcoreMesh(axis_name="seq", num_cores=1),
        scratch_types=(pltpu.SemaphoreType.DMA((7,)), pltpu.SemaphoreType.DMA((7,)), pltpu.SemaphoreType.DMA),
        compiler_params=pltpu.CompilerParams(collective_id=collective_id))(src)


def _all_gather_body(regions, total, handshake):
    def body(src_ref, out_ref, send_sems, recv_sems, local_sem):
        x, y, c = _mesh_pos()
        me, sibling = (x, y, c), (x, y, 1 - c)
        chips = _other_chips(x, y)
        if handshake:
            barrier = pltpu.get_barrier_semaphore()
            for peer in [sibling] + [(*chip, c) for chip in chips]:
                pl.semaphore_signal(barrier, inc=1, device_id=peer, device_id_type=MESH)
            pl.semaphore_wait(barrier, 4)

        def rows_of(dev, off, rows):
            start = pl.multiple_of(N_DEV * off + (4 * dev[0] + 2 * dev[1] + dev[2]) * rows, 8)
            return out_ref.at[pl.ds(start, rows), :]

        def copies(k, block, to, from_src):
            out = []
            for off, rows in regions:
                dst = rows_of(block, off, rows)
                out.append(pltpu.make_async_remote_copy(
                    src_ref=src_ref.at[pl.ds(off, rows), :] if from_src else dst, dst_ref=dst,
                    send_sem=send_sems.at[k], recv_sem=recv_sems.at[k], device_id=to, device_id_type=MESH))
            return out

        def drain(k):
            whole = out_ref.at[pl.ds(0, total), :]
            return pltpu.make_async_remote_copy(src_ref=whole, dst_ref=whole, send_sem=send_sems.at[k],
                                                recv_sem=recv_sems.at[k], device_id=me, device_id_type=MESH)

        for off, rows in regions:
            pltpu.make_async_copy(src_ref.at[pl.ds(off, rows), :], rows_of(me, off, rows), local_sem).start()
        first = copies(0, me, sibling, True)
        for j, chip in enumerate(chips):
            first += copies(1 + j, me, (*chip, c), True)
        for cp in first:
            cp.start()
        for j, chip in enumerate(chips):
            drain(1 + j).wait_recv()
            for cp in copies(4 + j, (*chip, c), sibling, False):
                cp.start()
        drain(0).wait_recv()
        for j in range(3):
            drain(4 + j).wait_recv()
        for k in range(7):
            drain(k).wait_send()
        pltpu.make_async_copy(src_ref, out_ref.at[pl.ds(0, total), :], local_sem).wait()

    return body


def _rs_pair(grads, total, name, collective_id):
    cols = grads[0][0].shape[1]
    sent = sum(rows for _, _, rows in grads)
    n = len(grads)

    def body(*refs):
        g_refs, recv_ref, send_sem, recv_sem = refs[:n], refs[n], refs[n + 1], refs[n + 2]
        x, y, c = _mesh_pos()
        sibling = (x, y, 1 - c)
        barrier = pltpu.get_barrier_semaphore()
        pl.semaphore_signal(barrier, inc=1, device_id=sibling, device_id_type=MESH)
        pl.semaphore_wait(barrier, 1)
        for q in range(4):
            for g_ref, (_, off, rows) in zip(g_refs, grads):
                theirs = g_ref.at[pl.ds(pl.multiple_of((2 * q + 1 - c) * rows, 8), rows), :]
                pltpu.make_async_remote_copy(
                    src_ref=theirs, dst_ref=recv_ref.at[q, pl.ds(off, rows), :], send_sem=send_sem, recv_sem=recv_sem,
                    device_id=sibling, device_id_type=MESH).start()
        everything = recv_ref.at[:, pl.ds(0, sent), :]
        whole = pltpu.make_async_remote_copy(src_ref=everything, dst_ref=everything, send_sem=send_sem,
                                             recv_sem=recv_sem, device_id=sibling, device_id_type=MESH)
        whole.wait_send()
        whole.wait_recv()

    return pl.kernel(
        body, name=name, out_type=jax.ShapeDtypeStruct((4, total, cols), F32),
        mesh=plsc.ScalarSubcoreMesh(axis_name="seq", num_cores=1),
        scratch_types=(pltpu.SemaphoreType.DMA, pltpu.SemaphoreType.DMA),
        compiler_params=pltpu.CompilerParams(collective_id=collective_id))(*[g for g, _, _ in grads])


def _pair_sum(grads, from_sibling, pos, name):
    cols = grads[0][0].shape[1]
    n = len(grads)

    def body(pos_ref, *refs):
        for i in range(n):
            s = refs[i][...] + refs[n + i][...]
            refs[2 * n + 2 * i][...] = s
            refs[2 * n + 2 * i + 1][...] = s.astype(BF16)

    in_specs = [pl.BlockSpec((None, None, rows, cols), lambda q, pos_ref: (q, pos_ref[0], 0, 0)) for _, _, rows in grads]
    in_specs += [pl.BlockSpec((None, rows, cols), lambda q, pos_ref, blk=off // rows: (q, blk, 0)) for _, off, rows in grads]
    out_specs, out_shape = [], []
    for _, _, rows in grads:
        out_specs += [pl.BlockSpec((None, rows, cols), lambda q, pos_ref: (q, 0, 0))] * 2
        out_shape += [jax.ShapeDtypeStruct((4, rows, cols), F32), jax.ShapeDtypeStruct((4, rows, cols), BF16)]
    outs = pl.pallas_call(
        body, name=name,
        grid_spec=pltpu.PrefetchScalarGridSpec(num_scalar_prefetch=1, grid=(4,), in_specs=in_specs, out_specs=out_specs),
        out_shape=out_shape,
        compiler_params=_cp(("parallel",)))(pos, *[g.reshape(4, 2, rows, cols) for g, _, rows in grads],
                                            *[from_sibling] * n)
    return [(outs[2 * i], outs[2 * i + 1]) for i in range(n)]


def _rs_chips(parts, total, name, collective_id):
    cols = parts[0][0].shape[2]
    sent = sum(rows for _, _, rows in parts)
    n = len(parts)

    def body(*refs):
        p_refs, out_ref, send_sems, recv_sems = refs[:n], refs[n], refs[n + 1], refs[n + 2]
        x, y, c = _mesh_pos()
        chips = _other_chips(x, y)
        barrier = pltpu.get_barrier_semaphore()
        for chip in chips:
            pl.semaphore_signal(barrier, inc=1, device_id=(*chip, c), device_id_type=MESH)
        pl.semaphore_wait(barrier, 3)
        for j, chip in enumerate(chips):
            q = 2 * chip[0] + chip[1]
            for p_ref, (_, off, rows) in zip(p_refs, parts):
                pltpu.make_async_remote_copy(
                    src_ref=p_ref.at[q], dst_ref=out_ref.at[j, pl.ds(off, rows), :], send_sem=send_sems.at[j],
                    recv_sem=recv_sems.at[j], device_id=(*chip, c), device_id_type=MESH).start()
        for j, chip in enumerate(chips):
            everything = out_ref.at[j, pl.ds(0, sent), :]
            whole = pltpu.make_async_remote_copy(src_ref=everything, dst_ref=everything, send_sem=send_sems.at[j],
                                                 recv_sem=recv_sems.at[j], device_id=(*chip, c), device_id_type=MESH)
            whole.wait_recv()
            whole.wait_send()

    return pl.kernel(
        body, name=name, out_type=jax.ShapeDtypeStruct((3, total, cols), BF16),
        mesh=plsc.ScalarSubcoreMesh(axis_name="seq", num_cores=1),
        scratch_types=(pltpu.SemaphoreType.DMA((3,)), pltpu.SemaphoreType.DMA((3,))),
        compiler_params=pltpu.CompilerParams(collective_id=collective_id))(*[p for p, _, _ in parts])


def _chip_sum(p, from_chips, off, rows, pos, name):
    cols = p.shape[2]

    def body(pos_ref, p_ref, r_ref, o_ref):
        acc = p_ref[...]
        for j in range(3):
            acc = acc + r_ref[j].astype(F32)
        o_ref[...] = acc

    return pl.pallas_call(
        body, name=name,
        grid_spec=pltpu.PrefetchScalarGridSpec(
            num_scalar_prefetch=1, grid=(1,),
            in_specs=[pl.BlockSpec((None, rows, cols), lambda i, pos_ref: (pos_ref[1], 0, 0)),
                      pl.BlockSpec((3, rows, cols), lambda i, pos_ref: (0, off // rows, 0))],
            out_specs=pl.BlockSpec((rows, cols), lambda i, pos_ref: (0, 0))),
        out_shape=jax.ShapeDtypeStruct((rows, cols), F32),
        compiler_params=_cp(("arbitrary",)))(pos, p, from_chips)


def _chip_sum_adamw(p, from_chips, off, rows, pos, w, m, v, tr, name):
    cols = p.shape[2]
    c1 = 1.0 - ADAM_B1 ** ADAM_STEP
    c2 = 1.0 - ADAM_B2 ** ADAM_STEP

    def body(pos_ref, p_ref, r_ref, w_ref, m_ref, v_ref, g_ref, d_ref, mo_ref, vo_ref):
        gv = p_ref[...]
        for j in range(3):
            gv = gv + r_ref[j].astype(F32)
        g_ref[...] = gv
        mn = ADAM_B1 * m_ref[...] + (1.0 - ADAM_B1) * gv
        vn = ADAM_B2 * v_ref[...] + (1.0 - ADAM_B2) * (gv * gv)
        mo_ref[...] = mn
        vo_ref[...] = vn
        d_ref[...] = -ADAM_LR * ((mn / c1) / (jnp.sqrt(vn / c2) + ADAM_EPS) + ADAM_WD * w_ref[...])

    tile = pl.BlockSpec((tr, cols), lambda i, pos_ref: (i, 0))
    shape = jax.ShapeDtypeStruct((rows, cols), F32)
    return pl.pallas_call(
        body, name=name,
        grid_spec=pltpu.PrefetchScalarGridSpec(
            num_scalar_prefetch=1, grid=(rows // tr,),
            in_specs=[pl.BlockSpec((None, tr, cols), lambda i, pos_ref: (pos_ref[1], i, 0)),
                      pl.BlockSpec((3, tr, cols), lambda i, pos_ref: (0, off // tr + i, 0)),
                      tile, tile, tile],
            out_specs=[tile] * 4),
        out_shape=[shape] * 4,
        compiler_params=_cp(("parallel",)))(pos, p, from_chips, w, m, v)


def _row_tile(rows, cap):
    t = min(rows, cap)
    while rows % t or t % 8:
        t -= 8
    return t


def _ada_mod(c_all, w, b, name):
    n = w.shape[1]

    def body(c_ref, w_ref, b_ref, o_ref):
        cv = c_ref[...]
        o_ref[...] = _exact_dot(cv * _sigmoid(cv), w_ref[...]) + b_ref[...]

    return pl.pallas_call(body, name=name, out_shape=jax.ShapeDtypeStruct((N_DEV, n), F32),
                          compiler_params=pltpu.CompilerParams(vmem_limit_bytes=VMEM_LIMIT))(c_all, w, b)


def _ada_grad(c_all, dmod, name):
    n = dmod.shape[1]

    def body(c_ref, d_ref, o_ref):
        cv = c_ref[...]
        o_ref[...] = _dot(cv * _sigmoid(cv), d_ref[...], 0, 0, lax.Precision.HIGHEST)

    return pl.pallas_call(body, name=name, out_shape=jax.ShapeDtypeStruct((D, n), F32),
                          compiler_params=pltpu.CompilerParams(vmem_limit_bytes=VMEM_LIMIT))(c_all, dmod)


def _adamw(w, g, m, v, name):
    rows, cols = w.shape
    tr = _row_tile(rows, 256) if rows % 8 == 0 else rows
    c1 = 1.0 - ADAM_B1 ** ADAM_STEP
    c2 = 1.0 - ADAM_B2 ** ADAM_STEP

    def body(w_ref, g_ref, m_ref, v_ref, d_ref, mo_ref, vo_ref):
        gv = g_ref[...]
        mn = ADAM_B1 * m_ref[...] + (1.0 - ADAM_B1) * gv
        vn = ADAM_B2 * v_ref[...] + (1.0 - ADAM_B2) * (gv * gv)
        mo_ref[...] = mn
        vo_ref[...] = vn
        d_ref[...] = -ADAM_LR * ((mn / c1) / (jnp.sqrt(vn / c2) + ADAM_EPS) + ADAM_WD * w_ref[...])

    spec = pl.BlockSpec((tr, cols), lambda i: (i, 0))
    shape = jax.ShapeDtypeStruct((rows, cols), F32)
    return pl.pallas_call(body, name=name, grid=(rows // tr,), in_specs=[spec] * 4, out_specs=[spec] * 3,
                          out_shape=[shape] * 3, compiler_params=_cp(("parallel",)))(w, g, m, v)


def _sum8_loss(v, loss_row, name):
    rows = v.shape[0] // N_DEV

    def body(v_ref, o_ref, l_ref):
        acc = v_ref[0:rows, :]
        for k in range(1, N_DEV):
            acc = acc + v_ref[k * rows:(k + 1) * rows, :]
        o_ref[...] = acc
        part = jnp.sum(acc[loss_row:loss_row + 8, :], axis=0, keepdims=True)
        l_ref[...] = jnp.broadcast_to(jnp.sum(part, axis=1, keepdims=True), (8, LANE))

    return pl.pallas_call(body, name=name,
                          out_shape=[jax.ShapeDtypeStruct((rows, LANE), F32), jax.ShapeDtypeStruct((8, LANE), F32)],
                          compiler_params=pltpu.CompilerParams(vmem_limit_bytes=VMEM_LIMIT))(v)


WEIGHT_NAMES = ("w_ada", "b_ada", "ffn1_norm", "ffn1_w_gate", "ffn1_w_up", "ffn1_w_down", "mix_norm", "w_in",
                "conv_w", "conv_b", "dt_bias", "a_log", "d_skip", "ssd_norm_w", "pool_w", "pool_b", "pool_scale",
                "w_out", "ffn2_norm", "ffn2_w_gate", "ffn2_w_up", "ffn2_w_down", "final_norm")

FF_SHARD = FF // N_DEV
IN_SHARD = D_IN // N_DEV
MAIN_SHARD = D_MAIN // N_DEV
EDGE = 16
OUT_SHARD = 2 * D_SSD // N_DEV
ADA_SHARD = 9 * D // N_DEV
POOL_SHARD_ROWS = 4 * 32 * POOL_GW // D
GPACK = dict(w_in=(0, MAIN_SHARD), w_out=(512, OUT_SHARD), pool_w=(768, POOL_SHARD_ROWS),
             gate1=(0, FF_SHARD), up1=(352, FF_SHARD), down1=(704, FF_SHARD),
             gate2=(0, FF_SHARD), up2=(352, FF_SHARD), down2=(704, FF_SHARD))
GROUP_ROWS = 1056

SMALL_ROWS = dict(dmod=(0, 72), ffn1_norm=(72, 8), mix_norm=(80, 8), ffn2_norm=(88, 8), final_norm=(96, 8),
                  ssd_norm_w=(104, 8), pool_scale=(112, 8), conv_b=(120, 16), conv_w=(136, 64), pool_b=(200, 8),
                  ssd=(208, 3), loss=(216, 8), w_in_dt=(224, 128), w_in_head=(352, 128), w_in_tail=(480, 128))
SMALL_TOTAL = 608

MAIN_FROM_OWN = (16, 14, 12, 10, 8, 6, 20, 18)
OWN_FROM_MAIN = (16, 18, 20, 22, 24, 26, 12, 14)


def _rows128(v, rows):
    flat = v.reshape(-1)
    return jnp.pad(flat, (0, rows * LANE - flat.shape[0])).reshape(rows, LANE)


def _pad_lanes(v):
    return jnp.pad(v.reshape(-1), (0, LANE - v.size))


def kernel(x, c, w_ada, b_ada, ffn1_norm, ffn1_w_gate, ffn1_w_up, ffn1_w_down, mix_norm, w_in, conv_w, conv_b, dt_bias, a_log, d_skip, ssd_norm_w, pool_w, pool_b, pool_scale, w_out, ffn2_norm, ffn2_w_gate, ffn2_w_up, ffn2_w_down, final_norm, loss_target, m_w_ada, m_b_ada, m_ffn1_norm, m_ffn1_w_gate, m_ffn1_w_up, m_ffn1_w_down, m_mix_norm, m_w_in, m_conv_w, m_conv_b, m_dt_bias, m_a_log, m_d_skip, m_ssd_norm_w, m_pool_w, m_pool_b, m_pool_scale, m_w_out, m_ffn2_norm, m_ffn2_w_gate, m_ffn2_w_up, m_ffn2_w_down, m_final_norm, v_w_ada, v_b_ada, v_ffn1_norm, v_ffn1_w_gate, v_ffn1_w_up, v_ffn1_w_down, v_mix_norm, v_w_in, v_conv_w, v_conv_b, v_dt_bias, v_a_log, v_d_skip, v_ssd_norm_w, v_pool_w, v_pool_b, v_pool_scale, v_w_out, v_ffn2_norm, v_ffn2_w_gate, v_ffn2_w_up, v_ffn2_w_down, v_final_norm):
    given = dict(locals())
    w = {n: given[n] for n in WEIGHT_NAMES}
    m = {n: given["m_" + n] for n in WEIGHT_NAMES}
    v = {n: given["v_" + n] for n in WEIGHT_NAMES}
    mx, my, mc = _mesh_pos()
    me = 4 * mx + 2 * my + mc

    w_in_t = w_in[0].T
    small = jnp.concatenate([c.reshape(-1), conv_w.reshape(-1), pool_b.reshape(-1), pool_w.reshape(-1),
                             w_in_t[0:EDGE].reshape(-1), w_in_t[IN_SHARD - EDGE:IN_SHARD].reshape(-1)])
    small_rows = 536
    gs = _all_gather(_rows128(small, small_rows), [(0, small_rows)], "ag_small").reshape(N_DEV, small_rows * LANE)
    c_all = gs[:, 0:D]
    conv_w_full = gs[:, 1024:2048].reshape(N_DEV, 4, 256).transpose(1, 0, 2).reshape(4, D_XBC)
    pool_b_full = gs[:, 2048:2176].reshape(N_DEV, 4, 32).transpose(1, 0, 2).reshape(1, D_POOL)
    pool_w_full = gs[:, 2176:2176 + 32768].reshape(N_DEV, 4, 32, POOL_GW).transpose(1, 0, 2, 3).reshape(4, POOL_GW, POOL_GW).astype(BF16)
    heads = gs[:, 34944:34944 + EDGE * D].reshape(N_DEV, EDGE, D)
    tails = gs[:, 34944 + EDGE * D:34944 + 2 * EDGE * D].reshape(N_DEV, EDGE, D)

    prev_tail = lax.dynamic_index_in_dim(tails, jnp.maximum(me - 1, 0), axis=0, keepdims=False)
    next_head = lax.dynamic_index_in_dim(heads, jnp.minimum(me + 1, N_DEV - 1), axis=0, keepdims=False)
    first = jnp.asarray(MAIN_FROM_OWN, jnp.int32)[me]

    def window(rows, before, size):
        total = EDGE + IN_SHARD + EDGE
        padded = jnp.pad(rows, ((before, total - before - rows.shape[0]), (0, 0)))
        return lax.dynamic_slice(padded, (first, 0), (size, D))

    main_shard = (window(prev_tail, 0, MAIN_SHARD) + window(w_in_t, EDGE, MAIN_SHARD)
                  + window(next_head, EDGE + IN_SHARD, MAIN_SHARD))
    dt_rows = jnp.concatenate([tails[5], heads[6]], axis=0)[4:4 + N_HEADS]
    w_dt = jnp.pad(dt_rows, ((0, LANE - N_HEADS), (0, 0))).astype(BF16)

    b_ada_cols = lax.dynamic_slice(b_ada, (0, me * ADA_SHARD), (1, ADA_SHARD))
    mod_part = _ada_mod(c_all, w_ada[0], b_ada_cols, "ada_mod")
    mod_all = _all_gather(mod_part, [(0, N_DEV)], "ag_mod").reshape(N_DEV, N_DEV, ADA_SHARD)
    mod = lax.dynamic_index_in_dim(mod_all, me, axis=1, keepdims=False).reshape(9, D)

    packs = (jnp.concatenate([ffn1_w_gate[0].T, ffn1_w_up[0].T], axis=0).astype(BF16),
             ffn1_w_down[0].astype(BF16),
             main_shard.astype(BF16),
             w_out[0].astype(BF16),
             jnp.concatenate([ffn2_w_gate[0].T, ffn2_w_up[0].T, ffn2_w_down[0]], axis=0).astype(BF16))
    packs, _ = lax.optimization_barrier((packs, c_all))
    ffn_regions = [(0, FF_SHARD), (FF_SHARD, FF_SHARD), (2 * FF_SHARD, FF_SHARD)]
    full_a = _all_gather_async(packs[0], ffn_regions[0:2], "ag_weights_ffn1_in", 1)
    full_d = _all_gather_async(packs[1], ffn_regions[0:1], "ag_weights_ffn1_out", 2)
    full_in = _all_gather_async(packs[2], [(0, MAIN_SHARD)], "ag_weights_in_proj", 9)
    full_out = _all_gather_async(packs[3], [(0, OUT_SHARD)], "ag_weights_out_proj", 10)
    full_2 = _all_gather_async(packs[4], ffn_regions, "ag_weights_ffn2", 11)
    wff1 = dict(gate=(full_a, 0), up=(full_a, 1), down=(full_d, 0))

    def in_proj_weights(x1):
        w_i, x1 = lax.optimization_barrier((full_in, x1))
        return x1, w_i

    def out_proj_weights(ys):
        w_o, ys = lax.optimization_barrier((full_out, ys))
        return ys, w_o

    def ffn2_weights(x2):
        w_2, x2 = lax.optimization_barrier((full_2, x2))
        return x2, dict(gate=(w_2, 0), up=(w_2, 1), down=(w_2, 2))

    later_weights = (in_proj_weights, out_proj_weights, ffn2_weights)

    vecs = dict(ffn1_norm=ffn1_norm, mix_norm=mix_norm, ffn2_norm=ffn2_norm, final_norm=final_norm.reshape(1, D),
                conv_w=conv_w_full, conv_b=conv_b, ssd_norm_w=ssd_norm_w, pool_b=pool_b_full, pool_scale=pool_scale,
                ssd_par=jnp.concatenate([_pad_lanes(dt_bias)[None], _pad_lanes(a_log)[None], _pad_lanes(d_skip)[None],
                                         jnp.zeros((5, LANE), F32)], axis=0))
    dx0, st, dw1, dw3, d_win, d_wout, d_pool_w = _local_step(
        x[0], loss_target[0], mod, wff1, w_dt, later_weights, pool_w_full, vecs, min(512, x.shape[1]))

    dwin, d_w_dt = d_win
    dwout = jnp.concatenate(d_wout, axis=0)
    dpool = d_pool_w.reshape(4, N_DEV, 32, POOL_GW).transpose(1, 0, 2, 3).reshape(N_DEV * POOL_SHARD_ROWS, D)
    pos = jnp.stack([mc, 2 * mx + my]).astype(jnp.int32)
    by_key = dict(zip(("gate1", "up1", "down1", "gate2", "up2", "down2", "w_out", "w_in", "pool_w"),
                      (*dw1, *dw3, dwout, dwin, dpool)))
    reduced = {}
    for tag, keys, cid in (("ffn2", ("gate2", "up2", "down2"), 3), ("mix", ("w_in", "w_out", "pool_w"), 5),
                           ("ffn1", ("gate1", "up1", "down1"), 7)):
        grads = [(by_key[k], *GPACK[k]) for k in keys]
        from_sibling = _rs_pair(grads, GROUP_ROWS, f"rs_pair_{tag}", cid)
        pairs = dict(zip(keys, _pair_sum(grads, from_sibling, pos, f"rs_pair_sum_{tag}")))
        from_chips = _rs_chips([(pairs[k][1], *GPACK[k]) for k in keys], GROUP_ROWS, f"rs_chips_{tag}", cid + 1)
        for k in keys:
            reduced[k] = (pairs[k][0], from_chips)

    delta, new_m, new_v, shard_grad = {}, {}, {}, {}
    fused = dict(gate1=("ffn1_w_gate", True, 176), up1=("ffn1_w_up", True, 176), down1=("ffn1_w_down", False, 176),
                 gate2=("ffn2_w_gate", True, 176), up2=("ffn2_w_up", True, 176), down2=("ffn2_w_down", False, 176),
                 w_out=("w_out", False, 128), pool_w=("pool_w", False, POOL_SHARD_ROWS))
    for k, (n, is_transposed, tr) in fused.items():
        shp = w[n].shape
        rows = GPACK[k][1]
        view = (lambda t: t[0].T) if is_transposed else (lambda t: t.reshape(rows, D))
        back = (lambda t: t.T[None]) if is_transposed else (lambda t: t.reshape(shp))
        g_, d_, m_, v_ = _chip_sum_adamw(reduced[k][0], reduced[k][1], *GPACK[k], pos, view(w[n]), view(m[n]), view(v[n]),
                                         tr, f"adamw_{n}")
        shard_grad[n], delta[n], new_m[n], new_v[n] = back(g_), back(d_), back(m_), back(v_)
    g_main = _chip_sum(reduced["w_in"][0], reduced["w_in"][1], *GPACK["w_in"], pos, "rs_chip_sum_w_in")

    dmod = jnp.concatenate([st["ffn1"][0:3], st["mix"][0:3], st["ffn2"][0:3]], axis=0)
    sg = jnp.concatenate([
        dmod.reshape(-1), st["ffn1"][3], st["mix"][3], st["ffn2"][3], st["loss"][0], st["gn"][0], st["pool"][1],
        st["conv"][4], st["conv"][0:4].reshape(-1), st["pool"][0], st["ssd"][0:3].reshape(-1),
        jnp.zeros((5 * LANE,), F32), st["loss"][1],
        d_w_dt[0:N_HEADS].reshape(-1), g_main[0:EDGE].reshape(-1), g_main[MAIN_SHARD - EDGE:MAIN_SHARD].reshape(-1)])
    sg_all = _all_gather(sg.reshape(SMALL_TOTAL, LANE), [(0, SMALL_TOTAL)], "ag_small_grads")
    tot, loss_b = _sum8_loss(sg_all, SMALL_ROWS["loss"][0], "small_sum")
    loss = loss_b[0, 0]
    per_dev = sg_all.reshape(N_DEV, SMALL_TOTAL * LANE)
    dmod_all = per_dev[:, 0:9 * D]
    g_w_ada = _ada_grad(c_all, lax.dynamic_slice(dmod_all, (0, me * ADA_SHARD), (N_DEV, ADA_SHARD)), "ada_grad")

    def edge_rows(k):
        off, n = SMALL_ROWS[k]
        return per_dev[:, off * LANE:(off + n) * LANE].reshape(N_DEV, EDGE, D)

    g_dt = tot[SMALL_ROWS["w_in_dt"][0]:SMALL_ROWS["w_in_dt"][0] + SMALL_ROWS["w_in_dt"][1]].reshape(N_HEADS, D)
    before = jnp.where(me == 6, g_dt, edge_rows("w_in_tail")[6])
    after = jnp.where(me == 5, g_dt, lax.dynamic_index_in_dim(edge_rows("w_in_head"), jnp.minimum(me + 1, N_DEV - 1),
                                                               axis=0, keepdims=False))
    first_own = jnp.asarray(OWN_FROM_MAIN, jnp.int32)[me]

    def own_window(rows, lead):
        total = EDGE + MAIN_SHARD + EDGE
        padded = jnp.pad(rows, ((lead, total - lead - rows.shape[0]), (0, 0)))
        return lax.dynamic_slice(padded, (first_own, 0), (IN_SHARD, D))

    g_win_t = own_window(before, 0) + own_window(g_main, EDGE) + own_window(after, EDGE + MAIN_SHARD)

    def tot_rows(k):
        off, n = SMALL_ROWS[k]
        return tot[off:off + n].reshape(-1)

    g_conv_w = lax.dynamic_slice(tot_rows("conv_w").reshape(4, D_XBC), (0, me * 256), (4, 256))
    g_pool_b = lax.dynamic_slice(tot_rows("pool_b").reshape(4, POOL_GW), (0, me * 32), (4, 32))
    g_ssd = tot_rows("ssd").reshape(3, LANE)
    grad = {
        "w_ada": g_w_ada[None], "b_ada": tot_rows("dmod").reshape(1, 9 * D),
        "ffn1_norm": tot_rows("ffn1_norm")[None], "mix_norm": tot_rows("mix_norm")[None],
        "ffn2_norm": tot_rows("ffn2_norm")[None], "final_norm": tot_rows("final_norm"),
        "ssd_norm_w": tot_rows("ssd_norm_w")[None], "pool_scale": tot_rows("pool_scale")[None],
        "conv_b": tot_rows("conv_b")[None], "conv_w": g_conv_w[None], "pool_b": g_pool_b[None],
        "dt_bias": g_ssd[0:1, 0:N_HEADS], "a_log": g_ssd[1:2, 0:N_HEADS], "d_skip": g_ssd[2:3, 0:N_HEADS],
        "w_in": g_win_t.T[None], **shard_grad,
    }

    d_, m_, v_ = _adamw(w_ada[0], g_w_ada, m_w_ada[0], v_w_ada[0], "adamw_w_ada")
    delta["w_ada"], new_m["w_ada"], new_v["w_ada"] = d_[None], m_[None], v_[None]
    d_, m_, v_ = _adamw(w_in[0].T, g_win_t, m_w_in[0].T, v_w_in[0].T, "adamw_w_in")
    delta["w_in"], new_m["w_in"], new_v["w_in"] = d_.T[None], m_.T[None], v_.T[None]
    big = ("w_ada", "w_in") + tuple(n for n, _, _ in fused.values())
    small_names = [n for n in WEIGHT_NAMES if n not in big]
    sizes = [LANE if w[n].size < LANE else w[n].size for n in small_names]
    small_rows_adam = -(-sum(sizes) // (8 * LANE)) * 8

    def pack_small(t):
        return _rows128(jnp.concatenate([_pad_lanes(t[n]) if t[n].size < LANE else t[n].reshape(-1) for n in small_names]),
                        small_rows_adam)

    d_s, m_s, v_s = _adamw(pack_small(w), pack_small(grad), pack_small(m), pack_small(v), "adamw_small")
    off = 0
    for n, size in zip(small_names, sizes):
        for res, packed in ((delta, d_s), (new_m, m_s), (new_v, v_s)):
            res[n] = packed.reshape(-1)[off:off + w[n].size].reshape(w[n].shape)
        off += size

    return (loss, dx0[None], *[grad[n] for n in WEIGHT_NAMES], *[delta[n] for n in WEIGHT_NAMES],
            *[new_m[n] for n in WEIGHT_NAMES], *[new_v[n] for n in WEIGHT_NAMES])
```

```python
import functools
import math

import jax
import jax.numpy as jnp
from jax import lax
from jax.experimental import pallas as pl
from jax.experimental.pallas import tpu as pltpu
from jax.experimental.pallas import tpu_sc as plsc

F32 = jnp.float32
BF16 = jnp.bfloat16
MESH = pl.DeviceIdType.MESH

N_DEV = 8
D = 1024
FF = 2816
D_SSD = 1024
N_HEADS = 16
HEAD_DIM = 64
N_GROUPS = 4
N_STATE = 128
CHUNK = 128
GROUP_W = D_SSD // N_GROUPS
D_XBC = D_SSD + 2 * N_GROUPS * N_STATE
D_POOL = 1024
POOL_WINDOWS = (2, 4, 8, 16)
POOL_GW = 256
D_IN = 4112
D_MAIN = 4096
COL_Z, COL_XBC, COL_U = 0, 1024, 3072
EPS = 1e-6
FFN_RES = 0.5
LANE = 128
HALO = 16

ADAM_LR, ADAM_B1, ADAM_B2, ADAM_EPS, ADAM_WD, ADAM_STEP = 0.001, 0.9, 0.999, 1e-08, 0.01, 10

VMEM_LIMIT = 56 << 20


def _cp(sem):
    return pltpu.CompilerParams(dimension_semantics=sem, vmem_limit_bytes=VMEM_LIMIT)


def _dot(a, b, ca, cb, prec=None):
    return lax.dot_general(a, b, (((ca,), (cb,)), ((), ())), precision=prec,
                           preferred_element_type=F32)


def _exact_dot(a, b):
    return _dot(a, b, 1, 0, lax.Precision.HIGHEST)


def _sigmoid(v):
    return 1.0 / (1.0 + jnp.exp(-v))


def _silu_grad(v, sg):
    return sg * (1.0 + v * (1.0 - sg))


def _mm_nt(a, bt, tm, tn, out_dtype, name):
    m, k = a.shape
    n = bt.shape[0]

    def body(a_ref, b_ref, o_ref):
        o_ref[...] = _dot(a_ref[...], b_ref[...], 1, 1).astype(out_dtype)

    return pl.pallas_call(
        body, name=name, grid=(n // tn, m // tm),
        in_specs=[pl.BlockSpec((tm, k), lambda j, i: (i, 0)),
                  pl.BlockSpec((tn, k), lambda j, i: (j, 0))],
        out_specs=pl.BlockSpec((tm, tn), lambda j, i: (i, j)),
        out_shape=jax.ShapeDtypeStruct((m, n), out_dtype),
        compiler_params=_cp(("parallel", "parallel")))(a, bt)


def _mm_tn(a, b, tm, tk, name):
    kk, m = a.shape
    n = b.shape[1]
    nk = kk // tk
    if nk == 1:
        def whole(a_ref, b_ref, o_ref):
            o_ref[...] = _dot(a_ref[...], b_ref[...], 0, 0)

        return pl.pallas_call(
            whole, name=name, grid=(m // tm,),
            in_specs=[pl.BlockSpec((kk, tm), lambda i: (0, i)),
                      pl.BlockSpec((kk, n), lambda i: (0, 0))],
            out_specs=pl.BlockSpec((tm, n), lambda i: (i, 0)),
            out_shape=jax.ShapeDtypeStruct((m, n), F32),
            compiler_params=_cp(("parallel",)))(a, b)

    def body(a_ref, b_ref, o_ref, acc):
        k = pl.program_id(1)

        @pl.when(k == 0)
        def _():
            acc[...] = jnp.zeros_like(acc)

        acc[...] += _dot(a_ref[...], b_ref[...], 0, 0)

        @pl.when(k == nk - 1)
        def _():
            o_ref[...] = acc[...]

    return pl.pallas_call(
        body, name=name, grid=(m // tm, nk),
        in_specs=[pl.BlockSpec((tk, tm), lambda i, k: (k, i)),
                  pl.BlockSpec((tk, n), lambda i, k: (k, 0))],
        out_specs=pl.BlockSpec((tm, n), lambda i, k: (i, 0)),
        out_shape=jax.ShapeDtypeStruct((m, n), F32),
        scratch_shapes=[pltpu.VMEM((tm, n), F32)],
        compiler_params=_cp(("parallel", "arbitrary")))(a, b)


def _mm_tn_rows(parts, b, tm, name):
    kk, n = b.shape
    blocks = [a.shape[1] // tm for a in parts]
    starts = [sum(blocks[:p]) for p in range(len(parts))]

    def body(*refs):
        a_refs, b_ref, o_ref = refs[:len(parts)], refs[len(parts)], refs[len(parts) + 1]
        i = pl.program_id(0)
        for p, a_ref in enumerate(a_refs):
            @pl.when(jnp.logical_and(i >= starts[p], i < starts[p] + blocks[p]))
            def _():
                o_ref[...] = _dot(a_ref[...], b_ref[...], 0, 0)

    def part_spec(p):
        return pl.BlockSpec((kk, tm), lambda i: (0, jnp.clip(i - starts[p], 0, blocks[p] - 1)))

    return pl.pallas_call(
        body, name=name, grid=(sum(blocks),),
        in_specs=[part_spec(p) for p in range(len(parts))] + [pl.BlockSpec((kk, n), lambda i: (0, 0))],
        out_specs=pl.BlockSpec((tm, n), lambda i: (i, 0)),
        out_shape=jax.ShapeDtypeStruct((sum(blocks) * tm, n), F32),
        compiler_params=_cp(("parallel",)))(*parts, b)


def _modulated(xv, wn, mod_ref, k):
    r = lax.rsqrt(jnp.mean(xv * xv, axis=-1, keepdims=True) + EPS)
    hn = xv * r * wn
    return (hn * (1.0 + mod_ref[3 * k + 1:3 * k + 2, :]) + mod_ref[3 * k:3 * k + 1, :]).astype(BF16)


def _prenorm(x, wn, mod, k, tm, name):
    seq = x.shape[0]

    def body(x_ref, wn_ref, mod_ref, h_ref):
        h_ref[...] = _modulated(x_ref[...], wn_ref[...], mod_ref, k)

    return pl.pallas_call(
        body, name=name, grid=(seq // tm,),
        in_specs=[pl.BlockSpec((tm, D), lambda i: (i, 0)),
                  pl.BlockSpec((1, D), lambda i: (0, 0)),
                  pl.BlockSpec((9, D), lambda i: (0, 0))],
        out_specs=pl.BlockSpec((tm, D), lambda i: (i, 0)),
        out_shape=jax.ShapeDtypeStruct((seq, D), BF16),
        compiler_params=_cp(("parallel",)))(x, wn, mod)


def _norm_bwd(dh, xv, dxo, branch, wn, sc, res, stats_ref, first):
    r = lax.rsqrt(jnp.mean(xv * xv, axis=-1, keepdims=True) + EPS)
    xn = xv * r
    dhn = dh * (1.0 + sc)
    dxn = dhn * wn
    dx = dxo + r * (dxn - xn * jnp.mean(dxn * xn, axis=-1, keepdims=True))
    rows = jnp.concatenate([
        jnp.sum(dh, axis=0, keepdims=True),
        jnp.sum(dh * (xn * wn), axis=0, keepdims=True),
        jnp.sum(branch * dxo, axis=0, keepdims=True) * res,
        jnp.sum(dhn * xn, axis=0, keepdims=True),
        jnp.zeros((4, D), F32)], axis=0)

    @pl.when(first)
    def _():
        stats_ref[...] = rows

    @pl.when(jnp.logical_not(first))
    def _():
        stats_ref[...] += rows

    return dx


def _loss_head(x3, wf, tgt, mod, tm, name):
    seq = x3.shape[0]

    def body(x_ref, w_ref, t_ref, mod_ref, dx_ref, df_ref, st_ref):
        xv = x_ref[...]
        wv = w_ref[...]
        r = lax.rsqrt(jnp.mean(xv * xv, axis=-1, keepdims=True) + EPS)
        xn = xv * r
        e = xn * wv - t_ref[...]
        dy = e * (1.0 / D)
        dxn = dy * wv
        dx = r * (dxn - xn * jnp.mean(dxn * xn, axis=-1, keepdims=True))
        dx_ref[...] = dx
        df_ref[...] = (dx * (FFN_RES * mod_ref[8:9, :])).astype(BF16)
        rows = jnp.concatenate([
            jnp.sum(dy * xn, axis=0, keepdims=True),
            jnp.sum(e * e, axis=0, keepdims=True) * (0.5 / D),
            jnp.zeros((6, D), F32)], axis=0)

        @pl.when(pl.program_id(0) == 0)
        def _():
            st_ref[...] = rows

        @pl.when(pl.program_id(0) != 0)
        def _():
            st_ref[...] += rows

    return pl.pallas_call(
        body, name=name, grid=(seq // tm,),
        in_specs=[pl.BlockSpec((tm, D), lambda i: (i, 0)),
                  pl.BlockSpec((1, D), lambda i: (0, 0)),
                  pl.BlockSpec((tm, D), lambda i: (i, 0)),
                  pl.BlockSpec((9, D), lambda i: (0, 0))],
        out_specs=[pl.BlockSpec((tm, D), lambda i: (i, 0)),
                   pl.BlockSpec((tm, D), lambda i: (i, 0)),
                   pl.BlockSpec((8, D), lambda i: (0, 0))],
        out_shape=[jax.ShapeDtypeStruct((seq, D), F32), jax.ShapeDtypeStruct((seq, D), BF16),
                   jax.ShapeDtypeStruct((8, D), F32)],
        compiler_params=_cp(("arbitrary",)))(x3, wf, tgt, mod)


def _ffn_up(h, wg, wu, tm, tn, name):
    seq = h.shape[0]
    nj = FF // tn

    def body(h_ref, wg_ref, wu_ref, a_ref, pg_ref, pu_ref):
        hv = h_ref[...]
        g = _dot(hv, wg_ref[...], 1, 1)
        u = _dot(hv, wu_ref[...], 1, 1)
        sg = _sigmoid(g)
        s = g * sg
        a_ref[...] = (s * u).astype(BF16)
        pg_ref[...] = (u * _silu_grad(g, sg)).astype(BF16)
        pu_ref[...] = s.astype(BF16)

    act = pl.BlockSpec((tm, tn), lambda j, i: (i, j))
    return pl.pallas_call(
        body, name=name, grid=(nj, seq // tm),
        in_specs=[pl.BlockSpec((tm, D), lambda j, i: (i, 0)),
                  pl.BlockSpec((tn, D), lambda j, i: (wg[1] * nj + j, 0)),
                  pl.BlockSpec((tn, D), lambda j, i: (wu[1] * nj + j, 0))],
        out_specs=[act, act, act],
        out_shape=[jax.ShapeDtypeStruct((seq, FF), BF16)] * 3,
        compiler_params=_cp(("parallel", "parallel")))(h, wg[0], wu[0])


def _ffn_down(a, w, blk, x, mod, grow, wn_next, k_next, tm, name):
    seq = a.shape[0]
    chain = k_next is not None

    def body(a_ref, w_ref, x_ref, mod_ref, wn_ref, xo_ref, f_ref, *rest):
        f = _dot(a_ref[...], w_ref[...], 1, 0)
        f_ref[...] = f.astype(BF16)
        xo = x_ref[...] + (FFN_RES * mod_ref[grow:grow + 1, :]) * f
        xo_ref[...] = xo
        if chain:
            rest[0][...] = _modulated(xo, wn_ref[...], mod_ref, k_next)

    tok = pl.BlockSpec((tm, D), lambda i: (i, 0))
    return pl.pallas_call(
        body, name=name, grid=(seq // tm,),
        in_specs=[pl.BlockSpec((tm, FF), lambda i: (i, 0)),
                  pl.BlockSpec((FF, D), lambda i: (blk, 0)),
                  tok,
                  pl.BlockSpec((9, D), lambda i: (0, 0)),
                  pl.BlockSpec((1, D), lambda i: (0, 0))],
        out_specs=[tok, tok] + ([tok] if chain else []),
        out_shape=[jax.ShapeDtypeStruct((seq, D), F32), jax.ShapeDtypeStruct((seq, D), BF16)]
        + ([jax.ShapeDtypeStruct((seq, D), BF16)] if chain else []),
        compiler_params=_cp(("parallel",)))(a, w, x, mod, wn_next)


def _ffn_bwd_da(df, w, blk, pg, pu, tm, tn, name):
    seq = df.shape[0]
    nj = FF // tn

    def body(df_ref, w_ref, pg_ref, pu_ref, dg_ref, du_ref):
        da = _dot(df_ref[...], w_ref[...], 1, 1)
        dg_ref[...] = (da * pg_ref[...].astype(F32)).astype(BF16)
        du_ref[...] = (da * pu_ref[...].astype(F32)).astype(BF16)

    act = pl.BlockSpec((tm, tn), lambda j, i: (i, j))
    return pl.pallas_call(
        body, name=name, grid=(nj, seq // tm),
        in_specs=[pl.BlockSpec((tm, D), lambda j, i: (i, 0)),
                  pl.BlockSpec((tn, D), lambda j, i: (blk * nj + j, 0)),
                  act, act],
        out_specs=[act, act],
        out_shape=[jax.ShapeDtypeStruct((seq, FF), BF16)] * 2,
        compiler_params=_cp(("parallel", "parallel")))(df, w, pg, pu)


def _ffn_bwd_dh(dg, du, wg, wu, x, dxo, fb, wn, mod, k, nxt, tm, name):
    seq = x.shape[0]

    def body(dg_ref, du_ref, wg_ref, wu_ref, x_ref, dxo_ref, f_ref, wn_ref, mod_ref, dx_ref, st_ref, *rest):
        dh = _dot(dg_ref[...], wg_ref[...], 1, 0) + _dot(du_ref[...], wu_ref[...], 1, 0)
        dx = _norm_bwd(dh, x_ref[...], dxo_ref[...], f_ref[...].astype(F32), wn_ref[...],
                       mod_ref[3 * k + 1:3 * k + 2, :], FFN_RES, st_ref, pl.program_id(0) == 0)
        dx_ref[...] = dx
        if nxt is not None:
            rest[0][...] = (dx * (nxt[1] * mod_ref[nxt[0]:nxt[0] + 1, :])).astype(BF16)

    tok = pl.BlockSpec((tm, D), lambda i: (i, 0))
    act = pl.BlockSpec((tm, FF), lambda i: (i, 0))
    return pl.pallas_call(
        body, name=name, grid=(seq // tm,),
        in_specs=[act, act,
                  pl.BlockSpec((FF, D), lambda i: (wg[1], 0)),
                  pl.BlockSpec((FF, D), lambda i: (wu[1], 0)),
                  tok, tok, tok,
                  pl.BlockSpec((1, D), lambda i: (0, 0)),
                  pl.BlockSpec((9, D), lambda i: (0, 0))],
        out_specs=[tok, pl.BlockSpec((8, D), lambda i: (0, 0))] + ([tok] if nxt is not None else []),
        out_shape=[jax.ShapeDtypeStruct((seq, D), F32), jax.ShapeDtypeStruct((8, D), F32)]
        + ([jax.ShapeDtypeStruct((seq, D), BF16)] if nxt is not None else []),
        compiler_params=_cp(("arbitrary",)))(dg, du, wg[0], wu[0], x, dxo, fb, wn, mod)


def _ffn_forward(x, h, w, mod, k, wn_next, k_next, tm, tag):
    a, pg, pu = _ffn_up(h, w["gate"], w["up"], tm, FF // 2, f"{tag}_up")
    outs = _ffn_down(a, *w["down"], x, mod, 3 * k + 2, wn_next, k_next, tm, f"{tag}_down")
    return outs[0], (outs[2] if k_next is not None else None), (x, h, pg, pu, a, outs[1])


def _ffn_backward(dxo, df, saved, w, wn, mod, k, nxt, tm, tag):
    x, h, pg, pu, a, fb = saved
    dg, du = _ffn_bwd_da(df, *w["down"], pg, pu, tm, FF // 2, f"{tag}_bwd_da")
    seq = x.shape[0]
    d_gate_t = _mm_tn(dg, h, 256, seq, f"{tag}_dw_gate")
    d_up_t = _mm_tn(du, h, 256, seq, f"{tag}_dw_up")
    d_down = _mm_tn(a, df, 256, seq, f"{tag}_dw_down")
    dws, dg, du = lax.optimization_barrier(((d_gate_t, d_up_t, d_down), dg, du))
    outs = _ffn_bwd_dh(dg, du, w["gate"], w["up"], x, dxo, fb, wn, mod, k, nxt, min(tm, 256), f"{tag}_bwd_dh")
    return outs[0], (outs[2] if nxt is not None else None), outs[1], dws


def _prev_rows(tm, col):
    return pl.BlockSpec((HALO, 1024), lambda i, j: (jnp.maximum(i * (tm // HALO) - 1, 0), col + j))


def _conv_pre(ext, cw, cb, rows):
    pre = cb + cw[3:4, :] * ext
    for s in (1, 2, 3):
        pre = pre + cw[3 - s:4 - s, :] * pltpu.roll(ext, s, 0)
    return pre[HALO:HALO + rows]


def _conv_fwd(proj, cw, cb, tm, name):
    seq = proj.shape[0]

    def body(x_ref, p_ref, cw_ref, cb_ref, o_ref):
        prev = jnp.where(pl.program_id(0) == 0, 0.0, p_ref[...])
        ext = jnp.concatenate([prev, x_ref[...]], axis=0)
        pre = _conv_pre(ext, cw_ref[...], cb_ref[...], tm)
        o_ref[...] = pre * _sigmoid(pre)

    c0 = COL_XBC // 1024
    return pl.pallas_call(
        body, name=name, grid=(seq // tm, 2),
        in_specs=[pl.BlockSpec((tm, 1024), lambda i, j: (i, c0 + j)),
                  _prev_rows(tm, c0),
                  pl.BlockSpec((4, 1024), lambda i, j: (0, j)),
                  pl.BlockSpec((1, 1024), lambda i, j: (0, j))],
        out_specs=pl.BlockSpec((tm, 1024), lambda i, j: (i, j)),
        out_shape=jax.ShapeDtypeStruct((seq, D_XBC), F32),
        compiler_params=_cp(("parallel", "parallel")))(proj, proj, cw, cb)


def _conv_bwd(dact, proj, cw, cb, tm, name):
    seq = proj.shape[0]
    ni = seq // tm

    def body(d_ref, dn_ref, x_ref, p_ref, n_ref, cw_ref, cb_ref, o_ref, st_ref):
        i = pl.program_id(1)
        cwv = cw_ref[...]
        prev = jnp.where(i == 0, 0.0, p_ref[...])
        ext = jnp.concatenate([prev, x_ref[...], n_ref[...]], axis=0)
        pre = _conv_pre(ext, cwv, cb_ref[...], tm + HALO)
        dnext = jnp.where(i == ni - 1, 0.0, dn_ref[...])
        dext = jnp.concatenate([d_ref[...], dnext], axis=0)
        dpre = dext * _silu_grad(pre, _sigmoid(pre))
        n = tm + HALO
        dx = cwv[3:4, :] * dpre
        for s in (1, 2, 3):
            dx = dx + cwv[3 - s:4 - s, :] * pltpu.roll(dpre, n - s, 0)
        o_ref[...] = dx[:tm].astype(BF16)
        dcur = dpre[:tm]
        rows = [jnp.sum(dcur * pltpu.roll(ext, 3 - k, 0)[HALO:HALO + tm], axis=0, keepdims=True) for k in range(3)]
        rows.append(jnp.sum(dcur * ext[HALO:HALO + tm], axis=0, keepdims=True))
        rows.append(jnp.sum(dcur, axis=0, keepdims=True))
        rows.append(jnp.zeros((3, 1024), F32))
        rows = jnp.concatenate(rows, axis=0)

        @pl.when(i == 0)
        def _():
            st_ref[...] = rows

        @pl.when(i != 0)
        def _():
            st_ref[...] += rows

    c0 = COL_XBC // 1024
    return pl.pallas_call(
        body, name=name, grid=(2, ni),
        in_specs=[pl.BlockSpec((tm, 1024), lambda j, i: (i, j)),
                  pl.BlockSpec((HALO, 1024), lambda j, i: (jnp.minimum((i + 1) * (tm // HALO), seq // HALO - 1), j)),
                  pl.BlockSpec((tm, 1024), lambda j, i: (i, c0 + j)),
                  pl.BlockSpec((HALO, 1024), lambda j, i: (jnp.maximum(i * (tm // HALO) - 1, 0), c0 + j)),
                  pl.BlockSpec((HALO, 1024), lambda j, i: (jnp.minimum((i + 1) * (tm // HALO), seq // HALO - 1), c0 + j)),
                  pl.BlockSpec((4, 1024), lambda j, i: (0, j)),
                  pl.BlockSpec((1, 1024), lambda j, i: (0, j))],
        out_specs=[pl.BlockSpec((tm, 1024), lambda j, i: (i, j)),
                   pl.BlockSpec((8, 1024), lambda j, i: (0, j))],
        out_shape=[jax.ShapeDtypeStruct((seq, D_XBC), BF16), jax.ShapeDtypeStruct((8, D_XBC), F32)],
        compiler_params=_cp(("parallel", "arbitrary")))(dact, dact, proj, proj, proj, cw, cb)


def _bf16_parts(x, n):
    parts, rest = [], x
    for _ in range(n):
        p = rest.astype(BF16)
        parts.append(p)
        rest = rest - p.astype(F32)
    return parts


def _pick(x, sel, n):
    m = x.shape[0]
    prod = _dot(jnp.concatenate(_bf16_parts(x, n), axis=0), sel, 1, 0)
    acc = prod[0:m]
    for i in range(1, n):
        acc = acc + prod[i * m:(i + 1) * m]
    return acc


def _running(mask, x, n):
    k = x.shape[1]
    prod = _dot(mask, jnp.concatenate(_bf16_parts(x, n), axis=1), 1, 0)
    acc = prod[:, 0:k]
    for i in range(1, n):
        acc = acc + prod[:, i * k:(i + 1) * k]
    return acc


def _head_expand():
    r = lax.broadcasted_iota(jnp.int32, (LANE, D_SSD), 0)
    c = lax.broadcasted_iota(jnp.int32, (LANE, D_SSD), 1)
    return (c // HEAD_DIM == r).astype(BF16)


def _head_reduce():
    r = lax.broadcasted_iota(jnp.int32, (D_SSD, LANE), 0)
    c = lax.broadcasted_iota(jnp.int32, (D_SSD, LANE), 1)
    return (r // HEAD_DIM == c).astype(BF16)


def _ssd_common(dtr, par):
    q = CHUNK
    v = dtr + par[0:1, :]
    dt = jnp.maximum(v, 0.0) + jnp.log(1.0 + jnp.exp(-jnp.abs(v)))
    a = -jnp.exp(par[1:2, :])
    adt = dt * a
    li = lax.broadcasted_iota(jnp.int32, (q, q), 0)
    si = lax.broadcasted_iota(jnp.int32, (q, q), 1)
    causal = li >= si
    acs = _running(causal.astype(BF16), adt, 3)
    expand = _head_expand()
    both_l = _pick(jnp.concatenate([dt, acs], axis=0), expand, 3)
    dt_l, acs_l = both_l[0:q], both_l[q:2 * q]
    dskip_l = _pick(jnp.broadcast_to(par[2:3, :], (16, LANE)), expand, 3)[0:1, :]
    last_l = acs_l[q - 1:q, :]
    return dict(v=v, dt=dt, a=a, acs=acs, acs_t=acs.T, causal=causal, dt_l=dt_l, acs_l=acs_l,
                ea_l=jnp.exp(acs_l), ds_l=jnp.exp(last_l - acs_l), cd_l=jnp.exp(last_l), dskip_l=dskip_l)


def _decay(cm, h):
    seg = cm["acs"][:, h:h + 1] - cm["acs_t"][h:h + 1, :]
    return jnp.exp(jnp.where(cm["causal"], seg, -jnp.inf))


def _lane_mask(r):
    lane = lax.broadcasted_iota(jnp.int32, (1, GROUP_W), 1)
    return lane // HEAD_DIM == r


def _ssd_fwd(xbc, proj, par, name):
    seq = xbc.shape[0]
    nc = seq // CHUNK
    q = CHUNK

    def body(x_ref, dt_ref, par_ref, y_ref, hp_ref, state):
        @pl.when(pl.program_id(0) == 0)
        def _():
            state[...] = jnp.zeros_like(state)

        cm = _ssd_common(dt_ref[...], par_ref[...])
        for g in range(N_GROUPS):
            lo = g * GROUP_W
            xs = x_ref[:, lo:lo + GROUP_W]
            bm = x_ref[:, D_SSD + g * N_STATE:D_SSD + (g + 1) * N_STATE].astype(BF16)
            cmat = x_ref[:, D_SSD + N_GROUPS * N_STATE + g * N_STATE:D_SSD + N_GROUPS * N_STATE + (g + 1) * N_STATE].astype(BF16)
            xdt = xs * cm["dt_l"][:, lo:lo + GROUP_W]
            xdt_b = xdt.astype(BF16)
            cb = _dot(cmat, bm, 1, 1)
            scores = jnp.concatenate([(cb * _decay(cm, 4 * g + r)).astype(BF16) for r in range(4)], axis=0)
            yd_heads = _dot(scores, xdt_b, 1, 0)
            yd = yd_heads[0:q]
            for r in range(1, 4):
                yd = jnp.where(_lane_mask(r), yd_heads[r * q:(r + 1) * q], yd)
            hg = state[g]
            hp_ref[0, g] = hg
            yo = _dot(cmat, hg.astype(BF16), 1, 0) * cm["ea_l"][:, lo:lo + GROUP_W]
            y_ref[:, lo:lo + GROUP_W] = yd + yo + cm["dskip_l"][:, lo:lo + GROUP_W] * xs
            xds = (xdt * cm["ds_l"][:, lo:lo + GROUP_W]).astype(BF16)
            state[g] = hg * cm["cd_l"][:, lo:lo + GROUP_W] + _dot(bm, xds, 0, 0)

    return pl.pallas_call(
        body, name=name, grid=(nc,),
        in_specs=[pl.BlockSpec((q, D_XBC), lambda c: (c, 0)),
                  pl.BlockSpec((q, LANE), lambda c: (c, 0)),
                  pl.BlockSpec((8, LANE), lambda c: (0, 0))],
        out_specs=[pl.BlockSpec((q, D_SSD), lambda c: (c, 0)),
                   pl.BlockSpec((1, N_GROUPS, N_STATE, GROUP_W), lambda c: (c, 0, 0, 0))],
        out_shape=[jax.ShapeDtypeStruct((seq, D_SSD), F32),
                   jax.ShapeDtypeStruct((nc, N_GROUPS, N_STATE, GROUP_W), F32)],
        scratch_shapes=[pltpu.VMEM((N_GROUPS, N_STATE, GROUP_W), F32)],
        compiler_params=_cp(("arbitrary",)))(xbc, proj, par)


def _ssd_bwd(dy, xbc, proj, par, hprev, name):
    seq = xbc.shape[0]
    nc = seq // CHUNK
    q = CHUNK

    def body(dy_ref, x_ref, dt_ref, par_ref, hp_ref, dx_ref, ddt_ref, st_ref, dstate):
        step = pl.program_id(0)

        @pl.when(step == 0)
        def _():
            dstate[...] = jnp.zeros_like(dstate)

        par = par_ref[...]
        cm = _ssd_common(dt_ref[...], par)
        reduce = _head_reduce()
        lane128 = lax.broadcasted_iota(jnp.int32, (1, LANE), 1)
        row128 = lax.broadcasted_iota(jnp.int32, (LANE, 1), 0)
        d_acs = jnp.zeros((q, LANE), F32)
        d_acs_t = jnp.zeros((LANE, q), F32)
        last_terms = []
        acs_terms = []
        dxdt_all = []
        for g in range(N_GROUPS):
            lo = g * GROUP_W
            sl = slice(lo, lo + GROUP_W)
            xs = x_ref[:, sl]
            bm32 = x_ref[:, D_SSD + g * N_STATE:D_SSD + (g + 1) * N_STATE]
            cm32 = x_ref[:, D_SSD + N_GROUPS * N_STATE + g * N_STATE:D_SSD + N_GROUPS * N_STATE + (g + 1) * N_STATE]
            bm = bm32.astype(BF16)
            cmat = cm32.astype(BF16)
            dyg = dy_ref[:, sl]
            dyg_b = dyg.astype(BF16)
            xdt = xs * cm["dt_l"][:, sl]
            xdt_b = xdt.astype(BF16)
            hg = hp_ref[0, g]
            hg_b = hg.astype(BF16)
            dhg = dstate[g]
            dhg_b = dhg.astype(BF16)
            ea = cm["ea_l"][:, sl]
            ds = cm["ds_l"][:, sl]
            cd = cm["cd_l"][:, sl]
            yoff = _dot(cmat, hg_b, 1, 0) * ea
            dw = (dyg * ea).astype(BF16)
            d_c = _dot(dw, hg_b, 1, 1)
            d_hprev = _dot(cmat, dw, 0, 0) + dhg * cd
            t_acs = dyg * yoff
            d_last_g = jnp.sum(dhg * hg, axis=0, keepdims=True) * cd
            xds_b = (xdt * ds).astype(BF16)
            dxds = _dot(bm, dhg_b, 1, 0)
            d_b = _dot(xds_b, dhg_b, 1, 1)
            dxdt = dxds * ds
            t_ds = dxds * xdt * ds
            t_acs = t_acs - t_ds
            d_last_g = d_last_g + jnp.sum(t_ds, axis=0, keepdims=True)
            cb = _dot(cmat, bm, 1, 1)
            d_cb = jnp.zeros((q, q), F32)
            decays = [_decay(cm, 4 * g + r) for r in range(4)]
            score_heads = [cb * dec for dec in decays]
            d_s_heads = _dot(jnp.concatenate([jnp.where(_lane_mask(r), dyg, 0.0).astype(BF16) for r in range(4)], axis=0),
                             xdt_b, 1, 1)
            dxdt_heads = _dot(jnp.concatenate([s.astype(BF16) for s in score_heads], axis=1), dyg_b, 0, 0)
            for r in range(4):
                h = 4 * g + r
                dec, s_h = decays[r], score_heads[r]
                d_s = d_s_heads[r * q:(r + 1) * q]
                dxdt = dxdt + jnp.where(_lane_mask(r), dxdt_heads[r * q:(r + 1) * q], 0.0)
                d_cb = d_cb + d_s * dec
                d_m = d_s * s_h
                d_acs = d_acs + jnp.where(lane128 == h, jnp.sum(d_m, axis=1, keepdims=True), 0.0)
                d_acs_t = d_acs_t + jnp.where(row128 == h, jnp.sum(d_m, axis=0, keepdims=True), 0.0)
            d_cb_b = d_cb.astype(BF16)
            d_c = d_c + _dot(d_cb_b, bm, 1, 0)
            d_b = d_b + _dot(d_cb_b, cmat, 0, 0)
            dstate[g] = d_hprev
            dx_ref[:, sl] = dxdt * cm["dt_l"][:, sl] + cm["dskip_l"][:, sl] * dyg
            dx_ref[:, D_SSD + g * N_STATE:D_SSD + (g + 1) * N_STATE] = d_b
            dx_ref[:, D_SSD + N_GROUPS * N_STATE + g * N_STATE:D_SSD + N_GROUPS * N_STATE + (g + 1) * N_STATE] = d_c
            acs_terms.append(t_acs)
            dxdt_all.append(dxdt * xs)
            last_terms.append(d_last_g)
        t_acs_l = jnp.concatenate(acs_terms, axis=1)
        d_dt_l = jnp.concatenate(dxdt_all, axis=1)
        d_last_l = jnp.concatenate(last_terms, axis=1)
        per_head = _pick(jnp.concatenate([t_acs_l, d_dt_l], axis=0), reduce, 2)
        skip_l = jnp.sum(dy_ref[...] * x_ref[:, 0:D_SSD], axis=0, keepdims=True)
        singles = _pick(jnp.concatenate([d_last_l, skip_l, jnp.zeros((14, D_SSD), F32)], axis=0), reduce, 3)
        d_acs = d_acs + per_head[0:q] - d_acs_t.T
        last_row = lax.broadcasted_iota(jnp.int32, (q, 1), 0) == q - 1
        d_acs = d_acs + jnp.where(last_row, singles[0:1, :], 0.0)
        li = lax.broadcasted_iota(jnp.int32, (q, q), 0)
        si = lax.broadcasted_iota(jnp.int32, (q, q), 1)
        d_adt = _running((si >= li).astype(BF16), d_acs, 3)
        d_dt = per_head[q:2 * q] + d_adt * cm["a"]
        d_dtr = d_dt * _sigmoid(cm["v"])
        ddt_ref[...] = d_dtr.astype(BF16)
        d_skip = singles[1:2, :]
        rows = jnp.concatenate([
            jnp.sum(d_dtr, axis=0, keepdims=True),
            jnp.sum(d_adt * cm["dt"], axis=0, keepdims=True) * cm["a"],
            d_skip,
            jnp.zeros((5, LANE), F32)], axis=0)

        @pl.when(step == 0)
        def _():
            st_ref[...] = rows

        @pl.when(step != 0)
        def _():
            st_ref[...] += rows

    rev = lambda c: nc - 1 - c
    return pl.pallas_call(
        body, name=name, grid=(nc,),
        in_specs=[pl.BlockSpec((q, D_SSD), lambda c: (rev(c), 0)),
                  pl.BlockSpec((q, D_XBC), lambda c: (rev(c), 0)),
                  pl.BlockSpec((q, LANE), lambda c: (rev(c), 0)),
                  pl.BlockSpec((8, LANE), lambda c: (0, 0)),
                  pl.BlockSpec((1, N_GROUPS, N_STATE, GROUP_W), lambda c: (rev(c), 0, 0, 0))],
        out_specs=[pl.BlockSpec((q, D_XBC), lambda c: (rev(c), 0)),
                   pl.BlockSpec((q, LANE), lambda c: (rev(c), 0)),
                   pl.BlockSpec((8, LANE), lambda c: (0, 0))],
        out_shape=[jax.ShapeDtypeStruct((seq, D_XBC), F32),
                   jax.ShapeDtypeStruct((seq, LANE), BF16),
                   jax.ShapeDtypeStruct((8, LANE), F32)],
        scratch_shapes=[pltpu.VMEM((N_GROUPS, N_STATE, GROUP_W), F32)],
        compiler_params=_cp(("arbitrary",)))(dy, xbc, proj, par, hprev)


def _gate_norm_fwd(y, proj, wn, tm, name):
    seq = y.shape[0]

    def body(y_ref, z_ref, w_ref, o_ref):
        for g in range(N_GROUPS):
            sl = slice(g * GROUP_W, (g + 1) * GROUP_W)
            zv = z_ref[:, sl]
            yz = y_ref[:, sl] * (zv * _sigmoid(zv))
            r = lax.rsqrt(jnp.mean(yz * yz, axis=-1, keepdims=True) + EPS)
            o_ref[:, sl] = (yz * r * w_ref[:, sl]).astype(BF16)

    tok = pl.BlockSpec((tm, D_SSD), lambda i: (i, 0))
    return pl.pallas_call(
        body, name=name, grid=(seq // tm,),
        in_specs=[tok, tok, pl.BlockSpec((1, D_SSD), lambda i: (0, 0))],
        out_specs=tok,
        out_shape=jax.ShapeDtypeStruct((seq, D_SSD), BF16),
        compiler_params=_cp(("parallel",)))(y, proj, wn)


def _gate_norm_bwd(dmix, wout, y, proj, wn, tm, name):
    seq = y.shape[0]

    def body(dm_ref, wo_ref, y_ref, z_ref, w_ref, dy_ref, dz_ref, st_ref):
        d_ys = _dot(dm_ref[...], wo_ref[...], 1, 1)
        rows = []
        for g in range(N_GROUPS):
            sl = slice(g * GROUP_W, (g + 1) * GROUP_W)
            zv = z_ref[:, sl]
            yv = y_ref[:, sl]
            sg = _sigmoid(zv)
            sz = zv * sg
            yz = yv * sz
            r = lax.rsqrt(jnp.mean(yz * yz, axis=-1, keepdims=True) + EPS)
            yn = yz * r
            dv = d_ys[:, sl]
            dyn = dv * w_ref[:, sl]
            dyz = r * (dyn - yn * jnp.mean(dyn * yn, axis=-1, keepdims=True))
            dy_ref[:, sl] = dyz * sz
            dz_ref[:, sl] = (dyz * yv * _silu_grad(zv, sg)).astype(BF16)
            rows.append(jnp.sum(dv * yn, axis=0, keepdims=True))
        rows = jnp.concatenate([jnp.concatenate(rows, axis=1), jnp.zeros((7, D_SSD), F32)], axis=0)

        @pl.when(pl.program_id(0) == 0)
        def _():
            st_ref[...] = rows

        @pl.when(pl.program_id(0) != 0)
        def _():
            st_ref[...] += rows

    tok = pl.BlockSpec((tm, D_SSD), lambda i: (i, 0))
    return pl.pallas_call(
        body, name=name, grid=(seq // tm,),
        in_specs=[tok, pl.BlockSpec((D_SSD, D), lambda i: (0, 0)), tok, tok, pl.BlockSpec((1, D_SSD), lambda i: (0, 0))],
        out_specs=[tok, tok, pl.BlockSpec((8, D_SSD), lambda i: (0, 0))],
        out_shape=[jax.ShapeDtypeStruct((seq, D_SSD), F32), jax.ShapeDtypeStruct((seq, D_SSD), BF16),
                   jax.ShapeDtypeStruct((8, D_SSD), F32)],
        compiler_params=_cp(("arbitrary",)))(dmix, wout, y, proj, wn)


def _pool_counts(t0, rows, w):
    pos = (t0 + 1 + lax.broadcasted_iota(jnp.int32, (rows, 1), 0)).astype(F32)
    return jnp.minimum(pos, float(w))


def _window_means(ext, t0):
    n = ext.shape[0]
    outs = []
    run = ext
    width = 1
    sums = {}
    while width < 16:
        run = run + pltpu.roll(run, width, 0)
        width *= 2
        sums[width] = run
    for g, w in enumerate(POOL_WINDOWS):
        sl = slice(g * POOL_GW, (g + 1) * POOL_GW)
        cnt = _pool_counts(t0, n - HALO, w)
        outs.append(sums[w][HALO:, sl] / cnt - ext[HALO:, sl])
    return outs


def _pool_fwd(proj, pw, pb, ps, tm, name):
    seq = proj.shape[0]

    def body(u_ref, p_ref, pw_ref, pb_ref, ps_ref, o_ref):
        i = pl.program_id(0)
        prev = jnp.where(i == 0, 0.0, p_ref[...])
        ext = jnp.concatenate([prev, u_ref[...]], axis=0)
        diffs = _window_means(ext, i * tm)
        for g in range(4):
            sl = slice(g * POOL_GW, (g + 1) * POOL_GW)
            out = _dot(diffs[g].astype(BF16), pw_ref[g], 1, 0) + pb_ref[:, sl]
            o_ref[:, sl] = (out * ps_ref[:, sl]).astype(BF16)

    c0 = COL_U // 1024
    vec = pl.BlockSpec((1, D_POOL), lambda i: (0, 0))
    return pl.pallas_call(
        body, name=name, grid=(seq // tm,),
        in_specs=[pl.BlockSpec((tm, 1024), lambda i: (i, c0)),
                  pl.BlockSpec((HALO, 1024), lambda i: (jnp.maximum(i * (tm // HALO) - 1, 0), c0)),
                  pl.BlockSpec((4, POOL_GW, POOL_GW), lambda i: (0, 0, 0)), vec, vec],
        out_specs=pl.BlockSpec((tm, D_POOL), lambda i: (i, 0)),
        out_shape=jax.ShapeDtypeStruct((seq, D_POOL), BF16),
        compiler_params=_cp(("parallel",)))(proj, proj, pw, pb, ps)


def _pool_bwd(dmix, wout, proj, pw, pb, ps, tm, name):
    seq = proj.shape[0]
    ni = seq // tm

    def body(dm_ref, dmn_ref, wo_ref, u_ref, p_ref, pw_ref, pb_ref, ps_ref, du_ref, dw_ref, st_ref):
        i = pl.program_id(0)
        prev = jnp.where(i == 0, 0.0, p_ref[...])
        ext = jnp.concatenate([prev, u_ref[...]], axis=0)
        diffs = _window_means(ext, i * tm)
        n = tm + HALO
        dext = _dot(jnp.concatenate([dm_ref[...], dmn_ref[...]], axis=0), wo_ref[...], 1, 1)
        past_end = jnp.logical_and(i == ni - 1, lax.broadcasted_iota(jnp.int32, (n, 1), 0) >= tm)
        dext = jnp.where(past_end, 0.0, dext)
        b_rows, s_rows = [], []
        for g, w in enumerate(POOL_WINDOWS):
            sl = slice(g * POOL_GW, (g + 1) * POOL_GW)
            wg = pw_ref[g]
            dout = dext[:, sl] * ps_ref[:, sl]
            dcur = dout[:tm]
            pre = _dot(diffs[g].astype(BF16), wg, 1, 0) + pb_ref[:, sl]
            s_rows.append(jnp.sum(dext[:tm, sl] * pre, axis=0, keepdims=True))
            b_rows.append(jnp.sum(dcur, axis=0, keepdims=True))
            dwg = _dot(diffs[g].astype(BF16), dcur.astype(BF16), 0, 0)

            @pl.when(i == 0)
            def _():
                dw_ref[g] = dwg

            @pl.when(i != 0)
            def _():
                dw_ref[g] += dwg

            ddiff = _dot(dout.astype(BF16), wg, 1, 1)
            scaled = ddiff / _pool_counts(i * tm, n, w)
            run = scaled
            width = 1
            while width < w:
                run = run + pltpu.roll(run, n - width, 0)
                width *= 2
            du_ref[:, sl] = (run[:tm] - ddiff[:tm]).astype(BF16)
        rows = jnp.concatenate([jnp.concatenate(b_rows, axis=1), jnp.concatenate(s_rows, axis=1),
                                jnp.zeros((6, D_POOL), F32)], axis=0)

        @pl.when(i == 0)
        def _():
            st_ref[...] = rows

        @pl.when(i != 0)
        def _():
            st_ref[...] += rows

    c0 = COL_U // 1024
    vec = pl.BlockSpec((1, D_POOL), lambda i: (0, 0))
    last = seq // HALO - 1
    return pl.pallas_call(
        body, name=name, grid=(ni,),
        in_specs=[pl.BlockSpec((tm, D), lambda i: (i, 0)),
                  pl.BlockSpec((HALO, D), lambda i: (jnp.minimum((i + 1) * (tm // HALO), last), 0)),
                  pl.BlockSpec((D_POOL, D), lambda i: (1, 0)),
                  pl.BlockSpec((tm, 1024), lambda i: (i, c0)),
                  pl.BlockSpec((HALO, 1024), lambda i: (jnp.maximum(i * (tm // HALO) - 1, 0), c0)),
                  pl.BlockSpec((4, POOL_GW, POOL_GW), lambda i: (0, 0, 0)), vec, vec],
        out_specs=[pl.BlockSpec((tm, D_POOL), lambda i: (i, 0)),
                   pl.BlockSpec((4, POOL_GW, POOL_GW), lambda i: (0, 0, 0)),
                   pl.BlockSpec((8, D_POOL), lambda i: (0, 0))],
        out_shape=[jax.ShapeDtypeStruct((seq, D_POOL), BF16),
                   jax.ShapeDtypeStruct((4, POOL_GW, POOL_GW), F32),
                   jax.ShapeDtypeStruct((8, D_POOL), F32)],
        compiler_params=_cp(("arbitrary",)))(dmix, dmix, wout, proj, proj, pw, pb, ps)


def _mix_out(ys, yp, wout, x1, mod, wn_next, tm, name):
    seq = ys.shape[0]

    def body(ys_ref, yp_ref, w_ref, x_ref, mod_ref, wn_ref, xo_ref, m_ref, h_ref):
        mix = _dot(ys_ref[...], w_ref[0:D_SSD, :], 1, 0) + _dot(yp_ref[...], w_ref[D_SSD:2 * D_SSD, :], 1, 0)
        m_ref[...] = mix.astype(BF16)
        xo = x_ref[...] + mod_ref[5:6, :] * mix
        xo_ref[...] = xo
        h_ref[...] = _modulated(xo, wn_ref[...], mod_ref, 2)

    tok = pl.BlockSpec((tm, D), lambda i: (i, 0))
    return pl.pallas_call(
        body, name=name, grid=(seq // tm,),
        in_specs=[tok, tok, pl.BlockSpec((2 * D_SSD, D), lambda i: (0, 0)), tok,
                  pl.BlockSpec((9, D), lambda i: (0, 0)), pl.BlockSpec((1, D), lambda i: (0, 0))],
        out_specs=[tok, tok, tok],
        out_shape=[jax.ShapeDtypeStruct((seq, D), F32), jax.ShapeDtypeStruct((seq, D), BF16),
                   jax.ShapeDtypeStruct((seq, D), BF16)],
        compiler_params=_cp(("parallel",)))(ys, yp, wout, x1, mod, wn_next)


def _mix_bwd_dh(dz, dxbc, du, ddt, w_main, w_dt, x1, dx2, mixb, wn, mod, tm, name):
    seq = x1.shape[0]

    def body(dz_ref, dx_ref, du_ref, ddt_ref, w_ref, wdt_ref, x_ref, dxo_ref, m_ref, wn_ref, mod_ref, o_ref, st_ref, df_ref):
        dh = (_dot(dz_ref[...], w_ref[COL_Z:COL_Z + 1024, :], 1, 0)
              + _dot(dx_ref[...], w_ref[COL_XBC:COL_XBC + D_XBC, :], 1, 0)
              + _dot(du_ref[...], w_ref[COL_U:COL_U + 1024, :], 1, 0)
              + _dot(ddt_ref[...], wdt_ref[...], 1, 0))
        dx = _norm_bwd(dh, x_ref[...], dxo_ref[...], m_ref[...].astype(F32), wn_ref[...],
                       mod_ref[4:5, :], 1.0, st_ref, pl.program_id(0) == 0)
        o_ref[...] = dx
        df_ref[...] = (dx * (FFN_RES * mod_ref[2:3, :])).astype(BF16)

    tok = pl.BlockSpec((tm, D), lambda i: (i, 0))
    return pl.pallas_call(
        body, name=name, grid=(seq // tm,),
        in_specs=[tok, pl.BlockSpec((tm, D_XBC), lambda i: (i, 0)), tok,
                  pl.BlockSpec((tm, LANE), lambda i: (i, 0)),
                  pl.BlockSpec((D_MAIN, D), lambda i: (0, 0)),
                  pl.BlockSpec((LANE, D), lambda i: (0, 0)),
                  tok, tok, tok,
                  pl.BlockSpec((1, D), lambda i: (0, 0)),
                  pl.BlockSpec((9, D), lambda i: (0, 0))],
        out_specs=[tok, pl.BlockSpec((8, D), lambda i: (0, 0)), tok],
        out_shape=[jax.ShapeDtypeStruct((seq, D), F32), jax.ShapeDtypeStruct((8, D), F32),
                   jax.ShapeDtypeStruct((seq, D), BF16)],
        compiler_params=_cp(("arbitrary",)))(dz, dxbc, du, ddt, w_main, w_dt, x1, dx2, mixb, wn, mod)


def _local_step(x, tgt, mod, wff1, w_dt, later_weights, pool_w, vecs, tm):
    seq = x.shape[0]
    in_proj_weights, out_proj_weights, ffn2_weights = later_weights
    h1 = _prenorm(x, vecs["ffn1_norm"], mod, 0, tm, "ffn1_prenorm")
    x1, h2, s1 = _ffn_forward(x, h1, wff1, mod, 0, vecs["mix_norm"], 1, tm, "ffn1")
    x1, w_main = in_proj_weights(x1)
    proj = _mm_nt(h2, w_main, tm, 2048, F32, "mix_in_proj")
    proj_dt = _mm_nt(h2, w_dt, tm, LANE, F32, "mix_in_proj_dt")
    xbc = _conv_fwd(proj, vecs["conv_w"], vecs["conv_b"], tm, "mix_conv")
    y, hprev = _ssd_fwd(xbc, proj_dt, vecs["ssd_par"], "mix_ssd")
    ys = _gate_norm_fwd(y, proj, vecs["ssd_norm_w"], tm, "mix_gate_norm")
    yp = _pool_fwd(proj, pool_w, vecs["pool_b"], vecs["pool_scale"], tm, "mix_pool")
    ys, wout = out_proj_weights(ys)
    x2, mixb, h3 = _mix_out(ys, yp, wout, x1, mod, vecs["ffn2_norm"], tm, "mix_out_proj")
    x2, wff2 = ffn2_weights(x2)
    x3, _, s3 = _ffn_forward(x2, h3, wff2, mod, 2, vecs["ffn2_norm"], None, tm, "ffn2")
    dx3, df3, st_loss = _loss_head(x3, vecs["final_norm"], tgt, mod, tm, "loss_head")

    dx2, dmix, st3, dw3 = _ffn_backward(dx3, df3, s3, wff2, vecs["ffn2_norm"], mod, 2, (5, 1.0), tm, "ffn2")
    d_wout = (_mm_tn(ys, dmix, 256, seq, "mix_dw_out_ssd"), _mm_tn(yp, dmix, 256, seq, "mix_dw_out_pool"))
    du, d_pool_w, st_pool = _pool_bwd(dmix, wout, proj, pool_w, vecs["pool_b"], vecs["pool_scale"], tm, "mix_pool_bwd")
    dy, dz, st_gn = _gate_norm_bwd(dmix, wout, y, proj, vecs["ssd_norm_w"], tm, "mix_gate_norm_bwd")
    dxbc_act, ddt, st_ssd = _ssd_bwd(dy, xbc, proj_dt, vecs["ssd_par"], hprev, "mix_ssd_bwd")
    dxbc, st_conv = _conv_bwd(dxbc_act, proj, vecs["conv_w"], vecs["conv_b"], tm, "mix_conv_bwd")
    dx1, st2, df1 = _mix_bwd_dh(dz, dxbc, du, ddt, w_main, w_dt, x1, dx2, mixb, vecs["mix_norm"], mod, min(tm, 256),
                                "mix_bwd_dh")
    d_win = (_mm_tn_rows([dz, dxbc, du], h2, 256, "mix_dw_in"), _mm_tn(ddt, h2, LANE, seq, "mix_dw_in_dt"))
    dx0, _, st1, dw1 = _ffn_backward(dx1, df1, s1, wff1, vecs["ffn1_norm"], mod, 0, None, tm, "ffn1")
    stats = dict(ffn1=st1, mix=st2, ffn2=st3, loss=st_loss, pool=st_pool, gn=st_gn, ssd=st_ssd, conv=st_conv)
    return dx0, stats, dw1, dw3, d_win, d_wout, d_pool_w


HBM_SPEC = pl.BlockSpec(memory_space=pltpu.HBM)


def _mesh_pos():
    return lax.axis_index("x"), lax.axis_index("y"), lax.axis_index("c")


def _other_chips(x, y):
    return [(1 - x, y), (x, 1 - y), (1 - x, 1 - y)]


def _all_gather(src, regions, name):
    total, cols = src.shape
    assert sum(r for _, r in regions) == total
    body = _all_gather_body(regions, total, False)
    return pl.pallas_call(
        body, name=name,
        out_shape=jax.ShapeDtypeStruct((N_DEV * total, cols), src.dtype),
        in_specs=[HBM_SPEC], out_specs=HBM_SPEC,
        scratch_shapes=[pltpu.SemaphoreType.DMA((7,)), pltpu.SemaphoreType.DMA((7,)), pltpu.SemaphoreType.DMA],
    )(src)


def _all_gather_async(src, regions, name, collective_id):
    total, cols = src.shape
    assert sum(r for _, r in regions) == total
    return pl.kernel(
        _all_gather_body(regions, total, True), name=name,
        out_type=jax.ShapeDtypeStruct((N_DEV * total, cols), src.dtype),
        mesh=plsc.ScalarSubcoreMesh(axis_name="seq", num_cores=1),
        scratch_types=(pltpu.SemaphoreType.DMA((7,)), pltpu.SemaphoreType.DMA((7,)), pltpu.SemaphoreType.DMA),
        compiler_params=pltpu.CompilerParams(collective_id=collective_id))(src)


def _all_gather_body(regions, total, handshake):
    def body(src_ref, out_ref, send_sems, recv_sems, local_sem):
        x, y, c = _mesh_pos()
        me, sibling = (x, y, c), (x, y, 1 - c)
        chips = _other_chips(x, y)
        if handshake:
            barrier = pltpu.get_barrier_semaphore()
            for peer in [sibling] + [(*chip, c) for chip in chips]:
                pl.semaphore_signal(barrier, inc=1, device_id=peer, device_id_type=MESH)
            pl.semaphore_wait(barrier, 4)

        def rows_of(dev, off, rows):
            start = pl.multiple_of(N_DEV * off + (4 * dev[0] + 2 * dev[1] + dev[2]) * rows, 8)
            return out_ref.at[pl.ds(start, rows), :]

        def copies(k, block, to, from_src):
            out = []
            for off, rows in regions:
                dst = rows_of(block, off, rows)
                out.append(pltpu.make_async_remote_copy(
                    src_ref=src_ref.at[pl.ds(off, rows), :] if from_src else dst, dst_ref=dst,
                    send_sem=send_sems.at[k], recv_sem=recv_sems.at[k], device_id=to, device_id_type=MESH))
            return out

        def drain(k):
            whole = out_ref.at[pl.ds(0, total), :]
            return pltpu.make_async_remote_copy(src_ref=whole, dst_ref=whole, send_sem=send_sems.at[k],
                                                recv_sem=recv_sems.at[k], device_id=me, device_id_type=MESH)

        for off, rows in regions:
            pltpu.make_async_copy(src_ref.at[pl.ds(off, rows), :], rows_of(me, off, rows), local_sem).start()
        first = copies(0, me, sibling, True)
        for j, chip in enumerate(chips):
            first += copies(1 + j, me, (*chip, c), True)
        for cp in first:
            cp.start()
        for j, chip in enumerate(chips):
            drain(1 + j).wait_recv()
            for cp in copies(4 + j, (*chip, c), sibling, False):
                cp.start()
        drain(0).wait_recv()
        for j in range(3):
            drain(4 + j).wait_recv()
        for k in range(7):
            drain(k).wait_send()
        pltpu.make_async_copy(src_ref, out_ref.at[pl.ds(0, total), :], local_sem).wait()

    return body


def _rs_pair(grads, total, name, collective_id):
    cols = grads[0][0].shape[1]
    sent = sum(rows for _, _, rows in grads)
    n = len(grads)

    def body(*refs):
        g_refs, recv_ref, send_sem, recv_sem = refs[:n], refs[n], refs[n + 1], refs[n + 2]
        x, y, c = _mesh_pos()
        sibling = (x, y, 1 - c)
        barrier = pltpu.get_barrier_semaphore()
        pl.semaphore_signal(barrier, inc=1, device_id=sibling, device_id_type=MESH)
        pl.semaphore_wait(barrier, 1)
        for q in range(4):
            for g_ref, (_, off, rows) in zip(g_refs, grads):
                theirs = g_ref.at[pl.ds(pl.multiple_of((2 * q + 1 - c) * rows, 8), rows), :]
                pltpu.make_async_remote_copy(
                    src_ref=theirs, dst_ref=recv_ref.at[q, pl.ds(off, rows), :], send_sem=send_sem, recv_sem=recv_sem,
                    device_id=sibling, device_id_type=MESH).start()
        everything = recv_ref.at[:, pl.ds(0, sent), :]
        whole = pltpu.make_async_remote_copy(src_ref=everything, dst_ref=everything, send_sem=send_sem,
                                             recv_sem=recv_sem, device_id=sibling, device_id_type=MESH)
        whole.wait_send()
        whole.wait_recv()

    return pl.kernel(
        body, name=name, out_type=jax.ShapeDtypeStruct((4, total, cols), F32),
        mesh=plsc.ScalarSubcoreMesh(axis_name="seq", num_cores=1),
        scratch_types=(pltpu.SemaphoreType.DMA, pltpu.SemaphoreType.DMA),
        compiler_params=pltpu.CompilerParams(collective_id=collective_id))(*[g for g, _, _ in grads])


def _pair_sum(grads, from_sibling, pos, name):
    cols = grads[0][0].shape[1]
    n = len(grads)

    def body(pos_ref, *refs):
        for i in range(n):
            s = refs[i][...] + refs[n + i][...]
            refs[2 * n + 2 * i][...] = s
            refs[2 * n + 2 * i + 1][...] = s.astype(BF16)

    in_specs = [pl.BlockSpec((None, None, rows, cols), lambda q, pos_ref: (q, pos_ref[0], 0, 0)) for _, _, rows in grads]
    in_specs += [pl.BlockSpec((None, rows, cols), lambda q, pos_ref, blk=off // rows: (q, blk, 0)) for _, off, rows in grads]
    out_specs, out_shape = [], []
    for _, _, rows in grads:
        out_specs += [pl.BlockSpec((None, rows, cols), lambda q, pos_ref: (q, 0, 0))] * 2
        out_shape += [jax.ShapeDtypeStruct((4, rows, cols), F32), jax.ShapeDtypeStruct((4, rows, cols), BF16)]
    outs = pl.pallas_call(
        body, name=name,
        grid_spec=pltpu.PrefetchScalarGridSpec(num_scalar_prefetch=1, grid=(4,), in_specs=in_specs, out_specs=out_specs),
        out_shape=out_shape,
        compiler_params=_cp(("parallel",)))(pos, *[g.reshape(4, 2, rows, cols) for g, _, rows in grads],
                                            *[from_sibling] * n)
    return [(outs[2 * i], outs[2 * i + 1]) for i in range(n)]


def _rs_chips(parts, total, name, collective_id):
    cols = parts[0][0].shape[2]
    sent = sum(rows for _, _, rows in parts)
    n = len(parts)

    def body(*refs):
        p_refs, out_ref, send_sems, recv_sems = refs[:n], refs[n], refs[n + 1], refs[n + 2]
        x, y, c = _mesh_pos()
        chips = _other_chips(x, y)
        barrier = pltpu.get_barrier_semaphore()
        for chip in chips:
            pl.semaphore_signal(barrier, inc=1, device_id=(*chip, c), device_id_type=MESH)
        pl.semaphore_wait(barrier, 3)
        for j, chip in enumerate(chips):
            q = 2 * chip[0] + chip[1]
            for p_ref, (_, off, rows) in zip(p_refs, parts):
                pltpu.make_async_remote_copy(
                    src_ref=p_ref.at[q], dst_ref=out_ref.at[j, pl.ds(off, rows), :], send_sem=send_sems.at[j],
                    recv_sem=recv_sems.at[j], device_id=(*chip, c), device_id_type=MESH).start()
        for j, chip in enumerate(chips):
            everything = out_ref.at[j, pl.ds(0, sent), :]
            whole = pltpu.make_async_remote_copy(src_ref=everything, dst_ref=everything, send_sem=send_sems.at[j],
                                                 recv_sem=recv_sems.at[j], device_id=(*chip, c), device_id_type=MESH)
            whole.wait_recv()
            whole.wait_send()

    return pl.kernel(
        body, name=name, out_type=jax.ShapeDtypeStruct((3, total, cols), BF16),
        mesh=plsc.ScalarSubcoreMesh(axis_name="seq", num_cores=1),
        scratch_types=(pltpu.SemaphoreType.DMA((3,)), pltpu.SemaphoreType.DMA((3,))),
        compiler_params=pltpu.CompilerParams(collective_id=collective_id))(*[p for p, _, _ in parts])


def _chip_sum(p, from_chips, off, rows, pos, name):
    cols = p.shape[2]

    def body(pos_ref, p_ref, r_ref, o_ref):
        acc = p_ref[...]
        for j in range(3):
            acc = acc + r_ref[j].astype(F32)
        o_ref[...] = acc

    return pl.pallas_call(
        body, name=name,
        grid_spec=pltpu.PrefetchScalarGridSpec(
            num_scalar_prefetch=1, grid=(1,),
            in_specs=[pl.BlockSpec((None, rows, cols), lambda i, pos_ref: (pos_ref[1], 0, 0)),
                      pl.BlockSpec((3, rows, cols), lambda i, pos_ref: (0, off // rows, 0))],
            out_specs=pl.BlockSpec((rows, cols), lambda i, pos_ref: (0, 0))),
        out_shape=jax.ShapeDtypeStruct((rows, cols), F32),
        compiler_params=_cp(("arbitrary",)))(pos, p, from_chips)


def _chip_sum_adamw(p, from_chips, off, rows, pos, w, m, v, tr, name):
    cols = p.shape[2]
    c1 = 1.0 - ADAM_B1 ** ADAM_STEP
    c2 = 1.0 - ADAM_B2 ** ADAM_STEP

    def body(pos_ref, p_ref, r_ref, w_ref, m_ref, v_ref, g_ref, d_ref, mo_ref, vo_ref):
        gv = p_ref[...]
        for j in range(3):
            gv = gv + r_ref[j].astype(F32)
        g_ref[...] = gv
        mn = ADAM_B1 * m_ref[...] + (1.0 - ADAM_B1) * gv
        vn = ADAM_B2 * v_ref[...] + (1.0 - ADAM_B2) * (gv * gv)
        mo_ref[...] = mn
        vo_ref[...] = vn
        d_ref[...] = -ADAM_LR * ((mn / c1) / (jnp.sqrt(vn / c2) + ADAM_EPS) + ADAM_WD * w_ref[...])

    tile = pl.BlockSpec((tr, cols), lambda i, pos_ref: (i, 0))
    shape = jax.ShapeDtypeStruct((rows, cols), F32)
    return pl.pallas_call(
        body, name=name,
        grid_spec=pltpu.PrefetchScalarGridSpec(
            num_scalar_prefetch=1, grid=(rows // tr,),
            in_specs=[pl.BlockSpec((None, tr, cols), lambda i, pos_ref: (pos_ref[1], i, 0)),
                      pl.BlockSpec((3, tr, cols), lambda i, pos_ref: (0, off // tr + i, 0)),
                      tile, tile, tile],
            out_specs=[tile] * 4),
        out_shape=[shape] * 4,
        compiler_params=_cp(("parallel",)))(pos, p, from_chips, w, m, v)


def _row_tile(rows, cap):
    t = min(rows, cap)
    while rows % t or t % 8:
        t -= 8
    return t


def _ada_mod(c_all, w, b, name):
    n = w.shape[1]

    def body(c_ref, w_ref, b_ref, o_ref):
        cv = c_ref[...]
        o_ref[...] = _exact_dot(cv * _sigmoid(cv), w_ref[...]) + b_ref[...]

    return pl.pallas_call(body, name=name, out_shape=jax.ShapeDtypeStruct((N_DEV, n), F32),
                          compiler_params=pltpu.CompilerParams(vmem_limit_bytes=VMEM_LIMIT))(c_all, w, b)


def _ada_grad(c_all, dmod, name):
    n = dmod.shape[1]

    def body(c_ref, d_ref, o_ref):
        cv = c_ref[...]
        o_ref[...] = _dot(cv * _sigmoid(cv), d_ref[...], 0, 0, lax.Precision.HIGHEST)

    return pl.pallas_call(body, name=name, out_shape=jax.ShapeDtypeStruct((D, n), F32),
                          compiler_params=pltpu.CompilerParams(vmem_limit_bytes=VMEM_LIMIT))(c_all, dmod)


def _adamw(w, g, m, v, name):
    rows, cols = w.shape
    tr = _row_tile(rows, 256) if rows % 8 == 0 else rows
    c1 = 1.0 - ADAM_B1 ** ADAM_STEP
    c2 = 1.0 - ADAM_B2 ** ADAM_STEP

    def body(w_ref, g_ref, m_ref, v_ref, d_ref, mo_ref, vo_ref):
        gv = g_ref[...]
        mn = ADAM_B1 * m_ref[...] + (1.0 - ADAM_B1) * gv
        vn = ADAM_B2 * v_ref[...] + (1.0 - ADAM_B2) * (gv * gv)
        mo_ref[...] = mn
        vo_ref[...] = vn
        d_ref[...] = -ADAM_LR * ((mn / c1) / (jnp.sqrt(vn / c2) + ADAM_EPS) + ADAM_WD * w_ref[...])

    spec = pl.BlockSpec((tr, cols), lambda i: (i, 0))
    shape = jax.ShapeDtypeStruct((rows, cols), F32)
    return pl.pallas_call(body, name=name, grid=(rows // tr,), in_specs=[spec] * 4, out_specs=[spec] * 3,
                          out_shape=[shape] * 3, compiler_params=_cp(("parallel",)))(w, g, m, v)


def _sum8_loss(v, loss_row, name):
    rows = v.shape[0] // N_DEV

    def body(v_ref, o_ref, l_ref):
        acc = v_ref[0:rows, :]
        for k in range(1, N_DEV):
            acc = acc + v_ref[k * rows:(k + 1) * rows, :]
        o_ref[...] = acc
        part = jnp.sum(acc[loss_row:loss_row + 8, :], axis=0, keepdims=True)
        l_ref[...] = jnp.broadcast_to(jnp.sum(part, axis=1, keepdims=True), (8, LANE))

    return pl.pallas_call(body, name=name,
                          out_shape=[jax.ShapeDtypeStruct((rows, LANE), F32), jax.ShapeDtypeStruct((8, LANE), F32)],
                          compiler_params=pltpu.CompilerParams(vmem_limit_bytes=VMEM_LIMIT))(v)


WEIGHT_NAMES = ("w_ada", "b_ada", "ffn1_norm", "ffn1_w_gate", "ffn1_w_up", "ffn1_w_down", "mix_norm", "w_in",
                "conv_w", "conv_b", "dt_bias", "a_log", "d_skip", "ssd_norm_w", "pool_w", "pool_b", "pool_scale",
                "w_out", "ffn2_norm", "ffn2_w_gate", "ffn2_w_up", "ffn2_w_down", "final_norm")

FF_SHARD = FF // N_DEV
IN_SHARD = D_IN // N_DEV
MAIN_SHARD = D_MAIN // N_DEV
EDGE = 16
OUT_SHARD = 2 * D_SSD // N_DEV
ADA_SHARD = 9 * D // N_DEV
POOL_SHARD_ROWS = 4 * 32 * POOL_GW // D
GPACK = dict(w_in=(0, MAIN_SHARD), w_out=(512, OUT_SHARD), pool_w=(768, POOL_SHARD_ROWS),
             gate1=(0, FF_SHARD), up1=(352, FF_SHARD), down1=(704, FF_SHARD),
             gate2=(0, FF_SHARD), up2=(352, FF_SHARD), down2=(704, FF_SHARD))
GROUP_ROWS = 1056

SMALL_ROWS = dict(dmod=(0, 72), ffn1_norm=(72, 8), mix_norm=(80, 8), ffn2_norm=(88, 8), final_norm=(96, 8),
                  ssd_norm_w=(104, 8), pool_scale=(112, 8), conv_b=(120, 16), conv_w=(136, 64), pool_b=(200, 8),
                  ssd=(208, 3), loss=(216, 8), w_in_dt=(224, 128), w_in_head=(352, 128), w_in_tail=(480, 128))
SMALL_TOTAL = 608

MAIN_FROM_OWN = (16, 14, 12, 10, 8, 6, 20, 18)
OWN_FROM_MAIN = (16, 18, 20, 22, 24, 26, 12, 14)


def _rows128(v, rows):
    flat = v.reshape(-1)
    return jnp.pad(flat, (0, rows * LANE - flat.shape[0])).reshape(rows, LANE)


def _pad_lanes(v):
    return jnp.pad(v.reshape(-1), (0, LANE - v.size))


def kernel(x, c, w_ada, b_ada, ffn1_norm, ffn1_w_gate, ffn1_w_up, ffn1_w_down, mix_norm, w_in, conv_w, conv_b, dt_bias, a_log, d_skip, ssd_norm_w, pool_w, pool_b, pool_scale, w_out, ffn2_norm, ffn2_w_gate, ffn2_w_up, ffn2_w_down, final_norm, loss_target, m_w_ada, m_b_ada, m_ffn1_norm, m_ffn1_w_gate, m_ffn1_w_up, m_ffn1_w_down, m_mix_norm, m_w_in, m_conv_w, m_conv_b, m_dt_bias, m_a_log, m_d_skip, m_ssd_norm_w, m_pool_w, m_pool_b, m_pool_scale, m_w_out, m_ffn2_norm, m_ffn2_w_gate, m_ffn2_w_up, m_ffn2_w_down, m_final_norm, v_w_ada, v_b_ada, v_ffn1_norm, v_ffn1_w_gate, v_ffn1_w_up, v_ffn1_w_down, v_mix_norm, v_w_in, v_conv_w, v_conv_b, v_dt_bias, v_a_log, v_d_skip, v_ssd_norm_w, v_pool_w, v_pool_b, v_pool_scale, v_w_out, v_ffn2_norm, v_ffn2_w_gate, v_ffn2_w_up, v_ffn2_w_down, v_final_norm):
    given = dict(locals())
    w = {n: given[n] for n in WEIGHT_NAMES}
    m = {n: given["m_" + n] for n in WEIGHT_NAMES}
    v = {n: given["v_" + n] for n in WEIGHT_NAMES}
    mx, my, mc = _mesh_pos()
    me = 4 * mx + 2 * my + mc

    w_in_t = w_in[0].T
    small = jnp.concatenate([c.reshape(-1), conv_w.reshape(-1), pool_b.reshape(-1), pool_w.reshape(-1),
                             w_in_t[0:EDGE].reshape(-1), w_in_t[IN_SHARD - EDGE:IN_SHARD].reshape(-1)])
    small_rows = 536
    gs = _all_gather(_rows128(small, small_rows), [(0, small_rows)], "ag_small").reshape(N_DEV, small_rows * LANE)
    c_all = gs[:, 0:D]
    conv_w_full = gs[:, 1024:2048].reshape(N_DEV, 4, 256).transpose(1, 0, 2).reshape(4, D_XBC)
    pool_b_full = gs[:, 2048:2176].reshape(N_DEV, 4, 32).transpose(1, 0, 2).reshape(1, D_POOL)
    pool_w_full = gs[:, 2176:2176 + 32768].reshape(N_DEV, 4, 32, POOL_GW).transpose(1, 0, 2, 3).reshape(4, POOL_GW, POOL_GW).astype(BF16)
    heads = gs[:, 34944:34944 + EDGE * D].reshape(N_DEV, EDGE, D)
    tails = gs[:, 34944 + EDGE * D:34944 + 2 * EDGE * D].reshape(N_DEV, EDGE, D)

    prev_tail = lax.dynamic_index_in_dim(tails, jnp.maximum(me - 1, 0), axis=0, keepdims=False)
    next_head = lax.dynamic_index_in_dim(heads, jnp.minimum(me + 1, N_DEV - 1), axis=0, keepdims=False)
    first = jnp.asarray(MAIN_FROM_OWN, jnp.int32)[me]

    def window(rows, before, size):
        total = EDGE + IN_SHARD + EDGE
        padded = jnp.pad(rows, ((before, total - before - rows.shape[0]), (0, 0)))
        return lax.dynamic_slice(padded, (first, 0), (size, D))

    main_shard = (window(prev_tail, 0, MAIN_SHARD) + window(w_in_t, EDGE, MAIN_SHARD)
                  + window(next_head, EDGE + IN_SHARD, MAIN_SHARD))
    dt_rows = jnp.concatenate([tails[5], heads[6]], axis=0)[4:4 + N_HEADS]
    w_dt = jnp.pad(dt_rows, ((0, LANE - N_HEADS), (0, 0))).astype(BF16)

    b_ada_cols = lax.dynamic_slice(b_ada, (0, me * ADA_SHARD), (1, ADA_SHARD))
    mod_part = _ada_mod(c_all, w_ada[0], b_ada_cols, "ada_mod")
    mod_all = _all_gather(mod_part, [(0, N_DEV)], "ag_mod").reshape(N_DEV, N_DEV, ADA_SHARD)
    mod = lax.dynamic_index_in_dim(mod_all, me, axis=1, keepdims=False).reshape(9, D)

    packs = (jnp.concatenate([ffn1_w_gate[0].T, ffn1_w_up[0].T], axis=0).astype(BF16),
             ffn1_w_down[0].astype(BF16),
             main_shard.astype(BF16),
             w_out[0].astype(BF16),
             jnp.concatenate([ffn2_w_gate[0].T, ffn2_w_up[0].T, ffn2_w_down[0]], axis=0).astype(BF16))
    packs, _ = lax.optimization_barrier((packs, c_all))
    ffn_regions = [(0, FF_SHARD), (FF_SHARD, FF_SHARD), (2 * FF_SHARD, FF_SHARD)]
    full_a = _all_gather_async(packs[0], ffn_regions[0:2], "ag_weights_ffn1_in", 1)
    full_d = _all_gather_async(packs[1], ffn_regions[0:1], "ag_weights_ffn1_out", 2)
    full_in = _all_gather_async(packs[2], [(0, MAIN_SHARD)], "ag_weights_in_proj", 9)
    full_out = _all_gather_async(packs[3], [(0, OUT_SHARD)], "ag_weights_out_proj", 10)
    full_2 = _all_gather_async(packs[4], ffn_regions, "ag_weights_ffn2", 11)
    wff1 = dict(gate=(full_a, 0), up=(full_a, 1), down=(full_d, 0))

    def in_proj_weights(x1):
        w_i, x1 = lax.optimization_barrier((full_in, x1))
        return x1, w_i

    def out_proj_weights(ys):
        w_o, ys = lax.optimization_barrier((full_out, ys))
        return ys, w_o

    def ffn2_weights(x2):
        w_2, x2 = lax.optimization_barrier((full_2, x2))
        return x2, dict(gate=(w_2, 0), up=(w_2, 1), down=(w_2, 2))

    later_weights = (in_proj_weights, out_proj_weights, ffn2_weights)

    vecs = dict(ffn1_norm=ffn1_norm, mix_norm=mix_norm, ffn2_norm=ffn2_norm, final_norm=final_norm.reshape(1, D),
                conv_w=conv_w_full, conv_b=conv_b, ssd_norm_w=ssd_norm_w, pool_b=pool_b_full, pool_scale=pool_scale,
                ssd_par=jnp.concatenate([_pad_lanes(dt_bias)[None], _pad_lanes(a_log)[None], _pad_lanes(d_skip)[None],
                                         jnp.zeros((5, LANE), F32)], axis=0))
    dx0, st, dw1, dw3, d_win, d_wout, d_pool_w = _local_step(
        x[0], loss_target[0], mod, wff1, w_dt, later_weights, pool_w_full, vecs, min(512, x.shape[1]))

    dwin, d_w_dt = d_win
    dwout = jnp.concatenate(d_wout, axis=0)
    dpool = d_pool_w.reshape(4, N_DEV, 32, POOL_GW).transpose(1, 0, 2, 3).reshape(N_DEV * POOL_SHARD_ROWS, D)
    pos = jnp.stack([mc, 2 * mx + my]).astype(jnp.int32)
    by_key = dict(zip(("gate1", "up1", "down1", "gate2", "up2", "down2", "w_out", "w_in", "pool_w"),
                      (*dw1, *dw3, dwout, dwin, dpool)))
    reduced = {}
    for tag, keys, cid in (("ffn2", ("gate2", "up2", "down2"), 3), ("mix", ("w_in", "w_out", "pool_w"), 5),
                           ("ffn1", ("gate1", "up1", "down1"), 7)):
        grads = [(by_key[k], *GPACK[k]) for k in keys]
        from_sibling = _rs_pair(grads, GROUP_ROWS, f"rs_pair_{tag}", cid)
        pairs = dict(zip(keys, _pair_sum(grads, from_sibling, pos, f"rs_pair_sum_{tag}")))
        from_chips = _rs_chips([(pairs[k][1], *GPACK[k]) for k in keys], GROUP_ROWS, f"rs_chips_{tag}", cid + 1)
        for k in keys:
            reduced[k] = (pairs[k][0], from_chips)

    delta, new_m, new_v, shard_grad = {}, {}, {}, {}
    fused = dict(gate1=("ffn1_w_gate", True, 176), up1=("ffn1_w_up", True, 176), down1=("ffn1_w_down", False, 176),
                 gate2=("ffn2_w_gate", True, 176), up2=("ffn2_w_up", True, 176), down2=("ffn2_w_down", False, 176),
                 w_out=("w_out", False, 128), pool_w=("pool_w", False, POOL_SHARD_ROWS))
    for k, (n, is_transposed, tr) in fused.items():
        shp = w[n].shape
        rows = GPACK[k][1]
        view = (lambda t: t[0].T) if is_transposed else (lambda t: t.reshape(rows, D))
        back = (lambda t: t.T[None]) if is_transposed else (lambda t: t.reshape(shp))
        g_, d_, m_, v_ = _chip_sum_adamw(reduced[k][0], reduced[k][1], *GPACK[k], pos, view(w[n]), view(m[n]), view(v[n]),
                                         tr, f"adamw_{n}")
        shard_grad[n], delta[n], new_m[n], new_v[n] = back(g_), back(d_), back(m_), back(v_)
    g_main = _chip_sum(reduced["w_in"][0], reduced["w_in"][1], *GPACK["w_in"], pos, "rs_chip_sum_w_in")

    dmod = jnp.concatenate([st["ffn1"][0:3], st["mix"][0:3], st["ffn2"][0:3]], axis=0)
    sg = jnp.concatenate([
        dmod.reshape(-1), st["ffn1"][3], st["mix"][3], st["ffn2"][3], st["loss"][0], st["gn"][0], st["pool"][1],
        st["conv"][4], st["conv"][0:4].reshape(-1), st["pool"][0], st["ssd"][0:3].reshape(-1),
        jnp.zeros((5 * LANE,), F32), st["loss"][1],
        d_w_dt[0:N_HEADS].reshape(-1), g_main[0:EDGE].reshape(-1), g_main[MAIN_SHARD - EDGE:MAIN_SHARD].reshape(-1)])
    sg_all = _all_gather(sg.reshape(SMALL_TOTAL, LANE), [(0, SMALL_TOTAL)], "ag_small_grads")
    tot, loss_b = _sum8_loss(sg_all, SMALL_ROWS["loss"][0], "small_sum")
    loss = loss_b[0, 0]
    per_dev = sg_all.reshape(N_DEV, SMALL_TOTAL * LANE)
    dmod_all = per_dev[:, 0:9 * D]
    g_w_ada = _ada_grad(c_all, lax.dynamic_slice(dmod_all, (0, me * ADA_SHARD), (N_DEV, ADA_SHARD)), "ada_grad")

    def edge_rows(k):
        off, n = SMALL_ROWS[k]
        return per_dev[:, off * LANE:(off + n) * LANE].reshape(N_DEV, EDGE, D)

    g_dt = tot[SMALL_ROWS["w_in_dt"][0]:SMALL_ROWS["w_in_dt"][0] + SMALL_ROWS["w_in_dt"][1]].reshape(N_HEADS, D)
    before = jnp.where(me == 6, g_dt, edge_rows("w_in_tail")[6])
    after = jnp.where(me == 5, g_dt, lax.dynamic_index_in_dim(edge_rows("w_in_head"), jnp.minimum(me + 1, N_DEV - 1),
                                                               axis=0, keepdims=False))
    first_own = jnp.asarray(OWN_FROM_MAIN, jnp.int32)[me]

    def own_window(rows, lead):
        total = EDGE + MAIN_SHARD + EDGE
        padded = jnp.pad(rows, ((lead, total - lead - rows.shape[0]), (0, 0)))
        return lax.dynamic_slice(padded, (first_own, 0), (IN_SHARD, D))

    g_win_t = own_window(before, 0) + own_window(g_main, EDGE) + own_window(after, EDGE + MAIN_SHARD)

    def tot_rows(k):
        off, n = SMALL_ROWS[k]
        return tot[off:off + n].reshape(-1)

    g_conv_w = lax.dynamic_slice(tot_rows("conv_w").reshape(4, D_XBC), (0, me * 256), (4, 256))
    g_pool_b = lax.dynamic_slice(tot_rows("pool_b").reshape(4, POOL_GW), (0, me * 32), (4, 32))
    g_ssd = tot_rows("ssd").reshape(3, LANE)
    grad = {
        "w_ada": g_w_ada[None], "b_ada": tot_rows("dmod").reshape(1, 9 * D),
        "ffn1_norm": tot_rows("ffn1_norm")[None], "mix_norm": tot_rows("mix_norm")[None],
        "ffn2_norm": tot_rows("ffn2_norm")[None], "final_norm": tot_rows("final_norm"),
        "ssd_norm_w": tot_rows("ssd_norm_w")[None], "pool_scale": tot_rows("pool_scale")[None],
        "conv_b": tot_rows("conv_b")[None], "conv_w": g_conv_w[None], "pool_b": g_pool_b[None],
        "dt_bias": g_ssd[0:1, 0:N_HEADS], "a_log": g_ssd[1:2, 0:N_HEADS], "d_skip": g_ssd[2:3, 0:N_HEADS],
        "w_in": g_win_t.T[None], **shard_grad,
    }

    d_, m_, v_ = _adamw(w_ada[0], g_w_ada, m_w_ada[0], v_w_ada[0], "adamw_w_ada")
    delta["w_ada"], new_m["w_ada"], new_v["w_ada"] = d_[None], m_[None], v_[None]
    d_, m_, v_ = _adamw(w_in[0].T, g_win_t, m_w_in[0].T, v_w_in[0].T, "adamw_w_in")
    delta["w_in"], new_m["w_in"], new_v["w_in"] = d_.T[None], m_.T[None], v_.T[None]
    big = ("w_ada", "w_in") + tuple(n for n, _, _ in fused.values())
    small_names = [n for n in WEIGHT_NAMES if n not in big]
    sizes = [LANE if w[n].size < LANE else w[n].size for n in small_names]
    small_rows_adam = -(-sum(sizes) // (8 * LANE)) * 8

    def pack_small(t):
        return _rows128(jnp.concatenate([_pad_lanes(t[n]) if t[n].size < LANE else t[n].reshape(-1) for n in small_names]),
                        small_rows_adam)

    d_s, m_s, v_s = _adamw(pack_small(w), pack_small(grad), pack_small(m), pack_small(v), "adamw_small")
    off = 0
    for n, size in zip(small_names, sizes):
        for res, packed in ((delta, d_s), (new_m, m_s), (new_v, v_s)):
            res[n] = packed.reshape(-1)[off:off + w[n].size].reshape(w[n].shape)
        off += size

    return (loss, dx0[None], *[grad[n] for n in WEIGHT_NAMES], *[delta[n] for n in WEIGHT_NAMES],
            *[new_m[n] for n in WEIGHT_NAMES], *[new_v[n] for n in WEIGHT_NAMES])
```

```python
import functools
import math

import jax
import jax.numpy as jnp
from jax import lax
from jax.experimental import pallas as pl
from jax.experimental.pallas import tpu as pltpu
from jax.experimental.pallas import tpu_sc as plsc

F32 = jnp.float32
BF16 = jnp.bfloat16
MESH = pl.DeviceIdType.MESH

N_DEV = 8
D = 1024
FF = 2816
D_SSD = 1024
N_HEADS = 16
HEAD_DIM = 64
N_GROUPS = 4
N_STATE = 128
CHUNK = 128
GROUP_W = D_SSD // N_GROUPS
D_XBC = D_SSD + 2 * N_GROUPS * N_STATE
D_POOL = 1024
POOL_WINDOWS = (2, 4, 8, 16)
POOL_GW = 256
D_IN = 4112
D_MAIN = 4096
COL_Z, COL_XBC, COL_U = 0, 1024, 3072
EPS = 1e-6
FFN_RES = 0.5
LANE = 128
HALO = 16

ADAM_LR, ADAM_B1, ADAM_B2, ADAM_EPS, ADAM_WD, ADAM_STEP = 0.001, 0.9, 0.999, 1e-08, 0.01, 10

VMEM_LIMIT = 56 << 20


def _cp(sem):
    return pltpu.CompilerParams(dimension_semantics=sem, vmem_limit_bytes=VMEM_LIMIT)


def _dot(a, b, ca, cb, prec=None):
    return lax.dot_general(a, b, (((ca,), (cb,)), ((), ())), precision=prec,
                           preferred_element_type=F32)


def _exact_dot(a, b):
    return _dot(a, b, 1, 0, lax.Precision.HIGHEST)


def _sigmoid(v):
    return 1.0 / (1.0 + jnp.exp(-v))


def _silu_grad(v, sg):
    return sg * (1.0 + v * (1.0 - sg))


def _mm_nt(a, bt, tm, tn, out_dtype, name):
    m, k = a.shape
    n = bt.shape[0]

    def body(a_ref, b_ref, o_ref):
        o_ref[...] = _dot(a_ref[...], b_ref[...], 1, 1).astype(out_dtype)

    return pl.pallas_call(
        body, name=name, grid=(n // tn, m // tm),
        in_specs=[pl.BlockSpec((tm, k), lambda j, i: (i, 0)),
                  pl.BlockSpec((tn, k), lambda j, i: (j, 0))],
        out_specs=pl.BlockSpec((tm, tn), lambda j, i: (i, j)),
        out_shape=jax.ShapeDtypeStruct((m, n), out_dtype),
        compiler_params=_cp(("parallel", "parallel")))(a, bt)


def _mm_tn(a, b, tm, tk, name):
    kk, m = a.shape
    n = b.shape[1]
    nk = kk // tk
    if nk == 1:
        def whole(a_ref, b_ref, o_ref):
            o_ref[...] = _dot(a_ref[...], b_ref[...], 0, 0)

        return pl.pallas_call(
            whole, name=name, grid=(m // tm,),
            in_specs=[pl.BlockSpec((kk, tm), lambda i: (0, i)),
                      pl.BlockSpec((kk, n), lambda i: (0, 0))],
            out_specs=pl.BlockSpec((tm, n), lambda i: (i, 0)),
            out_shape=jax.ShapeDtypeStruct((m, n), F32),
            compiler_params=_cp(("parallel",)))(a, b)

    def body(a_ref, b_ref, o_ref, acc):
        k = pl.program_id(1)

        @pl.when(k == 0)
        def _():
            acc[...] = jnp.zeros_like(acc)

        acc[...] += _dot(a_ref[...], b_ref[...], 0, 0)

        @pl.when(k == nk - 1)
        def _():
            o_ref[...] = acc[...]

    return pl.pallas_call(
        body, name=name, grid=(m // tm, nk),
        in_specs=[pl.BlockSpec((tk, tm), lambda i, k: (k, i)),
                  pl.BlockSpec((tk, n), lambda i, k: (k, 0))],
        out_specs=pl.BlockSpec((tm, n), lambda i, k: (i, 0)),
        out_shape=jax.ShapeDtypeStruct((m, n), F32),
        scratch_shapes=[pltpu.VMEM((tm, n), F32)],
        compiler_params=_cp(("parallel", "arbitrary")))(a, b)


def _mm_tn_rows(parts, b, tm, name):
    kk, n = b.shape
    blocks = [a.shape[1] // tm for a in parts]
    starts = [sum(blocks[:p]) for p in range(len(parts))]

    def body(*refs):
        a_refs, b_ref, o_ref = refs[:len(parts)], refs[len(parts)], refs[len(parts) + 1]
        i = pl.program_id(0)
        for p, a_ref in enumerate(a_refs):
            @pl.when(jnp.logical_and(i >= starts[p], i < starts[p] + blocks[p]))
            def _():
                o_ref[...] = _dot(a_ref[...], b_ref[...], 0, 0)

    def part_spec(p):
        return pl.BlockSpec((kk, tm), lambda i: (0, jnp.clip(i - starts[p], 0, blocks[p] - 1)))

    return pl.pallas_call(
        body, name=name, grid=(sum(blocks),),
        in_specs=[part_spec(p) for p in range(len(parts))] + [pl.BlockSpec((kk, n), lambda i: (0, 0))],
        out_specs=pl.BlockSpec((tm, n), lambda i: (i, 0)),
        out_shape=jax.ShapeDtypeStruct((sum(blocks) * tm, n), F32),
        compiler_params=_cp(("parallel",)))(*parts, b)


def _modulated(xv, wn, mod_ref, k):
    r = lax.rsqrt(jnp.mean(xv * xv, axis=-1, keepdims=True) + EPS)
    hn = xv * r * wn
    return (hn * (1.0 + mod_ref[3 * k + 1:3 * k + 2, :]) + mod_ref[3 * k:3 * k + 1, :]).astype(BF16)


def _prenorm(x, wn, mod, k, tm, name):
    seq = x.shape[0]

    def body(x_ref, wn_ref, mod_ref, h_ref):
        h_ref[...] = _modulated(x_ref[...], wn_ref[...], mod_ref, k)

    return pl.pallas_call(
        body, name=name, grid=(seq // tm,),
        in_specs=[pl.BlockSpec((tm, D), lambda i: (i, 0)),
                  pl.BlockSpec((1, D), lambda i: (0, 0)),
                  pl.BlockSpec((9, D), lambda i: (0, 0))],
        out_specs=pl.BlockSpec((tm, D), lambda i: (i, 0)),
        out_shape=jax.ShapeDtypeStruct((seq, D), BF16),
        compiler_params=_cp(("parallel",)))(x, wn, mod)


def _norm_bwd(dh, xv, dxo, branch, wn, sc, res, stats_ref, first):
    r = lax.rsqrt(jnp.mean(xv * xv, axis=-1, keepdims=True) + EPS)
    xn = xv * r
    dhn = dh * (1.0 + sc)
    dxn = dhn * wn
    dx = dxo + r * (dxn - xn * jnp.mean(dxn * xn, axis=-1, keepdims=True))
    rows = jnp.concatenate([
        jnp.sum(dh, axis=0, keepdims=True),
        jnp.sum(dh * (xn * wn), axis=0, keepdims=True),
        jnp.sum(branch * dxo, axis=0, keepdims=True) * res,
        jnp.sum(dhn * xn, axis=0, keepdims=True),
        jnp.zeros((4, D), F32)], axis=0)

    @pl.when(first)
    def _():
        stats_ref[...] = rows

    @pl.when(jnp.logical_not(first))
    def _():
        stats_ref[...] += rows

    return dx


def _ffn_up(h, wg, wu, tm, tn, name):
    seq = h.shape[0]
    nj = FF // tn

    def body(h_ref, wg_ref, wu_ref, a_ref, pg_ref, pu_ref):
        hv = h_ref[...]
        g = _dot(hv, wg_ref[...], 1, 1)
        u = _dot(hv, wu_ref[...], 1, 1)
        sg = _sigmoid(g)
        s = g * sg
        a_ref[...] = (s * u).astype(BF16)
        pg_ref[...] = (u * _silu_grad(g, sg)).astype(BF16)
        pu_ref[...] = s.astype(BF16)

    act = pl.BlockSpec((tm, tn), lambda j, i: (i, j))
    return pl.pallas_call(
        body, name=name, grid=(nj, seq // tm),
        in_specs=[pl.BlockSpec((tm, D), lambda j, i: (i, 0)),
                  pl.BlockSpec((tn, D), lambda j, i: (wg[1] * nj + j, 0)),
                  pl.BlockSpec((tn, D), lambda j, i: (wu[1] * nj + j, 0))],
        out_specs=[act, act, act],
        out_shape=[jax.ShapeDtypeStruct((seq, FF), BF16)] * 3,
        compiler_params=_cp(("parallel", "parallel")))(h, wg[0], wu[0])


def _ffn_down(a, w, blk, x, mod, grow, wn_next, k_next, tm, name):
    seq = a.shape[0]
    chain = k_next is not None

    def body(a_ref, w_ref, x_ref, mod_ref, wn_ref, xo_ref, f_ref, *rest):
        f = _dot(a_ref[...], w_ref[...], 1, 0)
        f_ref[...] = f.astype(BF16)
        xo = x_ref[...] + (FFN_RES * mod_ref[grow:grow + 1, :]) * f
        xo_ref[...] = xo
        if chain:
            rest[0][...] = _modulated(xo, wn_ref[...], mod_ref, k_next)

    tok = pl.BlockSpec((tm, D), lambda i: (i, 0))
    return pl.pallas_call(
        body, name=name, grid=(seq // tm,),
        in_specs=[pl.BlockSpec((tm, FF), lambda i: (i, 0)),
                  pl.BlockSpec((FF, D), lambda i: (blk, 0)),
                  tok,
                  pl.BlockSpec((9, D), lambda i: (0, 0)),
                  pl.BlockSpec((1, D), lambda i: (0, 0))],
        out_specs=[tok, tok] + ([tok] if chain else []),
        out_shape=[jax.ShapeDtypeStruct((seq, D), F32), jax.ShapeDtypeStruct((seq, D), BF16)]
        + ([jax.ShapeDtypeStruct((seq, D), BF16)] if chain else []),
        compiler_params=_cp(("parallel",)))(a, w, x, mod, wn_next)


def _ffn_down_loss(a, w, blk, x, mod, grow, wf, tgt, tm, name):
    seq = a.shape[0]

    def body(a_ref, w_ref, x_ref, mod_ref, wf_ref, t_ref, f_ref, dx_ref, df_ref, st_ref):
        f = _dot(a_ref[...], w_ref[...], 1, 0)
        f_ref[...] = f.astype(BF16)
        xv = x_ref[...] + (FFN_RES * mod_ref[grow:grow + 1, :]) * f
        wv = wf_ref[...]
        r = lax.rsqrt(jnp.mean(xv * xv, axis=-1, keepdims=True) + EPS)
        xn = xv * r
        e = xn * wv - t_ref[...]
        dy = e * (1.0 / D)
        dxn = dy * wv
        dx = r * (dxn - xn * jnp.mean(dxn * xn, axis=-1, keepdims=True))
        dx_ref[...] = dx
        df_ref[...] = (dx * (FFN_RES * mod_ref[grow:grow + 1, :])).astype(BF16)
        rows = jnp.concatenate([
            jnp.sum(dy * xn, axis=0, keepdims=True),
            jnp.sum(e * e, axis=0, keepdims=True) * (0.5 / D),
            jnp.zeros((6, D), F32)], axis=0)

        @pl.when(pl.program_id(0) == 0)
        def _():
            st_ref[...] = rows

        @pl.when(pl.program_id(0) != 0)
        def _():
            st_ref[...] += rows

    tok = pl.BlockSpec((tm, D), lambda i: (i, 0))
    return pl.pallas_call(
        body, name=name, grid=(seq // tm,),
        in_specs=[pl.BlockSpec((tm, FF), lambda i: (i, 0)),
                  pl.BlockSpec((FF, D), lambda i: (blk, 0)),
                  tok,
                  pl.BlockSpec((9, D), lambda i: (0, 0)),
                  pl.BlockSpec((1, D), lambda i: (0, 0)),
                  tok],
        out_specs=[tok, tok, tok, pl.BlockSpec((8, D), lambda i: (0, 0))],
        out_shape=[jax.ShapeDtypeStruct((seq, D), BF16), jax.ShapeDtypeStruct((seq, D), F32),
                   jax.ShapeDtypeStruct((seq, D), BF16), jax.ShapeDtypeStruct((8, D), F32)],
        compiler_params=_cp(("arbitrary",)))(a, w, x, mod, wf, tgt)


def _ffn_bwd_da(df, w, blk, pg, pu, tm, tn, name):
    seq = df.shape[0]
    nj = FF // tn

    def body(df_ref, w_ref, pg_ref, pu_ref, dg_ref, du_ref):
        da = _dot(df_ref[...], w_ref[...], 1, 1)
        dg_ref[...] = (da * pg_ref[...].astype(F32)).astype(BF16)
        du_ref[...] = (da * pu_ref[...].astype(F32)).astype(BF16)

    act = pl.BlockSpec((tm, tn), lambda j, i: (i, j))
    return pl.pallas_call(
        body, name=name, grid=(nj, seq // tm),
        in_specs=[pl.BlockSpec((tm, D), lambda j, i: (i, 0)),
                  pl.BlockSpec((tn, D), lambda j, i: (blk * nj + j, 0)),
                  act, act],
        out_specs=[act, act],
        out_shape=[jax.ShapeDtypeStruct((seq, FF), BF16)] * 2,
        compiler_params=_cp(("parallel", "parallel")))(df, w, pg, pu)


def _ffn_bwd_dh(dg, du, wg, wu, x, dxo, fb, wn, mod, k, nxt, tm, name):
    seq = x.shape[0]

    def body(dg_ref, du_ref, wg_ref, wu_ref, x_ref, dxo_ref, f_ref, wn_ref, mod_ref, dx_ref, st_ref, *rest):
        dh = _dot(dg_ref[...], wg_ref[...], 1, 0) + _dot(du_ref[...], wu_ref[...], 1, 0)
        dx = _norm_bwd(dh, x_ref[...], dxo_ref[...], f_ref[...].astype(F32), wn_ref[...],
                       mod_ref[3 * k + 1:3 * k + 2, :], FFN_RES, st_ref, pl.program_id(0) == 0)
        dx_ref[...] = dx
        if nxt is not None:
            rest[0][...] = (dx * (nxt[1] * mod_ref[nxt[0]:nxt[0] + 1, :])).astype(BF16)

    tok = pl.BlockSpec((tm, D), lambda i: (i, 0))
    act = pl.BlockSpec((tm, FF), lambda i: (i, 0))
    return pl.pallas_call(
        body, name=name, grid=(seq // tm,),
        in_specs=[act, act,
                  pl.BlockSpec((FF, D), lambda i: (wg[1], 0)),
                  pl.BlockSpec((FF, D), lambda i: (wu[1], 0)),
                  tok, tok, tok,
                  pl.BlockSpec((1, D), lambda i: (0, 0)),
                  pl.BlockSpec((9, D), lambda i: (0, 0))],
        out_specs=[tok, pl.BlockSpec((8, D), lambda i: (0, 0))] + ([tok] if nxt is not None else []),
        out_shape=[jax.ShapeDtypeStruct((seq, D), F32), jax.ShapeDtypeStruct((8, D), F32)]
        + ([jax.ShapeDtypeStruct((seq, D), BF16)] if nxt is not None else []),
        compiler_params=_cp(("arbitrary",)))(dg, du, wg[0], wu[0], x, dxo, fb, wn, mod)


def _ffn_forward(x, h, w, mod, k, wn_next, k_next, tm, tag):
    a, pg, pu = _ffn_up(h, w["gate"], w["up"], tm, FF // 2, f"{tag}_up")
    outs = _ffn_down(a, *w["down"], x, mod, 3 * k + 2, wn_next, k_next, tm, f"{tag}_down")
    return outs[0], (outs[2] if k_next is not None else None), (x, h, pg, pu, a, outs[1])


def _ffn_backward(dxo, df, saved, w, wn, mod, k, nxt, tm, tag):
    x, h, pg, pu, a, fb = saved
    dg, du = _ffn_bwd_da(df, *w["down"], pg, pu, tm, FF // 2, f"{tag}_bwd_da")
    seq = x.shape[0]
    d_gate_t = _mm_tn(dg, h, 256, seq, f"{tag}_dw_gate")
    d_up_t = _mm_tn(du, h, 256, seq, f"{tag}_dw_up")
    d_down = _mm_tn(a, df, 256, seq, f"{tag}_dw_down")
    dws, dg, du = lax.optimization_barrier(((d_gate_t, d_up_t, d_down), dg, du))
    outs = _ffn_bwd_dh(dg, du, w["gate"], w["up"], x, dxo, fb, wn, mod, k, nxt, min(tm, 256), f"{tag}_bwd_dh")
    return outs[0], (outs[2] if nxt is not None else None), outs[1], dws


def _prev_rows(tm, col):
    return pl.BlockSpec((HALO, 1024), lambda i, j: (jnp.maximum(i * (tm // HALO) - 1, 0), col + j))


def _conv_pre(ext, cw, cb, rows):
    pre = cb + cw[3:4, :] * ext
    for s in (1, 2, 3):
        pre = pre + cw[3 - s:4 - s, :] * pltpu.roll(ext, s, 0)
    return pre[HALO:HALO + rows]


def _conv_fwd(proj, cw, cb, tm, name):
    seq = proj.shape[0]

    def body(x_ref, p_ref, cw_ref, cb_ref, o_ref, g_ref):
        prev = jnp.where(pl.program_id(0) == 0, 0.0, p_ref[...])
        ext = jnp.concatenate([prev, x_ref[...]], axis=0)
        pre = _conv_pre(ext, cw_ref[...], cb_ref[...], tm)
        sg = _sigmoid(pre)
        o_ref[...] = pre * sg
        g_ref[...] = _silu_grad(pre, sg).astype(BF16)

    c0 = COL_XBC // 1024
    out = pl.BlockSpec((tm, 1024), lambda i, j: (i, j))
    return pl.pallas_call(
        body, name=name, grid=(seq // tm, 2),
        in_specs=[pl.BlockSpec((tm, 1024), lambda i, j: (i, c0 + j)),
                  _prev_rows(tm, c0),
                  pl.BlockSpec((4, 1024), lambda i, j: (0, j)),
                  pl.BlockSpec((1, 1024), lambda i, j: (0, j))],
        out_specs=[out, out],
        out_shape=[jax.ShapeDtypeStruct((seq, D_XBC), F32), jax.ShapeDtypeStruct((seq, D_XBC), BF16)],
        compiler_params=_cp(("parallel", "parallel")))(proj, proj, cw, cb)


def _conv_bwd(dact, slope, proj, cw, tm, name):
    seq = proj.shape[0]
    ni = seq // tm

    def body(d_ref, dn_ref, s_ref, sn_ref, x_ref, p_ref, cw_ref, o_ref, st_ref):
        i = pl.program_id(1)
        cwv = cw_ref[...]
        prev = jnp.where(i == 0, 0.0, p_ref[...])
        ext = jnp.concatenate([prev, x_ref[...]], axis=0)
        dnext = jnp.where(i == ni - 1, 0.0, dn_ref[...] * sn_ref[...].astype(F32))
        dpre = jnp.concatenate([d_ref[...] * s_ref[...].astype(F32), dnext], axis=0)
        n = tm + HALO
        dx = cwv[3:4, :] * dpre
        for s in (1, 2, 3):
            dx = dx + cwv[3 - s:4 - s, :] * pltpu.roll(dpre, n - s, 0)
        o_ref[...] = dx[:tm].astype(BF16)
        dcur = dpre[:tm]
        rows = [jnp.sum(dcur * pltpu.roll(ext, 3 - k, 0)[HALO:HALO + tm], axis=0, keepdims=True) for k in range(3)]
        rows.append(jnp.sum(dcur * ext[HALO:HALO + tm], axis=0, keepdims=True))
        rows.append(jnp.sum(dcur, axis=0, keepdims=True))
        rows.append(jnp.zeros((3, 1024), F32))
        rows = jnp.concatenate(rows, axis=0)

        @pl.when(i == 0)
        def _():
            st_ref[...] = rows

        @pl.when(i != 0)
        def _():
            st_ref[...] += rows

    c0 = COL_XBC // 1024
    cur = pl.BlockSpec((tm, 1024), lambda j, i: (i, j))
    nxt = pl.BlockSpec((HALO, 1024), lambda j, i: (jnp.minimum((i + 1) * (tm // HALO), seq // HALO - 1), j))
    return pl.pallas_call(
        body, name=name, grid=(2, ni),
        in_specs=[cur, nxt, cur, nxt,
                  pl.BlockSpec((tm, 1024), lambda j, i: (i, c0 + j)),
                  pl.BlockSpec((HALO, 1024), lambda j, i: (jnp.maximum(i * (tm // HALO) - 1, 0), c0 + j)),
                  pl.BlockSpec((4, 1024), lambda j, i: (0, j))],
        out_specs=[cur, pl.BlockSpec((8, 1024), lambda j, i: (0, j))],
        out_shape=[jax.ShapeDtypeStruct((seq, D_XBC), BF16), jax.ShapeDtypeStruct((8, D_XBC), F32)],
        compiler_params=_cp(("parallel", "arbitrary")))(dact, dact, slope, slope, proj, proj, cw)


def _bf16_parts(x, n):
    parts, rest = [], x
    for _ in range(n):
        p = rest.astype(BF16)
        parts.append(p)
        rest = rest - p.astype(F32)
    return parts


def _pick(x, sel, n):
    m = x.shape[0]
    prod = _dot(jnp.concatenate(_bf16_parts(x, n), axis=0), sel, 1, 0)
    acc = prod[0:m]
    for i in range(1, n):
        acc = acc + prod[i * m:(i + 1) * m]
    return acc


def _running(mask, x, n):
    k = x.shape[1]
    prod = _dot(mask, jnp.concatenate(_bf16_parts(x, n), axis=1), 1, 0)
    acc = prod[:, 0:k]
    for i in range(1, n):
        acc = acc + prod[:, i * k:(i + 1) * k]
    return acc


def _head_expand():
    r = lax.broadcasted_iota(jnp.int32, (LANE, D_SSD), 0)
    c = lax.broadcasted_iota(jnp.int32, (LANE, D_SSD), 1)
    return (c // HEAD_DIM == r).astype(BF16)


def _head_reduce():
    r = lax.broadcasted_iota(jnp.int32, (D_SSD, LANE), 0)
    c = lax.broadcasted_iota(jnp.int32, (D_SSD, LANE), 1)
    return (r // HEAD_DIM == c).astype(BF16)


def _ssd_common(dtr, par):
    q = CHUNK
    v = dtr + par[0:1, :]
    dt = jnp.maximum(v, 0.0) + jnp.log(1.0 + jnp.exp(-jnp.abs(v)))
    a = -jnp.exp(par[1:2, :])
    adt = dt * a
    li = lax.broadcasted_iota(jnp.int32, (q, q), 0)
    si = lax.broadcasted_iota(jnp.int32, (q, q), 1)
    causal = li >= si
    acs = _running(causal.astype(BF16), adt, 3)
    expand = _head_expand()
    both_l = _pick(jnp.concatenate([dt, acs], axis=0), expand, 3)
    dt_l, acs_l = both_l[0:q], both_l[q:2 * q]
    dskip_l = _pick(jnp.broadcast_to(par[2:3, :], (16, LANE)), expand, 3)[0:1, :]
    last_l = acs_l[q - 1:q, :]
    return dict(v=v, dt=dt, a=a, acs=acs, acs_t=acs.T, causal=causal, dt_l=dt_l, acs_l=acs_l,
                ea_l=jnp.exp(acs_l), ds_l=jnp.exp(last_l - acs_l), cd_l=jnp.exp(last_l), dskip_l=dskip_l)


def _decay(cm, h):
    seg = cm["acs"][:, h:h + 1] - cm["acs_t"][h:h + 1, :]
    return jnp.exp(jnp.where(cm["causal"], seg, -jnp.inf))


def _lane_mask(r):
    lane = lax.broadcasted_iota(jnp.int32, (1, GROUP_W), 1)
    return lane // HEAD_DIM == r


def _ssd_fwd(xbc, proj, par, name):
    seq = xbc.shape[0]
    nc = seq // CHUNK
    q = CHUNK

    def body(x_ref, dt_ref, par_ref, y_ref, hp_ref, state):
        @pl.when(pl.program_id(0) == 0)
        def _():
            state[...] = jnp.zeros_like(state)

        cm = _ssd_common(dt_ref[...], par_ref[...])
        for g in range(N_GROUPS):
            lo = g * GROUP_W
            xs = x_ref[:, lo:lo + GROUP_W]
            bm = x_ref[:, D_SSD + g * N_STATE:D_SSD + (g + 1) * N_STATE].astype(BF16)
            cmat = x_ref[:, D_SSD + N_GROUPS * N_STATE + g * N_STATE:D_SSD + N_GROUPS * N_STATE + (g + 1) * N_STATE].astype(BF16)
            xdt = xs * cm["dt_l"][:, lo:lo + GROUP_W]
            xdt_b = xdt.astype(BF16)
            cb = _dot(cmat, bm, 1, 1)
            scores = jnp.concatenate([(cb * _decay(cm, 4 * g + r)).astype(BF16) for r in range(4)], axis=0)
            yd_heads = _dot(scores, xdt_b, 1, 0)
            yd = yd_heads[0:q]
            for r in range(1, 4):
                yd = jnp.where(_lane_mask(r), yd_heads[r * q:(r + 1) * q], yd)
            hg = state[g]
            hp_ref[0, g] = hg
            yo = _dot(cmat, hg.astype(BF16), 1, 0) * cm["ea_l"][:, lo:lo + GROUP_W]
            y_ref[:, lo:lo + GROUP_W] = yd + yo + cm["dskip_l"][:, lo:lo + GROUP_W] * xs
            xds = (xdt * cm["ds_l"][:, lo:lo + GROUP_W]).astype(BF16)
            state[g] = hg * cm["cd_l"][:, lo:lo + GROUP_W] + _dot(bm, xds, 0, 0)

    return pl.pallas_call(
        body, name=name, grid=(nc,),
        in_specs=[pl.BlockSpec((q, D_XBC), lambda c: (c, 0)),
                  pl.BlockSpec((q, LANE), lambda c: (c, 0)),
                  pl.BlockSpec((8, LANE), lambda c: (0, 0))],
        out_specs=[pl.BlockSpec((q, D_SSD), lambda c: (c, 0)),
                   pl.BlockSpec((1, N_GROUPS, N_STATE, GROUP_W), lambda c: (c, 0, 0, 0))],
        out_shape=[jax.ShapeDtypeStruct((seq, D_SSD), F32),
                   jax.ShapeDtypeStruct((nc, N_GROUPS, N_STATE, GROUP_W), F32)],
        scratch_shapes=[pltpu.VMEM((N_GROUPS, N_STATE, GROUP_W), F32)],
        compiler_params=_cp(("arbitrary",)))(xbc, proj, par)


def _ssd_bwd(dy, xbc, proj, par, hprev, name):
    seq = xbc.shape[0]
    nc = seq // CHUNK
    q = CHUNK

    def body(dy_ref, x_ref, dt_ref, par_ref, hp_ref, dx_ref, ddt_ref, st_ref, dstate):
        step = pl.program_id(0)

        @pl.when(step == 0)
        def _():
            dstate[...] = jnp.zeros_like(dstate)

        par = par_ref[...]
        cm = _ssd_common(dt_ref[...], par)
        reduce = _head_reduce()
        lane128 = lax.broadcasted_iota(jnp.int32, (1, LANE), 1)
        row128 = lax.broadcasted_iota(jnp.int32, (LANE, 1), 0)
        d_acs = jnp.zeros((q, LANE), F32)
        d_acs_t = jnp.zeros((LANE, q), F32)
        last_terms = []
        acs_terms = []
        dxdt_all = []
        for g in range(N_GROUPS):
            lo = g * GROUP_W
            sl = slice(lo, lo + GROUP_W)
            xs = x_ref[:, sl]
            bm32 = x_ref[:, D_SSD + g * N_STATE:D_SSD + (g + 1) * N_STATE]
            cm32 = x_ref[:, D_SSD + N_GROUPS * N_STATE + g * N_STATE:D_SSD + N_GROUPS * N_STATE + (g + 1) * N_STATE]
            bm = bm32.astype(BF16)
            cmat = cm32.astype(BF16)
            dyg = dy_ref[:, sl]
            dyg_b = dyg.astype(BF16)
            xdt = xs * cm["dt_l"][:, sl]
            xdt_b = xdt.astype(BF16)
            hg = hp_ref[0, g]
            hg_b = hg.astype(BF16)
            dhg = dstate[g]
            dhg_b = dhg.astype(BF16)
            ea = cm["ea_l"][:, sl]
            ds = cm["ds_l"][:, sl]
            cd = cm["cd_l"][:, sl]
            yoff = _dot(cmat, hg_b, 1, 0) * ea
            dw = (dyg * ea).astype(BF16)
            d_c = _dot(dw, hg_b, 1, 1)
            d_hprev = _dot(cmat, dw, 0, 0) + dhg * cd
            t_acs = dyg * yoff
            d_last_g = jnp.sum(dhg * hg, axis=0, keepdims=True) * cd
            xds_b = (xdt * ds).astype(BF16)
            dxds = _dot(bm, dhg_b, 1, 0)
            d_b = _dot(xds_b, dhg_b, 1, 1)
            dxdt = dxds * ds
            t_ds = dxds * xdt * ds
            t_acs = t_acs - t_ds
            d_last_g = d_last_g + jnp.sum(t_ds, axis=0, keepdims=True)
            cb = _dot(cmat, bm, 1, 1)
            d_cb = jnp.zeros((q, q), F32)
            decays = [_decay(cm, 4 * g + r) for r in range(4)]
            score_heads = [cb * dec for dec in decays]
            d_s_heads = _dot(jnp.concatenate([jnp.where(_lane_mask(r), dyg, 0.0).astype(BF16) for r in range(4)], axis=0),
                             xdt_b, 1, 1)
            dxdt_heads = _dot(jnp.concatenate([s.astype(BF16) for s in score_heads], axis=1), dyg_b, 0, 0)
            for r in range(4):
                h = 4 * g + r
                dec, s_h = decays[r], score_heads[r]
                d_s = d_s_heads[r * q:(r + 1) * q]
                dxdt = dxdt + jnp.where(_lane_mask(r), dxdt_heads[r * q:(r + 1) * q], 0.0)
                d_cb = d_cb + d_s * dec
                d_m = d_s * s_h
                d_acs = d_acs + jnp.where(lane128 == h, jnp.sum(d_m, axis=1, keepdims=True), 0.0)
                d_acs_t = d_acs_t + jnp.where(row128 == h, jnp.sum(d_m, axis=0, keepdims=True), 0.0)
            d_cb_b = d_cb.astype(BF16)
            d_c = d_c + _dot(d_cb_b, bm, 1, 0)
            d_b = d_b + _dot(d_cb_b, cmat, 0, 0)
            dstate[g] = d_hprev
            dx_ref[:, sl] = dxdt * cm["dt_l"][:, sl] + cm["dskip_l"][:, sl] * dyg
            dx_ref[:, D_SSD + g * N_STATE:D_SSD + (g + 1) * N_STATE] = d_b
            dx_ref[:, D_SSD + N_GROUPS * N_STATE + g * N_STATE:D_SSD + N_GROUPS * N_STATE + (g + 1) * N_STATE] = d_c
            acs_terms.append(t_acs)
            dxdt_all.append(dxdt * xs)
            last_terms.append(d_last_g)
        t_acs_l = jnp.concatenate(acs_terms, axis=1)
        d_dt_l = jnp.concatenate(dxdt_all, axis=1)
        d_last_l = jnp.concatenate(last_terms, axis=1)
        per_head = _pick(jnp.concatenate([t_acs_l, d_dt_l], axis=0), reduce, 2)
        skip_l = jnp.sum(dy_ref[...] * x_ref[:, 0:D_SSD], axis=0, keepdims=True)
        singles = _pick(jnp.concatenate([d_last_l, skip_l, jnp.zeros((14, D_SSD), F32)], axis=0), reduce, 3)
        d_acs = d_acs + per_head[0:q] - d_acs_t.T
        last_row = lax.broadcasted_iota(jnp.int32, (q, 1), 0) == q - 1
        d_acs = d_acs + jnp.where(last_row, singles[0:1, :], 0.0)
        li = lax.broadcasted_iota(jnp.int32, (q, q), 0)
        si = lax.broadcasted_iota(jnp.int32, (q, q), 1)
        d_adt = _running((si >= li).astype(BF16), d_acs, 3)
        d_dt = per_head[q:2 * q] + d_adt * cm["a"]
        d_dtr = d_dt * _sigmoid(cm["v"])
        ddt_ref[...] = d_dtr.astype(BF16)
        d_skip = singles[1:2, :]
        rows = jnp.concatenate([
            jnp.sum(d_dtr, axis=0, keepdims=True),
            jnp.sum(d_adt * cm["dt"], axis=0, keepdims=True) * cm["a"],
            d_skip,
            jnp.zeros((5, LANE), F32)], axis=0)

        @pl.when(step == 0)
        def _():
            st_ref[...] = rows

        @pl.when(step != 0)
        def _():
            st_ref[...] += rows

    rev = lambda c: nc - 1 - c
    return pl.pallas_call(
        body, name=name, grid=(nc,),
        in_specs=[pl.BlockSpec((q, D_SSD), lambda c: (rev(c), 0)),
                  pl.BlockSpec((q, D_XBC), lambda c: (rev(c), 0)),
                  pl.BlockSpec((q, LANE), lambda c: (rev(c), 0)),
                  pl.BlockSpec((8, LANE), lambda c: (0, 0)),
                  pl.BlockSpec((1, N_GROUPS, N_STATE, GROUP_W), lambda c: (rev(c), 0, 0, 0))],
        out_specs=[pl.BlockSpec((q, D_XBC), lambda c: (rev(c), 0)),
                   pl.BlockSpec((q, LANE), lambda c: (rev(c), 0)),
                   pl.BlockSpec((8, LANE), lambda c: (0, 0))],
        out_shape=[jax.ShapeDtypeStruct((seq, D_XBC), F32),
                   jax.ShapeDtypeStruct((seq, LANE), BF16),
                   jax.ShapeDtypeStruct((8, LANE), F32)],
        scratch_shapes=[pltpu.VMEM((N_GROUPS, N_STATE, GROUP_W), F32)],
        compiler_params=_cp(("arbitrary",)))(dy, xbc, proj, par, hprev)


def _gate_norm_fwd(y, proj, wn, tm, name):
    seq = y.shape[0]

    def body(y_ref, z_ref, w_ref, o_ref):
        for g in range(N_GROUPS):
            sl = slice(g * GROUP_W, (g + 1) * GROUP_W)
            zv = z_ref[:, sl]
            yz = y_ref[:, sl] * (zv * _sigmoid(zv))
            r = lax.rsqrt(jnp.mean(yz * yz, axis=-1, keepdims=True) + EPS)
            o_ref[:, sl] = (yz * r * w_ref[:, sl]).astype(BF16)

    tok = pl.BlockSpec((tm, D_SSD), lambda i: (i, 0))
    return pl.pallas_call(
        body, name=name, grid=(seq // tm,),
        in_specs=[tok, tok, pl.BlockSpec((1, D_SSD), lambda i: (0, 0))],
        out_specs=tok,
        out_shape=jax.ShapeDtypeStruct((seq, D_SSD), BF16),
        compiler_params=_cp(("parallel",)))(y, proj, wn)


def _gate_norm_bwd(dmix, wout, y, proj, wn, tm, name):
    seq = y.shape[0]

    def body(dm_ref, wo_ref, y_ref, z_ref, w_ref, dy_ref, dz_ref, st_ref):
        d_ys = _dot(dm_ref[...], wo_ref[...], 1, 1)
        rows = []
        for g in range(N_GROUPS):
            sl = slice(g * GROUP_W, (g + 1) * GROUP_W)
            zv = z_ref[:, sl]
            yv = y_ref[:, sl]
            sg = _sigmoid(zv)
            sz = zv * sg
            yz = yv * sz
            r = lax.rsqrt(jnp.mean(yz * yz, axis=-1, keepdims=True) + EPS)
            yn = yz * r
            dv = d_ys[:, sl]
            dyn = dv * w_ref[:, sl]
            dyz = r * (dyn - yn * jnp.mean(dyn * yn, axis=-1, keepdims=True))
            dy_ref[:, sl] = dyz * sz
            dz_ref[:, sl] = (dyz * yv * _silu_grad(zv, sg)).astype(BF16)
            rows.append(jnp.sum(dv * yn, axis=0, keepdims=True))
        rows = jnp.concatenate([jnp.concatenate(rows, axis=1), jnp.zeros((7, D_SSD), F32)], axis=0)

        @pl.when(pl.program_id(0) == 0)
        def _():
            st_ref[...] = rows

        @pl.when(pl.program_id(0) != 0)
        def _():
            st_ref[...] += rows

    tok = pl.BlockSpec((tm, D_SSD), lambda i: (i, 0))
    return pl.pallas_call(
        body, name=name, grid=(seq // tm,),
        in_specs=[tok, pl.BlockSpec((D_SSD, D), lambda i: (0, 0)), tok, tok, pl.BlockSpec((1, D_SSD), lambda i: (0, 0))],
        out_specs=[tok, tok, pl.BlockSpec((8, D_SSD), lambda i: (0, 0))],
        out_shape=[jax.ShapeDtypeStruct((seq, D_SSD), F32), jax.ShapeDtypeStruct((seq, D_SSD), BF16),
                   jax.ShapeDtypeStruct((8, D_SSD), F32)],
        compiler_params=_cp(("arbitrary",)))(dmix, wout, y, proj, wn)


def _pool_counts(t0, rows, w):
    pos = (t0 + 1 + lax.broadcasted_iota(jnp.int32, (rows, 1), 0)).astype(F32)
    return jnp.minimum(pos, float(w))


def _window_means(ext, t0):
    n = ext.shape[0]
    outs = []
    run = ext
    width = 1
    sums = {}
    while width < 16:
        run = run + pltpu.roll(run, width, 0)
        width *= 2
        sums[width] = run
    for g, w in enumerate(POOL_WINDOWS):
        sl = slice(g * POOL_GW, (g + 1) * POOL_GW)
        cnt = _pool_counts(t0, n - HALO, w)
        outs.append(sums[w][HALO:, sl] / cnt - ext[HALO:, sl])
    return outs


def _pool_fwd(proj, pw, pb, ps, tm, name):
    seq = proj.shape[0]

    def body(u_ref, p_ref, pw_ref, pb_ref, ps_ref, o_ref):
        i = pl.program_id(0)
        prev = jnp.where(i == 0, 0.0, p_ref[...])
        ext = jnp.concatenate([prev, u_ref[...]], axis=0)
        diffs = _window_means(ext, i * tm)
        for g in range(4):
            sl = slice(g * POOL_GW, (g + 1) * POOL_GW)
            out = _dot(diffs[g].astype(BF16), pw_ref[g], 1, 0) + pb_ref[:, sl]
            o_ref[:, sl] = (out * ps_ref[:, sl]).astype(BF16)

    c0 = COL_U // 1024
    vec = pl.BlockSpec((1, D_POOL), lambda i: (0, 0))
    return pl.pallas_call(
        body, name=name, grid=(seq // tm,),
        in_specs=[pl.BlockSpec((tm, 1024), lambda i: (i, c0)),
                  pl.BlockSpec((HALO, 1024), lambda i: (jnp.maximum(i * (tm // HALO) - 1, 0), c0)),
                  pl.BlockSpec((4, POOL_GW, POOL_GW), lambda i: (0, 0, 0)), vec, vec],
        out_specs=pl.BlockSpec((tm, D_POOL), lambda i: (i, 0)),
        out_shape=jax.ShapeDtypeStruct((seq, D_POOL), BF16),
        compiler_params=_cp(("parallel",)))(proj, proj, pw, pb, ps)


def _pool_bwd(dmix, wout, proj, pw, pb, ps, tm, name):
    seq = proj.shape[0]
    ni = seq // tm

    def body(dm_ref, dmn_ref, wo_ref, u_ref, p_ref, pw_ref, pb_ref, ps_ref, du_ref, dw_ref, st_ref):
        i = pl.program_id(0)
        prev = jnp.where(i == 0, 0.0, p_ref[...])
        ext = jnp.concatenate([prev, u_ref[...]], axis=0)
        diffs = _window_means(ext, i * tm)
        n = tm + HALO
        dext = _dot(jnp.concatenate([dm_ref[...], dmn_ref[...]], axis=0), wo_ref[...], 1, 1)
        past_end = jnp.logical_and(i == ni - 1, lax.broadcasted_iota(jnp.int32, (n, 1), 0) >= tm)
        dext = jnp.where(past_end, 0.0, dext)
        b_rows, s_rows = [], []
        for g, w in enumerate(POOL_WINDOWS):
            sl = slice(g * POOL_GW, (g + 1) * POOL_GW)
            wg = pw_ref[g]
            dout = dext[:, sl] * ps_ref[:, sl]
            dcur = dout[:tm]
            pre = _dot(diffs[g].astype(BF16), wg, 1, 0) + pb_ref[:, sl]
            s_rows.append(jnp.sum(dext[:tm, sl] * pre, axis=0, keepdims=True))
            b_rows.append(jnp.sum(dcur, axis=0, keepdims=True))
            dwg = _dot(diffs[g].astype(BF16), dcur.astype(BF16), 0, 0)

            @pl.when(i == 0)
            def _():
                dw_ref[g] = dwg

            @pl.when(i != 0)
            def _():
                dw_ref[g] += dwg

            ddiff = _dot(dout.astype(BF16), wg, 1, 1)
            scaled = ddiff / _pool_counts(i * tm, n, w)
            run = scaled
            width = 1
            while width < w:
                run = run + pltpu.roll(run, n - width, 0)
                width *= 2
            du_ref[:, sl] = (run[:tm] - ddiff[:tm]).astype(BF16)
        rows = jnp.concatenate([jnp.concatenate(b_rows, axis=1), jnp.concatenate(s_rows, axis=1),
                                jnp.zeros((6, D_POOL), F32)], axis=0)

        @pl.when(i == 0)
        def _():
            st_ref[...] = rows

        @pl.when(i != 0)
        def _():
            st_ref[...] += rows

    c0 = COL_U // 1024
    vec = pl.BlockSpec((1, D_POOL), lambda i: (0, 0))
    last = seq // HALO - 1
    return pl.pallas_call(
        body, name=name, grid=(ni,),
        in_specs=[pl.BlockSpec((tm, D), lambda i: (i, 0)),
                  pl.BlockSpec((HALO, D), lambda i: (jnp.minimum((i + 1) * (tm // HALO), last), 0)),
                  pl.BlockSpec((D_POOL, D), lambda i: (1, 0)),
                  pl.BlockSpec((tm, 1024), lambda i: (i, c0)),
                  pl.BlockSpec((HALO, 1024), lambda i: (jnp.maximum(i * (tm // HALO) - 1, 0), c0)),
                  pl.BlockSpec((4, POOL_GW, POOL_GW), lambda i: (0, 0, 0)), vec, vec],
        out_specs=[pl.BlockSpec((tm, D_POOL), lambda i: (i, 0)),
                   pl.BlockSpec((4, POOL_GW, POOL_GW), lambda i: (0, 0, 0)),
                   pl.BlockSpec((8, D_POOL), lambda i: (0, 0))],
        out_shape=[jax.ShapeDtypeStruct((seq, D_POOL), BF16),
                   jax.ShapeDtypeStruct((4, POOL_GW, POOL_GW), F32),
                   jax.ShapeDtypeStruct((8, D_POOL), F32)],
        compiler_params=_cp(("arbitrary",)))(dmix, dmix, wout, proj, proj, pw, pb, ps)


def _mix_out(ys, yp, wout, x1, mod, wn_next, tm, name):
    seq = ys.shape[0]

    def body(ys_ref, yp_ref, w_ref, x_ref, mod_ref, wn_ref, xo_ref, m_ref, h_ref):
        mix = _dot(ys_ref[...], w_ref[0:D_SSD, :], 1, 0) + _dot(yp_ref[...], w_ref[D_SSD:2 * D_SSD, :], 1, 0)
        m_ref[...] = mix.astype(BF16)
        xo = x_ref[...] + mod_ref[5:6, :] * mix
        xo_ref[...] = xo
        h_ref[...] = _modulated(xo, wn_ref[...], mod_ref, 2)

    tok = pl.BlockSpec((tm, D), lambda i: (i, 0))
    return pl.pallas_call(
        body, name=name, grid=(seq // tm,),
        in_specs=[tok, tok, pl.BlockSpec((2 * D_SSD, D), lambda i: (0, 0)), tok,
                  pl.BlockSpec((9, D), lambda i: (0, 0)), pl.BlockSpec((1, D), lambda i: (0, 0))],
        out_specs=[tok, tok, tok],
        out_shape=[jax.ShapeDtypeStruct((seq, D), F32), jax.ShapeDtypeStruct((seq, D), BF16),
                   jax.ShapeDtypeStruct((seq, D), BF16)],
        compiler_params=_cp(("parallel",)))(ys, yp, wout, x1, mod, wn_next)


def _mix_bwd_dh(dz, dxbc, du, ddt, w_main, w_dt, x1, dx2, mixb, wn, mod, tm, name):
    seq = x1.shape[0]

    def body(dz_ref, dx_ref, du_ref, ddt_ref, w_ref, wdt_ref, x_ref, dxo_ref, m_ref, wn_ref, mod_ref, o_ref, st_ref, df_ref):
        dh = (_dot(dz_ref[...], w_ref[COL_Z:COL_Z + 1024, :], 1, 0)
              + _dot(dx_ref[...], w_ref[COL_XBC:COL_XBC + D_XBC, :], 1, 0)
              + _dot(du_ref[...], w_ref[COL_U:COL_U + 1024, :], 1, 0)
              + _dot(ddt_ref[...], wdt_ref[...], 1, 0))
        dx = _norm_bwd(dh, x_ref[...], dxo_ref[...], m_ref[...].astype(F32), wn_ref[...],
                       mod_ref[4:5, :], 1.0, st_ref, pl.program_id(0) == 0)
        o_ref[...] = dx
        df_ref[...] = (dx * (FFN_RES * mod_ref[2:3, :])).astype(BF16)

    tok = pl.BlockSpec((tm, D), lambda i: (i, 0))
    return pl.pallas_call(
        body, name=name, grid=(seq // tm,),
        in_specs=[tok, pl.BlockSpec((tm, D_XBC), lambda i: (i, 0)), tok,
                  pl.BlockSpec((tm, LANE), lambda i: (i, 0)),
                  pl.BlockSpec((D_MAIN, D), lambda i: (0, 0)),
                  pl.BlockSpec((LANE, D), lambda i: (0, 0)),
                  tok, tok, tok,
                  pl.BlockSpec((1, D), lambda i: (0, 0)),
                  pl.BlockSpec((9, D), lambda i: (0, 0))],
        out_specs=[tok, pl.BlockSpec((8, D), lambda i: (0, 0)), tok],
        out_shape=[jax.ShapeDtypeStruct((seq, D), F32), jax.ShapeDtypeStruct((8, D), F32),
                   jax.ShapeDtypeStruct((seq, D), BF16)],
        compiler_params=_cp(("arbitrary",)))(dz, dxbc, du, ddt, w_main, w_dt, x1, dx2, mixb, wn, mod)


def _local_step(x, tgt, mod, wff1, w_dt, later_weights, pool_w, vecs, tm):
    seq = x.shape[0]
    in_proj_weights, out_proj_weights, ffn2_weights = later_weights
    h1 = _prenorm(x, vecs["ffn1_norm"], mod, 0, tm, "ffn1_prenorm")
    x1, h2, s1 = _ffn_forward(x, h1, wff1, mod, 0, vecs["mix_norm"], 1, tm, "ffn1")
    x1, w_main = in_proj_weights(x1)
    proj = _mm_nt(h2, w_main, tm, 2048, F32, "mix_in_proj")
    proj_dt = _mm_nt(h2, w_dt, tm, LANE, F32, "mix_in_proj_dt")
    xbc, conv_slope = _conv_fwd(proj, vecs["conv_w"], vecs["conv_b"], tm, "mix_conv")
    y, hprev = _ssd_fwd(xbc, proj_dt, vecs["ssd_par"], "mix_ssd")
    ys = _gate_norm_fwd(y, proj, vecs["ssd_norm_w"], tm, "mix_gate_norm")
    yp = _pool_fwd(proj, pool_w, vecs["pool_b"], vecs["pool_scale"], tm, "mix_pool")
    ys, wout = out_proj_weights(ys)
    x2, mixb, h3 = _mix_out(ys, yp, wout, x1, mod, vecs["ffn2_norm"], tm, "mix_out_proj")
    x2, wff2 = ffn2_weights(x2)
    a3, pg3, pu3 = _ffn_up(h3, wff2["gate"], wff2["up"], tm, FF // 2, "ffn2_up")
    fb3, dx3, df3, st_loss = _ffn_down_loss(a3, *wff2["down"], x2, mod, 8, vecs["final_norm"], tgt, tm, "ffn2_down_loss")
    s3 = (x2, h3, pg3, pu3, a3, fb3)

    dx2, dmix, st3, dw3 = _ffn_backward(dx3, df3, s3, wff2, vecs["ffn2_norm"], mod, 2, (5, 1.0), tm, "ffn2")
    d_wout = (_mm_tn(ys, dmix, 256, seq, "mix_dw_out_ssd"), _mm_tn(yp, dmix, 256, seq, "mix_dw_out_pool"))
    du, d_pool_w, st_pool = _pool_bwd(dmix, wout, proj, pool_w, vecs["pool_b"], vecs["pool_scale"], tm, "mix_pool_bwd")
    dy, dz, st_gn = _gate_norm_bwd(dmix, wout, y, proj, vecs["ssd_norm_w"], tm, "mix_gate_norm_bwd")
    dxbc_act, ddt, st_ssd = _ssd_bwd(dy, xbc, proj_dt, vecs["ssd_par"], hprev, "mix_ssd_bwd")
    dxbc, st_conv = _conv_bwd(dxbc_act, conv_slope, proj, vecs["conv_w"], tm, "mix_conv_bwd")
    dx1, st2, df1 = _mix_bwd_dh(dz, dxbc, du, ddt, w_main, w_dt, x1, dx2, mixb, vecs["mix_norm"], mod, min(tm, 256),
                                "mix_bwd_dh")
    d_win = (_mm_tn_rows([dz, dxbc, du], h2, 256, "mix_dw_in"), _mm_tn(ddt, h2, LANE, seq, "mix_dw_in_dt"))
    dx0, _, st1, dw1 = _ffn_backward(dx1, df1, s1, wff1, vecs["ffn1_norm"], mod, 0, None, tm, "ffn1")
    stats = dict(ffn1=st1, mix=st2, ffn2=st3, loss=st_loss, pool=st_pool, gn=st_gn, ssd=st_ssd, conv=st_conv)
    return dx0, stats, dw1, dw3, d_win, d_wout, d_pool_w


HBM_SPEC = pl.BlockSpec(memory_space=pltpu.HBM)


def _mesh_pos():
    return lax.axis_index("x"), lax.axis_index("y"), lax.axis_index("c")


def _other_chips(x, y):
    return [(1 - x, y), (x, 1 - y), (1 - x, 1 - y)]


def _all_gather(src, regions, name):
    total, cols = src.shape
    assert sum(r for _, r in regions) == total
    body = _all_gather_body(regions, total, False)
    return pl.pallas_call(
        body, name=name,
        out_shape=jax.ShapeDtypeStruct((N_DEV * total, cols), src.dtype),
        in_specs=[HBM_SPEC], out_specs=HBM_SPEC,
        scratch_shapes=[pltpu.SemaphoreType.DMA((7,)), pltpu.SemaphoreType.DMA((7,)), pltpu.SemaphoreType.DMA],
    )(src)


def _all_gather_async(src, regions, name, collective_id):
    total, cols = src.shape
    assert sum(r for _, r in regions) == total
    return pl.kernel(
        _all_gather_body(regions, total, True), name=name,
        out_type=jax.ShapeDtypeStruct((N_DEV * total, cols), src.dtype),
        mesh=plsc.ScalarSubcoreMesh(axis_name="seq", num_cores=1),
        scratch_types=(pltpu.SemaphoreType.DMA((7,)), pltpu.SemaphoreType.DMA((7,)), pltpu.SemaphoreType.DMA),
        compiler_params=pltpu.CompilerParams(collective_id=collective_id))(src)


def _all_gather_body(regions, total, handshake):
    def body(src_ref, out_ref, send_sems, recv_sems, local_sem):
        x, y, c = _mesh_pos()
        me, sibling = (x, y, c), (x, y, 1 - c)
        chips = _other_chips(x, y)
        if handshake:
            barrier = pltpu.get_barrier_semaphore()
            for peer in [sibling] + [(*chip, c) for chip in chips]:
                pl.semaphore_signal(barrier, inc=1, device_id=peer, device_id_type=MESH)
            pl.semaphore_wait(barrier, 4)

        def rows_of(dev, off, rows):
            start = pl.multiple_of(N_DEV * off + (4 * dev[0] + 2 * dev[1] + dev[2]) * rows, 8)
            return out_ref.at[pl.ds(start, rows), :]

        def copies(k, block, to, from_src):
            out = []
            for off, rows in regions:
                dst = rows_of(block, off, rows)
                out.append(pltpu.make_async_remote_copy(
                    src_ref=src_ref.at[pl.ds(off, rows), :] if from_src else dst, dst_ref=dst,
                    send_sem=send_sems.at[k], recv_sem=recv_sems.at[k], device_id=to, device_id_type=MESH))
            return out

        def drain(k):
            whole = out_ref.at[pl.ds(0, total), :]
            return pltpu.make_async_remote_copy(src_ref=whole, dst_ref=whole, send_sem=send_sems.at[k],
                                                recv_sem=recv_sems.at[k], device_id=me, device_id_type=MESH)

        for off, rows in regions:
            pltpu.make_async_copy(src_ref.at[pl.ds(off, rows), :], rows_of(me, off, rows), local_sem).start()
        first = copies(0, me, sibling, True)
        for j, chip in enumerate(chips):
            first += copies(1 + j, me, (*chip, c), True)
        for cp in first:
            cp.start()
        for j, chip in enumerate(chips):
            drain(1 + j).wait_recv()
            for cp in copies(4 + j, (*chip, c), sibling, False):
                cp.start()
        drain(0).wait_recv()
        for j in range(3):
            drain(4 + j).wait_recv()
        for k in range(7):
            drain(k).wait_send()
        pltpu.make_async_copy(src_ref, out_ref.at[pl.ds(0, total), :], local_sem).wait()

    return body


def _rs_pair(grads, total, name, collective_id):
    cols = grads[0][0].shape[1]
    sent = sum(rows for _, _, rows in grads)
    n = len(grads)

    def body(*refs):
        g_refs, recv_ref, send_sem, recv_sem = refs[:n], refs[n], refs[n + 1], refs[n + 2]
        x, y, c = _mesh_pos()
        sibling = (x, y, 1 - c)
        barrier = pltpu.get_barrier_semaphore()
        pl.semaphore_signal(barrier, inc=1, device_id=sibling, device_id_type=MESH)
        pl.semaphore_wait(barrier, 1)
        for q in range(4):
            for g_ref, (_, off, rows) in zip(g_refs, grads):
                theirs = g_ref.at[pl.ds(pl.multiple_of((2 * q + 1 - c) * rows, 8), rows), :]
                pltpu.make_async_remote_copy(
                    src_ref=theirs, dst_ref=recv_ref.at[q, pl.ds(off, rows), :], send_sem=send_sem, recv_sem=recv_sem,
                    device_id=sibling, device_id_type=MESH).start()
        everything = recv_ref.at[:, pl.ds(0, sent), :]
        whole = pltpu.make_async_remote_copy(src_ref=everything, dst_ref=everything, send_sem=send_sem,
                                             recv_sem=recv_sem, device_id=sibling, device_id_type=MESH)
        whole.wait_send()
        whole.wait_recv()

    return pl.kernel(
        body, name=name, out_type=jax.ShapeDtypeStruct((4, total, cols), F32),
        mesh=plsc.ScalarSubcoreMesh(axis_name="seq", num_cores=1),
        scratch_types=(pltpu.SemaphoreType.DMA, pltpu.SemaphoreType.DMA),
        compiler_params=pltpu.CompilerParams(collective_id=collective_id))(*[g for g, _, _ in grads])


def _pair_sum(grads, from_sibling, pos, name):
    cols = grads[0][0].shape[1]
    n = len(grads)

    def body(pos_ref, *refs):
        mine = pl.program_id(0) == pos_ref[1]
        for i in range(n):
            s = refs[i][...] + refs[n + i][...]
            refs[2 * n + 2 * i + 1][...] = s.astype(BF16)

            @pl.when(mine)
            def _():
                refs[2 * n + 2 * i][...] = s

    in_specs = [pl.BlockSpec((None, None, rows, cols), lambda q, pos_ref: (q, pos_ref[0], 0, 0)) for _, _, rows in grads]
    in_specs += [pl.BlockSpec((None, rows, cols), lambda q, pos_ref, blk=off // rows: (q, blk, 0)) for _, off, rows in grads]
    out_specs, out_shape = [], []
    for _, _, rows in grads:
        out_specs += [pl.BlockSpec((rows, cols), lambda q, pos_ref: (0, 0)),
                      pl.BlockSpec((None, rows, cols), lambda q, pos_ref: (q, 0, 0))]
        out_shape += [jax.ShapeDtypeStruct((rows, cols), F32), jax.ShapeDtypeStruct((4, rows, cols), BF16)]
    outs = pl.pallas_call(
        body, name=name,
        grid_spec=pltpu.PrefetchScalarGridSpec(num_scalar_prefetch=1, grid=(4,), in_specs=in_specs, out_specs=out_specs),
        out_shape=out_shape,
        compiler_params=_cp(("arbitrary",)))(pos, *[g.reshape(4, 2, rows, cols) for g, _, rows in grads],
                                             *[from_sibling] * n)
    return [(outs[2 * i], outs[2 * i + 1]) for i in range(n)]


def _rs_chips(parts, total, name, collective_id):
    cols = parts[0][0].shape[2]
    sent = sum(rows for _, _, rows in parts)
    n = len(parts)

    def body(*refs):
        p_refs, out_ref, send_sems, recv_sems = refs[:n], refs[n], refs[n + 1], refs[n + 2]
        x, y, c = _mesh_pos()
        chips = _other_chips(x, y)
        barrier = pltpu.get_barrier_semaphore()
        for chip in chips:
            pl.semaphore_signal(barrier, inc=1, device_id=(*chip, c), device_id_type=MESH)
        pl.semaphore_wait(barrier, 3)
        for j, chip in enumerate(chips):
            q = 2 * chip[0] + chip[1]
            for p_ref, (_, off, rows) in zip(p_refs, parts):
                pltpu.make_async_remote_copy(
                    src_ref=p_ref.at[q], dst_ref=out_ref.at[j, pl.ds(off, rows), :], send_sem=send_sems.at[j],
                    recv_sem=recv_sems.at[j], device_id=(*chip, c), device_id_type=MESH).start()
        for j, chip in enumerate(chips):
            everything = out_ref.at[j, pl.ds(0, sent), :]
            whole = pltpu.make_async_remote_copy(src_ref=everything, dst_ref=everything, send_sem=send_sems.at[j],
                                                 recv_sem=recv_sems.at[j], device_id=(*chip, c), device_id_type=MESH)
            whole.wait_recv()
            whole.wait_send()

    return pl.kernel(
        body, name=name, out_type=jax.ShapeDtypeStruct((3, total, cols), BF16),
        mesh=plsc.ScalarSubcoreMesh(axis_name="seq", num_cores=1),
        scratch_types=(pltpu.SemaphoreType.DMA((3,)), pltpu.SemaphoreType.DMA((3,))),
        compiler_params=pltpu.CompilerParams(collective_id=collective_id))(*[p for p, _, _ in parts])


def _chip_sum(p, from_chips, off, rows, name):
    cols = p.shape[1]

    def body(p_ref, r_ref, o_ref):
        acc = p_ref[...]
        for j in range(3):
            acc = acc + r_ref[j].astype(F32)
        o_ref[...] = acc

    return pl.pallas_call(
        body, name=name, grid=(1,),
        in_specs=[pl.BlockSpec((rows, cols), lambda i: (0, 0)),
                  pl.BlockSpec((3, rows, cols), lambda i: (0, off // rows, 0))],
        out_specs=pl.BlockSpec((rows, cols), lambda i: (0, 0)),
        out_shape=jax.ShapeDtypeStruct((rows, cols), F32),
        compiler_params=_cp(("arbitrary",)))(p, from_chips)


def _chip_sum_adamw(p, from_chips, off, rows, w, m, v, tr, name):
    cols = p.shape[1]
    c1 = 1.0 - ADAM_B1 ** ADAM_STEP
    c2 = 1.0 - ADAM_B2 ** ADAM_STEP

    def body(p_ref, r_ref, w_ref, m_ref, v_ref, g_ref, d_ref, mo_ref, vo_ref):
        gv = p_ref[...]
        for j in range(3):
            gv = gv + r_ref[j].astype(F32)
        g_ref[...] = gv
        mn = ADAM_B1 * m_ref[...] + (1.0 - ADAM_B1) * gv
        vn = ADAM_B2 * v_ref[...] + (1.0 - ADAM_B2) * (gv * gv)
        mo_ref[...] = mn
        vo_ref[...] = vn
        d_ref[...] = -ADAM_LR * ((mn / c1) / (jnp.sqrt(vn / c2) + ADAM_EPS) + ADAM_WD * w_ref[...])

    tile = pl.BlockSpec((tr, cols), lambda i: (i, 0))
    shape = jax.ShapeDtypeStruct((rows, cols), F32)
    return pl.pallas_call(
        body, name=name, grid=(rows // tr,),
        in_specs=[tile, pl.BlockSpec((3, tr, cols), lambda i: (0, off // tr + i, 0)), tile, tile, tile],
        out_specs=[tile] * 4,
        out_shape=[shape] * 4,
        compiler_params=_cp(("parallel",)))(p, from_chips, w, m, v)


def _row_tile(rows, cap):
    t = min(rows, cap)
    while rows % t or t % 8:
        t -= 8
    return t


def _ada_mod(c_all, w, b, name):
    n = w.shape[1]

    def body(c_ref, w_ref, b_ref, o_ref):
        cv = c_ref[...]
        o_ref[...] = _exact_dot(cv * _sigmoid(cv), w_ref[...]) + b_ref[...]

    return pl.pallas_call(body, name=name, out_shape=jax.ShapeDtypeStruct((N_DEV, n), F32),
                          compiler_params=pltpu.CompilerParams(vmem_limit_bytes=VMEM_LIMIT))(c_all, w, b)


def _ada_grad(c_all, dmod, name):
    n = dmod.shape[1]

    def body(c_ref, d_ref, o_ref):
        cv = c_ref[...]
        o_ref[...] = _dot(cv * _sigmoid(cv), d_ref[...], 0, 0, lax.Precision.HIGHEST)

    return pl.pallas_call(body, name=name, out_shape=jax.ShapeDtypeStruct((D, n), F32),
                          compiler_params=pltpu.CompilerParams(vmem_limit_bytes=VMEM_LIMIT))(c_all, dmod)


def _adamw(w, g, m, v, name):
    rows, cols = w.shape
    tr = _row_tile(rows, 256) if rows % 8 == 0 else rows
    c1 = 1.0 - ADAM_B1 ** ADAM_STEP
    c2 = 1.0 - ADAM_B2 ** ADAM_STEP

    def body(w_ref, g_ref, m_ref, v_ref, d_ref, mo_ref, vo_ref):
        gv = g_ref[...]
        mn = ADAM_B1 * m_ref[...] + (1.0 - ADAM_B1) * gv
        vn = ADAM_B2 * v_ref[...] + (1.0 - ADAM_B2) * (gv * gv)
        mo_ref[...] = mn
        vo_ref[...] = vn
        d_ref[...] = -ADAM_LR * ((mn / c1) / (jnp.sqrt(vn / c2) + ADAM_EPS) + ADAM_WD * w_ref[...])

    spec = pl.BlockSpec((tr, cols), lambda i: (i, 0))
    shape = jax.ShapeDtypeStruct((rows, cols), F32)
    return pl.pallas_call(body, name=name, grid=(rows // tr,), in_specs=[spec] * 4, out_specs=[spec] * 3,
                          out_shape=[shape] * 3, compiler_params=_cp(("parallel",)))(w, g, m, v)


def _sum8_loss(v, loss_row, name):
    rows = v.shape[0] // N_DEV

    def body(v_ref, o_ref, l_ref):
        acc = v_ref[0:rows, :]
        for k in range(1, N_DEV):
            acc = acc + v_ref[k * rows:(k + 1) * rows, :]
        o_ref[...] = acc
        part = jnp.sum(acc[loss_row:loss_row + 8, :], axis=0, keepdims=True)
        l_ref[...] = jnp.broadcast_to(jnp.sum(part, axis=1, keepdims=True), (8, LANE))

    return pl.pallas_call(body, name=name,
                          out_shape=[jax.ShapeDtypeStruct((rows, LANE), F32), jax.ShapeDtypeStruct((8, LANE), F32)],
                          compiler_params=pltpu.CompilerParams(vmem_limit_bytes=VMEM_LIMIT))(v)


WEIGHT_NAMES = ("w_ada", "b_ada", "ffn1_norm", "ffn1_w_gate", "ffn1_w_up", "ffn1_w_down", "mix_norm", "w_in",
                "conv_w", "conv_b", "dt_bias", "a_log", "d_skip", "ssd_norm_w", "pool_w", "pool_b", "pool_scale",
                "w_out", "ffn2_norm", "ffn2_w_gate", "ffn2_w_up", "ffn2_w_down", "final_norm")

FF_SHARD = FF // N_DEV
IN_SHARD = D_IN // N_DEV
MAIN_SHARD = D_MAIN // N_DEV
EDGE = 16
OUT_SHARD = 2 * D_SSD // N_DEV
ADA_SHARD = 9 * D // N_DEV
POOL_SHARD_ROWS = 4 * 32 * POOL_GW // D
GPACK = dict(w_in=(0, MAIN_SHARD), w_out=(512, OUT_SHARD), pool_w=(768, POOL_SHARD_ROWS),
             gate1=(0, FF_SHARD), up1=(352, FF_SHARD), down1=(704, FF_SHARD),
             gate2=(0, FF_SHARD), up2=(352, FF_SHARD), down2=(704, FF_SHARD))
GROUP_ROWS = 1056

SMALL_ROWS = dict(dmod=(0, 72), ffn1_norm=(72, 8), mix_norm=(80, 8), ffn2_norm=(88, 8), final_norm=(96, 8),
                  ssd_norm_w=(104, 8), pool_scale=(112, 8), conv_b=(120, 16), conv_w=(136, 64), pool_b=(200, 8),
                  ssd=(208, 3), loss=(216, 8), w_in_dt=(224, 128), w_in_head=(352, 128), w_in_tail=(480, 128))
SMALL_TOTAL = 608

MAIN_FROM_OWN = (16, 14, 12, 10, 8, 6, 20, 18)
OWN_FROM_MAIN = (16, 18, 20, 22, 24, 26, 12, 14)


def _rows128(v, rows):
    flat = v.reshape(-1)
    return jnp.pad(flat, (0, rows * LANE - flat.shape[0])).reshape(rows, LANE)


def _pad_lanes(v):
    return jnp.pad(v.reshape(-1), (0, LANE - v.size))


def kernel(x, c, w_ada, b_ada, ffn1_norm, ffn1_w_gate, ffn1_w_up, ffn1_w_down, mix_norm, w_in, conv_w, conv_b, dt_bias, a_log, d_skip, ssd_norm_w, pool_w, pool_b, pool_scale, w_out, ffn2_norm, ffn2_w_gate, ffn2_w_up, ffn2_w_down, final_norm, loss_target, m_w_ada, m_b_ada, m_ffn1_norm, m_ffn1_w_gate, m_ffn1_w_up, m_ffn1_w_down, m_mix_norm, m_w_in, m_conv_w, m_conv_b, m_dt_bias, m_a_log, m_d_skip, m_ssd_norm_w, m_pool_w, m_pool_b, m_pool_scale, m_w_out, m_ffn2_norm, m_ffn2_w_gate, m_ffn2_w_up, m_ffn2_w_down, m_final_norm, v_w_ada, v_b_ada, v_ffn1_norm, v_ffn1_w_gate, v_ffn1_w_up, v_ffn1_w_down, v_mix_norm, v_w_in, v_conv_w, v_conv_b, v_dt_bias, v_a_log, v_d_skip, v_ssd_norm_w, v_pool_w, v_pool_b, v_pool_scale, v_w_out, v_ffn2_norm, v_ffn2_w_gate, v_ffn2_w_up, v_ffn2_w_down, v_final_norm):
    given = dict(locals())
    w = {n: given[n] for n in WEIGHT_NAMES}
    m = {n: given["m_" + n] for n in WEIGHT_NAMES}
    v = {n: given["v_" + n] for n in WEIGHT_NAMES}
    mx, my, mc = _mesh_pos()
    me = 4 * mx + 2 * my + mc

    w_in_t = w_in[0].T
    small = jnp.concatenate([c.reshape(-1), conv_w.reshape(-1), pool_b.reshape(-1), pool_w.reshape(-1),
                             w_in_t[0:EDGE].reshape(-1), w_in_t[IN_SHARD - EDGE:IN_SHARD].reshape(-1)])
    small_rows = 536
    gs = _all_gather(_rows128(small, small_rows), [(0, small_rows)], "ag_small").reshape(N_DEV, small_rows * LANE)
    c_all = gs[:, 0:D]
    conv_w_full = gs[:, 1024:2048].reshape(N_DEV, 4, 256).transpose(1, 0, 2).reshape(4, D_XBC)
    pool_b_full = gs[:, 2048:2176].reshape(N_DEV, 4, 32).transpose(1, 0, 2).reshape(1, D_POOL)
    pool_w_full = gs[:, 2176:2176 + 32768].reshape(N_DEV, 4, 32, POOL_GW).transpose(1, 0, 2, 3).reshape(4, POOL_GW, POOL_GW).astype(BF16)
    heads = gs[:, 34944:34944 + EDGE * D].reshape(N_DEV, EDGE, D)
    tails = gs[:, 34944 + EDGE * D:34944 + 2 * EDGE * D].reshape(N_DEV, EDGE, D)

    prev_tail = lax.dynamic_index_in_dim(tails, jnp.maximum(me - 1, 0), axis=0, keepdims=False)
    next_head = lax.dynamic_index_in_dim(heads, jnp.minimum(me + 1, N_DEV - 1), axis=0, keepdims=False)
    first = jnp.asarray(MAIN_FROM_OWN, jnp.int32)[me]

    def window(rows, before, size):
        total = EDGE + IN_SHARD + EDGE
        padded = jnp.pad(rows, ((before, total - before - rows.shape[0]), (0, 0)))
        return lax.dynamic_slice(padded, (first, 0), (size, D))

    main_shard = (window(prev_tail, 0, MAIN_SHARD) + window(w_in_t, EDGE, MAIN_SHARD)
                  + window(next_head, EDGE + IN_SHARD, MAIN_SHARD))
    dt_rows = jnp.concatenate([tails[5], heads[6]], axis=0)[4:4 + N_HEADS]
    w_dt = jnp.pad(dt_rows, ((0, LANE - N_HEADS), (0, 0))).astype(BF16)

    b_ada_cols = lax.dynamic_slice(b_ada, (0, me * ADA_SHARD), (1, ADA_SHARD))
    mod_part = _ada_mod(c_all, w_ada[0], b_ada_cols, "ada_mod")
    mod_all = _all_gather(mod_part, [(0, N_DEV)], "ag_mod").reshape(N_DEV, N_DEV, ADA_SHARD)
    mod = lax.dynamic_index_in_dim(mod_all, me, axis=1, keepdims=False).reshape(9, D)

    packs = (jnp.concatenate([ffn1_w_gate[0].T, ffn1_w_up[0].T], axis=0).astype(BF16),
             ffn1_w_down[0].astype(BF16),
             main_shard.astype(BF16),
             w_out[0].astype(BF16),
             jnp.concatenate([ffn2_w_gate[0].T, ffn2_w_up[0].T, ffn2_w_down[0]], axis=0).astype(BF16))
    packs, _ = lax.optimization_barrier((packs, c_all))
    ffn_regions = [(0, FF_SHARD), (FF_SHARD, FF_SHARD), (2 * FF_SHARD, FF_SHARD)]
    full_a = _all_gather_async(packs[0], ffn_regions[0:2], "ag_weights_ffn1_in", 1)
    full_d = _all_gather_async(packs[1], ffn_regions[0:1], "ag_weights_ffn1_out", 2)
    full_in = _all_gather_async(packs[2], [(0, MAIN_SHARD)], "ag_weights_in_proj", 9)
    full_out = _all_gather_async(packs[3], [(0, OUT_SHARD)], "ag_weights_out_proj", 10)
    full_2 = _all_gather_async(packs[4], ffn_regions, "ag_weights_ffn2", 11)
    wff1 = dict(gate=(full_a, 0), up=(full_a, 1), down=(full_d, 0))

    def in_proj_weights(x1):
        w_i, x1 = lax.optimization_barrier((full_in, x1))
        return x1, w_i

    def out_proj_weights(ys):
        w_o, ys = lax.optimization_barrier((full_out, ys))
        return ys, w_o

    def ffn2_weights(x2):
        w_2, x2 = lax.optimization_barrier((full_2, x2))
        return x2, dict(gate=(w_2, 0), up=(w_2, 1), down=(w_2, 2))

    later_weights = (in_proj_weights, out_proj_weights, ffn2_weights)

    vecs = dict(ffn1_norm=ffn1_norm, mix_norm=mix_norm, ffn2_norm=ffn2_norm, final_norm=final_norm.reshape(1, D),
                conv_w=conv_w_full, conv_b=conv_b, ssd_norm_w=ssd_norm_w, pool_b=pool_b_full, pool_scale=pool_scale,
                ssd_par=jnp.concatenate([_pad_lanes(dt_bias)[None], _pad_lanes(a_log)[None], _pad_lanes(d_skip)[None],
                                         jnp.zeros((5, LANE), F32)], axis=0))
    dx0, st, dw1, dw3, d_win, d_wout, d_pool_w = _local_step(
        x[0], loss_target[0], mod, wff1, w_dt, later_weights, pool_w_full, vecs, min(512, x.shape[1]))

    dwin, d_w_dt = d_win
    dwout = jnp.concatenate(d_wout, axis=0)
    dpool = d_pool_w.reshape(4, N_DEV, 32, POOL_GW).transpose(1, 0, 2, 3).reshape(N_DEV * POOL_SHARD_ROWS, D)
    pos = jnp.stack([mc, 2 * mx + my]).astype(jnp.int32)
    by_key = dict(zip(("gate1", "up1", "down1", "gate2", "up2", "down2", "w_out", "w_in", "pool_w"),
                      (*dw1, *dw3, dwout, dwin, dpool)))
    reduced = {}
    for tag, keys, cid in (("ffn2", ("gate2", "up2", "down2"), 3), ("mix", ("w_in", "w_out", "pool_w"), 5),
                           ("ffn1", ("gate1", "up1", "down1"), 7)):
        grads = [(by_key[k], *GPACK[k]) for k in keys]
        from_sibling = _rs_pair(grads, GROUP_ROWS, f"rs_pair_{tag}", cid)
        pairs = dict(zip(keys, _pair_sum(grads, from_sibling, pos, f"rs_pair_sum_{tag}")))
        from_chips = _rs_chips([(pairs[k][1], *GPACK[k]) for k in keys], GROUP_ROWS, f"rs_chips_{tag}", cid + 1)
        for k in keys:
            reduced[k] = (pairs[k][0], from_chips)

    delta, new_m, new_v, shard_grad = {}, {}, {}, {}
    fused = dict(gate1=("ffn1_w_gate", True, 176), up1=("ffn1_w_up", True, 176), down1=("ffn1_w_down", False, 176),
                 gate2=("ffn2_w_gate", True, 176), up2=("ffn2_w_up", True, 176), down2=("ffn2_w_down", False, 176),
                 w_out=("w_out", False, 128), pool_w=("pool_w", False, POOL_SHARD_ROWS))
    for k, (n, is_transposed, tr) in fused.items():
        shp = w[n].shape
        rows = GPACK[k][1]
        view = (lambda t: t[0].T) if is_transposed else (lambda t: t.reshape(rows, D))
        back = (lambda t: t.T[None]) if is_transposed else (lambda t: t.reshape(shp))
        g_, d_, m_, v_ = _chip_sum_adamw(reduced[k][0], reduced[k][1], *GPACK[k], view(w[n]), view(m[n]), view(v[n]),
                                         tr, f"adamw_{n}")
        shard_grad[n], delta[n], new_m[n], new_v[n] = back(g_), back(d_), back(m_), back(v_)
    g_main = _chip_sum(reduced["w_in"][0], reduced["w_in"][1], *GPACK["w_in"], "rs_chip_sum_w_in")

    dmod = jnp.concatenate([st["ffn1"][0:3], st["mix"][0:3], st["ffn2"][0:3]], axis=0)
    sg = jnp.concatenate([
        dmod.reshape(-1), st["ffn1"][3], st["mix"][3], st["ffn2"][3], st["loss"][0], st["gn"][0], st["pool"][1],
        st["conv"][4], st["conv"][0:4].reshape(-1), st["pool"][0], st["ssd"][0:3].reshape(-1),
        jnp.zeros((5 * LANE,), F32), st["loss"][1],
        d_w_dt[0:N_HEADS].reshape(-1), g_main[0:EDGE].reshape(-1), g_main[MAIN_SHARD - EDGE:MAIN_SHARD].reshape(-1)])
    sg_all = _all_gather(sg.reshape(SMALL_TOTAL, LANE), [(0, SMALL_TOTAL)], "ag_small_grads")
    tot, loss_b = _sum8_loss(sg_all, SMALL_ROWS["loss"][0], "small_sum")
    loss = loss_b[0, 0]
    per_dev = sg_all.reshape(N_DEV, SMALL_TOTAL * LANE)
    dmod_all = per_dev[:, 0:9 * D]
    g_w_ada = _ada_grad(c_all, lax.dynamic_slice(dmod_all, (0, me * ADA_SHARD), (N_DEV, ADA_SHARD)), "ada_grad")

    def edge_rows(k):
        off, n = SMALL_ROWS[k]
        return per_dev[:, off * LANE:(off + n) * LANE].reshape(N_DEV, EDGE, D)

    g_dt = tot[SMALL_ROWS["w_in_dt"][0]:SMALL_ROWS["w_in_dt"][0] + SMALL_ROWS["w_in_dt"][1]].reshape(N_HEADS, D)
    before = jnp.where(me == 6, g_dt, edge_rows("w_in_tail")[6])
    after = jnp.where(me == 5, g_dt, lax.dynamic_index_in_dim(edge_rows("w_in_head"), jnp.minimum(me + 1, N_DEV - 1),
                                                               axis=0, keepdims=False))
    first_own = jnp.asarray(OWN_FROM_MAIN, jnp.int32)[me]

    def own_window(rows, lead):
        total = EDGE + MAIN_SHARD + EDGE
        padded = jnp.pad(rows, ((lead, total - lead - rows.shape[0]), (0, 0)))
        return lax.dynamic_slice(padded, (first_own, 0), (IN_SHARD, D))

    g_win_t = own_window(before, 0) + own_window(g_main, EDGE) + own_window(after, EDGE + MAIN_SHARD)

    def tot_rows(k):
        off, n = SMALL_ROWS[k]
        return tot[off:off + n].reshape(-1)

    g_conv_w = lax.dynamic_slice(tot_rows("conv_w").reshape(4, D_XBC), (0, me * 256), (4, 256))
    g_pool_b = lax.dynamic_slice(tot_rows("pool_b").reshape(4, POOL_GW), (0, me * 32), (4, 32))
    g_ssd = tot_rows("ssd").reshape(3, LANE)
    grad = {
        "w_ada": g_w_ada[None], "b_ada": tot_rows("dmod").reshape(1, 9 * D),
        "ffn1_norm": tot_rows("ffn1_norm")[None], "mix_norm": tot_rows("mix_norm")[None],
        "ffn2_norm": tot_rows("ffn2_norm")[None], "final_norm": tot_rows("final_norm"),
        "ssd_norm_w": tot_rows("ssd_norm_w")[None], "pool_scale": tot_rows("pool_scale")[None],
        "conv_b": tot_rows("conv_b")[None], "conv_w": g_conv_w[None], "pool_b": g_pool_b[None],
        "dt_bias": g_ssd[0:1, 0:N_HEADS], "a_log": g_ssd[1:2, 0:N_HEADS], "d_skip": g_ssd[2:3, 0:N_HEADS],
        "w_in": g_win_t.T[None], **shard_grad,
    }

    d_, m_, v_ = _adamw(w_ada[0], g_w_ada, m_w_ada[0], v_w_ada[0], "adamw_w_ada")
    delta["w_ada"], new_m["w_ada"], new_v["w_ada"] = d_[None], m_[None], v_[None]
    d_, m_, v_ = _adamw(w_in[0].T, g_win_t, m_w_in[0].T, v_w_in[0].T, "adamw_w_in")
    delta["w_in"], new_m["w_in"], new_v["w_in"] = d_.T[None], m_.T[None], v_.T[None]
    big = ("w_ada", "w_in") + tuple(n for n, _, _ in fused.values())
    small_names = [n for n in WEIGHT_NAMES if n not in big]
    sizes = [LANE if w[n].size < LANE else w[n].size for n in small_names]
    small_rows_adam = -(-sum(sizes) // (8 * LANE)) * 8

    def pack_small(t):
        return _rows128(jnp.concatenate([_pad_lanes(t[n]) if t[n].size < LANE else t[n].reshape(-1) for n in small_names]),
                        small_rows_adam)

    d_s, m_s, v_s = _adamw(pack_small(w), pack_small(grad), pack_small(m), pack_small(v), "adamw_small")
    off = 0
    for n, size in zip(small_names, sizes):
        for res, packed in ((delta, d_s), (new_m, m_s), (new_v, v_s)):
            res[n] = packed.reshape(-1)[off:off + w[n].size].reshape(w[n].shape)
        off += size

    return (loss, dx0[None], *[grad[n] for n in WEIGHT_NAMES], *[delta[n] for n in WEIGHT_NAMES],
            *[new_m[n] for n in WEIGHT_NAMES], *[new_v[n] for n in WEIGHT_NAMES])
```

```python
import jax
import jax.numpy as jnp
from jax import lax
from jax.experimental import pallas as pl
from jax.experimental.pallas import tpu as pltpu
from jax.experimental.pallas import tpu_sc as plsc

F32 = jnp.float32
BF16 = jnp.bfloat16
MESH = pl.DeviceIdType.MESH

N_DEV = 8
D = 1024
FF = 2816
D_SSD = 1024
N_HEADS = 16
HEAD_DIM = 64
N_GROUPS = 4
N_STATE = 128
CHUNK = 128
GROUP_W = D_SSD // N_GROUPS
D_XBC = D_SSD + 2 * N_GROUPS * N_STATE
D_POOL = 1024
POOL_WINDOWS = (2, 4, 8, 16)
POOL_GW = 256
D_IN = 4112
D_MAIN = 4096
COL_Z, COL_XBC, COL_U = 0, 1024, 3072
EPS = 1e-6
FFN_RES = 0.5
LANE = 128
HALO = 16

ADAM_LR, ADAM_B1, ADAM_B2, ADAM_EPS, ADAM_WD, ADAM_STEP = 0.001, 0.9, 0.999, 1e-08, 0.01, 10

VMEM_LIMIT = 56 << 20


def _cp(sem):
    return pltpu.CompilerParams(dimension_semantics=sem, vmem_limit_bytes=VMEM_LIMIT)


def _dot(a, b, ca, cb, prec=None):
    return lax.dot_general(a, b, (((ca,), (cb,)), ((), ())), precision=prec,
                           preferred_element_type=F32)


def _exact_dot(a, b):
    return _dot(a, b, 1, 0, lax.Precision.HIGHEST)


def _sigmoid(v):
    return 1.0 / (1.0 + jnp.exp(-v))


def _silu_grad(v, sg):
    return sg * (1.0 + v * (1.0 - sg))


def _mm_nt(a, bt, tm, tn, out_dtype, name):
    m, k = a.shape
    n = bt.shape[0]

    def body(a_ref, b_ref, o_ref):
        o_ref[...] = _dot(a_ref[...], b_ref[...], 1, 1).astype(out_dtype)

    return pl.pallas_call(
        body, name=name, grid=(n // tn, m // tm),
        in_specs=[pl.BlockSpec((tm, k), lambda j, i: (i, 0)),
                  pl.BlockSpec((tn, k), lambda j, i: (j, 0))],
        out_specs=pl.BlockSpec((tm, tn), lambda j, i: (i, j)),
        out_shape=jax.ShapeDtypeStruct((m, n), out_dtype),
        compiler_params=_cp(("parallel", "parallel")))(a, bt)


def _mm_tn(a, b, tm, tk, name):
    kk, m = a.shape
    n = b.shape[1]
    nk = kk // tk
    if nk == 1:
        def whole(a_ref, b_ref, o_ref):
            o_ref[...] = _dot(a_ref[...], b_ref[...], 0, 0)

        return pl.pallas_call(
            whole, name=name, grid=(m // tm,),
            in_specs=[pl.BlockSpec((kk, tm), lambda i: (0, i)),
                      pl.BlockSpec((kk, n), lambda i: (0, 0))],
            out_specs=pl.BlockSpec((tm, n), lambda i: (i, 0)),
            out_shape=jax.ShapeDtypeStruct((m, n), F32),
            compiler_params=_cp(("parallel",)))(a, b)

    def body(a_ref, b_ref, o_ref, acc):
        k = pl.program_id(1)

        @pl.when(k == 0)
        def _():
            acc[...] = jnp.zeros_like(acc)

        acc[...] += _dot(a_ref[...], b_ref[...], 0, 0)

        @pl.when(k == nk - 1)
        def _():
            o_ref[...] = acc[...]

    return pl.pallas_call(
        body, name=name, grid=(m // tm, nk),
        in_specs=[pl.BlockSpec((tk, tm), lambda i, k: (k, i)),
                  pl.BlockSpec((tk, n), lambda i, k: (k, 0))],
        out_specs=pl.BlockSpec((tm, n), lambda i, k: (i, 0)),
        out_shape=jax.ShapeDtypeStruct((m, n), F32),
        scratch_shapes=[pltpu.VMEM((tm, n), F32)],
        compiler_params=_cp(("parallel", "arbitrary")))(a, b)


def _mm_tn_rows(parts, b, tm, name):
    kk, n = b.shape
    blocks = [a.shape[1] // tm for a in parts]
    starts = [sum(blocks[:p]) for p in range(len(parts))]

    def body(*refs):
        a_refs, b_ref, o_ref = refs[:len(parts)], refs[len(parts)], refs[len(parts) + 1]
        i = pl.program_id(0)
        for p, a_ref in enumerate(a_refs):
            @pl.when(jnp.logical_and(i >= starts[p], i < starts[p] + blocks[p]))
            def _():
                o_ref[...] = _dot(a_ref[...], b_ref[...], 0, 0)

    def part_spec(p):
        return pl.BlockSpec((kk, tm), lambda i: (0, jnp.clip(i - starts[p], 0, blocks[p] - 1)))

    return pl.pallas_call(
        body, name=name, grid=(sum(blocks),),
        in_specs=[part_spec(p) for p in range(len(parts))] + [pl.BlockSpec((kk, n), lambda i: (0, 0))],
        out_specs=pl.BlockSpec((tm, n), lambda i: (i, 0)),
        out_shape=jax.ShapeDtypeStruct((sum(blocks) * tm, n), F32),
        compiler_params=_cp(("parallel",)))(*parts, b)


def _modulated(xv, wn, mod_ref, k):
    r = lax.rsqrt(jnp.mean(xv * xv, axis=-1, keepdims=True) + EPS)
    hn = xv * r * wn
    return (hn * (1.0 + mod_ref[3 * k + 1:3 * k + 2, :]) + mod_ref[3 * k:3 * k + 1, :]).astype(BF16)


def _prenorm(x, wn, mod, k, tm, name):
    seq = x.shape[0]

    def body(x_ref, wn_ref, mod_ref, h_ref):
        h_ref[...] = _modulated(x_ref[...], wn_ref[...], mod_ref, k)

    return pl.pallas_call(
        body, name=name, grid=(seq // tm,),
        in_specs=[pl.BlockSpec((tm, D), lambda i: (i, 0)),
                  pl.BlockSpec((1, D), lambda i: (0, 0)),
                  pl.BlockSpec((9, D), lambda i: (0, 0))],
        out_specs=pl.BlockSpec((tm, D), lambda i: (i, 0)),
        out_shape=jax.ShapeDtypeStruct((seq, D), BF16),
        compiler_params=_cp(("parallel",)))(x, wn, mod)


def _norm_bwd(dh, xv, dxo, branch, wn, sc, res, stats_ref, first):
    r = lax.rsqrt(jnp.mean(xv * xv, axis=-1, keepdims=True) + EPS)
    xn = xv * r
    dhn = dh * (1.0 + sc)
    dxn = dhn * wn
    dx = dxo + r * (dxn - xn * jnp.mean(dxn * xn, axis=-1, keepdims=True))
    rows = jnp.concatenate([
        jnp.sum(dh, axis=0, keepdims=True),
        jnp.sum(dh * (xn * wn), axis=0, keepdims=True),
        jnp.sum(branch * dxo, axis=0, keepdims=True) * res,
        jnp.sum(dhn * xn, axis=0, keepdims=True),
        jnp.zeros((4, D), F32)], axis=0)

    @pl.when(first)
    def _():
        stats_ref[...] = rows

    @pl.when(jnp.logical_not(first))
    def _():
        stats_ref[...] += rows

    return dx


def _ffn_up(h, wg, wu, tm, tn, name):
    seq = h.shape[0]
    nj = FF // tn

    def body(h_ref, wg_ref, wu_ref, a_ref, pg_ref, pu_ref):
        hv = h_ref[...]
        g = _dot(hv, wg_ref[...], 1, 1)
        u = _dot(hv, wu_ref[...], 1, 1)
        sg = _sigmoid(g)
        s = g * sg
        a_ref[...] = (s * u).astype(BF16)
        pg_ref[...] = (u * _silu_grad(g, sg)).astype(BF16)
        pu_ref[...] = s.astype(BF16)

    act = pl.BlockSpec((tm, tn), lambda j, i: (i, j))
    return pl.pallas_call(
        body, name=name, grid=(nj, seq // tm),
        in_specs=[pl.BlockSpec((tm, D), lambda j, i: (i, 0)),
                  pl.BlockSpec((tn, D), lambda j, i: (wg[1] * nj + j, 0)),
                  pl.BlockSpec((tn, D), lambda j, i: (wu[1] * nj + j, 0))],
        out_specs=[act, act, act],
        out_shape=[jax.ShapeDtypeStruct((seq, FF), BF16)] * 3,
        compiler_params=_cp(("parallel", "parallel")))(h, wg[0], wu[0])


def _ffn_down(a, w, blk, x, mod, grow, wn_next, k_next, tm, name):
    seq = a.shape[0]
    chain = k_next is not None

    def body(a_ref, w_ref, x_ref, mod_ref, wn_ref, xo_ref, f_ref, *rest):
        f = _dot(a_ref[...], w_ref[...], 1, 0)
        f_ref[...] = f.astype(BF16)
        xo = x_ref[...] + (FFN_RES * mod_ref[grow:grow + 1, :]) * f
        xo_ref[...] = xo
        if chain:
            rest[0][...] = _modulated(xo, wn_ref[...], mod_ref, k_next)

    tok = pl.BlockSpec((tm, D), lambda i: (i, 0))
    return pl.pallas_call(
        body, name=name, grid=(seq // tm,),
        in_specs=[pl.BlockSpec((tm, FF), lambda i: (i, 0)),
                  pl.BlockSpec((FF, D), lambda i: (blk, 0)),
                  tok,
                  pl.BlockSpec((9, D), lambda i: (0, 0)),
                  pl.BlockSpec((1, D), lambda i: (0, 0))],
        out_specs=[tok, tok] + ([tok] if chain else []),
        out_shape=[jax.ShapeDtypeStruct((seq, D), F32), jax.ShapeDtypeStruct((seq, D), BF16)]
        + ([jax.ShapeDtypeStruct((seq, D), BF16)] if chain else []),
        compiler_params=_cp(("parallel",)))(a, w, x, mod, wn_next)


def _ffn_down_loss(a, w, blk, x, mod, grow, wf, tgt, tm, name):
    seq = a.shape[0]

    def body(a_ref, w_ref, x_ref, mod_ref, wf_ref, t_ref, f_ref, dx_ref, df_ref, st_ref):
        f = _dot(a_ref[...], w_ref[...], 1, 0)
        f_ref[...] = f.astype(BF16)
        xv = x_ref[...] + (FFN_RES * mod_ref[grow:grow + 1, :]) * f
        wv = wf_ref[...]
        r = lax.rsqrt(jnp.mean(xv * xv, axis=-1, keepdims=True) + EPS)
        xn = xv * r
        e = xn * wv - t_ref[...]
        dy = e * (1.0 / D)
        dxn = dy * wv
        dx = r * (dxn - xn * jnp.mean(dxn * xn, axis=-1, keepdims=True))
        dx_ref[...] = dx
        df_ref[...] = (dx * (FFN_RES * mod_ref[grow:grow + 1, :])).astype(BF16)
        rows = jnp.concatenate([
            jnp.sum(dy * xn, axis=0, keepdims=True),
            jnp.sum(e * e, axis=0, keepdims=True) * (0.5 / D),
            jnp.zeros((6, D), F32)], axis=0)

        @pl.when(pl.program_id(0) == 0)
        def _():
            st_ref[...] = rows

        @pl.when(pl.program_id(0) != 0)
        def _():
            st_ref[...] += rows

    tok = pl.BlockSpec((tm, D), lambda i: (i, 0))
    return pl.pallas_call(
        body, name=name, grid=(seq // tm,),
        in_specs=[pl.BlockSpec((tm, FF), lambda i: (i, 0)),
                  pl.BlockSpec((FF, D), lambda i: (blk, 0)),
                  tok,
                  pl.BlockSpec((9, D), lambda i: (0, 0)),
                  pl.BlockSpec((1, D), lambda i: (0, 0)),
                  tok],
        out_specs=[tok, tok, tok, pl.BlockSpec((8, D), lambda i: (0, 0))],
        out_shape=[jax.ShapeDtypeStruct((seq, D), BF16), jax.ShapeDtypeStruct((seq, D), F32),
                   jax.ShapeDtypeStruct((seq, D), BF16), jax.ShapeDtypeStruct((8, D), F32)],
        compiler_params=_cp(("arbitrary",)))(a, w, x, mod, wf, tgt)


def _ffn_bwd_da(df, w, blk, pg, pu, tm, tn, name):
    seq = df.shape[0]
    nj = FF // tn

    def body(df_ref, w_ref, pg_ref, pu_ref, dg_ref, du_ref):
        da = _dot(df_ref[...], w_ref[...], 1, 1)
        dg_ref[...] = (da * pg_ref[...].astype(F32)).astype(BF16)
        du_ref[...] = (da * pu_ref[...].astype(F32)).astype(BF16)

    act = pl.BlockSpec((tm, tn), lambda j, i: (i, j))
    return pl.pallas_call(
        body, name=name, grid=(nj, seq // tm),
        in_specs=[pl.BlockSpec((tm, D), lambda j, i: (i, 0)),
                  pl.BlockSpec((tn, D), lambda j, i: (blk * nj + j, 0)),
                  act, act],
        out_specs=[act, act],
        out_shape=[jax.ShapeDtypeStruct((seq, FF), BF16)] * 2,
        compiler_params=_cp(("parallel", "parallel")))(df, w, pg, pu)


def _ffn_bwd_dh(dg, du, wg, wu, x, dxo, fb, wn, mod, k, nxt, tm, name):
    seq = x.shape[0]

    def body(dg_ref, du_ref, wg_ref, wu_ref, x_ref, dxo_ref, f_ref, wn_ref, mod_ref, dx_ref, st_ref, *rest):
        dh = _dot(dg_ref[...], wg_ref[...], 1, 0) + _dot(du_ref[...], wu_ref[...], 1, 0)
        dx = _norm_bwd(dh, x_ref[...], dxo_ref[...], f_ref[...].astype(F32), wn_ref[...],
                       mod_ref[3 * k + 1:3 * k + 2, :], FFN_RES, st_ref, pl.program_id(0) == 0)
        dx_ref[...] = dx
        if nxt is not None:
            rest[0][...] = (dx * (nxt[1] * mod_ref[nxt[0]:nxt[0] + 1, :])).astype(BF16)

    tok = pl.BlockSpec((tm, D), lambda i: (i, 0))
    act = pl.BlockSpec((tm, FF), lambda i: (i, 0))
    return pl.pallas_call(
        body, name=name, grid=(seq // tm,),
        in_specs=[act, act,
                  pl.BlockSpec((FF, D), lambda i: (wg[1], 0)),
                  pl.BlockSpec((FF, D), lambda i: (wu[1], 0)),
                  tok, tok, tok,
                  pl.BlockSpec((1, D), lambda i: (0, 0)),
                  pl.BlockSpec((9, D), lambda i: (0, 0))],
        out_specs=[tok, pl.BlockSpec((8, D), lambda i: (0, 0))] + ([tok] if nxt is not None else []),
        out_shape=[jax.ShapeDtypeStruct((seq, D), F32), jax.ShapeDtypeStruct((8, D), F32)]
        + ([jax.ShapeDtypeStruct((seq, D), BF16)] if nxt is not None else []),
        compiler_params=_cp(("arbitrary",)))(dg, du, wg[0], wu[0], x, dxo, fb, wn, mod)


def _ffn_forward(x, h, w, mod, k, wn_next, k_next, tm, tag):
    a, pg, pu = _ffn_up(h, w["gate"], w["up"], tm, FF // 2, f"{tag}_up")
    outs = _ffn_down(a, *w["down"], x, mod, 3 * k + 2, wn_next, k_next, tm, f"{tag}_down")
    return outs[0], (outs[2] if k_next is not None else None), (x, h, pg, pu, a, outs[1])


def _ffn_backward(dxo, df, saved, w, wn, mod, k, nxt, tm, tag):
    x, h, pg, pu, a, fb = saved
    dg, du = _ffn_bwd_da(df, *w["down"], pg, pu, tm, FF // 2, f"{tag}_bwd_da")
    seq = x.shape[0]
    d_gate_t = _mm_tn(dg, h, 256, seq, f"{tag}_dw_gate")
    d_up_t = _mm_tn(du, h, 256, seq, f"{tag}_dw_up")
    d_down = _mm_tn(a, df, 256, seq, f"{tag}_dw_down")
    dws, dg, du = lax.optimization_barrier(((d_gate_t, d_up_t, d_down), dg, du))
    outs = _ffn_bwd_dh(dg, du, w["gate"], w["up"], x, dxo, fb, wn, mod, k, nxt, min(tm, 256), f"{tag}_bwd_dh")
    return outs[0], (outs[2] if nxt is not None else None), outs[1], dws


def _prev_rows(tm, col):
    return pl.BlockSpec((HALO, 1024), lambda i, j: (jnp.maximum(i * (tm // HALO) - 1, 0), col + j))


def _conv_pre(ext, cw, cb, rows):
    pre = cb + cw[3:4, :] * ext
    for s in (1, 2, 3):
        pre = pre + cw[3 - s:4 - s, :] * pltpu.roll(ext, s, 0)
    return pre[HALO:HALO + rows]


def _conv_fwd(proj, cw, cb, tm, name):
    seq = proj.shape[0]

    def body(x_ref, p_ref, cw_ref, cb_ref, o_ref, g_ref):
        prev = jnp.where(pl.program_id(0) == 0, 0.0, p_ref[...])
        ext = jnp.concatenate([prev, x_ref[...]], axis=0)
        pre = _conv_pre(ext, cw_ref[...], cb_ref[...], tm)
        sg = _sigmoid(pre)
        o_ref[...] = pre * sg
        g_ref[...] = _silu_grad(pre, sg).astype(BF16)

    c0 = COL_XBC // 1024
    out = pl.BlockSpec((tm, 1024), lambda i, j: (i, j))
    return pl.pallas_call(
        body, name=name, grid=(seq // tm, 2),
        in_specs=[pl.BlockSpec((tm, 1024), lambda i, j: (i, c0 + j)),
                  _prev_rows(tm, c0),
                  pl.BlockSpec((4, 1024), lambda i, j: (0, j)),
                  pl.BlockSpec((1, 1024), lambda i, j: (0, j))],
        out_specs=[out, out],
        out_shape=[jax.ShapeDtypeStruct((seq, D_XBC), F32), jax.ShapeDtypeStruct((seq, D_XBC), BF16)],
        compiler_params=_cp(("parallel", "parallel")))(proj, proj, cw, cb)


def _conv_bwd(dact, slope, proj, cw, tm, name):
    seq = proj.shape[0]
    ni = seq // tm

    def body(d_ref, dn_ref, s_ref, sn_ref, x_ref, p_ref, cw_ref, o_ref, st_ref):
        i = pl.program_id(1)
        cwv = cw_ref[...]
        prev = jnp.where(i == 0, 0.0, p_ref[...])
        ext = jnp.concatenate([prev, x_ref[...]], axis=0)
        dnext = jnp.where(i == ni - 1, 0.0, dn_ref[...] * sn_ref[...].astype(F32))
        dpre = jnp.concatenate([d_ref[...] * s_ref[...].astype(F32), dnext], axis=0)
        n = tm + HALO
        dx = cwv[3:4, :] * dpre
        for s in (1, 2, 3):
            dx = dx + cwv[3 - s:4 - s, :] * pltpu.roll(dpre, n - s, 0)
        o_ref[...] = dx[:tm].astype(BF16)
        dcur = dpre[:tm]
        rows = [jnp.sum(dcur * pltpu.roll(ext, 3 - k, 0)[HALO:HALO + tm], axis=0, keepdims=True) for k in range(3)]
        rows.append(jnp.sum(dcur * ext[HALO:HALO + tm], axis=0, keepdims=True))
        rows.append(jnp.sum(dcur, axis=0, keepdims=True))
        rows.append(jnp.zeros((3, 1024), F32))
        rows = jnp.concatenate(rows, axis=0)

        @pl.when(i == 0)
        def _():
            st_ref[...] = rows

        @pl.when(i != 0)
        def _():
            st_ref[...] += rows

    c0 = COL_XBC // 1024
    cur = pl.BlockSpec((tm, 1024), lambda j, i: (i, j))
    nxt = pl.BlockSpec((HALO, 1024), lambda j, i: (jnp.minimum((i + 1) * (tm // HALO), seq // HALO - 1), j))
    return pl.pallas_call(
        body, name=name, grid=(2, ni),
        in_specs=[cur, nxt, cur, nxt,
                  pl.BlockSpec((tm, 1024), lambda j, i: (i, c0 + j)),
                  pl.BlockSpec((HALO, 1024), lambda j, i: (jnp.maximum(i * (tm // HALO) - 1, 0), c0 + j)),
                  pl.BlockSpec((4, 1024), lambda j, i: (0, j))],
        out_specs=[cur, pl.BlockSpec((8, 1024), lambda j, i: (0, j))],
        out_shape=[jax.ShapeDtypeStruct((seq, D_XBC), BF16), jax.ShapeDtypeStruct((8, D_XBC), F32)],
        compiler_params=_cp(("parallel", "arbitrary")))(dact, dact, slope, slope, proj, proj, cw)


def _bf16_parts(x, n):
    parts, rest = [], x
    for _ in range(n):
        p = rest.astype(BF16)
        parts.append(p)
        rest = rest - p.astype(F32)
    return parts


def _pick(x, sel, n):
    m = x.shape[0]
    prod = _dot(jnp.concatenate(_bf16_parts(x, n), axis=0), sel, 1, 0)
    acc = prod[0:m]
    for i in range(1, n):
        acc = acc + prod[i * m:(i + 1) * m]
    return acc


def _running(mask, x, n):
    k = x.shape[1]
    prod = _dot(mask, jnp.concatenate(_bf16_parts(x, n), axis=1), 1, 0)
    acc = prod[:, 0:k]
    for i in range(1, n):
        acc = acc + prod[:, i * k:(i + 1) * k]
    return acc


def _head_expand():
    r = lax.broadcasted_iota(jnp.int32, (LANE, D_SSD), 0)
    c = lax.broadcasted_iota(jnp.int32, (LANE, D_SSD), 1)
    return (c // HEAD_DIM == r).astype(BF16)


def _head_reduce():
    r = lax.broadcasted_iota(jnp.int32, (D_SSD, LANE), 0)
    c = lax.broadcasted_iota(jnp.int32, (D_SSD, LANE), 1)
    return (r // HEAD_DIM == c).astype(BF16)


def _ssd_common(dtr, par):
    q = CHUNK
    v = dtr + par[0:1, :]
    dt = jnp.maximum(v, 0.0) + jnp.log(1.0 + jnp.exp(-jnp.abs(v)))
    a = -jnp.exp(par[1:2, :])
    adt = dt * a
    li = lax.broadcasted_iota(jnp.int32, (q, q), 0)
    si = lax.broadcasted_iota(jnp.int32, (q, q), 1)
    causal = li >= si
    acs = _running(causal.astype(BF16), adt, 3)
    expand = _head_expand()
    both_l = _pick(jnp.concatenate([dt, acs], axis=0), expand, 3)
    dt_l, acs_l = both_l[0:q], both_l[q:2 * q]
    dskip_l = _pick(jnp.broadcast_to(par[2:3, :], (16, LANE)), expand, 3)[0:1, :]
    last_l = acs_l[q - 1:q, :]
    return dict(v=v, dt=dt, a=a, acs=acs, acs_t=acs.T, causal=causal, dt_l=dt_l, acs_l=acs_l,
                ea_l=jnp.exp(acs_l), ds_l=jnp.exp(last_l - acs_l), cd_l=jnp.exp(last_l), dskip_l=dskip_l)


def _decay(cm, h):
    seg = cm["acs"][:, h:h + 1] - cm["acs_t"][h:h + 1, :]
    return jnp.exp(jnp.where(cm["causal"], seg, -jnp.inf))


def _lane_mask(r):
    lane = lax.broadcasted_iota(jnp.int32, (1, GROUP_W), 1)
    return lane // HEAD_DIM == r


def _ssd_fwd(xbc, proj, par, name):
    seq = xbc.shape[0]
    nc = seq // CHUNK
    q = CHUNK

    def body(x_ref, dt_ref, par_ref, y_ref, hp_ref, state):
        @pl.when(pl.program_id(0) == 0)
        def _():
            state[...] = jnp.zeros_like(state)

        cm = _ssd_common(dt_ref[...], par_ref[...])
        for g in range(N_GROUPS):
            lo = g * GROUP_W
            xs = x_ref[:, lo:lo + GROUP_W]
            bm = x_ref[:, D_SSD + g * N_STATE:D_SSD + (g + 1) * N_STATE].astype(BF16)
            cmat = x_ref[:, D_SSD + N_GROUPS * N_STATE + g * N_STATE:D_SSD + N_GROUPS * N_STATE + (g + 1) * N_STATE].astype(BF16)
            xdt = xs * cm["dt_l"][:, lo:lo + GROUP_W]
            xdt_b = xdt.astype(BF16)
            cb = _dot(cmat, bm, 1, 1)
            scores = jnp.concatenate([(cb * _decay(cm, 4 * g + r)).astype(BF16) for r in range(4)], axis=0)
            yd_heads = _dot(scores, xdt_b, 1, 0)
            yd = yd_heads[0:q]
            for r in range(1, 4):
                yd = jnp.where(_lane_mask(r), yd_heads[r * q:(r + 1) * q], yd)
            hg = state[g]
            hp_ref[0, g] = hg
            yo = _dot(cmat, hg.astype(BF16), 1, 0) * cm["ea_l"][:, lo:lo + GROUP_W]
            y_ref[:, lo:lo + GROUP_W] = yd + yo + cm["dskip_l"][:, lo:lo + GROUP_W] * xs
            xds = (xdt * cm["ds_l"][:, lo:lo + GROUP_W]).astype(BF16)
            state[g] = hg * cm["cd_l"][:, lo:lo + GROUP_W] + _dot(bm, xds, 0, 0)

    return pl.pallas_call(
        body, name=name, grid=(nc,),
        in_specs=[pl.BlockSpec((q, D_XBC), lambda c: (c, 0)),
                  pl.BlockSpec((q, LANE), lambda c: (c, 0)),
                  pl.BlockSpec((8, LANE), lambda c: (0, 0))],
        out_specs=[pl.BlockSpec((q, D_SSD), lambda c: (c, 0)),
                   pl.BlockSpec((1, N_GROUPS, N_STATE, GROUP_W), lambda c: (c, 0, 0, 0))],
        out_shape=[jax.ShapeDtypeStruct((seq, D_SSD), F32),
                   jax.ShapeDtypeStruct((nc, N_GROUPS, N_STATE, GROUP_W), F32)],
        scratch_shapes=[pltpu.VMEM((N_GROUPS, N_STATE, GROUP_W), F32)],
        compiler_params=_cp(("arbitrary",)))(xbc, proj, par)


def _ssd_bwd(dy, xbc, proj, par, hprev, name):
    seq = xbc.shape[0]
    nc = seq // CHUNK
    q = CHUNK

    def body(dy_ref, x_ref, dt_ref, par_ref, hp_ref, dx_ref, ddt_ref, st_ref, dstate):
        step = pl.program_id(0)

        @pl.when(step == 0)
        def _():
            dstate[...] = jnp.zeros_like(dstate)

        par = par_ref[...]
        cm = _ssd_common(dt_ref[...], par)
        reduce = _head_reduce()
        lane128 = lax.broadcasted_iota(jnp.int32, (1, LANE), 1)
        row128 = lax.broadcasted_iota(jnp.int32, (LANE, 1), 0)
        d_acs = jnp.zeros((q, LANE), F32)
        d_acs_t = jnp.zeros((LANE, q), F32)
        last_terms = []
        acs_terms = []
        dxdt_all = []
        for g in range(N_GROUPS):
            lo = g * GROUP_W
            sl = slice(lo, lo + GROUP_W)
            xs = x_ref[:, sl]
            bm32 = x_ref[:, D_SSD + g * N_STATE:D_SSD + (g + 1) * N_STATE]
            cm32 = x_ref[:, D_SSD + N_GROUPS * N_STATE + g * N_STATE:D_SSD + N_GROUPS * N_STATE + (g + 1) * N_STATE]
            bm = bm32.astype(BF16)
            cmat = cm32.astype(BF16)
            dyg = dy_ref[:, sl]
            dyg_b = dyg.astype(BF16)
            xdt = xs * cm["dt_l"][:, sl]
            xdt_b = xdt.astype(BF16)
            hg = hp_ref[0, g]
            hg_b = hg.astype(BF16)
            dhg = dstate[g]
            dhg_b = dhg.astype(BF16)
            ea = cm["ea_l"][:, sl]
            ds = cm["ds_l"][:, sl]
            cd = cm["cd_l"][:, sl]
            yoff = _dot(cmat, hg_b, 1, 0) * ea
            dw = (dyg * ea).astype(BF16)
            d_c = _dot(dw, hg_b, 1, 1)
            d_hprev = _dot(cmat, dw, 0, 0) + dhg * cd
            t_acs = dyg * yoff
            d_last_g = jnp.sum(dhg * hg, axis=0, keepdims=True) * cd
            xds_b = (xdt * ds).astype(BF16)
            dxds = _dot(bm, dhg_b, 1, 0)
            d_b = _dot(xds_b, dhg_b, 1, 1)
            dxdt = dxds * ds
            t_ds = dxds * xdt * ds
            t_acs = t_acs - t_ds
            d_last_g = d_last_g + jnp.sum(t_ds, axis=0, keepdims=True)
            cb = _dot(cmat, bm, 1, 1)
            d_cb = jnp.zeros((q, q), F32)
            decays = [_decay(cm, 4 * g + r) for r in range(4)]
            score_heads = [cb * dec for dec in decays]
            d_s_heads = _dot(jnp.concatenate([jnp.where(_lane_mask(r), dyg, 0.0).astype(BF16) for r in range(4)], axis=0),
                             xdt_b, 1, 1)
            dxdt_heads = _dot(jnp.concatenate([s.astype(BF16) for s in score_heads], axis=1), dyg_b, 0, 0)
            for r in range(4):
                h = 4 * g + r
                dec, s_h = decays[r], score_heads[r]
                d_s = d_s_heads[r * q:(r + 1) * q]
                dxdt = dxdt + jnp.where(_lane_mask(r), dxdt_heads[r * q:(r + 1) * q], 0.0)
                d_cb = d_cb + d_s * dec
                d_m = d_s * s_h
                d_acs = d_acs + jnp.where(lane128 == h, jnp.sum(d_m, axis=1, keepdims=True), 0.0)
                d_acs_t = d_acs_t + jnp.where(row128 == h, jnp.sum(d_m, axis=0, keepdims=True), 0.0)
            d_cb_b = d_cb.astype(BF16)
            d_c = d_c + _dot(d_cb_b, bm, 1, 0)
            d_b = d_b + _dot(d_cb_b, cmat, 0, 0)
            dstate[g] = d_hprev
            dx_ref[:, sl] = dxdt * cm["dt_l"][:, sl] + cm["dskip_l"][:, sl] * dyg
            dx_ref[:, D_SSD + g * N_STATE:D_SSD + (g + 1) * N_STATE] = d_b
            dx_ref[:, D_SSD + N_GROUPS * N_STATE + g * N_STATE:D_SSD + N_GROUPS * N_STATE + (g + 1) * N_STATE] = d_c
            acs_terms.append(t_acs)
            dxdt_all.append(dxdt * xs)
            last_terms.append(d_last_g)
        t_acs_l = jnp.concatenate(acs_terms, axis=1)
        d_dt_l = jnp.concatenate(dxdt_all, axis=1)
        d_last_l = jnp.concatenate(last_terms, axis=1)
        per_head = _pick(jnp.concatenate([t_acs_l, d_dt_l], axis=0), reduce, 2)
        skip_l = jnp.sum(dy_ref[...] * x_ref[:, 0:D_SSD], axis=0, keepdims=True)
        singles = _pick(jnp.concatenate([d_last_l, skip_l, jnp.zeros((14, D_SSD), F32)], axis=0), reduce, 3)
        d_acs = d_acs + per_head[0:q] - d_acs_t.T
        last_row = lax.broadcasted_iota(jnp.int32, (q, 1), 0) == q - 1
        d_acs = d_acs + jnp.where(last_row, singles[0:1, :], 0.0)
        li = lax.broadcasted_iota(jnp.int32, (q, q), 0)
        si = lax.broadcasted_iota(jnp.int32, (q, q), 1)
        d_adt = _running((si >= li).astype(BF16), d_acs, 3)
        d_dt = per_head[q:2 * q] + d_adt * cm["a"]
        d_dtr = d_dt * _sigmoid(cm["v"])
        ddt_ref[...] = d_dtr.astype(BF16)
        d_skip = singles[1:2, :]
        rows = jnp.concatenate([
            jnp.sum(d_dtr, axis=0, keepdims=True),
            jnp.sum(d_adt * cm["dt"], axis=0, keepdims=True) * cm["a"],
            d_skip,
            jnp.zeros((5, LANE), F32)], axis=0)

        @pl.when(step == 0)
        def _():
            st_ref[...] = rows

        @pl.when(step != 0)
        def _():
            st_ref[...] += rows

    rev = lambda c: nc - 1 - c
    return pl.pallas_call(
        body, name=name, grid=(nc,),
        in_specs=[pl.BlockSpec((q, D_SSD), lambda c: (rev(c), 0)),
                  pl.BlockSpec((q, D_XBC), lambda c: (rev(c), 0)),
                  pl.BlockSpec((q, LANE), lambda c: (rev(c), 0)),
                  pl.BlockSpec((8, LANE), lambda c: (0, 0)),
                  pl.BlockSpec((1, N_GROUPS, N_STATE, GROUP_W), lambda c: (rev(c), 0, 0, 0))],
        out_specs=[pl.BlockSpec((q, D_XBC), lambda c: (rev(c), 0)),
                   pl.BlockSpec((q, LANE), lambda c: (rev(c), 0)),
                   pl.BlockSpec((8, LANE), lambda c: (0, 0))],
        out_shape=[jax.ShapeDtypeStruct((seq, D_XBC), F32),
                   jax.ShapeDtypeStruct((seq, LANE), BF16),
                   jax.ShapeDtypeStruct((8, LANE), F32)],
        scratch_shapes=[pltpu.VMEM((N_GROUPS, N_STATE, GROUP_W), F32)],
        compiler_params=_cp(("arbitrary",)))(dy, xbc, proj, par, hprev)


def _gate_norm_bwd(dmix, wout, y, proj, wn, tm, name):
    seq = y.shape[0]

    def body(dm_ref, wo_ref, y_ref, z_ref, w_ref, dy_ref, dz_ref, st_ref):
        d_ys = _dot(dm_ref[...], wo_ref[...], 1, 1)
        rows = []
        for g in range(N_GROUPS):
            sl = slice(g * GROUP_W, (g + 1) * GROUP_W)
            zv = z_ref[:, sl]
            yv = y_ref[:, sl]
            sg = _sigmoid(zv)
            sz = zv * sg
            yz = yv * sz
            r = lax.rsqrt(jnp.mean(yz * yz, axis=-1, keepdims=True) + EPS)
            yn = yz * r
            dv = d_ys[:, sl]
            dyn = dv * w_ref[:, sl]
            dyz = r * (dyn - yn * jnp.mean(dyn * yn, axis=-1, keepdims=True))
            dy_ref[:, sl] = dyz * sz
            dz_ref[:, sl] = (dyz * yv * _silu_grad(zv, sg)).astype(BF16)
            rows.append(jnp.sum(dv * yn, axis=0, keepdims=True))
        rows = jnp.concatenate([jnp.concatenate(rows, axis=1), jnp.zeros((7, D_SSD), F32)], axis=0)

        @pl.when(pl.program_id(0) == 0)
        def _():
            st_ref[...] = rows

        @pl.when(pl.program_id(0) != 0)
        def _():
            st_ref[...] += rows

    tok = pl.BlockSpec((tm, D_SSD), lambda i: (i, 0))
    return pl.pallas_call(
        body, name=name, grid=(seq // tm,),
        in_specs=[tok, pl.BlockSpec((D_SSD, D), lambda i: (0, 0)), tok, tok, pl.BlockSpec((1, D_SSD), lambda i: (0, 0))],
        out_specs=[tok, tok, pl.BlockSpec((8, D_SSD), lambda i: (0, 0))],
        out_shape=[jax.ShapeDtypeStruct((seq, D_SSD), F32), jax.ShapeDtypeStruct((seq, D_SSD), BF16),
                   jax.ShapeDtypeStruct((8, D_SSD), F32)],
        compiler_params=_cp(("arbitrary",)))(dmix, wout, y, proj, wn)


def _pool_counts(t0, rows, w):
    pos = (t0 + 1 + lax.broadcasted_iota(jnp.int32, (rows, 1), 0)).astype(F32)
    return jnp.minimum(pos, float(w))


def _window_means(ext, t0):
    n = ext.shape[0]
    outs = []
    run = ext
    width = 1
    sums = {}
    while width < 16:
        run = run + pltpu.roll(run, width, 0)
        width *= 2
        sums[width] = run
    for g, w in enumerate(POOL_WINDOWS):
        sl = slice(g * POOL_GW, (g + 1) * POOL_GW)
        cnt = _pool_counts(t0, n - HALO, w)
        outs.append(sums[w][HALO:, sl] / cnt - ext[HALO:, sl])
    return outs


def _pool_bwd(dmix, wout, proj, pw, pb, ps, tm, name):
    seq = proj.shape[0]
    ni = seq // tm

    def body(dm_ref, dmn_ref, wo_ref, u_ref, p_ref, pw_ref, pb_ref, ps_ref, du_ref, dw_ref, st_ref):
        i = pl.program_id(0)
        prev = jnp.where(i == 0, 0.0, p_ref[...])
        ext = jnp.concatenate([prev, u_ref[...]], axis=0)
        diffs = _window_means(ext, i * tm)
        n = tm + HALO
        dext = _dot(jnp.concatenate([dm_ref[...], dmn_ref[...]], axis=0), wo_ref[...], 1, 1)
        past_end = jnp.logical_and(i == ni - 1, lax.broadcasted_iota(jnp.int32, (n, 1), 0) >= tm)
        dext = jnp.where(past_end, 0.0, dext)
        b_rows, s_rows = [], []
        for g, w in enumerate(POOL_WINDOWS):
            sl = slice(g * POOL_GW, (g + 1) * POOL_GW)
            wg = pw_ref[g]
            dout = dext[:, sl] * ps_ref[:, sl]
            dcur = dout[:tm]
            pre = _dot(diffs[g].astype(BF16), wg, 1, 0) + pb_ref[:, sl]
            s_rows.append(jnp.sum(dext[:tm, sl] * pre, axis=0, keepdims=True))
            b_rows.append(jnp.sum(dcur, axis=0, keepdims=True))
            dwg = _dot(diffs[g].astype(BF16), dcur.astype(BF16), 0, 0)

            @pl.when(i == 0)
            def _():
                dw_ref[g] = dwg

            @pl.when(i != 0)
            def _():
                dw_ref[g] += dwg

            ddiff = _dot(dout.astype(BF16), wg, 1, 1)
            scaled = ddiff / _pool_counts(i * tm, n, w)
            run = scaled
            width = 1
            while width < w:
                run = run + pltpu.roll(run, n - width, 0)
                width *= 2
            du_ref[:, sl] = (run[:tm] - ddiff[:tm]).astype(BF16)
        rows = jnp.concatenate([jnp.concatenate(b_rows, axis=1), jnp.concatenate(s_rows, axis=1),
                                jnp.zeros((6, D_POOL), F32)], axis=0)

        @pl.when(i == 0)
        def _():
            st_ref[...] = rows

        @pl.when(i != 0)
        def _():
            st_ref[...] += rows

    c0 = COL_U // 1024
    vec = pl.BlockSpec((1, D_POOL), lambda i: (0, 0))
    last = seq // HALO - 1
    return pl.pallas_call(
        body, name=name, grid=(ni,),
        in_specs=[pl.BlockSpec((tm, D), lambda i: (i, 0)),
                  pl.BlockSpec((HALO, D), lambda i: (jnp.minimum((i + 1) * (tm // HALO), last), 0)),
                  pl.BlockSpec((D_POOL, D), lambda i: (1, 0)),
                  pl.BlockSpec((tm, 1024), lambda i: (i, c0)),
                  pl.BlockSpec((HALO, 1024), lambda i: (jnp.maximum(i * (tm // HALO) - 1, 0), c0)),
                  pl.BlockSpec((4, POOL_GW, POOL_GW), lambda i: (0, 0, 0)), vec, vec],
        out_specs=[pl.BlockSpec((tm, D_POOL), lambda i: (i, 0)),
                   pl.BlockSpec((4, POOL_GW, POOL_GW), lambda i: (0, 0, 0)),
                   pl.BlockSpec((8, D_POOL), lambda i: (0, 0))],
        out_shape=[jax.ShapeDtypeStruct((seq, D_POOL), BF16),
                   jax.ShapeDtypeStruct((4, POOL_GW, POOL_GW), F32),
                   jax.ShapeDtypeStruct((8, D_POOL), F32)],
        compiler_params=_cp(("arbitrary",)))(dmix, dmix, wout, proj, proj, pw, pb, ps)


def _mix_heads_out(y, proj, gn_w, pw, pb, ps, wout, x1, mod, wn_next, tm, name):
    seq = y.shape[0]

    def body(y_ref, z_ref, gw_ref, u_ref, p_ref, pw_ref, pb_ref, ps_ref, w_ref, x_ref, mod_ref, wn_ref,
             ys_ref, yp_ref, xo_ref, m_ref, h_ref):
        i = pl.program_id(0)
        ys_parts = []
        for g in range(N_GROUPS):
            sl = slice(g * GROUP_W, (g + 1) * GROUP_W)
            zv = z_ref[:, sl]
            yz = y_ref[:, sl] * (zv * _sigmoid(zv))
            r = lax.rsqrt(jnp.mean(yz * yz, axis=-1, keepdims=True) + EPS)
            ys_parts.append((yz * r * gw_ref[:, sl]).astype(BF16))
        ys = jnp.concatenate(ys_parts, axis=1)
        prev = jnp.where(i == 0, 0.0, p_ref[...])
        diffs = _window_means(jnp.concatenate([prev, u_ref[...]], axis=0), i * tm)
        yp_parts = []
        for g in range(4):
            sl = slice(g * POOL_GW, (g + 1) * POOL_GW)
            out = _dot(diffs[g].astype(BF16), pw_ref[g], 1, 0) + pb_ref[:, sl]
            yp_parts.append((out * ps_ref[:, sl]).astype(BF16))
        yp = jnp.concatenate(yp_parts, axis=1)
        ys_ref[...] = ys
        yp_ref[...] = yp
        mix = _dot(ys, w_ref[0:D_SSD, :], 1, 0) + _dot(yp, w_ref[D_SSD:2 * D_SSD, :], 1, 0)
        m_ref[...] = mix.astype(BF16)
        xo = x_ref[...] + mod_ref[5:6, :] * mix
        xo_ref[...] = xo
        h_ref[...] = _modulated(xo, wn_ref[...], mod_ref, 2)

    c0 = COL_U // 1024
    tok = pl.BlockSpec((tm, D), lambda i: (i, 0))
    vec = pl.BlockSpec((1, D), lambda i: (0, 0))
    return pl.pallas_call(
        body, name=name, grid=(seq // tm,),
        in_specs=[tok, tok, vec,
                  pl.BlockSpec((tm, 1024), lambda i: (i, c0)),
                  pl.BlockSpec((HALO, 1024), lambda i: (jnp.maximum(i * (tm // HALO) - 1, 0), c0)),
                  pl.BlockSpec((4, POOL_GW, POOL_GW), lambda i: (0, 0, 0)), vec, vec,
                  pl.BlockSpec((2 * D_SSD, D), lambda i: (0, 0)), tok,
                  pl.BlockSpec((9, D), lambda i: (0, 0)), vec],
        out_specs=[tok, tok, tok, tok, tok],
        out_shape=[jax.ShapeDtypeStruct((seq, D), BF16), jax.ShapeDtypeStruct((seq, D), BF16),
                   jax.ShapeDtypeStruct((seq, D), F32), jax.ShapeDtypeStruct((seq, D), BF16),
                   jax.ShapeDtypeStruct((seq, D), BF16)],
        compiler_params=_cp(("parallel",)))(y, proj, gn_w, proj, proj, pw, pb, ps, wout, x1, mod, wn_next)


def _mix_bwd_dh(dz, dxbc, du, ddt, w_main, w_dt, x1, dx2, mixb, wn, mod, tm, name):
    seq = x1.shape[0]

    def body(dz_ref, dx_ref, du_ref, ddt_ref, w_ref, wdt_ref, x_ref, dxo_ref, m_ref, wn_ref, mod_ref, o_ref, st_ref, df_ref):
        dh = (_dot(dz_ref[...], w_ref[COL_Z:COL_Z + 1024, :], 1, 0)
              + _dot(dx_ref[...], w_ref[COL_XBC:COL_XBC + D_XBC, :], 1, 0)
              + _dot(du_ref[...], w_ref[COL_U:COL_U + 1024, :], 1, 0)
              + _dot(ddt_ref[...], wdt_ref[...], 1, 0))
        dx = _norm_bwd(dh, x_ref[...], dxo_ref[...], m_ref[...].astype(F32), wn_ref[...],
                       mod_ref[4:5, :], 1.0, st_ref, pl.program_id(0) == 0)
        o_ref[...] = dx
        df_ref[...] = (dx * (FFN_RES * mod_ref[2:3, :])).astype(BF16)

    tok = pl.BlockSpec((tm, D), lambda i: (i, 0))
    return pl.pallas_call(
        body, name=name, grid=(seq // tm,),
        in_specs=[tok, pl.BlockSpec((tm, D_XBC), lambda i: (i, 0)), tok,
                  pl.BlockSpec((tm, LANE), lambda i: (i, 0)),
                  pl.BlockSpec((D_MAIN, D), lambda i: (0, 0)),
                  pl.BlockSpec((LANE, D), lambda i: (0, 0)),
                  tok, tok, tok,
                  pl.BlockSpec((1, D), lambda i: (0, 0)),
                  pl.BlockSpec((9, D), lambda i: (0, 0))],
        out_specs=[tok, pl.BlockSpec((8, D), lambda i: (0, 0)), tok],
        out_shape=[jax.ShapeDtypeStruct((seq, D), F32), jax.ShapeDtypeStruct((8, D), F32),
                   jax.ShapeDtypeStruct((seq, D), BF16)],
        compiler_params=_cp(("arbitrary",)))(dz, dxbc, du, ddt, w_main, w_dt, x1, dx2, mixb, wn, mod)


def _local_step(x, tgt, mod, wff1, w_dt, later_weights, pool_w, vecs, tm):
    seq = x.shape[0]
    in_proj_weights, out_proj_weights, ffn2_weights = later_weights
    h1 = _prenorm(x, vecs["ffn1_norm"], mod, 0, tm, "ffn1_prenorm")
    x1, h2, s1 = _ffn_forward(x, h1, wff1, mod, 0, vecs["mix_norm"], 1, tm, "ffn1")
    x1, w_main = in_proj_weights(x1)
    proj = _mm_nt(h2, w_main, tm, 2048, F32, "mix_in_proj")
    proj_dt = _mm_nt(h2, w_dt, tm, LANE, F32, "mix_in_proj_dt")
    xbc, conv_slope = _conv_fwd(proj, vecs["conv_w"], vecs["conv_b"], tm, "mix_conv")
    y, hprev = _ssd_fwd(xbc, proj_dt, vecs["ssd_par"], "mix_ssd")
    y, wout = out_proj_weights(y)
    ys, yp, x2, mixb, h3 = _mix_heads_out(y, proj, vecs["ssd_norm_w"], pool_w, vecs["pool_b"], vecs["pool_scale"],
                                          wout, x1, mod, vecs["ffn2_norm"], tm, "mix_heads_out")
    x2, wff2 = ffn2_weights(x2)
    a3, pg3, pu3 = _ffn_up(h3, wff2["gate"], wff2["up"], tm, FF // 2, "ffn2_up")
    fb3, dx3, df3, st_loss = _ffn_down_loss(a3, *wff2["down"], x2, mod, 8, vecs["final_norm"], tgt, tm, "ffn2_down_loss")
    s3 = (x2, h3, pg3, pu3, a3, fb3)

    dx2, dmix, st3, dw3 = _ffn_backward(dx3, df3, s3, wff2, vecs["ffn2_norm"], mod, 2, (5, 1.0), tm, "ffn2")
    d_wout = (_mm_tn(ys, dmix, 256, seq, "mix_dw_out_ssd"), _mm_tn(yp, dmix, 256, seq, "mix_dw_out_pool"))
    du, d_pool_w, st_pool = _pool_bwd(dmix, wout, proj, pool_w, vecs["pool_b"], vecs["pool_scale"], tm, "mix_pool_bwd")
    dy, dz, st_gn = _gate_norm_bwd(dmix, wout, y, proj, vecs["ssd_norm_w"], tm, "mix_gate_norm_bwd")
    dxbc_act, ddt, st_ssd = _ssd_bwd(dy, xbc, proj_dt, vecs["ssd_par"], hprev, "mix_ssd_bwd")
    dxbc, st_conv = _conv_bwd(dxbc_act, conv_slope, proj, vecs["conv_w"], tm, "mix_conv_bwd")
    dx1, st2, df1 = _mix_bwd_dh(dz, dxbc, du, ddt, w_main, w_dt, x1, dx2, mixb, vecs["mix_norm"], mod, min(tm, 256),
                                "mix_bwd_dh")
    d_win = (_mm_tn_rows([dz, dxbc, du], h2, 256, "mix_dw_in"), _mm_tn(ddt, h2, LANE, seq, "mix_dw_in_dt"))
    dx0, _, st1, dw1 = _ffn_backward(dx1, df1, s1, wff1, vecs["ffn1_norm"], mod, 0, None, tm, "ffn1")
    stats = dict(ffn1=st1, mix=st2, ffn2=st3, loss=st_loss, pool=st_pool, gn=st_gn, ssd=st_ssd, conv=st_conv)
    return dx0, stats, dw1, dw3, d_win, d_wout, d_pool_w


HBM_SPEC = pl.BlockSpec(memory_space=pltpu.HBM)


def _mesh_pos():
    return lax.axis_index("x"), lax.axis_index("y"), lax.axis_index("c")


def _other_chips(x, y):
    return [(1 - x, y), (x, 1 - y), (1 - x, 1 - y)]


def _all_gather(src, regions, name):
    total, cols = src.shape
    assert sum(r for _, r in regions) == total
    body = _all_gather_body(regions, total, False)
    return pl.pallas_call(
        body, name=name,
        out_shape=jax.ShapeDtypeStruct((N_DEV * total, cols), src.dtype),
        in_specs=[HBM_SPEC], out_specs=HBM_SPEC,
        scratch_shapes=[pltpu.SemaphoreType.DMA((7,)), pltpu.SemaphoreType.DMA((7,)), pltpu.SemaphoreType.DMA],
    )(src)


def _all_gather_async(src, regions, name, collective_id):
    total, cols = src.shape
    assert sum(r for _, r in regions) == total
    return pl.kernel(
        _all_gather_body(regions, total, True), name=name,
        out_type=jax.ShapeDtypeStruct((N_DEV * total, cols), src.dtype),
        mesh=plsc.ScalarSubcoreMesh(axis_name="seq", num_cores=1),
        scratch_types=(pltpu.SemaphoreType.DMA((7,)), pltpu.SemaphoreType.DMA((7,)), pltpu.SemaphoreType.DMA),
        compiler_params=pltpu.CompilerParams(collective_id=collective_id))(src)


def _all_gather_body(regions, total, handshake):
    def body(src_ref, out_ref, send_sems, recv_sems, local_sem):
        x, y, c = _mesh_pos()
        me, sibling = (x, y, c), (x, y, 1 - c)
        chips = _other_chips(x, y)
        if handshake:
            barrier = pltpu.get_barrier_semaphore()
            for peer in [sibling] + [(*chip, c) for chip in chips]:
                pl.semaphore_signal(barrier, inc=1, device_id=peer, device_id_type=MESH)
            pl.semaphore_wait(barrier, 4)

        def rows_of(dev, off, rows):
            start = pl.multiple_of(N_DEV * off + (4 * dev[0] + 2 * dev[1] + dev[2]) * rows, 8)
            return out_ref.at[pl.ds(start, rows), :]

        def copies(k, block, to, from_src):
            out = []
            for off, rows in regions:
                dst = rows_of(block, off, rows)
                out.append(pltpu.make_async_remote_copy(
                    src_ref=src_ref.at[pl.ds(off, rows), :] if from_src else dst, dst_ref=dst,
                    send_sem=send_sems.at[k], recv_sem=recv_sems.at[k], device_id=to, device_id_type=MESH))
            return out

        def drain(k):
            whole = out_ref.at[pl.ds(0, total), :]
            return pltpu.make_async_remote_copy(src_ref=whole, dst_ref=whole, send_sem=send_sems.at[k],
                                                recv_sem=recv_sems.at[k], device_id=me, device_id_type=MESH)

        for off, rows in regions:
            pltpu.make_async_copy(src_ref.at[pl.ds(off, rows), :], rows_of(me, off, rows), local_sem).start()
        first = copies(0, me, sibling, True)
        for j, chip in enumerate(chips):
            first += copies(1 + j, me, (*chip, c), True)
        for cp in first:
            cp.start()
        for j, chip in enumerate(chips):
            drain(1 + j).wait_recv()
            for cp in copies(4 + j, (*chip, c), sibling, False):
                cp.start()
        drain(0).wait_recv()
        for j in range(3):
            drain(4 + j).wait_recv()
        for k in range(7):
            drain(k).wait_send()
        pltpu.make_async_copy(src_ref, out_ref.at[pl.ds(0, total), :], local_sem).wait()

    return body


def _rs_pair(grads, total, name, collective_id):
    cols = grads[0][0].shape[1]
    sent = sum(rows for _, _, rows in grads)
    n = len(grads)

    def body(*refs):
        g_refs, recv_ref, send_sem, recv_sem = refs[:n], refs[n], refs[n + 1], refs[n + 2]
        x, y, c = _mesh_pos()
        sibling = (x, y, 1 - c)
        barrier = pltpu.get_barrier_semaphore()
        pl.semaphore_signal(barrier, inc=1, device_id=sibling, device_id_type=MESH)
        pl.semaphore_wait(barrier, 1)
        for q in range(4):
            for g_ref, (_, off, rows) in zip(g_refs, grads):
                theirs = g_ref.at[pl.ds(pl.multiple_of((2 * q + 1 - c) * rows, 8), rows), :]
                pltpu.make_async_remote_copy(
                    src_ref=theirs, dst_ref=recv_ref.at[q, pl.ds(off, rows), :], send_sem=send_sem, recv_sem=recv_sem,
                    device_id=sibling, device_id_type=MESH).start()
        everything = recv_ref.at[:, pl.ds(0, sent), :]
        whole = pltpu.make_async_remote_copy(src_ref=everything, dst_ref=everything, send_sem=send_sem,
                                             recv_sem=recv_sem, device_id=sibling, device_id_type=MESH)
        whole.wait_send()
        whole.wait_recv()

    return pl.kernel(
        body, name=name, out_type=jax.ShapeDtypeStruct((4, total, cols), F32),
        mesh=plsc.ScalarSubcoreMesh(axis_name="seq", num_cores=1),
        scratch_types=(pltpu.SemaphoreType.DMA, pltpu.SemaphoreType.DMA),
        compiler_params=pltpu.CompilerParams(collective_id=collective_id))(*[g for g, _, _ in grads])


def _pair_sum(grads, from_sibling, pos, name):
    cols = grads[0][0].shape[1]
    n = len(grads)

    def body(pos_ref, *refs):
        mine = pl.program_id(0) == pos_ref[1]
        for i in range(n):
            s = refs[i][...] + refs[n + i][...]
            refs[2 * n + 2 * i + 1][...] = s.astype(BF16)

            @pl.when(mine)
            def _():
                refs[2 * n + 2 * i][...] = s

    in_specs = [pl.BlockSpec((None, None, rows, cols), lambda q, pos_ref: (q, pos_ref[0], 0, 0)) for _, _, rows in grads]
    in_specs += [pl.BlockSpec((None, rows, cols), lambda q, pos_ref, blk=off // rows: (q, blk, 0)) for _, off, rows in grads]
    out_specs, out_shape = [], []
    for _, _, rows in grads:
        out_specs += [pl.BlockSpec((rows, cols), lambda q, pos_ref: (0, 0)),
                      pl.BlockSpec((None, rows, cols), lambda q, pos_ref: (q, 0, 0))]
        out_shape += [jax.ShapeDtypeStruct((rows, cols), F32), jax.ShapeDtypeStruct((4, rows, cols), BF16)]
    outs = pl.pallas_call(
        body, name=name,
        grid_spec=pltpu.PrefetchScalarGridSpec(num_scalar_prefetch=1, grid=(4,), in_specs=in_specs, out_specs=out_specs),
        out_shape=out_shape,
        compiler_params=_cp(("arbitrary",)))(pos, *[g.reshape(4, 2, rows, cols) for g, _, rows in grads],
                                             *[from_sibling] * n)
    return [(outs[2 * i], outs[2 * i + 1]) for i in range(n)]


def _rs_chips(parts, total, name, collective_id):
    cols = parts[0][0].shape[2]
    sent = sum(rows for _, _, rows in parts)
    n = len(parts)

    def body(*refs):
        p_refs, out_ref, send_sems, recv_sems = refs[:n], refs[n], refs[n + 1], refs[n + 2]
        x, y, c = _mesh_pos()
        chips = _other_chips(x, y)
        barrier = pltpu.get_barrier_semaphore()
        for chip in chips:
            pl.semaphore_signal(barrier, inc=1, device_id=(*chip, c), device_id_type=MESH)
        pl.semaphore_wait(barrier, 3)
        for j, chip in enumerate(chips):
            q = 2 * chip[0] + chip[1]
            for p_ref, (_, off, rows) in zip(p_refs, parts):
                pltpu.make_async_remote_copy(
                    src_ref=p_ref.at[q], dst_ref=out_ref.at[j, pl.ds(off, rows), :], send_sem=send_sems.at[j],
                    recv_sem=recv_sems.at[j], device_id=(*chip, c), device_id_type=MESH).start()
        for j, chip in enumerate(chips):
            everything = out_ref.at[j, pl.ds(0, sent), :]
            whole = pltpu.make_async_remote_copy(src_ref=everything, dst_ref=everything, send_sem=send_sems.at[j],
                                                 recv_sem=recv_sems.at[j], device_id=(*chip, c), device_id_type=MESH)
            whole.wait_recv()
            whole.wait_send()

    return pl.kernel(
        body, name=name, out_type=jax.ShapeDtypeStruct((3, total, cols), BF16),
        mesh=plsc.ScalarSubcoreMesh(axis_name="seq", num_cores=1),
        scratch_types=(pltpu.SemaphoreType.DMA((3,)), pltpu.SemaphoreType.DMA((3,))),
        compiler_params=pltpu.CompilerParams(collective_id=collective_id))(*[p for p, _, _ in parts])


def _chip_sum(p, from_chips, off, rows, name):
    cols = p.shape[1]

    def body(p_ref, r_ref, o_ref):
        acc = p_ref[...]
        for j in range(3):
            acc = acc + r_ref[j].astype(F32)
        o_ref[...] = acc

    return pl.pallas_call(
        body, name=name, grid=(1,),
        in_specs=[pl.BlockSpec((rows, cols), lambda i: (0, 0)),
                  pl.BlockSpec((3, rows, cols), lambda i: (0, off // rows, 0))],
        out_specs=pl.BlockSpec((rows, cols), lambda i: (0, 0)),
        out_shape=jax.ShapeDtypeStruct((rows, cols), F32),
        compiler_params=_cp(("arbitrary",)))(p, from_chips)


def _chip_sum_adamw(p, from_chips, off, rows, w, m, v, tr, name):
    cols = p.shape[1]
    c1 = 1.0 - ADAM_B1 ** ADAM_STEP
    c2 = 1.0 - ADAM_B2 ** ADAM_STEP

    def body(p_ref, r_ref, w_ref, m_ref, v_ref, g_ref, d_ref, mo_ref, vo_ref):
        gv = p_ref[...]
        for j in range(3):
            gv = gv + r_ref[j].astype(F32)
        g_ref[...] = gv
        mn = ADAM_B1 * m_ref[...] + (1.0 - ADAM_B1) * gv
        vn = ADAM_B2 * v_ref[...] + (1.0 - ADAM_B2) * (gv * gv)
        mo_ref[...] = mn
        vo_ref[...] = vn
        d_ref[...] = -ADAM_LR * ((mn / c1) / (jnp.sqrt(vn / c2) + ADAM_EPS) + ADAM_WD * w_ref[...])

    tile = pl.BlockSpec((tr, cols), lambda i: (i, 0))
    shape = jax.ShapeDtypeStruct((rows, cols), F32)
    return pl.pallas_call(
        body, name=name, grid=(rows // tr,),
        in_specs=[tile, pl.BlockSpec((3, tr, cols), lambda i: (0, off // tr + i, 0)), tile, tile, tile],
        out_specs=[tile] * 4,
        out_shape=[shape] * 4,
        compiler_params=_cp(("parallel",)))(p, from_chips, w, m, v)


def _row_tile(rows, cap):
    t = min(rows, cap)
    while rows % t or t % 8:
        t -= 8
    return t


def _ada_mod(c_all, w, b, name):
    n = w.shape[1]

    def body(c_ref, w_ref, b_ref, o_ref):
        cv = c_ref[...]
        o_ref[...] = _exact_dot(cv * _sigmoid(cv), w_ref[...]) + b_ref[...]

    return pl.pallas_call(body, name=name, out_shape=jax.ShapeDtypeStruct((N_DEV, n), F32),
                          compiler_params=pltpu.CompilerParams(vmem_limit_bytes=VMEM_LIMIT))(c_all, w, b)


def _ada_grad(c_all, dmod, name):
    n = dmod.shape[1]

    def body(c_ref, d_ref, o_ref):
        cv = c_ref[...]
        o_ref[...] = _dot(cv * _sigmoid(cv), d_ref[...], 0, 0, lax.Precision.HIGHEST)

    return pl.pallas_call(body, name=name, out_shape=jax.ShapeDtypeStruct((D, n), F32),
                          compiler_params=pltpu.CompilerParams(vmem_limit_bytes=VMEM_LIMIT))(c_all, dmod)


def _adamw(w, g, m, v, name):
    rows, cols = w.shape
    tr = _row_tile(rows, 256) if rows % 8 == 0 else rows
    c1 = 1.0 - ADAM_B1 ** ADAM_STEP
    c2 = 1.0 - ADAM_B2 ** ADAM_STEP

    def body(w_ref, g_ref, m_ref, v_ref, d_ref, mo_ref, vo_ref):
        gv = g_ref[...]
        mn = ADAM_B1 * m_ref[...] + (1.0 - ADAM_B1) * gv
        vn = ADAM_B2 * v_ref[...] + (1.0 - ADAM_B2) * (gv * gv)
        mo_ref[...] = mn
        vo_ref[...] = vn
        d_ref[...] = -ADAM_LR * ((mn / c1) / (jnp.sqrt(vn / c2) + ADAM_EPS) + ADAM_WD * w_ref[...])

    spec = pl.BlockSpec((tr, cols), lambda i: (i, 0))
    shape = jax.ShapeDtypeStruct((rows, cols), F32)
    return pl.pallas_call(body, name=name, grid=(rows // tr,), in_specs=[spec] * 4, out_specs=[spec] * 3,
                          out_shape=[shape] * 3, compiler_params=_cp(("parallel",)))(w, g, m, v)


def _sum8_loss(v, loss_row, name):
    rows = v.shape[0] // N_DEV

    def body(v_ref, o_ref, l_ref):
        acc = v_ref[0:rows, :]
        for k in range(1, N_DEV):
            acc = acc + v_ref[k * rows:(k + 1) * rows, :]
        o_ref[...] = acc
        part = jnp.sum(acc[loss_row:loss_row + 8, :], axis=0, keepdims=True)
        l_ref[...] = jnp.broadcast_to(jnp.sum(part, axis=1, keepdims=True), (8, LANE))

    return pl.pallas_call(body, name=name,
                          out_shape=[jax.ShapeDtypeStruct((rows, LANE), F32), jax.ShapeDtypeStruct((8, LANE), F32)],
                          compiler_params=pltpu.CompilerParams(vmem_limit_bytes=VMEM_LIMIT))(v)


WEIGHT_NAMES = ("w_ada", "b_ada", "ffn1_norm", "ffn1_w_gate", "ffn1_w_up", "ffn1_w_down", "mix_norm", "w_in",
                "conv_w", "conv_b", "dt_bias", "a_log", "d_skip", "ssd_norm_w", "pool_w", "pool_b", "pool_scale",
                "w_out", "ffn2_norm", "ffn2_w_gate", "ffn2_w_up", "ffn2_w_down", "final_norm")

FF_SHARD = FF // N_DEV
IN_SHARD = D_IN // N_DEV
MAIN_SHARD = D_MAIN // N_DEV
EDGE = 16
OUT_SHARD = 2 * D_SSD // N_DEV
ADA_SHARD = 9 * D // N_DEV
POOL_SHARD_ROWS = 4 * 32 * POOL_GW // D
GPACK = dict(w_in=(0, MAIN_SHARD), w_out=(512, OUT_SHARD), pool_w=(768, POOL_SHARD_ROWS),
             gate1=(0, FF_SHARD), up1=(352, FF_SHARD), down1=(704, FF_SHARD),
             gate2=(0, FF_SHARD), up2=(352, FF_SHARD), down2=(704, FF_SHARD))
GROUP_ROWS = 1056

SMALL_ROWS = dict(dmod=(0, 72), ffn1_norm=(72, 8), mix_norm=(80, 8), ffn2_norm=(88, 8), final_norm=(96, 8),
                  ssd_norm_w=(104, 8), pool_scale=(112, 8), conv_b=(120, 16), conv_w=(136, 64), pool_b=(200, 8),
                  ssd=(208, 3), loss=(216, 8), w_in_dt=(224, 128), w_in_head=(352, 128), w_in_tail=(480, 128))
SMALL_TOTAL = 608

MAIN_FROM_OWN = (16, 14, 12, 10, 8, 6, 20, 18)
OWN_FROM_MAIN = (16, 18, 20, 22, 24, 26, 12, 14)


def _rows128(v, rows):
    flat = v.reshape(-1)
    return jnp.pad(flat, (0, rows * LANE - flat.shape[0])).reshape(rows, LANE)


def _pad_lanes(v):
    return jnp.pad(v.reshape(-1), (0, LANE - v.size))


def kernel(x, c, w_ada, b_ada, ffn1_norm, ffn1_w_gate, ffn1_w_up, ffn1_w_down, mix_norm, w_in, conv_w, conv_b, dt_bias, a_log, d_skip, ssd_norm_w, pool_w, pool_b, pool_scale, w_out, ffn2_norm, ffn2_w_gate, ffn2_w_up, ffn2_w_down, final_norm, loss_target, m_w_ada, m_b_ada, m_ffn1_norm, m_ffn1_w_gate, m_ffn1_w_up, m_ffn1_w_down, m_mix_norm, m_w_in, m_conv_w, m_conv_b, m_dt_bias, m_a_log, m_d_skip, m_ssd_norm_w, m_pool_w, m_pool_b, m_pool_scale, m_w_out, m_ffn2_norm, m_ffn2_w_gate, m_ffn2_w_up, m_ffn2_w_down, m_final_norm, v_w_ada, v_b_ada, v_ffn1_norm, v_ffn1_w_gate, v_ffn1_w_up, v_ffn1_w_down, v_mix_norm, v_w_in, v_conv_w, v_conv_b, v_dt_bias, v_a_log, v_d_skip, v_ssd_norm_w, v_pool_w, v_pool_b, v_pool_scale, v_w_out, v_ffn2_norm, v_ffn2_w_gate, v_ffn2_w_up, v_ffn2_w_down, v_final_norm):
    given = dict(locals())
    w = {n: given[n] for n in WEIGHT_NAMES}
    m = {n: given["m_" + n] for n in WEIGHT_NAMES}
    v = {n: given["v_" + n] for n in WEIGHT_NAMES}
    mx, my, mc = _mesh_pos()
    me = 4 * mx + 2 * my + mc

    w_in_t = w_in[0].T
    small = jnp.concatenate([c.reshape(-1), conv_w.reshape(-1), pool_b.reshape(-1), pool_w.reshape(-1),
                             w_in_t[0:EDGE].reshape(-1), w_in_t[IN_SHARD - EDGE:IN_SHARD].reshape(-1)])
    small_rows = 536
    gs = _all_gather(_rows128(small, small_rows), [(0, small_rows)], "ag_small").reshape(N_DEV, small_rows * LANE)
    c_all = gs[:, 0:D]
    conv_w_full = gs[:, 1024:2048].reshape(N_DEV, 4, 256).transpose(1, 0, 2).reshape(4, D_XBC)
    pool_b_full = gs[:, 2048:2176].reshape(N_DEV, 4, 32).transpose(1, 0, 2).reshape(1, D_POOL)
    pool_w_full = gs[:, 2176:2176 + 32768].reshape(N_DEV, 4, 32, POOL_GW).transpose(1, 0, 2, 3).reshape(4, POOL_GW, POOL_GW).astype(BF16)
    heads = gs[:, 34944:34944 + EDGE * D].reshape(N_DEV, EDGE, D)
    tails = gs[:, 34944 + EDGE * D:34944 + 2 * EDGE * D].reshape(N_DEV, EDGE, D)

    prev_tail = lax.dynamic_index_in_dim(tails, jnp.maximum(me - 1, 0), axis=0, keepdims=False)
    next_head = lax.dynamic_index_in_dim(heads, jnp.minimum(me + 1, N_DEV - 1), axis=0, keepdims=False)
    first = jnp.asarray(MAIN_FROM_OWN, jnp.int32)[me]

    def window(rows, before, size):
        total = EDGE + IN_SHARD + EDGE
        padded = jnp.pad(rows, ((before, total - before - rows.shape[0]), (0, 0)))
        return lax.dynamic_slice(padded, (first, 0), (size, D))

    main_shard = (window(prev_tail, 0, MAIN_SHARD) + window(w_in_t, EDGE, MAIN_SHARD)
                  + window(next_head, EDGE + IN_SHARD, MAIN_SHARD))
    dt_rows = jnp.concatenate([tails[5], heads[6]], axis=0)[4:4 + N_HEADS]
    w_dt = jnp.pad(dt_rows, ((0, LANE - N_HEADS), (0, 0))).astype(BF16)

    b_ada_cols = lax.dynamic_slice(b_ada, (0, me * ADA_SHARD), (1, ADA_SHARD))
    mod_part = _ada_mod(c_all, w_ada[0], b_ada_cols, "ada_mod")
    mod_all = _all_gather(mod_part, [(0, N_DEV)], "ag_mod").reshape(N_DEV, N_DEV, ADA_SHARD)
    mod = lax.dynamic_index_in_dim(mod_all, me, axis=1, keepdims=False).reshape(9, D)

    packs = (jnp.concatenate([ffn1_w_gate[0].T, ffn1_w_up[0].T], axis=0).astype(BF16),
             ffn1_w_down[0].astype(BF16),
             main_shard.astype(BF16),
             w_out[0].astype(BF16),
             jnp.concatenate([ffn2_w_gate[0].T, ffn2_w_up[0].T, ffn2_w_down[0]], axis=0).astype(BF16))
    packs, _ = lax.optimization_barrier((packs, c_all))
    ffn_regions = [(0, FF_SHARD), (FF_SHARD, FF_SHARD), (2 * FF_SHARD, FF_SHARD)]
    full_a = _all_gather_async(packs[0], ffn_regions[0:2], "ag_weights_ffn1_in", 1)
    full_d = _all_gather_async(packs[1], ffn_regions[0:1], "ag_weights_ffn1_out", 2)
    full_in = _all_gather_async(packs[2], [(0, MAIN_SHARD)], "ag_weights_in_proj", 9)
    full_out = _all_gather_async(packs[3], [(0, OUT_SHARD)], "ag_weights_out_proj", 10)
    full_2 = _all_gather_async(packs[4], ffn_regions, "ag_weights_ffn2", 11)
    wff1 = dict(gate=(full_a, 0), up=(full_a, 1), down=(full_d, 0))

    def in_proj_weights(x1):
        w_i, x1 = lax.optimization_barrier((full_in, x1))
        return x1, w_i

    def out_proj_weights(ys):
        w_o, ys = lax.optimization_barrier((full_out, ys))
        return ys, w_o

    def ffn2_weights(x2):
        w_2, x2 = lax.optimization_barrier((full_2, x2))
        return x2, dict(gate=(w_2, 0), up=(w_2, 1), down=(w_2, 2))

    later_weights = (in_proj_weights, out_proj_weights, ffn2_weights)

    vecs = dict(ffn1_norm=ffn1_norm, mix_norm=mix_norm, ffn2_norm=ffn2_norm, final_norm=final_norm.reshape(1, D),
                conv_w=conv_w_full, conv_b=conv_b, ssd_norm_w=ssd_norm_w, pool_b=pool_b_full, pool_scale=pool_scale,
                ssd_par=jnp.concatenate([_pad_lanes(dt_bias)[None], _pad_lanes(a_log)[None], _pad_lanes(d_skip)[None],
                                         jnp.zeros((5, LANE), F32)], axis=0))
    dx0, st, dw1, dw3, d_win, d_wout, d_pool_w = _local_step(
        x[0], loss_target[0], mod, wff1, w_dt, later_weights, pool_w_full, vecs, min(512, x.shape[1]))

    dwin, d_w_dt = d_win
    dwout = jnp.concatenate(d_wout, axis=0)
    dpool = d_pool_w.reshape(4, N_DEV, 32, POOL_GW).transpose(1, 0, 2, 3).reshape(N_DEV * POOL_SHARD_ROWS, D)
    pos = jnp.stack([mc, 2 * mx + my]).astype(jnp.int32)
    by_key = dict(zip(("gate1", "up1", "down1", "gate2", "up2", "down2", "w_out", "w_in", "pool_w"),
                      (*dw1, *dw3, dwout, dwin, dpool)))
    reduced = {}
    for tag, keys, cid in (("ffn2", ("gate2", "up2", "down2"), 3), ("mix", ("w_in", "w_out", "pool_w"), 5),
                           ("ffn1", ("gate1", "up1", "down1"), 7)):
        grads = [(by_key[k], *GPACK[k]) for k in keys]
        from_sibling = _rs_pair(grads, GROUP_ROWS, f"rs_pair_{tag}", cid)
        pairs = dict(zip(keys, _pair_sum(grads, from_sibling, pos, f"rs_pair_sum_{tag}")))
        from_chips = _rs_chips([(pairs[k][1], *GPACK[k]) for k in keys], GROUP_ROWS, f"rs_chips_{tag}", cid + 1)
        for k in keys:
            reduced[k] = (pairs[k][0], from_chips)

    delta, new_m, new_v, shard_grad = {}, {}, {}, {}
    fused = dict(gate1=("ffn1_w_gate", True, 176), up1=("ffn1_w_up", True, 176), down1=("ffn1_w_down", False, 176),
                 gate2=("ffn2_w_gate", True, 176), up2=("ffn2_w_up", True, 176), down2=("ffn2_w_down", False, 176),
                 w_out=("w_out", False, 128), pool_w=("pool_w", False, POOL_SHARD_ROWS))
    for k, (n, is_transposed, tr) in fused.items():
        shp = w[n].shape
        rows = GPACK[k][1]
        view = (lambda t: t[0].T) if is_transposed else (lambda t: t.reshape(rows, D))
        back = (lambda t: t.T[None]) if is_transposed else (lambda t: t.reshape(shp))
        g_, d_, m_, v_ = _chip_sum_adamw(reduced[k][0], reduced[k][1], *GPACK[k], view(w[n]), view(m[n]), view(v[n]),
                                         tr, f"adamw_{n}")
        shard_grad[n], delta[n], new_m[n], new_v[n] = back(g_), back(d_), back(m_), back(v_)
    g_main = _chip_sum(reduced["w_in"][0], reduced["w_in"][1], *GPACK["w_in"], "rs_chip_sum_w_in")

    dmod = jnp.concatenate([st["ffn1"][0:3], st["mix"][0:3], st["ffn2"][0:3]], axis=0)
    sg = jnp.concatenate([
        dmod.reshape(-1), st["ffn1"][3], st["mix"][3], st["ffn2"][3], st["loss"][0], st["gn"][0], st["pool"][1],
        st["conv"][4], st["conv"][0:4].reshape(-1), st["pool"][0], st["ssd"][0:3].reshape(-1),
        jnp.zeros((5 * LANE,), F32), st["loss"][1],
        d_w_dt[0:N_HEADS].reshape(-1), g_main[0:EDGE].reshape(-1), g_main[MAIN_SHARD - EDGE:MAIN_SHARD].reshape(-1)])
    sg_all = _all_gather(sg.reshape(SMALL_TOTAL, LANE), [(0, SMALL_TOTAL)], "ag_small_grads")
    tot, loss_b = _sum8_loss(sg_all, SMALL_ROWS["loss"][0], "small_sum")
    loss = loss_b[0, 0]
    per_dev = sg_all.reshape(N_DEV, SMALL_TOTAL * LANE)
    dmod_all = per_dev[:, 0:9 * D]
    g_w_ada = _ada_grad(c_all, lax.dynamic_slice(dmod_all, (0, me * ADA_SHARD), (N_DEV, ADA_SHARD)), "ada_grad")

    def edge_rows(k):
        off, n = SMALL_ROWS[k]
        return per_dev[:, off * LANE:(off + n) * LANE].reshape(N_DEV, EDGE, D)

    g_dt = tot[SMALL_ROWS["w_in_dt"][0]:SMALL_ROWS["w_in_dt"][0] + SMALL_ROWS["w_in_dt"][1]].reshape(N_HEADS, D)
    before = jnp.where(me == 6, g_dt, edge_rows("w_in_tail")[6])
    after = jnp.where(me == 5, g_dt, lax.dynamic_index_in_dim(edge_rows("w_in_head"), jnp.minimum(me + 1, N_DEV - 1),
                                                               axis=0, keepdims=False))
    first_own = jnp.asarray(OWN_FROM_MAIN, jnp.int32)[me]

    def own_window(rows, lead):
        total = EDGE + MAIN_SHARD + EDGE
        padded = jnp.pad(rows, ((lead, total - lead - rows.shape[0]), (0, 0)))
        return lax.dynamic_slice(padded, (first_own, 0), (IN_SHARD, D))

    g_win_t = own_window(before, 0) + own_window(g_main, EDGE) + own_window(after, EDGE + MAIN_SHARD)

    def tot_rows(k):
        off, n = SMALL_ROWS[k]
        return tot[off:off + n].reshape(-1)

    g_conv_w = lax.dynamic_slice(tot_rows("conv_w").reshape(4, D_XBC), (0, me * 256), (4, 256))
    g_pool_b = lax.dynamic_slice(tot_rows("pool_b").reshape(4, POOL_GW), (0, me * 32), (4, 32))
    g_ssd = tot_rows("ssd").reshape(3, LANE)
    grad = {
        "w_ada": g_w_ada[None], "b_ada": tot_rows("dmod").reshape(1, 9 * D),
        "ffn1_norm": tot_rows("ffn1_norm")[None], "mix_norm": tot_rows("mix_norm")[None],
        "ffn2_norm": tot_rows("ffn2_norm")[None], "final_norm": tot_rows("final_norm"),
        "ssd_norm_w": tot_rows("ssd_norm_w")[None], "pool_scale": tot_rows("pool_scale")[None],
        "conv_b": tot_rows("conv_b")[None], "conv_w": g_conv_w[None], "pool_b": g_pool_b[None],
        "dt_bias": g_ssd[0:1, 0:N_HEADS], "a_log": g_ssd[1:2, 0:N_HEADS], "d_skip": g_ssd[2:3, 0:N_HEADS],
        "w_in": g_win_t.T[None], **shard_grad,
    }

    d_, m_, v_ = _adamw(w_ada[0], g_w_ada, m_w_ada[0], v_w_ada[0], "adamw_w_ada")
    delta["w_ada"], new_m["w_ada"], new_v["w_ada"] = d_[None], m_[None], v_[None]
    d_, m_, v_ = _adamw(w_in[0].T, g_win_t, m_w_in[0].T, v_w_in[0].T, "adamw_w_in")
    delta["w_in"], new_m["w_in"], new_v["w_in"] = d_.T[None], m_.T[None], v_.T[None]
    big = ("w_ada", "w_in") + tuple(n for n, _, _ in fused.values())
    small_names = [n for n in WEIGHT_NAMES if n not in big]
    sizes = [LANE if w[n].size < LANE else w[n].size for n in small_names]
    small_rows_adam = -(-sum(sizes) // (8 * LANE)) * 8

    def pack_small(t):
        return _rows128(jnp.concatenate([_pad_lanes(t[n]) if t[n].size < LANE else t[n].reshape(-1) for n in small_names]),
                        small_rows_adam)

    d_s, m_s, v_s = _adamw(pack_small(w), pack_small(grad), pack_small(m), pack_small(v), "adamw_small")
    off = 0
    for n, size in zip(small_names, sizes):
        for res, packed in ((delta, d_s), (new_m, m_s), (new_v, v_s)):
            res[n] = packed.reshape(-1)[off:off + w[n].size].reshape(w[n].shape)
        off += size

    return (loss, dx0[None], *[grad[n] for n in WEIGHT_NAMES], *[delta[n] for n in WEIGHT_NAMES],
            *[new_m[n] for n in WEIGHT_NAMES], *[new_v[n] for n in WEIGHT_NAMES])
```

```python
import jax
import jax.numpy as jnp
from jax import lax
from jax.experimental import pallas as pl
from jax.experimental.pallas import tpu as pltpu
from jax.experimental.pallas import tpu_sc as plsc

F32 = jnp.float32
BF16 = jnp.bfloat16
MESH = pl.DeviceIdType.MESH

N_DEV = 8
D = 1024
FF = 2816
D_SSD = 1024
N_HEADS = 16
HEAD_DIM = 64
N_GROUPS = 4
N_STATE = 128
CHUNK = 128
GROUP_W = D_SSD // N_GROUPS
D_XBC = D_SSD + 2 * N_GROUPS * N_STATE
D_POOL = 1024
POOL_WINDOWS = (2, 4, 8, 16)
POOL_GW = 256
D_IN = 4112
D_MAIN = 4096
COL_Z, COL_XBC, COL_U = 0, 1024, 3072
EPS = 1e-6
FFN_RES = 0.5
LANE = 128
HALO = 16

ADAM_LR, ADAM_B1, ADAM_B2, ADAM_EPS, ADAM_WD, ADAM_STEP = 0.001, 0.9, 0.999, 1e-08, 0.01, 10

VMEM_LIMIT = 56 << 20


def _cp(sem):
    return pltpu.CompilerParams(dimension_semantics=sem, vmem_limit_bytes=VMEM_LIMIT)


def _dot(a, b, ca, cb, prec=None):
    return lax.dot_general(a, b, (((ca,), (cb,)), ((), ())), precision=prec,
                           preferred_element_type=F32)


def _exact_dot(a, b):
    return _dot(a, b, 1, 0, lax.Precision.HIGHEST)


def _sigmoid(v):
    return 1.0 / (1.0 + jnp.exp(-v))


def _silu_grad(v, sg):
    return sg * (1.0 + v * (1.0 - sg))


def _mm_nt(a, bt, tm, tn, out_dtype, name):
    m, k = a.shape
    n = bt.shape[0]

    def body(a_ref, b_ref, o_ref):
        o_ref[...] = _dot(a_ref[...], b_ref[...], 1, 1).astype(out_dtype)

    return pl.pallas_call(
        body, name=name, grid=(n // tn, m // tm),
        in_specs=[pl.BlockSpec((tm, k), lambda j, i: (i, 0)),
                  pl.BlockSpec((tn, k), lambda j, i: (j, 0))],
        out_specs=pl.BlockSpec((tm, tn), lambda j, i: (i, j)),
        out_shape=jax.ShapeDtypeStruct((m, n), out_dtype),
        compiler_params=_cp(("parallel", "parallel")))(a, bt)


def _mm_tn(a, b, tm, tk, name):
    kk, m = a.shape
    n = b.shape[1]
    nk = kk // tk
    if nk == 1:
        def whole(a_ref, b_ref, o_ref):
            o_ref[...] = _dot(a_ref[...], b_ref[...], 0, 0)

        return pl.pallas_call(
            whole, name=name, grid=(m // tm,),
            in_specs=[pl.BlockSpec((kk, tm), lambda i: (0, i)),
                      pl.BlockSpec((kk, n), lambda i: (0, 0))],
            out_specs=pl.BlockSpec((tm, n), lambda i: (i, 0)),
            out_shape=jax.ShapeDtypeStruct((m, n), F32),
            compiler_params=_cp(("parallel",)))(a, b)

    def body(a_ref, b_ref, o_ref, acc):
        k = pl.program_id(1)

        @pl.when(k == 0)
        def _():
            acc[...] = jnp.zeros_like(acc)

        acc[...] += _dot(a_ref[...], b_ref[...], 0, 0)

        @pl.when(k == nk - 1)
        def _():
            o_ref[...] = acc[...]

    return pl.pallas_call(
        body, name=name, grid=(m // tm, nk),
        in_specs=[pl.BlockSpec((tk, tm), lambda i, k: (k, i)),
                  pl.BlockSpec((tk, n), lambda i, k: (k, 0))],
        out_specs=pl.BlockSpec((tm, n), lambda i, k: (i, 0)),
        out_shape=jax.ShapeDtypeStruct((m, n), F32),
        scratch_shapes=[pltpu.VMEM((tm, n), F32)],
        compiler_params=_cp(("parallel", "arbitrary")))(a, b)


def _mm_tn_rows(parts, b, tm, name):
    kk, n = b.shape
    blocks = [a.shape[1] // tm for a in parts]
    starts = [sum(blocks[:p]) for p in range(len(parts))]

    def body(*refs):
        a_refs, b_ref, o_ref = refs[:len(parts)], refs[len(parts)], refs[len(parts) + 1]
        i = pl.program_id(0)
        for p, a_ref in enumerate(a_refs):
            @pl.when(jnp.logical_and(i >= starts[p], i < starts[p] + blocks[p]))
            def _():
                o_ref[...] = _dot(a_ref[...], b_ref[...], 0, 0)

    def part_spec(p):
        return pl.BlockSpec((kk, tm), lambda i: (0, jnp.clip(i - starts[p], 0, blocks[p] - 1)))

    return pl.pallas_call(
        body, name=name, grid=(sum(blocks),),
        in_specs=[part_spec(p) for p in range(len(parts))] + [pl.BlockSpec((kk, n), lambda i: (0, 0))],
        out_specs=pl.BlockSpec((tm, n), lambda i: (i, 0)),
        out_shape=jax.ShapeDtypeStruct((sum(blocks) * tm, n), F32),
        compiler_params=_cp(("parallel",)))(*parts, b)


def _modulated(xv, wn, mod_ref, k):
    r = lax.rsqrt(jnp.mean(xv * xv, axis=-1, keepdims=True) + EPS)
    hn = xv * r * wn
    return (hn * (1.0 + mod_ref[3 * k + 1:3 * k + 2, :]) + mod_ref[3 * k:3 * k + 1, :]).astype(BF16)


def _prenorm(x, wn, mod, k, tm, name):
    seq = x.shape[0]

    def body(x_ref, wn_ref, mod_ref, h_ref):
        h_ref[...] = _modulated(x_ref[...], wn_ref[...], mod_ref, k)

    return pl.pallas_call(
        body, name=name, grid=(seq // tm,),
        in_specs=[pl.BlockSpec((tm, D), lambda i: (i, 0)),
                  pl.BlockSpec((1, D), lambda i: (0, 0)),
                  pl.BlockSpec((9, D), lambda i: (0, 0))],
        out_specs=pl.BlockSpec((tm, D), lambda i: (i, 0)),
        out_shape=jax.ShapeDtypeStruct((seq, D), BF16),
        compiler_params=_cp(("parallel",)))(x, wn, mod)


def _norm_bwd(dh, xv, dxo, branch, wn, sc, res, stats_ref, first):
    r = lax.rsqrt(jnp.mean(xv * xv, axis=-1, keepdims=True) + EPS)
    xn = xv * r
    dhn = dh * (1.0 + sc)
    dxn = dhn * wn
    dx = dxo + r * (dxn - xn * jnp.mean(dxn * xn, axis=-1, keepdims=True))
    rows = jnp.concatenate([
        jnp.sum(dh, axis=0, keepdims=True),
        jnp.sum(dh * (xn * wn), axis=0, keepdims=True),
        jnp.sum(branch * dxo, axis=0, keepdims=True) * res,
        jnp.sum(dhn * xn, axis=0, keepdims=True),
        jnp.zeros((4, D), F32)], axis=0)

    @pl.when(first)
    def _():
        stats_ref[...] = rows

    @pl.when(jnp.logical_not(first))
    def _():
        stats_ref[...] += rows

    return dx


def _ffn_up(h, wg, wu, tm, tn, name):
    seq = h.shape[0]
    nj = FF // tn

    def body(h_ref, wg_ref, wu_ref, a_ref, pg_ref, pu_ref):
        hv = h_ref[...]
        g = _dot(hv, wg_ref[...], 1, 1)
        u = _dot(hv, wu_ref[...], 1, 1)
        sg = _sigmoid(g)
        s = g * sg
        a_ref[...] = (s * u).astype(BF16)
        pg_ref[...] = (u * _silu_grad(g, sg)).astype(BF16)
        pu_ref[...] = s.astype(BF16)

    act = pl.BlockSpec((tm, tn), lambda j, i: (i, j))
    return pl.pallas_call(
        body, name=name, grid=(nj, seq // tm),
        in_specs=[pl.BlockSpec((tm, D), lambda j, i: (i, 0)),
                  pl.BlockSpec((tn, D), lambda j, i: (wg[1] * nj + j, 0)),
                  pl.BlockSpec((tn, D), lambda j, i: (wu[1] * nj + j, 0))],
        out_specs=[act, act, act],
        out_shape=[jax.ShapeDtypeStruct((seq, FF), BF16)] * 3,
        compiler_params=_cp(("parallel", "parallel")))(h, wg[0], wu[0])


def _ffn_down(a, w, blk, x, mod, grow, wn_next, k_next, w_dt, tm, name):
    seq = a.shape[0]

    def body(a_ref, w_ref, x_ref, mod_ref, wn_ref, wdt_ref, xo_ref, f_ref, h_ref, dt_ref):
        f = _dot(a_ref[...], w_ref[...], 1, 0)
        f_ref[...] = f.astype(BF16)
        xo = x_ref[...] + (FFN_RES * mod_ref[grow:grow + 1, :]) * f
        xo_ref[...] = xo
        h = _modulated(xo, wn_ref[...], mod_ref, k_next)
        h_ref[...] = h
        dt_ref[...] = _dot(h, wdt_ref[...], 1, 1)

    tok = pl.BlockSpec((tm, D), lambda i: (i, 0))
    return pl.pallas_call(
        body, name=name, grid=(seq // tm,),
        in_specs=[pl.BlockSpec((tm, FF), lambda i: (i, 0)),
                  pl.BlockSpec((FF, D), lambda i: (blk, 0)),
                  tok,
                  pl.BlockSpec((9, D), lambda i: (0, 0)),
                  pl.BlockSpec((1, D), lambda i: (0, 0)),
                  pl.BlockSpec((LANE, D), lambda i: (0, 0))],
        out_specs=[tok, tok, tok, pl.BlockSpec((tm, LANE), lambda i: (i, 0))],
        out_shape=[jax.ShapeDtypeStruct((seq, D), F32), jax.ShapeDtypeStruct((seq, D), BF16),
                   jax.ShapeDtypeStruct((seq, D), BF16), jax.ShapeDtypeStruct((seq, LANE), F32)],
        compiler_params=_cp(("parallel",)))(a, w, x, mod, wn_next, w_dt)


def _ffn_down_loss(a, w, blk, x, mod, grow, wf, tgt, tm, name):
    seq = a.shape[0]

    def body(a_ref, w_ref, x_ref, mod_ref, wf_ref, t_ref, f_ref, dx_ref, df_ref, st_ref):
        f = _dot(a_ref[...], w_ref[...], 1, 0)
        f_ref[...] = f.astype(BF16)
        xv = x_ref[...] + (FFN_RES * mod_ref[grow:grow + 1, :]) * f
        wv = wf_ref[...]
        r = lax.rsqrt(jnp.mean(xv * xv, axis=-1, keepdims=True) + EPS)
        xn = xv * r
        e = xn * wv - t_ref[...]
        dy = e * (1.0 / D)
        dxn = dy * wv
        dx = r * (dxn - xn * jnp.mean(dxn * xn, axis=-1, keepdims=True))
        dx_ref[...] = dx
        df_ref[...] = (dx * (FFN_RES * mod_ref[grow:grow + 1, :])).astype(BF16)
        rows = jnp.concatenate([
            jnp.sum(dy * xn, axis=0, keepdims=True),
            jnp.sum(e * e, axis=0, keepdims=True) * (0.5 / D),
            jnp.zeros((6, D), F32)], axis=0)

        @pl.when(pl.program_id(0) == 0)
        def _():
            st_ref[...] = rows

        @pl.when(pl.program_id(0) != 0)
        def _():
            st_ref[...] += rows

    tok = pl.BlockSpec((tm, D), lambda i: (i, 0))
    return pl.pallas_call(
        body, name=name, grid=(seq // tm,),
        in_specs=[pl.BlockSpec((tm, FF), lambda i: (i, 0)),
                  pl.BlockSpec((FF, D), lambda i: (blk, 0)),
                  tok,
                  pl.BlockSpec((9, D), lambda i: (0, 0)),
                  pl.BlockSpec((1, D), lambda i: (0, 0)),
                  tok],
        out_specs=[tok, tok, tok, pl.BlockSpec((8, D), lambda i: (0, 0))],
        out_shape=[jax.ShapeDtypeStruct((seq, D), BF16), jax.ShapeDtypeStruct((seq, D), F32),
                   jax.ShapeDtypeStruct((seq, D), BF16), jax.ShapeDtypeStruct((8, D), F32)],
        compiler_params=_cp(("arbitrary",)))(a, w, x, mod, wf, tgt)


def _ffn_bwd_da(df, w, blk, pg, pu, tm, tn, name):
    seq = df.shape[0]
    nj = FF // tn

    def body(df_ref, w_ref, pg_ref, pu_ref, dg_ref, du_ref):
        da = _dot(df_ref[...], w_ref[...], 1, 1)
        dg_ref[...] = (da * pg_ref[...].astype(F32)).astype(BF16)
        du_ref[...] = (da * pu_ref[...].astype(F32)).astype(BF16)

    act = pl.BlockSpec((tm, tn), lambda j, i: (i, j))
    return pl.pallas_call(
        body, name=name, grid=(nj, seq // tm),
        in_specs=[pl.BlockSpec((tm, D), lambda j, i: (i, 0)),
                  pl.BlockSpec((tn, D), lambda j, i: (blk * nj + j, 0)),
                  act, act],
        out_specs=[act, act],
        out_shape=[jax.ShapeDtypeStruct((seq, FF), BF16)] * 2,
        compiler_params=_cp(("parallel", "parallel")))(df, w, pg, pu)


def _ffn_bwd_dh(dg, du, wg, wu, x, dxo, fb, wn, mod, k, nxt, tm, name):
    seq = x.shape[0]

    def body(dg_ref, du_ref, wg_ref, wu_ref, x_ref, dxo_ref, f_ref, wn_ref, mod_ref, dx_ref, st_ref, *rest):
        dh = _dot(dg_ref[...], wg_ref[...], 1, 0) + _dot(du_ref[...], wu_ref[...], 1, 0)
        dx = _norm_bwd(dh, x_ref[...], dxo_ref[...], f_ref[...].astype(F32), wn_ref[...],
                       mod_ref[3 * k + 1:3 * k + 2, :], FFN_RES, st_ref, pl.program_id(0) == 0)
        dx_ref[...] = dx
        if nxt is not None:
            rest[0][...] = (dx * (nxt[1] * mod_ref[nxt[0]:nxt[0] + 1, :])).astype(BF16)

    tok = pl.BlockSpec((tm, D), lambda i: (i, 0))
    act = pl.BlockSpec((tm, FF), lambda i: (i, 0))
    return pl.pallas_call(
        body, name=name, grid=(seq // tm,),
        in_specs=[act, act,
                  pl.BlockSpec((FF, D), lambda i: (wg[1], 0)),
                  pl.BlockSpec((FF, D), lambda i: (wu[1], 0)),
                  tok, tok, tok,
                  pl.BlockSpec((1, D), lambda i: (0, 0)),
                  pl.BlockSpec((9, D), lambda i: (0, 0))],
        out_specs=[tok, pl.BlockSpec((8, D), lambda i: (0, 0))] + ([tok] if nxt is not None else []),
        out_shape=[jax.ShapeDtypeStruct((seq, D), F32), jax.ShapeDtypeStruct((8, D), F32)]
        + ([jax.ShapeDtypeStruct((seq, D), BF16)] if nxt is not None else []),
        compiler_params=_cp(("arbitrary",)))(dg, du, wg[0], wu[0], x, dxo, fb, wn, mod)


def _ffn_forward(x, h, w, mod, k, wn_next, k_next, w_dt, tm, tag):
    a, pg, pu = _ffn_up(h, w["gate"], w["up"], tm, FF // 2, f"{tag}_up")
    xo, fb, h_next, dt_next = _ffn_down(a, *w["down"], x, mod, 3 * k + 2, wn_next, k_next, w_dt, tm, f"{tag}_down")
    return xo, h_next, dt_next, (x, h, pg, pu, a, fb)


def _ffn_backward(dxo, df, saved, w, wn, mod, k, nxt, tm, tag):
    x, h, pg, pu, a, fb = saved
    dg, du = _ffn_bwd_da(df, *w["down"], pg, pu, tm, FF // 2, f"{tag}_bwd_da")
    seq = x.shape[0]
    d_gate_t = _mm_tn(dg, h, 256, seq, f"{tag}_dw_gate")
    d_up_t = _mm_tn(du, h, 256, seq, f"{tag}_dw_up")
    d_down = _mm_tn(a, df, 256, seq, f"{tag}_dw_down")
    dws, dg, du = lax.optimization_barrier(((d_gate_t, d_up_t, d_down), dg, du))
    outs = _ffn_bwd_dh(dg, du, w["gate"], w["up"], x, dxo, fb, wn, mod, k, nxt, min(tm, 256), f"{tag}_bwd_dh")
    return outs[0], (outs[2] if nxt is not None else None), outs[1], dws


def _prev_rows(tm, col):
    return pl.BlockSpec((HALO, 1024), lambda i, j: (jnp.maximum(i * (tm // HALO) - 1, 0), col + j))


def _conv_pre(ext, cw, cb, rows):
    pre = cb + cw[3:4, :] * ext
    for s in (1, 2, 3):
        pre = pre + cw[3 - s:4 - s, :] * pltpu.roll(ext, s, 0)
    return pre[HALO:HALO + rows]


def _conv_fwd(proj, cw, cb, tm, name):
    seq = proj.shape[0]

    def body(x_ref, p_ref, cw_ref, cb_ref, o_ref, g_ref):
        prev = jnp.where(pl.program_id(0) == 0, 0.0, p_ref[...])
        ext = jnp.concatenate([prev, x_ref[...]], axis=0)
        pre = _conv_pre(ext, cw_ref[...], cb_ref[...], tm)
        sg = _sigmoid(pre)
        o_ref[...] = pre * sg
        g_ref[...] = _silu_grad(pre, sg).astype(BF16)

    c0 = COL_XBC // 1024
    out = pl.BlockSpec((tm, 1024), lambda i, j: (i, j))
    return pl.pallas_call(
        body, name=name, grid=(seq // tm, 2),
        in_specs=[pl.BlockSpec((tm, 1024), lambda i, j: (i, c0 + j)),
                  _prev_rows(tm, c0),
                  pl.BlockSpec((4, 1024), lambda i, j: (0, j)),
                  pl.BlockSpec((1, 1024), lambda i, j: (0, j))],
        out_specs=[out, out],
        out_shape=[jax.ShapeDtypeStruct((seq, D_XBC), F32), jax.ShapeDtypeStruct((seq, D_XBC), BF16)],
        compiler_params=_cp(("parallel", "parallel")))(proj, proj, cw, cb)


def _conv_bwd(dact, slope, proj, cw, tm, name):
    seq = proj.shape[0]
    ni = seq // tm

    def body(d_ref, dn_ref, s_ref, sn_ref, x_ref, p_ref, cw_ref, o_ref, st_ref):
        i = pl.program_id(1)
        cwv = cw_ref[...]
        prev = jnp.where(i == 0, 0.0, p_ref[...])
        ext = jnp.concatenate([prev, x_ref[...]], axis=0)
        dnext = jnp.where(i == ni - 1, 0.0, dn_ref[...] * sn_ref[...].astype(F32))
        dpre = jnp.concatenate([d_ref[...] * s_ref[...].astype(F32), dnext], axis=0)
        n = tm + HALO
        dx = cwv[3:4, :] * dpre
        for s in (1, 2, 3):
            dx = dx + cwv[3 - s:4 - s, :] * pltpu.roll(dpre, n - s, 0)
        o_ref[...] = dx[:tm].astype(BF16)
        dcur = dpre[:tm]
        rows = [jnp.sum(dcur * pltpu.roll(ext, 3 - k, 0)[HALO:HALO + tm], axis=0, keepdims=True) for k in range(3)]
        rows.append(jnp.sum(dcur * ext[HALO:HALO + tm], axis=0, keepdims=True))
        rows.append(jnp.sum(dcur, axis=0, keepdims=True))
        rows.append(jnp.zeros((3, 1024), F32))
        rows = jnp.concatenate(rows, axis=0)

        @pl.when(i == 0)
        def _():
            st_ref[...] = rows

        @pl.when(i != 0)
        def _():
            st_ref[...] += rows

    c0 = COL_XBC // 1024
    cur = pl.BlockSpec((tm, 1024), lambda j, i: (i, j))
    nxt = pl.BlockSpec((HALO, 1024), lambda j, i: (jnp.minimum((i + 1) * (tm // HALO), seq // HALO - 1), j))
    return pl.pallas_call(
        body, name=name, grid=(2, ni),
        in_specs=[cur, nxt, cur, nxt,
                  pl.BlockSpec((tm, 1024), lambda j, i: (i, c0 + j)),
                  pl.BlockSpec((HALO, 1024), lambda j, i: (jnp.maximum(i * (tm // HALO) - 1, 0), c0 + j)),
                  pl.BlockSpec((4, 1024), lambda j, i: (0, j))],
        out_specs=[cur, pl.BlockSpec((8, 1024), lambda j, i: (0, j))],
        out_shape=[jax.ShapeDtypeStruct((seq, D_XBC), BF16), jax.ShapeDtypeStruct((8, D_XBC), F32)],
        compiler_params=_cp(("parallel", "arbitrary")))(dact, dact, slope, slope, proj, proj, cw)


def _bf16_parts(x, n):
    parts, rest = [], x
    for _ in range(n):
        p = rest.astype(BF16)
        parts.append(p)
        rest = rest - p.astype(F32)
    return parts


def _pick(x, sel, n):
    m = x.shape[0]
    prod = _dot(jnp.concatenate(_bf16_parts(x, n), axis=0), sel, 1, 0)
    acc = prod[0:m]
    for i in range(1, n):
        acc = acc + prod[i * m:(i + 1) * m]
    return acc


def _running(mask, x, n):
    k = x.shape[1]
    prod = _dot(mask, jnp.concatenate(_bf16_parts(x, n), axis=1), 1, 0)
    acc = prod[:, 0:k]
    for i in range(1, n):
        acc = acc + prod[:, i * k:(i + 1) * k]
    return acc


def _head_expand():
    r = lax.broadcasted_iota(jnp.int32, (LANE, D_SSD), 0)
    c = lax.broadcasted_iota(jnp.int32, (LANE, D_SSD), 1)
    return (c // HEAD_DIM == r).astype(BF16)


def _head_reduce():
    r = lax.broadcasted_iota(jnp.int32, (D_SSD, LANE), 0)
    c = lax.broadcasted_iota(jnp.int32, (D_SSD, LANE), 1)
    return (r // HEAD_DIM == c).astype(BF16)


def _ssd_common(dtr, par):
    q = CHUNK
    v = dtr + par[0:1, :]
    dt = jnp.maximum(v, 0.0) + jnp.log(1.0 + jnp.exp(-jnp.abs(v)))
    a = -jnp.exp(par[1:2, :])
    adt = dt * a
    li = lax.broadcasted_iota(jnp.int32, (q, q), 0)
    si = lax.broadcasted_iota(jnp.int32, (q, q), 1)
    causal = li >= si
    acs = _running(causal.astype(BF16), adt, 3)
    expand = _head_expand()
    both_l = _pick(jnp.concatenate([dt, acs], axis=0), expand, 3)
    dt_l, acs_l = both_l[0:q], both_l[q:2 * q]
    dskip_l = _pick(jnp.broadcast_to(par[2:3, :], (16, LANE)), expand, 3)[0:1, :]
    last_l = acs_l[q - 1:q, :]
    return dict(v=v, dt=dt, a=a, acs=acs, acs_t=acs.T, causal=causal, dt_l=dt_l, acs_l=acs_l,
                ea_l=jnp.exp(acs_l), ds_l=jnp.exp(last_l - acs_l), cd_l=jnp.exp(last_l), dskip_l=dskip_l)


def _decay(cm, h):
    seg = cm["acs"][:, h:h + 1] - cm["acs_t"][h:h + 1, :]
    return jnp.exp(jnp.where(cm["causal"], seg, -jnp.inf))


def _lane_mask(r):
    lane = lax.broadcasted_iota(jnp.int32, (1, GROUP_W), 1)
    return lane // HEAD_DIM == r


def _ssd_fwd(xbc, proj, par, name):
    seq = xbc.shape[0]
    nc = seq // CHUNK
    q = CHUNK

    def body(x_ref, dt_ref, par_ref, y_ref, hp_ref, state):
        @pl.when(pl.program_id(0) == 0)
        def _():
            state[...] = jnp.zeros_like(state)

        cm = _ssd_common(dt_ref[...], par_ref[...])
        for g in range(N_GROUPS):
            lo = g * GROUP_W
            xs = x_ref[:, lo:lo + GROUP_W]
            bm = x_ref[:, D_SSD + g * N_STATE:D_SSD + (g + 1) * N_STATE].astype(BF16)
            cmat = x_ref[:, D_SSD + N_GROUPS * N_STATE + g * N_STATE:D_SSD + N_GROUPS * N_STATE + (g + 1) * N_STATE].astype(BF16)
            xdt = xs * cm["dt_l"][:, lo:lo + GROUP_W]
            xdt_b = xdt.astype(BF16)
            cb = _dot(cmat, bm, 1, 1)
            scores = jnp.concatenate([(cb * _decay(cm, 4 * g + r)).astype(BF16) for r in range(4)], axis=0)
            yd_heads = _dot(scores, xdt_b, 1, 0)
            yd = yd_heads[0:q]
            for r in range(1, 4):
                yd = jnp.where(_lane_mask(r), yd_heads[r * q:(r + 1) * q], yd)
            hg = state[g]
            hp_ref[0, g] = hg
            yo = _dot(cmat, hg.astype(BF16), 1, 0) * cm["ea_l"][:, lo:lo + GROUP_W]
            y_ref[:, lo:lo + GROUP_W] = yd + yo + cm["dskip_l"][:, lo:lo + GROUP_W] * xs
            xds = (xdt * cm["ds_l"][:, lo:lo + GROUP_W]).astype(BF16)
            state[g] = hg * cm["cd_l"][:, lo:lo + GROUP_W] + _dot(bm, xds, 0, 0)

    return pl.pallas_call(
        body, name=name, grid=(nc,),
        in_specs=[pl.BlockSpec((q, D_XBC), lambda c: (c, 0)),
                  pl.BlockSpec((q, LANE), lambda c: (c, 0)),
                  pl.BlockSpec((8, LANE), lambda c: (0, 0))],
        out_specs=[pl.BlockSpec((q, D_SSD), lambda c: (c, 0)),
                   pl.BlockSpec((1, N_GROUPS, N_STATE, GROUP_W), lambda c: (c, 0, 0, 0))],
        out_shape=[jax.ShapeDtypeStruct((seq, D_SSD), F32),
                   jax.ShapeDtypeStruct((nc, N_GROUPS, N_STATE, GROUP_W), F32)],
        scratch_shapes=[pltpu.VMEM((N_GROUPS, N_STATE, GROUP_W), F32)],
        compiler_params=_cp(("arbitrary",)))(xbc, proj, par)


def _ssd_bwd(dy, xbc, proj, par, hprev, name):
    seq = xbc.shape[0]
    nc = seq // CHUNK
    q = CHUNK

    def body(dy_ref, x_ref, dt_ref, par_ref, hp_ref, dx_ref, ddt_ref, st_ref, dstate):
        step = pl.program_id(0)

        @pl.when(step == 0)
        def _():
            dstate[...] = jnp.zeros_like(dstate)

        par = par_ref[...]
        cm = _ssd_common(dt_ref[...], par)
        reduce = _head_reduce()
        lane128 = lax.broadcasted_iota(jnp.int32, (1, LANE), 1)
        row128 = lax.broadcasted_iota(jnp.int32, (LANE, 1), 0)
        d_acs = jnp.zeros((q, LANE), F32)
        d_acs_t = jnp.zeros((LANE, q), F32)
        last_terms = []
        acs_terms = []
        dxdt_all = []
        for g in range(N_GROUPS):
            lo = g * GROUP_W
            sl = slice(lo, lo + GROUP_W)
            xs = x_ref[:, sl]
            bm32 = x_ref[:, D_SSD + g * N_STATE:D_SSD + (g + 1) * N_STATE]
            cm32 = x_ref[:, D_SSD + N_GROUPS * N_STATE + g * N_STATE:D_SSD + N_GROUPS * N_STATE + (g + 1) * N_STATE]
            bm = bm32.astype(BF16)
            cmat = cm32.astype(BF16)
            dyg = dy_ref[:, sl]
            dyg_b = dyg.astype(BF16)
            xdt = xs * cm["dt_l"][:, sl]
            xdt_b = xdt.astype(BF16)
            hg = hp_ref[0, g]
            hg_b = hg.astype(BF16)
            dhg = dstate[g]
            dhg_b = dhg.astype(BF16)
            ea = cm["ea_l"][:, sl]
            ds = cm["ds_l"][:, sl]
            cd = cm["cd_l"][:, sl]
            yoff = _dot(cmat, hg_b, 1, 0) * ea
            dw = (dyg * ea).astype(BF16)
            d_c = _dot(dw, hg_b, 1, 1)
            d_hprev = _dot(cmat, dw, 0, 0) + dhg * cd
            t_acs = dyg * yoff
            d_last_g = jnp.sum(dhg * hg, axis=0, keepdims=True) * cd
            xds_b = (xdt * ds).astype(BF16)
            dxds = _dot(bm, dhg_b, 1, 0)
            d_b = _dot(xds_b, dhg_b, 1, 1)
            dxdt = dxds * ds
            t_ds = dxds * xdt * ds
            t_acs = t_acs - t_ds
            d_last_g = d_last_g + jnp.sum(t_ds, axis=0, keepdims=True)
            cb = _dot(cmat, bm, 1, 1)
            d_cb = jnp.zeros((q, q), F32)
            decays = [_decay(cm, 4 * g + r) for r in range(4)]
            score_heads = [cb * dec for dec in decays]
            d_s_heads = _dot(jnp.concatenate([jnp.where(_lane_mask(r), dyg, 0.0).astype(BF16) for r in range(4)], axis=0),
                             xdt_b, 1, 1)
            dxdt_heads = _dot(jnp.concatenate([s.astype(BF16) for s in score_heads], axis=1), dyg_b, 0, 0)
            for r in range(4):
                h = 4 * g + r
                dec, s_h = decays[r], score_heads[r]
                d_s = d_s_heads[r * q:(r + 1) * q]
                dxdt = dxdt + jnp.where(_lane_mask(r), dxdt_heads[r * q:(r + 1) * q], 0.0)
                d_cb = d_cb + d_s * dec
                d_m = d_s * s_h
                d_acs = d_acs + jnp.where(lane128 == h, jnp.sum(d_m, axis=1, keepdims=True), 0.0)
                d_acs_t = d_acs_t + jnp.where(row128 == h, jnp.sum(d_m, axis=0, keepdims=True), 0.0)
            d_cb_b = d_cb.astype(BF16)
            d_c = d_c + _dot(d_cb_b, bm, 1, 0)
            d_b = d_b + _dot(d_cb_b, cmat, 0, 0)
            dstate[g] = d_hprev
            dx_ref[:, sl] = dxdt * cm["dt_l"][:, sl] + cm["dskip_l"][:, sl] * dyg
            dx_ref[:, D_SSD + g * N_STATE:D_SSD + (g + 1) * N_STATE] = d_b
            dx_ref[:, D_SSD + N_GROUPS * N_STATE + g * N_STATE:D_SSD + N_GROUPS * N_STATE + (g + 1) * N_STATE] = d_c
            acs_terms.append(t_acs)
            dxdt_all.append(dxdt * xs)
            last_terms.append(d_last_g)
        t_acs_l = jnp.concatenate(acs_terms, axis=1)
        d_dt_l = jnp.concatenate(dxdt_all, axis=1)
        d_last_l = jnp.concatenate(last_terms, axis=1)
        per_head = _pick(jnp.concatenate([t_acs_l, d_dt_l], axis=0), reduce, 2)
        skip_l = jnp.sum(dy_ref[...] * x_ref[:, 0:D_SSD], axis=0, keepdims=True)
        singles = _pick(jnp.concatenate([d_last_l, skip_l, jnp.zeros((14, D_SSD), F32)], axis=0), reduce, 3)
        d_acs = d_acs + per_head[0:q] - d_acs_t.T
        last_row = lax.broadcasted_iota(jnp.int32, (q, 1), 0) == q - 1
        d_acs = d_acs + jnp.where(last_row, singles[0:1, :], 0.0)
        li = lax.broadcasted_iota(jnp.int32, (q, q), 0)
        si = lax.broadcasted_iota(jnp.int32, (q, q), 1)
        d_adt = _running((si >= li).astype(BF16), d_acs, 3)
        d_dt = per_head[q:2 * q] + d_adt * cm["a"]
        d_dtr = d_dt * _sigmoid(cm["v"])
        ddt_ref[...] = d_dtr.astype(BF16)
        d_skip = singles[1:2, :]
        rows = jnp.concatenate([
            jnp.sum(d_dtr, axis=0, keepdims=True),
            jnp.sum(d_adt * cm["dt"], axis=0, keepdims=True) * cm["a"],
            d_skip,
            jnp.zeros((5, LANE), F32)], axis=0)

        @pl.when(step == 0)
        def _():
            st_ref[...] = rows

        @pl.when(step != 0)
        def _():
            st_ref[...] += rows

    rev = lambda c: nc - 1 - c
    return pl.pallas_call(
        body, name=name, grid=(nc,),
        in_specs=[pl.BlockSpec((q, D_SSD), lambda c: (rev(c), 0)),
                  pl.BlockSpec((q, D_XBC), lambda c: (rev(c), 0)),
                  pl.BlockSpec((q, LANE), lambda c: (rev(c), 0)),
                  pl.BlockSpec((8, LANE), lambda c: (0, 0)),
                  pl.BlockSpec((1, N_GROUPS, N_STATE, GROUP_W), lambda c: (rev(c), 0, 0, 0))],
        out_specs=[pl.BlockSpec((q, D_XBC), lambda c: (rev(c), 0)),
                   pl.BlockSpec((q, LANE), lambda c: (rev(c), 0)),
                   pl.BlockSpec((8, LANE), lambda c: (0, 0))],
        out_shape=[jax.ShapeDtypeStruct((seq, D_XBC), F32),
                   jax.ShapeDtypeStruct((seq, LANE), BF16),
                   jax.ShapeDtypeStruct((8, LANE), F32)],
        scratch_shapes=[pltpu.VMEM((N_GROUPS, N_STATE, GROUP_W), F32)],
        compiler_params=_cp(("arbitrary",)))(dy, xbc, proj, par, hprev)


def _gate_norm_bwd(dmix, wout, y, proj, wn, tm, name):
    seq = y.shape[0]

    def body(dm_ref, wo_ref, y_ref, z_ref, w_ref, dy_ref, dz_ref, st_ref):
        d_ys = _dot(dm_ref[...], wo_ref[...], 1, 1)
        rows = []
        for g in range(N_GROUPS):
            sl = slice(g * GROUP_W, (g + 1) * GROUP_W)
            zv = z_ref[:, sl]
            yv = y_ref[:, sl]
            sg = _sigmoid(zv)
            sz = zv * sg
            yz = yv * sz
            r = lax.rsqrt(jnp.mean(yz * yz, axis=-1, keepdims=True) + EPS)
            yn = yz * r
            dv = d_ys[:, sl]
            dyn = dv * w_ref[:, sl]
            dyz = r * (dyn - yn * jnp.mean(dyn * yn, axis=-1, keepdims=True))
            dy_ref[:, sl] = dyz * sz
            dz_ref[:, sl] = (dyz * yv * _silu_grad(zv, sg)).astype(BF16)
            rows.append(jnp.sum(dv * yn, axis=0, keepdims=True))
        rows = jnp.concatenate([jnp.concatenate(rows, axis=1), jnp.zeros((7, D_SSD), F32)], axis=0)

        @pl.when(pl.program_id(0) == 0)
        def _():
            st_ref[...] = rows

        @pl.when(pl.program_id(0) != 0)
        def _():
            st_ref[...] += rows

    tok = pl.BlockSpec((tm, D_SSD), lambda i: (i, 0))
    return pl.pallas_call(
        body, name=name, grid=(seq // tm,),
        in_specs=[tok, pl.BlockSpec((D_SSD, D), lambda i: (0, 0)), tok, tok, pl.BlockSpec((1, D_SSD), lambda i: (0, 0))],
        out_specs=[tok, tok, pl.BlockSpec((8, D_SSD), lambda i: (0, 0))],
        out_shape=[jax.ShapeDtypeStruct((seq, D_SSD), F32), jax.ShapeDtypeStruct((seq, D_SSD), BF16),
                   jax.ShapeDtypeStruct((8, D_SSD), F32)],
        compiler_params=_cp(("arbitrary",)))(dmix, wout, y, proj, wn)


def _pool_counts(t0, rows, w):
    pos = (t0 + 1 + lax.broadcasted_iota(jnp.int32, (rows, 1), 0)).astype(F32)
    return jnp.minimum(pos, float(w))


def _window_means(ext, t0):
    n = ext.shape[0]
    outs = []
    run = ext
    width = 1
    sums = {}
    while width < 16:
        run = run + pltpu.roll(run, width, 0)
        width *= 2
        sums[width] = run
    for g, w in enumerate(POOL_WINDOWS):
        sl = slice(g * POOL_GW, (g + 1) * POOL_GW)
        cnt = _pool_counts(t0, n - HALO, w)
        outs.append(sums[w][HALO:, sl] / cnt - ext[HALO:, sl])
    return outs


def _pool_bwd(dmix, wout, proj, pw, pb, ps, tm, name):
    seq = proj.shape[0]
    ni = seq // tm

    def body(dm_ref, dmn_ref, wo_ref, u_ref, p_ref, pw_ref, pb_ref, ps_ref, du_ref, dw_ref, st_ref):
        i = pl.program_id(0)
        prev = jnp.where(i == 0, 0.0, p_ref[...])
        ext = jnp.concatenate([prev, u_ref[...]], axis=0)
        diffs = _window_means(ext, i * tm)
        n = tm + HALO
        dext = _dot(jnp.concatenate([dm_ref[...], dmn_ref[...]], axis=0), wo_ref[...], 1, 1)
        past_end = jnp.logical_and(i == ni - 1, lax.broadcasted_iota(jnp.int32, (n, 1), 0) >= tm)
        dext = jnp.where(past_end, 0.0, dext)
        b_rows, s_rows = [], []
        for g, w in enumerate(POOL_WINDOWS):
            sl = slice(g * POOL_GW, (g + 1) * POOL_GW)
            wg = pw_ref[g]
            dout = dext[:, sl] * ps_ref[:, sl]
            dcur = dout[:tm]
            pre = _dot(diffs[g].astype(BF16), wg, 1, 0) + pb_ref[:, sl]
            s_rows.append(jnp.sum(dext[:tm, sl] * pre, axis=0, keepdims=True))
            b_rows.append(jnp.sum(dcur, axis=0, keepdims=True))
            dwg = _dot(diffs[g].astype(BF16), dcur.astype(BF16), 0, 0)

            @pl.when(i == 0)
            def _():
                dw_ref[g] = dwg

            @pl.when(i != 0)
            def _():
                dw_ref[g] += dwg

            ddiff = _dot(dout.astype(BF16), wg, 1, 1)
            scaled = ddiff / _pool_counts(i * tm, n, w)
            run = scaled
            width = 1
            while width < w:
                run = run + pltpu.roll(run, n - width, 0)
                width *= 2
            du_ref[:, sl] = (run[:tm] - ddiff[:tm]).astype(BF16)
        rows = jnp.concatenate([jnp.concatenate(b_rows, axis=1), jnp.concatenate(s_rows, axis=1),
                                jnp.zeros((6, D_POOL), F32)], axis=0)

        @pl.when(i == 0)
        def _():
            st_ref[...] = rows

        @pl.when(i != 0)
        def _():
            st_ref[...] += rows

    c0 = COL_U // 1024
    vec = pl.BlockSpec((1, D_POOL), lambda i: (0, 0))
    last = seq // HALO - 1
    return pl.pallas_call(
        body, name=name, grid=(ni,),
        in_specs=[pl.BlockSpec((tm, D), lambda i: (i, 0)),
                  pl.BlockSpec((HALO, D), lambda i: (jnp.minimum((i + 1) * (tm // HALO), last), 0)),
                  pl.BlockSpec((D_POOL, D), lambda i: (1, 0)),
                  pl.BlockSpec((tm, 1024), lambda i: (i, c0)),
                  pl.BlockSpec((HALO, 1024), lambda i: (jnp.maximum(i * (tm // HALO) - 1, 0), c0)),
                  pl.BlockSpec((4, POOL_GW, POOL_GW), lambda i: (0, 0, 0)), vec, vec],
        out_specs=[pl.BlockSpec((tm, D_POOL), lambda i: (i, 0)),
                   pl.BlockSpec((4, POOL_GW, POOL_GW), lambda i: (0, 0, 0)),
                   pl.BlockSpec((8, D_POOL), lambda i: (0, 0))],
        out_shape=[jax.ShapeDtypeStruct((seq, D_POOL), BF16),
                   jax.ShapeDtypeStruct((4, POOL_GW, POOL_GW), F32),
                   jax.ShapeDtypeStruct((8, D_POOL), F32)],
        compiler_params=_cp(("arbitrary",)))(dmix, dmix, wout, proj, proj, pw, pb, ps)


def _mix_heads_out(y, proj, gn_w, pw, pb, ps, wout, x1, mod, wn_next, tm, name):
    seq = y.shape[0]

    def body(y_ref, z_ref, gw_ref, u_ref, p_ref, pw_ref, pb_ref, ps_ref, w_ref, x_ref, mod_ref, wn_ref,
             ys_ref, yp_ref, xo_ref, m_ref, h_ref):
        i = pl.program_id(0)
        ys_parts = []
        for g in range(N_GROUPS):
            sl = slice(g * GROUP_W, (g + 1) * GROUP_W)
            zv = z_ref[:, sl]
            yz = y_ref[:, sl] * (zv * _sigmoid(zv))
            r = lax.rsqrt(jnp.mean(yz * yz, axis=-1, keepdims=True) + EPS)
            ys_parts.append((yz * r * gw_ref[:, sl]).astype(BF16))
        ys = jnp.concatenate(ys_parts, axis=1)
        prev = jnp.where(i == 0, 0.0, p_ref[...])
        diffs = _window_means(jnp.concatenate([prev, u_ref[...]], axis=0), i * tm)
        yp_parts = []
        for g in range(4):
            sl = slice(g * POOL_GW, (g + 1) * POOL_GW)
            out = _dot(diffs[g].astype(BF16), pw_ref[g], 1, 0) + pb_ref[:, sl]
            yp_parts.append((out * ps_ref[:, sl]).astype(BF16))
        yp = jnp.concatenate(yp_parts, axis=1)
        ys_ref[...] = ys
        yp_ref[...] = yp
        mix = _dot(ys, w_ref[0:D_SSD, :], 1, 0) + _dot(yp, w_ref[D_SSD:2 * D_SSD, :], 1, 0)
        m_ref[...] = mix.astype(BF16)
        xo = x_ref[...] + mod_ref[5:6, :] * mix
        xo_ref[...] = xo
        h_ref[...] = _modulated(xo, wn_ref[...], mod_ref, 2)

    c0 = COL_U // 1024
    tok = pl.BlockSpec((tm, D), lambda i: (i, 0))
    vec = pl.BlockSpec((1, D), lambda i: (0, 0))
    return pl.pallas_call(
        body, name=name, grid=(seq // tm,),
        in_specs=[tok, tok, vec,
                  pl.BlockSpec((tm, 1024), lambda i: (i, c0)),
                  pl.BlockSpec((HALO, 1024), lambda i: (jnp.maximum(i * (tm // HALO) - 1, 0), c0)),
                  pl.BlockSpec((4, POOL_GW, POOL_GW), lambda i: (0, 0, 0)), vec, vec,
                  pl.BlockSpec((2 * D_SSD, D), lambda i: (0, 0)), tok,
                  pl.BlockSpec((9, D), lambda i: (0, 0)), vec],
        out_specs=[tok, tok, tok, tok, tok],
        out_shape=[jax.ShapeDtypeStruct((seq, D), BF16), jax.ShapeDtypeStruct((seq, D), BF16),
                   jax.ShapeDtypeStruct((seq, D), F32), jax.ShapeDtypeStruct((seq, D), BF16),
                   jax.ShapeDtypeStruct((seq, D), BF16)],
        compiler_params=_cp(("parallel",)))(y, proj, gn_w, proj, proj, pw, pb, ps, wout, x1, mod, wn_next)


def _mix_bwd_dh(dz, dxbc, du, ddt, w_main, w_dt, x1, dx2, mixb, wn, mod, tm, name):
    seq = x1.shape[0]

    def body(dz_ref, dx_ref, du_ref, ddt_ref, w_ref, wdt_ref, x_ref, dxo_ref, m_ref, wn_ref, mod_ref, o_ref, st_ref, df_ref):
        dh = (_dot(dz_ref[...], w_ref[COL_Z:COL_Z + 1024, :], 1, 0)
              + _dot(dx_ref[...], w_ref[COL_XBC:COL_XBC + D_XBC, :], 1, 0)
              + _dot(du_ref[...], w_ref[COL_U:COL_U + 1024, :], 1, 0)
              + _dot(ddt_ref[...], wdt_ref[...], 1, 0))
        dx = _norm_bwd(dh, x_ref[...], dxo_ref[...], m_ref[...].astype(F32), wn_ref[...],
                       mod_ref[4:5, :], 1.0, st_ref, pl.program_id(0) == 0)
        o_ref[...] = dx
        df_ref[...] = (dx * (FFN_RES * mod_ref[2:3, :])).astype(BF16)

    tok = pl.BlockSpec((tm, D), lambda i: (i, 0))
    return pl.pallas_call(
        body, name=name, grid=(seq // tm,),
        in_specs=[tok, pl.BlockSpec((tm, D_XBC), lambda i: (i, 0)), tok,
                  pl.BlockSpec((tm, LANE), lambda i: (i, 0)),
                  pl.BlockSpec((D_MAIN, D), lambda i: (0, 0)),
                  pl.BlockSpec((LANE, D), lambda i: (0, 0)),
                  tok, tok, tok,
                  pl.BlockSpec((1, D), lambda i: (0, 0)),
                  pl.BlockSpec((9, D), lambda i: (0, 0))],
        out_specs=[tok, pl.BlockSpec((8, D), lambda i: (0, 0)), tok],
        out_shape=[jax.ShapeDtypeStruct((seq, D), F32), jax.ShapeDtypeStruct((8, D), F32),
                   jax.ShapeDtypeStruct((seq, D), BF16)],
        compiler_params=_cp(("arbitrary",)))(dz, dxbc, du, ddt, w_main, w_dt, x1, dx2, mixb, wn, mod)


def _local_step(x, tgt, mod, wff1, w_dt, later_weights, pool_w, vecs, tm):
    seq = x.shape[0]
    in_proj_weights, out_proj_weights, ffn2_weights = later_weights
    h1 = _prenorm(x, vecs["ffn1_norm"], mod, 0, tm, "ffn1_prenorm")
    x1, h2, proj_dt, s1 = _ffn_forward(x, h1, wff1, mod, 0, vecs["mix_norm"], 1, w_dt, tm, "ffn1")
    x1, w_main = in_proj_weights(x1)
    proj = _mm_nt(h2, w_main, tm, 2048, F32, "mix_in_proj")
    xbc, conv_slope = _conv_fwd(proj, vecs["conv_w"], vecs["conv_b"], tm, "mix_conv")
    y, hprev = _ssd_fwd(xbc, proj_dt, vecs["ssd_par"], "mix_ssd")
    y, wout = out_proj_weights(y)
    ys, yp, x2, mixb, h3 = _mix_heads_out(y, proj, vecs["ssd_norm_w"], pool_w, vecs["pool_b"], vecs["pool_scale"],
                                          wout, x1, mod, vecs["ffn2_norm"], tm, "mix_heads_out")
    x2, wff2 = ffn2_weights(x2)
    a3, pg3, pu3 = _ffn_up(h3, wff2["gate"], wff2["up"], tm, FF // 2, "ffn2_up")
    fb3, dx3, df3, st_loss = _ffn_down_loss(a3, *wff2["down"], x2, mod, 8, vecs["final_norm"], tgt, tm, "ffn2_down_loss")
    s3 = (x2, h3, pg3, pu3, a3, fb3)

    dx2, dmix, st3, dw3 = _ffn_backward(dx3, df3, s3, wff2, vecs["ffn2_norm"], mod, 2, (5, 1.0), tm, "ffn2")
    d_wout = (_mm_tn(ys, dmix, 256, seq, "mix_dw_out_ssd"), _mm_tn(yp, dmix, 256, seq, "mix_dw_out_pool"))
    du, d_pool_w, st_pool = _pool_bwd(dmix, wout, proj, pool_w, vecs["pool_b"], vecs["pool_scale"], tm, "mix_pool_bwd")
    dy, dz, st_gn = _gate_norm_bwd(dmix, wout, y, proj, vecs["ssd_norm_w"], tm, "mix_gate_norm_bwd")
    dxbc_act, ddt, st_ssd = _ssd_bwd(dy, xbc, proj_dt, vecs["ssd_par"], hprev, "mix_ssd_bwd")
    dxbc, st_conv = _conv_bwd(dxbc_act, conv_slope, proj, vecs["conv_w"], tm, "mix_conv_bwd")
    dx1, st2, df1 = _mix_bwd_dh(dz, dxbc, du, ddt, w_main, w_dt, x1, dx2, mixb, vecs["mix_norm"], mod, min(tm, 256),
                                "mix_bwd_dh")
    d_win = (_mm_tn_rows([dz, dxbc, du], h2, 256, "mix_dw_in"), _mm_tn(ddt, h2, LANE, seq, "mix_dw_in_dt"))
    dx0, _, st1, dw1 = _ffn_backward(dx1, df1, s1, wff1, vecs["ffn1_norm"], mod, 0, None, tm, "ffn1")
    stats = dict(ffn1=st1, mix=st2, ffn2=st3, loss=st_loss, pool=st_pool, gn=st_gn, ssd=st_ssd, conv=st_conv)
    return dx0, stats, dw1, dw3, d_win, d_wout, d_pool_w


HBM_SPEC = pl.BlockSpec(memory_space=pltpu.HBM)


def _mesh_pos():
    return lax.axis_index("x"), lax.axis_index("y"), lax.axis_index("c")


def _other_chips(x, y):
    return [(1 - x, y), (x, 1 - y), (1 - x, 1 - y)]


def _all_gather(src, regions, name):
    total, cols = src.shape
    assert sum(r for _, r in regions) == total
    body = _all_gather_body(regions, total, False)
    return pl.pallas_call(
        body, name=name,
        out_shape=jax.ShapeDtypeStruct((N_DEV * total, cols), src.dtype),
        in_specs=[HBM_SPEC], out_specs=HBM_SPEC,
        scratch_shapes=[pltpu.SemaphoreType.DMA((7,)), pltpu.SemaphoreType.DMA((7,)), pltpu.SemaphoreType.DMA],
    )(src)


def _all_gather_async(src, regions, name, collective_id):
    total, cols = src.shape
    assert sum(r for _, r in regions) == total
    return pl.kernel(
        _all_gather_body(regions, total, True), name=name,
        out_type=jax.ShapeDtypeStruct((N_DEV * total, cols), src.dtype),
        mesh=plsc.ScalarSubcoreMesh(axis_name="seq", num_cores=1),
        scratch_types=(pltpu.SemaphoreType.DMA((7,)), pltpu.SemaphoreType.DMA((7,)), pltpu.SemaphoreType.DMA),
        compiler_params=pltpu.CompilerParams(collective_id=collective_id))(src)


def _all_gather_body(regions, total, handshake):
    def body(src_ref, out_ref, send_sems, recv_sems, local_sem):
        x, y, c = _mesh_pos()
        me, sibling = (x, y, c), (x, y, 1 - c)
        chips = _other_chips(x, y)
        if handshake:
            barrier = pltpu.get_barrier_semaphore()
            for peer in [sibling] + [(*chip, c) for chip in chips]:
                pl.semaphore_signal(barrier, inc=1, device_id=peer, device_id_type=MESH)
            pl.semaphore_wait(barrier, 4)

        def rows_of(dev, off, rows):
            start = pl.multiple_of(N_DEV * off + (4 * dev[0] + 2 * dev[1] + dev[2]) * rows, 8)
            return out_ref.at[pl.ds(start, rows), :]

        def copies(k, block, to, from_src):
            out = []
            for off, rows in regions:
                dst = rows_of(block, off, rows)
                out.append(pltpu.make_async_remote_copy(
                    src_ref=src_ref.at[pl.ds(off, rows), :] if from_src else dst, dst_ref=dst,
                    send_sem=send_sems.at[k], recv_sem=recv_sems.at[k], device_id=to, device_id_type=MESH))
            return out

        def drain(k):
            whole = out_ref.at[pl.ds(0, total), :]
            return pltpu.make_async_remote_copy(src_ref=whole, dst_ref=whole, send_sem=send_sems.at[k],
                                                recv_sem=recv_sems.at[k], device_id=me, device_id_type=MESH)

        for off, rows in regions:
            pltpu.make_async_copy(src_ref.at[pl.ds(off, rows), :], rows_of(me, off, rows), local_sem).start()
        first = copies(0, me, sibling, True)
        for j, chip in enumerate(chips):
            first += copies(1 + j, me, (*chip, c), True)
        for cp in first:
            cp.start()
        for j, chip in enumerate(chips):
            drain(1 + j).wait_recv()
            for cp in copies(4 + j, (*chip, c), sibling, False):
                cp.start()
        drain(0).wait_recv()
        for j in range(3):
            drain(4 + j).wait_recv()
        for k in range(7):
            drain(k).wait_send()
        pltpu.make_async_copy(src_ref, out_ref.at[pl.ds(0, total), :], local_sem).wait()

    return body


def _rs_pair(grads, total, name, collective_id):
    cols = grads[0][0].shape[1]
    sent = sum(rows for _, _, rows in grads)
    n = len(grads)

    def body(*refs):
        g_refs, recv_ref, send_sem, recv_sem = refs[:n], refs[n], refs[n + 1], refs[n + 2]
        x, y, c = _mesh_pos()
        sibling = (x, y, 1 - c)
        barrier = pltpu.get_barrier_semaphore()
        pl.semaphore_signal(barrier, inc=1, device_id=sibling, device_id_type=MESH)
        pl.semaphore_wait(barrier, 1)
        for q in range(4):
            for g_ref, (_, off, rows) in zip(g_refs, grads):
                theirs = g_ref.at[pl.ds(pl.multiple_of((2 * q + 1 - c) * rows, 8), rows), :]
                pltpu.make_async_remote_copy(
                    src_ref=theirs, dst_ref=recv_ref.at[q, pl.ds(off, rows), :], send_sem=send_sem, recv_sem=recv_sem,
                    device_id=sibling, device_id_type=MESH).start()
        everything = recv_ref.at[:, pl.ds(0, sent), :]
        whole = pltpu.make_async_remote_copy(src_ref=everything, dst_ref=everything, send_sem=send_sem,
                                             recv_sem=recv_sem, device_id=sibling, device_id_type=MESH)
        whole.wait_send()
        whole.wait_recv()

    return pl.kernel(
        body, name=name, out_type=jax.ShapeDtypeStruct((4, total, cols), F32),
        mesh=plsc.ScalarSubcoreMesh(axis_name="seq", num_cores=1),
        scratch_types=(pltpu.SemaphoreType.DMA, pltpu.SemaphoreType.DMA),
        compiler_params=pltpu.CompilerParams(collective_id=collective_id))(*[g for g, _, _ in grads])


def _pair_sum(grads, from_sibling, pos, name):
    cols = grads[0][0].shape[1]
    n = len(grads)

    def body(pos_ref, *refs):
        mine = pl.program_id(0) == pos_ref[1]
        for i in range(n):
            s = refs[i][...] + refs[n + i][...]
            refs[2 * n + 2 * i + 1][...] = s.astype(BF16)

            @pl.when(mine)
            def _():
                refs[2 * n + 2 * i][...] = s

    in_specs = [pl.BlockSpec((None, None, rows, cols), lambda q, pos_ref: (q, pos_ref[0], 0, 0)) for _, _, rows in grads]
    in_specs += [pl.BlockSpec((None, rows, cols), lambda q, pos_ref, blk=off // rows: (q, blk, 0)) for _, off, rows in grads]
    out_specs, out_shape = [], []
    for _, _, rows in grads:
        out_specs += [pl.BlockSpec((rows, cols), lambda q, pos_ref: (0, 0)),
                      pl.BlockSpec((None, rows, cols), lambda q, pos_ref: (q, 0, 0))]
        out_shape += [jax.ShapeDtypeStruct((rows, cols), F32), jax.ShapeDtypeStruct((4, rows, cols), BF16)]
    outs = pl.pallas_call(
        body, name=name,
        grid_spec=pltpu.PrefetchScalarGridSpec(num_scalar_prefetch=1, grid=(4,), in_specs=in_specs, out_specs=out_specs),
        out_shape=out_shape,
        compiler_params=_cp(("arbitrary",)))(pos, *[g.reshape(4, 2, rows, cols) for g, _, rows in grads],
                                             *[from_sibling] * n)
    return [(outs[2 * i], outs[2 * i + 1]) for i in range(n)]


def _rs_chips(parts, total, name, collective_id):
    cols = parts[0][0].shape[2]
    sent = sum(rows for _, _, rows in parts)
    n = len(parts)

    def body(*refs):
        p_refs, out_ref, send_sems, recv_sems = refs[:n], refs[n], refs[n + 1], refs[n + 2]
        x, y, c = _mesh_pos()
        chips = _other_chips(x, y)
        barrier = pltpu.get_barrier_semaphore()
        for chip in chips:
            pl.semaphore_signal(barrier, inc=1, device_id=(*chip, c), device_id_type=MESH)
        pl.semaphore_wait(barrier, 3)
        for j, chip in enumerate(chips):
            q = 2 * chip[0] + chip[1]
            for p_ref, (_, off, rows) in zip(p_refs, parts):
                pltpu.make_async_remote_copy(
                    src_ref=p_ref.at[q], dst_ref=out_ref.at[j, pl.ds(off, rows), :], send_sem=send_sems.at[j],
                    recv_sem=recv_sems.at[j], device_id=(*chip, c), device_id_type=MESH).start()
        for j, chip in enumerate(chips):
            everything = out_ref.at[j, pl.ds(0, sent), :]
            whole = pltpu.make_async_remote_copy(src_ref=everything, dst_ref=everything, send_sem=send_sems.at[j],
                                                 recv_sem=recv_sems.at[j], device_id=(*chip, c), device_id_type=MESH)
            whole.wait_recv()
            whole.wait_send()

    return pl.kernel(
        body, name=name, out_type=jax.ShapeDtypeStruct((3, total, cols), BF16),
        mesh=plsc.ScalarSubcoreMesh(axis_name="seq", num_cores=1),
        scratch_types=(pltpu.SemaphoreType.DMA((3,)), pltpu.SemaphoreType.DMA((3,))),
        compiler_params=pltpu.CompilerParams(collective_id=collective_id))(*[p for p, _, _ in parts])


def _chip_sum(p, from_chips, off, rows, name):
    cols = p.shape[1]

    def body(p_ref, r_ref, o_ref):
        acc = p_ref[...]
        for j in range(3):
            acc = acc + r_ref[j].astype(F32)
        o_ref[...] = acc

    return pl.pallas_call(
        body, name=name, grid=(1,),
        in_specs=[pl.BlockSpec((rows, cols), lambda i: (0, 0)),
                  pl.BlockSpec((3, rows, cols), lambda i: (0, off // rows, 0))],
        out_specs=pl.BlockSpec((rows, cols), lambda i: (0, 0)),
        out_shape=jax.ShapeDtypeStruct((rows, cols), F32),
        compiler_params=_cp(("arbitrary",)))(p, from_chips)


def _chip_sum_adamw(p, from_chips, off, rows, w, m, v, tr, name):
    cols = p.shape[1]
    c1 = 1.0 - ADAM_B1 ** ADAM_STEP
    c2 = 1.0 - ADAM_B2 ** ADAM_STEP

    def body(p_ref, r_ref, w_ref, m_ref, v_ref, g_ref, d_ref, mo_ref, vo_ref):
        gv = p_ref[...]
        for j in range(3):
            gv = gv + r_ref[j].astype(F32)
        g_ref[...] = gv
        mn = ADAM_B1 * m_ref[...] + (1.0 - ADAM_B1) * gv
        vn = ADAM_B2 * v_ref[...] + (1.0 - ADAM_B2) * (gv * gv)
        mo_ref[...] = mn
        vo_ref[...] = vn
        d_ref[...] = -ADAM_LR * ((mn / c1) / (jnp.sqrt(vn / c2) + ADAM_EPS) + ADAM_WD * w_ref[...])

    tile = pl.BlockSpec((tr, cols), lambda i: (i, 0))
    shape = jax.ShapeDtypeStruct((rows, cols), F32)
    return pl.pallas_call(
        body, name=name, grid=(rows // tr,),
        in_specs=[tile, pl.BlockSpec((3, tr, cols), lambda i: (0, off // tr + i, 0)), tile, tile, tile],
        out_specs=[tile] * 4,
        out_shape=[shape] * 4,
        compiler_params=_cp(("parallel",)))(p, from_chips, w, m, v)


def _row_tile(rows, cap):
    t = min(rows, cap)
    while rows % t or t % 8:
        t -= 8
    return t


def _ada_mod(c_all, w, b, name):
    n = w.shape[1]

    def body(c_ref, w_ref, b_ref, o_ref):
        cv = c_ref[...]
        o_ref[...] = _exact_dot(cv * _sigmoid(cv), w_ref[...]) + b_ref[...]

    return pl.pallas_call(body, name=name, out_shape=jax.ShapeDtypeStruct((N_DEV, n), F32),
                          compiler_params=pltpu.CompilerParams(vmem_limit_bytes=VMEM_LIMIT))(c_all, w, b)


def _ada_grad(c_all, dmod, name):
    n = dmod.shape[1]

    def body(c_ref, d_ref, o_ref):
        cv = c_ref[...]
        o_ref[...] = _dot(cv * _sigmoid(cv), d_ref[...], 0, 0, lax.Precision.HIGHEST)

    return pl.pallas_call(body, name=name, out_shape=jax.ShapeDtypeStruct((D, n), F32),
                          compiler_params=pltpu.CompilerParams(vmem_limit_bytes=VMEM_LIMIT))(c_all, dmod)


def _adamw(w, g, m, v, name):
    rows, cols = w.shape
    tr = _row_tile(rows, 256) if rows % 8 == 0 else rows
    c1 = 1.0 - ADAM_B1 ** ADAM_STEP
    c2 = 1.0 - ADAM_B2 ** ADAM_STEP

    def body(w_ref, g_ref, m_ref, v_ref, d_ref, mo_ref, vo_ref):
        gv = g_ref[...]
        mn = ADAM_B1 * m_ref[...] + (1.0 - ADAM_B1) * gv
        vn = ADAM_B2 * v_ref[...] + (1.0 - ADAM_B2) * (gv * gv)
        mo_ref[...] = mn
        vo_ref[...] = vn
        d_ref[...] = -ADAM_LR * ((mn / c1) / (jnp.sqrt(vn / c2) + ADAM_EPS) + ADAM_WD * w_ref[...])

    spec = pl.BlockSpec((tr, cols), lambda i: (i, 0))
    shape = jax.ShapeDtypeStruct((rows, cols), F32)
    return pl.pallas_call(body, name=name, grid=(rows // tr,), in_specs=[spec] * 4, out_specs=[spec] * 3,
                          out_shape=[shape] * 3, compiler_params=_cp(("parallel",)))(w, g, m, v)


def _sum8_loss(v, loss_row, name):
    rows = v.shape[0] // N_DEV

    def body(v_ref, o_ref, l_ref):
        acc = v_ref[0:rows, :]
        for k in range(1, N_DEV):
            acc = acc + v_ref[k * rows:(k + 1) * rows, :]
        o_ref[...] = acc
        part = jnp.sum(acc[loss_row:loss_row + 8, :], axis=0, keepdims=True)
        l_ref[...] = jnp.broadcast_to(jnp.sum(part, axis=1, keepdims=True), (8, LANE))

    return pl.pallas_call(body, name=name,
                          out_shape=[jax.ShapeDtypeStruct((rows, LANE), F32), jax.ShapeDtypeStruct((8, LANE), F32)],
                          compiler_params=pltpu.CompilerParams(vmem_limit_bytes=VMEM_LIMIT))(v)


WEIGHT_NAMES = ("w_ada", "b_ada", "ffn1_norm", "ffn1_w_gate", "ffn1_w_up", "ffn1_w_down", "mix_norm", "w_in",
                "conv_w", "conv_b", "dt_bias", "a_log", "d_skip", "ssd_norm_w", "pool_w", "pool_b", "pool_scale",
                "w_out", "ffn2_norm", "ffn2_w_gate", "ffn2_w_up", "ffn2_w_down", "final_norm")

FF_SHARD = FF // N_DEV
IN_SHARD = D_IN // N_DEV
MAIN_SHARD = D_MAIN // N_DEV
EDGE = 16
OUT_SHARD = 2 * D_SSD // N_DEV
ADA_SHARD = 9 * D // N_DEV
POOL_SHARD_ROWS = 4 * 32 * POOL_GW // D
GPACK = dict(w_in=(0, MAIN_SHARD), w_out=(512, OUT_SHARD), pool_w=(768, POOL_SHARD_ROWS),
             gate1=(0, FF_SHARD), up1=(352, FF_SHARD), down1=(704, FF_SHARD),
             gate2=(0, FF_SHARD), up2=(352, FF_SHARD), down2=(704, FF_SHARD))
GROUP_ROWS = 1056

SMALL_ROWS = dict(dmod=(0, 72), ffn1_norm=(72, 8), mix_norm=(80, 8), ffn2_norm=(88, 8), final_norm=(96, 8),
                  ssd_norm_w=(104, 8), pool_scale=(112, 8), conv_b=(120, 16), conv_w=(136, 64), pool_b=(200, 8),
                  ssd=(208, 3), loss=(216, 8), w_in_dt=(224, 128), w_in_head=(352, 128), w_in_tail=(480, 128))
SMALL_TOTAL = 608

MAIN_FROM_OWN = (16, 14, 12, 10, 8, 6, 20, 18)
OWN_FROM_MAIN = (16, 18, 20, 22, 24, 26, 12, 14)


def _rows128(v, rows):
    flat = v.reshape(-1)
    return jnp.pad(flat, (0, rows * LANE - flat.shape[0])).reshape(rows, LANE)


def _pad_lanes(v):
    return jnp.pad(v.reshape(-1), (0, LANE - v.size))


def kernel(x, c, w_ada, b_ada, ffn1_norm, ffn1_w_gate, ffn1_w_up, ffn1_w_down, mix_norm, w_in, conv_w, conv_b, dt_bias, a_log, d_skip, ssd_norm_w, pool_w, pool_b, pool_scale, w_out, ffn2_norm, ffn2_w_gate, ffn2_w_up, ffn2_w_down, final_norm, loss_target, m_w_ada, m_b_ada, m_ffn1_norm, m_ffn1_w_gate, m_ffn1_w_up, m_ffn1_w_down, m_mix_norm, m_w_in, m_conv_w, m_conv_b, m_dt_bias, m_a_log, m_d_skip, m_ssd_norm_w, m_pool_w, m_pool_b, m_pool_scale, m_w_out, m_ffn2_norm, m_ffn2_w_gate, m_ffn2_w_up, m_ffn2_w_down, m_final_norm, v_w_ada, v_b_ada, v_ffn1_norm, v_ffn1_w_gate, v_ffn1_w_up, v_ffn1_w_down, v_mix_norm, v_w_in, v_conv_w, v_conv_b, v_dt_bias, v_a_log, v_d_skip, v_ssd_norm_w, v_pool_w, v_pool_b, v_pool_scale, v_w_out, v_ffn2_norm, v_ffn2_w_gate, v_ffn2_w_up, v_ffn2_w_down, v_final_norm):
    given = dict(locals())
    w = {n: given[n] for n in WEIGHT_NAMES}
    m = {n: given["m_" + n] for n in WEIGHT_NAMES}
    v = {n: given["v_" + n] for n in WEIGHT_NAMES}
    mx, my, mc = _mesh_pos()
    me = 4 * mx + 2 * my + mc

    w_in_t = w_in[0].T
    small = jnp.concatenate([c.reshape(-1), conv_w.reshape(-1), pool_b.reshape(-1), pool_w.reshape(-1),
                             w_in_t[0:EDGE].reshape(-1), w_in_t[IN_SHARD - EDGE:IN_SHARD].reshape(-1)])
    small_rows = 536
    gs = _all_gather(_rows128(small, small_rows), [(0, small_rows)], "ag_small").reshape(N_DEV, small_rows * LANE)
    c_all = gs[:, 0:D]
    conv_w_full = gs[:, 1024:2048].reshape(N_DEV, 4, 256).transpose(1, 0, 2).reshape(4, D_XBC)
    pool_b_full = gs[:, 2048:2176].reshape(N_DEV, 4, 32).transpose(1, 0, 2).reshape(1, D_POOL)
    pool_w_full = gs[:, 2176:2176 + 32768].reshape(N_DEV, 4, 32, POOL_GW).transpose(1, 0, 2, 3).reshape(4, POOL_GW, POOL_GW).astype(BF16)
    heads = gs[:, 34944:34944 + EDGE * D].reshape(N_DEV, EDGE, D)
    tails = gs[:, 34944 + EDGE * D:34944 + 2 * EDGE * D].reshape(N_DEV, EDGE, D)

    prev_tail = lax.dynamic_index_in_dim(tails, jnp.maximum(me - 1, 0), axis=0, keepdims=False)
    next_head = lax.dynamic_index_in_dim(heads, jnp.minimum(me + 1, N_DEV - 1), axis=0, keepdims=False)
    first = jnp.asarray(MAIN_FROM_OWN, jnp.int32)[me]

    def window(rows, before, size):
        total = EDGE + IN_SHARD + EDGE
        padded = jnp.pad(rows, ((before, total - before - rows.shape[0]), (0, 0)))
        return lax.dynamic_slice(padded, (first, 0), (size, D))

    main_shard = (window(prev_tail, 0, MAIN_SHARD) + window(w_in_t, EDGE, MAIN_SHARD)
                  + window(next_head, EDGE + IN_SHARD, MAIN_SHARD))
    dt_rows = jnp.concatenate([tails[5], heads[6]], axis=0)[4:4 + N_HEADS]
    w_dt = jnp.pad(dt_rows, ((0, LANE - N_HEADS), (0, 0))).astype(BF16)

    b_ada_cols = lax.dynamic_slice(b_ada, (0, me * ADA_SHARD), (1, ADA_SHARD))
    mod_part = _ada_mod(c_all, w_ada[0], b_ada_cols, "ada_mod")
    mod_all = _all_gather(mod_part, [(0, N_DEV)], "ag_mod").reshape(N_DEV, N_DEV, ADA_SHARD)
    mod = lax.dynamic_index_in_dim(mod_all, me, axis=1, keepdims=False).reshape(9, D)

    packs = (jnp.concatenate([ffn1_w_gate[0].T, ffn1_w_up[0].T], axis=0).astype(BF16),
             ffn1_w_down[0].astype(BF16),
             main_shard.astype(BF16),
             w_out[0].astype(BF16),
             jnp.concatenate([ffn2_w_gate[0].T, ffn2_w_up[0].T, ffn2_w_down[0]], axis=0).astype(BF16))
    packs, _ = lax.optimization_barrier((packs, c_all))
    ffn_regions = [(0, FF_SHARD), (FF_SHARD, FF_SHARD), (2 * FF_SHARD, FF_SHARD)]
    full_a = _all_gather_async(packs[0], ffn_regions[0:2], "ag_weights_ffn1_in", 1)
    full_d = _all_gather_async(packs[1], ffn_regions[0:1], "ag_weights_ffn1_out", 2)
    full_in = _all_gather_async(packs[2], [(0, MAIN_SHARD)], "ag_weights_in_proj", 9)
    full_out = _all_gather_async(packs[3], [(0, OUT_SHARD)], "ag_weights_out_proj", 10)
    full_2 = _all_gather_async(packs[4], ffn_regions, "ag_weights_ffn2", 11)
    wff1 = dict(gate=(full_a, 0), up=(full_a, 1), down=(full_d, 0))

    def in_proj_weights(x1):
        w_i, x1 = lax.optimization_barrier((full_in, x1))
        return x1, w_i

    def out_proj_weights(ys):
        w_o, ys = lax.optimization_barrier((full_out, ys))
        return ys, w_o

    def ffn2_weights(x2):
        w_2, x2 = lax.optimization_barrier((full_2, x2))
        return x2, dict(gate=(w_2, 0), up=(w_2, 1), down=(w_2, 2))

    later_weights = (in_proj_weights, out_proj_weights, ffn2_weights)

    vecs = dict(ffn1_norm=ffn1_norm, mix_norm=mix_norm, ffn2_norm=ffn2_norm, final_norm=final_norm.reshape(1, D),
                conv_w=conv_w_full, conv_b=conv_b, ssd_norm_w=ssd_norm_w, pool_b=pool_b_full, pool_scale=pool_scale,
                ssd_par=jnp.concatenate([_pad_lanes(dt_bias)[None], _pad_lanes(a_log)[None], _pad_lanes(d_skip)[None],
                                         jnp.zeros((5, LANE), F32)], axis=0))
    dx0, st, dw1, dw3, d_win, d_wout, d_pool_w = _local_step(
        x[0], loss_target[0], mod, wff1, w_dt, later_weights, pool_w_full, vecs, min(512, x.shape[1]))

    dwin, d_w_dt = d_win
    dwout = jnp.concatenate(d_wout, axis=0)
    dpool = d_pool_w.reshape(4, N_DEV, 32, POOL_GW).transpose(1, 0, 2, 3).reshape(N_DEV * POOL_SHARD_ROWS, D)
    pos = jnp.stack([mc, 2 * mx + my]).astype(jnp.int32)
    by_key = dict(zip(("gate1", "up1", "down1", "gate2", "up2", "down2", "w_out", "w_in", "pool_w"),
                      (*dw1, *dw3, dwout, dwin, dpool)))
    reduced = {}
    for tag, keys, cid in (("ffn2", ("gate2", "up2", "down2"), 3), ("mix", ("w_in", "w_out", "pool_w"), 5),
                           ("ffn1", ("gate1", "up1", "down1"), 7)):
        grads = [(by_key[k], *GPACK[k]) for k in keys]
        from_sibling = _rs_pair(grads, GROUP_ROWS, f"rs_pair_{tag}", cid)
        pairs = dict(zip(keys, _pair_sum(grads, from_sibling, pos, f"rs_pair_sum_{tag}")))
        from_chips = _rs_chips([(pairs[k][1], *GPACK[k]) for k in keys], GROUP_ROWS, f"rs_chips_{tag}", cid + 1)
        for k in keys:
            reduced[k] = (pairs[k][0], from_chips)

    delta, new_m, new_v, shard_grad = {}, {}, {}, {}
    fused = dict(gate1=("ffn1_w_gate", True, 176), up1=("ffn1_w_up", True, 176), down1=("ffn1_w_down", False, 176),
                 gate2=("ffn2_w_gate", True, 176), up2=("ffn2_w_up", True, 176), down2=("ffn2_w_down", False, 176),
                 w_out=("w_out", False, 128), pool_w=("pool_w", False, POOL_SHARD_ROWS))
    for k, (n, is_transposed, tr) in fused.items():
        shp = w[n].shape
        rows = GPACK[k][1]
        view = (lambda t: t[0].T) if is_transposed else (lambda t: t.reshape(rows, D))
        back = (lambda t: t.T[None]) if is_transposed else (lambda t: t.reshape(shp))
        g_, d_, m_, v_ = _chip_sum_adamw(reduced[k][0], reduced[k][1], *GPACK[k], view(w[n]), view(m[n]), view(v[n]),
                                         tr, f"adamw_{n}")
        shard_grad[n], delta[n], new_m[n], new_v[n] = back(g_), back(d_), back(m_), back(v_)
    g_main = _chip_sum(reduced["w_in"][0], reduced["w_in"][1], *GPACK["w_in"], "rs_chip_sum_w_in")

    dmod = jnp.concatenate([st["ffn1"][0:3], st["mix"][0:3], st["ffn2"][0:3]], axis=0)
    sg = jnp.concatenate([
        dmod.reshape(-1), st["ffn1"][3], st["mix"][3], st["ffn2"][3], st["loss"][0], st["gn"][0], st["pool"][1],
        st["conv"][4], st["conv"][0:4].reshape(-1), st["pool"][0], st["ssd"][0:3].reshape(-1),
        jnp.zeros((5 * LANE,), F32), st["loss"][1],
        d_w_dt[0:N_HEADS].reshape(-1), g_main[0:EDGE].reshape(-1), g_main[MAIN_SHARD - EDGE:MAIN_SHARD].reshape(-1)])
    sg_all = _all_gather(sg.reshape(SMALL_TOTAL, LANE), [(0, SMALL_TOTAL)], "ag_small_grads")
    tot, loss_b = _sum8_loss(sg_all, SMALL_ROWS["loss"][0], "small_sum")
    loss = loss_b[0, 0]
    per_dev = sg_all.reshape(N_DEV, SMALL_TOTAL * LANE)
    dmod_all = per_dev[:, 0:9 * D]
    g_w_ada = _ada_grad(c_all, lax.dynamic_slice(dmod_all, (0, me * ADA_SHARD), (N_DEV, ADA_SHARD)), "ada_grad")

    def edge_rows(k):
        off, n = SMALL_ROWS[k]
        return per_dev[:, off * LANE:(off + n) * LANE].reshape(N_DEV, EDGE, D)

    g_dt = tot[SMALL_ROWS["w_in_dt"][0]:SMALL_ROWS["w_in_dt"][0] + SMALL_ROWS["w_in_dt"][1]].reshape(N_HEADS, D)
    before = jnp.where(me == 6, g_dt, edge_rows("w_in_tail")[6])
    after = jnp.where(me == 5, g_dt, lax.dynamic_index_in_dim(edge_rows("w_in_head"), jnp.minimum(me + 1, N_DEV - 1),
                                                               axis=0, keepdims=False))
    first_own = jnp.asarray(OWN_FROM_MAIN, jnp.int32)[me]

    def own_window(rows, lead):
        total = EDGE + MAIN_SHARD + EDGE
        padded = jnp.pad(rows, ((lead, total - lead - rows.shape[0]), (0, 0)))
        return lax.dynamic_slice(padded, (first_own, 0), (IN_SHARD, D))

    g_win_t = own_window(before, 0) + own_window(g_main, EDGE) + own_window(after, EDGE + MAIN_SHARD)

    def tot_rows(k):
        off, n = SMALL_ROWS[k]
        return tot[off:off + n].reshape(-1)

    g_conv_w = lax.dynamic_slice(tot_rows("conv_w").reshape(4, D_XBC), (0, me * 256), (4, 256))
    g_pool_b = lax.dynamic_slice(tot_rows("pool_b").reshape(4, POOL_GW), (0, me * 32), (4, 32))
    g_ssd = tot_rows("ssd").reshape(3, LANE)
    grad = {
        "w_ada": g_w_ada[None], "b_ada": tot_rows("dmod").reshape(1, 9 * D),
        "ffn1_norm": tot_rows("ffn1_norm")[None], "mix_norm": tot_rows("mix_norm")[None],
        "ffn2_norm": tot_rows("ffn2_norm")[None], "final_norm": tot_rows("final_norm"),
        "ssd_norm_w": tot_rows("ssd_norm_w")[None], "pool_scale": tot_rows("pool_scale")[None],
        "conv_b": tot_rows("conv_b")[None], "conv_w": g_conv_w[None], "pool_b": g_pool_b[None],
        "dt_bias": g_ssd[0:1, 0:N_HEADS], "a_log": g_ssd[1:2, 0:N_HEADS], "d_skip": g_ssd[2:3, 0:N_HEADS],
        "w_in": g_win_t.T[None], **shard_grad,
    }

    d_, m_, v_ = _adamw(w_ada[0], g_w_ada, m_w_ada[0], v_w_ada[0], "adamw_w_ada")
    delta["w_ada"], new_m["w_ada"], new_v["w_ada"] = d_[None], m_[None], v_[None]
    d_, m_, v_ = _adamw(w_in[0].T, g_win_t, m_w_in[0].T, v_w_in[0].T, "adamw_w_in")
    delta["w_in"], new_m["w_in"], new_v["w_in"] = d_.T[None], m_.T[None], v_.T[None]
    big = ("w_ada", "w_in") + tuple(n for n, _, _ in fused.values())
    small_names = [n for n in WEIGHT_NAMES if n not in big]
    sizes = [LANE if w[n].size < LANE else w[n].size for n in small_names]
    small_rows_adam = -(-sum(sizes) // (8 * LANE)) * 8

    def pack_small(t):
        return _rows128(jnp.concatenate([_pad_lanes(t[n]) if t[n].size < LANE else t[n].reshape(-1) for n in small_names]),
                        small_rows_adam)

    d_s, m_s, v_s = _adamw(pack_small(w), pack_small(grad), pack_small(m), pack_small(v), "adamw_small")
    off = 0
    for n, size in zip(small_names, sizes):
        for res, packed in ((delta, d_s), (new_m, m_s), (new_v, v_s)):
            res[n] = packed.reshape(-1)[off:off + w[n].size].reshape(w[n].shape)
        off += size

    return (loss, dx0[None], *[grad[n] for n in WEIGHT_NAMES], *[delta[n] for n in WEIGHT_NAMES],
            *[new_m[n] for n in WEIGHT_NAMES], *[new_v[n] for n in WEIGHT_NAMES])
```

```python
import jax
import jax.numpy as jnp
from jax import lax
from jax.experimental import pallas as pl
from jax.experimental.pallas import tpu as pltpu
from jax.experimental.pallas import tpu_sc as plsc

F32 = jnp.float32
BF16 = jnp.bfloat16
MESH = pl.DeviceIdType.MESH

N_DEV = 8
D = 1024
FF = 2816
D_SSD = 1024
N_HEADS = 16
HEAD_DIM = 64
N_GROUPS = 4
N_STATE = 128
CHUNK = 128
GROUP_W = D_SSD // N_GROUPS
D_XBC = D_SSD + 2 * N_GROUPS * N_STATE
D_POOL = 1024
POOL_WINDOWS = (2, 4, 8, 16)
POOL_GW = 256
D_IN = 4112
D_MAIN = 4096
COL_Z, COL_XBC, COL_U = 0, 1024, 3072
EPS = 1e-6
FFN_RES = 0.5
LANE = 128
HALO = 16

ADAM_LR, ADAM_B1, ADAM_B2, ADAM_EPS, ADAM_WD, ADAM_STEP = 0.001, 0.9, 0.999, 1e-08, 0.01, 10

VMEM_LIMIT = 56 << 20


def _cp(sem):
    return pltpu.CompilerParams(dimension_semantics=sem, vmem_limit_bytes=VMEM_LIMIT)


def _dot(a, b, ca, cb, prec=None):
    return lax.dot_general(a, b, (((ca,), (cb,)), ((), ())), precision=prec,
                           preferred_element_type=F32)


def _exact_dot(a, b):
    return _dot(a, b, 1, 0, lax.Precision.HIGHEST)


def _sigmoid(v):
    return 1.0 / (1.0 + jnp.exp(-v))


def _silu_grad(v, sg):
    return sg * (1.0 + v * (1.0 - sg))


def _mm_nt(a, bt, tm, tn, out_dtype, name):
    m, k = a.shape
    n = bt.shape[0]

    def body(a_ref, b_ref, o_ref):
        o_ref[...] = _dot(a_ref[...], b_ref[...], 1, 1).astype(out_dtype)

    return pl.pallas_call(
        body, name=name, grid=(n // tn, m // tm),
        in_specs=[pl.BlockSpec((tm, k), lambda j, i: (i, 0)),
                  pl.BlockSpec((tn, k), lambda j, i: (j, 0))],
        out_specs=pl.BlockSpec((tm, tn), lambda j, i: (i, j)),
        out_shape=jax.ShapeDtypeStruct((m, n), out_dtype),
        compiler_params=_cp(("parallel", "parallel")))(a, bt)


def _mm_tn(a, b, tm, tk, name):
    kk, m = a.shape
    n = b.shape[1]
    nk = kk // tk
    if nk == 1:
        def whole(a_ref, b_ref, o_ref):
            o_ref[...] = _dot(a_ref[...], b_ref[...], 0, 0)

        return pl.pallas_call(
            whole, name=name, grid=(m // tm,),
            in_specs=[pl.BlockSpec((kk, tm), lambda i: (0, i)),
                      pl.BlockSpec((kk, n), lambda i: (0, 0))],
            out_specs=pl.BlockSpec((tm, n), lambda i: (i, 0)),
            out_shape=jax.ShapeDtypeStruct((m, n), F32),
            compiler_params=_cp(("parallel",)))(a, b)

    def body(a_ref, b_ref, o_ref, acc):
        k = pl.program_id(1)

        @pl.when(k == 0)
        def _():
            acc[...] = jnp.zeros_like(acc)

        acc[...] += _dot(a_ref[...], b_ref[...], 0, 0)

        @pl.when(k == nk - 1)
        def _():
            o_ref[...] = acc[...]

    return pl.pallas_call(
        body, name=name, grid=(m // tm, nk),
        in_specs=[pl.BlockSpec((tk, tm), lambda i, k: (k, i)),
                  pl.BlockSpec((tk, n), lambda i, k: (k, 0))],
        out_specs=pl.BlockSpec((tm, n), lambda i, k: (i, 0)),
        out_shape=jax.ShapeDtypeStruct((m, n), F32),
        scratch_shapes=[pltpu.VMEM((tm, n), F32)],
        compiler_params=_cp(("parallel", "arbitrary")))(a, b)


def _mm_tn_rows(parts, b, tm, name):
    kk, n = b.shape
    blocks = [a.shape[1] // tm for a in parts]
    starts = [sum(blocks[:p]) for p in range(len(parts))]

    def body(*refs):
        a_refs, b_ref, o_ref = refs[:len(parts)], refs[len(parts)], refs[len(parts) + 1]
        i = pl.program_id(0)
        for p, a_ref in enumerate(a_refs):
            @pl.when(jnp.logical_and(i >= starts[p], i < starts[p] + blocks[p]))
            def _():
                o_ref[...] = _dot(a_ref[...], b_ref[...], 0, 0)

    def part_spec(p):
        return pl.BlockSpec((kk, tm), lambda i: (0, jnp.clip(i - starts[p], 0, blocks[p] - 1)))

    return pl.pallas_call(
        body, name=name, grid=(sum(blocks),),
        in_specs=[part_spec(p) for p in range(len(parts))] + [pl.BlockSpec((kk, n), lambda i: (0, 0))],
        out_specs=pl.BlockSpec((tm, n), lambda i: (i, 0)),
        out_shape=jax.ShapeDtypeStruct((sum(blocks) * tm, n), F32),
        compiler_params=_cp(("parallel",)))(*parts, b)


def _modulated(xv, wn, mod_ref, k):
    r = lax.rsqrt(jnp.mean(xv * xv, axis=-1, keepdims=True) + EPS)
    hn = xv * r * wn
    return (hn * (1.0 + mod_ref[3 * k + 1:3 * k + 2, :]) + mod_ref[3 * k:3 * k + 1, :]).astype(BF16)


def _prenorm(x, wn, mod, k, tm, name):
    seq = x.shape[0]

    def body(x_ref, wn_ref, mod_ref, h_ref):
        h_ref[...] = _modulated(x_ref[...], wn_ref[...], mod_ref, k)

    return pl.pallas_call(
        body, name=name, grid=(seq // tm,),
        in_specs=[pl.BlockSpec((tm, D), lambda i: (i, 0)),
                  pl.BlockSpec((1, D), lambda i: (0, 0)),
                  pl.BlockSpec((9, D), lambda i: (0, 0))],
        out_specs=pl.BlockSpec((tm, D), lambda i: (i, 0)),
        out_shape=jax.ShapeDtypeStruct((seq, D), BF16),
        compiler_params=_cp(("parallel",)))(x, wn, mod)


def _norm_bwd(dh, xv, dxo, branch, wn, sc, res, stats_ref, first):
    r = lax.rsqrt(jnp.mean(xv * xv, axis=-1, keepdims=True) + EPS)
    xn = xv * r
    dhn = dh * (1.0 + sc)
    dxn = dhn * wn
    dx = dxo + r * (dxn - xn * jnp.mean(dxn * xn, axis=-1, keepdims=True))
    rows = jnp.concatenate([
        jnp.sum(dh, axis=0, keepdims=True),
        jnp.sum(dh * (xn * wn), axis=0, keepdims=True),
        jnp.sum(branch * dxo, axis=0, keepdims=True) * res,
        jnp.sum(dhn * xn, axis=0, keepdims=True),
        jnp.zeros((4, D), F32)], axis=0)

    @pl.when(first)
    def _():
        stats_ref[...] = rows

    @pl.when(jnp.logical_not(first))
    def _():
        stats_ref[...] += rows

    return dx


def _ffn_up(h, wg, wu, tm, tn, name):
    seq = h.shape[0]
    nj = FF // tn

    def body(h_ref, wg_ref, wu_ref, a_ref, pg_ref, pu_ref):
        hv = h_ref[...]
        g = _dot(hv, wg_ref[...], 1, 1)
        u = _dot(hv, wu_ref[...], 1, 1)
        sg = _sigmoid(g)
        s = g * sg
        a_ref[...] = (s * u).astype(BF16)
        pg_ref[...] = (u * _silu_grad(g, sg)).astype(BF16)
        pu_ref[...] = s.astype(BF16)

    act = pl.BlockSpec((tm, tn), lambda j, i: (i, j))
    return pl.pallas_call(
        body, name=name, grid=(nj, seq // tm),
        in_specs=[pl.BlockSpec((tm, D), lambda j, i: (i, 0)),
                  pl.BlockSpec((tn, D), lambda j, i: (wg[1] * nj + j, 0)),
                  pl.BlockSpec((tn, D), lambda j, i: (wu[1] * nj + j, 0))],
        out_specs=[act, act, act],
        out_shape=[jax.ShapeDtypeStruct((seq, FF), BF16)] * 3,
        compiler_params=_cp(("parallel", "parallel")))(h, wg[0], wu[0])


def _ffn_down(a, w, blk, x, mod, grow, wn_next, k_next, w_dt, tm, name):
    seq = a.shape[0]

    def body(a_ref, w_ref, x_ref, mod_ref, wn_ref, wdt_ref, xo_ref, f_ref, h_ref, dt_ref):
        f = _dot(a_ref[...], w_ref[...], 1, 0)
        f_ref[...] = f.astype(BF16)
        xo = x_ref[...] + (FFN_RES * mod_ref[grow:grow + 1, :]) * f
        xo_ref[...] = xo
        h = _modulated(xo, wn_ref[...], mod_ref, k_next)
        h_ref[...] = h
        dt_ref[...] = _dot(h, wdt_ref[...], 1, 1)

    tok = pl.BlockSpec((tm, D), lambda i: (i, 0))
    return pl.pallas_call(
        body, name=name, grid=(seq // tm,),
        in_specs=[pl.BlockSpec((tm, FF), lambda i: (i, 0)),
                  pl.BlockSpec((FF, D), lambda i: (blk, 0)),
                  tok,
                  pl.BlockSpec((9, D), lambda i: (0, 0)),
                  pl.BlockSpec((1, D), lambda i: (0, 0)),
                  pl.BlockSpec((LANE, D), lambda i: (0, 0))],
        out_specs=[tok, tok, tok, pl.BlockSpec((tm, LANE), lambda i: (i, 0))],
        out_shape=[jax.ShapeDtypeStruct((seq, D), F32), jax.ShapeDtypeStruct((seq, D), BF16),
                   jax.ShapeDtypeStruct((seq, D), BF16), jax.ShapeDtypeStruct((seq, LANE), F32)],
        compiler_params=_cp(("parallel",)))(a, w, x, mod, wn_next, w_dt)


def _ffn_down_loss(a, w, blk, x, mod, grow, wf, tgt, tm, name):
    seq = a.shape[0]

    def body(a_ref, w_ref, x_ref, mod_ref, wf_ref, t_ref, f_ref, dx_ref, df_ref, st_ref):
        f = _dot(a_ref[...], w_ref[...], 1, 0)
        f_ref[...] = f.astype(BF16)
        xv = x_ref[...] + (FFN_RES * mod_ref[grow:grow + 1, :]) * f
        wv = wf_ref[...]
        r = lax.rsqrt(jnp.mean(xv * xv, axis=-1, keepdims=True) + EPS)
        xn = xv * r
        e = xn * wv - t_ref[...]
        dy = e * (1.0 / D)
        dxn = dy * wv
        dx = r * (dxn - xn * jnp.mean(dxn * xn, axis=-1, keepdims=True))
        dx_ref[...] = dx
        df_ref[...] = (dx * (FFN_RES * mod_ref[grow:grow + 1, :])).astype(BF16)
        rows = jnp.concatenate([
            jnp.sum(dy * xn, axis=0, keepdims=True),
            jnp.sum(e * e, axis=0, keepdims=True) * (0.5 / D),
            jnp.zeros((6, D), F32)], axis=0)

        @pl.when(pl.program_id(0) == 0)
        def _():
            st_ref[...] = rows

        @pl.when(pl.program_id(0) != 0)
        def _():
            st_ref[...] += rows

    tok = pl.BlockSpec((tm, D), lambda i: (i, 0))
    return pl.pallas_call(
        body, name=name, grid=(seq // tm,),
        in_specs=[pl.BlockSpec((tm, FF), lambda i: (i, 0)),
                  pl.BlockSpec((FF, D), lambda i: (blk, 0)),
                  tok,
                  pl.BlockSpec((9, D), lambda i: (0, 0)),
                  pl.BlockSpec((1, D), lambda i: (0, 0)),
                  tok],
        out_specs=[tok, tok, tok, pl.BlockSpec((8, D), lambda i: (0, 0))],
        out_shape=[jax.ShapeDtypeStruct((seq, D), BF16), jax.ShapeDtypeStruct((seq, D), F32),
                   jax.ShapeDtypeStruct((seq, D), BF16), jax.ShapeDtypeStruct((8, D), F32)],
        compiler_params=_cp(("arbitrary",)))(a, w, x, mod, wf, tgt)


def _ffn_bwd_da(df, w, blk, pg, pu, tm, tn, name):
    seq = df.shape[0]
    nj = FF // tn

    def body(df_ref, w_ref, pg_ref, pu_ref, dg_ref, du_ref):
        da = _dot(df_ref[...], w_ref[...], 1, 1)
        dg_ref[...] = (da * pg_ref[...].astype(F32)).astype(BF16)
        du_ref[...] = (da * pu_ref[...].astype(F32)).astype(BF16)

    act = pl.BlockSpec((tm, tn), lambda j, i: (i, j))
    return pl.pallas_call(
        body, name=name, grid=(nj, seq // tm),
        in_specs=[pl.BlockSpec((tm, D), lambda j, i: (i, 0)),
                  pl.BlockSpec((tn, D), lambda j, i: (blk * nj + j, 0)),
                  act, act],
        out_specs=[act, act],
        out_shape=[jax.ShapeDtypeStruct((seq, FF), BF16)] * 2,
        compiler_params=_cp(("parallel", "parallel")))(df, w, pg, pu)


def _ffn_bwd_dh(dg, du, wg, wu, x, dxo, fb, wn, mod, k, nxt, tm, name):
    seq = x.shape[0]

    def body(dg_ref, du_ref, wg_ref, wu_ref, x_ref, dxo_ref, f_ref, wn_ref, mod_ref, dx_ref, st_ref, *rest):
        dh = _dot(dg_ref[...], wg_ref[...], 1, 0) + _dot(du_ref[...], wu_ref[...], 1, 0)
        dx = _norm_bwd(dh, x_ref[...], dxo_ref[...], f_ref[...].astype(F32), wn_ref[...],
                       mod_ref[3 * k + 1:3 * k + 2, :], FFN_RES, st_ref, pl.program_id(0) == 0)
        dx_ref[...] = dx
        if nxt is not None:
            rest[0][...] = (dx * (nxt[1] * mod_ref[nxt[0]:nxt[0] + 1, :])).astype(BF16)

    tok = pl.BlockSpec((tm, D), lambda i: (i, 0))
    act = pl.BlockSpec((tm, FF), lambda i: (i, 0))
    return pl.pallas_call(
        body, name=name, grid=(seq // tm,),
        in_specs=[act, act,
                  pl.BlockSpec((FF, D), lambda i: (wg[1], 0)),
                  pl.BlockSpec((FF, D), lambda i: (wu[1], 0)),
                  tok, tok, tok,
                  pl.BlockSpec((1, D), lambda i: (0, 0)),
                  pl.BlockSpec((9, D), lambda i: (0, 0))],
        out_specs=[tok, pl.BlockSpec((8, D), lambda i: (0, 0))] + ([tok] if nxt is not None else []),
        out_shape=[jax.ShapeDtypeStruct((seq, D), F32), jax.ShapeDtypeStruct((8, D), F32)]
        + ([jax.ShapeDtypeStruct((seq, D), BF16)] if nxt is not None else []),
        compiler_params=_cp(("arbitrary",)))(dg, du, wg[0], wu[0], x, dxo, fb, wn, mod)


def _ffn_forward(x, h, w, mod, k, wn_next, k_next, w_dt, tm, tag):
    a, pg, pu = _ffn_up(h, w["gate"], w["up"], tm, FF // 2, f"{tag}_up")
    xo, fb, h_next, dt_next = _ffn_down(a, *w["down"], x, mod, 3 * k + 2, wn_next, k_next, w_dt, tm, f"{tag}_down")
    return xo, h_next, dt_next, (x, h, pg, pu, a, fb)


def _ffn_backward(dxo, df, saved, w, wn, mod, k, nxt, tm, tag):
    x, h, pg, pu, a, fb = saved
    dg, du = _ffn_bwd_da(df, *w["down"], pg, pu, tm, FF // 2, f"{tag}_bwd_da")
    seq = x.shape[0]
    d_gate_t = _mm_tn(dg, h, 256, seq, f"{tag}_dw_gate")
    d_up_t = _mm_tn(du, h, 256, seq, f"{tag}_dw_up")
    d_down = _mm_tn(a, df, 256, seq, f"{tag}_dw_down")
    dws, dg, du = lax.optimization_barrier(((d_gate_t, d_up_t, d_down), dg, du))
    outs = _ffn_bwd_dh(dg, du, w["gate"], w["up"], x, dxo, fb, wn, mod, k, nxt, min(tm, 256), f"{tag}_bwd_dh")
    return outs[0], (outs[2] if nxt is not None else None), outs[1], dws


def _prev_rows(tm, col):
    return pl.BlockSpec((HALO, 1024), lambda i, j: (jnp.maximum(i * (tm // HALO) - 1, 0), col + j))


def _conv_pre(ext, cw, cb, rows):
    pre = cb + cw[3:4, :] * ext
    for s in (1, 2, 3):
        pre = pre + cw[3 - s:4 - s, :] * pltpu.roll(ext, s, 0)
    return pre[HALO:HALO + rows]


def _conv_fwd(proj, cw, cb, tm, name):
    seq = proj.shape[0]

    def body(x_ref, p_ref, cw_ref, cb_ref, o_ref, g_ref):
        prev = jnp.where(pl.program_id(0) == 0, 0.0, p_ref[...])
        ext = jnp.concatenate([prev, x_ref[...]], axis=0)
        pre = _conv_pre(ext, cw_ref[...], cb_ref[...], tm)
        sg = _sigmoid(pre)
        o_ref[...] = pre * sg
        g_ref[...] = _silu_grad(pre, sg).astype(BF16)

    c0 = COL_XBC // 1024
    out = pl.BlockSpec((tm, 1024), lambda i, j: (i, j))
    return pl.pallas_call(
        body, name=name, grid=(seq // tm, 2),
        in_specs=[pl.BlockSpec((tm, 1024), lambda i, j: (i, c0 + j)),
                  _prev_rows(tm, c0),
                  pl.BlockSpec((4, 1024), lambda i, j: (0, j)),
                  pl.BlockSpec((1, 1024), lambda i, j: (0, j))],
        out_specs=[out, out],
        out_shape=[jax.ShapeDtypeStruct((seq, D_XBC), F32), jax.ShapeDtypeStruct((seq, D_XBC), BF16)],
        compiler_params=_cp(("parallel", "parallel")))(proj, proj, cw, cb)


def _conv_bwd(dact, slope, proj, cw, tm, name):
    seq = proj.shape[0]
    ni = seq // tm

    def body(d_ref, dn_ref, s_ref, sn_ref, x_ref, p_ref, cw_ref, o_ref, st_ref):
        i = pl.program_id(1)
        cwv = cw_ref[...]
        prev = jnp.where(i == 0, 0.0, p_ref[...])
        ext = jnp.concatenate([prev, x_ref[...]], axis=0)
        dnext = jnp.where(i == ni - 1, 0.0, dn_ref[...] * sn_ref[...].astype(F32))
        dpre = jnp.concatenate([d_ref[...] * s_ref[...].astype(F32), dnext], axis=0)
        n = tm + HALO
        dx = cwv[3:4, :] * dpre
        for s in (1, 2, 3):
            dx = dx + cwv[3 - s:4 - s, :] * pltpu.roll(dpre, n - s, 0)
        o_ref[...] = dx[:tm].astype(BF16)
        dcur = dpre[:tm]
        rows = [jnp.sum(dcur * pltpu.roll(ext, 3 - k, 0)[HALO:HALO + tm], axis=0, keepdims=True) for k in range(3)]
        rows.append(jnp.sum(dcur * ext[HALO:HALO + tm], axis=0, keepdims=True))
        rows.append(jnp.sum(dcur, axis=0, keepdims=True))
        rows.append(jnp.zeros((3, 1024), F32))
        rows = jnp.concatenate(rows, axis=0)

        @pl.when(i == 0)
        def _():
            st_ref[...] = rows

        @pl.when(i != 0)
        def _():
            st_ref[...] += rows

    c0 = COL_XBC // 1024
    cur = pl.BlockSpec((tm, 1024), lambda j, i: (i, j))
    nxt = pl.BlockSpec((HALO, 1024), lambda j, i: (jnp.minimum((i + 1) * (tm // HALO), seq // HALO - 1), j))
    return pl.pallas_call(
        body, name=name, grid=(2, ni),
        in_specs=[cur, nxt, cur, nxt,
                  pl.BlockSpec((tm, 1024), lambda j, i: (i, c0 + j)),
                  pl.BlockSpec((HALO, 1024), lambda j, i: (jnp.maximum(i * (tm // HALO) - 1, 0), c0 + j)),
                  pl.BlockSpec((4, 1024), lambda j, i: (0, j))],
        out_specs=[cur, pl.BlockSpec((8, 1024), lambda j, i: (0, j))],
        out_shape=[jax.ShapeDtypeStruct((seq, D_XBC), BF16), jax.ShapeDtypeStruct((8, D_XBC), F32)],
        compiler_params=_cp(("parallel", "arbitrary")))(dact, dact, slope, slope, proj, proj, cw)


def _bf16_parts(x, n):
    parts, rest = [], x
    for _ in range(n):
        p = rest.astype(BF16)
        parts.append(p)
        rest = rest - p.astype(F32)
    return parts


def _pick(x, sel, n):
    m = x.shape[0]
    prod = _dot(jnp.concatenate(_bf16_parts(x, n), axis=0), sel, 1, 0)
    acc = prod[0:m]
    for i in range(1, n):
        acc = acc + prod[i * m:(i + 1) * m]
    return acc


def _running(mask, x, n):
    k = x.shape[1]
    prod = _dot(mask, jnp.concatenate(_bf16_parts(x, n), axis=1), 1, 0)
    acc = prod[:, 0:k]
    for i in range(1, n):
        acc = acc + prod[:, i * k:(i + 1) * k]
    return acc


def _head_expand():
    r = lax.broadcasted_iota(jnp.int32, (LANE, D_SSD), 0)
    c = lax.broadcasted_iota(jnp.int32, (LANE, D_SSD), 1)
    return (c // HEAD_DIM == r).astype(BF16)


def _head_reduce():
    r = lax.broadcasted_iota(jnp.int32, (D_SSD, LANE), 0)
    c = lax.broadcasted_iota(jnp.int32, (D_SSD, LANE), 1)
    return (r // HEAD_DIM == c).astype(BF16)


def _ssd_common(dtr, par):
    q = CHUNK
    v = dtr + par[0:1, :]
    dt = jnp.maximum(v, 0.0) + jnp.log(1.0 + jnp.exp(-jnp.abs(v)))
    a = -jnp.exp(par[1:2, :])
    adt = dt * a
    li = lax.broadcasted_iota(jnp.int32, (q, q), 0)
    si = lax.broadcasted_iota(jnp.int32, (q, q), 1)
    causal = li >= si
    acs = _running(causal.astype(BF16), adt, 3)
    expand = _head_expand()
    both_l = _pick(jnp.concatenate([dt, acs], axis=0), expand, 3)
    dt_l, acs_l = both_l[0:q], both_l[q:2 * q]
    dskip_l = _pick(jnp.broadcast_to(par[2:3, :], (16, LANE)), expand, 3)[0:1, :]
    last_l = acs_l[q - 1:q, :]
    return dict(v=v, dt=dt, a=a, acs=acs, acs_t=acs.T, causal=causal, dt_l=dt_l, acs_l=acs_l,
                ea_l=jnp.exp(acs_l), ds_l=jnp.exp(last_l - acs_l), cd_l=jnp.exp(last_l), dskip_l=dskip_l)


def _decay(cm, h):
    seg = cm["acs"][:, h:h + 1] - cm["acs_t"][h:h + 1, :]
    return jnp.exp(jnp.where(cm["causal"], seg, -jnp.inf))


def _lane_mask(r):
    lane = lax.broadcasted_iota(jnp.int32, (1, GROUP_W), 1)
    return lane // HEAD_DIM == r


def _ssd_fwd(xbc, proj, par, name):
    seq = xbc.shape[0]
    nc = seq // CHUNK
    q = CHUNK

    def body(x_ref, dt_ref, par_ref, y_ref, hp_ref, state):
        @pl.when(pl.program_id(0) == 0)
        def _():
            state[...] = jnp.zeros_like(state)

        cm = _ssd_common(dt_ref[...], par_ref[...])
        for g in range(N_GROUPS):
            lo = g * GROUP_W
            xs = x_ref[:, lo:lo + GROUP_W]
            bm = x_ref[:, D_SSD + g * N_STATE:D_SSD + (g + 1) * N_STATE].astype(BF16)
            cmat = x_ref[:, D_SSD + N_GROUPS * N_STATE + g * N_STATE:D_SSD + N_GROUPS * N_STATE + (g + 1) * N_STATE].astype(BF16)
            xdt = xs * cm["dt_l"][:, lo:lo + GROUP_W]
            xdt_b = xdt.astype(BF16)
            cb = _dot(cmat, bm, 1, 1)
            scores = jnp.concatenate([(cb * _decay(cm, 4 * g + r)).astype(BF16) for r in range(4)], axis=0)
            yd_heads = _dot(scores, xdt_b, 1, 0)
            yd = yd_heads[0:q]
            for r in range(1, 4):
                yd = jnp.where(_lane_mask(r), yd_heads[r * q:(r + 1) * q], yd)
            hg = state[g]
            hp_ref[0, g] = hg
            yo = _dot(cmat, hg.astype(BF16), 1, 0) * cm["ea_l"][:, lo:lo + GROUP_W]
            y_ref[:, lo:lo + GROUP_W] = yd + yo + cm["dskip_l"][:, lo:lo + GROUP_W] * xs
            xds = (xdt * cm["ds_l"][:, lo:lo + GROUP_W]).astype(BF16)
            state[g] = hg * cm["cd_l"][:, lo:lo + GROUP_W] + _dot(bm, xds, 0, 0)

    return pl.pallas_call(
        body, name=name, grid=(nc,),
        in_specs=[pl.BlockSpec((q, D_XBC), lambda c: (c, 0)),
                  pl.BlockSpec((q, LANE), lambda c: (c, 0)),
                  pl.BlockSpec((8, LANE), lambda c: (0, 0))],
        out_specs=[pl.BlockSpec((q, D_SSD), lambda c: (c, 0)),
                   pl.BlockSpec((1, N_GROUPS, N_STATE, GROUP_W), lambda c: (c, 0, 0, 0))],
        out_shape=[jax.ShapeDtypeStruct((seq, D_SSD), F32),
                   jax.ShapeDtypeStruct((nc, N_GROUPS, N_STATE, GROUP_W), F32)],
        scratch_shapes=[pltpu.VMEM((N_GROUPS, N_STATE, GROUP_W), F32)],
        compiler_params=_cp(("arbitrary",)))(xbc, proj, par)


def _ssd_bwd(dy, xbc, proj, par, hprev, name):
    seq = xbc.shape[0]
    nc = seq // CHUNK
    q = CHUNK

    def body(dy_ref, x_ref, dt_ref, par_ref, hp_ref, dx_ref, ddt_ref, st_ref, dstate):
        step = pl.program_id(0)

        @pl.when(step == 0)
        def _():
            dstate[...] = jnp.zeros_like(dstate)

        par = par_ref[...]
        cm = _ssd_common(dt_ref[...], par)
        reduce = _head_reduce()
        lane128 = lax.broadcasted_iota(jnp.int32, (1, LANE), 1)
        row128 = lax.broadcasted_iota(jnp.int32, (LANE, 1), 0)
        d_acs = jnp.zeros((q, LANE), F32)
        d_acs_t = jnp.zeros((LANE, q), F32)
        last_terms = []
        acs_terms = []
        dxdt_all = []
        for g in range(N_GROUPS):
            lo = g * GROUP_W
            sl = slice(lo, lo + GROUP_W)
            xs = x_ref[:, sl]
            bm32 = x_ref[:, D_SSD + g * N_STATE:D_SSD + (g + 1) * N_STATE]
            cm32 = x_ref[:, D_SSD + N_GROUPS * N_STATE + g * N_STATE:D_SSD + N_GROUPS * N_STATE + (g + 1) * N_STATE]
            bm = bm32.astype(BF16)
            cmat = cm32.astype(BF16)
            dyg = dy_ref[:, sl]
            dyg_b = dyg.astype(BF16)
            xdt = xs * cm["dt_l"][:, sl]
            xdt_b = xdt.astype(BF16)
            hg = hp_ref[0, g]
            hg_b = hg.astype(BF16)
            dhg = dstate[g]
            dhg_b = dhg.astype(BF16)
            ea = cm["ea_l"][:, sl]
            ds = cm["ds_l"][:, sl]
            cd = cm["cd_l"][:, sl]
            yoff = _dot(cmat, hg_b, 1, 0) * ea
            dw = (dyg * ea).astype(BF16)
            d_c = _dot(dw, hg_b, 1, 1)
            d_hprev = _dot(cmat, dw, 0, 0) + dhg * cd
            t_acs = dyg * yoff
            d_last_g = jnp.sum(dhg * hg, axis=0, keepdims=True) * cd
            xds_b = (xdt * ds).astype(BF16)
            dxds = _dot(bm, dhg_b, 1, 0)
            d_b = _dot(xds_b, dhg_b, 1, 1)
            dxdt = dxds * ds
            t_ds = dxds * xdt * ds
            t_acs = t_acs - t_ds
            d_last_g = d_last_g + jnp.sum(t_ds, axis=0, keepdims=True)
            cb = _dot(cmat, bm, 1, 1)
            d_cb = jnp.zeros((q, q), F32)
            decays = [_decay(cm, 4 * g + r) for r in range(4)]
            score_heads = [cb * dec for dec in decays]
            d_s_heads = _dot(jnp.concatenate([jnp.where(_lane_mask(r), dyg, 0.0).astype(BF16) for r in range(4)], axis=0),
                             xdt_b, 1, 1)
            dxdt_heads = _dot(jnp.concatenate([s.astype(BF16) for s in score_heads], axis=1), dyg_b, 0, 0)
            for r in range(4):
                h = 4 * g + r
                dec, s_h = decays[r], score_heads[r]
                d_s = d_s_heads[r * q:(r + 1) * q]
                dxdt = dxdt + jnp.where(_lane_mask(r), dxdt_heads[r * q:(r + 1) * q], 0.0)
                d_cb = d_cb + d_s * dec
                d_m = d_s * s_h
                d_acs = d_acs + jnp.where(lane128 == h, jnp.sum(d_m, axis=1, keepdims=True), 0.0)
                d_acs_t = d_acs_t + jnp.where(row128 == h, jnp.sum(d_m, axis=0, keepdims=True), 0.0)
            d_cb_b = d_cb.astype(BF16)
            d_c = d_c + _dot(d_cb_b, bm, 1, 0)
            d_b = d_b + _dot(d_cb_b, cmat, 0, 0)
            dstate[g] = d_hprev
            dx_ref[:, sl] = dxdt * cm["dt_l"][:, sl] + cm["dskip_l"][:, sl] * dyg
            dx_ref[:, D_SSD + g * N_STATE:D_SSD + (g + 1) * N_STATE] = d_b
            dx_ref[:, D_SSD + N_GROUPS * N_STATE + g * N_STATE:D_SSD + N_GROUPS * N_STATE + (g + 1) * N_STATE] = d_c
            acs_terms.append(t_acs)
            dxdt_all.append(dxdt * xs)
            last_terms.append(d_last_g)
        t_acs_l = jnp.concatenate(acs_terms, axis=1)
        d_dt_l = jnp.concatenate(dxdt_all, axis=1)
        d_last_l = jnp.concatenate(last_terms, axis=1)
        per_head = _pick(jnp.concatenate([t_acs_l, d_dt_l], axis=0), reduce, 2)
        skip_l = jnp.sum(dy_ref[...] * x_ref[:, 0:D_SSD], axis=0, keepdims=True)
        singles = _pick(jnp.concatenate([d_last_l, skip_l, jnp.zeros((14, D_SSD), F32)], axis=0), reduce, 3)
        d_acs = d_acs + per_head[0:q] - d_acs_t.T
        last_row = lax.broadcasted_iota(jnp.int32, (q, 1), 0) == q - 1
        d_acs = d_acs + jnp.where(last_row, singles[0:1, :], 0.0)
        li = lax.broadcasted_iota(jnp.int32, (q, q), 0)
        si = lax.broadcasted_iota(jnp.int32, (q, q), 1)
        d_adt = _running((si >= li).astype(BF16), d_acs, 3)
        d_dt = per_head[q:2 * q] + d_adt * cm["a"]
        d_dtr = d_dt * _sigmoid(cm["v"])
        ddt_ref[...] = d_dtr.astype(BF16)
        d_skip = singles[1:2, :]
        rows = jnp.concatenate([
            jnp.sum(d_dtr, axis=0, keepdims=True),
            jnp.sum(d_adt * cm["dt"], axis=0, keepdims=True) * cm["a"],
            d_skip,
            jnp.zeros((5, LANE), F32)], axis=0)

        @pl.when(step == 0)
        def _():
            st_ref[...] = rows

        @pl.when(step != 0)
        def _():
            st_ref[...] += rows

    rev = lambda c: nc - 1 - c
    return pl.pallas_call(
        body, name=name, grid=(nc,),
        in_specs=[pl.BlockSpec((q, D_SSD), lambda c: (rev(c), 0)),
                  pl.BlockSpec((q, D_XBC), lambda c: (rev(c), 0)),
                  pl.BlockSpec((q, LANE), lambda c: (rev(c), 0)),
                  pl.BlockSpec((8, LANE), lambda c: (0, 0)),
                  pl.BlockSpec((1, N_GROUPS, N_STATE, GROUP_W), lambda c: (rev(c), 0, 0, 0))],
        out_specs=[pl.BlockSpec((q, D_XBC), lambda c: (rev(c), 0)),
                   pl.BlockSpec((q, LANE), lambda c: (rev(c), 0)),
                   pl.BlockSpec((8, LANE), lambda c: (0, 0))],
        out_shape=[jax.ShapeDtypeStruct((seq, D_XBC), F32),
                   jax.ShapeDtypeStruct((seq, LANE), BF16),
                   jax.ShapeDtypeStruct((8, LANE), F32)],
        scratch_shapes=[pltpu.VMEM((N_GROUPS, N_STATE, GROUP_W), F32)],
        compiler_params=_cp(("arbitrary",)))(dy, xbc, proj, par, hprev)


def _gate_norm_bwd(dmix, wout, y, proj, wn, tm, name):
    seq = y.shape[0]

    def body(dm_ref, wo_ref, y_ref, z_ref, w_ref, dy_ref, dz_ref, st_ref):
        d_ys = _dot(dm_ref[...], wo_ref[...], 1, 1)
        rows = []
        for g in range(N_GROUPS):
            sl = slice(g * GROUP_W, (g + 1) * GROUP_W)
            zv = z_ref[:, sl]
            yv = y_ref[:, sl]
            sg = _sigmoid(zv)
            sz = zv * sg
            yz = yv * sz
            r = lax.rsqrt(jnp.mean(yz * yz, axis=-1, keepdims=True) + EPS)
            yn = yz * r
            dv = d_ys[:, sl]
            dyn = dv * w_ref[:, sl]
            dyz = r * (dyn - yn * jnp.mean(dyn * yn, axis=-1, keepdims=True))
            dy_ref[:, sl] = dyz * sz
            dz_ref[:, sl] = (dyz * yv * _silu_grad(zv, sg)).astype(BF16)
            rows.append(jnp.sum(dv * yn, axis=0, keepdims=True))
        rows = jnp.concatenate([jnp.concatenate(rows, axis=1), jnp.zeros((7, D_SSD), F32)], axis=0)

        @pl.when(pl.program_id(0) == 0)
        def _():
            st_ref[...] = rows

        @pl.when(pl.program_id(0) != 0)
        def _():
            st_ref[...] += rows

    tok = pl.BlockSpec((tm, D_SSD), lambda i: (i, 0))
    return pl.pallas_call(
        body, name=name, grid=(seq // tm,),
        in_specs=[tok, pl.BlockSpec((D_SSD, D), lambda i: (0, 0)), tok, tok, pl.BlockSpec((1, D_SSD), lambda i: (0, 0))],
        out_specs=[tok, tok, pl.BlockSpec((8, D_SSD), lambda i: (0, 0))],
        out_shape=[jax.ShapeDtypeStruct((seq, D_SSD), F32), jax.ShapeDtypeStruct((seq, D_SSD), BF16),
                   jax.ShapeDtypeStruct((8, D_SSD), F32)],
        compiler_params=_cp(("arbitrary",)))(dmix, wout, y, proj, wn)


def _pool_counts(t0, rows, w):
    pos = (t0 + 1 + lax.broadcasted_iota(jnp.int32, (rows, 1), 0)).astype(F32)
    return jnp.minimum(pos, float(w))


def _window_means(ext, t0):
    n = ext.shape[0]
    outs = []
    run = ext
    width = 1
    sums = {}
    while width < 16:
        run = run + pltpu.roll(run, width, 0)
        width *= 2
        sums[width] = run
    for g, w in enumerate(POOL_WINDOWS):
        sl = slice(g * POOL_GW, (g + 1) * POOL_GW)
        cnt = _pool_counts(t0, n - HALO, w)
        outs.append(sums[w][HALO:, sl] / cnt - ext[HALO:, sl])
    return outs


def _pool_bwd(dmix, wout, proj, pw, pb, ps, tm, name):
    seq = proj.shape[0]
    ni = seq // tm

    def body(dm_ref, dmn_ref, wo_ref, u_ref, p_ref, pw_ref, pb_ref, ps_ref, du_ref, dw_ref, st_ref):
        i = pl.program_id(0)
        prev = jnp.where(i == 0, 0.0, p_ref[...])
        ext = jnp.concatenate([prev, u_ref[...]], axis=0)
        diffs = _window_means(ext, i * tm)
        n = tm + HALO
        dext = _dot(jnp.concatenate([dm_ref[...], dmn_ref[...]], axis=0), wo_ref[...], 1, 1)
        past_end = jnp.logical_and(i == ni - 1, lax.broadcasted_iota(jnp.int32, (n, 1), 0) >= tm)
        dext = jnp.where(past_end, 0.0, dext)
        b_rows, s_rows = [], []
        for g, w in enumerate(POOL_WINDOWS):
            sl = slice(g * POOL_GW, (g + 1) * POOL_GW)
            wg = pw_ref[g]
            dout = dext[:, sl] * ps_ref[:, sl]
            dcur = dout[:tm]
            pre = _dot(diffs[g].astype(BF16), wg, 1, 0) + pb_ref[:, sl]
            s_rows.append(jnp.sum(dext[:tm, sl] * pre, axis=0, keepdims=True))
            b_rows.append(jnp.sum(dcur, axis=0, keepdims=True))
            dwg = _dot(diffs[g].astype(BF16), dcur.astype(BF16), 0, 0)

            @pl.when(i == 0)
            def _():
                dw_ref[g] = dwg

            @pl.when(i != 0)
            def _():
                dw_ref[g] += dwg

            ddiff = _dot(dout.astype(BF16), wg, 1, 1)
            scaled = ddiff / _pool_counts(i * tm, n, w)
            run = scaled
            width = 1
            while width < w:
                run = run + pltpu.roll(run, n - width, 0)
                width *= 2
            du_ref[:, sl] = (run[:tm] - ddiff[:tm]).astype(BF16)
        rows = jnp.concatenate([jnp.concatenate(b_rows, axis=1), jnp.concatenate(s_rows, axis=1),
                                jnp.zeros((6, D_POOL), F32)], axis=0)

        @pl.when(i == 0)
        def _():
            st_ref[...] = rows

        @pl.when(i != 0)
        def _():
            st_ref[...] += rows

    c0 = COL_U // 1024
    vec = pl.BlockSpec((1, D_POOL), lambda i: (0, 0))
    last = seq // HALO - 1
    return pl.pallas_call(
        body, name=name, grid=(ni,),
        in_specs=[pl.BlockSpec((tm, D), lambda i: (i, 0)),
                  pl.BlockSpec((HALO, D), lambda i: (jnp.minimum((i + 1) * (tm // HALO), last), 0)),
                  pl.BlockSpec((D_POOL, D), lambda i: (1, 0)),
                  pl.BlockSpec((tm, 1024), lambda i: (i, c0)),
                  pl.BlockSpec((HALO, 1024), lambda i: (jnp.maximum(i * (tm // HALO) - 1, 0), c0)),
                  pl.BlockSpec((4, POOL_GW, POOL_GW), lambda i: (0, 0, 0)), vec, vec],
        out_specs=[pl.BlockSpec((tm, D_POOL), lambda i: (i, 0)),
                   pl.BlockSpec((4, POOL_GW, POOL_GW), lambda i: (0, 0, 0)),
                   pl.BlockSpec((8, D_POOL), lambda i: (0, 0))],
        out_shape=[jax.ShapeDtypeStruct((seq, D_POOL), BF16),
                   jax.ShapeDtypeStruct((4, POOL_GW, POOL_GW), F32),
                   jax.ShapeDtypeStruct((8, D_POOL), F32)],
        compiler_params=_cp(("arbitrary",)))(dmix, dmix, wout, proj, proj, pw, pb, ps)


def _mix_heads_out(y, proj, gn_w, pw, pb, ps, wout, x1, mod, wn_next, tm, name):
    seq = y.shape[0]

    def body(y_ref, z_ref, gw_ref, u_ref, p_ref, pw_ref, pb_ref, ps_ref, w_ref, x_ref, mod_ref, wn_ref,
             ys_ref, yp_ref, xo_ref, m_ref, h_ref):
        i = pl.program_id(0)
        ys_parts = []
        for g in range(N_GROUPS):
            sl = slice(g * GROUP_W, (g + 1) * GROUP_W)
            zv = z_ref[:, sl]
            yz = y_ref[:, sl] * (zv * _sigmoid(zv))
            r = lax.rsqrt(jnp.mean(yz * yz, axis=-1, keepdims=True) + EPS)
            ys_parts.append((yz * r * gw_ref[:, sl]).astype(BF16))
        ys = jnp.concatenate(ys_parts, axis=1)
        prev = jnp.where(i == 0, 0.0, p_ref[...])
        diffs = _window_means(jnp.concatenate([prev, u_ref[...]], axis=0), i * tm)
        yp_parts = []
        for g in range(4):
            sl = slice(g * POOL_GW, (g + 1) * POOL_GW)
            out = _dot(diffs[g].astype(BF16), pw_ref[g], 1, 0) + pb_ref[:, sl]
            yp_parts.append((out * ps_ref[:, sl]).astype(BF16))
        yp = jnp.concatenate(yp_parts, axis=1)
        ys_ref[...] = ys
        yp_ref[...] = yp
        mix = _dot(ys, w_ref[0:D_SSD, :], 1, 0) + _dot(yp, w_ref[D_SSD:2 * D_SSD, :], 1, 0)
        m_ref[...] = mix.astype(BF16)
        xo = x_ref[...] + mod_ref[5:6, :] * mix
        xo_ref[...] = xo
        h_ref[...] = _modulated(xo, wn_ref[...], mod_ref, 2)

    c0 = COL_U // 1024
    tok = pl.BlockSpec((tm, D), lambda i: (i, 0))
    vec = pl.BlockSpec((1, D), lambda i: (0, 0))
    return pl.pallas_call(
        body, name=name, grid=(seq // tm,),
        in_specs=[tok, tok, vec,
                  pl.BlockSpec((tm, 1024), lambda i: (i, c0)),
                  pl.BlockSpec((HALO, 1024), lambda i: (jnp.maximum(i * (tm // HALO) - 1, 0), c0)),
                  pl.BlockSpec((4, POOL_GW, POOL_GW), lambda i: (0, 0, 0)), vec, vec,
                  pl.BlockSpec((2 * D_SSD, D), lambda i: (0, 0)), tok,
                  pl.BlockSpec((9, D), lambda i: (0, 0)), vec],
        out_specs=[tok, tok, tok, tok, tok],
        out_shape=[jax.ShapeDtypeStruct((seq, D), BF16), jax.ShapeDtypeStruct((seq, D), BF16),
                   jax.ShapeDtypeStruct((seq, D), F32), jax.ShapeDtypeStruct((seq, D), BF16),
                   jax.ShapeDtypeStruct((seq, D), BF16)],
        compiler_params=_cp(("parallel",)))(y, proj, gn_w, proj, proj, pw, pb, ps, wout, x1, mod, wn_next)


def _mix_bwd_dh(dz, dxbc, du, ddt, w_main, w_dt, x1, dx2, mixb, wn, mod, tm, name):
    seq = x1.shape[0]

    def body(dz_ref, dx_ref, du_ref, ddt_ref, w_ref, wdt_ref, x_ref, dxo_ref, m_ref, wn_ref, mod_ref, o_ref, st_ref, df_ref):
        dh = (_dot(dz_ref[...], w_ref[COL_Z:COL_Z + 1024, :], 1, 0)
              + _dot(dx_ref[...], w_ref[COL_XBC:COL_XBC + D_XBC, :], 1, 0)
              + _dot(du_ref[...], w_ref[COL_U:COL_U + 1024, :], 1, 0)
              + _dot(ddt_ref[...], wdt_ref[...], 1, 0))
        dx = _norm_bwd(dh, x_ref[...], dxo_ref[...], m_ref[...].astype(F32), wn_ref[...],
                       mod_ref[4:5, :], 1.0, st_ref, pl.program_id(0) == 0)
        o_ref[...] = dx
        df_ref[...] = (dx * (FFN_RES * mod_ref[2:3, :])).astype(BF16)

    tok = pl.BlockSpec((tm, D), lambda i: (i, 0))
    return pl.pallas_call(
        body, name=name, grid=(seq // tm,),
        in_specs=[tok, pl.BlockSpec((tm, D_XBC), lambda i: (i, 0)), tok,
                  pl.BlockSpec((tm, LANE), lambda i: (i, 0)),
                  pl.BlockSpec((D_MAIN, D), lambda i: (0, 0)),
                  pl.BlockSpec((LANE, D), lambda i: (0, 0)),
                  tok, tok, tok,
                  pl.BlockSpec((1, D), lambda i: (0, 0)),
                  pl.BlockSpec((9, D), lambda i: (0, 0))],
        out_specs=[tok, pl.BlockSpec((8, D), lambda i: (0, 0)), tok],
        out_shape=[jax.ShapeDtypeStruct((seq, D), F32), jax.ShapeDtypeStruct((8, D), F32),
                   jax.ShapeDtypeStruct((seq, D), BF16)],
        compiler_params=_cp(("arbitrary",)))(dz, dxbc, du, ddt, w_main, w_dt, x1, dx2, mixb, wn, mod)


def _local_step(x, tgt, mod, wff1, w_dt, later_weights, pool_w, vecs, tm):
    seq = x.shape[0]
    in_proj_weights, out_proj_weights, ffn2_weights = later_weights
    h1 = _prenorm(x, vecs["ffn1_norm"], mod, 0, tm, "ffn1_prenorm")
    x1, h2, proj_dt, s1 = _ffn_forward(x, h1, wff1, mod, 0, vecs["mix_norm"], 1, w_dt, tm, "ffn1")
    x1, w_main = in_proj_weights(x1)
    proj = _mm_nt(h2, w_main, tm, 2048, F32, "mix_in_proj")
    xbc, conv_slope = _conv_fwd(proj, vecs["conv_w"], vecs["conv_b"], tm, "mix_conv")
    y, hprev = _ssd_fwd(xbc, proj_dt, vecs["ssd_par"], "mix_ssd")
    y, wout = out_proj_weights(y)
    ys, yp, x2, mixb, h3 = _mix_heads_out(y, proj, vecs["ssd_norm_w"], pool_w, vecs["pool_b"], vecs["pool_scale"],
                                          wout, x1, mod, vecs["ffn2_norm"], tm, "mix_heads_out")
    x2, wff2 = ffn2_weights(x2)
    a3, pg3, pu3 = _ffn_up(h3, wff2["gate"], wff2["up"], tm, FF // 2, "ffn2_up")
    fb3, dx3, df3, st_loss = _ffn_down_loss(a3, *wff2["down"], x2, mod, 8, vecs["final_norm"], tgt, tm, "ffn2_down_loss")
    s3 = (x2, h3, pg3, pu3, a3, fb3)

    dx2, dmix, st3, dw3 = _ffn_backward(dx3, df3, s3, wff2, vecs["ffn2_norm"], mod, 2, (5, 1.0), tm, "ffn2")
    d_wout = (_mm_tn(ys, dmix, 256, seq, "mix_dw_out_ssd"), _mm_tn(yp, dmix, 256, seq, "mix_dw_out_pool"))
    du, d_pool_w, st_pool = _pool_bwd(dmix, wout, proj, pool_w, vecs["pool_b"], vecs["pool_scale"], tm, "mix_pool_bwd")
    dy, dz, st_gn = _gate_norm_bwd(dmix, wout, y, proj, vecs["ssd_norm_w"], tm, "mix_gate_norm_bwd")
    dxbc_act, ddt, st_ssd = _ssd_bwd(dy, xbc, proj_dt, vecs["ssd_par"], hprev, "mix_ssd_bwd")
    dxbc, st_conv = _conv_bwd(dxbc_act, conv_slope, proj, vecs["conv_w"], tm, "mix_conv_bwd")
    dx1, st2, df1 = _mix_bwd_dh(dz, dxbc, du, ddt, w_main, w_dt, x1, dx2, mixb, vecs["mix_norm"], mod, min(tm, 256),
                                "mix_bwd_dh")
    d_win = (_mm_tn_rows([dz, dxbc, du], h2, 256, "mix_dw_in"), _mm_tn(ddt, h2, LANE, seq, "mix_dw_in_dt"))
    dx0, _, st1, dw1 = _ffn_backward(dx1, df1, s1, wff1, vecs["ffn1_norm"], mod, 0, None, tm, "ffn1")
    stats = dict(ffn1=st1, mix=st2, ffn2=st3, loss=st_loss, pool=st_pool, gn=st_gn, ssd=st_ssd, conv=st_conv)
    return dx0, stats, dw1, dw3, d_win, d_wout, d_pool_w


HBM_SPEC = pl.BlockSpec(memory_space=pltpu.HBM)


def _mesh_pos():
    return lax.axis_index("x"), lax.axis_index("y"), lax.axis_index("c")


def _other_chips(x, y):
    return [(1 - x, y), (x, 1 - y), (1 - x, 1 - y)]


def _all_gather(src, regions, name):
    total, cols = src.shape
    assert sum(r for _, r in regions) == total
    body = _all_gather_body(regions, total, False)
    return pl.pallas_call(
        body, name=name,
        out_shape=jax.ShapeDtypeStruct((N_DEV * total, cols), src.dtype),
        in_specs=[HBM_SPEC], out_specs=HBM_SPEC,
        scratch_shapes=[pltpu.SemaphoreType.DMA((7,)), pltpu.SemaphoreType.DMA((7,)), pltpu.SemaphoreType.DMA],
    )(src)


def _all_gather_async(src, regions, name, collective_id):
    total, cols = src.shape
    assert sum(r for _, r in regions) == total
    return pl.kernel(
        _all_gather_body(regions, total, True), name=name,
        out_type=jax.ShapeDtypeStruct((N_DEV * total, cols), src.dtype),
        mesh=plsc.ScalarSubcoreMesh(axis_name="seq", num_cores=1),
        scratch_types=(pltpu.SemaphoreType.DMA((7,)), pltpu.SemaphoreType.DMA((7,)), pltpu.SemaphoreType.DMA),
        compiler_params=pltpu.CompilerParams(collective_id=collective_id))(src)


def _all_gather_body(regions, total, handshake):
    def body(src_ref, out_ref, send_sems, recv_sems, local_sem):
        x, y, c = _mesh_pos()
        me, sibling = (x, y, c), (x, y, 1 - c)
        flip = lambda v, bit: v + bit - 2 * v * bit
        first_chip = (flip(x, 1 - c), flip(y, c))
        second_chip = (flip(x, c), flip(y, 1 - c))
        far_chip = (1 - x, 1 - y)
        if handshake:
            barrier = pltpu.get_barrier_semaphore()
            for peer in (sibling, (*first_chip, c), (*second_chip, c)):
                pl.semaphore_signal(barrier, inc=1, device_id=peer, device_id_type=MESH)
            pl.semaphore_wait(barrier, 3)

        def rows_of(dev, off, rows):
            start = pl.multiple_of(N_DEV * off + (4 * dev[0] + 2 * dev[1] + dev[2]) * rows, 8)
            return out_ref.at[pl.ds(start, rows), :]

        def copies(k, block, to, from_src):
            out = []
            for off, rows in regions:
                dst = rows_of(block, off, rows)
                out.append(pltpu.make_async_remote_copy(
                    src_ref=src_ref.at[pl.ds(off, rows), :] if from_src else dst, dst_ref=dst,
                    send_sem=send_sems.at[k], recv_sem=recv_sems.at[k], device_id=to, device_id_type=MESH))
            return out

        def drain(k):
            whole = out_ref.at[pl.ds(0, total), :]
            return pltpu.make_async_remote_copy(src_ref=whole, dst_ref=whole, send_sem=send_sems.at[k],
                                                recv_sem=recv_sems.at[k], device_id=me, device_id_type=MESH)

        for off, rows in regions:
            pltpu.make_async_copy(src_ref.at[pl.ds(off, rows), :], rows_of(me, off, rows), local_sem).start()
        first = copies(0, me, sibling, True) + copies(1, me, (*first_chip, c), True) + copies(2, me, (*second_chip, c), True)
        for cp in first:
            cp.start()
        drain(1).wait_recv()
        for cp in copies(3, (*first_chip, c), (*second_chip, c), False) + copies(4, (*first_chip, c), sibling, False):
            cp.start()
        drain(2).wait_recv()
        for cp in copies(5, (*second_chip, c), sibling, False):
            cp.start()
        drain(3).wait_recv()
        for cp in copies(6, (*far_chip, c), sibling, False):
            cp.start()
        drain(0).wait_recv()
        for j in range(3):
            drain(4 + j).wait_recv()
        for k in range(7):
            drain(k).wait_send()
        pltpu.make_async_copy(src_ref, out_ref.at[pl.ds(0, total), :], local_sem).wait()

    return body


def _rs_pair(grads, total, name, collective_id):
    cols = grads[0][0].shape[1]
    sent = sum(rows for _, _, rows in grads)
    n = len(grads)

    def body(*refs):
        g_refs, recv_ref, send_sem, recv_sem = refs[:n], refs[n], refs[n + 1], refs[n + 2]
        x, y, c = _mesh_pos()
        sibling = (x, y, 1 - c)
        barrier = pltpu.get_barrier_semaphore()
        pl.semaphore_signal(barrier, inc=1, device_id=sibling, device_id_type=MESH)
        pl.semaphore_wait(barrier, 1)
        for q in range(4):
            for g_ref, (_, off, rows) in zip(g_refs, grads):
                theirs = g_ref.at[pl.ds(pl.multiple_of((2 * q + 1 - c) * rows, 8), rows), :]
                pltpu.make_async_remote_copy(
                    src_ref=theirs, dst_ref=recv_ref.at[q, pl.ds(off, rows), :], send_sem=send_sem, recv_sem=recv_sem,
                    device_id=sibling, device_id_type=MESH).start()
        everything = recv_ref.at[:, pl.ds(0, sent), :]
        whole = pltpu.make_async_remote_copy(src_ref=everything, dst_ref=everything, send_sem=send_sem,
                                             recv_sem=recv_sem, device_id=sibling, device_id_type=MESH)
        whole.wait_send()
        whole.wait_recv()

    return pl.kernel(
        body, name=name, out_type=jax.ShapeDtypeStruct((4, total, cols), F32),
        mesh=plsc.ScalarSubcoreMesh(axis_name="seq", num_cores=1),
        scratch_types=(pltpu.SemaphoreType.DMA, pltpu.SemaphoreType.DMA),
        compiler_params=pltpu.CompilerParams(collective_id=collective_id))(*[g for g, _, _ in grads])


def _pair_sum(grads, from_sibling, pos, name):
    cols = grads[0][0].shape[1]
    n = len(grads)

    def body(pos_ref, *refs):
        mine = pl.program_id(0) == pos_ref[1]
        for i in range(n):
            s = refs[i][...] + refs[n + i][...]
            refs[2 * n + 2 * i + 1][...] = s.astype(BF16)

            @pl.when(mine)
            def _():
                refs[2 * n + 2 * i][...] = s

    in_specs = [pl.BlockSpec((None, None, rows, cols), lambda q, pos_ref: (q, pos_ref[0], 0, 0)) for _, _, rows in grads]
    in_specs += [pl.BlockSpec((None, rows, cols), lambda q, pos_ref, blk=off // rows: (q, blk, 0)) for _, off, rows in grads]
    out_specs, out_shape = [], []
    for _, _, rows in grads:
        out_specs += [pl.BlockSpec((rows, cols), lambda q, pos_ref: (0, 0)),
                      pl.BlockSpec((None, rows, cols), lambda q, pos_ref: (q, 0, 0))]
        out_shape += [jax.ShapeDtypeStruct((rows, cols), F32), jax.ShapeDtypeStruct((4, rows, cols), BF16)]
    outs = pl.pallas_call(
        body, name=name,
        grid_spec=pltpu.PrefetchScalarGridSpec(num_scalar_prefetch=1, grid=(4,), in_specs=in_specs, out_specs=out_specs),
        out_shape=out_shape,
        compiler_params=_cp(("arbitrary",)))(pos, *[g.reshape(4, 2, rows, cols) for g, _, rows in grads],
                                             *[from_sibling] * n)
    return [(outs[2 * i], outs[2 * i + 1]) for i in range(n)]


def _rs_chips(parts, total, name, collective_id):
    cols = parts[0][0].shape[2]
    sent = sum(rows for _, _, rows in parts)
    n = len(parts)

    def body(*refs):
        p_refs, out_ref, send_sems, recv_sems = refs[:n], refs[n], refs[n + 1], refs[n + 2]
        x, y, c = _mesh_pos()
        chips = _other_chips(x, y)
        barrier = pltpu.get_barrier_semaphore()
        for chip in chips:
            pl.semaphore_signal(barrier, inc=1, device_id=(*chip, c), device_id_type=MESH)
        pl.semaphore_wait(barrier, 3)
        for j, chip in enumerate(chips):
            q = 2 * chip[0] + chip[1]
            for p_ref, (_, off, rows) in zip(p_refs, parts):
                pltpu.make_async_remote_copy(
                    src_ref=p_ref.at[q], dst_ref=out_ref.at[j, pl.ds(off, rows), :], send_sem=send_sems.at[j],
                    recv_sem=recv_sems.at[j], device_id=(*chip, c), device_id_type=MESH).start()
        for j, chip in enumerate(chips):
            everything = out_ref.at[j, pl.ds(0, sent), :]
            whole = pltpu.make_async_remote_copy(src_ref=everything, dst_ref=everything, send_sem=send_sems.at[j],
                                                 recv_sem=recv_sems.at[j], device_id=(*chip, c), device_id_type=MESH)
            whole.wait_recv()
            whole.wait_send()

    return pl.kernel(
        body, name=name, out_type=jax.ShapeDtypeStruct((3, total, cols), BF16),
        mesh=plsc.ScalarSubcoreMesh(axis_name="seq", num_cores=1),
        scratch_types=(pltpu.SemaphoreType.DMA((3,)), pltpu.SemaphoreType.DMA((3,))),
        compiler_params=pltpu.CompilerParams(collective_id=collective_id))(*[p for p, _, _ in parts])


def _chip_sum(p, from_chips, off, rows, name):
    cols = p.shape[1]

    def body(p_ref, r_ref, o_ref):
        acc = p_ref[...]
        for j in range(3):
            acc = acc + r_ref[j].astype(F32)
        o_ref[...] = acc

    return pl.pallas_call(
        body, name=name, grid=(1,),
        in_specs=[pl.BlockSpec((rows, cols), lambda i: (0, 0)),
                  pl.BlockSpec((3, rows, cols), lambda i: (0, off // rows, 0))],
        out_specs=pl.BlockSpec((rows, cols), lambda i: (0, 0)),
        out_shape=jax.ShapeDtypeStruct((rows, cols), F32),
        compiler_params=_cp(("arbitrary",)))(p, from_chips)


def _chip_sum_adamw(p, from_chips, off, rows, w, m, v, tr, name):
    cols = p.shape[1]
    c1 = 1.0 - ADAM_B1 ** ADAM_STEP
    c2 = 1.0 - ADAM_B2 ** ADAM_STEP

    def body(p_ref, r_ref, w_ref, m_ref, v_ref, g_ref, d_ref, mo_ref, vo_ref):
        gv = p_ref[...]
        for j in range(3):
            gv = gv + r_ref[j].astype(F32)
        g_ref[...] = gv
        mn = ADAM_B1 * m_ref[...] + (1.0 - ADAM_B1) * gv
        vn = ADAM_B2 * v_ref[...] + (1.0 - ADAM_B2) * (gv * gv)
        mo_ref[...] = mn
        vo_ref[...] = vn
        d_ref[...] = -ADAM_LR * ((mn / c1) / (jnp.sqrt(vn / c2) + ADAM_EPS) + ADAM_WD * w_ref[...])

    tile = pl.BlockSpec((tr, cols), lambda i: (i, 0))
    shape = jax.ShapeDtypeStruct((rows, cols), F32)
    return pl.pallas_call(
        body, name=name, grid=(rows // tr,),
        in_specs=[tile, pl.BlockSpec((3, tr, cols), lambda i: (0, off // tr + i, 0)), tile, tile, tile],
        out_specs=[tile] * 4,
        out_shape=[shape] * 4,
        compiler_params=_cp(("parallel",)))(p, from_chips, w, m, v)


def _row_tile(rows, cap):
    t = min(rows, cap)
    while rows % t or t % 8:
        t -= 8
    return t


def _ada_mod(c_all, w, b, name):
    n = w.shape[1]

    def body(c_ref, w_ref, b_ref, o_ref):
        cv = c_ref[...]
        o_ref[...] = _exact_dot(cv * _sigmoid(cv), w_ref[...]) + b_ref[...]

    return pl.pallas_call(body, name=name, out_shape=jax.ShapeDtypeStruct((N_DEV, n), F32),
                          compiler_params=pltpu.CompilerParams(vmem_limit_bytes=VMEM_LIMIT))(c_all, w, b)


def _ada_grad(c_all, dmod, name):
    n = dmod.shape[1]

    def body(c_ref, d_ref, o_ref):
        cv = c_ref[...]
        o_ref[...] = _dot(cv * _sigmoid(cv), d_ref[...], 0, 0, lax.Precision.HIGHEST)

    return pl.pallas_call(body, name=name, out_shape=jax.ShapeDtypeStruct((D, n), F32),
                          compiler_params=pltpu.CompilerParams(vmem_limit_bytes=VMEM_LIMIT))(c_all, dmod)


def _adamw(w, g, m, v, name):
    rows, cols = w.shape
    tr = _row_tile(rows, 256) if rows % 8 == 0 else rows
    c1 = 1.0 - ADAM_B1 ** ADAM_STEP
    c2 = 1.0 - ADAM_B2 ** ADAM_STEP

    def body(w_ref, g_ref, m_ref, v_ref, d_ref, mo_ref, vo_ref):
        gv = g_ref[...]
        mn = ADAM_B1 * m_ref[...] + (1.0 - ADAM_B1) * gv
        vn = ADAM_B2 * v_ref[...] + (1.0 - ADAM_B2) * (gv * gv)
        mo_ref[...] = mn
        vo_ref[...] = vn
        d_ref[...] = -ADAM_LR * ((mn / c1) / (jnp.sqrt(vn / c2) + ADAM_EPS) + ADAM_WD * w_ref[...])

    spec = pl.BlockSpec((tr, cols), lambda i: (i, 0))
    shape = jax.ShapeDtypeStruct((rows, cols), F32)
    return pl.pallas_call(body, name=name, grid=(rows // tr,), in_specs=[spec] * 4, out_specs=[spec] * 3,
                          out_shape=[shape] * 3, compiler_params=_cp(("parallel",)))(w, g, m, v)


def _sum8_loss(v, loss_row, name):
    rows = v.shape[0] // N_DEV

    def body(v_ref, o_ref, l_ref):
        acc = v_ref[0:rows, :]
        for k in range(1, N_DEV):
            acc = acc + v_ref[k * rows:(k + 1) * rows, :]
        o_ref[...] = acc
        part = jnp.sum(acc[loss_row:loss_row + 8, :], axis=0, keepdims=True)
        l_ref[...] = jnp.broadcast_to(jnp.sum(part, axis=1, keepdims=True), (8, LANE))

    return pl.pallas_call(body, name=name,
                          out_shape=[jax.ShapeDtypeStruct((rows, LANE), F32), jax.ShapeDtypeStruct((8, LANE), F32)],
                          compiler_params=pltpu.CompilerParams(vmem_limit_bytes=VMEM_LIMIT))(v)


WEIGHT_NAMES = ("w_ada", "b_ada", "ffn1_norm", "ffn1_w_gate", "ffn1_w_up", "ffn1_w_down", "mix_norm", "w_in",
                "conv_w", "conv_b", "dt_bias", "a_log", "d_skip", "ssd_norm_w", "pool_w", "pool_b", "pool_scale",
                "w_out", "ffn2_norm", "ffn2_w_gate", "ffn2_w_up", "ffn2_w_down", "final_norm")

FF_SHARD = FF // N_DEV
IN_SHARD = D_IN // N_DEV
MAIN_SHARD = D_MAIN // N_DEV
EDGE = 16
OUT_SHARD = 2 * D_SSD // N_DEV
ADA_SHARD = 9 * D // N_DEV
POOL_SHARD_ROWS = 4 * 32 * POOL_GW // D
GPACK = dict(w_in=(0, MAIN_SHARD), w_out=(512, OUT_SHARD), pool_w=(768, POOL_SHARD_ROWS),
             gate1=(0, FF_SHARD), up1=(352, FF_SHARD), down1=(704, FF_SHARD),
             gate2=(0, FF_SHARD), up2=(352, FF_SHARD), down2=(704, FF_SHARD))
GROUP_ROWS = 1056

SMALL_ROWS = dict(dmod=(0, 72), ffn1_norm=(72, 8), mix_norm=(80, 8), ffn2_norm=(88, 8), final_norm=(96, 8),
                  ssd_norm_w=(104, 8), pool_scale=(112, 8), conv_b=(120, 16), conv_w=(136, 64), pool_b=(200, 8),
                  ssd=(208, 3), loss=(216, 8), w_in_dt=(224, 128), w_in_head=(352, 128), w_in_tail=(480, 128))
SMALL_TOTAL = 608

MAIN_FROM_OWN = (16, 14, 12, 10, 8, 6, 20, 18)
OWN_FROM_MAIN = (16, 18, 20, 22, 24, 26, 12, 14)


def _rows128(v, rows):
    flat = v.reshape(-1)
    return jnp.pad(flat, (0, rows * LANE - flat.shape[0])).reshape(rows, LANE)


def _pad_lanes(v):
    return jnp.pad(v.reshape(-1), (0, LANE - v.size))


def kernel(x, c, w_ada, b_ada, ffn1_norm, ffn1_w_gate, ffn1_w_up, ffn1_w_down, mix_norm, w_in, conv_w, conv_b, dt_bias, a_log, d_skip, ssd_norm_w, pool_w, pool_b, pool_scale, w_out, ffn2_norm, ffn2_w_gate, ffn2_w_up, ffn2_w_down, final_norm, loss_target, m_w_ada, m_b_ada, m_ffn1_norm, m_ffn1_w_gate, m_ffn1_w_up, m_ffn1_w_down, m_mix_norm, m_w_in, m_conv_w, m_conv_b, m_dt_bias, m_a_log, m_d_skip, m_ssd_norm_w, m_pool_w, m_pool_b, m_pool_scale, m_w_out, m_ffn2_norm, m_ffn2_w_gate, m_ffn2_w_up, m_ffn2_w_down, m_final_norm, v_w_ada, v_b_ada, v_ffn1_norm, v_ffn1_w_gate, v_ffn1_w_up, v_ffn1_w_down, v_mix_norm, v_w_in, v_conv_w, v_conv_b, v_dt_bias, v_a_log, v_d_skip, v_ssd_norm_w, v_pool_w, v_pool_b, v_pool_scale, v_w_out, v_ffn2_norm, v_ffn2_w_gate, v_ffn2_w_up, v_ffn2_w_down, v_final_norm):
    given = dict(locals())
    w = {n: given[n] for n in WEIGHT_NAMES}
    m = {n: given["m_" + n] for n in WEIGHT_NAMES}
    v = {n: given["v_" + n] for n in WEIGHT_NAMES}
    mx, my, mc = _mesh_pos()
    me = 4 * mx + 2 * my + mc

    w_in_t = w_in[0].T
    small = jnp.concatenate([c.reshape(-1), conv_w.reshape(-1), pool_b.reshape(-1), pool_w.reshape(-1),
                             w_in_t[0:EDGE].reshape(-1), w_in_t[IN_SHARD - EDGE:IN_SHARD].reshape(-1)])
    small_rows = 536
    gs = _all_gather(_rows128(small, small_rows), [(0, small_rows)], "ag_small").reshape(N_DEV, small_rows * LANE)
    c_all = gs[:, 0:D]
    conv_w_full = gs[:, 1024:2048].reshape(N_DEV, 4, 256).transpose(1, 0, 2).reshape(4, D_XBC)
    pool_b_full = gs[:, 2048:2176].reshape(N_DEV, 4, 32).transpose(1, 0, 2).reshape(1, D_POOL)
    pool_w_full = gs[:, 2176:2176 + 32768].reshape(N_DEV, 4, 32, POOL_GW).transpose(1, 0, 2, 3).reshape(4, POOL_GW, POOL_GW).astype(BF16)
    heads = gs[:, 34944:34944 + EDGE * D].reshape(N_DEV, EDGE, D)
    tails = gs[:, 34944 + EDGE * D:34944 + 2 * EDGE * D].reshape(N_DEV, EDGE, D)

    prev_tail = lax.dynamic_index_in_dim(tails, jnp.maximum(me - 1, 0), axis=0, keepdims=False)
    next_head = lax.dynamic_index_in_dim(heads, jnp.minimum(me + 1, N_DEV - 1), axis=0, keepdims=False)
    first = jnp.asarray(MAIN_FROM_OWN, jnp.int32)[me]

    def window(rows, before, size):
        total = EDGE + IN_SHARD + EDGE
        padded = jnp.pad(rows, ((before, total - before - rows.shape[0]), (0, 0)))
        return lax.dynamic_slice(padded, (first, 0), (size, D))

    main_shard = (window(prev_tail, 0, MAIN_SHARD) + window(w_in_t, EDGE, MAIN_SHARD)
                  + window(next_head, EDGE + IN_SHARD, MAIN_SHARD))
    dt_rows = jnp.concatenate([tails[5], heads[6]], axis=0)[4:4 + N_HEADS]
    w_dt = jnp.pad(dt_rows, ((0, LANE - N_HEADS), (0, 0))).astype(BF16)

    b_ada_cols = lax.dynamic_slice(b_ada, (0, me * ADA_SHARD), (1, ADA_SHARD))
    mod_part = _ada_mod(c_all, w_ada[0], b_ada_cols, "ada_mod")
    mod_all = _all_gather(mod_part, [(0, N_DEV)], "ag_mod").reshape(N_DEV, N_DEV, ADA_SHARD)
    mod = lax.dynamic_index_in_dim(mod_all, me, axis=1, keepdims=False).reshape(9, D)

    packs = (jnp.concatenate([ffn1_w_gate[0].T, ffn1_w_up[0].T], axis=0).astype(BF16),
             ffn1_w_down[0].astype(BF16),
             main_shard.astype(BF16),
             w_out[0].astype(BF16),
             jnp.concatenate([ffn2_w_gate[0].T, ffn2_w_up[0].T, ffn2_w_down[0]], axis=0).astype(BF16))
    packs, _ = lax.optimization_barrier((packs, c_all))
    ffn_regions = [(0, FF_SHARD), (FF_SHARD, FF_SHARD), (2 * FF_SHARD, FF_SHARD)]
    full_a = _all_gather_async(packs[0], ffn_regions[0:2], "ag_weights_ffn1_in", 1)
    full_d = _all_gather_async(packs[1], ffn_regions[0:1], "ag_weights_ffn1_out", 2)
    full_in = _all_gather_async(packs[2], [(0, MAIN_SHARD)], "ag_weights_in_proj", 9)
    full_out = _all_gather_async(packs[3], [(0, OUT_SHARD)], "ag_weights_out_proj", 10)
    full_2 = _all_gather_async(packs[4], ffn_regions, "ag_weights_ffn2", 11)
    wff1 = dict(gate=(full_a, 0), up=(full_a, 1), down=(full_d, 0))

    def in_proj_weights(x1):
        w_i, x1 = lax.optimization_barrier((full_in, x1))
        return x1, w_i

    def out_proj_weights(ys):
        w_o, ys = lax.optimization_barrier((full_out, ys))
        return ys, w_o

    def ffn2_weights(x2):
        w_2, x2 = lax.optimization_barrier((full_2, x2))
        return x2, dict(gate=(w_2, 0), up=(w_2, 1), down=(w_2, 2))

    later_weights = (in_proj_weights, out_proj_weights, ffn2_weights)

    vecs = dict(ffn1_norm=ffn1_norm, mix_norm=mix_norm, ffn2_norm=ffn2_norm, final_norm=final_norm.reshape(1, D),
                conv_w=conv_w_full, conv_b=conv_b, ssd_norm_w=ssd_norm_w, pool_b=pool_b_full, pool_scale=pool_scale,
                ssd_par=jnp.concatenate([_pad_lanes(dt_bias)[None], _pad_lanes(a_log)[None], _pad_lanes(d_skip)[None],
                                         jnp.zeros((5, LANE), F32)], axis=0))
    dx0, st, dw1, dw3, d_win, d_wout, d_pool_w = _local_step(
        x[0], loss_target[0], mod, wff1, w_dt, later_weights, pool_w_full, vecs, min(512, x.shape[1]))

    dwin, d_w_dt = d_win
    dwout = jnp.concatenate(d_wout, axis=0)
    dpool = d_pool_w.reshape(4, N_DEV, 32, POOL_GW).transpose(1, 0, 2, 3).reshape(N_DEV * POOL_SHARD_ROWS, D)
    pos = jnp.stack([mc, 2 * mx + my]).astype(jnp.int32)
    by_key = dict(zip(("gate1", "up1", "down1", "gate2", "up2", "down2", "w_out", "w_in", "pool_w"),
                      (*dw1, *dw3, dwout, dwin, dpool)))
    reduced = {}
    for tag, keys, cid in (("ffn2", ("gate2", "up2", "down2"), 3), ("mix", ("w_in", "w_out", "pool_w"), 5),
                           ("ffn1", ("gate1", "up1", "down1"), 7)):
        grads = [(by_key[k], *GPACK[k]) for k in keys]
        from_sibling = _rs_pair(grads, GROUP_ROWS, f"rs_pair_{tag}", cid)
        pairs = dict(zip(keys, _pair_sum(grads, from_sibling, pos, f"rs_pair_sum_{tag}")))
        from_chips = _rs_chips([(pairs[k][1], *GPACK[k]) for k in keys], GROUP_ROWS, f"rs_chips_{tag}", cid + 1)
        for k in keys:
            reduced[k] = (pairs[k][0], from_chips)

    delta, new_m, new_v, shard_grad = {}, {}, {}, {}
    fused = dict(gate1=("ffn1_w_gate", True, 176), up1=("ffn1_w_up", True, 176), down1=("ffn1_w_down", False, 176),
                 gate2=("ffn2_w_gate", True, 176), up2=("ffn2_w_up", True, 176), down2=("ffn2_w_down", False, 176),
                 w_out=("w_out", False, 128), pool_w=("pool_w", False, POOL_SHARD_ROWS))
    for k, (n, is_transposed, tr) in fused.items():
        shp = w[n].shape
        rows = GPACK[k][1]
        view = (lambda t: t[0].T) if is_transposed else (lambda t: t.reshape(rows, D))
        back = (lambda t: t.T[None]) if is_transposed else (lambda t: t.reshape(shp))
        g_, d_, m_, v_ = _chip_sum_adamw(reduced[k][0], reduced[k][1], *GPACK[k], view(w[n]), view(m[n]), view(v[n]),
                                         tr, f"adamw_{n}")
        shard_grad[n], delta[n], new_m[n], new_v[n] = back(g_), back(d_), back(m_), back(v_)
    g_main = _chip_sum(reduced["w_in"][0], reduced["w_in"][1], *GPACK["w_in"], "rs_chip_sum_w_in")

    dmod = jnp.concatenate([st["ffn1"][0:3], st["mix"][0:3], st["ffn2"][0:3]], axis=0)
    sg = jnp.concatenate([
        dmod.reshape(-1), st["ffn1"][3], st["mix"][3], st["ffn2"][3], st["loss"][0], st["gn"][0], st["pool"][1],
        st["conv"][4], st["conv"][0:4].reshape(-1), st["pool"][0], st["ssd"][0:3].reshape(-1),
        jnp.zeros((5 * LANE,), F32), st["loss"][1],
        d_w_dt[0:N_HEADS].reshape(-1), g_main[0:EDGE].reshape(-1), g_main[MAIN_SHARD - EDGE:MAIN_SHARD].reshape(-1)])
    sg_all = _all_gather(sg.reshape(SMALL_TOTAL, LANE), [(0, SMALL_TOTAL)], "ag_small_grads")
    tot, loss_b = _sum8_loss(sg_all, SMALL_ROWS["loss"][0], "small_sum")
    loss = loss_b[0, 0]
    per_dev = sg_all.reshape(N_DEV, SMALL_TOTAL * LANE)
    dmod_all = per_dev[:, 0:9 * D]
    g_w_ada = _ada_grad(c_all, lax.dynamic_slice(dmod_all, (0, me * ADA_SHARD), (N_DEV, ADA_SHARD)), "ada_grad")

    def edge_rows(k):
        off, n = SMALL_ROWS[k]
        return per_dev[:, off * LANE:(off + n) * LANE].reshape(N_DEV, EDGE, D)

    g_dt = tot[SMALL_ROWS["w_in_dt"][0]:SMALL_ROWS["w_in_dt"][0] + SMALL_ROWS["w_in_dt"][1]].reshape(N_HEADS, D)
    before = jnp.where(me == 6, g_dt, edge_rows("w_in_tail")[6])
    after = jnp.where(me == 5, g_dt, lax.dynamic_index_in_dim(edge_rows("w_in_head"), jnp.minimum(me + 1, N_DEV - 1),
                                                               axis=0, keepdims=False))
    first_own = jnp.asarray(OWN_FROM_MAIN, jnp.int32)[me]

    def own_window(rows, lead):
        total = EDGE + MAIN_SHARD + EDGE
        padded = jnp.pad(rows, ((lead, total - lead - rows.shape[0]), (0, 0)))
        return lax.dynamic_slice(padded, (first_own, 0), (IN_SHARD, D))

    g_win_t = own_window(before, 0) + own_window(g_main, EDGE) + own_window(after, EDGE + MAIN_SHARD)

    def tot_rows(k):
        off, n = SMALL_ROWS[k]
        return tot[off:off + n].reshape(-1)

    g_conv_w = lax.dynamic_slice(tot_rows("conv_w").reshape(4, D_XBC), (0, me * 256), (4, 256))
    g_pool_b = lax.dynamic_slice(tot_rows("pool_b").reshape(4, POOL_GW), (0, me * 32), (4, 32))
    g_ssd = tot_rows("ssd").reshape(3, LANE)
    grad = {
        "w_ada": g_w_ada[None], "b_ada": tot_rows("dmod").reshape(1, 9 * D),
        "ffn1_norm": tot_rows("ffn1_norm")[None], "mix_norm": tot_rows("mix_norm")[None],
        "ffn2_norm": tot_rows("ffn2_norm")[None], "final_norm": tot_rows("final_norm"),
        "ssd_norm_w": tot_rows("ssd_norm_w")[None], "pool_scale": tot_rows("pool_scale")[None],
        "conv_b": tot_rows("conv_b")[None], "conv_w": g_conv_w[None], "pool_b": g_pool_b[None],
        "dt_bias": g_ssd[0:1, 0:N_HEADS], "a_log": g_ssd[1:2, 0:N_HEADS], "d_skip": g_ssd[2:3, 0:N_HEADS],
        "w_in": g_win_t.T[None], **shard_grad,
    }

    d_, m_, v_ = _adamw(w_ada[0], g_w_ada, m_w_ada[0], v_w_ada[0], "adamw_w_ada")
    delta["w_ada"], new_m["w_ada"], new_v["w_ada"] = d_[None], m_[None], v_[None]
    d_, m_, v_ = _adamw(w_in[0].T, g_win_t, m_w_in[0].T, v_w_in[0].T, "adamw_w_in")
    delta["w_in"], new_m["w_in"], new_v["w_in"] = d_.T[None], m_.T[None], v_.T[None]
    big = ("w_ada", "w_in") + tuple(n for n, _, _ in fused.values())
    small_names = [n for n in WEIGHT_NAMES if n not in big]
    sizes = [LANE if w[n].size < LANE else w[n].size for n in small_names]
    small_rows_adam = -(-sum(sizes) // (8 * LANE)) * 8

    def pack_small(t):
        return _rows128(jnp.concatenate([_pad_lanes(t[n]) if t[n].size < LANE else t[n].reshape(-1) for n in small_names]),
                        small_rows_adam)

    d_s, m_s, v_s = _adamw(pack_small(w), pack_small(grad), pack_small(m), pack_small(v), "adamw_small")
    off = 0
    for n, size in zip(small_names, sizes):
        for res, packed in ((delta, d_s), (new_m, m_s), (new_v, v_s)):
            res[n] = packed.reshape(-1)[off:off + w[n].size].reshape(w[n].shape)
        off += size

    return (loss, dx0[None], *[grad[n] for n in WEIGHT_NAMES], *[delta[n] for n in WEIGHT_NAMES],
            *[new_m[n] for n in WEIGHT_NAMES], *[new_v[n] for n in WEIGHT_NAMES])
```

```python
import jax
import jax.numpy as jnp
from jax import lax
from jax.experimental import pallas as pl
from jax.experimental.pallas import tpu as pltpu
from jax.experimental.pallas import tpu_sc as plsc

F32 = jnp.float32
BF16 = jnp.bfloat16
MESH = pl.DeviceIdType.MESH

N_DEV = 8
D = 1024
FF = 2816
D_SSD = 1024
N_HEADS = 16
HEAD_DIM = 64
N_GROUPS = 4
N_STATE = 128
CHUNK = 128
GROUP_W = D_SSD // N_GROUPS
D_XBC = D_SSD + 2 * N_GROUPS * N_STATE
D_POOL = 1024
POOL_WINDOWS = (2, 4, 8, 16)
POOL_GW = 256
D_IN = 4112
D_MAIN = 4096
COL_Z, COL_XBC, COL_U = 0, 1024, 3072
EPS = 1e-6
FFN_RES = 0.5
LANE = 128
HALO = 16

ADAM_LR, ADAM_B1, ADAM_B2, ADAM_EPS, ADAM_WD, ADAM_STEP = 0.001, 0.9, 0.999, 1e-08, 0.01, 10

VMEM_LIMIT = 56 << 20


def _cp(sem):
    return pltpu.CompilerParams(dimension_semantics=sem, vmem_limit_bytes=VMEM_LIMIT)


def _dot(a, b, ca, cb, prec=None):
    return lax.dot_general(a, b, (((ca,), (cb,)), ((), ())), precision=prec,
                           preferred_element_type=F32)


def _exact_dot(a, b):
    return _dot(a, b, 1, 0, lax.Precision.HIGHEST)


def _sigmoid(v):
    return 1.0 / (1.0 + jnp.exp(-v))


def _silu_grad(v, sg):
    return sg * (1.0 + v * (1.0 - sg))


def _mm_nt(a, bt, tm, tn, out_dtype, name):
    m, k = a.shape
    n = bt.shape[0]

    def body(a_ref, b_ref, o_ref):
        o_ref[...] = _dot(a_ref[...], b_ref[...], 1, 1).astype(out_dtype)

    return pl.pallas_call(
        body, name=name, grid=(n // tn, m // tm),
        in_specs=[pl.BlockSpec((tm, k), lambda j, i: (i, 0)),
                  pl.BlockSpec((tn, k), lambda j, i: (j, 0))],
        out_specs=pl.BlockSpec((tm, tn), lambda j, i: (i, j)),
        out_shape=jax.ShapeDtypeStruct((m, n), out_dtype),
        compiler_params=_cp(("parallel", "parallel")))(a, bt)


def _mm_tn(a, b, tm, tk, name):
    kk, m = a.shape
    n = b.shape[1]
    nk = kk // tk
    if nk == 1:
        def whole(a_ref, b_ref, o_ref):
            o_ref[...] = _dot(a_ref[...], b_ref[...], 0, 0)

        return pl.pallas_call(
            whole, name=name, grid=(m // tm,),
            in_specs=[pl.BlockSpec((kk, tm), lambda i: (0, i)),
                      pl.BlockSpec((kk, n), lambda i: (0, 0))],
            out_specs=pl.BlockSpec((tm, n), lambda i: (i, 0)),
            out_shape=jax.ShapeDtypeStruct((m, n), F32),
            compiler_params=_cp(("parallel",)))(a, b)

    def body(a_ref, b_ref, o_ref, acc):
        k = pl.program_id(1)

        @pl.when(k == 0)
        def _():
            acc[...] = jnp.zeros_like(acc)

        acc[...] += _dot(a_ref[...], b_ref[...], 0, 0)

        @pl.when(k == nk - 1)
        def _():
            o_ref[...] = acc[...]

    return pl.pallas_call(
        body, name=name, grid=(m // tm, nk),
        in_specs=[pl.BlockSpec((tk, tm), lambda i, k: (k, i)),
                  pl.BlockSpec((tk, n), lambda i, k: (k, 0))],
        out_specs=pl.BlockSpec((tm, n), lambda i, k: (i, 0)),
        out_shape=jax.ShapeDtypeStruct((m, n), F32),
        scratch_shapes=[pltpu.VMEM((tm, n), F32)],
        compiler_params=_cp(("parallel", "arbitrary")))(a, b)


def _mm_tn_rows(parts, b, tm, name):
    kk, n = b.shape
    blocks = [a.shape[1] // tm for a in parts]
    starts = [sum(blocks[:p]) for p in range(len(parts))]

    def body(*refs):
        a_refs, b_ref, o_ref = refs[:len(parts)], refs[len(parts)], refs[len(parts) + 1]
        i = pl.program_id(0)
        for p, a_ref in enumerate(a_refs):
            @pl.when(jnp.logical_and(i >= starts[p], i < starts[p] + blocks[p]))
            def _():
                o_ref[...] = _dot(a_ref[...], b_ref[...], 0, 0)

    def part_spec(p):
        return pl.BlockSpec((kk, tm), lambda i: (0, jnp.clip(i - starts[p], 0, blocks[p] - 1)))

    return pl.pallas_call(
        body, name=name, grid=(sum(blocks),),
        in_specs=[part_spec(p) for p in range(len(parts))] + [pl.BlockSpec((kk, n), lambda i: (0, 0))],
        out_specs=pl.BlockSpec((tm, n), lambda i: (i, 0)),
        out_shape=jax.ShapeDtypeStruct((sum(blocks) * tm, n), F32),
        compiler_params=_cp(("parallel",)))(*parts, b)


def _modulated(xv, wn, mod_ref, k):
    r = lax.rsqrt(jnp.mean(xv * xv, axis=-1, keepdims=True) + EPS)
    hn = xv * r * wn
    return (hn * (1.0 + mod_ref[3 * k + 1:3 * k + 2, :]) + mod_ref[3 * k:3 * k + 1, :]).astype(BF16)


def _prenorm(x, wn, mod, k, tm, name):
    seq = x.shape[0]

    def body(x_ref, wn_ref, mod_ref, h_ref):
        h_ref[...] = _modulated(x_ref[...], wn_ref[...], mod_ref, k)

    return pl.pallas_call(
        body, name=name, grid=(seq // tm,),
        in_specs=[pl.BlockSpec((tm, D), lambda i: (i, 0)),
                  pl.BlockSpec((1, D), lambda i: (0, 0)),
                  pl.BlockSpec((9, D), lambda i: (0, 0))],
        out_specs=pl.BlockSpec((tm, D), lambda i: (i, 0)),
        out_shape=jax.ShapeDtypeStruct((seq, D), BF16),
        compiler_params=_cp(("parallel",)))(x, wn, mod)


def _norm_bwd(dh, xv, dxo, branch, wn, sc, res, stats_ref, first):
    r = lax.rsqrt(jnp.mean(xv * xv, axis=-1, keepdims=True) + EPS)
    xn = xv * r
    dhn = dh * (1.0 + sc)
    dxn = dhn * wn
    dx = dxo + r * (dxn - xn * jnp.mean(dxn * xn, axis=-1, keepdims=True))
    rows = jnp.concatenate([
        jnp.sum(dh, axis=0, keepdims=True),
        jnp.sum(dh * (xn * wn), axis=0, keepdims=True),
        jnp.sum(branch * dxo, axis=0, keepdims=True) * res,
        jnp.sum(dhn * xn, axis=0, keepdims=True),
        jnp.zeros((4, D), F32)], axis=0)

    @pl.when(first)
    def _():
        stats_ref[...] = rows

    @pl.when(jnp.logical_not(first))
    def _():
        stats_ref[...] += rows

    return dx


def _ffn_up(h, wg, wu, tm, tn, name):
    seq = h.shape[0]
    nj = FF // tn

    def body(h_ref, wg_ref, wu_ref, a_ref, pg_ref, pu_ref):
        hv = h_ref[...]
        g = _dot(hv, wg_ref[...], 1, 1)
        u = _dot(hv, wu_ref[...], 1, 1)
        sg = _sigmoid(g)
        s = g * sg
        a_ref[...] = (s * u).astype(BF16)
        pg_ref[...] = (u * _silu_grad(g, sg)).astype(BF16)
        pu_ref[...] = s.astype(BF16)

    act = pl.BlockSpec((tm, tn), lambda j, i: (i, j))
    return pl.pallas_call(
        body, name=name, grid=(nj, seq // tm),
        in_specs=[pl.BlockSpec((tm, D), lambda j, i: (i, 0)),
                  pl.BlockSpec((tn, D), lambda j, i: (wg[1] * nj + j, 0)),
                  pl.BlockSpec((tn, D), lambda j, i: (wu[1] * nj + j, 0))],
        out_specs=[act, act, act],
        out_shape=[jax.ShapeDtypeStruct((seq, FF), BF16)] * 3,
        compiler_params=_cp(("parallel", "parallel")))(h, wg[0], wu[0])


def _ffn_down(a, w, blk, x, mod, grow, wn_next, k_next, w_dt, tm, name):
    seq = a.shape[0]

    def body(a_ref, w_ref, x_ref, mod_ref, wn_ref, wdt_ref, xo_ref, f_ref, h_ref, dt_ref):
        f = _dot(a_ref[...], w_ref[...], 1, 0)
        f_ref[...] = f.astype(BF16)
        xo = x_ref[...] + (FFN_RES * mod_ref[grow:grow + 1, :]) * f
        xo_ref[...] = xo
        h = _modulated(xo, wn_ref[...], mod_ref, k_next)
        h_ref[...] = h
        dt_ref[...] = _dot(h, wdt_ref[...], 1, 1)

    tok = pl.BlockSpec((tm, D), lambda i: (i, 0))
    return pl.pallas_call(
        body, name=name, grid=(seq // tm,),
        in_specs=[pl.BlockSpec((tm, FF), lambda i: (i, 0)),
                  pl.BlockSpec((FF, D), lambda i: (blk, 0)),
                  tok,
                  pl.BlockSpec((9, D), lambda i: (0, 0)),
                  pl.BlockSpec((1, D), lambda i: (0, 0)),
                  pl.BlockSpec((LANE, D), lambda i: (0, 0))],
        out_specs=[tok, tok, tok, pl.BlockSpec((tm, LANE), lambda i: (i, 0))],
        out_shape=[jax.ShapeDtypeStruct((seq, D), F32), jax.ShapeDtypeStruct((seq, D), BF16),
                   jax.ShapeDtypeStruct((seq, D), BF16), jax.ShapeDtypeStruct((seq, LANE), F32)],
        compiler_params=_cp(("parallel",)))(a, w, x, mod, wn_next, w_dt)


def _ffn_down_loss(a, w, blk, x, mod, grow, wf, tgt, tm, name):
    seq = a.shape[0]

    def body(a_ref, w_ref, x_ref, mod_ref, wf_ref, t_ref, f_ref, dx_ref, df_ref, st_ref):
        f = _dot(a_ref[...], w_ref[...], 1, 0)
        f_ref[...] = f.astype(BF16)
        xv = x_ref[...] + (FFN_RES * mod_ref[grow:grow + 1, :]) * f
        wv = wf_ref[...]
        r = lax.rsqrt(jnp.mean(xv * xv, axis=-1, keepdims=True) + EPS)
        xn = xv * r
        e = xn * wv - t_ref[...]
        dy = e * (1.0 / D)
        dxn = dy * wv
        dx = r * (dxn - xn * jnp.mean(dxn * xn, axis=-1, keepdims=True))
        dx_ref[...] = dx
        df_ref[...] = (dx * (FFN_RES * mod_ref[grow:grow + 1, :])).astype(BF16)
        rows = jnp.concatenate([
            jnp.sum(dy * xn, axis=0, keepdims=True),
            jnp.sum(e * e, axis=0, keepdims=True) * (0.5 / D),
            jnp.zeros((6, D), F32)], axis=0)

        @pl.when(pl.program_id(0) == 0)
        def _():
            st_ref[...] = rows

        @pl.when(pl.program_id(0) != 0)
        def _():
            st_ref[...] += rows

    tok = pl.BlockSpec((tm, D), lambda i: (i, 0))
    return pl.pallas_call(
        body, name=name, grid=(seq // tm,),
        in_specs=[pl.BlockSpec((tm, FF), lambda i: (i, 0)),
                  pl.BlockSpec((FF, D), lambda i: (blk, 0)),
                  tok,
                  pl.BlockSpec((9, D), lambda i: (0, 0)),
                  pl.BlockSpec((1, D), lambda i: (0, 0)),
                  tok],
        out_specs=[tok, tok, tok, pl.BlockSpec((8, D), lambda i: (0, 0))],
        out_shape=[jax.ShapeDtypeStruct((seq, D), BF16), jax.ShapeDtypeStruct((seq, D), F32),
                   jax.ShapeDtypeStruct((seq, D), BF16), jax.ShapeDtypeStruct((8, D), F32)],
        compiler_params=_cp(("arbitrary",)))(a, w, x, mod, wf, tgt)


def _ffn_bwd_da(df, w, blk, pg, pu, tm, tn, name):
    seq = df.shape[0]
    nj = FF // tn

    def body(df_ref, w_ref, pg_ref, pu_ref, dg_ref, du_ref):
        da = _dot(df_ref[...], w_ref[...], 1, 1)
        dg_ref[...] = (da * pg_ref[...].astype(F32)).astype(BF16)
        du_ref[...] = (da * pu_ref[...].astype(F32)).astype(BF16)

    act = pl.BlockSpec((tm, tn), lambda j, i: (i, j))
    return pl.pallas_call(
        body, name=name, grid=(nj, seq // tm),
        in_specs=[pl.BlockSpec((tm, D), lambda j, i: (i, 0)),
                  pl.BlockSpec((tn, D), lambda j, i: (blk * nj + j, 0)),
                  act, act],
        out_specs=[act, act],
        out_shape=[jax.ShapeDtypeStruct((seq, FF), BF16)] * 2,
        compiler_params=_cp(("parallel", "parallel")))(df, w, pg, pu)


def _ffn_bwd_dh(dg, du, wg, wu, x, dxo, fb, wn, mod, k, nxt, tm, name):
    seq = x.shape[0]

    def body(dg_ref, du_ref, wg_ref, wu_ref, x_ref, dxo_ref, f_ref, wn_ref, mod_ref, dx_ref, st_ref, *rest):
        dh = _dot(dg_ref[...], wg_ref[...], 1, 0) + _dot(du_ref[...], wu_ref[...], 1, 0)
        dx = _norm_bwd(dh, x_ref[...], dxo_ref[...], f_ref[...].astype(F32), wn_ref[...],
                       mod_ref[3 * k + 1:3 * k + 2, :], FFN_RES, st_ref, pl.program_id(0) == 0)
        dx_ref[...] = dx
        if nxt is not None:
            rest[0][...] = (dx * (nxt[1] * mod_ref[nxt[0]:nxt[0] + 1, :])).astype(BF16)

    tok = pl.BlockSpec((tm, D), lambda i: (i, 0))
    act = pl.BlockSpec((tm, FF), lambda i: (i, 0))
    return pl.pallas_call(
        body, name=name, grid=(seq // tm,),
        in_specs=[act, act,
                  pl.BlockSpec((FF, D), lambda i: (wg[1], 0)),
                  pl.BlockSpec((FF, D), lambda i: (wu[1], 0)),
                  tok, tok, tok,
                  pl.BlockSpec((1, D), lambda i: (0, 0)),
                  pl.BlockSpec((9, D), lambda i: (0, 0))],
        out_specs=[tok, pl.BlockSpec((8, D), lambda i: (0, 0))] + ([tok] if nxt is not None else []),
        out_shape=[jax.ShapeDtypeStruct((seq, D), F32), jax.ShapeDtypeStruct((8, D), F32)]
        + ([jax.ShapeDtypeStruct((seq, D), BF16)] if nxt is not None else []),
        compiler_params=_cp(("arbitrary",)))(dg, du, wg[0], wu[0], x, dxo, fb, wn, mod)


def _ffn_forward(x, h, w, mod, k, wn_next, k_next, w_dt, tm, tag):
    a, pg, pu = _ffn_up(h, w["gate"], w["up"], tm, FF // 2, f"{tag}_up")
    xo, fb, h_next, dt_next = _ffn_down(a, *w["down"], x, mod, 3 * k + 2, wn_next, k_next, w_dt, tm, f"{tag}_down")
    return xo, h_next, dt_next, (x, h, pg, pu, a, fb)


def _ffn_backward(dxo, df, saved, w, wn, mod, k, nxt, tm, tag):
    x, h, pg, pu, a, fb = saved
    dg, du = _ffn_bwd_da(df, *w["down"], pg, pu, tm, FF // 2, f"{tag}_bwd_da")
    seq = x.shape[0]
    d_gate_t = _mm_tn(dg, h, 256, seq, f"{tag}_dw_gate")
    d_up_t = _mm_tn(du, h, 256, seq, f"{tag}_dw_up")
    d_down = _mm_tn(a, df, 256, seq, f"{tag}_dw_down")
    dws, dg, du = lax.optimization_barrier(((d_gate_t, d_up_t, d_down), dg, du))
    outs = _ffn_bwd_dh(dg, du, w["gate"], w["up"], x, dxo, fb, wn, mod, k, nxt, min(tm, 256), f"{tag}_bwd_dh")
    return outs[0], (outs[2] if nxt is not None else None), outs[1], dws


def _prev_rows(tm, col):
    return pl.BlockSpec((HALO, 1024), lambda i, j: (jnp.maximum(i * (tm // HALO) - 1, 0), col + j))


def _conv_pre(ext, cw, cb, rows):
    pre = cb + cw[3:4, :] * ext
    for s in (1, 2, 3):
        pre = pre + cw[3 - s:4 - s, :] * pltpu.roll(ext, s, 0)
    return pre[HALO:HALO + rows]


def _conv_fwd(proj, cw, cb, tm, name):
    seq = proj.shape[0]

    def body(x_ref, p_ref, cw_ref, cb_ref, o_ref, g_ref):
        prev = jnp.where(pl.program_id(0) == 0, 0.0, p_ref[...])
        ext = jnp.concatenate([prev, x_ref[...]], axis=0)
        pre = _conv_pre(ext, cw_ref[...], cb_ref[...], tm)
        sg = _sigmoid(pre)
        o_ref[...] = pre * sg
        g_ref[...] = _silu_grad(pre, sg).astype(BF16)

    c0 = COL_XBC // 1024
    out = pl.BlockSpec((tm, 1024), lambda i, j: (i, j))
    return pl.pallas_call(
        body, name=name, grid=(seq // tm, 2),
        in_specs=[pl.BlockSpec((tm, 1024), lambda i, j: (i, c0 + j)),
                  _prev_rows(tm, c0),
                  pl.BlockSpec((4, 1024), lambda i, j: (0, j)),
                  pl.BlockSpec((1, 1024), lambda i, j: (0, j))],
        out_specs=[out, out],
        out_shape=[jax.ShapeDtypeStruct((seq, D_XBC), F32), jax.ShapeDtypeStruct((seq, D_XBC), BF16)],
        compiler_params=_cp(("parallel", "parallel")))(proj, proj, cw, cb)


def _conv_bwd(dact, slope, proj, cw, tm, name):
    seq = proj.shape[0]
    ni = seq // tm

    def body(d_ref, dn_ref, s_ref, sn_ref, x_ref, p_ref, cw_ref, o_ref, st_ref):
        i = pl.program_id(1)
        cwv = cw_ref[...]
        prev = jnp.where(i == 0, 0.0, p_ref[...])
        ext = jnp.concatenate([prev, x_ref[...]], axis=0)
        dnext = jnp.where(i == ni - 1, 0.0, dn_ref[...] * sn_ref[...].astype(F32))
        dpre = jnp.concatenate([d_ref[...] * s_ref[...].astype(F32), dnext], axis=0)
        n = tm + HALO
        dx = cwv[3:4, :] * dpre
        for s in (1, 2, 3):
            dx = dx + cwv[3 - s:4 - s, :] * pltpu.roll(dpre, n - s, 0)
        o_ref[...] = dx[:tm].astype(BF16)
        dcur = dpre[:tm]
        rows = [jnp.sum(dcur * pltpu.roll(ext, 3 - k, 0)[HALO:HALO + tm], axis=0, keepdims=True) for k in range(3)]
        rows.append(jnp.sum(dcur * ext[HALO:HALO + tm], axis=0, keepdims=True))
        rows.append(jnp.sum(dcur, axis=0, keepdims=True))
        rows.append(jnp.zeros((3, 1024), F32))
        rows = jnp.concatenate(rows, axis=0)

        @pl.when(i == 0)
        def _():
            st_ref[...] = rows

        @pl.when(i != 0)
        def _():
            st_ref[...] += rows

    c0 = COL_XBC // 1024
    cur = pl.BlockSpec((tm, 1024), lambda j, i: (i, j))
    nxt = pl.BlockSpec((HALO, 1024), lambda j, i: (jnp.minimum((i + 1) * (tm // HALO), seq // HALO - 1), j))
    return pl.pallas_call(
        body, name=name, grid=(2, ni),
        in_specs=[cur, nxt, cur, nxt,
                  pl.BlockSpec((tm, 1024), lambda j, i: (i, c0 + j)),
                  pl.BlockSpec((HALO, 1024), lambda j, i: (jnp.maximum(i * (tm // HALO) - 1, 0), c0 + j)),
                  pl.BlockSpec((4, 1024), lambda j, i: (0, j))],
        out_specs=[cur, pl.BlockSpec((8, 1024), lambda j, i: (0, j))],
        out_shape=[jax.ShapeDtypeStruct((seq, D_XBC), BF16), jax.ShapeDtypeStruct((8, D_XBC), F32)],
        compiler_params=_cp(("parallel", "arbitrary")))(dact, dact, slope, slope, proj, proj, cw)


def _bf16_parts(x, n):
    parts, rest = [], x
    for _ in range(n):
        p = rest.astype(BF16)
        parts.append(p)
        rest = rest - p.astype(F32)
    return parts


def _pick(x, sel, n):
    m = x.shape[0]
    prod = _dot(jnp.concatenate(_bf16_parts(x, n), axis=0), sel, 1, 0)
    acc = prod[0:m]
    for i in range(1, n):
        acc = acc + prod[i * m:(i + 1) * m]
    return acc


def _running(mask, x, n):
    k = x.shape[1]
    prod = _dot(mask, jnp.concatenate(_bf16_parts(x, n), axis=1), 1, 0)
    acc = prod[:, 0:k]
    for i in range(1, n):
        acc = acc + prod[:, i * k:(i + 1) * k]
    return acc


def _head_expand():
    r = lax.broadcasted_iota(jnp.int32, (LANE, D_SSD), 0)
    c = lax.broadcasted_iota(jnp.int32, (LANE, D_SSD), 1)
    return (c // HEAD_DIM == r).astype(BF16)


def _head_reduce():
    r = lax.broadcasted_iota(jnp.int32, (D_SSD, LANE), 0)
    c = lax.broadcasted_iota(jnp.int32, (D_SSD, LANE), 1)
    return (r // HEAD_DIM == c).astype(BF16)


def _ssd_common(dtr, par):
    q = CHUNK
    v = dtr + par[0:1, :]
    dt = jnp.maximum(v, 0.0) + jnp.log(1.0 + jnp.exp(-jnp.abs(v)))
    a = -jnp.exp(par[1:2, :])
    adt = dt * a
    li = lax.broadcasted_iota(jnp.int32, (q, q), 0)
    si = lax.broadcasted_iota(jnp.int32, (q, q), 1)
    causal = li >= si
    acs = _running(causal.astype(BF16), adt, 3)
    expand = _head_expand()
    both_l = _pick(jnp.concatenate([dt, acs], axis=0), expand, 3)
    dt_l, acs_l = both_l[0:q], both_l[q:2 * q]
    dskip_l = _pick(jnp.broadcast_to(par[2:3, :], (16, LANE)), expand, 3)[0:1, :]
    last_l = acs_l[q - 1:q, :]
    return dict(v=v, dt=dt, a=a, acs=acs, acs_t=acs.T, causal=causal, dt_l=dt_l, acs_l=acs_l,
                ea_l=jnp.exp(acs_l), ds_l=jnp.exp(last_l - acs_l), cd_l=jnp.exp(last_l), dskip_l=dskip_l)


def _decay(cm, h):
    seg = cm["acs"][:, h:h + 1] - cm["acs_t"][h:h + 1, :]
    return jnp.exp(jnp.where(cm["causal"], seg, -jnp.inf))


def _lane_mask(r):
    lane = lax.broadcasted_iota(jnp.int32, (1, GROUP_W), 1)
    return lane // HEAD_DIM == r


def _ssd_fwd(xbc, proj, par, name):
    seq = xbc.shape[0]
    nc = seq // CHUNK
    q = CHUNK

    def body(x_ref, dt_ref, par_ref, y_ref, hp_ref, state):
        @pl.when(pl.program_id(0) == 0)
        def _():
            state[...] = jnp.zeros_like(state)

        cm = _ssd_common(dt_ref[...], par_ref[...])
        for g in range(N_GROUPS):
            lo = g * GROUP_W
            xs = x_ref[:, lo:lo + GROUP_W]
            bm = x_ref[:, D_SSD + g * N_STATE:D_SSD + (g + 1) * N_STATE].astype(BF16)
            cmat = x_ref[:, D_SSD + N_GROUPS * N_STATE + g * N_STATE:D_SSD + N_GROUPS * N_STATE + (g + 1) * N_STATE].astype(BF16)
            xdt = xs * cm["dt_l"][:, lo:lo + GROUP_W]
            xdt_b = xdt.astype(BF16)
            cb = _dot(cmat, bm, 1, 1)
            scores = jnp.concatenate([(cb * _decay(cm, 4 * g + r)).astype(BF16) for r in range(4)], axis=0)
            yd_heads = _dot(scores, xdt_b, 1, 0)
            yd = yd_heads[0:q]
            for r in range(1, 4):
                yd = jnp.where(_lane_mask(r), yd_heads[r * q:(r + 1) * q], yd)
            hg = state[g]
            hp_ref[0, g] = hg
            yo = _dot(cmat, hg.astype(BF16), 1, 0) * cm["ea_l"][:, lo:lo + GROUP_W]
            y_ref[:, lo:lo + GROUP_W] = yd + yo + cm["dskip_l"][:, lo:lo + GROUP_W] * xs
            xds = (xdt * cm["ds_l"][:, lo:lo + GROUP_W]).astype(BF16)
            state[g] = hg * cm["cd_l"][:, lo:lo + GROUP_W] + _dot(bm, xds, 0, 0)

    return pl.pallas_call(
        body, name=name, grid=(nc,),
        in_specs=[pl.BlockSpec((q, D_XBC), lambda c: (c, 0)),
                  pl.BlockSpec((q, LANE), lambda c: (c, 0)),
                  pl.BlockSpec((8, LANE), lambda c: (0, 0))],
        out_specs=[pl.BlockSpec((q, D_SSD), lambda c: (c, 0)),
                   pl.BlockSpec((1, N_GROUPS, N_STATE, GROUP_W), lambda c: (c, 0, 0, 0))],
        out_shape=[jax.ShapeDtypeStruct((seq, D_SSD), F32),
                   jax.ShapeDtypeStruct((nc, N_GROUPS, N_STATE, GROUP_W), F32)],
        scratch_shapes=[pltpu.VMEM((N_GROUPS, N_STATE, GROUP_W), F32)],
        compiler_params=_cp(("arbitrary",)))(xbc, proj, par)


def _ssd_bwd(dy, xbc, proj, par, hprev, name):
    seq = xbc.shape[0]
    nc = seq // CHUNK
    q = CHUNK

    def body(dy_ref, x_ref, dt_ref, par_ref, hp_ref, dx_ref, ddt_ref, st_ref, dstate):
        step = pl.program_id(0)

        @pl.when(step == 0)
        def _():
            dstate[...] = jnp.zeros_like(dstate)

        par = par_ref[...]
        cm = _ssd_common(dt_ref[...], par)
        reduce = _head_reduce()
        lane128 = lax.broadcasted_iota(jnp.int32, (1, LANE), 1)
        row128 = lax.broadcasted_iota(jnp.int32, (LANE, 1), 0)
        d_acs = jnp.zeros((q, LANE), F32)
        d_acs_t = jnp.zeros((LANE, q), F32)
        last_terms = []
        acs_terms = []
        dxdt_all = []
        for g in range(N_GROUPS):
            lo = g * GROUP_W
            sl = slice(lo, lo + GROUP_W)
            xs = x_ref[:, sl]
            bm32 = x_ref[:, D_SSD + g * N_STATE:D_SSD + (g + 1) * N_STATE]
            cm32 = x_ref[:, D_SSD + N_GROUPS * N_STATE + g * N_STATE:D_SSD + N_GROUPS * N_STATE + (g + 1) * N_STATE]
            bm = bm32.astype(BF16)
            cmat = cm32.astype(BF16)
            dyg = dy_ref[:, sl]
            dyg_b = dyg.astype(BF16)
            xdt = xs * cm["dt_l"][:, sl]
            xdt_b = xdt.astype(BF16)
            hg = hp_ref[0, g]
            hg_b = hg.astype(BF16)
            dhg = dstate[g]
            dhg_b = dhg.astype(BF16)
            ea = cm["ea_l"][:, sl]
            ds = cm["ds_l"][:, sl]
            cd = cm["cd_l"][:, sl]
            yoff = _dot(cmat, hg_b, 1, 0) * ea
            dw = (dyg * ea).astype(BF16)
            d_c = _dot(dw, hg_b, 1, 1)
            d_hprev = _dot(cmat, dw, 0, 0) + dhg * cd
            t_acs = dyg * yoff
            d_last_g = jnp.sum(dhg * hg, axis=0, keepdims=True) * cd
            xds_b = (xdt * ds).astype(BF16)
            dxds = _dot(bm, dhg_b, 1, 0)
            d_b = _dot(xds_b, dhg_b, 1, 1)
            dxdt = dxds * ds
            t_ds = dxds * xdt * ds
            t_acs = t_acs - t_ds
            d_last_g = d_last_g + jnp.sum(t_ds, axis=0, keepdims=True)
            cb = _dot(cmat, bm, 1, 1)
            d_cb = jnp.zeros((q, q), F32)
            decays = [_decay(cm, 4 * g + r) for r in range(4)]
            score_heads = [cb * dec for dec in decays]
            d_s_heads = _dot(jnp.concatenate([jnp.where(_lane_mask(r), dyg, 0.0).astype(BF16) for r in range(4)], axis=0),
                             xdt_b, 1, 1)
            dxdt_heads = _dot(jnp.concatenate([s.astype(BF16) for s in score_heads], axis=1), dyg_b, 0, 0)
            for r in range(4):
                h = 4 * g + r
                dec, s_h = decays[r], score_heads[r]
                d_s = d_s_heads[r * q:(r + 1) * q]
                dxdt = dxdt + jnp.where(_lane_mask(r), dxdt_heads[r * q:(r + 1) * q], 0.0)
                d_cb = d_cb + d_s * dec
                d_m = d_s * s_h
                d_acs = d_acs + jnp.where(lane128 == h, jnp.sum(d_m, axis=1, keepdims=True), 0.0)
                d_acs_t = d_acs_t + jnp.where(row128 == h, jnp.sum(d_m, axis=0, keepdims=True), 0.0)
            d_cb_b = d_cb.astype(BF16)
            d_c = d_c + _dot(d_cb_b, bm, 1, 0)
            d_b = d_b + _dot(d_cb_b, cmat, 0, 0)
            dstate[g] = d_hprev
            dx_ref[:, sl] = dxdt * cm["dt_l"][:, sl] + cm["dskip_l"][:, sl] * dyg
            dx_ref[:, D_SSD + g * N_STATE:D_SSD + (g + 1) * N_STATE] = d_b
            dx_ref[:, D_SSD + N_GROUPS * N_STATE + g * N_STATE:D_SSD + N_GROUPS * N_STATE + (g + 1) * N_STATE] = d_c
            acs_terms.append(t_acs)
            dxdt_all.append(dxdt * xs)
            last_terms.append(d_last_g)
        t_acs_l = jnp.concatenate(acs_terms, axis=1)
        d_dt_l = jnp.concatenate(dxdt_all, axis=1)
        d_last_l = jnp.concatenate(last_terms, axis=1)
        per_head = _pick(jnp.concatenate([t_acs_l, d_dt_l], axis=0), reduce, 2)
        skip_l = jnp.sum(dy_ref[...] * x_ref[:, 0:D_SSD], axis=0, keepdims=True)
        singles = _pick(jnp.concatenate([d_last_l, skip_l, jnp.zeros((14, D_SSD), F32)], axis=0), reduce, 3)
        d_acs = d_acs + per_head[0:q] - d_acs_t.T
        last_row = lax.broadcasted_iota(jnp.int32, (q, 1), 0) == q - 1
        d_acs = d_acs + jnp.where(last_row, singles[0:1, :], 0.0)
        li = lax.broadcasted_iota(jnp.int32, (q, q), 0)
        si = lax.broadcasted_iota(jnp.int32, (q, q), 1)
        d_adt = _running((si >= li).astype(BF16), d_acs, 3)
        d_dt = per_head[q:2 * q] + d_adt * cm["a"]
        d_dtr = d_dt * _sigmoid(cm["v"])
        ddt_ref[...] = d_dtr.astype(BF16)
        d_skip = singles[1:2, :]
        rows = jnp.concatenate([
            jnp.sum(d_dtr, axis=0, keepdims=True),
            jnp.sum(d_adt * cm["dt"], axis=0, keepdims=True) * cm["a"],
            d_skip,
            jnp.zeros((5, LANE), F32)], axis=0)

        @pl.when(step == 0)
        def _():
            st_ref[...] = rows

        @pl.when(step != 0)
        def _():
            st_ref[...] += rows

    rev = lambda c: nc - 1 - c
    return pl.pallas_call(
        body, name=name, grid=(nc,),
        in_specs=[pl.BlockSpec((q, D_SSD), lambda c: (rev(c), 0)),
                  pl.BlockSpec((q, D_XBC), lambda c: (rev(c), 0)),
                  pl.BlockSpec((q, LANE), lambda c: (rev(c), 0)),
                  pl.BlockSpec((8, LANE), lambda c: (0, 0)),
                  pl.BlockSpec((1, N_GROUPS, N_STATE, GROUP_W), lambda c: (rev(c), 0, 0, 0))],
        out_specs=[pl.BlockSpec((q, D_XBC), lambda c: (rev(c), 0)),
                   pl.BlockSpec((q, LANE), lambda c: (rev(c), 0)),
                   pl.BlockSpec((8, LANE), lambda c: (0, 0))],
        out_shape=[jax.ShapeDtypeStruct((seq, D_XBC), F32),
                   jax.ShapeDtypeStruct((seq, LANE), BF16),
                   jax.ShapeDtypeStruct((8, LANE), F32)],
        scratch_shapes=[pltpu.VMEM((N_GROUPS, N_STATE, GROUP_W), F32)],
        compiler_params=_cp(("arbitrary",)))(dy, xbc, proj, par, hprev)


def _gate_norm_bwd(dmix, wout, y, proj, wn, tm, name):
    seq = y.shape[0]

    def body(dm_ref, wo_ref, y_ref, z_ref, w_ref, dy_ref, dz_ref, st_ref):
        d_ys = _dot(dm_ref[...], wo_ref[...], 1, 1)
        rows = []
        for g in range(N_GROUPS):
            sl = slice(g * GROUP_W, (g + 1) * GROUP_W)
            zv = z_ref[:, sl]
            yv = y_ref[:, sl]
            sg = _sigmoid(zv)
            sz = zv * sg
            yz = yv * sz
            r = lax.rsqrt(jnp.mean(yz * yz, axis=-1, keepdims=True) + EPS)
            yn = yz * r
            dv = d_ys[:, sl]
            dyn = dv * w_ref[:, sl]
            dyz = r * (dyn - yn * jnp.mean(dyn * yn, axis=-1, keepdims=True))
            dy_ref[:, sl] = dyz * sz
            dz_ref[:, sl] = (dyz * yv * _silu_grad(zv, sg)).astype(BF16)
            rows.append(jnp.sum(dv * yn, axis=0, keepdims=True))
        rows = jnp.concatenate([jnp.concatenate(rows, axis=1), jnp.zeros((7, D_SSD), F32)], axis=0)

        @pl.when(pl.program_id(0) == 0)
        def _():
            st_ref[...] = rows

        @pl.when(pl.program_id(0) != 0)
        def _():
            st_ref[...] += rows

    tok = pl.BlockSpec((tm, D_SSD), lambda i: (i, 0))
    return pl.pallas_call(
        body, name=name, grid=(seq // tm,),
        in_specs=[tok, pl.BlockSpec((D_SSD, D), lambda i: (0, 0)), tok, tok, pl.BlockSpec((1, D_SSD), lambda i: (0, 0))],
        out_specs=[tok, tok, pl.BlockSpec((8, D_SSD), lambda i: (0, 0))],
        out_shape=[jax.ShapeDtypeStruct((seq, D_SSD), F32), jax.ShapeDtypeStruct((seq, D_SSD), BF16),
                   jax.ShapeDtypeStruct((8, D_SSD), F32)],
        compiler_params=_cp(("arbitrary",)))(dmix, wout, y, proj, wn)


def _pool_counts(t0, rows, w):
    pos = (t0 + 1 + lax.broadcasted_iota(jnp.int32, (rows, 1), 0)).astype(F32)
    return jnp.minimum(pos, float(w))


def _window_means(ext, t0):
    n = ext.shape[0]
    outs = []
    run = ext
    width = 1
    sums = {}
    while width < 16:
        run = run + pltpu.roll(run, width, 0)
        width *= 2
        sums[width] = run
    for g, w in enumerate(POOL_WINDOWS):
        sl = slice(g * POOL_GW, (g + 1) * POOL_GW)
        cnt = _pool_counts(t0, n - HALO, w)
        outs.append(sums[w][HALO:, sl] / cnt - ext[HALO:, sl])
    return outs


def _pool_bwd(dmix, wout, proj, pw, pb, ps, tm, name):
    seq = proj.shape[0]
    ni = seq // tm

    def body(dm_ref, dmn_ref, wo_ref, u_ref, p_ref, pw_ref, pb_ref, ps_ref, du_ref, dw_ref, st_ref):
        i = pl.program_id(0)
        prev = jnp.where(i == 0, 0.0, p_ref[...])
        ext = jnp.concatenate([prev, u_ref[...]], axis=0)
        diffs = _window_means(ext, i * tm)
        n = tm + HALO
        dext = _dot(jnp.concatenate([dm_ref[...], dmn_ref[...]], axis=0), wo_ref[...], 1, 1)
        past_end = jnp.logical_and(i == ni - 1, lax.broadcasted_iota(jnp.int32, (n, 1), 0) >= tm)
        dext = jnp.where(past_end, 0.0, dext)
        b_rows, s_rows = [], []
        for g, w in enumerate(POOL_WINDOWS):
            sl = slice(g * POOL_GW, (g + 1) * POOL_GW)
            wg = pw_ref[g]
            dout = dext[:, sl] * ps_ref[:, sl]
            dcur = dout[:tm]
            pre = _dot(diffs[g].astype(BF16), wg, 1, 0) + pb_ref[:, sl]
            s_rows.append(jnp.sum(dext[:tm, sl] * pre, axis=0, keepdims=True))
            b_rows.append(jnp.sum(dcur, axis=0, keepdims=True))
            dwg = _dot(diffs[g].astype(BF16), dcur.astype(BF16), 0, 0)

            @pl.when(i == 0)
            def _():
                dw_ref[g] = dwg

            @pl.when(i != 0)
            def _():
                dw_ref[g] += dwg

            ddiff = _dot(dout.astype(BF16), wg, 1, 1)
            scaled = ddiff / _pool_counts(i * tm, n, w)
            run = scaled
            width = 1
            while width < w:
                run = run + pltpu.roll(run, n - width, 0)
                width *= 2
            du_ref[:, sl] = (run[:tm] - ddiff[:tm]).astype(BF16)
        rows = jnp.concatenate([jnp.concatenate(b_rows, axis=1), jnp.concatenate(s_rows, axis=1),
                                jnp.zeros((6, D_POOL), F32)], axis=0)

        @pl.when(i == 0)
        def _():
            st_ref[...] = rows

        @pl.when(i != 0)
        def _():
            st_ref[...] += rows

    c0 = COL_U // 1024
    vec = pl.BlockSpec((1, D_POOL), lambda i: (0, 0))
    last = seq // HALO - 1
    return pl.pallas_call(
        body, name=name, grid=(ni,),
        in_specs=[pl.BlockSpec((tm, D), lambda i: (i, 0)),
                  pl.BlockSpec((HALO, D), lambda i: (jnp.minimum((i + 1) * (tm // HALO), last), 0)),
                  pl.BlockSpec((D_POOL, D), lambda i: (1, 0)),
                  pl.BlockSpec((tm, 1024), lambda i: (i, c0)),
                  pl.BlockSpec((HALO, 1024), lambda i: (jnp.maximum(i * (tm // HALO) - 1, 0), c0)),
                  pl.BlockSpec((4, POOL_GW, POOL_GW), lambda i: (0, 0, 0)), vec, vec],
        out_specs=[pl.BlockSpec((tm, D_POOL), lambda i: (i, 0)),
                   pl.BlockSpec((4, POOL_GW, POOL_GW), lambda i: (0, 0, 0)),
                   pl.BlockSpec((8, D_POOL), lambda i: (0, 0))],
        out_shape=[jax.ShapeDtypeStruct((seq, D_POOL), BF16),
                   jax.ShapeDtypeStruct((4, POOL_GW, POOL_GW), F32),
                   jax.ShapeDtypeStruct((8, D_POOL), F32)],
        compiler_params=_cp(("arbitrary",)))(dmix, dmix, wout, proj, proj, pw, pb, ps)


def _mix_heads_out(y, proj, gn_w, pw, pb, ps, wout, x1, mod, wn_next, tm, name):
    seq = y.shape[0]

    def body(y_ref, z_ref, gw_ref, u_ref, p_ref, pw_ref, pb_ref, ps_ref, w_ref, x_ref, mod_ref, wn_ref,
             ys_ref, yp_ref, xo_ref, m_ref, h_ref):
        i = pl.program_id(0)
        ys_parts = []
        for g in range(N_GROUPS):
            sl = slice(g * GROUP_W, (g + 1) * GROUP_W)
            zv = z_ref[:, sl]
            yz = y_ref[:, sl] * (zv * _sigmoid(zv))
            r = lax.rsqrt(jnp.mean(yz * yz, axis=-1, keepdims=True) + EPS)
            ys_parts.append((yz * r * gw_ref[:, sl]).astype(BF16))
        ys = jnp.concatenate(ys_parts, axis=1)
        prev = jnp.where(i == 0, 0.0, p_ref[...])
        diffs = _window_means(jnp.concatenate([prev, u_ref[...]], axis=0), i * tm)
        yp_parts = []
        for g in range(4):
            sl = slice(g * POOL_GW, (g + 1) * POOL_GW)
            out = _dot(diffs[g].astype(BF16), pw_ref[g], 1, 0) + pb_ref[:, sl]
            yp_parts.append((out * ps_ref[:, sl]).astype(BF16))
        yp = jnp.concatenate(yp_parts, axis=1)
        ys_ref[...] = ys
        yp_ref[...] = yp
        mix = _dot(ys, w_ref[0:D_SSD, :], 1, 0) + _dot(yp, w_ref[D_SSD:2 * D_SSD, :], 1, 0)
        m_ref[...] = mix.astype(BF16)
        xo = x_ref[...] + mod_ref[5:6, :] * mix
        xo_ref[...] = xo
        h_ref[...] = _modulated(xo, wn_ref[...], mod_ref, 2)

    c0 = COL_U // 1024
    tok = pl.BlockSpec((tm, D), lambda i: (i, 0))
    vec = pl.BlockSpec((1, D), lambda i: (0, 0))
    return pl.pallas_call(
        body, name=name, grid=(seq // tm,),
        in_specs=[tok, tok, vec,
                  pl.BlockSpec((tm, 1024), lambda i: (i, c0)),
                  pl.BlockSpec((HALO, 1024), lambda i: (jnp.maximum(i * (tm // HALO) - 1, 0), c0)),
                  pl.BlockSpec((4, POOL_GW, POOL_GW), lambda i: (0, 0, 0)), vec, vec,
                  pl.BlockSpec((2 * D_SSD, D), lambda i: (0, 0)), tok,
                  pl.BlockSpec((9, D), lambda i: (0, 0)), vec],
        out_specs=[tok, tok, tok, tok, tok],
        out_shape=[jax.ShapeDtypeStruct((seq, D), BF16), jax.ShapeDtypeStruct((seq, D), BF16),
                   jax.ShapeDtypeStruct((seq, D), F32), jax.ShapeDtypeStruct((seq, D), BF16),
                   jax.ShapeDtypeStruct((seq, D), BF16)],
        compiler_params=_cp(("parallel",)))(y, proj, gn_w, proj, proj, pw, pb, ps, wout, x1, mod, wn_next)


def _mix_bwd_dh(dz, dxbc, du, ddt, w_main, w_dt, x1, dx2, mixb, wn, mod, tm, name):
    seq = x1.shape[0]

    def body(dz_ref, dx_ref, du_ref, ddt_ref, w_ref, wdt_ref, x_ref, dxo_ref, m_ref, wn_ref, mod_ref, o_ref, st_ref, df_ref):
        dh = (_dot(dz_ref[...], w_ref[COL_Z:COL_Z + 1024, :], 1, 0)
              + _dot(dx_ref[...], w_ref[COL_XBC:COL_XBC + D_XBC, :], 1, 0)
              + _dot(du_ref[...], w_ref[COL_U:COL_U + 1024, :], 1, 0)
              + _dot(ddt_ref[...], wdt_ref[...], 1, 0))
        dx = _norm_bwd(dh, x_ref[...], dxo_ref[...], m_ref[...].astype(F32), wn_ref[...],
                       mod_ref[4:5, :], 1.0, st_ref, pl.program_id(0) == 0)
        o_ref[...] = dx
        df_ref[...] = (dx * (FFN_RES * mod_ref[2:3, :])).astype(BF16)

    tok = pl.BlockSpec((tm, D), lambda i: (i, 0))
    return pl.pallas_call(
        body, name=name, grid=(seq // tm,),
        in_specs=[tok, pl.BlockSpec((tm, D_XBC), lambda i: (i, 0)), tok,
                  pl.BlockSpec((tm, LANE), lambda i: (i, 0)),
                  pl.BlockSpec((D_MAIN, D), lambda i: (0, 0)),
                  pl.BlockSpec((LANE, D), lambda i: (0, 0)),
                  tok, tok, tok,
                  pl.BlockSpec((1, D), lambda i: (0, 0)),
                  pl.BlockSpec((9, D), lambda i: (0, 0))],
        out_specs=[tok, pl.BlockSpec((8, D), lambda i: (0, 0)), tok],
        out_shape=[jax.ShapeDtypeStruct((seq, D), F32), jax.ShapeDtypeStruct((8, D), F32),
                   jax.ShapeDtypeStruct((seq, D), BF16)],
        compiler_params=_cp(("arbitrary",)))(dz, dxbc, du, ddt, w_main, w_dt, x1, dx2, mixb, wn, mod)


def _local_step(x, tgt, mod, wff1, w_dt, later_weights, pool_w, vecs, tm):
    seq = x.shape[0]
    in_proj_weights, out_proj_weights, ffn2_weights = later_weights
    h1 = _prenorm(x, vecs["ffn1_norm"], mod, 0, tm, "ffn1_prenorm")
    x1, h2, proj_dt, s1 = _ffn_forward(x, h1, wff1, mod, 0, vecs["mix_norm"], 1, w_dt, tm, "ffn1")
    x1, w_main = in_proj_weights(x1)
    proj = _mm_nt(h2, w_main, tm, 2048, F32, "mix_in_proj")
    xbc, conv_slope = _conv_fwd(proj, vecs["conv_w"], vecs["conv_b"], tm, "mix_conv")
    y, hprev = _ssd_fwd(xbc, proj_dt, vecs["ssd_par"], "mix_ssd")
    y, wout = out_proj_weights(y)
    ys, yp, x2, mixb, h3 = _mix_heads_out(y, proj, vecs["ssd_norm_w"], pool_w, vecs["pool_b"], vecs["pool_scale"],
                                          wout, x1, mod, vecs["ffn2_norm"], tm, "mix_heads_out")
    x2, wff2 = ffn2_weights(x2)
    a3, pg3, pu3 = _ffn_up(h3, wff2["gate"], wff2["up"], tm, FF // 2, "ffn2_up")
    fb3, dx3, df3, st_loss = _ffn_down_loss(a3, *wff2["down"], x2, mod, 8, vecs["final_norm"], tgt, tm, "ffn2_down_loss")
    s3 = (x2, h3, pg3, pu3, a3, fb3)

    dx2, dmix, st3, dw3 = _ffn_backward(dx3, df3, s3, wff2, vecs["ffn2_norm"], mod, 2, (5, 1.0), tm, "ffn2")
    d_wout = (_mm_tn(ys, dmix, 256, seq, "mix_dw_out_ssd"), _mm_tn(yp, dmix, 256, seq, "mix_dw_out_pool"))
    du, d_pool_w, st_pool = _pool_bwd(dmix, wout, proj, pool_w, vecs["pool_b"], vecs["pool_scale"], tm, "mix_pool_bwd")
    dy, dz, st_gn = _gate_norm_bwd(dmix, wout, y, proj, vecs["ssd_norm_w"], tm, "mix_gate_norm_bwd")
    dxbc_act, ddt, st_ssd = _ssd_bwd(dy, xbc, proj_dt, vecs["ssd_par"], hprev, "mix_ssd_bwd")
    dxbc, st_conv = _conv_bwd(dxbc_act, conv_slope, proj, vecs["conv_w"], tm, "mix_conv_bwd")
    dx1, st2, df1 = _mix_bwd_dh(dz, dxbc, du, ddt, w_main, w_dt, x1, dx2, mixb, vecs["mix_norm"], mod, min(tm, 256),
                                "mix_bwd_dh")
    d_win = (_mm_tn_rows([dz, dxbc, du], h2, 256, "mix_dw_in"), _mm_tn(ddt, h2, LANE, seq, "mix_dw_in_dt"))
    dx0, _, st1, dw1 = _ffn_backward(dx1, df1, s1, wff1, vecs["ffn1_norm"], mod, 0, None, tm, "ffn1")
    stats = dict(ffn1=st1, mix=st2, ffn2=st3, loss=st_loss, pool=st_pool, gn=st_gn, ssd=st_ssd, conv=st_conv)
    return dx0, stats, dw1, dw3, d_win, d_wout, d_pool_w


HBM_SPEC = pl.BlockSpec(memory_space=pltpu.HBM)


def _mesh_pos():
    return lax.axis_index("x"), lax.axis_index("y"), lax.axis_index("c")


def _other_chips(x, y):
    return [(1 - x, y), (x, 1 - y), (1 - x, 1 - y)]


def _all_gather(src, regions, name):
    total, cols = src.shape
    assert sum(r for _, r in regions) == total
    body = _all_gather_body(regions, total, False)
    return pl.pallas_call(
        body, name=name,
        out_shape=jax.ShapeDtypeStruct((N_DEV * total, cols), src.dtype),
        in_specs=[HBM_SPEC], out_specs=HBM_SPEC,
        scratch_shapes=[pltpu.SemaphoreType.DMA((7,)), pltpu.SemaphoreType.DMA((7,)), pltpu.SemaphoreType.DMA],
    )(src)


def _all_gather_async(src, regions, name, collective_id):
    total, cols = src.shape
    assert sum(r for _, r in regions) == total
    return pl.kernel(
        _all_gather_body(regions, total, True), name=name,
        out_type=jax.ShapeDtypeStruct((N_DEV * total, cols), src.dtype),
        mesh=plsc.ScalarSubcoreMesh(axis_name="seq", num_cores=1),
        scratch_types=(pltpu.SemaphoreType.DMA((7,)), pltpu.SemaphoreType.DMA((7,)), pltpu.SemaphoreType.DMA),
        compiler_params=pltpu.CompilerParams(collective_id=collective_id))(src)


def _all_gather_body(regions, total, handshake):
    def body(src_ref, out_ref, send_sems, recv_sems, local_sem):
        x, y, c = _mesh_pos()
        me, sibling = (x, y, c), (x, y, 1 - c)
        flip = lambda v, bit: v + bit - 2 * v * bit
        first_chip = (flip(x, 1 - c), flip(y, c))
        second_chip = (flip(x, c), flip(y, 1 - c))
        far_chip = (1 - x, 1 - y)
        if handshake:
            barrier = pltpu.get_barrier_semaphore()
            for peer in (sibling, (*first_chip, c), (*second_chip, c)):
                pl.semaphore_signal(barrier, inc=1, device_id=peer, device_id_type=MESH)
            pl.semaphore_wait(barrier, 3)

        def rows_of(dev, off, rows):
            start = pl.multiple_of(N_DEV * off + (4 * dev[0] + 2 * dev[1] + dev[2]) * rows, 8)
            return out_ref.at[pl.ds(start, rows), :]

        def copies(k, block, to, from_src):
            out = []
            for off, rows in regions:
                dst = rows_of(block, off, rows)
                out.append(pltpu.make_async_remote_copy(
                    src_ref=src_ref.at[pl.ds(off, rows), :] if from_src else dst, dst_ref=dst,
                    send_sem=send_sems.at[k], recv_sem=recv_sems.at[k], device_id=to, device_id_type=MESH))
            return out

        def drain(k):
            whole = out_ref.at[pl.ds(0, total), :]
            return pltpu.make_async_remote_copy(src_ref=whole, dst_ref=whole, send_sem=send_sems.at[k],
                                                recv_sem=recv_sems.at[k], device_id=me, device_id_type=MESH)

        for off, rows in regions:
            pltpu.make_async_copy(src_ref.at[pl.ds(off, rows), :], rows_of(me, off, rows), local_sem).start()
        first = copies(0, me, sibling, True) + copies(1, me, (*first_chip, c), True) + copies(2, me, (*second_chip, c), True)
        for cp in first:
            cp.start()
        drain(1).wait_recv()
        for cp in copies(3, (*first_chip, c), (*second_chip, c), False) + copies(4, (*first_chip, c), sibling, False):
            cp.start()
        drain(2).wait_recv()
        for cp in copies(5, (*second_chip, c), sibling, False):
            cp.start()
        drain(3).wait_recv()
        for cp in copies(6, (*far_chip, c), sibling, False):
            cp.start()
        drain(0).wait_recv()
        for j in range(3):
            drain(4 + j).wait_recv()
        for k in range(7):
            drain(k).wait_send()
        pltpu.make_async_copy(src_ref, out_ref.at[pl.ds(0, total), :], local_sem).wait()

    return body


def _rs_pair(grads, total, name, collective_id):
    cols = grads[0][0].shape[1]
    sent = sum(rows for _, _, rows in grads)
    n = len(grads)

    def body(*refs):
        g_refs, recv_ref, send_sem, recv_sem = refs[:n], refs[n], refs[n + 1], refs[n + 2]
        x, y, c = _mesh_pos()
        sibling = (x, y, 1 - c)
        barrier = pltpu.get_barrier_semaphore()
        pl.semaphore_signal(barrier, inc=1, device_id=sibling, device_id_type=MESH)
        pl.semaphore_wait(barrier, 1)
        for q in range(4):
            for g_ref, (_, off, rows) in zip(g_refs, grads):
                theirs = g_ref.at[pl.ds(pl.multiple_of((2 * q + 1 - c) * rows, 8), rows), :]
                pltpu.make_async_remote_copy(
                    src_ref=theirs, dst_ref=recv_ref.at[q, pl.ds(off, rows), :], send_sem=send_sem, recv_sem=recv_sem,
                    device_id=sibling, device_id_type=MESH).start()
        everything = recv_ref.at[:, pl.ds(0, sent), :]
        whole = pltpu.make_async_remote_copy(src_ref=everything, dst_ref=everything, send_sem=send_sem,
                                             recv_sem=recv_sem, device_id=sibling, device_id_type=MESH)
        whole.wait_send()
        whole.wait_recv()

    return pl.kernel(
        body, name=name, out_type=jax.ShapeDtypeStruct((4, total, cols), F32),
        mesh=plsc.ScalarSubcoreMesh(axis_name="seq", num_cores=1),
        scratch_types=(pltpu.SemaphoreType.DMA, pltpu.SemaphoreType.DMA),
        compiler_params=pltpu.CompilerParams(collective_id=collective_id))(*[g for g, _, _ in grads])


def _pair_sum(grads, from_sibling, pos, name):
    cols = grads[0][0].shape[1]
    n = len(grads)

    def body(pos_ref, *refs):
        mine = pl.program_id(0) == pos_ref[1]
        for i in range(n):
            s = refs[i][...] + refs[n + i][...]
            refs[2 * n + 2 * i + 1][...] = s.astype(BF16)

            @pl.when(mine)
            def _():
                refs[2 * n + 2 * i][...] = s

    in_specs = [pl.BlockSpec((None, None, rows, cols), lambda q, pos_ref: (q, pos_ref[0], 0, 0)) for _, _, rows in grads]
    in_specs += [pl.BlockSpec((None, rows, cols), lambda q, pos_ref, blk=off // rows: (q, blk, 0)) for _, off, rows in grads]
    out_specs, out_shape = [], []
    for _, _, rows in grads:
        out_specs += [pl.BlockSpec((rows, cols), lambda q, pos_ref: (0, 0)),
                      pl.BlockSpec((None, rows, cols), lambda q, pos_ref: (q, 0, 0))]
        out_shape += [jax.ShapeDtypeStruct((rows, cols), F32), jax.ShapeDtypeStruct((4, rows, cols), BF16)]
    outs = pl.pallas_call(
        body, name=name,
        grid_spec=pltpu.PrefetchScalarGridSpec(num_scalar_prefetch=1, grid=(4,), in_specs=in_specs, out_specs=out_specs),
        out_shape=out_shape,
        compiler_params=_cp(("arbitrary",)))(pos, *[g.reshape(4, 2, rows, cols) for g, _, rows in grads],
                                             *[from_sibling] * n)
    return [(outs[2 * i], outs[2 * i + 1]) for i in range(n)]


def _rs_chips(parts, total, name, collective_id):
    cols = parts[0][0].shape[2]
    sent = sum(rows for _, _, rows in parts)
    n = len(parts)

    def body(*refs):
        p_refs, out_ref, send_sems, recv_sems = refs[:n], refs[n], refs[n + 1], refs[n + 2]
        x, y, c = _mesh_pos()
        chips = _other_chips(x, y)
        barrier = pltpu.get_barrier_semaphore()
        for chip in chips:
            pl.semaphore_signal(barrier, inc=1, device_id=(*chip, c), device_id_type=MESH)
        pl.semaphore_wait(barrier, 3)
        for j, chip in enumerate(chips):
            q = 2 * chip[0] + chip[1]
            for p_ref, (_, off, rows) in zip(p_refs, parts):
                pltpu.make_async_remote_copy(
                    src_ref=p_ref.at[q], dst_ref=out_ref.at[j, pl.ds(off, rows), :], send_sem=send_sems.at[j],
                    recv_sem=recv_sems.at[j], device_id=(*chip, c), device_id_type=MESH).start()
        for j, chip in enumerate(chips):
            everything = out_ref.at[j, pl.ds(0, sent), :]
            whole = pltpu.make_async_remote_copy(src_ref=everything, dst_ref=everything, send_sem=send_sems.at[j],
                                                 recv_sem=recv_sems.at[j], device_id=(*chip, c), device_id_type=MESH)
            whole.wait_recv()
            whole.wait_send()

    return pl.kernel(
        body, name=name, out_type=jax.ShapeDtypeStruct((3, total, cols), BF16),
        mesh=plsc.ScalarSubcoreMesh(axis_name="seq", num_cores=1),
        scratch_types=(pltpu.SemaphoreType.DMA((3,)), pltpu.SemaphoreType.DMA((3,))),
        compiler_params=pltpu.CompilerParams(collective_id=collective_id))(*[p for p, _, _ in parts])


def _chip_sum(p, from_chips, off, rows, name):
    cols = p.shape[1]

    def body(p_ref, r_ref, o_ref):
        acc = p_ref[...]
        for j in range(3):
            acc = acc + r_ref[j].astype(F32)
        o_ref[...] = acc

    return pl.pallas_call(
        body, name=name, grid=(1,),
        in_specs=[pl.BlockSpec((rows, cols), lambda i: (0, 0)),
                  pl.BlockSpec((3, rows, cols), lambda i: (0, off // rows, 0))],
        out_specs=pl.BlockSpec((rows, cols), lambda i: (0, 0)),
        out_shape=jax.ShapeDtypeStruct((rows, cols), F32),
        compiler_params=_cp(("arbitrary",)))(p, from_chips)


def _chip_sum_adamw(p, from_chips, off, rows, w, m, v, tr, name):
    cols = p.shape[1]
    c1 = 1.0 - ADAM_B1 ** ADAM_STEP
    c2 = 1.0 - ADAM_B2 ** ADAM_STEP

    def body(p_ref, r_ref, w_ref, m_ref, v_ref, g_ref, d_ref, mo_ref, vo_ref):
        gv = p_ref[...]
        for j in range(3):
            gv = gv + r_ref[j].astype(F32)
        g_ref[...] = gv
        mn = ADAM_B1 * m_ref[...] + (1.0 - ADAM_B1) * gv
        vn = ADAM_B2 * v_ref[...] + (1.0 - ADAM_B2) * (gv * gv)
        mo_ref[...] = mn
        vo_ref[...] = vn
        d_ref[...] = -ADAM_LR * ((mn / c1) / (jnp.sqrt(vn / c2) + ADAM_EPS) + ADAM_WD * w_ref[...])

    tile = pl.BlockSpec((tr, cols), lambda i: (i, 0))
    shape = jax.ShapeDtypeStruct((rows, cols), F32)
    return pl.pallas_call(
        body, name=name, grid=(rows // tr,),
        in_specs=[tile, pl.BlockSpec((3, tr, cols), lambda i: (0, off // tr + i, 0)), tile, tile, tile],
        out_specs=[tile] * 4,
        out_shape=[shape] * 4,
        compiler_params=_cp(("parallel",)))(p, from_chips, w, m, v)


def _row_tile(rows, cap):
    t = min(rows, cap)
    while rows % t or t % 8:
        t -= 8
    return t


def _ada_mod(c_all, w, b, name):
    n = w.shape[1]

    def body(c_ref, w_ref, b_ref, o_ref):
        cv = c_ref[...]
        o_ref[...] = _exact_dot(cv * _sigmoid(cv), w_ref[...]) + b_ref[...]

    return pl.pallas_call(body, name=name, out_shape=jax.ShapeDtypeStruct((N_DEV, n), F32),
                          compiler_params=pltpu.CompilerParams(vmem_limit_bytes=VMEM_LIMIT))(c_all, w, b)


def _ada_grad(c_all, dmod, name):
    n = dmod.shape[1]

    def body(c_ref, d_ref, o_ref):
        cv = c_ref[...]
        o_ref[...] = _dot(cv * _sigmoid(cv), d_ref[...], 0, 0, lax.Precision.HIGHEST)

    return pl.pallas_call(body, name=name, out_shape=jax.ShapeDtypeStruct((D, n), F32),
                          compiler_params=pltpu.CompilerParams(vmem_limit_bytes=VMEM_LIMIT))(c_all, dmod)


def _adamw(w, g, m, v, name):
    rows, cols = w.shape
    tr = _row_tile(rows, 256) if rows % 8 == 0 else rows
    c1 = 1.0 - ADAM_B1 ** ADAM_STEP
    c2 = 1.0 - ADAM_B2 ** ADAM_STEP

    def body(w_ref, g_ref, m_ref, v_ref, d_ref, mo_ref, vo_ref):
        gv = g_ref[...]
        mn = ADAM_B1 * m_ref[...] + (1.0 - ADAM_B1) * gv
        vn = ADAM_B2 * v_ref[...] + (1.0 - ADAM_B2) * (gv * gv)
        mo_ref[...] = mn
        vo_ref[...] = vn
        d_ref[...] = -ADAM_LR * ((mn / c1) / (jnp.sqrt(vn / c2) + ADAM_EPS) + ADAM_WD * w_ref[...])

    spec = pl.BlockSpec((tr, cols), lambda i: (i, 0))
    shape = jax.ShapeDtypeStruct((rows, cols), F32)
    return pl.pallas_call(body, name=name, grid=(rows // tr,), in_specs=[spec] * 4, out_specs=[spec] * 3,
                          out_shape=[shape] * 3, compiler_params=_cp(("parallel",)))(w, g, m, v)


def _sum8_loss(v, loss_row, name):
    rows = v.shape[0] // N_DEV

    def body(v_ref, o_ref, l_ref):
        acc = v_ref[0:rows, :]
        for k in range(1, N_DEV):
            acc = acc + v_ref[k * rows:(k + 1) * rows, :]
        o_ref[...] = acc
        part = jnp.sum(acc[loss_row:loss_row + 8, :], axis=0, keepdims=True)
        l_ref[...] = jnp.broadcast_to(jnp.sum(part, axis=1, keepdims=True), (8, LANE))

    return pl.pallas_call(body, name=name,
                          out_shape=[jax.ShapeDtypeStruct((rows, LANE), F32), jax.ShapeDtypeStruct((8, LANE), F32)],
                          compiler_params=pltpu.CompilerParams(vmem_limit_bytes=VMEM_LIMIT))(v)


WEIGHT_NAMES = ("w_ada", "b_ada", "ffn1_norm", "ffn1_w_gate", "ffn1_w_up", "ffn1_w_down", "mix_norm", "w_in",
                "conv_w", "conv_b", "dt_bias", "a_log", "d_skip", "ssd_norm_w", "pool_w", "pool_b", "pool_scale",
                "w_out", "ffn2_norm", "ffn2_w_gate", "ffn2_w_up", "ffn2_w_down", "final_norm")

FF_SHARD = FF // N_DEV
IN_SHARD = D_IN // N_DEV
MAIN_SHARD = D_MAIN // N_DEV
EDGE = 16
OUT_SHARD = 2 * D_SSD // N_DEV
ADA_SHARD = 9 * D // N_DEV
POOL_SHARD_ROWS = 4 * 32 * POOL_GW // D
GPACK = dict(w_in=(0, MAIN_SHARD), w_out=(512, OUT_SHARD), pool_w=(768, POOL_SHARD_ROWS),
             gate1=(0, FF_SHARD), up1=(352, FF_SHARD), down1=(704, FF_SHARD),
             gate2=(0, FF_SHARD), up2=(352, FF_SHARD), down2=(704, FF_SHARD))
GROUP_ROWS = 1056

SMALL_ROWS = dict(dmod=(0, 72), ffn1_norm=(72, 8), mix_norm=(80, 8), ffn2_norm=(88, 8), final_norm=(96, 8),
                  ssd_norm_w=(104, 8), pool_scale=(112, 8), conv_b=(120, 16), conv_w=(136, 64), pool_b=(200, 8),
                  ssd=(208, 3), loss=(216, 8), w_in_dt=(224, 128), w_in_head=(352, 128), w_in_tail=(480, 128))
SMALL_TOTAL = 608

MAIN_FROM_OWN = (16, 14, 12, 10, 8, 6, 20, 18)
OWN_FROM_MAIN = (16, 18, 20, 22, 24, 26, 12, 14)


def _rows128(v, rows):
    flat = v.reshape(-1)
    return jnp.pad(flat, (0, rows * LANE - flat.shape[0])).reshape(rows, LANE)


def _pad_lanes(v):
    return jnp.pad(v.reshape(-1), (0, LANE - v.size))


def kernel(x, c, w_ada, b_ada, ffn1_norm, ffn1_w_gate, ffn1_w_up, ffn1_w_down, mix_norm, w_in, conv_w, conv_b, dt_bias, a_log, d_skip, ssd_norm_w, pool_w, pool_b, pool_scale, w_out, ffn2_norm, ffn2_w_gate, ffn2_w_up, ffn2_w_down, final_norm, loss_target, m_w_ada, m_b_ada, m_ffn1_norm, m_ffn1_w_gate, m_ffn1_w_up, m_ffn1_w_down, m_mix_norm, m_w_in, m_conv_w, m_conv_b, m_dt_bias, m_a_log, m_d_skip, m_ssd_norm_w, m_pool_w, m_pool_b, m_pool_scale, m_w_out, m_ffn2_norm, m_ffn2_w_gate, m_ffn2_w_up, m_ffn2_w_down, m_final_norm, v_w_ada, v_b_ada, v_ffn1_norm, v_ffn1_w_gate, v_ffn1_w_up, v_ffn1_w_down, v_mix_norm, v_w_in, v_conv_w, v_conv_b, v_dt_bias, v_a_log, v_d_skip, v_ssd_norm_w, v_pool_w, v_pool_b, v_pool_scale, v_w_out, v_ffn2_norm, v_ffn2_w_gate, v_ffn2_w_up, v_ffn2_w_down, v_final_norm):
    given = dict(locals())
    w = {n: given[n] for n in WEIGHT_NAMES}
    m = {n: given["m_" + n] for n in WEIGHT_NAMES}
    v = {n: given["v_" + n] for n in WEIGHT_NAMES}
    mx, my, mc = _mesh_pos()
    me = 4 * mx + 2 * my + mc

    w_in_t = w_in[0].T
    c_all = _all_gather(c.reshape(8, LANE), [(0, 8)], "ag_c").reshape(N_DEV, D)
    small = jnp.concatenate([conv_w.reshape(-1), pool_b.reshape(-1), pool_w.reshape(-1),
                             w_in_t[0:EDGE].reshape(-1), w_in_t[IN_SHARD - EDGE:IN_SHARD].reshape(-1)])
    small_rows = 528
    gs = _all_gather_async(_rows128(small, small_rows), [(0, small_rows)], "ag_small", 12).reshape(N_DEV, small_rows * LANE)
    conv_w_full = gs[:, 0:1024].reshape(N_DEV, 4, 256).transpose(1, 0, 2).reshape(4, D_XBC)
    pool_b_full = gs[:, 1024:1152].reshape(N_DEV, 4, 32).transpose(1, 0, 2).reshape(1, D_POOL)
    pool_w_full = gs[:, 1152:1152 + 32768].reshape(N_DEV, 4, 32, POOL_GW).transpose(1, 0, 2, 3).reshape(4, POOL_GW, POOL_GW).astype(BF16)
    heads = gs[:, 33920:33920 + EDGE * D].reshape(N_DEV, EDGE, D)
    tails = gs[:, 33920 + EDGE * D:33920 + 2 * EDGE * D].reshape(N_DEV, EDGE, D)

    prev_tail = lax.dynamic_index_in_dim(tails, jnp.maximum(me - 1, 0), axis=0, keepdims=False)
    next_head = lax.dynamic_index_in_dim(heads, jnp.minimum(me + 1, N_DEV - 1), axis=0, keepdims=False)
    first = jnp.asarray(MAIN_FROM_OWN, jnp.int32)[me]

    def window(rows, before, size):
        total = EDGE + IN_SHARD + EDGE
        padded = jnp.pad(rows, ((before, total - before - rows.shape[0]), (0, 0)))
        return lax.dynamic_slice(padded, (first, 0), (size, D))

    main_shard = (window(prev_tail, 0, MAIN_SHARD) + window(w_in_t, EDGE, MAIN_SHARD)
                  + window(next_head, EDGE + IN_SHARD, MAIN_SHARD))
    dt_rows = jnp.concatenate([tails[5], heads[6]], axis=0)[4:4 + N_HEADS]
    w_dt = jnp.pad(dt_rows, ((0, LANE - N_HEADS), (0, 0))).astype(BF16)

    b_ada_cols = lax.dynamic_slice(b_ada, (0, me * ADA_SHARD), (1, ADA_SHARD))
    mod_part = _ada_mod(c_all, w_ada[0], b_ada_cols, "ada_mod")
    mod_all = _all_gather(mod_part, [(0, N_DEV)], "ag_mod").reshape(N_DEV, N_DEV, ADA_SHARD)
    mod = lax.dynamic_index_in_dim(mod_all, me, axis=1, keepdims=False).reshape(9, D)

    first_packs = (jnp.concatenate([ffn1_w_gate[0].T, ffn1_w_up[0].T], axis=0).astype(BF16),
                   ffn1_w_down[0].astype(BF16))
    first_packs, _ = lax.optimization_barrier((first_packs, c_all))
    in_pack = main_shard.astype(BF16)
    later_packs, in_pack = lax.optimization_barrier((
        (w_out[0].astype(BF16), jnp.concatenate([ffn2_w_gate[0].T, ffn2_w_up[0].T, ffn2_w_down[0]], axis=0).astype(BF16)),
        in_pack))
    packs = (*first_packs, in_pack, *later_packs)
    ffn_regions = [(0, FF_SHARD), (FF_SHARD, FF_SHARD), (2 * FF_SHARD, FF_SHARD)]
    full_a = _all_gather_async(packs[0], ffn_regions[0:2], "ag_weights_ffn1_in", 1)
    full_d = _all_gather_async(packs[1], ffn_regions[0:1], "ag_weights_ffn1_out", 2)
    full_in = _all_gather_async(packs[2], [(0, MAIN_SHARD)], "ag_weights_in_proj", 9)
    full_out = _all_gather_async(packs[3], [(0, OUT_SHARD)], "ag_weights_out_proj", 10)
    full_2 = _all_gather_async(packs[4], ffn_regions, "ag_weights_ffn2", 11)
    wff1 = dict(gate=(full_a, 0), up=(full_a, 1), down=(full_d, 0))

    def in_proj_weights(x1):
        w_i, x1 = lax.optimization_barrier((full_in, x1))
        return x1, w_i

    def out_proj_weights(ys):
        w_o, ys = lax.optimization_barrier((full_out, ys))
        return ys, w_o

    def ffn2_weights(x2):
        w_2, x2 = lax.optimization_barrier((full_2, x2))
        return x2, dict(gate=(w_2, 0), up=(w_2, 1), down=(w_2, 2))

    later_weights = (in_proj_weights, out_proj_weights, ffn2_weights)

    vecs = dict(ffn1_norm=ffn1_norm, mix_norm=mix_norm, ffn2_norm=ffn2_norm, final_norm=final_norm.reshape(1, D),
                conv_w=conv_w_full, conv_b=conv_b, ssd_norm_w=ssd_norm_w, pool_b=pool_b_full, pool_scale=pool_scale,
                ssd_par=jnp.concatenate([_pad_lanes(dt_bias)[None], _pad_lanes(a_log)[None], _pad_lanes(d_skip)[None],
                                         jnp.zeros((5, LANE), F32)], axis=0))
    dx0, st, dw1, dw3, d_win, d_wout, d_pool_w = _local_step(
        x[0], loss_target[0], mod, wff1, w_dt, later_weights, pool_w_full, vecs, min(512, x.shape[1]))

    dwin, d_w_dt = d_win
    dwout = jnp.concatenate(d_wout, axis=0)
    dpool = d_pool_w.reshape(4, N_DEV, 32, POOL_GW).transpose(1, 0, 2, 3).reshape(N_DEV * POOL_SHARD_ROWS, D)
    pos = jnp.stack([mc, 2 * mx + my]).astype(jnp.int32)
    by_key = dict(zip(("gate1", "up1", "down1", "gate2", "up2", "down2", "w_out", "w_in", "pool_w"),
                      (*dw1, *dw3, dwout, dwin, dpool)))
    reduced = {}
    for tag, keys, cid in (("ffn2", ("gate2", "up2", "down2"), 3), ("mix", ("w_in", "w_out", "pool_w"), 5),
                           ("ffn1", ("gate1", "up1", "down1"), 7)):
        grads = [(by_key[k], *GPACK[k]) for k in keys]
        from_sibling = _rs_pair(grads, GROUP_ROWS, f"rs_pair_{tag}", cid)
        pairs = dict(zip(keys, _pair_sum(grads, from_sibling, pos, f"rs_pair_sum_{tag}")))
        from_chips = _rs_chips([(pairs[k][1], *GPACK[k]) for k in keys], GROUP_ROWS, f"rs_chips_{tag}", cid + 1)
        for k in keys:
            reduced[k] = (pairs[k][0], from_chips)

    delta, new_m, new_v, shard_grad = {}, {}, {}, {}
    fused = dict(gate1=("ffn1_w_gate", True, 176), up1=("ffn1_w_up", True, 176), down1=("ffn1_w_down", False, 176),
                 gate2=("ffn2_w_gate", True, 176), up2=("ffn2_w_up", True, 176), down2=("ffn2_w_down", False, 176),
                 w_out=("w_out", False, 128), pool_w=("pool_w", False, POOL_SHARD_ROWS))
    for k, (n, is_transposed, tr) in fused.items():
        shp = w[n].shape
        rows = GPACK[k][1]
        view = (lambda t: t[0].T) if is_transposed else (lambda t: t.reshape(rows, D))
        back = (lambda t: t.T[None]) if is_transposed else (lambda t: t.reshape(shp))
        g_, d_, m_, v_ = _chip_sum_adamw(reduced[k][0], reduced[k][1], *GPACK[k], view(w[n]), view(m[n]), view(v[n]),
                                         tr, f"adamw_{n}")
        shard_grad[n], delta[n], new_m[n], new_v[n] = back(g_), back(d_), back(m_), back(v_)
    g_main = _chip_sum(reduced["w_in"][0], reduced["w_in"][1], *GPACK["w_in"], "rs_chip_sum_w_in")

    dmod = jnp.concatenate([st["ffn1"][0:3], st["mix"][0:3], st["ffn2"][0:3]], axis=0)
    sg = jnp.concatenate([
        dmod.reshape(-1), st["ffn1"][3], st["mix"][3], st["ffn2"][3], st["loss"][0], st["gn"][0], st["pool"][1],
        st["conv"][4], st["conv"][0:4].reshape(-1), st["pool"][0], st["ssd"][0:3].reshape(-1),
        jnp.zeros((5 * LANE,), F32), st["loss"][1],
        d_w_dt[0:N_HEADS].reshape(-1), g_main[0:EDGE].reshape(-1), g_main[MAIN_SHARD - EDGE:MAIN_SHARD].reshape(-1)])
    sg_all = _all_gather(sg.reshape(SMALL_TOTAL, LANE), [(0, SMALL_TOTAL)], "ag_small_grads")
    tot, loss_b = _sum8_loss(sg_all, SMALL_ROWS["loss"][0], "small_sum")
    loss = loss_b[0, 0]
    per_dev = sg_all.reshape(N_DEV, SMALL_TOTAL * LANE)
    dmod_all = per_dev[:, 0:9 * D]
    g_w_ada = _ada_grad(c_all, lax.dynamic_slice(dmod_all, (0, me * ADA_SHARD), (N_DEV, ADA_SHARD)), "ada_grad")

    def edge_rows(k):
        off, n = SMALL_ROWS[k]
        return per_dev[:, off * LANE:(off + n) * LANE].reshape(N_DEV, EDGE, D)

    g_dt = tot[SMALL_ROWS["w_in_dt"][0]:SMALL_ROWS["w_in_dt"][0] + SMALL_ROWS["w_in_dt"][1]].reshape(N_HEADS, D)
    before = jnp.where(me == 6, g_dt, edge_rows("w_in_tail")[6])
    after = jnp.where(me == 5, g_dt, lax.dynamic_index_in_dim(edge_rows("w_in_head"), jnp.minimum(me + 1, N_DEV - 1),
                                                               axis=0, keepdims=False))
    first_own = jnp.asarray(OWN_FROM_MAIN, jnp.int32)[me]

    def own_window(rows, lead):
        total = EDGE + MAIN_SHARD + EDGE
        padded = jnp.pad(rows, ((lead, total - lead - rows.shape[0]), (0, 0)))
        return lax.dynamic_slice(padded, (first_own, 0), (IN_SHARD, D))

    g_win_t = own_window(before, 0) + own_window(g_main, EDGE) + own_window(after, EDGE + MAIN_SHARD)

    def tot_rows(k):
        off, n = SMALL_ROWS[k]
        return tot[off:off + n].reshape(-1)

    g_conv_w = lax.dynamic_slice(tot_rows("conv_w").reshape(4, D_XBC), (0, me * 256), (4, 256))
    g_pool_b = lax.dynamic_slice(tot_rows("pool_b").reshape(4, POOL_GW), (0, me * 32), (4, 32))
    g_ssd = tot_rows("ssd").reshape(3, LANE)
    grad = {
        "w_ada": g_w_ada[None], "b_ada": tot_rows("dmod").reshape(1, 9 * D),
        "ffn1_norm": tot_rows("ffn1_norm")[None], "mix_norm": tot_rows("mix_norm")[None],
        "ffn2_norm": tot_rows("ffn2_norm")[None], "final_norm": tot_rows("final_norm"),
        "ssd_norm_w": tot_rows("ssd_norm_w")[None], "pool_scale": tot_rows("pool_scale")[None],
        "conv_b": tot_rows("conv_b")[None], "conv_w": g_conv_w[None], "pool_b": g_pool_b[None],
        "dt_bias": g_ssd[0:1, 0:N_HEADS], "a_log": g_ssd[1:2, 0:N_HEADS], "d_skip": g_ssd[2:3, 0:N_HEADS],
        "w_in": g_win_t.T[None], **shard_grad,
    }

    d_, m_, v_ = _adamw(w_ada[0], g_w_ada, m_w_ada[0], v_w_ada[0], "adamw_w_ada")
    delta["w_ada"], new_m["w_ada"], new_v["w_ada"] = d_[None], m_[None], v_[None]
    d_, m_, v_ = _adamw(w_in[0].T, g_win_t, m_w_in[0].T, v_w_in[0].T, "adamw_w_in")
    delta["w_in"], new_m["w_in"], new_v["w_in"] = d_.T[None], m_.T[None], v_.T[None]
    big = ("w_ada", "w_in") + tuple(n for n, _, _ in fused.values())
    small_names = [n for n in WEIGHT_NAMES if n not in big]
    sizes = [LANE if w[n].size < LANE else w[n].size for n in small_names]
    small_rows_adam = -(-sum(sizes) // (8 * LANE)) * 8

    def pack_small(t):
        return _rows128(jnp.concatenate([_pad_lanes(t[n]) if t[n].size < LANE else t[n].reshape(-1) for n in small_names]),
                        small_rows_adam)

    d_s, m_s, v_s = _adamw(pack_small(w), pack_small(grad), pack_small(m), pack_small(v), "adamw_small")
    off = 0
    for n, size in zip(small_names, sizes):
        for res, packed in ((delta, d_s), (new_m, m_s), (new_v, v_s)):
            res[n] = packed.reshape(-1)[off:off + w[n].size].reshape(w[n].shape)
        off += size

    return (loss, dx0[None], *[grad[n] for n in WEIGHT_NAMES], *[delta[n] for n in WEIGHT_NAMES],
            *[new_m[n] for n in WEIGHT_NAMES], *[new_v[n] for n in WEIGHT_NAMES])
```
